```python
import math
import jax, jax.numpy as jnp
from jax import lax
import numpy as np

D_MODEL = 1024
BATCH = 4
SEQ = 4096
DEPTH = 2

GRID_W = 64
HEAD_DIM = 64
BRANCH_WIDTH = 512
N_BRANCHES = 4
FOURIER_GROUPS = 8
FOURIER_GROUP_DIM = BRANCH_WIDTH // FOURIER_GROUPS
GQA_Q_HEADS = 8
GQA_KV_HEADS = 2
MLSTM_HEADS = 4
MLSTM_HEAD_DIM = BRANCH_WIDTH // MLSTM_HEADS
MLSTM_CHUNK = 128
DIFF_HEADS = 4
DIFF_QK_DIM = 64
DIFF_V_DIM = 2 * DIFF_QK_DIM
REL_BUCKETS = 32
REL_MAX_DIST = 128
Q_BLOCK = 128
D_FF = 2816
CONV_W = 3
ROPE_BASE = 10000.0
EPS = 1e-6

A_W = BRANCH_WIDTH
B_Q_W = GQA_Q_HEADS * HEAD_DIM
B_KV_W = GQA_KV_HEADS * HEAD_DIM
C_W = BRANCH_WIDTH
C_GATE_W = 4 * MLSTM_HEADS
D_QK_W = DIFF_HEADS * 2 * DIFF_QK_DIM
D_V_W = DIFF_HEADS * DIFF_V_DIM
GATE_W = N_BRANCHES * D_MODEL
SPLIT_WIDTHS = (A_W, B_Q_W, B_KV_W, B_KV_W, C_W, C_W, C_W, C_W, C_GATE_W, D_QK_W, D_QK_W, D_V_W, GATE_W)
IN_WIDTH = A_W + B_Q_W + 2 * B_KV_W + 4 * C_W + C_GATE_W + 2 * D_QK_W + D_V_W + GATE_W

kernel_name = 'hybrid_fourier_gqa_mlstm_diffattn_encoder'

F32 = jnp.float32


def rms_norm(x, g):
    xf = x.astype(F32)
    y = xf * lax.rsqrt(jnp.mean(xf * xf, axis=-1, keepdims=True) + EPS)
    return (y * g.astype(F32)).astype(x.dtype)


def split_cols(p):
    points = [int(v) for v in np.cumsum(SPLIT_WIDTHS)[:-1]]
    return jnp.split(p, points, axis=-1)


def fourier_mix(a):
    b, s, _ = a.shape
    g = a.reshape(b, s, FOURIER_GROUPS, FOURIER_GROUP_DIM).transpose(0, 2, 1, 3).astype(F32)
    f = jnp.real(jnp.fft.fft2(g, norm='ortho'))
    return f.transpose(0, 2, 1, 3).reshape(b, s, BRANCH_WIDTH).astype(a.dtype)


def axial_rope_tables(seq):
    rows = seq // GRID_W
    row_id = jnp.repeat(jnp.arange(rows, dtype=F32), GRID_W)
    col_id = jnp.tile(jnp.arange(GRID_W, dtype=F32), rows)
    n_pairs = HEAD_DIM // 4
    inv_freq = ROPE_BASE ** (-jnp.arange(n_pairs, dtype=F32) / n_pairs)
    ang = jnp.concatenate([row_id[:, None] * inv_freq, col_id[:, None] * inv_freq], axis=-1)
    return jnp.cos(ang), jnp.sin(ang)


def apply_rope(x, cos, sin):
    xf = x.astype(F32)
    x1, x2 = jnp.split(xf, 2, axis=-1)
    c = cos[None, :, None, :]
    s = sin[None, :, None, :]
    return jnp.concatenate([x1 * c - x2 * s, x1 * s + x2 * c], axis=-1).astype(x.dtype)


def gqa_attention(q, k, v):
    b, s, hq, d = q.shape
    grp = hq // GQA_KV_HEADS
    nb = s // Q_BLOCK
    scale = d ** -0.5
    qb = q.reshape(b, nb, Q_BLOCK, GQA_KV_HEADS, grp, d).transpose(1, 0, 2, 3, 4, 5)

    def one_block(qblk):
        sc = jnp.einsum('blkgd,bskd->bkgls', qblk, k, preferred_element_type=F32) * scale
        p = jax.nn.softmax(sc, axis=-1).astype(v.dtype)
        return jnp.einsum('bkgls,bskd->blkgd', p, v)

    out = lax.map(one_block, qb)
    return out.transpose(1, 0, 2, 3, 4, 5).reshape(b, s, hq * d)


def gqa_branch(q, k, v, qk_g, cos, sin):
    b, s, _ = q.shape
    q = q.reshape(b, s, GQA_Q_HEADS, HEAD_DIM)
    k = k.reshape(b, s, GQA_KV_HEADS, HEAD_DIM)
    v = v.reshape(b, s, GQA_KV_HEADS, HEAD_DIM)
    q = apply_rope(rms_norm(q, qk_g[0]), cos, sin)
    k = apply_rope(rms_norm(k, qk_g[1]), cos, sin)
    return gqa_attention(q, k, v)


def mlstm_scan(q, k, v, i_pre, f_pre):
    b, h, s, d = q.shape
    L = MLSTM_CHUNK
    nc = s // L
    k = k * d ** -0.5
    logf = jax.nn.log_sigmoid(f_pre)

    def chunks(t):
        return jnp.moveaxis(t.reshape(b, h, nc, L, *t.shape[3:]), 2, 0)

    xs = (chunks(q), chunks(k), chunks(v), chunks(i_pre), chunks(logf))
    lower = jnp.tril(jnp.ones((L, L), dtype=bool))

    def step(carry, inp):
        C, n, m = carry
        qt, kt, vt, it, ft = inp
        bcum = jnp.cumsum(ft, axis=-1)
        dmat = bcum[..., :, None] - bcum[..., None, :] + it[..., None, :]
        dmat = jnp.where(lower, dmat, -jnp.inf)
        inter = bcum + m[..., None]
        m_t = jnp.maximum(inter, jnp.max(dmat, axis=-1))
        w_intra = jnp.exp(dmat - m_t[..., None])
        w_inter = jnp.exp(inter - m_t)
        s_qk = jnp.einsum('bhld,bhsd->bhls', qt, kt) * w_intra
        num = w_inter[..., None] * jnp.einsum('bhld,bhde->bhle', qt, C) + jnp.einsum('bhls,bhse->bhle', s_qk, vt)
        den = w_inter * jnp.einsum('bhld,bhd->bhl', qt, n) + jnp.sum(s_qk, axis=-1)
        h_out = num / jnp.maximum(jnp.abs(den), jnp.exp(-m_t))[..., None]
        btot = bcum[..., -1]
        g_s = btot[..., None] - bcum + it
        m_new = jnp.maximum(btot + m, jnp.max(g_s, axis=-1))
        decay = jnp.exp(btot + m - m_new)
        w_s = jnp.exp(g_s - m_new[..., None])
        C_new = decay[..., None, None] * C + jnp.einsum('bhs,bhsd,bhse->bhde', w_s, kt, vt)
        n_new = decay[..., None] * n + jnp.einsum('bhs,bhsd->bhd', w_s, kt)
        return (C_new, n_new, m_new), h_out

    init = (jnp.zeros((b, h, d, d), F32), jnp.zeros((b, h, d), F32), jnp.zeros((b, h), F32))
    _, hs = lax.scan(step, init, xs)
    return jnp.moveaxis(hs, 0, 2).reshape(b, h, s, d)


def mlstm_branch(q, k, v, o, gate_pre, gate_bias, norm_g):
    b, s, _ = q.shape

    def heads(t):
        return t.reshape(b, s, MLSTM_HEADS, MLSTM_HEAD_DIM).transpose(0, 2, 1, 3).astype(F32)

    qh, kh, vh = heads(q), heads(k), heads(v)
    gp = gate_pre.reshape(b, s, 4, MLSTM_HEADS).astype(F32) + gate_bias.astype(F32)
    gp = gp.transpose(2, 0, 3, 1)
    h_fwd = mlstm_scan(qh, kh, vh, gp[0], gp[1])
    rev = lambda t: jnp.flip(t, axis=2)
    h_bwd = rev(mlstm_scan(rev(qh), rev(kh), rev(vh), jnp.flip(gp[2], axis=-1), jnp.flip(gp[3], axis=-1)))
    hsum = (h_fwd + h_bwd).transpose(0, 2, 1, 3)
    hsum = rms_norm(hsum, norm_g.reshape(MLSTM_HEADS, MLSTM_HEAD_DIM))
    return (jax.nn.sigmoid(o.astype(F32)) * hsum.reshape(b, s, BRANCH_WIDTH)).astype(q.dtype)


def rel_bucket(rel):
    half = REL_BUCKETS // 2
    max_exact = half // 2
    ret = jnp.where(rel > 0, half, 0)
    n = jnp.abs(rel)
    nf = jnp.maximum(n, 1).astype(F32)
    large = max_exact + (jnp.log(nf / max_exact) / math.log(REL_MAX_DIST / max_exact) * (half - max_exact)).astype(jnp.int32)
    large = jnp.minimum(large, half - 1)
    return ret + jnp.where(n < max_exact, n, large)


def diff_attention(q, k, v, lam, rel_bias):
    b, s = q.shape[:2]
    nb = s // Q_BLOCK
    scale = DIFF_QK_DIM ** -0.5
    qb = q.reshape(b, nb, Q_BLOCK, DIFF_HEADS, 2, DIFF_QK_DIM).transpose(1, 0, 2, 3, 4, 5)
    starts = jnp.arange(nb, dtype=jnp.int32) * Q_BLOCK
    kpos = jnp.arange(s, dtype=jnp.int32)

    def one_block(args):
        qblk, start = args
        qpos = start + jnp.arange(Q_BLOCK, dtype=jnp.int32)
        bias = rel_bias[rel_bucket(kpos[None, :] - qpos[:, None])].astype(F32).transpose(2, 0, 1)
        sc = jnp.einsum('blhmd,bshmd->bhmls', qblk, k, preferred_element_type=F32) * scale + bias[None, :, None]
        p = jax.nn.softmax(sc, axis=-1)
        w = (p[:, :, 0] - lam * p[:, :, 1]).astype(v.dtype)
        return jnp.einsum('bhls,bshd->blhd', w, v)

    out = lax.map(one_block, (qb, starts))
    return out.transpose(1, 0, 2, 3, 4).reshape(b, s, DIFF_HEADS, DIFF_V_DIM)


def diff_branch(q, k, v, lam_params, sub_g, rel_bias, layer_number):
    b, s, _ = q.shape
    q = q.reshape(b, s, DIFF_HEADS, 2, DIFF_QK_DIM)
    k = k.reshape(b, s, DIFF_HEADS, 2, DIFF_QK_DIM)
    v = v.reshape(b, s, DIFF_HEADS, DIFF_V_DIM)
    lam_init = 0.8 - 0.6 * math.exp(-0.3 * (layer_number - 1))
    lp = lam_params.astype(F32)
    lam = jnp.exp(jnp.sum(lp[0] * lp[1])) - jnp.exp(jnp.sum(lp[2] * lp[3])) + lam_init
    o = diff_attention(q, k, v, lam, rel_bias)
    o = rms_norm(o, sub_g) * (1.0 - lam_init)
    return o.reshape(b, s, BRANCH_WIDTH)


def conv_ffn(v, w_up, conv_w, conv_b, w_down):
    s = v.shape[1]
    up = v @ w_up
    a, lin = jnp.split(up, 2, axis=-1)
    pad = CONV_W // 2
    ap = jnp.pad(a, ((0, 0), (pad, pad), (0, 0)))
    a = sum(ap[:, j:j + s] * conv_w[j] for j in range(CONV_W)) + conv_b
    return (jax.nn.gelu(a) * lin) @ w_down


def setup_inputs(seed: int = 0) -> dict:
    key = jax.random.key(seed)
    ks = jax.random.split(key, 17)
    nrm = lambda k, shape: jax.random.normal(k, shape, F32)
    gate_base = jnp.array([0.0, 3.0, 0.0, 3.0], F32)[None, :, None]
    return {
        'x': nrm(ks[0], (BATCH, SEQ, D_MODEL)),
        'norm_mix_g': 1.0 + 0.02 * nrm(ks[1], (DEPTH, D_MODEL)),
        'w_in': nrm(ks[2], (DEPTH, D_MODEL, IN_WIDTH)) * D_MODEL ** -0.5,
        'mlstm_gate_bias': gate_base + 0.1 * nrm(ks[3], (DEPTH, 4, MLSTM_HEADS)),
        'qk_norm_g': 1.0 + 0.02 * nrm(ks[4], (DEPTH, 2, HEAD_DIM)),
        'mlstm_norm_g': 1.0 + 0.02 * nrm(ks[5], (DEPTH, BRANCH_WIDTH)),
        'diff_lambda': 0.1 * nrm(ks[6], (DEPTH, 4, DIFF_QK_DIM)),
        'diff_norm_g': 1.0 + 0.02 * nrm(ks[7], (DEPTH, DIFF_V_DIM)),
        'rel_bias': 0.5 * nrm(ks[8], (REL_BUCKETS, DIFF_HEADS)),
        'w_branch': nrm(ks[9], (DEPTH, N_BRANCHES, BRANCH_WIDTH, D_MODEL)) * BRANCH_WIDTH ** -0.5,
        'w_out': nrm(ks[10], (DEPTH, D_MODEL, D_MODEL)) * D_MODEL ** -0.5,
        'norm_ffn_g': 1.0 + 0.02 * nrm(ks[11], (DEPTH, D_MODEL)),
        'w_up': nrm(ks[12], (DEPTH, D_MODEL, 2 * D_FF)) * D_MODEL ** -0.5,
        'conv_w': nrm(ks[13], (DEPTH, CONV_W, D_FF)) * CONV_W ** -0.5,
        'conv_b': 0.02 * nrm(ks[14], (DEPTH, D_FF)),
        'w_down': nrm(ks[15], (DEPTH, D_FF, D_MODEL)) * D_FF ** -0.5,
        'final_norm_g': 1.0 + 0.02 * nrm(ks[16], (D_MODEL,)),
    }


def reference(x, norm_mix_g, w_in, mlstm_gate_bias, qk_norm_g, mlstm_norm_g, diff_lambda, diff_norm_g, rel_bias, w_branch, w_out, norm_ffn_g, w_up, conv_w, conv_b, w_down, final_norm_g):
    b, s, _ = x.shape
    cos, sin = axial_rope_tables(s)
    for layer in range(DEPTH):
        u = rms_norm(x, norm_mix_g[layer])
        proj = jnp.einsum('bsd,de->bse', u, w_in[layer])
        (a_in, bq, bk, bv, cq, ck, cv, co, cgate, dq, dk, dv, gates) = split_cols(proj)
        y_a = fourier_mix(a_in)
        y_b = gqa_branch(bq, bk, bv, qk_norm_g[layer], cos, sin)
        y_c = mlstm_branch(cq, ck, cv, co, cgate, mlstm_gate_bias[layer], mlstm_norm_g[layer])
        y_d = diff_branch(dq, dk, dv, diff_lambda[layer], diff_norm_g[layer], rel_bias, layer + 1)
        ys = jnp.stack([y_a, y_b, y_c, y_d], axis=2)
        branch = jnp.einsum('bsnc,ncd->bsnd', ys, w_branch[layer])
        g = jax.nn.sigmoid(gates.reshape(b, s, N_BRANCHES, D_MODEL))
        merged = jnp.sum(g * branch, axis=2)
        x = x + merged @ w_out[layer]
        v = rms_norm(x, norm_ffn_g[layer])
        x = x + conv_ffn(v, w_up[layer], conv_w[layer], conv_b[layer], w_down[layer])
    return rms_norm(x, final_norm_g)
```

```python
import functools
import math

import jax
import jax.numpy as jnp
from jax import lax
from jax.experimental import pallas as pl
from jax.experimental.pallas import tpu as pltpu

F32 = jnp.float32
BF16 = jnp.bfloat16

GRID_W = 64
HEAD_DIM = 64
BRANCH_WIDTH = 512
N_BRANCHES = 4
FOURIER_GROUP_DIM = 64
GQA_Q_HEADS = 8
GQA_KV_HEADS = 2
MLSTM_HEADS = 4
MLSTM_HEAD_DIM = 128
MLSTM_CHUNK = 128
DIFF_HEADS = 4
DIFF_QK_DIM = 64
DIFF_V_DIM = 128
REL_BUCKETS = 32
REL_MAX_DIST = 128
CONV_W = 3
ROPE_BASE = 10000.0
EPS = 1e-6
LOG2E = math.log2(math.e)

V7X_LANES = 128
V7X_VMEM_BYTES = 64 * 1024 * 1024
V7X_VMEM_CAP = V7X_VMEM_BYTES - 8 * 1024 * 1024

OFF_GATES = 0
OFF_A = 4096
OFF_BQ = 4608
OFF_CQ = 5120
OFF_CK = 5632
OFF_CV = 6144
OFF_CO = 6656
OFF_DQ = 7168
OFF_DK = 7680
OFF_DV = 8192
OFF_BKV = 8704
P_WIDTH = 8960


def _params(sem, vmem_bytes):
    limit = int(min(max(vmem_bytes * 3 // 2 + (4 << 20), 16 << 20), V7X_VMEM_CAP))
    return pltpu.CompilerParams(dimension_semantics=sem, vmem_limit_bytes=limit)


def _norm_mm_kernel(x_ref, g_ref, w_ref, o_ref, xn_ref):
    @pl.when(pl.program_id(1) == 0)
    def _():
        x = x_ref[...]
        ms = jnp.mean(x * x, axis=-1, keepdims=True)
        xn_ref[...] = (x * lax.rsqrt(ms + EPS) * g_ref[...]).astype(BF16)

    o_ref[...] = jnp.dot(xn_ref[...], w_ref[...], preferred_element_type=F32).astype(o_ref.dtype)


def _norm_mm_gate_kernel(x_ref, g_ref, w_ref, wg_ref, o_ref, og_ref, xn_ref):
    @pl.when(pl.program_id(1) == 0)
    def _():
        x = x_ref[...]
        ms = jnp.mean(x * x, axis=-1, keepdims=True)
        xn = (x * lax.rsqrt(ms + EPS) * g_ref[...]).astype(BF16)
        xn_ref[...] = xn
        og_ref[...] = jnp.dot(xn, wg_ref[...], preferred_element_type=F32)

    o_ref[...] = jnp.dot(xn_ref[...], w_ref[...], preferred_element_type=F32).astype(o_ref.dtype)


def norm_matmul(x, g, w, w_gate=None, *, tm, tn):
    m, k = x.shape
    n = w.shape[1]
    grid = (m // tm, n // tn)
    vmem = 2 * tm * k * 4 + tm * k * 2 + 2 * k * tn * 2 + 2 * tm * tn * 2 + 4 * tm * k
    x_spec = pl.BlockSpec((tm, k), lambda i, j: (i, 0))
    g_spec = pl.BlockSpec((1, k), lambda i, j: (0, 0))
    w_spec = pl.BlockSpec((k, tn), lambda i, j: (0, j))
    o_spec = pl.BlockSpec((tm, tn), lambda i, j: (i, j))
    scratch = [pltpu.VMEM((tm, k), BF16)]
    g2 = g.reshape(1, k).astype(F32)
    if w_gate is None:
        return pl.pallas_call(
            _norm_mm_kernel,
            grid=grid,
            in_specs=[x_spec, g_spec, w_spec],
            out_specs=o_spec,
            out_shape=jax.ShapeDtypeStruct((m, n), BF16),
            scratch_shapes=scratch,
            compiler_params=_params(("parallel", "arbitrary"), vmem),
            name="norm_matmul",
        )(x, g2, w)
    ng = w_gate.shape[1]
    return pl.pallas_call(
        _norm_mm_gate_kernel,
        grid=grid,
        in_specs=[x_spec, g_spec, w_spec, pl.BlockSpec((k, ng), lambda i, j: (0, 0))],
        out_specs=[o_spec, pl.BlockSpec((tm, ng), lambda i, j: (i, 0))],
        out_shape=[jax.ShapeDtypeStruct((m, n), BF16), jax.ShapeDtypeStruct((m, ng), F32)],
        scratch_shapes=scratch,
        compiler_params=_params(("parallel", "arbitrary"), vmem),
        name="norm_matmul_gate",
    )(x, g2, w, w_gate)


def _fourier_kernel(a_ref, bd_ref, dft_ref, o_ref, z_ref, *, row_chunk):
    s = a_ref.shape[1]
    w = a_ref.shape[2]

    @pl.when(pl.program_id(1) == 0)
    def _():
        for r in range(0, s, row_chunk):
            a = a_ref[0, r:r + row_chunk, :]
            zc = jnp.dot(a, bd_ref[...], preferred_element_type=F32)
            z_ref[r:r + row_chunk, :] = zc[:, :w].astype(BF16)
            z_ref[s + r:s + r + row_chunk, :] = zc[:, w:].astype(BF16)

    o_ref[0] = jnp.dot(dft_ref[...], z_ref[...], preferred_element_type=F32).astype(o_ref.dtype)


def fourier_tables(s):
    cg = FOURIER_GROUP_DIM
    jj = jnp.arange(cg, dtype=jnp.int32)
    ang_c = (2.0 * math.pi / cg) * ((jj[:, None] * jj[None, :]) % cg).astype(F32)
    eye_g = jnp.eye(BRANCH_WIDTH // cg, dtype=F32)
    bd_c = jnp.kron(eye_g, jnp.cos(ang_c)) * cg ** -0.5
    bd_s = jnp.kron(eye_g, jnp.sin(ang_c)) * cg ** -0.5
    bd = jnp.concatenate([bd_c, bd_s], axis=1).astype(BF16)
    kk = jnp.arange(s, dtype=jnp.int32)
    ang_hi = (2.0 * math.pi / cg) * ((kk[:, None] * jj[None, :]) % cg).astype(F32)
    n_lo = s // cg
    ll = jnp.arange(n_lo, dtype=jnp.int32)
    ang_lo = (2.0 * math.pi / s) * ((kk[:, None] * ll[None, :]) % s).astype(F32)
    ch, sh = jnp.cos(ang_hi)[:, :, None], jnp.sin(ang_hi)[:, :, None]
    cl, sl = jnp.cos(ang_lo)[:, None, :], jnp.sin(ang_lo)[:, None, :]
    scale = s ** -0.5
    cos_t = ((ch * cl - sh * sl) * scale).reshape(s, s)
    sin_t = ((sh * cl + ch * sl) * scale).reshape(s, s)
    dft = jnp.concatenate([cos_t, -sin_t], axis=1).astype(BF16)
    return bd, dft


def fourier_mix(p3, bd, dft, *, tm):
    b, s, _ = p3.shape
    w = BRANCH_WIDTH
    vmem = 2 * s * w * 2 + 2 * tm * 2 * s * 2 + 2 * s * w * 2 + 2 * w * 2 * w * 2 + 2 * tm * w * 2 + 8 * tm * w
    return pl.pallas_call(
        functools.partial(_fourier_kernel, row_chunk=min(s, 512)),
        grid=(b, s // tm),
        in_specs=[
            pl.BlockSpec((1, s, w), lambda bi, i: (bi, 0, OFF_A // w)),
            pl.BlockSpec((w, 2 * w), lambda bi, i: (0, 0)),
            pl.BlockSpec((tm, 2 * s), lambda bi, i: (i, 0)),
        ],
        out_specs=pl.BlockSpec((1, tm, w), lambda bi, i: (bi, i, 0)),
        out_shape=jax.ShapeDtypeStruct((b, s, w), BF16),
        scratch_shapes=[pltpu.VMEM((2 * s, w), BF16)],
        compiler_params=_params(("parallel", "arbitrary"), vmem),
        name="fourier",
    )(p3, bd, dft)


def rope_tables(s):
    rows = s // GRID_W
    row_id = jnp.repeat(jnp.arange(rows, dtype=F32), GRID_W)
    col_id = jnp.tile(jnp.arange(GRID_W, dtype=F32), rows)
    n_pairs = HEAD_DIM // 4
    inv_freq = ROPE_BASE ** (-jnp.arange(n_pairs, dtype=F32) / n_pairs)
    ang = jnp.concatenate([row_id[:, None] * inv_freq, col_id[:, None] * inv_freq], axis=-1)
    cos, sin = jnp.cos(ang), jnp.sin(ang)
    return jnp.concatenate([cos, cos], axis=-1), jnp.concatenate([-sin, sin], axis=-1)


def _norm_rope(xh, g, cos2, sin2):
    half = HEAD_DIM // 2
    ms = jnp.mean(xh * xh, axis=-1, keepdims=True)
    y = xh * lax.rsqrt(ms + EPS) * g
    rot = jnp.concatenate([y[:, half:], y[:, :half]], axis=-1)
    return y * cos2 + rot * sin2


def _gqa_prep_kernel(q_ref, kv_ref, g_ref, cos_ref, sin_ref, qo_ref, ko_ref, vo_ref):
    d = HEAD_DIM
    cos2, sin2 = cos_ref[...], sin_ref[...]
    gq, gk = g_ref[0:1, :], g_ref[1:2, :]
    q = q_ref[...].astype(F32)
    kv = kv_ref[...].astype(F32)
    q_scale = d ** -0.5 * LOG2E
    qs = [_norm_rope(q[:, h * d:(h + 1) * d], gq, cos2, sin2) * q_scale for h in range(GQA_Q_HEADS)]
    qo_ref[...] = jnp.concatenate(qs, axis=-1).astype(BF16)
    ks = [_norm_rope(kv[:, h * d:(h + 1) * d], gk, cos2, sin2) for h in range(GQA_KV_HEADS)]
    ko_ref[...] = jnp.concatenate(ks, axis=-1).astype(BF16)
    tm = q.shape[0]
    lane = lax.broadcasted_iota(jnp.int32, (tm, d), 1)
    ones_col = jnp.where(lane == 0, 1.0, 0.0).astype(F32)
    off = GQA_KV_HEADS * d
    vs = []
    for h in range(GQA_KV_HEADS):
        vs += [kv[:, off + h * d:off + (h + 1) * d], ones_col]
    vo_ref[...] = jnp.concatenate(vs, axis=-1).astype(BF16)


def gqa_prep(p2, qk_g, cos2, sin2, *, s, tm):
    m = p2.shape[0]
    nq = GQA_Q_HEADS * HEAD_DIM
    nkv = GQA_KV_HEADS * HEAD_DIM
    tiles_per_seq = s // tm
    return pl.pallas_call(
        _gqa_prep_kernel,
        grid=(m // tm,),
        in_specs=[
            pl.BlockSpec((tm, nq), lambda i: (i, OFF_BQ // nq)),
            pl.BlockSpec((tm, 2 * nkv), lambda i: (i, OFF_BKV // (2 * nkv))),
            pl.BlockSpec((2, HEAD_DIM), lambda i: (0, 0)),
            pl.BlockSpec((tm, HEAD_DIM), lambda i: (i % tiles_per_seq, 0)),
            pl.BlockSpec((tm, HEAD_DIM), lambda i: (i % tiles_per_seq, 0)),
        ],
        out_specs=[
            pl.BlockSpec((tm, nq), lambda i: (i, 0)),
            pl.BlockSpec((tm, nkv), lambda i: (i, 0)),
            pl.BlockSpec((tm, 2 * nkv), lambda i: (i, 0)),
        ],
        out_shape=[
            jax.ShapeDtypeStruct((m, nq), BF16),
            jax.ShapeDtypeStruct((m, nkv), BF16),
            jax.ShapeDtypeStruct((m, 2 * nkv), BF16),
        ],
        compiler_params=_params(("parallel",), 16 * tm * nq * 4),
        name="gqa_prep",
    )(p2, p2, qk_g.astype(F32), cos2, sin2)


def _gqa_attn_kernel(q_ref, kt_ref, v_ref, o_ref, m_ref, acc_ref, *, tk):
    d = HEAD_DIM
    tq = q_ref.shape[1]
    grp = q_ref.shape[2] // d
    s = kt_ref.shape[2]
    q_all = q_ref[0]
    q = jnp.concatenate([q_all[:, g * d:(g + 1) * d] for g in range(grp)], axis=0)
    m_ref[...] = jnp.full(m_ref.shape, -jnp.inf, F32)
    acc_ref[...] = jnp.zeros(acc_ref.shape, F32)

    def body(c, carry):
        off = pl.multiple_of(c * tk, tk)
        kt = kt_ref[0, :, pl.ds(off, tk)]
        v = v_ref[0, pl.ds(off, tk), :]
        sc = jnp.dot(q, kt, preferred_element_type=F32)
        m_prev = m_ref[...]
        m_new = jnp.maximum(m_prev, jnp.max(sc, axis=-1, keepdims=True))
        alpha = jnp.exp2(m_prev - m_new)
        p = jnp.exp2(sc - pltpu.repeat(m_new, tk // V7X_LANES, axis=1)).astype(BF16)
        acc_ref[...] = acc_ref[...] * alpha + jnp.dot(p, v, preferred_element_type=F32)
        m_ref[...] = m_new
        return carry

    lax.fori_loop(0, s // tk, body, 0)
    acc = acc_ref[...]
    out = acc[:, :d] / acc[:, d:d + 1]
    o_ref[0] = jnp.concatenate([out[g * tq:(g + 1) * tq] for g in range(grp)], axis=-1).astype(o_ref.dtype)


def gqa_attention(q3, kt3, v3, *, tq, tk):
    b, s, nq = q3.shape
    d = HEAD_DIM
    grp = GQA_Q_HEADS // GQA_KV_HEADS
    mrows = grp * tq
    vmem = 2 * (tq * grp * d * 2 + d * s * 2 + s * 128 * 2 + tq * grp * d * 2) + 2 * mrows * 128 * 4 + 3 * mrows * tk * 4
    return pl.pallas_call(
        functools.partial(_gqa_attn_kernel, tk=tk),
        grid=(b, GQA_KV_HEADS, s // tq),
        in_specs=[
            pl.BlockSpec((1, tq, grp * d), lambda bi, kv, i: (bi, i, kv)),
            pl.BlockSpec((1, d, s), lambda bi, kv, i: (bi, kv, 0)),
            pl.BlockSpec((1, s, 2 * d), lambda bi, kv, i: (bi, 0, kv)),
        ],
        out_specs=pl.BlockSpec((1, tq, grp * d), lambda bi, kv, i: (bi, i, kv)),
        out_shape=jax.ShapeDtypeStruct((b, s, nq), BF16),
        scratch_shapes=[pltpu.VMEM((mrows, V7X_LANES), F32), pltpu.VMEM((mrows, 2 * d), F32)],
        compiler_params=_params(("parallel", "parallel", "parallel"), vmem),
        name="gqa_attn",
    )(q3, kt3, v3)


def _mlstm_chunk(direction, c, m, q_ref, k_ref, v_ref, r_sc, cm_sc, b_sc, st_ref, out_sc):
    L = MLSTM_CHUNK
    dh = MLSTM_HEAD_DIM
    scale = dh ** -0.5
    off = pl.multiple_of(c * L, L)
    q = q_ref[0, pl.ds(off, L), :]
    k = k_ref[0, pl.ds(off, L), :]
    v = v_ref[0, pl.ds(off, L), :]
    r_row = r_sc[direction, pl.ds(c, 1), :]
    cm_row = cm_sc[direction, pl.ds(c, 1), :]
    b_row = b_sc[direction, pl.ds(c, 1), :]
    rmax = jnp.max(r_row, axis=-1, keepdims=True)
    btot = b_row[:, L - 1:L] if direction == 0 else b_row[:, 0:1]

    def col(x_row):
        return jnp.transpose(jnp.broadcast_to(x_row, (L, L)))

    cmat = jnp.maximum(m, col(cm_row))
    row_i = lax.broadcasted_iota(jnp.int32, (L, L), 0)
    col_i = lax.broadcasted_iota(jnp.int32, (L, L), 1)
    mask = (row_i >= col_i) if direction == 0 else (row_i <= col_i)
    w_intra = jnp.where(mask, jnp.exp(r_row - cmat), 0.0)
    w_inter = jnp.exp(m - cmat)
    den_floor = jnp.exp(-(col(b_row) + cmat))
    c_last = jnp.maximum(m, rmax)
    w_state = jnp.exp(col(r_row) - c_last) * scale
    decay = jnp.exp(m - c_last)

    s_qk = lax.dot_general(q, k, (((1,), (1,)), ((), ())), preferred_element_type=F32) * (w_intra * scale)
    v32 = v.astype(F32)
    v_aug = jnp.concatenate([v, jnp.ones((L, dh), BF16)], axis=-1)
    state = st_ref[...]
    inter = jnp.dot(q, state.astype(BF16), preferred_element_type=F32)
    intra = jnp.dot(s_qk.astype(BF16), v_aug, preferred_element_type=F32)
    h_aug = jnp.concatenate([w_inter, w_inter], axis=-1) * inter + intra
    out_sc[pl.ds(off, L), :] = h_aug[:, :dh] / jnp.maximum(jnp.abs(h_aug[:, dh:]), den_floor)
    kv_w = jnp.concatenate([w_state * v32, w_state], axis=-1).astype(BF16)
    upd = lax.dot_general(k, kv_w, (((0,), (0,)), ((), ())), preferred_element_type=F32)
    st_ref[...] = decay * state + upd
    return btot + c_last


def _mlstm_kernel(bias_ref, q_ref, k_ref, v_ref, o_ref, gate_ref, g_ref, y_ref,
                  r_sc, cm_sc, b_sc, hf_sc, hb_sc, stf_sc, stb_sc):
    L = MLSTM_CHUNK
    head = pl.program_id(1)
    nc = q_ref.shape[1] // L
    lane = lax.broadcasted_iota(jnp.int32, (nc, L), 1)
    shifts = [1 << t for t in range(int(math.log2(L)))]
    for d in range(2):
        i_pre = gate_ref[0, 2 * d, 0] + bias_ref[2 * d, head]
        f_pre = gate_ref[0, 2 * d + 1, 0] + bias_ref[2 * d + 1, head]
        logf = jnp.minimum(f_pre, 0.0) - jnp.log1p(jnp.exp(-jnp.abs(f_pre)))
        bc = logf
        for sh in shifts:
            if d == 0:
                bc = bc + jnp.where(lane >= sh, pltpu.roll(bc, sh, axis=1), 0.0)
            else:
                bc = bc + jnp.where(lane < L - sh, pltpu.roll(bc, L - sh, axis=1), 0.0)
        r = i_pre - bc
        cm = r
        for sh in shifts:
            if d == 0:
                cm = jnp.maximum(cm, jnp.where(lane >= sh, pltpu.roll(cm, sh, axis=1), -jnp.inf))
            else:
                cm = jnp.maximum(cm, jnp.where(lane < L - sh, pltpu.roll(cm, L - sh, axis=1), -jnp.inf))
        r_sc[d] = r
        cm_sc[d] = cm
        b_sc[d] = bc
    stf_sc[...] = jnp.zeros(stf_sc.shape, F32)
    stb_sc[...] = jnp.zeros(stb_sc.shape, F32)

    def body(c, carry):
        m_f, m_b = carry
        m_f = _mlstm_chunk(0, c, m_f, q_ref, k_ref, v_ref, r_sc, cm_sc, b_sc, stf_sc, hf_sc)
        m_b = _mlstm_chunk(1, nc - 1 - c, m_b, q_ref, k_ref, v_ref, r_sc, cm_sc, b_sc, stb_sc, hb_sc)
        return m_f, m_b

    zero = jnp.zeros((1, 1), F32)
    lax.fori_loop(0, nc, body, (zero, zero))
    hsum = hf_sc[...] + hb_sc[...]
    ms = jnp.mean(hsum * hsum, axis=-1, keepdims=True)
    y = hsum * lax.rsqrt(ms + EPS) * g_ref[...]
    y_ref[0] = (jax.nn.sigmoid(o_ref[0].astype(F32)) * y).astype(y_ref.dtype)


def mlstm_branch(p3, gates5, gate_bias, norm_g):
    b, s, _ = p3.shape
    L = MLSTM_CHUNK
    dh = MLSTM_HEAD_DIM
    nc = s // L
    blk = lambda off: pl.BlockSpec((1, s, dh), lambda bi, h, off=off: (bi, 0, off // dh + h))
    vmem = 2 * 5 * s * dh * 2 + 2 * s * dh * 4 + 6 * nc * L * 4 + 2 * dh * 2 * dh * 4 + 4 * s * dh * 4
    return pl.pallas_call(
        _mlstm_kernel,
        grid=(b, MLSTM_HEADS),
        in_specs=[
            pl.BlockSpec(memory_space=pltpu.SMEM),
            blk(OFF_CQ), blk(OFF_CK), blk(OFF_CV), blk(OFF_CO),
            pl.BlockSpec((1, 4, 1, nc, L), lambda bi, h: (bi, 0, h, 0, 0)),
            pl.BlockSpec((1, dh), lambda bi, h: (0, h)),
        ],
        out_specs=pl.BlockSpec((1, s, dh), lambda bi, h: (bi, 0, h)),
        out_shape=jax.ShapeDtypeStruct((b, s, BRANCH_WIDTH), BF16),
        scratch_shapes=[
            pltpu.VMEM((2, nc, L), F32), pltpu.VMEM((2, nc, L), F32), pltpu.VMEM((2, nc, L), F32),
            pltpu.VMEM((s, dh), F32), pltpu.VMEM((s, dh), F32),
            pltpu.VMEM((dh, 2 * dh), F32), pltpu.VMEM((dh, 2 * dh), F32),
        ],
        compiler_params=_params(("parallel", "parallel"), vmem),
        name="mlstm",
    )(gate_bias.astype(F32), p3, p3, p3, p3, gates5, norm_g.reshape(1, BRANCH_WIDTH).astype(F32))


def _rel_bucket(rel):
    half = REL_BUCKETS // 2
    max_exact = half // 2
    ret = jnp.where(rel > 0, half, 0)
    n = jnp.abs(rel)
    nf = jnp.maximum(n, 1).astype(F32)
    large = max_exact + (jnp.log(nf / max_exact) / math.log(REL_MAX_DIST / max_exact) * (half - max_exact)).astype(jnp.int32)
    large = jnp.minimum(large, half - 1)
    return ret + jnp.where(n < max_exact, n, large)


def diff_bias_tiles(rel_bias, t):
    assert t >= REL_MAX_DIST
    delta = jnp.arange(-2, 3, dtype=jnp.int32)[:, None, None] * t
    rel = delta + jnp.arange(t, dtype=jnp.int32)[None, None, :] - jnp.arange(t, dtype=jnp.int32)[None, :, None]
    tiles = rel_bias.astype(F32)[_rel_bucket(rel)]
    return jnp.transpose(tiles, (3, 0, 1, 2)) * LOG2E


def _diff_attn_kernel(q_ref, kt_ref, v_ref, bias_ref, lam_ref, g_ref, o_ref, m_ref, acc_ref, *, t, lam_init):
    dq = DIFF_QK_DIM
    dv = DIFF_V_DIM
    s = kt_ref.shape[2]
    i = pl.program_id(2)
    q_scale = dq ** -0.5 * LOG2E
    q_all = (q_ref[0].astype(F32) * q_scale).astype(BF16)
    qs = [q_all[:, :dq], q_all[:, dq:]]
    m_ref[...] = jnp.full(m_ref.shape, -jnp.inf, F32)
    acc_ref[...] = jnp.zeros(acc_ref.shape, F32)

    def body(j, carry):
        off = pl.multiple_of(j * t, t)
        bias = bias_ref[0, jnp.clip(j - i, -2, 2) + 2]
        v = v_ref[0, pl.ds(off, t), :]
        for mp in range(2):
            kt = kt_ref[0, mp * dq:(mp + 1) * dq, pl.ds(off, t)]
            sc = jnp.dot(qs[mp], kt, preferred_element_type=F32) + bias
            m_prev = m_ref[mp]
            m_new = jnp.maximum(m_prev, jnp.max(sc, axis=-1, keepdims=True))
            alpha = jnp.exp2(m_prev - m_new)
            p = jnp.exp2(sc - pltpu.repeat(m_new, t // V7X_LANES, axis=1))
            l_new = alpha * acc_ref[mp, :, dv:] + jnp.sum(p, axis=-1, keepdims=True)
            pv = jnp.dot(p.astype(BF16), v, preferred_element_type=F32)
            acc_ref[mp, :, :dv] = acc_ref[mp, :, :dv] * alpha + pv
            acc_ref[mp, :, dv:] = l_new
            m_ref[mp] = m_new
        return carry

    lax.fori_loop(0, s // t, body, 0)
    lp = lam_ref[...]
    lam = (jnp.exp(jnp.sum(lp[0:1] * lp[1:2], axis=-1, keepdims=True))
           - jnp.exp(jnp.sum(lp[2:3] * lp[3:4], axis=-1, keepdims=True)) + lam_init)
    o0 = acc_ref[0, :, :dv] / acc_ref[0, :, dv:]
    o1 = acc_ref[1, :, :dv] / acc_ref[1, :, dv:]
    o = o0 - lam * o1
    ms = jnp.mean(o * o, axis=-1, keepdims=True)
    o_ref[0] = (o * lax.rsqrt(ms + EPS) * g_ref[...] * (1.0 - lam_init)).astype(o_ref.dtype)


def diff_attention(p3, kt3, bias_tiles, lam_params, sub_g, *, t, layer_number):
    b, s, _ = p3.shape
    dv = DIFF_V_DIM
    lam_init = 0.8 - 0.6 * math.exp(-0.3 * (layer_number - 1))
    vmem = 2 * (t * 128 * 2 + 128 * s * 2 + s * dv * 2 + 5 * t * t * 4 + t * dv * 2) + 2 * t * 128 * 4 * 3 + 6 * t * t * 4
    return pl.pallas_call(
        functools.partial(_diff_attn_kernel, t=t, lam_init=lam_init),
        grid=(b, DIFF_HEADS, s // t),
        in_specs=[
            pl.BlockSpec((1, t, 128), lambda bi, h, i: (bi, i, OFF_DQ // 128 + h)),
            pl.BlockSpec((1, 128, s), lambda bi, h, i: (bi, h, 0)),
            pl.BlockSpec((1, s, dv), lambda bi, h, i: (bi, 0, OFF_DV // dv + h)),
            pl.BlockSpec((1, 5, t, t), lambda bi, h, i: (h, 0, 0, 0)),
            pl.BlockSpec((4, DIFF_QK_DIM), lambda bi, h, i: (0, 0)),
            pl.BlockSpec((1, dv), lambda bi, h, i: (0, 0)),
        ],
        out_specs=pl.BlockSpec((1, t, dv), lambda bi, h, i: (bi, i, h)),
        out_shape=jax.ShapeDtypeStruct((b, s, BRANCH_WIDTH), BF16),
        scratch_shapes=[pltpu.VMEM((2, t, V7X_LANES), F32), pltpu.VMEM((2, t, 2 * dv), F32)],
        compiler_params=_params(("parallel", "parallel", "parallel"), vmem),
        name="diff_attn",
    )(p3, kt3, p3, bias_tiles, lam_params.astype(F32), sub_g.reshape(1, dv).astype(F32))


def _merge_kernel(ya_ref, yb_ref, yc_ref, yd_ref, gate_ref, wb_ref, wo_ref, x_ref, o_ref):
    dm = x_ref.shape[1]
    merged = None
    for n, y_ref in enumerate((ya_ref, yb_ref, yc_ref, yd_ref)):
        br = jnp.dot(y_ref[...], wb_ref[n], preferred_element_type=F32)
        term = jax.nn.sigmoid(gate_ref[:, n * dm:(n + 1) * dm].astype(F32)) * br
        merged = term if merged is None else merged + term
    o_ref[...] = x_ref[...] + jnp.dot(merged.astype(BF16), wo_ref[...], preferred_element_type=F32)


def merge_branches(ys, p2, w_branch, w_out, x2, *, tm):
    m, dm = x2.shape
    w = BRANCH_WIDTH
    y_spec = pl.BlockSpec((tm, w), lambda i: (i, 0))
    vmem = 2 * (4 * tm * w * 2 + tm * 4 * dm * 2 + 4 * w * dm * 2 + dm * dm * 2 + 2 * tm * dm * 4) + 6 * tm * dm * 4
    return pl.pallas_call(
        _merge_kernel,
        grid=(m // tm,),
        in_specs=[
            y_spec, y_spec, y_spec, y_spec,
            pl.BlockSpec((tm, N_BRANCHES * dm), lambda i: (i, OFF_GATES // (N_BRANCHES * dm))),
            pl.BlockSpec((N_BRANCHES, w, dm), lambda i: (0, 0, 0)),
            pl.BlockSpec((dm, dm), lambda i: (0, 0)),
            pl.BlockSpec((tm, dm), lambda i: (i, 0)),
        ],
        out_specs=pl.BlockSpec((tm, dm), lambda i: (i, 0)),
        out_shape=jax.ShapeDtypeStruct((m, dm), F32),
        compiler_params=_params(("parallel",), vmem),
        name="merge",
    )(*ys, p2, w_branch, w_out, x2)


HALO_ROWS = 16


def _ffn_down_kernel(a_ref, lin_ref, ap_ref, an_ref, cw_ref, cb_ref, wd_ref, x_ref, gf_ref, o_ref,
                     *, tiles_per_seq, final_norm):
    tm = a_ref.shape[0]
    i = pl.program_id(0)
    pos = i % tiles_per_seq
    a = a_ref[...].astype(F32)
    prev_row = jnp.where(pos == 0, 0.0, ap_ref[HALO_ROWS - 1:HALO_ROWS, :].astype(F32))
    next_row = jnp.where(pos == tiles_per_seq - 1, 0.0, an_ref[0:1, :].astype(F32))
    rows = lax.broadcasted_iota(jnp.int32, (tm, 1), 0)
    a_m1 = jnp.where(rows == 0, prev_row, pltpu.roll(a, 1, axis=0))
    a_p1 = jnp.where(rows == tm - 1, next_row, pltpu.roll(a, tm - 1, axis=0))
    cw = cw_ref[...]
    c = a_m1 * cw[0:1] + a * cw[1:2] + a_p1 * cw[2:3] + cb_ref[...]
    gelu = 0.5 * c * (1.0 + jnp.tanh(math.sqrt(2.0 / math.pi) * (c + 0.044715 * (c * c * c))))
    hmid = (gelu * lin_ref[...].astype(F32)).astype(BF16)
    y = x_ref[...] + jnp.dot(hmid, wd_ref[...], preferred_element_type=F32)
    if final_norm:
        ms = jnp.mean(y * y, axis=-1, keepdims=True)
        y = y * lax.rsqrt(ms + EPS) * gf_ref[...]
    o_ref[...] = y


def ffn_down(up2, conv_w, conv_b, w_down, x2, final_g, *, s, tm, final_norm):
    m, dm = x2.shape
    f = w_down.shape[0]
    tiles_per_seq = s // tm
    hb = tm // HALO_ROWS
    n_halo = m // HALO_ROWS
    vmem = 2 * (2 * tm * f * 2 + 2 * HALO_ROWS * f * 2 + f * dm * 2 + 2 * tm * dm * 4) + 8 * tm * f * 4
    return pl.pallas_call(
        functools.partial(_ffn_down_kernel, tiles_per_seq=tiles_per_seq, final_norm=final_norm),
        grid=(m // tm,),
        in_specs=[
            pl.BlockSpec((tm, f), lambda i: (i, 0)),
            pl.BlockSpec((tm, f), lambda i: (i, 1)),
            pl.BlockSpec((HALO_ROWS, f), lambda i: (jnp.maximum(i * hb - 1, 0), 0)),
            pl.BlockSpec((HALO_ROWS, f), lambda i: (jnp.minimum((i + 1) * hb, n_halo - 1), 0)),
            pl.BlockSpec((CONV_W, f), lambda i: (0, 0)),
            pl.BlockSpec((1, f), lambda i: (0, 0)),
            pl.BlockSpec((f, dm), lambda i: (0, 0)),
            pl.BlockSpec((tm, dm), lambda i: (i, 0)),
            pl.BlockSpec((1, dm), lambda i: (0, 0)),
        ],
        out_specs=pl.BlockSpec((tm, dm), lambda i: (i, 0)),
        out_shape=jax.ShapeDtypeStruct((m, dm), F32),
        compiler_params=_params(("parallel",), vmem),
        name="ffn_down",
    )(up2, up2, up2, up2, conv_w.astype(F32), conv_b.reshape(1, f).astype(F32), w_down, x2,
      final_g.reshape(1, dm).astype(F32))


def _arrange_w_in(w):
    widths = (512, 512, 128, 128, 512, 512, 512, 512, 16, 512, 512, 512, 4096)
    offs = [0]
    for wd in widths:
        offs.append(offs[-1] + wd)
    seg = lambda n: w[:, offs[n]:offs[n + 1]]
    a, bq, bk, bv, cq, ck, cv, co, cgate, dq, dk, dv, gates = (seg(n) for n in range(len(widths)))
    main = jnp.concatenate([gates, a, bq, cq, ck, cv, co, dq, dk, dv, bk, bv], axis=1).astype(BF16)
    gate = jnp.pad(cgate, ((0, 0), (0, V7X_LANES - cgate.shape[1]))).astype(BF16)
    return main, gate


def kernel(x, norm_mix_g, w_in, mlstm_gate_bias, qk_norm_g, mlstm_norm_g, diff_lambda, diff_norm_g, rel_bias,
           w_branch, w_out, norm_ffn_g, w_up, conv_w, conv_b, w_down, final_norm_g):
    b, s, dm = x.shape
    depth = w_in.shape[0]
    m = b * s
    d_ff = w_down.shape[1]
    L = MLSTM_CHUNK
    t_diff = 512

    bd, dft = fourier_tables(s)
    cos2, sin2 = rope_tables(s)
    bias_tiles = diff_bias_tiles(rel_bias, t_diff)

    x2 = x.reshape(m, dm)
    for layer in range(depth):
        w_main, w_gate = _arrange_w_in(w_in[layer])
        p2, cgate = norm_matmul(x2, norm_mix_g[layer], w_main, w_gate, tm=1024, tn=1280)
        p3 = p2.reshape(b, s, P_WIDTH)

        y_a = fourier_mix(p3, bd, dft, tm=512)

        q_b, k_b, v_b = gqa_prep(p2, qk_norm_g[layer], cos2, sin2, s=s, tm=512)
        kt_b = jnp.swapaxes(k_b.reshape(b, s, GQA_KV_HEADS * HEAD_DIM), 1, 2)
        y_b = gqa_attention(q_b.reshape(b, s, -1), kt_b, v_b.reshape(b, s, -1), tq=128, tk=1024)

        gates5 = jnp.transpose(cgate[:, :4 * MLSTM_HEADS].reshape(b, s, 4, MLSTM_HEADS), (0, 2, 3, 1))
        gates5 = gates5.reshape(b, 4, MLSTM_HEADS, s // L, L)
        y_c = mlstm_branch(p3, gates5, mlstm_gate_bias[layer], mlstm_norm_g[layer])

        kt_d = jnp.swapaxes(p3[:, :, OFF_DK:OFF_DK + BRANCH_WIDTH], 1, 2)
        y_d = diff_attention(p3, kt_d, bias_tiles, diff_lambda[layer], diff_norm_g[layer],
                             t=t_diff, layer_number=layer + 1)

        ys = [y.reshape(m, BRANCH_WIDTH) for y in (y_a, y_b, y_c, y_d)]
        x2 = merge_branches(ys, p2, w_branch[layer].astype(BF16), w_out[layer].astype(BF16), x2, tm=512)

        up2 = norm_matmul(x2, norm_ffn_g[layer], w_up[layer].astype(BF16), tm=1024, tn=d_ff // 2)
        x2 = ffn_down(up2, conv_w[layer], conv_b[layer], w_down[layer].astype(BF16), x2, final_norm_g,
                      s=s, tm=256, final_norm=(layer == depth - 1))
    return x2.reshape(b, s, dm)
```

```python
import functools
import math

import jax
import jax.numpy as jnp
from jax import lax
from jax.experimental import pallas as pl
from jax.experimental.pallas import tpu as pltpu

F32 = jnp.float32
BF16 = jnp.bfloat16

GRID_W = 64
HEAD_DIM = 64
BRANCH_WIDTH = 512
N_BRANCHES = 4
FOURIER_GROUP_DIM = 64
GQA_Q_HEADS = 8
GQA_KV_HEADS = 2
MLSTM_HEADS = 4
MLSTM_HEAD_DIM = 128
MLSTM_CHUNK = 128
DIFF_HEADS = 4
DIFF_QK_DIM = 64
DIFF_V_DIM = 128
REL_BUCKETS = 32
REL_MAX_DIST = 128
CONV_W = 3
ROPE_BASE = 10000.0
EPS = 1e-6
LOG2E = math.log2(math.e)

V7X_LANES = 128
V7X_VMEM_BYTES = 64 * 1024 * 1024
V7X_VMEM_CAP = V7X_VMEM_BYTES - 8 * 1024 * 1024

OFF_GATES = 0
OFF_A = 4096
OFF_BQ = 4608
OFF_CQ = 5120
OFF_CK = 5632
OFF_CV = 6144
OFF_CO = 6656
OFF_DQ = 7168
OFF_DK = 7680
OFF_DV = 8192
OFF_BKV = 8704
P_WIDTH = 8960


def _params(sem, vmem_bytes):
    limit = int(min(max(vmem_bytes * 3 // 2 + (4 << 20), 16 << 20), V7X_VMEM_CAP))
    return pltpu.CompilerParams(dimension_semantics=sem, vmem_limit_bytes=limit)


def _norm_mm_kernel(x_ref, g_ref, w_ref, o_ref, xn_ref):
    @pl.when(pl.program_id(1) == 0)
    def _():
        x = x_ref[...]
        ms = jnp.mean(x * x, axis=-1, keepdims=True)
        xn_ref[...] = (x * lax.rsqrt(ms + EPS) * g_ref[...]).astype(BF16)

    o_ref[...] = jnp.dot(xn_ref[...], w_ref[...], preferred_element_type=F32).astype(o_ref.dtype)


def _norm_mm_gate_kernel(x_ref, g_ref, w_ref, wg_ref, o_ref, og_ref, xn_ref):
    @pl.when(pl.program_id(1) == 0)
    def _():
        x = x_ref[...]
        ms = jnp.mean(x * x, axis=-1, keepdims=True)
        xn = (x * lax.rsqrt(ms + EPS) * g_ref[...]).astype(BF16)
        xn_ref[...] = xn
        og_ref[...] = jnp.dot(xn, wg_ref[...], preferred_element_type=F32)

    o_ref[...] = jnp.dot(xn_ref[...], w_ref[...], preferred_element_type=F32).astype(o_ref.dtype)


def norm_matmul(x, g, w, w_gate=None, *, tm, tn):
    m, k = x.shape
    n = w.shape[1]
    grid = (m // tm, n // tn)
    vmem = 2 * tm * k * 4 + tm * k * 2 + 2 * k * tn * 2 + 2 * tm * tn * 2 + 4 * tm * k
    x_spec = pl.BlockSpec((tm, k), lambda i, j: (i, 0))
    g_spec = pl.BlockSpec((1, k), lambda i, j: (0, 0))
    w_spec = pl.BlockSpec((k, tn), lambda i, j: (0, j))
    o_spec = pl.BlockSpec((tm, tn), lambda i, j: (i, j))
    scratch = [pltpu.VMEM((tm, k), BF16)]
    g2 = g.reshape(1, k).astype(F32)
    if w_gate is None:
        return pl.pallas_call(
            _norm_mm_kernel,
            grid=grid,
            in_specs=[x_spec, g_spec, w_spec],
            out_specs=o_spec,
            out_shape=jax.ShapeDtypeStruct((m, n), BF16),
            scratch_shapes=scratch,
            compiler_params=_params(("parallel", "arbitrary"), vmem),
            name="norm_matmul",
        )(x, g2, w)
    ng = w_gate.shape[1]
    return pl.pallas_call(
        _norm_mm_gate_kernel,
        grid=grid,
        in_specs=[x_spec, g_spec, w_spec, pl.BlockSpec((k, ng), lambda i, j: (0, 0))],
        out_specs=[o_spec, pl.BlockSpec((tm, ng), lambda i, j: (i, 0))],
        out_shape=[jax.ShapeDtypeStruct((m, n), BF16), jax.ShapeDtypeStruct((m, ng), F32)],
        scratch_shapes=scratch,
        compiler_params=_params(("parallel", "arbitrary"), vmem),
        name="norm_matmul_gate",
    )(x, g2, w, w_gate)


def _fourier_kernel(a_ref, bd_ref, dft_ref, o_ref, z_ref, *, row_chunk):
    s = a_ref.shape[1]
    w = a_ref.shape[2]

    @pl.when(pl.program_id(1) == 0)
    def _():
        for r in range(0, s, row_chunk):
            a = a_ref[0, r:r + row_chunk, :]
            zc = jnp.dot(a, bd_ref[...], preferred_element_type=F32)
            z_ref[r:r + row_chunk, :] = zc[:, :w].astype(BF16)
            z_ref[s + r:s + r + row_chunk, :] = zc[:, w:].astype(BF16)

    o_ref[0] = jnp.dot(dft_ref[...], z_ref[...], preferred_element_type=F32).astype(o_ref.dtype)


def fourier_tables(s):
    cg = FOURIER_GROUP_DIM
    jj = jnp.arange(cg, dtype=jnp.int32)
    ang_c = (2.0 * math.pi / cg) * ((jj[:, None] * jj[None, :]) % cg).astype(F32)
    eye_g = jnp.eye(BRANCH_WIDTH // cg, dtype=F32)
    bd_c = jnp.kron(eye_g, jnp.cos(ang_c)) * cg ** -0.5
    bd_s = jnp.kron(eye_g, jnp.sin(ang_c)) * cg ** -0.5
    bd = jnp.concatenate([bd_c, bd_s], axis=1).astype(BF16)
    n_lo = s // cg
    nn = jnp.arange(s, dtype=jnp.int32)
    ang_hi = (2.0 * math.pi / cg) * ((jj[:, None] * nn[None, :]) % cg).astype(F32)
    ll = jnp.arange(n_lo, dtype=jnp.int32)
    ang_lo = (2.0 * math.pi / s) * ((ll[:, None] * nn[None, :]) % s).astype(F32)
    ch, sh = jnp.cos(ang_hi)[:, None, :], jnp.sin(ang_hi)[:, None, :]
    cl, sl = jnp.cos(ang_lo)[None, :, :], jnp.sin(ang_lo)[None, :, :]
    scale = s ** -0.5
    cos_t = ((ch * cl - sh * sl) * scale).astype(BF16).reshape(s, s)
    sin_t = ((sh * cl + ch * sl) * -scale).astype(BF16).reshape(s, s)
    dft = jnp.concatenate([cos_t, sin_t], axis=1)
    return bd, dft


def fourier_mix(p3, bd, dft, *, tm):
    b, s, _ = p3.shape
    w = BRANCH_WIDTH
    vmem = 2 * s * w * 2 + 2 * tm * 2 * s * 2 + 2 * s * w * 2 + 2 * w * 2 * w * 2 + 2 * tm * w * 2 + 8 * tm * w
    return pl.pallas_call(
        functools.partial(_fourier_kernel, row_chunk=min(s, 512)),
        grid=(b, s // tm),
        in_specs=[
            pl.BlockSpec((1, s, w), lambda bi, i: (bi, 0, OFF_A // w)),
            pl.BlockSpec((w, 2 * w), lambda bi, i: (0, 0)),
            pl.BlockSpec((tm, 2 * s), lambda bi, i: (i, 0)),
        ],
        out_specs=pl.BlockSpec((1, tm, w), lambda bi, i: (bi, i, 0)),
        out_shape=jax.ShapeDtypeStruct((b, s, w), BF16),
        scratch_shapes=[pltpu.VMEM((2 * s, w), BF16)],
        compiler_params=_params(("parallel", "arbitrary"), vmem),
        name="fourier",
    )(p3, bd, dft)


def rope_tables(s):
    rows = s // GRID_W
    row_id = jnp.repeat(jnp.arange(rows, dtype=F32), GRID_W)
    col_id = jnp.tile(jnp.arange(GRID_W, dtype=F32), rows)
    n_pairs = HEAD_DIM // 4
    inv_freq = ROPE_BASE ** (-jnp.arange(n_pairs, dtype=F32) / n_pairs)
    ang = jnp.concatenate([row_id[:, None] * inv_freq, col_id[:, None] * inv_freq], axis=-1)
    cos, sin = jnp.cos(ang), jnp.sin(ang)
    return jnp.concatenate([cos, cos], axis=-1), jnp.concatenate([-sin, sin], axis=-1)


def _norm_rope(xh, g, cos2, sin2):
    half = HEAD_DIM // 2
    ms = jnp.mean(xh * xh, axis=-1, keepdims=True)
    y = xh * lax.rsqrt(ms + EPS) * g
    rot = jnp.concatenate([y[:, half:], y[:, :half]], axis=-1)
    return y * cos2 + rot * sin2


def _gqa_prep_kernel(q_ref, kv_ref, g_ref, cos_ref, sin_ref, qo_ref, ko_ref, vo_ref):
    d = HEAD_DIM
    cos2, sin2 = cos_ref[...], sin_ref[...]
    gq, gk = g_ref[0:1, :], g_ref[1:2, :]
    q = q_ref[...].astype(F32)
    kv = kv_ref[...].astype(F32)
    q_scale = d ** -0.5 * LOG2E
    qs = [_norm_rope(q[:, h * d:(h + 1) * d], gq, cos2, sin2) * q_scale for h in range(GQA_Q_HEADS)]
    qo_ref[...] = jnp.concatenate(qs, axis=-1).astype(BF16)
    ks = [_norm_rope(kv[:, h * d:(h + 1) * d], gk, cos2, sin2) for h in range(GQA_KV_HEADS)]
    ko_ref[...] = jnp.concatenate(ks, axis=-1).astype(BF16)
    tm = q.shape[0]
    lane = lax.broadcasted_iota(jnp.int32, (tm, d), 1)
    ones_col = jnp.where(lane == 0, 1.0, 0.0).astype(F32)
    off = GQA_KV_HEADS * d
    vs = []
    for h in range(GQA_KV_HEADS):
        vs += [kv[:, off + h * d:off + (h + 1) * d], ones_col]
    vo_ref[...] = jnp.concatenate(vs, axis=-1).astype(BF16)


def gqa_prep(p2, qk_g, cos2, sin2, *, s, tm):
    m = p2.shape[0]
    nq = GQA_Q_HEADS * HEAD_DIM
    nkv = GQA_KV_HEADS * HEAD_DIM
    tiles_per_seq = s // tm
    return pl.pallas_call(
        _gqa_prep_kernel,
        grid=(m // tm,),
        in_specs=[
            pl.BlockSpec((tm, nq), lambda i: (i, OFF_BQ // nq)),
            pl.BlockSpec((tm, 2 * nkv), lambda i: (i, OFF_BKV // (2 * nkv))),
            pl.BlockSpec((2, HEAD_DIM), lambda i: (0, 0)),
            pl.BlockSpec((tm, HEAD_DIM), lambda i: (i % tiles_per_seq, 0)),
            pl.BlockSpec((tm, HEAD_DIM), lambda i: (i % tiles_per_seq, 0)),
        ],
        out_specs=[
            pl.BlockSpec((tm, nq), lambda i: (i, 0)),
            pl.BlockSpec((tm, nkv), lambda i: (i, 0)),
            pl.BlockSpec((tm, 2 * nkv), lambda i: (i, 0)),
        ],
        out_shape=[
            jax.ShapeDtypeStruct((m, nq), BF16),
            jax.ShapeDtypeStruct((m, nkv), BF16),
            jax.ShapeDtypeStruct((m, 2 * nkv), BF16),
        ],
        compiler_params=_params(("parallel",), 16 * tm * nq * 4),
        name="gqa_prep",
    )(p2, p2, qk_g.astype(F32), cos2, sin2)


def _gqa_attn_kernel(q_ref, kt_ref, v_ref, o_ref, q_sc, m_ref, acc_ref, *, tk):
    d = HEAD_DIM
    tq = q_ref.shape[1]
    grp = q_ref.shape[2] // d
    s = kt_ref.shape[2]
    for g in range(grp):
        q_sc[g * tq:(g + 1) * tq, :] = q_ref[0, :, g * d:(g + 1) * d]

    def qk(item):
        c, g = item
        return jnp.dot(q_sc[g * tq:(g + 1) * tq, :], kt_ref[0, :, c * tk:(c + 1) * tk],
                       preferred_element_type=F32)

    items = [(c, g) for c in range(s // tk) for g in range(grp)]
    sc_next = qk(items[0])
    for n, (c, g) in enumerate(items):
        rows = slice(g * tq, (g + 1) * tq)
        sc = sc_next
        if n + 1 < len(items):
            sc_next = qk(items[n + 1])
        v = v_ref[0, c * tk:(c + 1) * tk, :]
        row_max = jnp.max(sc, axis=-1, keepdims=True)
        if c == 0:
            m_new = jnp.broadcast_to(row_max, (tq, V7X_LANES))
        else:
            m_prev = m_ref[rows, :]
            m_new = jnp.maximum(m_prev, row_max)
        p = jnp.exp2(sc - pltpu.repeat(m_new, tk // V7X_LANES, axis=1)).astype(BF16)
        pv = jnp.dot(p, v, preferred_element_type=F32)
        if c == 0:
            acc_ref[rows, :] = pv
        else:
            acc_ref[rows, :] = acc_ref[rows, :] * jnp.exp2(m_prev - m_new) + pv
        m_ref[rows, :] = m_new
    for g in range(grp):
        acc = acc_ref[g * tq:(g + 1) * tq, :]
        o_ref[0, :, g * d:(g + 1) * d] = (acc[:, :d] / acc[:, d:d + 1]).astype(o_ref.dtype)


def gqa_attention(q3, kt3, v3, *, tq, tk):
    b, s, nq = q3.shape
    d = HEAD_DIM
    grp = GQA_Q_HEADS // GQA_KV_HEADS
    mrows = grp * tq
    vmem = 2 * (tq * grp * d * 2 + d * s * 2 + s * 128 * 2 + tq * grp * d * 2) + 2 * mrows * 128 * 4 + 3 * mrows * tk * 4
    return pl.pallas_call(
        functools.partial(_gqa_attn_kernel, tk=tk),
        grid=(b, GQA_KV_HEADS, s // tq),
        in_specs=[
            pl.BlockSpec((1, tq, grp * d), lambda bi, kv, i: (bi, i, kv)),
            pl.BlockSpec((1, d, s), lambda bi, kv, i: (bi, kv, 0)),
            pl.BlockSpec((1, s, 2 * d), lambda bi, kv, i: (bi, 0, kv)),
        ],
        out_specs=pl.BlockSpec((1, tq, grp * d), lambda bi, kv, i: (bi, i, kv)),
        out_shape=jax.ShapeDtypeStruct((b, s, nq), BF16),
        scratch_shapes=[pltpu.VMEM((mrows, d), BF16), pltpu.VMEM((mrows, V7X_LANES), F32),
                        pltpu.VMEM((mrows, 2 * d), F32)],
        compiler_params=_params(("parallel", "parallel", "parallel"), vmem),
        name="gqa_attn",
    )(q3, kt3, v3)


def _mlstm_chunk(direction, c, m, q_ref, k_ref, v_ref, r_sc, cm_sc, b_sc, st_ref, out_sc):
    L = MLSTM_CHUNK
    dh = MLSTM_HEAD_DIM
    scale = dh ** -0.5
    off = pl.multiple_of(c * L, L)
    q = q_ref[0, pl.ds(off, L), :]
    k = k_ref[0, pl.ds(off, L), :]
    v = v_ref[0, pl.ds(off, L), :]
    r_row = r_sc[direction, pl.ds(c, 1), :]
    cm_row = cm_sc[direction, pl.ds(c, 1), :]
    b_row = b_sc[direction, pl.ds(c, 1), :]
    rmax = jnp.max(r_row, axis=-1, keepdims=True)
    btot = b_row[:, L - 1:L] if direction == 0 else b_row[:, 0:1]

    def col(x_row):
        return jnp.transpose(jnp.broadcast_to(x_row, (L, L)))

    cmat = jnp.maximum(m, col(cm_row))
    row_i = lax.broadcasted_iota(jnp.int32, (L, L), 0)
    col_i = lax.broadcasted_iota(jnp.int32, (L, L), 1)
    mask = (row_i >= col_i) if direction == 0 else (row_i <= col_i)
    w_intra = jnp.where(mask, jnp.exp(r_row - cmat), 0.0)
    w_inter = jnp.exp(m - cmat)
    den_floor = jnp.exp(-(col(b_row) + cmat))
    c_last = jnp.maximum(m, rmax)
    w_state = jnp.exp(col(r_row) - c_last) * scale
    decay = jnp.exp(m - c_last)

    s_qk = lax.dot_general(q, k, (((1,), (1,)), ((), ())), preferred_element_type=F32) * (w_intra * scale)
    v32 = v.astype(F32)
    v_aug = jnp.concatenate([v, jnp.ones((L, dh), BF16)], axis=-1)
    state = st_ref[...]
    inter = jnp.dot(q, state.astype(BF16), preferred_element_type=F32)
    intra = jnp.dot(s_qk.astype(BF16), v_aug, preferred_element_type=F32)
    h_aug = jnp.concatenate([w_inter, w_inter], axis=-1) * inter + intra
    out_sc[pl.ds(off, L), :] = h_aug[:, :dh] / jnp.maximum(jnp.abs(h_aug[:, dh:]), den_floor)
    kv_w = jnp.concatenate([w_state * v32, w_state], axis=-1).astype(BF16)
    upd = lax.dot_general(k, kv_w, (((0,), (0,)), ((), ())), preferred_element_type=F32)
    st_ref[...] = decay * state + upd
    return btot + c_last


def _mlstm_kernel(bias_ref, q_ref, k_ref, v_ref, o_ref, gate_ref, g_ref, y_ref,
                  r_sc, cm_sc, b_sc, hf_sc, hb_sc, stf_sc, stb_sc):
    L = MLSTM_CHUNK
    head = pl.program_id(1)
    nc = q_ref.shape[1] // L
    lane = lax.broadcasted_iota(jnp.int32, (nc, L), 1)
    shifts = [1 << t for t in range(int(math.log2(L)))]
    for d in range(2):
        i_pre = gate_ref[0, 2 * d, 0] + bias_ref[2 * d, head]
        f_pre = gate_ref[0, 2 * d + 1, 0] + bias_ref[2 * d + 1, head]
        logf = jnp.minimum(f_pre, 0.0) - jnp.log1p(jnp.exp(-jnp.abs(f_pre)))
        bc = logf
        for sh in shifts:
            if d == 0:
                bc = bc + jnp.where(lane >= sh, pltpu.roll(bc, sh, axis=1), 0.0)
            else:
                bc = bc + jnp.where(lane < L - sh, pltpu.roll(bc, L - sh, axis=1), 0.0)
        r = i_pre - bc
        cm = r
        for sh in shifts:
            if d == 0:
                cm = jnp.maximum(cm, jnp.where(lane >= sh, pltpu.roll(cm, sh, axis=1), -jnp.inf))
            else:
                cm = jnp.maximum(cm, jnp.where(lane < L - sh, pltpu.roll(cm, L - sh, axis=1), -jnp.inf))
        r_sc[d] = r
        cm_sc[d] = cm
        b_sc[d] = bc
    stf_sc[...] = jnp.zeros(stf_sc.shape, F32)
    stb_sc[...] = jnp.zeros(stb_sc.shape, F32)

    def body(c, carry):
        m_f, m_b = carry
        m_f = _mlstm_chunk(0, c, m_f, q_ref, k_ref, v_ref, r_sc, cm_sc, b_sc, stf_sc, hf_sc)
        m_b = _mlstm_chunk(1, nc - 1 - c, m_b, q_ref, k_ref, v_ref, r_sc, cm_sc, b_sc, stb_sc, hb_sc)
        return m_f, m_b

    zero = jnp.zeros((1, 1), F32)
    lax.fori_loop(0, nc, body, (zero, zero))
    hsum = hf_sc[...] + hb_sc[...]
    ms = jnp.mean(hsum * hsum, axis=-1, keepdims=True)
    y = hsum * lax.rsqrt(ms + EPS) * g_ref[...]
    y_ref[0] = (jax.nn.sigmoid(o_ref[0].astype(F32)) * y).astype(y_ref.dtype)


def mlstm_branch(p3, gates5, gate_bias, norm_g):
    b, s, _ = p3.shape
    L = MLSTM_CHUNK
    dh = MLSTM_HEAD_DIM
    nc = s // L
    blk = lambda off: pl.BlockSpec((1, s, dh), lambda bi, h, off=off: (bi, 0, off // dh + h))
    vmem = 2 * 5 * s * dh * 2 + 2 * s * dh * 4 + 6 * nc * L * 4 + 2 * dh * 2 * dh * 4 + 4 * s * dh * 4
    return pl.pallas_call(
        _mlstm_kernel,
        grid=(b, MLSTM_HEADS),
        in_specs=[
            pl.BlockSpec(memory_space=pltpu.SMEM),
            blk(OFF_CQ), blk(OFF_CK), blk(OFF_CV), blk(OFF_CO),
            pl.BlockSpec((1, 4, 1, nc, L), lambda bi, h: (bi, 0, h, 0, 0)),
            pl.BlockSpec((1, dh), lambda bi, h: (0, h)),
        ],
        out_specs=pl.BlockSpec((1, s, dh), lambda bi, h: (bi, 0, h)),
        out_shape=jax.ShapeDtypeStruct((b, s, BRANCH_WIDTH), BF16),
        scratch_shapes=[
            pltpu.VMEM((2, nc, L), F32), pltpu.VMEM((2, nc, L), F32), pltpu.VMEM((2, nc, L), F32),
            pltpu.VMEM((s, dh), F32), pltpu.VMEM((s, dh), F32),
            pltpu.VMEM((dh, 2 * dh), F32), pltpu.VMEM((dh, 2 * dh), F32),
        ],
        compiler_params=_params(("parallel", "parallel"), vmem),
        name="mlstm",
    )(gate_bias.astype(F32), p3, p3, p3, p3, gates5, norm_g.reshape(1, BRANCH_WIDTH).astype(F32))


def _rel_bucket(rel):
    half = REL_BUCKETS // 2
    max_exact = half // 2
    ret = jnp.where(rel > 0, half, 0)
    n = jnp.abs(rel)
    nf = jnp.maximum(n, 1).astype(F32)
    large = max_exact + (jnp.log(nf / max_exact) / math.log(REL_MAX_DIST / max_exact) * (half - max_exact)).astype(jnp.int32)
    large = jnp.minimum(large, half - 1)
    return ret + jnp.where(n < max_exact, n, large)


def diff_bias_tiles(rel_bias, t):
    assert t >= REL_MAX_DIST
    nh = rel_bias.shape[1]
    k = jnp.arange(2 * t, dtype=jnp.int32)
    rel = jnp.arange(-2, 3, dtype=jnp.int32)[:, None] * t + jnp.where(k < t, k, k - 2 * t)[None, :]
    onehot = (_rel_bucket(rel)[:, :, None] == jnp.arange(REL_BUCKETS, dtype=jnp.int32)).astype(F32)
    period = jnp.einsum('dkb,bh->hdk', onehot, rel_bias.astype(F32) * LOG2E, precision=lax.Precision.HIGHEST)
    far = period[:, 0::4, 0]
    flat = jnp.broadcast_to(period[:, :, None, :], (nh, 5, t, 2 * t)).reshape(nh, 5, 2 * t * t)
    return flat[:, :, :t * (2 * t - 1)].reshape(nh, 5, t, 2 * t - 1)[:, :, :, :t], far


def _diff_attn_kernel(far_ref, q_ref, kt_ref, v_ref, bias_ref, lam_ref, g_ref, o_ref,
                      q_sc, vaug_sc, m_ref, acc_ref, *, t, lam_init):
    dq = DIFF_QK_DIM
    dv = DIFF_V_DIM
    s = kt_ref.shape[2]
    nt = s // t
    head = pl.program_id(1)
    i = pl.program_id(2)

    @pl.when(i == 0)
    def _():
        vaug_sc[:, :dv] = v_ref[0]
        lane = lax.broadcasted_iota(jnp.int32, (s, dv), 1)
        vaug_sc[:, dv:] = jnp.where(lane == 0, 1.0, 0.0).astype(BF16)

    q_scale = dq ** -0.5 * LOG2E
    q_all = (q_ref[0].astype(F32) * q_scale).astype(BF16)
    q_sc[0] = q_all[:, :dq]
    q_sc[1] = q_all[:, dq:]

    def key_tile(delta):
        j = lax.rem(i + delta, nt)
        return j, pl.multiple_of(j * t, t)

    def qk(item):
        delta, mp = item
        _, off = key_tile(delta)
        return jnp.dot(q_sc[mp], kt_ref[0, mp * dq:(mp + 1) * dq, pl.ds(off, t)], preferred_element_type=F32)

    items = [(delta, mp) for delta in range(nt) for mp in range(2)]
    sc_next = qk(items[0])
    for n, (delta, mp) in enumerate(items):
        sc = sc_next
        if n + 1 < len(items):
            sc_next = qk(items[n + 1])
        j, off = key_tile(delta)
        if delta in (0, 1, nt - 1):
            sc = sc + bias_ref[0, jnp.clip(j - i, -2, 2) + 2]
            shift = None
        else:
            shift = jnp.where(j > i, far_ref[head, 1], far_ref[head, 0])
        row_max = jnp.max(sc, axis=-1, keepdims=True)
        if shift is not None:
            row_max = row_max + shift
        if delta == 0:
            m_new = jnp.broadcast_to(row_max, (t, V7X_LANES))
        else:
            m_prev = m_ref[mp]
            m_new = jnp.maximum(m_prev, row_max)
        m_sub = m_new if shift is None else m_new - shift
        p = jnp.exp2(sc - pltpu.repeat(m_sub, t // V7X_LANES, axis=1)).astype(BF16)
        pv = jnp.dot(p, vaug_sc[pl.ds(off, t), :], preferred_element_type=F32)
        if delta == 0:
            acc_ref[mp] = pv
        else:
            alpha = jnp.exp2(m_prev - m_new)
            acc_ref[mp] = acc_ref[mp] * pltpu.repeat(alpha, 2 * dv // V7X_LANES, axis=1) + pv
        m_ref[mp] = m_new
    lp = lam_ref[...]
    lam = (jnp.exp(jnp.sum(lp[0:1] * lp[1:2], axis=-1, keepdims=True))
           - jnp.exp(jnp.sum(lp[2:3] * lp[3:4], axis=-1, keepdims=True)) + lam_init)
    o0 = acc_ref[0, :, :dv] / acc_ref[0, :, dv:dv + 1]
    o1 = acc_ref[1, :, :dv] / acc_ref[1, :, dv:dv + 1]
    o = o0 - lam * o1
    ms = jnp.mean(o * o, axis=-1, keepdims=True)
    o_ref[0] = (o * lax.rsqrt(ms + EPS) * g_ref[...] * (1.0 - lam_init)).astype(o_ref.dtype)


def diff_attention(p3, kt3, bias_tiles, bias_far, lam_params, sub_g, *, t, layer_number):
    b, s, _ = p3.shape
    dv = DIFF_V_DIM
    assert s // t >= 4, "tiles 2 .. s/t-2 steps away from the query tile must all be beyond REL_MAX_DIST"
    lam_init = 0.8 - 0.6 * math.exp(-0.3 * (layer_number - 1))
    vmem = (2 * (t * 128 * 2 + 128 * s * 2 + s * dv * 2 + 5 * t * t * 4 + t * dv * 2) + s * 2 * dv * 2
            + 2 * t * 128 * 4 * 3 + 6 * t * t * 4)
    return pl.pallas_call(
        functools.partial(_diff_attn_kernel, t=t, lam_init=lam_init),
        grid=(b, DIFF_HEADS, s // t),
        in_specs=[
            pl.BlockSpec(memory_space=pltpu.SMEM),
            pl.BlockSpec((1, t, 128), lambda bi, h, i: (bi, i, OFF_DQ // 128 + h)),
            pl.BlockSpec((1, 128, s), lambda bi, h, i: (bi, h, 0)),
            pl.BlockSpec((1, s, dv), lambda bi, h, i: (bi, 0, OFF_DV // dv + h)),
            pl.BlockSpec((1, 5, t, t), lambda bi, h, i: (h, 0, 0, 0)),
            pl.BlockSpec((4, DIFF_QK_DIM), lambda bi, h, i: (0, 0)),
            pl.BlockSpec((1, dv), lambda bi, h, i: (0, 0)),
        ],
        out_specs=pl.BlockSpec((1, t, dv), lambda bi, h, i: (bi, i, h)),
        out_shape=jax.ShapeDtypeStruct((b, s, BRANCH_WIDTH), BF16),
        scratch_shapes=[pltpu.VMEM((2, t, DIFF_QK_DIM), BF16), pltpu.VMEM((s, 2 * dv), BF16),
                        pltpu.VMEM((2, t, V7X_LANES), F32), pltpu.VMEM((2, t, 2 * dv), F32)],
        compiler_params=_params(("parallel", "parallel", "arbitrary"), vmem),
        name="diff_attn",
    )(bias_far, p3, kt3, p3, bias_tiles, lam_params.astype(F32), sub_g.reshape(1, dv).astype(F32))


def _merge_kernel(ya_ref, yb_ref, yc_ref, yd_ref, gate_ref, wb_ref, wo_ref, x_ref, o_ref):
    dm = x_ref.shape[1]
    merged = None
    for n, y_ref in enumerate((ya_ref, yb_ref, yc_ref, yd_ref)):
        br = jnp.dot(y_ref[...], wb_ref[n], preferred_element_type=F32)
        term = jax.nn.sigmoid(gate_ref[:, n * dm:(n + 1) * dm].astype(F32)) * br
        merged = term if merged is None else merged + term
    o_ref[...] = x_ref[...] + jnp.dot(merged.astype(BF16), wo_ref[...], preferred_element_type=F32)


def merge_branches(ys, p2, w_branch, w_out, x2, *, tm):
    m, dm = x2.shape
    w = BRANCH_WIDTH
    y_spec = pl.BlockSpec((tm, w), lambda i: (i, 0))
    vmem = 2 * (4 * tm * w * 2 + tm * 4 * dm * 2 + 4 * w * dm * 2 + dm * dm * 2 + 2 * tm * dm * 4) + 6 * tm * dm * 4
    return pl.pallas_call(
        _merge_kernel,
        grid=(m // tm,),
        in_specs=[
            y_spec, y_spec, y_spec, y_spec,
            pl.BlockSpec((tm, N_BRANCHES * dm), lambda i: (i, OFF_GATES // (N_BRANCHES * dm))),
            pl.BlockSpec((N_BRANCHES, w, dm), lambda i: (0, 0, 0)),
            pl.BlockSpec((dm, dm), lambda i: (0, 0)),
            pl.BlockSpec((tm, dm), lambda i: (i, 0)),
        ],
        out_specs=pl.BlockSpec((tm, dm), lambda i: (i, 0)),
        out_shape=jax.ShapeDtypeStruct((m, dm), F32),
        compiler_params=_params(("parallel",), vmem),
        name="merge",
    )(*ys, p2, w_branch, w_out, x2)


HALO_ROWS = 16


def _ffn_down_kernel(a_ref, lin_ref, ap_ref, an_ref, cw_ref, cb_ref, wd_ref, x_ref, gf_ref, o_ref,
                     *, tiles_per_seq, final_norm):
    tm = a_ref.shape[0]
    i = pl.program_id(0)
    pos = i % tiles_per_seq
    a = a_ref[...].astype(F32)
    prev_row = jnp.where(pos == 0, 0.0, ap_ref[HALO_ROWS - 1:HALO_ROWS, :].astype(F32))
    next_row = jnp.where(pos == tiles_per_seq - 1, 0.0, an_ref[0:1, :].astype(F32))
    rows = lax.broadcasted_iota(jnp.int32, (tm, 1), 0)
    a_m1 = jnp.where(rows == 0, prev_row, pltpu.roll(a, 1, axis=0))
    a_p1 = jnp.where(rows == tm - 1, next_row, pltpu.roll(a, tm - 1, axis=0))
    cw = cw_ref[...]
    c = a_m1 * cw[0:1] + a * cw[1:2] + a_p1 * cw[2:3] + cb_ref[...]
    gelu = 0.5 * c * (1.0 + jnp.tanh(math.sqrt(2.0 / math.pi) * (c + 0.044715 * (c * c * c))))
    hmid = (gelu * lin_ref[...].astype(F32)).astype(BF16)
    y = x_ref[...] + jnp.dot(hmid, wd_ref[...], preferred_element_type=F32)
    if final_norm:
        ms = jnp.mean(y * y, axis=-1, keepdims=True)
        y = y * lax.rsqrt(ms + EPS) * gf_ref[...]
    o_ref[...] = y


def ffn_down(up2, conv_w, conv_b, w_down, x2, final_g, *, s, tm, final_norm):
    m, dm = x2.shape
    f = w_down.shape[0]
    tiles_per_seq = s // tm
    hb = tm // HALO_ROWS
    n_halo = m // HALO_ROWS
    vmem = 2 * (2 * tm * f * 2 + 2 * HALO_ROWS * f * 2 + f * dm * 2 + 2 * tm * dm * 4) + 8 * tm * f * 4
    return pl.pallas_call(
        functools.partial(_ffn_down_kernel, tiles_per_seq=tiles_per_seq, final_norm=final_norm),
        grid=(m // tm,),
        in_specs=[
            pl.BlockSpec((tm, f), lambda i: (i, 0)),
            pl.BlockSpec((tm, f), lambda i: (i, 1)),
            pl.BlockSpec((HALO_ROWS, f), lambda i: (jnp.maximum(i * hb - 1, 0), 0)),
            pl.BlockSpec((HALO_ROWS, f), lambda i: (jnp.minimum((i + 1) * hb, n_halo - 1), 0)),
            pl.BlockSpec((CONV_W, f), lambda i: (0, 0)),
            pl.BlockSpec((1, f), lambda i: (0, 0)),
            pl.BlockSpec((f, dm), lambda i: (0, 0)),
            pl.BlockSpec((tm, dm), lambda i: (i, 0)),
            pl.BlockSpec((1, dm), lambda i: (0, 0)),
        ],
        out_specs=pl.BlockSpec((tm, dm), lambda i: (i, 0)),
        out_shape=jax.ShapeDtypeStruct((m, dm), F32),
        compiler_params=_params(("parallel",), vmem),
        name="ffn_down",
    )(up2, up2, up2, up2, conv_w.astype(F32), conv_b.reshape(1, f).astype(F32), w_down, x2,
      final_g.reshape(1, dm).astype(F32))


def _arrange_w_in(w):
    widths = (512, 512, 128, 128, 512, 512, 512, 512, 16, 512, 512, 512, 4096)
    offs = [0]
    for wd in widths:
        offs.append(offs[-1] + wd)
    seg = lambda n: w[:, offs[n]:offs[n + 1]]
    a, bq, bk, bv, cq, ck, cv, co, cgate, dq, dk, dv, gates = (seg(n) for n in range(len(widths)))
    main = jnp.concatenate([gates, a, bq, cq, ck, cv, co, dq, dk, dv, bk, bv], axis=1).astype(BF16)
    gate = jnp.pad(cgate, ((0, 0), (0, V7X_LANES - cgate.shape[1]))).astype(BF16)
    return main, gate


def kernel(x, norm_mix_g, w_in, mlstm_gate_bias, qk_norm_g, mlstm_norm_g, diff_lambda, diff_norm_g, rel_bias,
           w_branch, w_out, norm_ffn_g, w_up, conv_w, conv_b, w_down, final_norm_g):
    b, s, dm = x.shape
    depth = w_in.shape[0]
    m = b * s
    d_ff = w_down.shape[1]
    L = MLSTM_CHUNK
    t_diff = 512

    bd, dft = fourier_tables(s)
    cos2, sin2 = rope_tables(s)
    bias_tiles, bias_far = diff_bias_tiles(rel_bias, t_diff)

    x2 = x.reshape(m, dm)
    for layer in range(depth):
        w_main, w_gate = _arrange_w_in(w_in[layer])
        p2, cgate = norm_matmul(x2, norm_mix_g[layer], w_main, w_gate, tm=1024, tn=1280)
        p3 = p2.reshape(b, s, P_WIDTH)

        y_a = fourier_mix(p3, bd, dft, tm=512)

        q_b, k_b, v_b = gqa_prep(p2, qk_norm_g[layer], cos2, sin2, s=s, tm=512)
        kt_b = jnp.swapaxes(k_b.reshape(b, s, GQA_KV_HEADS * HEAD_DIM), 1, 2)
        y_b = gqa_attention(q_b.reshape(b, s, -1), kt_b, v_b.reshape(b, s, -1), tq=256, tk=1024)

        gates5 = jnp.transpose(cgate[:, :4 * MLSTM_HEADS].reshape(b, s, 4, MLSTM_HEADS), (0, 2, 3, 1))
        gates5 = gates5.reshape(b, 4, MLSTM_HEADS, s // L, L)
        y_c = mlstm_branch(p3, gates5, mlstm_gate_bias[layer], mlstm_norm_g[layer])

        kt_d = jnp.swapaxes(p3[:, :, OFF_DK:OFF_DK + BRANCH_WIDTH], 1, 2)
        y_d = diff_attention(p3, kt_d, bias_tiles, bias_far, diff_lambda[layer], diff_norm_g[layer],
                             t=t_diff, layer_number=layer + 1)

        ys = [y.reshape(m, BRANCH_WIDTH) for y in (y_a, y_b, y_c, y_d)]
        x2 = merge_branches(ys, p2, w_branch[layer].astype(BF16), w_out[layer].astype(BF16), x2, tm=512)

        up2 = norm_matmul(x2, norm_ffn_g[layer], w_up[layer].astype(BF16), tm=1024, tn=d_ff // 2)
        x2 = ffn_down(up2, conv_w[layer], conv_b[layer], w_down[layer].astype(BF16), x2, final_norm_g,
                      s=s, tm=256, final_norm=(layer == depth - 1))
    return x2.reshape(b, s, dm)
```

```python
import functools
import math

import jax
import jax.numpy as jnp
from jax import lax
from jax.experimental import pallas as pl
from jax.experimental.pallas import tpu as pltpu

F32 = jnp.float32
BF16 = jnp.bfloat16

GRID_W = 64
HEAD_DIM = 64
BRANCH_WIDTH = 512
N_BRANCHES = 4
FOURIER_GROUP_DIM = 64
GQA_Q_HEADS = 8
GQA_KV_HEADS = 2
MLSTM_HEADS = 4
MLSTM_HEAD_DIM = 128
MLSTM_CHUNK = 128
DIFF_HEADS = 4
DIFF_QK_DIM = 64
DIFF_V_DIM = 128
REL_BUCKETS = 32
REL_MAX_DIST = 128
CONV_W = 3
ROPE_BASE = 10000.0
EPS = 1e-6
LOG2E = math.log2(math.e)

V7X_LANES = 128
V7X_VMEM_BYTES = 64 * 1024 * 1024
V7X_VMEM_CAP = V7X_VMEM_BYTES - 8 * 1024 * 1024

OFF_GATES = 0
OFF_A = 4096
OFF_BQ = 4608
OFF_CQ = 5120
OFF_CK = 5632
OFF_CV = 6144
OFF_CO = 6656
OFF_DQ = 7168
OFF_DK = 7680
OFF_DV = 8192
OFF_BKV = 8704
P_WIDTH = 8960


def _params(sem, vmem_bytes):
    limit = int(min(max(vmem_bytes * 3 // 2 + (4 << 20), 16 << 20), V7X_VMEM_CAP))
    return pltpu.CompilerParams(dimension_semantics=sem, vmem_limit_bytes=limit)


def _norm_mm_kernel(x_ref, g_ref, w_ref, o_ref, xn_ref):
    @pl.when(pl.program_id(1) == 0)
    def _():
        x = x_ref[...]
        ms = jnp.mean(x * x, axis=-1, keepdims=True)
        xn_ref[...] = (x * lax.rsqrt(ms + EPS) * g_ref[...]).astype(BF16)

    o_ref[...] = jnp.dot(xn_ref[...], w_ref[...], preferred_element_type=F32).astype(o_ref.dtype)


def _norm_mm_gate_kernel(x_ref, g_ref, w_ref, wg_ref, o_ref, og_ref, xn_ref):
    @pl.when(pl.program_id(1) == 0)
    def _():
        x = x_ref[...]
        ms = jnp.mean(x * x, axis=-1, keepdims=True)
        xn = (x * lax.rsqrt(ms + EPS) * g_ref[...]).astype(BF16)
        xn_ref[...] = xn
        og_ref[...] = jnp.dot(xn, wg_ref[...], preferred_element_type=F32)

    o_ref[...] = jnp.dot(xn_ref[...], w_ref[...], preferred_element_type=F32).astype(o_ref.dtype)


def norm_matmul(x, g, w, w_gate=None, *, tm, tn):
    m, k = x.shape
    n = w.shape[1]
    grid = (m // tm, n // tn)
    vmem = 2 * tm * k * 4 + tm * k * 2 + 2 * k * tn * 2 + 2 * tm * tn * 2 + 4 * tm * k
    x_spec = pl.BlockSpec((tm, k), lambda i, j: (i, 0))
    g_spec = pl.BlockSpec((1, k), lambda i, j: (0, 0))
    w_spec = pl.BlockSpec((k, tn), lambda i, j: (0, j))
    o_spec = pl.BlockSpec((tm, tn), lambda i, j: (i, j))
    scratch = [pltpu.VMEM((tm, k), BF16)]
    g2 = g.reshape(1, k).astype(F32)
    if w_gate is None:
        return pl.pallas_call(
            _norm_mm_kernel,
            grid=grid,
            in_specs=[x_spec, g_spec, w_spec],
            out_specs=o_spec,
            out_shape=jax.ShapeDtypeStruct((m, n), BF16),
            scratch_shapes=scratch,
            compiler_params=_params(("parallel", "arbitrary"), vmem),
            name="norm_matmul",
        )(x, g2, w)
    ng = w_gate.shape[1]
    return pl.pallas_call(
        _norm_mm_gate_kernel,
        grid=grid,
        in_specs=[x_spec, g_spec, w_spec, pl.BlockSpec((k, ng), lambda i, j: (0, 0))],
        out_specs=[o_spec, pl.BlockSpec((tm, ng), lambda i, j: (i, 0))],
        out_shape=[jax.ShapeDtypeStruct((m, n), BF16), jax.ShapeDtypeStruct((m, ng), F32)],
        scratch_shapes=scratch,
        compiler_params=_params(("parallel", "arbitrary"), vmem),
        name="norm_matmul_gate",
    )(x, g2, w, w_gate)


def _fourier_kernel(a_ref, bd_ref, dft_ref, o_ref, z_ref, *, row_chunk):
    s = a_ref.shape[1]
    w = a_ref.shape[2]

    @pl.when(pl.program_id(1) == 0)
    def _():
        for r in range(0, s, row_chunk):
            a = a_ref[0, r:r + row_chunk, :]
            zc = jnp.dot(a, bd_ref[...], preferred_element_type=F32)
            z_ref[r:r + row_chunk, :] = zc[:, :w].astype(BF16)
            z_ref[s + r:s + r + row_chunk, :] = zc[:, w:].astype(BF16)

    dft = dft_ref[...].reshape(o_ref.shape[1], 2 * s)
    o_ref[0] = jnp.dot(dft, z_ref[...], preferred_element_type=F32).astype(o_ref.dtype)


def fourier_tables(s):
    cg = FOURIER_GROUP_DIM
    jj = jnp.arange(cg, dtype=jnp.int32)
    ang_c = (2.0 * math.pi / cg) * ((jj[:, None] * jj[None, :]) % cg).astype(F32)
    eye_g = jnp.eye(BRANCH_WIDTH // cg, dtype=F32)
    bd_c = jnp.kron(eye_g, jnp.cos(ang_c)) * cg ** -0.5
    bd_s = jnp.kron(eye_g, jnp.sin(ang_c)) * cg ** -0.5
    bd = jnp.concatenate([bd_c, bd_s], axis=1).astype(BF16)
    n_lo = s // cg
    nn = jnp.arange(s, dtype=jnp.int32)
    ang_hi = (2.0 * math.pi / cg) * ((jj[:, None] * nn[None, :]) % cg).astype(F32)
    ll = jnp.arange(n_lo, dtype=jnp.int32)
    ang_lo = (2.0 * math.pi / s) * ((ll[:, None] * nn[None, :]) % s).astype(F32)
    ch, sh = jnp.cos(ang_hi), jnp.sin(ang_hi)
    cl, sl = jnp.cos(ang_lo), jnp.sin(ang_lo)
    scale = s ** -0.5
    a2 = jnp.concatenate([ch, -sh], axis=1)[:, None, :]
    c2 = jnp.concatenate([sh, ch], axis=1)[:, None, :]
    b2 = jnp.concatenate([cl, cl], axis=1)[None, :, :]
    d2 = jnp.concatenate([sl, sl], axis=1)[None, :, :]
    dft = ((a2 * b2 - c2 * d2) * scale).astype(BF16)
    return bd, dft


def fourier_mix(p3, bd, dft, *, tm):
    b, s, _ = p3.shape
    w = BRANCH_WIDTH
    vmem = 2 * s * w * 2 + 2 * tm * 2 * s * 2 + 2 * s * w * 2 + 2 * w * 2 * w * 2 + 2 * tm * w * 2 + 8 * tm * w
    return pl.pallas_call(
        functools.partial(_fourier_kernel, row_chunk=min(s, 512)),
        grid=(b, s // tm),
        in_specs=[
            pl.BlockSpec((1, s, w), lambda bi, i: (bi, 0, OFF_A // w)),
            pl.BlockSpec((w, 2 * w), lambda bi, i: (0, 0)),
            pl.BlockSpec((tm // dft.shape[1], dft.shape[1], 2 * s), lambda bi, i: (i, 0, 0)),
        ],
        out_specs=pl.BlockSpec((1, tm, w), lambda bi, i: (bi, i, 0)),
        out_shape=jax.ShapeDtypeStruct((b, s, w), BF16),
        scratch_shapes=[pltpu.VMEM((2 * s, w), BF16)],
        compiler_params=_params(("parallel", "arbitrary"), vmem),
        name="fourier",
    )(p3, bd, dft)


def rope_tables(s):
    rows = s // GRID_W
    row_id = jnp.repeat(jnp.arange(rows, dtype=F32), GRID_W)
    col_id = jnp.tile(jnp.arange(GRID_W, dtype=F32), rows)
    n_pairs = HEAD_DIM // 4
    inv_freq = ROPE_BASE ** (-jnp.arange(n_pairs, dtype=F32) / n_pairs)
    ang = jnp.concatenate([row_id[:, None] * inv_freq, col_id[:, None] * inv_freq], axis=-1)
    cos, sin = jnp.cos(ang), jnp.sin(ang)
    return jnp.concatenate([cos, cos] * 2, axis=-1), jnp.concatenate([-sin, sin] * 2, axis=-1)


def _norm_rope(x, g, seg, cos2, sin2):
    half = HEAD_DIM // 2
    x2 = x * x
    hi = x2.astype(BF16)
    lo = (x2 - hi.astype(F32)).astype(BF16)
    ms = jnp.dot(hi, seg, preferred_element_type=F32) + jnp.dot(lo, seg, preferred_element_type=F32)
    y = x * lax.rsqrt(ms + EPS) * g
    lane = lax.broadcasted_iota(jnp.int32, (x.shape[0], V7X_LANES), 1)
    first_half = (lane % HEAD_DIM) < half
    outs = []
    for cb in range(x.shape[1] // V7X_LANES):
        yb = y[:, cb * V7X_LANES:(cb + 1) * V7X_LANES]
        rot = jnp.where(first_half, pltpu.roll(yb, V7X_LANES - half, axis=1), pltpu.roll(yb, half, axis=1))
        outs.append(yb * cos2 + rot * sin2)
    return outs[0] if len(outs) == 1 else jnp.concatenate(outs, axis=-1)


def _gqa_prep_kernel(q_ref, kv_ref, gq_ref, gk_ref, seg_ref, cos_ref, sin_ref, qo_ref, ko_ref, vo_ref):
    d = HEAD_DIM
    nk = GQA_KV_HEADS * d
    cos2, sin2 = cos_ref[...], sin_ref[...]
    q = q_ref[...].astype(F32)
    kv = kv_ref[...].astype(F32)
    qo_ref[...] = _norm_rope(q, gq_ref[...], seg_ref[...], cos2, sin2).astype(BF16)
    ko_ref[...] = _norm_rope(kv[:, :nk], gk_ref[...], seg_ref[:nk, :nk], cos2, sin2).astype(BF16)
    v = kv[:, nk:]
    lane = lax.broadcasted_iota(jnp.int32, v.shape, 1)
    ones_col = jnp.where(lane == d, 1.0, 0.0)
    vo_ref[:, :nk] = jnp.where(lane < d, v, ones_col).astype(BF16)
    vo_ref[:, nk:] = jnp.where(lane < d, pltpu.roll(v, d, axis=1), ones_col).astype(BF16)


def gqa_prep(p2, qk_g, cos2, sin2, *, s, tm):
    m = p2.shape[0]
    nq = GQA_Q_HEADS * HEAD_DIM
    nkv = GQA_KV_HEADS * HEAD_DIM
    assert nkv == V7X_LANES
    tiles_per_seq = s // tm
    q_scale = HEAD_DIM ** -0.5 * LOG2E
    gq = jnp.tile(qk_g[0].astype(F32) * q_scale, GQA_Q_HEADS).reshape(1, nq)
    gk = jnp.tile(qk_g[1].astype(F32), GQA_KV_HEADS).reshape(1, nkv)
    seg = jnp.kron(jnp.eye(GQA_Q_HEADS, dtype=F32), jnp.full((HEAD_DIM, HEAD_DIM), 1.0 / HEAD_DIM, F32)).astype(BF16)
    return pl.pallas_call(
        _gqa_prep_kernel,
        grid=(m // tm,),
        in_specs=[
            pl.BlockSpec((tm, nq), lambda i: (i, OFF_BQ // nq)),
            pl.BlockSpec((tm, 2 * nkv), lambda i: (i, OFF_BKV // (2 * nkv))),
            pl.BlockSpec((1, nq), lambda i: (0, 0)),
            pl.BlockSpec((1, nkv), lambda i: (0, 0)),
            pl.BlockSpec((nq, nq), lambda i: (0, 0)),
            pl.BlockSpec((tm, V7X_LANES), lambda i: (i % tiles_per_seq, 0)),
            pl.BlockSpec((tm, V7X_LANES), lambda i: (i % tiles_per_seq, 0)),
        ],
        out_specs=[
            pl.BlockSpec((tm, nq), lambda i: (i, 0)),
            pl.BlockSpec((tm, nkv), lambda i: (i, 0)),
            pl.BlockSpec((tm, 2 * nkv), lambda i: (i, 0)),
        ],
        out_shape=[
            jax.ShapeDtypeStruct((m, nq), BF16),
            jax.ShapeDtypeStruct((m, nkv), BF16),
            jax.ShapeDtypeStruct((m, 2 * nkv), BF16),
        ],
        compiler_params=_params(("parallel",), 16 * tm * nq * 4),
        name="gqa_prep",
    )(p2, p2, gq, gk, seg, cos2, sin2)


def _gqa_attn_kernel(q_ref, kt_ref, v_ref, o_ref, q_sc, m_ref, acc_ref, *, tk):
    d = HEAD_DIM
    tq = q_ref.shape[1]
    grp = q_ref.shape[2] // d
    s = kt_ref.shape[2]
    for g in range(grp):
        q_sc[g * tq:(g + 1) * tq, :] = q_ref[0, :, g * d:(g + 1) * d]

    def qk(item):
        c, g = item
        return jnp.dot(q_sc[g * tq:(g + 1) * tq, :], kt_ref[0, :, c * tk:(c + 1) * tk],
                       preferred_element_type=F32)

    items = [(c, g) for c in range(s // tk) for g in range(grp)]
    sc_next = qk(items[0])
    for n, (c, g) in enumerate(items):
        rows = slice(g * tq, (g + 1) * tq)
        sc = sc_next
        if n + 1 < len(items):
            sc_next = qk(items[n + 1])
        v = v_ref[0, c * tk:(c + 1) * tk, :]
        row_max = jnp.max(sc, axis=-1, keepdims=True)
        if c == 0:
            m_new = jnp.broadcast_to(row_max, (tq, V7X_LANES))
        else:
            m_prev = m_ref[rows, :]
            m_new = jnp.maximum(m_prev, row_max)
        p = jnp.exp2(sc - pltpu.repeat(m_new, tk // V7X_LANES, axis=1)).astype(BF16)
        pv = jnp.dot(p, v, preferred_element_type=F32)
        if c == 0:
            acc_ref[rows, :] = pv
        else:
            acc_ref[rows, :] = acc_ref[rows, :] * jnp.exp2(m_prev - m_new) + pv
        m_ref[rows, :] = m_new
    for g in range(grp):
        acc = acc_ref[g * tq:(g + 1) * tq, :]
        o_ref[0, :, g * d:(g + 1) * d] = (acc[:, :d] / acc[:, d:d + 1]).astype(o_ref.dtype)


def gqa_attention(q3, kt3, v3, *, tq, tk):
    b, s, nq = q3.shape
    d = HEAD_DIM
    grp = GQA_Q_HEADS // GQA_KV_HEADS
    mrows = grp * tq
    vmem = 2 * (tq * grp * d * 2 + d * s * 2 + s * 128 * 2 + tq * grp * d * 2) + 2 * mrows * 128 * 4 + 3 * mrows * tk * 4
    return pl.pallas_call(
        functools.partial(_gqa_attn_kernel, tk=tk),
        grid=(b, GQA_KV_HEADS, s // tq),
        in_specs=[
            pl.BlockSpec((1, tq, grp * d), lambda bi, kv, i: (bi, i, kv)),
            pl.BlockSpec((1, d, s), lambda bi, kv, i: (bi, kv, 0)),
            pl.BlockSpec((1, s, 2 * d), lambda bi, kv, i: (bi, 0, kv)),
        ],
        out_specs=pl.BlockSpec((1, tq, grp * d), lambda bi, kv, i: (bi, i, kv)),
        out_shape=jax.ShapeDtypeStruct((b, s, nq), BF16),
        scratch_shapes=[pltpu.VMEM((mrows, d), BF16), pltpu.VMEM((mrows, V7X_LANES), F32),
                        pltpu.VMEM((mrows, 2 * d), F32)],
        compiler_params=_params(("parallel", "parallel", "parallel"), vmem),
        name="gqa_attn",
    )(q3, kt3, v3)


def _mlstm_step(chains, ms, q_ref, k_ref, v_ref, r_sc, cm_sc, b_sc, st_sc, h_sc):
    L = MLSTM_CHUNK
    dh = MLSTM_HEAD_DIM
    assert L == dh
    scale = dh ** -0.5
    row_i = lax.broadcasted_iota(jnp.int32, (L, L), 0)
    col_i = lax.broadcasted_iota(jnp.int32, (L, L), 1)

    def col(x_row):
        return jnp.transpose(jnp.broadcast_to(x_row, (L, L)))

    pre = []
    for (hh, direction, c), m in zip(chains, ms):
        off = pl.multiple_of(c * L, L)
        lanes = slice(hh * dh, (hh + 1) * dh)
        q = q_ref[0, pl.ds(off, L), lanes]
        k = k_ref[0, pl.ds(off, L), lanes]
        v = v_ref[0, pl.ds(off, L), lanes]
        r_row = r_sc[hh, direction, pl.ds(c, 1), :]
        cm_row = cm_sc[hh, direction, pl.ds(c, 1), :]
        b_row = b_sc[hh, direction, pl.ds(c, 1), :]
        rmax = jnp.max(r_row, axis=-1, keepdims=True)
        btot = b_row[:, L - 1:L] if direction == 0 else b_row[:, 0:1]
        cmat = jnp.maximum(m, col(cm_row))
        mask = (row_i >= col_i) if direction == 0 else (row_i <= col_i)
        c_last = jnp.maximum(m, rmax)
        w_state = jnp.exp(col(r_row) - c_last) * scale
        pre.append(dict(
            off=off, lanes=lanes, q=q, k=k, v=v,
            w_intra=jnp.where(mask, jnp.exp(r_row - cmat), 0.0) * scale,
            w_inter=jnp.exp(m - cmat),
            den_floor=jnp.exp(-(col(b_row) + cmat)),
            decay=jnp.exp(m - c_last),
            kv_w=jnp.concatenate([w_state * v.astype(F32), w_state], axis=-1).astype(BF16),
            m_new=btot + c_last,
        ))
    s_raw = [lax.dot_general(p["q"], p["k"], (((1,), (1,)), ((), ())), preferred_element_type=F32) for p in pre]
    states = [st_sc[hh, direction] for hh, direction, _ in chains]
    inter = [jnp.dot(p["q"], st.astype(BF16), preferred_element_type=F32) for p, st in zip(pre, states)]
    upd = [lax.dot_general(p["k"], p["kv_w"], (((0,), (0,)), ((), ())), preferred_element_type=F32) for p in pre]
    for n, ((hh, direction, _), p) in enumerate(zip(chains, pre)):
        st_sc[hh, direction] = p["decay"] * states[n] + upd[n]
        v_aug = jnp.concatenate([p["v"], jnp.ones((L, dh), BF16)], axis=-1)
        intra = jnp.dot((s_raw[n] * p["w_intra"]).astype(BF16), v_aug, preferred_element_type=F32)
        h_aug = jnp.concatenate([p["w_inter"], p["w_inter"]], axis=-1) * inter[n] + intra
        h_sc[pl.ds(p["off"], L), p["lanes"]] += h_aug[:, :dh] / jnp.maximum(jnp.abs(h_aug[:, dh:]), p["den_floor"])
    return [p["m_new"] for p in pre]


def _mlstm_kernel(bias_ref, q_ref, k_ref, v_ref, o_ref, gate_ref, g_ref, y_ref,
                  r_sc, cm_sc, b_sc, h_sc, st_sc):
    L = MLSTM_CHUNK
    dh = MLSTM_HEAD_DIM
    hpb = q_ref.shape[2] // dh
    head0 = pl.program_id(1) * hpb
    nc = q_ref.shape[1] // L
    lane = lax.broadcasted_iota(jnp.int32, (nc, L), 1)
    shifts = [1 << t for t in range(int(math.log2(L)))]
    for hh in range(hpb):
        for d in range(2):
            i_pre = gate_ref[0, 2 * d, hh] + bias_ref[2 * d, head0 + hh]
            f_pre = gate_ref[0, 2 * d + 1, hh] + bias_ref[2 * d + 1, head0 + hh]
            logf = jnp.minimum(f_pre, 0.0) - jnp.log1p(jnp.exp(-jnp.abs(f_pre)))
            bc = logf
            for sh in shifts:
                if d == 0:
                    bc = bc + jnp.where(lane >= sh, pltpu.roll(bc, sh, axis=1), 0.0)
                else:
                    bc = bc + jnp.where(lane < L - sh, pltpu.roll(bc, L - sh, axis=1), 0.0)
            r = i_pre - bc
            cm = r
            for sh in shifts:
                if d == 0:
                    cm = jnp.maximum(cm, jnp.where(lane >= sh, pltpu.roll(cm, sh, axis=1), -jnp.inf))
                else:
                    cm = jnp.maximum(cm, jnp.where(lane < L - sh, pltpu.roll(cm, L - sh, axis=1), -jnp.inf))
            r_sc[hh, d] = r
            cm_sc[hh, d] = cm
            b_sc[hh, d] = bc
    st_sc[...] = jnp.zeros(st_sc.shape, F32)
    h_sc[...] = jnp.zeros(h_sc.shape, F32)

    def body(c, ms):
        chains = [(hh, d, c if d == 0 else nc - 1 - c) for hh in range(hpb) for d in range(2)]
        return tuple(_mlstm_step(chains, ms, q_ref, k_ref, v_ref, r_sc, cm_sc, b_sc, st_sc, h_sc))

    lax.fori_loop(0, nc, body, tuple(jnp.zeros((1, 1), F32) for _ in range(2 * hpb)))
    for hh in range(hpb):
        lanes = slice(hh * dh, (hh + 1) * dh)
        hsum = h_sc[:, lanes]
        ms = jnp.mean(hsum * hsum, axis=-1, keepdims=True)
        y = hsum * lax.rsqrt(ms + EPS) * g_ref[:, lanes]
        y_ref[0, :, lanes] = (jax.nn.sigmoid(o_ref[0, :, lanes].astype(F32)) * y).astype(y_ref.dtype)


def mlstm_branch(p3, gates5, gate_bias, norm_g, *, heads_per_block):
    b, s, _ = p3.shape
    L = MLSTM_CHUNK
    hpb = heads_per_block
    wb = hpb * MLSTM_HEAD_DIM
    nc = s // L
    blk = lambda off: pl.BlockSpec((1, s, wb), lambda bi, h, off=off: (bi, 0, off // wb + h))
    vmem = 2 * 5 * s * wb * 2 + s * wb * 4 + 6 * hpb * nc * L * 4 + 4 * hpb * wb * wb * 4 + 3 * s * wb * 4
    return pl.pallas_call(
        _mlstm_kernel,
        grid=(b, MLSTM_HEADS // hpb),
        in_specs=[
            pl.BlockSpec(memory_space=pltpu.SMEM),
            blk(OFF_CQ), blk(OFF_CK), blk(OFF_CV), blk(OFF_CO),
            pl.BlockSpec((1, 4, hpb, nc, L), lambda bi, h: (bi, 0, h, 0, 0)),
            pl.BlockSpec((1, wb), lambda bi, h: (0, h)),
        ],
        out_specs=pl.BlockSpec((1, s, wb), lambda bi, h: (bi, 0, h)),
        out_shape=jax.ShapeDtypeStruct((b, s, BRANCH_WIDTH), BF16),
        scratch_shapes=[
            pltpu.VMEM((hpb, 2, nc, L), F32), pltpu.VMEM((hpb, 2, nc, L), F32), pltpu.VMEM((hpb, 2, nc, L), F32),
            pltpu.VMEM((s, wb), F32),
            pltpu.VMEM((hpb, 2, MLSTM_HEAD_DIM, 2 * MLSTM_HEAD_DIM), F32),
        ],
        compiler_params=_params(("parallel", "parallel"), vmem),
        name="mlstm",
    )(gate_bias.astype(F32), p3, p3, p3, p3, gates5, norm_g.reshape(1, BRANCH_WIDTH).astype(F32))


def _rel_bucket(rel):
    half = REL_BUCKETS // 2
    max_exact = half // 2
    ret = jnp.where(rel > 0, half, 0)
    n = jnp.abs(rel)
    nf = jnp.maximum(n, 1).astype(F32)
    large = max_exact + (jnp.log(nf / max_exact) / math.log(REL_MAX_DIST / max_exact) * (half - max_exact)).astype(jnp.int32)
    large = jnp.minimum(large, half - 1)
    return ret + jnp.where(n < max_exact, n, large)


def diff_bias_tiles(rel_bias, t):
    assert t >= REL_MAX_DIST
    nh = rel_bias.shape[1]
    k = jnp.arange(2 * t, dtype=jnp.int32)
    rel = jnp.arange(-2, 3, dtype=jnp.int32)[:, None] * t + jnp.where(k < t, k, k - 2 * t)[None, :]
    onehot = (_rel_bucket(rel)[:, :, None] == jnp.arange(REL_BUCKETS, dtype=jnp.int32)).astype(F32)
    period = jnp.einsum('dkb,bh->hdk', onehot, rel_bias.astype(F32) * LOG2E, precision=lax.Precision.HIGHEST)
    far = period[:, 0::4, 0]
    flat = jnp.broadcast_to(period[:, :, None, :], (nh, 5, t, 2 * t)).reshape(nh, 5, 2 * t * t)
    return flat[:, :, :t * (2 * t - 1)].reshape(nh, 5, t, 2 * t - 1)[:, :, :, :t], far


def _diff_attn_kernel(far_ref, q_ref, kt_ref, v_ref, bias_ref, lam_ref, g_ref, o_ref,
                      q_sc, vaug_sc, m_ref, acc_ref, *, t, lam_init):
    dq = DIFF_QK_DIM
    dv = DIFF_V_DIM
    s = kt_ref.shape[2]
    nt = s // t
    head = pl.program_id(1)
    i = pl.program_id(2)

    @pl.when(i == 0)
    def _():
        vaug_sc[:, :dv] = v_ref[0]
        lane = lax.broadcasted_iota(jnp.int32, (s, dv), 1)
        vaug_sc[:, dv:] = jnp.where(lane == 0, 1.0, 0.0).astype(BF16)

    q_scale = dq ** -0.5 * LOG2E
    q_all = (q_ref[0].astype(F32) * q_scale).astype(BF16)
    q_sc[0] = q_all[:, :dq]
    q_sc[1] = q_all[:, dq:]

    def key_tile(delta):
        j = lax.rem(i + delta, nt)
        return j, pl.multiple_of(j * t, t)

    def qk(item):
        delta, mp = item
        _, off = key_tile(delta)
        return jnp.dot(q_sc[mp], kt_ref[0, mp * dq:(mp + 1) * dq, pl.ds(off, t)], preferred_element_type=F32)

    items = [(delta, mp) for delta in range(nt) for mp in range(2)]
    sc_next = qk(items[0])
    for n, (delta, mp) in enumerate(items):
        sc = sc_next
        if n + 1 < len(items):
            sc_next = qk(items[n + 1])
        j, off = key_tile(delta)
        if delta in (0, 1, nt - 1):
            sc = sc + bias_ref[0, jnp.clip(j - i, -2, 2) + 2]
            shift = None
        else:
            shift = jnp.where(j > i, far_ref[head, 1], far_ref[head, 0])
        row_max = jnp.max(sc, axis=-1, keepdims=True)
        if shift is not None:
            row_max = row_max + shift
        if delta == 0:
            m_new = jnp.broadcast_to(row_max, (t, V7X_LANES))
        else:
            m_prev = m_ref[mp]
            m_new = jnp.maximum(m_prev, row_max)
        m_sub = m_new if shift is None else m_new - shift
        p = jnp.exp2(sc - pltpu.repeat(m_sub, t // V7X_LANES, axis=1)).astype(BF16)
        pv = jnp.dot(p, vaug_sc[pl.ds(off, t), :], preferred_element_type=F32)
        if delta == 0:
            acc_ref[mp] = pv
        else:
            alpha = jnp.exp2(m_prev - m_new)
            acc_ref[mp] = acc_ref[mp] * pltpu.repeat(alpha, 2 * dv // V7X_LANES, axis=1) + pv
        m_ref[mp] = m_new
    lp = lam_ref[...]
    lam = (jnp.exp(jnp.sum(lp[0:1] * lp[1:2], axis=-1, keepdims=True))
           - jnp.exp(jnp.sum(lp[2:3] * lp[3:4], axis=-1, keepdims=True)) + lam_init)
    o0 = acc_ref[0, :, :dv] / acc_ref[0, :, dv:dv + 1]
    o1 = acc_ref[1, :, :dv] / acc_ref[1, :, dv:dv + 1]
    o = o0 - lam * o1
    ms = jnp.mean(o * o, axis=-1, keepdims=True)
    o_ref[0] = (o * lax.rsqrt(ms + EPS) * g_ref[...] * (1.0 - lam_init)).astype(o_ref.dtype)


def diff_attention(p3, kt3, bias_tiles, bias_far, lam_params, sub_g, *, t, layer_number):
    b, s, _ = p3.shape
    dv = DIFF_V_DIM
    assert s // t >= 4, "tiles 2 .. s/t-2 steps away from the query tile must all be beyond REL_MAX_DIST"
    lam_init = 0.8 - 0.6 * math.exp(-0.3 * (layer_number - 1))
    vmem = (2 * (t * 128 * 2 + 128 * s * 2 + s * dv * 2 + 5 * t * t * 4 + t * dv * 2) + s * 2 * dv * 2
            + 2 * t * 128 * 4 * 3 + 6 * t * t * 4)
    return pl.pallas_call(
        functools.partial(_diff_attn_kernel, t=t, lam_init=lam_init),
        grid=(b, DIFF_HEADS, s // t),
        in_specs=[
            pl.BlockSpec(memory_space=pltpu.SMEM),
            pl.BlockSpec((1, t, 128), lambda bi, h, i: (bi, i, OFF_DQ // 128 + h)),
            pl.BlockSpec((1, 128, s), lambda bi, h, i: (bi, h, 0)),
            pl.BlockSpec((1, s, dv), lambda bi, h, i: (bi, 0, OFF_DV // dv + h)),
            pl.BlockSpec((1, 5, t, t), lambda bi, h, i: (h, 0, 0, 0)),
            pl.BlockSpec((4, DIFF_QK_DIM), lambda bi, h, i: (0, 0)),
            pl.BlockSpec((1, dv), lambda bi, h, i: (0, 0)),
        ],
        out_specs=pl.BlockSpec((1, t, dv), lambda bi, h, i: (bi, i, h)),
        out_shape=jax.ShapeDtypeStruct((b, s, BRANCH_WIDTH), BF16),
        scratch_shapes=[pltpu.VMEM((2, t, DIFF_QK_DIM), BF16), pltpu.VMEM((s, 2 * dv), BF16),
                        pltpu.VMEM((2, t, V7X_LANES), F32), pltpu.VMEM((2, t, 2 * dv), F32)],
        compiler_params=_params(("parallel", "parallel", "arbitrary"), vmem),
        name="diff_attn",
    )(bias_far, p3, kt3, p3, bias_tiles, lam_params.astype(F32), sub_g.reshape(1, dv).astype(F32))


def _merge_kernel(ya_ref, yb_ref, yc_ref, yd_ref, gate_ref, wb_ref, wo_ref, x_ref, o_ref):
    dm = x_ref.shape[1]
    merged = None
    for n, y_ref in enumerate((ya_ref, yb_ref, yc_ref, yd_ref)):
        br = jnp.dot(y_ref[...], wb_ref[n], preferred_element_type=F32)
        term = jax.nn.sigmoid(gate_ref[:, n * dm:(n + 1) * dm].astype(F32)) * br
        merged = term if merged is None else merged + term
    o_ref[...] = x_ref[...] + jnp.dot(merged.astype(BF16), wo_ref[...], preferred_element_type=F32)


def merge_branches(ys, p2, w_branch, w_out, x2, *, tm):
    m, dm = x2.shape
    w = BRANCH_WIDTH
    y_spec = pl.BlockSpec((tm, w), lambda i: (i, 0))
    vmem = 2 * (4 * tm * w * 2 + tm * 4 * dm * 2 + 4 * w * dm * 2 + dm * dm * 2 + 2 * tm * dm * 4) + 6 * tm * dm * 4
    return pl.pallas_call(
        _merge_kernel,
        grid=(m // tm,),
        in_specs=[
            y_spec, y_spec, y_spec, y_spec,
            pl.BlockSpec((tm, N_BRANCHES * dm), lambda i: (i, OFF_GATES // (N_BRANCHES * dm))),
            pl.BlockSpec((N_BRANCHES, w, dm), lambda i: (0, 0, 0)),
            pl.BlockSpec((dm, dm), lambda i: (0, 0)),
            pl.BlockSpec((tm, dm), lambda i: (i, 0)),
        ],
        out_specs=pl.BlockSpec((tm, dm), lambda i: (i, 0)),
        out_shape=jax.ShapeDtypeStruct((m, dm), F32),
        compiler_params=_params(("parallel",), vmem),
        name="merge",
    )(*ys, p2, w_branch, w_out, x2)


HALO_ROWS = 16


def _ffn_down_kernel(a_ref, lin_ref, ap_ref, an_ref, cw_ref, cb_ref, wd_ref, x_ref, gf_ref, o_ref,
                     *, tiles_per_seq, final_norm):
    tm = a_ref.shape[0]
    i = pl.program_id(0)
    pos = i % tiles_per_seq
    a = a_ref[...].astype(F32)
    prev_row = jnp.where(pos == 0, 0.0, ap_ref[HALO_ROWS - 1:HALO_ROWS, :].astype(F32))
    next_row = jnp.where(pos == tiles_per_seq - 1, 0.0, an_ref[0:1, :].astype(F32))
    rows = lax.broadcasted_iota(jnp.int32, (tm, 1), 0)
    a_m1 = jnp.where(rows == 0, prev_row, pltpu.roll(a, 1, axis=0))
    a_p1 = jnp.where(rows == tm - 1, next_row, pltpu.roll(a, tm - 1, axis=0))
    cw = cw_ref[...]
    c = a_m1 * cw[0:1] + a * cw[1:2] + a_p1 * cw[2:3] + cb_ref[...]
    gelu = 0.5 * c * (1.0 + jnp.tanh(math.sqrt(2.0 / math.pi) * (c + 0.044715 * (c * c * c))))
    hmid = (gelu * lin_ref[...].astype(F32)).astype(BF16)
    y = x_ref[...] + jnp.dot(hmid, wd_ref[...], preferred_element_type=F32)
    if final_norm:
        ms = jnp.mean(y * y, axis=-1, keepdims=True)
        y = y * lax.rsqrt(ms + EPS) * gf_ref[...]
    o_ref[...] = y


def ffn_down(up2, conv_w, conv_b, w_down, x2, final_g, *, s, tm, final_norm):
    m, dm = x2.shape
    f = w_down.shape[0]
    tiles_per_seq = s // tm
    hb = tm // HALO_ROWS
    n_halo = m // HALO_ROWS
    vmem = 2 * (2 * tm * f * 2 + 2 * HALO_ROWS * f * 2 + f * dm * 2 + 2 * tm * dm * 4) + 8 * tm * f * 4
    return pl.pallas_call(
        functools.partial(_ffn_down_kernel, tiles_per_seq=tiles_per_seq, final_norm=final_norm),
        grid=(m // tm,),
        in_specs=[
            pl.BlockSpec((tm, f), lambda i: (i, 0)),
            pl.BlockSpec((tm, f), lambda i: (i, 1)),
            pl.BlockSpec((HALO_ROWS, f), lambda i: (jnp.maximum(i * hb - 1, 0), 0)),
            pl.BlockSpec((HALO_ROWS, f), lambda i: (jnp.minimum((i + 1) * hb, n_halo - 1), 0)),
            pl.BlockSpec((CONV_W, f), lambda i: (0, 0)),
            pl.BlockSpec((1, f), lambda i: (0, 0)),
            pl.BlockSpec((f, dm), lambda i: (0, 0)),
            pl.BlockSpec((tm, dm), lambda i: (i, 0)),
            pl.BlockSpec((1, dm), lambda i: (0, 0)),
        ],
        out_specs=pl.BlockSpec((tm, dm), lambda i: (i, 0)),
        out_shape=jax.ShapeDtypeStruct((m, dm), F32),
        compiler_params=_params(("parallel",), vmem),
        name="ffn_down",
    )(up2, up2, up2, up2, conv_w.astype(F32), conv_b.reshape(1, f).astype(F32), w_down, x2,
      final_g.reshape(1, dm).astype(F32))


def _arrange_w_in(w):
    widths = (512, 512, 128, 128, 512, 512, 512, 512, 16, 512, 512, 512, 4096)
    offs = [0]
    for wd in widths:
        offs.append(offs[-1] + wd)
    seg = lambda n: w[:, offs[n]:offs[n + 1]]
    a, bq, bk, bv, cq, ck, cv, co, cgate, dq, dk, dv, gates = (seg(n) for n in range(len(widths)))
    main = jnp.concatenate([gates, a, bq, cq, ck, cv, co, dq, dk, dv, bk, bv], axis=1).astype(BF16)
    gate = jnp.pad(cgate, ((0, 0), (0, V7X_LANES - cgate.shape[1]))).astype(BF16)
    return main, gate


def kernel(x, norm_mix_g, w_in, mlstm_gate_bias, qk_norm_g, mlstm_norm_g, diff_lambda, diff_norm_g, rel_bias,
           w_branch, w_out, norm_ffn_g, w_up, conv_w, conv_b, w_down, final_norm_g):
    b, s, dm = x.shape
    depth = w_in.shape[0]
    m = b * s
    d_ff = w_down.shape[1]
    L = MLSTM_CHUNK
    t_diff = 512

    bd, dft = fourier_tables(s)
    cos2, sin2 = rope_tables(s)
    bias_tiles, bias_far = diff_bias_tiles(rel_bias, t_diff)

    x2 = x.reshape(m, dm)
    for layer in range(depth):
        w_main, w_gate = _arrange_w_in(w_in[layer])
        p2, cgate = norm_matmul(x2, norm_mix_g[layer], w_main, w_gate, tm=1024, tn=1280)
        p3 = p2.reshape(b, s, P_WIDTH)

        y_a = fourier_mix(p3, bd, dft, tm=512)

        q_b, k_b, v_b = gqa_prep(p2, qk_norm_g[layer], cos2, sin2, s=s, tm=512)
        kt_b = jnp.swapaxes(k_b.reshape(b, s, GQA_KV_HEADS * HEAD_DIM), 1, 2)
        y_b = gqa_attention(q_b.reshape(b, s, -1), kt_b, v_b.reshape(b, s, -1), tq=256, tk=1024)

        gates5 = jnp.transpose(cgate[:, :4 * MLSTM_HEADS].reshape(b, s, 4, MLSTM_HEADS), (0, 2, 3, 1))
        gates5 = gates5.reshape(b, 4, MLSTM_HEADS, s // L, L)
        y_c = mlstm_branch(p3, gates5, mlstm_gate_bias[layer], mlstm_norm_g[layer], heads_per_block=2)

        kt_d = jnp.swapaxes(p3[:, :, OFF_DK:OFF_DK + BRANCH_WIDTH], 1, 2)
        y_d = diff_attention(p3, kt_d, bias_tiles, bias_far, diff_lambda[layer], diff_norm_g[layer],
                             t=t_diff, layer_number=layer + 1)

        ys = [y.reshape(m, BRANCH_WIDTH) for y in (y_a, y_b, y_c, y_d)]
        x2 = merge_branches(ys, p2, w_branch[layer].astype(BF16), w_out[layer].astype(BF16), x2, tm=512)

        up2 = norm_matmul(x2, norm_ffn_g[layer], w_up[layer].astype(BF16), tm=1024, tn=d_ff // 2)
        x2 = ffn_down(up2, conv_w[layer], conv_b[layer], w_down[layer].astype(BF16), x2, final_norm_g,
                      s=s, tm=256, final_norm=(layer == depth - 1))
    return x2.reshape(b, s, dm)
```

```python
import functools
import math

import jax
import jax.numpy as jnp
from jax import lax
from jax.experimental import pallas as pl
from jax.experimental.pallas import tpu as pltpu

F32 = jnp.float32
BF16 = jnp.bfloat16

GRID_W = 64
HEAD_DIM = 64
BRANCH_WIDTH = 512
N_BRANCHES = 4
FOURIER_GROUP_DIM = 64
GQA_Q_HEADS = 8
GQA_KV_HEADS = 2
MLSTM_HEADS = 4
MLSTM_HEAD_DIM = 128
MLSTM_CHUNK = 128
DIFF_HEADS = 4
DIFF_QK_DIM = 64
DIFF_V_DIM = 128
REL_BUCKETS = 32
REL_MAX_DIST = 128
CONV_W = 3
ROPE_BASE = 10000.0
EPS = 1e-6
LOG2E = math.log2(math.e)

V7X_LANES = 128
V7X_VMEM_BYTES = 64 * 1024 * 1024
V7X_VMEM_CAP = V7X_VMEM_BYTES - 8 * 1024 * 1024

OFF_GATES = 0
OFF_A = 4096
OFF_BQ = 4608
OFF_CQ = 5120
OFF_CK = 5632
OFF_CV = 6144
OFF_CO = 6656
OFF_DQ = 7168
OFF_DK = 7680
OFF_DV = 8192
OFF_BKV = 8704
P_WIDTH = 8960


def _params(sem, vmem_bytes):
    limit = int(min(max(vmem_bytes * 3 // 2 + (4 << 20), 16 << 20), V7X_VMEM_CAP))
    return pltpu.CompilerParams(dimension_semantics=sem, vmem_limit_bytes=limit)


def _norm_mm_kernel(x_ref, g_ref, w_ref, o_ref, xn_ref):
    @pl.when(pl.program_id(1) == 0)
    def _():
        x = x_ref[...]
        ms = jnp.mean(x * x, axis=-1, keepdims=True)
        xn_ref[...] = (x * lax.rsqrt(ms + EPS) * g_ref[...]).astype(BF16)

    o_ref[...] = jnp.dot(xn_ref[...], w_ref[...], preferred_element_type=F32).astype(o_ref.dtype)


def _norm_mm_gate_kernel(x_ref, g_ref, w_ref, wg_ref, o_ref, og_ref, xn_ref):
    @pl.when(pl.program_id(1) == 0)
    def _():
        x = x_ref[...]
        ms = jnp.mean(x * x, axis=-1, keepdims=True)
        xn = (x * lax.rsqrt(ms + EPS) * g_ref[...]).astype(BF16)
        xn_ref[...] = xn
        og_ref[...] = jnp.dot(xn, wg_ref[...], preferred_element_type=F32)

    o_ref[...] = jnp.dot(xn_ref[...], w_ref[...], preferred_element_type=F32).astype(o_ref.dtype)


def norm_matmul(x, g, w, w_gate=None, *, tm, tn):
    m, k = x.shape
    n = w.shape[1]
    grid = (m // tm, n // tn)
    vmem = 2 * tm * k * 4 + tm * k * 2 + 2 * k * tn * 2 + 2 * tm * tn * 2 + 4 * tm * k
    x_spec = pl.BlockSpec((tm, k), lambda i, j: (i, 0))
    g_spec = pl.BlockSpec((1, k), lambda i, j: (0, 0))
    w_spec = pl.BlockSpec((k, tn), lambda i, j: (0, j))
    o_spec = pl.BlockSpec((tm, tn), lambda i, j: (i, j))
    scratch = [pltpu.VMEM((tm, k), BF16)]
    g2 = g.reshape(1, k).astype(F32)
    if w_gate is None:
        return pl.pallas_call(
            _norm_mm_kernel,
            grid=grid,
            in_specs=[x_spec, g_spec, w_spec],
            out_specs=o_spec,
            out_shape=jax.ShapeDtypeStruct((m, n), BF16),
            scratch_shapes=scratch,
            compiler_params=_params(("parallel", "arbitrary"), vmem),
            name="norm_matmul",
        )(x, g2, w)
    ng = w_gate.shape[1]
    return pl.pallas_call(
        _norm_mm_gate_kernel,
        grid=grid,
        in_specs=[x_spec, g_spec, w_spec, pl.BlockSpec((k, ng), lambda i, j: (0, 0))],
        out_specs=[o_spec, pl.BlockSpec((tm, ng), lambda i, j: (i, 0))],
        out_shape=[jax.ShapeDtypeStruct((m, n), BF16), jax.ShapeDtypeStruct((m, ng), F32)],
        scratch_shapes=scratch,
        compiler_params=_params(("parallel", "arbitrary"), vmem),
        name="norm_matmul_gate",
    )(x, g2, w, w_gate)


def _fourier_kernel(a_ref, bd_ref, dft_ref, o_ref, z_ref, *, row_chunk):
    s = a_ref.shape[1]
    w = a_ref.shape[2]

    @pl.when(pl.program_id(1) == 0)
    def _():
        for r in range(0, s, row_chunk):
            a = a_ref[0, r:r + row_chunk, :]
            zc = jnp.dot(a, bd_ref[...], preferred_element_type=F32)
            z_ref[r:r + row_chunk, :] = zc[:, :w].astype(BF16)
            z_ref[s + r:s + r + row_chunk, :] = zc[:, w:].astype(BF16)

    dft = dft_ref[...].reshape(o_ref.shape[1], 2 * s)
    o_ref[0] = jnp.dot(dft, z_ref[...], preferred_element_type=F32).astype(o_ref.dtype)


def fourier_tables(s):
    cg = FOURIER_GROUP_DIM
    jj = jnp.arange(cg, dtype=jnp.int32)
    ang_c = (2.0 * math.pi / cg) * ((jj[:, None] * jj[None, :]) % cg).astype(F32)
    eye_g = jnp.eye(BRANCH_WIDTH // cg, dtype=F32)
    bd_c = jnp.kron(eye_g, jnp.cos(ang_c)) * cg ** -0.5
    bd_s = jnp.kron(eye_g, jnp.sin(ang_c)) * cg ** -0.5
    bd = jnp.concatenate([bd_c, bd_s], axis=1).astype(BF16)
    n_lo = s // cg
    nn = jnp.arange(s, dtype=jnp.int32)
    ang_hi = (2.0 * math.pi / cg) * ((jj[:, None] * nn[None, :]) % cg).astype(F32)
    ll = jnp.arange(n_lo, dtype=jnp.int32)
    ang_lo = (2.0 * math.pi / s) * ((ll[:, None] * nn[None, :]) % s).astype(F32)
    ch, sh = jnp.cos(ang_hi), jnp.sin(ang_hi)
    cl, sl = jnp.cos(ang_lo), jnp.sin(ang_lo)
    scale = s ** -0.5
    a2 = jnp.concatenate([ch, -sh], axis=1)[:, None, :]
    c2 = jnp.concatenate([sh, ch], axis=1)[:, None, :]
    b2 = jnp.concatenate([cl, cl], axis=1)[None, :, :]
    d2 = jnp.concatenate([sl, sl], axis=1)[None, :, :]
    dft = ((a2 * b2 - c2 * d2) * scale).astype(BF16)
    return bd, dft


def fourier_mix(p3, bd, dft, *, tm):
    b, s, _ = p3.shape
    w = BRANCH_WIDTH
    vmem = 2 * s * w * 2 + 2 * tm * 2 * s * 2 + 2 * s * w * 2 + 2 * w * 2 * w * 2 + 2 * tm * w * 2 + 8 * tm * w
    return pl.pallas_call(
        functools.partial(_fourier_kernel, row_chunk=min(s, 512)),
        grid=(b, s // tm),
        in_specs=[
            pl.BlockSpec((1, s, w), lambda bi, i: (bi, 0, OFF_A // w)),
            pl.BlockSpec((w, 2 * w), lambda bi, i: (0, 0)),
            pl.BlockSpec((tm // dft.shape[1], dft.shape[1], 2 * s), lambda bi, i: (i, 0, 0)),
        ],
        out_specs=pl.BlockSpec((1, tm, w), lambda bi, i: (bi, i, 0)),
        out_shape=jax.ShapeDtypeStruct((b, s, w), BF16),
        scratch_shapes=[pltpu.VMEM((2 * s, w), BF16)],
        compiler_params=_params(("parallel", "arbitrary"), vmem),
        name="fourier",
    )(p3, bd, dft)


def rope_tables(s):
    rows = s // GRID_W
    row_id = jnp.repeat(jnp.arange(rows, dtype=F32), GRID_W)
    col_id = jnp.tile(jnp.arange(GRID_W, dtype=F32), rows)
    n_pairs = HEAD_DIM // 4
    inv_freq = ROPE_BASE ** (-jnp.arange(n_pairs, dtype=F32) / n_pairs)
    ang = jnp.concatenate([row_id[:, None] * inv_freq, col_id[:, None] * inv_freq], axis=-1)
    cos, sin = jnp.cos(ang), jnp.sin(ang)
    return jnp.concatenate([cos, cos] * 2, axis=-1), jnp.concatenate([-sin, sin] * 2, axis=-1)


def _norm_rope(x, g, seg, cos2, sin2):
    half = HEAD_DIM // 2
    x2 = x * x
    hi = x2.astype(BF16)
    lo = (x2 - hi.astype(F32)).astype(BF16)
    ms = jnp.dot(hi, seg, preferred_element_type=F32) + jnp.dot(lo, seg, preferred_element_type=F32)
    y = x * lax.rsqrt(ms + EPS) * g
    lane = lax.broadcasted_iota(jnp.int32, (x.shape[0], V7X_LANES), 1)
    first_half = (lane % HEAD_DIM) < half
    outs = []
    for cb in range(x.shape[1] // V7X_LANES):
        yb = y[:, cb * V7X_LANES:(cb + 1) * V7X_LANES]
        rot = jnp.where(first_half, pltpu.roll(yb, V7X_LANES - half, axis=1), pltpu.roll(yb, half, axis=1))
        outs.append(yb * cos2 + rot * sin2)
    return outs[0] if len(outs) == 1 else jnp.concatenate(outs, axis=-1)


def _gqa_prep_kernel(q_ref, kv_ref, gq_ref, gk_ref, seg_ref, cos_ref, sin_ref, qo_ref, ko_ref, vo_ref):
    d = HEAD_DIM
    nk = GQA_KV_HEADS * d
    cos2, sin2 = cos_ref[...], sin_ref[...]
    q = q_ref[...].astype(F32)
    kv = kv_ref[...].astype(F32)
    qo_ref[...] = _norm_rope(q, gq_ref[...], seg_ref[...], cos2, sin2).astype(BF16)
    ko_ref[...] = _norm_rope(kv[:, :nk], gk_ref[...], seg_ref[:nk, :nk], cos2, sin2).astype(BF16)
    v = kv[:, nk:]
    lane = lax.broadcasted_iota(jnp.int32, v.shape, 1)
    ones_col = jnp.where(lane == d, 1.0, 0.0)
    vo_ref[:, :nk] = jnp.where(lane < d, v, ones_col).astype(BF16)
    vo_ref[:, nk:] = jnp.where(lane < d, pltpu.roll(v, d, axis=1), ones_col).astype(BF16)


def gqa_prep(p2, qk_g, cos2, sin2, *, s, tm):
    m = p2.shape[0]
    nq = GQA_Q_HEADS * HEAD_DIM
    nkv = GQA_KV_HEADS * HEAD_DIM
    assert nkv == V7X_LANES
    tiles_per_seq = s // tm
    q_scale = HEAD_DIM ** -0.5 * LOG2E
    gq = jnp.tile(qk_g[0].astype(F32) * q_scale, GQA_Q_HEADS).reshape(1, nq)
    gk = jnp.tile(qk_g[1].astype(F32), GQA_KV_HEADS).reshape(1, nkv)
    seg = jnp.kron(jnp.eye(GQA_Q_HEADS, dtype=F32), jnp.full((HEAD_DIM, HEAD_DIM), 1.0 / HEAD_DIM, F32)).astype(BF16)
    return pl.pallas_call(
        _gqa_prep_kernel,
        grid=(m // tm,),
        in_specs=[
            pl.BlockSpec((tm, nq), lambda i: (i, OFF_BQ // nq)),
            pl.BlockSpec((tm, 2 * nkv), lambda i: (i, OFF_BKV // (2 * nkv))),
            pl.BlockSpec((1, nq), lambda i: (0, 0)),
            pl.BlockSpec((1, nkv), lambda i: (0, 0)),
            pl.BlockSpec((nq, nq), lambda i: (0, 0)),
            pl.BlockSpec((tm, V7X_LANES), lambda i: (i % tiles_per_seq, 0)),
            pl.BlockSpec((tm, V7X_LANES), lambda i: (i % tiles_per_seq, 0)),
        ],
        out_specs=[
            pl.BlockSpec((tm, nq), lambda i: (i, 0)),
            pl.BlockSpec((tm, nkv), lambda i: (i, 0)),
            pl.BlockSpec((tm, 2 * nkv), lambda i: (i, 0)),
        ],
        out_shape=[
            jax.ShapeDtypeStruct((m, nq), BF16),
            jax.ShapeDtypeStruct((m, nkv), BF16),
            jax.ShapeDtypeStruct((m, 2 * nkv), BF16),
        ],
        compiler_params=_params(("parallel",), 16 * tm * nq * 4),
        name="gqa_prep",
    )(p2, p2, gq, gk, seg, cos2, sin2)


def _gqa_attn_kernel(q_ref, kt_ref, v_ref, o_ref, q_sc, m_ref, acc_ref, *, tk):
    d = HEAD_DIM
    tq = q_ref.shape[1]
    grp = q_ref.shape[2] // d
    s = kt_ref.shape[2]
    for g in range(grp):
        q_sc[g * tq:(g + 1) * tq, :] = q_ref[0, :, g * d:(g + 1) * d]

    def qk(item):
        c, g = item
        return jnp.dot(q_sc[g * tq:(g + 1) * tq, :], kt_ref[0, :, c * tk:(c + 1) * tk],
                       preferred_element_type=F32)

    items = [(c, g) for c in range(s // tk) for g in range(grp)]
    sc_next = qk(items[0])
    for n, (c, g) in enumerate(items):
        rows = slice(g * tq, (g + 1) * tq)
        sc = sc_next
        if n + 1 < len(items):
            sc_next = qk(items[n + 1])
        v = v_ref[0, c * tk:(c + 1) * tk, :]
        row_max = jnp.max(sc, axis=-1, keepdims=True)
        if c == 0:
            m_new = jnp.broadcast_to(row_max, (tq, V7X_LANES))
        else:
            m_prev = m_ref[rows, :]
            m_new = jnp.maximum(m_prev, row_max)
        p = jnp.exp2((sc - pltpu.repeat(m_new, tk // V7X_LANES, axis=1)).astype(BF16))
        pv = jnp.dot(p, v, preferred_element_type=F32)
        if c == 0:
            acc_ref[rows, :] = pv
        else:
            acc_ref[rows, :] = acc_ref[rows, :] * jnp.exp2(m_prev - m_new) + pv
        m_ref[rows, :] = m_new
    for g in range(grp):
        acc = acc_ref[g * tq:(g + 1) * tq, :]
        o_ref[0, :, g * d:(g + 1) * d] = (acc[:, :d] / acc[:, d:d + 1]).astype(o_ref.dtype)


def gqa_attention(q3, kt3, v3, *, tq, tk):
    b, s, nq = q3.shape
    d = HEAD_DIM
    grp = GQA_Q_HEADS // GQA_KV_HEADS
    mrows = grp * tq
    vmem = 2 * (tq * grp * d * 2 + d * s * 2 + s * 128 * 2 + tq * grp * d * 2) + 2 * mrows * 128 * 4 + 3 * mrows * tk * 4
    return pl.pallas_call(
        functools.partial(_gqa_attn_kernel, tk=tk),
        grid=(b, GQA_KV_HEADS, s // tq),
        in_specs=[
            pl.BlockSpec((1, tq, grp * d), lambda bi, kv, i: (bi, i, kv)),
            pl.BlockSpec((1, d, s), lambda bi, kv, i: (bi, kv, 0)),
            pl.BlockSpec((1, s, 2 * d), lambda bi, kv, i: (bi, 0, kv)),
        ],
        out_specs=pl.BlockSpec((1, tq, grp * d), lambda bi, kv, i: (bi, i, kv)),
        out_shape=jax.ShapeDtypeStruct((b, s, nq), BF16),
        scratch_shapes=[pltpu.VMEM((mrows, d), BF16), pltpu.VMEM((mrows, V7X_LANES), F32),
                        pltpu.VMEM((mrows, 2 * d), F32)],
        compiler_params=_params(("parallel", "parallel", "parallel"), vmem),
        name="gqa_attn",
    )(q3, kt3, v3)


def _mlstm_step(chains, ms, q_ref, k_ref, v_ref, r_sc, cm_sc, b_sc, st_sc, h_sc):
    L = MLSTM_CHUNK
    dh = MLSTM_HEAD_DIM
    assert L == dh
    scale = dh ** -0.5
    row_i = lax.broadcasted_iota(jnp.int32, (L, L), 0)
    col_i = lax.broadcasted_iota(jnp.int32, (L, L), 1)

    def col(x_row):
        return jnp.transpose(jnp.broadcast_to(x_row, (L, L)))

    pre = []
    for (hh, direction, c), m in zip(chains, ms):
        off = pl.multiple_of(c * L, L)
        lanes = slice(hh * dh, (hh + 1) * dh)
        q = q_ref[0, pl.ds(off, L), lanes]
        k = k_ref[0, pl.ds(off, L), lanes]
        v = v_ref[0, pl.ds(off, L), lanes]
        r_row = r_sc[hh, direction, pl.ds(c, 1), :]
        cm_row = cm_sc[hh, direction, pl.ds(c, 1), :]
        b_row = b_sc[hh, direction, pl.ds(c, 1), :]
        rmax = jnp.max(r_row, axis=-1, keepdims=True)
        btot = b_row[:, L - 1:L] if direction == 0 else b_row[:, 0:1]
        cmat = jnp.maximum(m, col(cm_row))
        mask = (row_i >= col_i) if direction == 0 else (row_i <= col_i)
        c_last = jnp.maximum(m, rmax)
        w_state = jnp.exp(col(r_row) - c_last) * scale
        pre.append(dict(
            off=off, lanes=lanes, q=q, k=k, v=v,
            w_intra=jnp.where(mask, jnp.exp(r_row - cmat), 0.0) * scale,
            w_inter=jnp.exp(m - cmat),
            den_floor=jnp.exp(-(col(b_row) + cmat)),
            decay=jnp.exp(m - c_last),
            kv_w=jnp.concatenate([w_state * v.astype(F32), w_state], axis=-1).astype(BF16),
            m_new=btot + c_last,
        ))
    s_raw = [lax.dot_general(p["q"], p["k"], (((1,), (1,)), ((), ())), preferred_element_type=F32) for p in pre]
    states = [st_sc[hh, direction] for hh, direction, _ in chains]
    inter = [jnp.dot(p["q"], st.astype(BF16), preferred_element_type=F32) for p, st in zip(pre, states)]
    upd = [lax.dot_general(p["k"], p["kv_w"], (((0,), (0,)), ((), ())), preferred_element_type=F32) for p in pre]
    for n, ((hh, direction, _), p) in enumerate(zip(chains, pre)):
        st_sc[hh, direction] = p["decay"] * states[n] + upd[n]
        v_aug = jnp.concatenate([p["v"], jnp.ones((L, dh), BF16)], axis=-1)
        intra = jnp.dot((s_raw[n] * p["w_intra"]).astype(BF16), v_aug, preferred_element_type=F32)
        h_aug = jnp.concatenate([p["w_inter"], p["w_inter"]], axis=-1) * inter[n] + intra
        h_sc[pl.ds(p["off"], L), p["lanes"]] += h_aug[:, :dh] / jnp.maximum(jnp.abs(h_aug[:, dh:]), p["den_floor"])
    return [p["m_new"] for p in pre]


def _mlstm_kernel(bias_ref, q_ref, k_ref, v_ref, o_ref, gate_ref, g_ref, y_ref,
                  r_sc, cm_sc, b_sc, h_sc, st_sc):
    L = MLSTM_CHUNK
    dh = MLSTM_HEAD_DIM
    hpb = q_ref.shape[2] // dh
    head0 = pl.program_id(1) * hpb
    nc = q_ref.shape[1] // L
    lane = lax.broadcasted_iota(jnp.int32, (nc, L), 1)
    shifts = [1 << t for t in range(int(math.log2(L)))]
    for hh in range(hpb):
        for d in range(2):
            i_pre = gate_ref[0, 2 * d, hh] + bias_ref[2 * d, head0 + hh]
            f_pre = gate_ref[0, 2 * d + 1, hh] + bias_ref[2 * d + 1, head0 + hh]
            logf = jnp.minimum(f_pre, 0.0) - jnp.log1p(jnp.exp(-jnp.abs(f_pre)))
            bc = logf
            for sh in shifts:
                if d == 0:
                    bc = bc + jnp.where(lane >= sh, pltpu.roll(bc, sh, axis=1), 0.0)
                else:
                    bc = bc + jnp.where(lane < L - sh, pltpu.roll(bc, L - sh, axis=1), 0.0)
            r = i_pre - bc
            cm = r
            for sh in shifts:
                if d == 0:
                    cm = jnp.maximum(cm, jnp.where(lane >= sh, pltpu.roll(cm, sh, axis=1), -jnp.inf))
                else:
                    cm = jnp.maximum(cm, jnp.where(lane < L - sh, pltpu.roll(cm, L - sh, axis=1), -jnp.inf))
            r_sc[hh, d] = r
            cm_sc[hh, d] = cm
            b_sc[hh, d] = bc
    st_sc[...] = jnp.zeros(st_sc.shape, F32)
    h_sc[...] = jnp.zeros(h_sc.shape, F32)

    def body(c, ms):
        chains = [(hh, d, c if d == 0 else nc - 1 - c) for hh in range(hpb) for d in range(2)]
        return tuple(_mlstm_step(chains, ms, q_ref, k_ref, v_ref, r_sc, cm_sc, b_sc, st_sc, h_sc))

    lax.fori_loop(0, nc, body, tuple(jnp.zeros((1, 1), F32) for _ in range(2 * hpb)))
    for hh in range(hpb):
        lanes = slice(hh * dh, (hh + 1) * dh)
        hsum = h_sc[:, lanes]
        ms = jnp.mean(hsum * hsum, axis=-1, keepdims=True)
        y = hsum * lax.rsqrt(ms + EPS) * g_ref[:, lanes]
        y_ref[0, :, lanes] = (jax.nn.sigmoid(o_ref[0, :, lanes].astype(F32)) * y).astype(y_ref.dtype)


def mlstm_branch(p3, gates5, gate_bias, norm_g, *, heads_per_block):
    b, s, _ = p3.shape
    L = MLSTM_CHUNK
    hpb = heads_per_block
    wb = hpb * MLSTM_HEAD_DIM
    nc = s // L
    blk = lambda off: pl.BlockSpec((1, s, wb), lambda bi, h, off=off: (bi, 0, off // wb + h))
    vmem = 2 * 5 * s * wb * 2 + s * wb * 4 + 6 * hpb * nc * L * 4 + 4 * hpb * wb * wb * 4 + 3 * s * wb * 4
    return pl.pallas_call(
        _mlstm_kernel,
        grid=(b, MLSTM_HEADS // hpb),
        in_specs=[
            pl.BlockSpec(memory_space=pltpu.SMEM),
            blk(OFF_CQ), blk(OFF_CK), blk(OFF_CV), blk(OFF_CO),
            pl.BlockSpec((1, 4, hpb, nc, L), lambda bi, h: (bi, 0, h, 0, 0)),
            pl.BlockSpec((1, wb), lambda bi, h: (0, h)),
        ],
        out_specs=pl.BlockSpec((1, s, wb), lambda bi, h: (bi, 0, h)),
        out_shape=jax.ShapeDtypeStruct((b, s, BRANCH_WIDTH), BF16),
        scratch_shapes=[
            pltpu.VMEM((hpb, 2, nc, L), F32), pltpu.VMEM((hpb, 2, nc, L), F32), pltpu.VMEM((hpb, 2, nc, L), F32),
            pltpu.VMEM((s, wb), F32),
            pltpu.VMEM((hpb, 2, MLSTM_HEAD_DIM, 2 * MLSTM_HEAD_DIM), F32),
        ],
        compiler_params=_params(("parallel", "parallel"), vmem),
        name="mlstm",
    )(gate_bias.astype(F32), p3, p3, p3, p3, gates5, norm_g.reshape(1, BRANCH_WIDTH).astype(F32))


def _rel_bucket(rel):
    half = REL_BUCKETS // 2
    max_exact = half // 2
    ret = jnp.where(rel > 0, half, 0)
    n = jnp.abs(rel)
    nf = jnp.maximum(n, 1).astype(F32)
    large = max_exact + (jnp.log(nf / max_exact) / math.log(REL_MAX_DIST / max_exact) * (half - max_exact)).astype(jnp.int32)
    large = jnp.minimum(large, half - 1)
    return ret + jnp.where(n < max_exact, n, large)


def diff_bias_tiles(rel_bias, t):
    assert t >= REL_MAX_DIST
    k = jnp.arange(2 * t, dtype=jnp.int32)
    rel = jnp.arange(-2, 3, dtype=jnp.int32)[:, None] * t + jnp.where(k < t, k, k - 2 * t)[None, :]
    onehot = (_rel_bucket(rel)[:, :, None] == jnp.arange(REL_BUCKETS, dtype=jnp.int32)).astype(F32)
    period = jnp.einsum('dkb,bh->hdk', onehot, rel_bias.astype(F32) * LOG2E, precision=lax.Precision.HIGHEST)
    far = period[:, 0::4, 0]
    return period, far


def _diff_attn_kernel(far_ref, q_ref, kt_ref, v_ref, period_ref, lam_ref, g_ref, o_ref,
                      q_sc, vaug_sc, bias_sc, m_ref, acc_ref, *, t, lam_init):
    dq = DIFF_QK_DIM
    dv = DIFF_V_DIM
    s = kt_ref.shape[2]
    nt = s // t
    head = pl.program_id(1)
    i = pl.program_id(2)

    @pl.when(i == 0)
    def _():
        vaug_sc[:, :dv] = v_ref[0]
        lane = lax.broadcasted_iota(jnp.int32, (s, dv), 1)
        vaug_sc[:, dv:] = jnp.where(lane == 0, 1.0, 0.0).astype(BF16)
        for dl in range(period_ref.shape[1]):
            full = jnp.broadcast_to(period_ref[0, dl:dl + 1, :], (t, 2 * t))
            bias_sc[dl] = pltpu.roll(full, 0, axis=1, stride=1, stride_axis=0)[:, :t]

    q_scale = dq ** -0.5 * LOG2E
    q_all = (q_ref[0].astype(F32) * q_scale).astype(BF16)
    q_sc[0] = q_all[:, :dq]
    q_sc[1] = q_all[:, dq:]

    def key_tile(delta):
        j = lax.rem(i + delta, nt)
        return j, pl.multiple_of(j * t, t)

    def qk(item):
        delta, mp = item
        _, off = key_tile(delta)
        return jnp.dot(q_sc[mp], kt_ref[0, mp * dq:(mp + 1) * dq, pl.ds(off, t)], preferred_element_type=F32)

    items = [(delta, mp) for delta in range(nt) for mp in range(2)]
    sc_next = qk(items[0])
    for n, (delta, mp) in enumerate(items):
        sc = sc_next
        if n + 1 < len(items):
            sc_next = qk(items[n + 1])
        j, off = key_tile(delta)
        if delta in (0, 1, nt - 1):
            sc = sc + bias_sc[jnp.clip(j - i, -2, 2) + 2]
            shift = None
        else:
            shift = jnp.where(j > i, far_ref[head, 1], far_ref[head, 0])
        row_max = jnp.max(sc, axis=-1, keepdims=True)
        if shift is not None:
            row_max = row_max + shift
        if delta == 0:
            m_new = jnp.broadcast_to(row_max, (t, V7X_LANES))
        else:
            m_prev = m_ref[mp]
            m_new = jnp.maximum(m_prev, row_max)
        m_sub = m_new if shift is None else m_new - shift
        p = jnp.exp2((sc - pltpu.repeat(m_sub, t // V7X_LANES, axis=1)).astype(BF16))
        pv = jnp.dot(p, vaug_sc[pl.ds(off, t), :], preferred_element_type=F32)
        if delta == 0:
            acc_ref[mp] = pv
        else:
            alpha = jnp.exp2(m_prev - m_new)
            acc_ref[mp] = acc_ref[mp] * pltpu.repeat(alpha, 2 * dv // V7X_LANES, axis=1) + pv
        m_ref[mp] = m_new
    lp = lam_ref[...]
    lam = (jnp.exp(jnp.sum(lp[0:1] * lp[1:2], axis=-1, keepdims=True))
           - jnp.exp(jnp.sum(lp[2:3] * lp[3:4], axis=-1, keepdims=True)) + lam_init)
    o0 = acc_ref[0, :, :dv] / acc_ref[0, :, dv:dv + 1]
    o1 = acc_ref[1, :, :dv] / acc_ref[1, :, dv:dv + 1]
    o = o0 - lam * o1
    ms = jnp.mean(o * o, axis=-1, keepdims=True)
    o_ref[0] = (o * lax.rsqrt(ms + EPS) * g_ref[...] * (1.0 - lam_init)).astype(o_ref.dtype)


def diff_attention(p3, kt3, bias_period, bias_far, lam_params, sub_g, *, t, layer_number):
    b, s, _ = p3.shape
    dv = DIFF_V_DIM
    n_off = bias_period.shape[1]
    assert s // t >= 4, "tiles 2 .. s/t-2 steps away from the query tile must all be beyond REL_MAX_DIST"
    lam_init = 0.8 - 0.6 * math.exp(-0.3 * (layer_number - 1))
    vmem = (2 * (t * 128 * 2 + 128 * s * 2 + s * dv * 2 + t * dv * 2) + n_off * t * t * 4 + s * 2 * dv * 2
            + 2 * t * 128 * 4 * 3 + 8 * t * t * 4)
    return pl.pallas_call(
        functools.partial(_diff_attn_kernel, t=t, lam_init=lam_init),
        grid=(b, DIFF_HEADS, s // t),
        in_specs=[
            pl.BlockSpec(memory_space=pltpu.SMEM),
            pl.BlockSpec((1, t, 128), lambda bi, h, i: (bi, i, OFF_DQ // 128 + h)),
            pl.BlockSpec((1, 128, s), lambda bi, h, i: (bi, h, 0)),
            pl.BlockSpec((1, s, dv), lambda bi, h, i: (bi, 0, OFF_DV // dv + h)),
            pl.BlockSpec((1, n_off, 2 * t), lambda bi, h, i: (h, 0, 0)),
            pl.BlockSpec((4, DIFF_QK_DIM), lambda bi, h, i: (0, 0)),
            pl.BlockSpec((1, dv), lambda bi, h, i: (0, 0)),
        ],
        out_specs=pl.BlockSpec((1, t, dv), lambda bi, h, i: (bi, i, h)),
        out_shape=jax.ShapeDtypeStruct((b, s, BRANCH_WIDTH), BF16),
        scratch_shapes=[pltpu.VMEM((2, t, DIFF_QK_DIM), BF16), pltpu.VMEM((s, 2 * dv), BF16),
                        pltpu.VMEM((n_off, t, t), F32),
                        pltpu.VMEM((2, t, V7X_LANES), F32), pltpu.VMEM((2, t, 2 * dv), F32)],
        compiler_params=_params(("parallel", "parallel", "arbitrary"), vmem),
        name="diff_attn",
    )(bias_far, p3, kt3, p3, bias_period, lam_params.astype(F32), sub_g.reshape(1, dv).astype(F32))


def _merge_kernel(ya_ref, yb_ref, yc_ref, yd_ref, gate_ref, wb_ref, wo_ref, x_ref, o_ref):
    dm = x_ref.shape[1]
    merged = None
    for n, y_ref in enumerate((ya_ref, yb_ref, yc_ref, yd_ref)):
        br = jnp.dot(y_ref[...], wb_ref[n], preferred_element_type=F32)
        term = jax.nn.sigmoid(gate_ref[:, n * dm:(n + 1) * dm].astype(F32)) * br
        merged = term if merged is None else merged + term
    o_ref[...] = x_ref[...] + jnp.dot(merged.astype(BF16), wo_ref[...], preferred_element_type=F32)


def merge_branches(ys, p2, w_branch, w_out, x2, *, tm):
    m, dm = x2.shape
    w = BRANCH_WIDTH
    y_spec = pl.BlockSpec((tm, w), lambda i: (i, 0))
    vmem = 2 * (4 * tm * w * 2 + tm * 4 * dm * 2 + 4 * w * dm * 2 + dm * dm * 2 + 2 * tm * dm * 4) + 6 * tm * dm * 4
    return pl.pallas_call(
        _merge_kernel,
        grid=(m // tm,),
        in_specs=[
            y_spec, y_spec, y_spec, y_spec,
            pl.BlockSpec((tm, N_BRANCHES * dm), lambda i: (i, OFF_GATES // (N_BRANCHES * dm))),
            pl.BlockSpec((N_BRANCHES, w, dm), lambda i: (0, 0, 0)),
            pl.BlockSpec((dm, dm), lambda i: (0, 0)),
            pl.BlockSpec((tm, dm), lambda i: (i, 0)),
        ],
        out_specs=pl.BlockSpec((tm, dm), lambda i: (i, 0)),
        out_shape=jax.ShapeDtypeStruct((m, dm), F32),
        compiler_params=_params(("parallel",), vmem),
        name="merge",
    )(*ys, p2, w_branch, w_out, x2)


HALO_ROWS = 16


def _ffn_down_kernel(a_ref, lin_ref, ap_ref, an_ref, shift_ref, cw_ref, cb_ref, wd_ref, x_ref, gf_ref, o_ref,
                     *, tiles_per_seq, final_norm, col_chunk):
    tm, f = a_ref.shape
    sub = 8
    i = pl.program_id(0)
    pos = i % tiles_per_seq
    r8 = lax.broadcasted_iota(jnp.int32, (sub, 1), 0)
    k0 = math.sqrt(2.0 / math.pi)
    bounds = list(range(0, f, col_chunk)) + [f]
    chunks = [slice(lo, hi) for lo, hi in zip(bounds[:-1], bounds[1:])]

    def shifts(cols):
        return jnp.dot(shift_ref[...], a_ref[:, cols], preferred_element_type=F32)

    y = x_ref[...]
    sh_next = shifts(chunks[0])
    for n, cols in enumerate(chunks):
        sh = sh_next
        if n + 1 < len(chunks):
            sh_next = shifts(chunks[n + 1])
        cw = cw_ref[:, cols]
        c = sh[:tm] * cw[0:1] + a_ref[:, cols].astype(F32) * cw[1:2] + sh[tm:] * cw[2:3] + cb_ref[:, cols]
        prev_row = jnp.where(pos == 0, 0.0, ap_ref[HALO_ROWS - 1:HALO_ROWS, cols].astype(F32))
        next_row = jnp.where(pos == tiles_per_seq - 1, 0.0, an_ref[0:1, cols].astype(F32))
        top = c[:sub] + jnp.where(r8 == 0, prev_row * cw[0:1], 0.0)
        bot = c[tm - sub:] + jnp.where(r8 == sub - 1, next_row * cw[2:3], 0.0)
        c = jnp.concatenate([top, c[sub:tm - sub], bot], axis=0)
        t = jnp.tanh(c * ((c * c) * (k0 * 0.044715) + k0))
        hmid = (c + c * t).astype(BF16) * lin_ref[:, cols]
        y = y + jnp.dot(hmid, wd_ref[cols, :], preferred_element_type=F32)
    if final_norm:
        ms = jnp.mean(y * y, axis=-1, keepdims=True)
        y = y * lax.rsqrt(ms + EPS) * gf_ref[...]
    o_ref[...] = y


def ffn_down(up2, conv_w, conv_b, w_down, x2, final_g, *, s, tm, final_norm):
    m, dm = x2.shape
    f = w_down.shape[0]
    tiles_per_seq = s // tm
    hb = tm // HALO_ROWS
    n_halo = m // HALO_ROWS
    vmem = 2 * (2 * tm * f * 2 + 2 * HALO_ROWS * f * 2 + f * dm * 2 + 2 * tm * dm * 4) + 10 * tm * f * 4
    r = jnp.arange(tm, dtype=jnp.int32)
    shift = jnp.concatenate([r[:, None] - 1 == r[None, :], r[:, None] + 1 == r[None, :]], axis=0).astype(BF16)
    return pl.pallas_call(
        functools.partial(_ffn_down_kernel, tiles_per_seq=tiles_per_seq, final_norm=final_norm, col_chunk=512),
        grid=(m // tm,),
        in_specs=[
            pl.BlockSpec((tm, f), lambda i: (i, 0)),
            pl.BlockSpec((tm, f), lambda i: (i, 1)),
            pl.BlockSpec((HALO_ROWS, f), lambda i: (jnp.maximum(i * hb - 1, 0), 0)),
            pl.BlockSpec((HALO_ROWS, f), lambda i: (jnp.minimum((i + 1) * hb, n_halo - 1), 0)),
            pl.BlockSpec((2 * tm, tm), lambda i: (0, 0)),
            pl.BlockSpec((CONV_W, f), lambda i: (0, 0)),
            pl.BlockSpec((1, f), lambda i: (0, 0)),
            pl.BlockSpec((f, dm), lambda i: (0, 0)),
            pl.BlockSpec((tm, dm), lambda i: (i, 0)),
            pl.BlockSpec((1, dm), lambda i: (0, 0)),
        ],
        out_specs=pl.BlockSpec((tm, dm), lambda i: (i, 0)),
        out_shape=jax.ShapeDtypeStruct((m, dm), F32),
        compiler_params=_params(("parallel",), vmem),
        name="ffn_down",
    )(up2, up2, up2, up2, shift, conv_w.astype(F32), conv_b.reshape(1, f).astype(F32), w_down, x2,
      final_g.reshape(1, dm).astype(F32))


def _arrange_w_in(w):
    widths = (512, 512, 128, 128, 512, 512, 512, 512, 16, 512, 512, 512, 4096)
    offs = [0]
    for wd in widths:
        offs.append(offs[-1] + wd)
    seg = lambda n: w[:, offs[n]:offs[n + 1]]
    a, bq, bk, bv, cq, ck, cv, co, cgate, dq, dk, dv, gates = (seg(n) for n in range(len(widths)))
    main = jnp.concatenate([gates, a, bq, cq, ck, cv, co, dq, dk, dv, bk, bv], axis=1).astype(BF16)
    gate = jnp.pad(cgate, ((0, 0), (0, V7X_LANES - cgate.shape[1]))).astype(BF16)
    return main, gate


def kernel(x, norm_mix_g, w_in, mlstm_gate_bias, qk_norm_g, mlstm_norm_g, diff_lambda, diff_norm_g, rel_bias,
           w_branch, w_out, norm_ffn_g, w_up, conv_w, conv_b, w_down, final_norm_g):
    b, s, dm = x.shape
    depth = w_in.shape[0]
    m = b * s
    d_ff = w_down.shape[1]
    L = MLSTM_CHUNK
    t_diff = 512

    bd, dft = fourier_tables(s)
    cos2, sin2 = rope_tables(s)
    bias_tiles, bias_far = diff_bias_tiles(rel_bias, t_diff)

    x2 = x.reshape(m, dm)
    for layer in range(depth):
        w_main, w_gate = _arrange_w_in(w_in[layer])
        p2, cgate = norm_matmul(x2, norm_mix_g[layer], w_main, w_gate, tm=1024, tn=1280)
        p3 = p2.reshape(b, s, P_WIDTH)

        y_a = fourier_mix(p3, bd, dft, tm=512)

        q_b, k_b, v_b = gqa_prep(p2, qk_norm_g[layer], cos2, sin2, s=s, tm=512)
        kt_b = jnp.swapaxes(k_b.reshape(b, s, GQA_KV_HEADS * HEAD_DIM), 1, 2)
        y_b = gqa_attention(q_b.reshape(b, s, -1), kt_b, v_b.reshape(b, s, -1), tq=256, tk=1024)

        gates5 = jnp.transpose(cgate[:, :4 * MLSTM_HEADS].reshape(b, s, 4, MLSTM_HEADS), (0, 2, 3, 1))
        gates5 = gates5.reshape(b, 4, MLSTM_HEADS, s // L, L)
        y_c = mlstm_branch(p3, gates5, mlstm_gate_bias[layer], mlstm_norm_g[layer], heads_per_block=2)

        kt_d = jnp.swapaxes(p3[:, :, OFF_DK:OFF_DK + BRANCH_WIDTH], 1, 2)
        y_d = diff_attention(p3, kt_d, bias_tiles, bias_far, diff_lambda[layer], diff_norm_g[layer],
                             t=t_diff, layer_number=layer + 1)

        ys = [y.reshape(m, BRANCH_WIDTH) for y in (y_a, y_b, y_c, y_d)]
        x2 = merge_branches(ys, p2, w_branch[layer].astype(BF16), w_out[layer].astype(BF16), x2, tm=512)

        half_lin = jnp.concatenate([jnp.ones((d_ff,), F32), jnp.full((d_ff,), 0.5, F32)])
        up2 = norm_matmul(x2, norm_ffn_g[layer], (w_up[layer] * half_lin).astype(BF16), tm=1024, tn=d_ff)
        x2 = ffn_down(up2, conv_w[layer], conv_b[layer], w_down[layer].astype(BF16), x2, final_norm_g,
                      s=s, tm=256, final_norm=(layer == depth - 1))
    return x2.reshape(b, s, dm)
```

```python
import functools
import math

import jax
import jax.numpy as jnp
from jax import lax
from jax.experimental import pallas as pl
from jax.experimental.pallas import tpu as pltpu

F32 = jnp.float32
BF16 = jnp.bfloat16

GRID_W = 64
HEAD_DIM = 64
BRANCH_WIDTH = 512
N_BRANCHES = 4
FOURIER_GROUP_DIM = 64
GQA_Q_HEADS = 8
GQA_KV_HEADS = 2
MLSTM_HEADS = 4
MLSTM_HEAD_DIM = 128
MLSTM_CHUNK = 128
DIFF_HEADS = 4
DIFF_QK_DIM = 64
DIFF_V_DIM = 128
REL_BUCKETS = 32
REL_MAX_DIST = 128
CONV_W = 3
ROPE_BASE = 10000.0
EPS = 1e-6
LOG2E = math.log2(math.e)

V7X_LANES = 128
V7X_VMEM_BYTES = 64 * 1024 * 1024
V7X_VMEM_CAP = V7X_VMEM_BYTES - 8 * 1024 * 1024

OFF_GATES = 0
OFF_A = 4096
OFF_BQ = 4608
OFF_CQ = 5120
OFF_CK = 5632
OFF_CV = 6144
OFF_CO = 6656
OFF_DQ = 7168
OFF_DK = 7680
OFF_DV = 8192
OFF_BKV = 8704
P_WIDTH = 8960


def _params(sem, vmem_bytes):
    limit = int(min(max(vmem_bytes * 3 // 2 + (4 << 20), 16 << 20), V7X_VMEM_CAP))
    return pltpu.CompilerParams(dimension_semantics=sem, vmem_limit_bytes=limit)


def _norm_mm_kernel(x_ref, g_ref, w_ref, o_ref, xn_ref):
    @pl.when(pl.program_id(1) == 0)
    def _():
        x = x_ref[...]
        ms = jnp.mean(x * x, axis=-1, keepdims=True)
        xn_ref[...] = (x * lax.rsqrt(ms + EPS) * g_ref[...]).astype(BF16)

    o_ref[...] = jnp.dot(xn_ref[...], w_ref[...], preferred_element_type=F32).astype(o_ref.dtype)


def _norm_mm_gate_kernel(x_ref, g_ref, w_ref, wg_ref, o_ref, og_ref, xn_ref):
    @pl.when(pl.program_id(1) == 0)
    def _():
        x = x_ref[...]
        ms = jnp.mean(x * x, axis=-1, keepdims=True)
        xn = (x * lax.rsqrt(ms + EPS) * g_ref[...]).astype(BF16)
        xn_ref[...] = xn
        og_ref[...] = jnp.dot(xn, wg_ref[...], preferred_element_type=F32)

    o_ref[...] = jnp.dot(xn_ref[...], w_ref[...], preferred_element_type=F32).astype(o_ref.dtype)


def norm_matmul(x, g, w, w_gate=None, *, tm, tn):
    m, k = x.shape
    n = w.shape[1]
    grid = (m // tm, n // tn)
    vmem = 2 * tm * k * 4 + tm * k * 2 + 2 * k * tn * 2 + 2 * tm * tn * 2 + 4 * tm * k
    x_spec = pl.BlockSpec((tm, k), lambda i, j: (i, 0))
    g_spec = pl.BlockSpec((1, k), lambda i, j: (0, 0))
    w_spec = pl.BlockSpec((k, tn), lambda i, j: (0, j))
    o_spec = pl.BlockSpec((tm, tn), lambda i, j: (i, j))
    scratch = [pltpu.VMEM((tm, k), BF16)]
    g2 = g.reshape(1, k).astype(F32)
    if w_gate is None:
        return pl.pallas_call(
            _norm_mm_kernel,
            grid=grid,
            in_specs=[x_spec, g_spec, w_spec],
            out_specs=o_spec,
            out_shape=jax.ShapeDtypeStruct((m, n), BF16),
            scratch_shapes=scratch,
            compiler_params=_params(("parallel", "arbitrary"), vmem),
            name="norm_matmul",
        )(x, g2, w)
    ng = w_gate.shape[1]
    return pl.pallas_call(
        _norm_mm_gate_kernel,
        grid=grid,
        in_specs=[x_spec, g_spec, w_spec, pl.BlockSpec((k, ng), lambda i, j: (0, 0))],
        out_specs=[o_spec, pl.BlockSpec((tm, ng), lambda i, j: (i, 0))],
        out_shape=[jax.ShapeDtypeStruct((m, n), BF16), jax.ShapeDtypeStruct((m, ng), F32)],
        scratch_shapes=scratch,
        compiler_params=_params(("parallel", "arbitrary"), vmem),
        name="norm_matmul_gate",
    )(x, g2, w, w_gate)


def _fourier_kernel(a_ref, bd_ref, dft_ref, o_ref, z_ref, *, row_chunk):
    s = a_ref.shape[1]
    w = a_ref.shape[2]

    @pl.when(pl.program_id(1) == 0)
    def _():
        for r in range(0, s, row_chunk):
            a = a_ref[0, r:r + row_chunk, :]
            zc = jnp.dot(a, bd_ref[...], preferred_element_type=F32)
            z_ref[r:r + row_chunk, :] = zc[:, :w].astype(BF16)
            z_ref[s + r:s + r + row_chunk, :] = zc[:, w:].astype(BF16)

    dft = dft_ref[...].reshape(o_ref.shape[1], 2 * s)
    o_ref[0] = jnp.dot(dft, z_ref[...], preferred_element_type=F32).astype(o_ref.dtype)


def fourier_tables(s):
    cg = FOURIER_GROUP_DIM
    jj = jnp.arange(cg, dtype=jnp.int32)
    ang_c = (2.0 * math.pi / cg) * ((jj[:, None] * jj[None, :]) % cg).astype(F32)
    eye_g = jnp.eye(BRANCH_WIDTH // cg, dtype=F32)
    bd_c = jnp.kron(eye_g, jnp.cos(ang_c)) * cg ** -0.5
    bd_s = jnp.kron(eye_g, jnp.sin(ang_c)) * cg ** -0.5
    bd = jnp.concatenate([bd_c, bd_s], axis=1).astype(BF16)
    n_lo = s // cg
    nn = jnp.arange(s, dtype=jnp.int32)
    ang_hi = (2.0 * math.pi / cg) * ((jj[:, None] * nn[None, :]) % cg).astype(F32)
    ll = jnp.arange(n_lo, dtype=jnp.int32)
    ang_lo = (2.0 * math.pi / s) * ((ll[:, None] * nn[None, :]) % s).astype(F32)
    ch, sh = jnp.cos(ang_hi), jnp.sin(ang_hi)
    cl, sl = jnp.cos(ang_lo), jnp.sin(ang_lo)
    scale = s ** -0.5
    a2 = jnp.concatenate([ch, -sh], axis=1)[:, None, :]
    c2 = jnp.concatenate([sh, ch], axis=1)[:, None, :]
    b2 = jnp.concatenate([cl, cl], axis=1)[None, :, :]
    d2 = jnp.concatenate([sl, sl], axis=1)[None, :, :]
    dft = ((a2 * b2 - c2 * d2) * scale).astype(BF16)
    return bd, dft


def fourier_mix(p3, bd, dft, *, tm):
    b, s, _ = p3.shape
    w = BRANCH_WIDTH
    vmem = 2 * s * w * 2 + 2 * tm * 2 * s * 2 + 2 * s * w * 2 + 2 * w * 2 * w * 2 + 2 * tm * w * 2 + 8 * tm * w
    return pl.pallas_call(
        functools.partial(_fourier_kernel, row_chunk=min(s, 512)),
        grid=(b, s // tm),
        in_specs=[
            pl.BlockSpec((1, s, w), lambda bi, i: (bi, 0, OFF_A // w)),
            pl.BlockSpec((w, 2 * w), lambda bi, i: (0, 0)),
            pl.BlockSpec((tm // dft.shape[1], dft.shape[1], 2 * s), lambda bi, i: (i, 0, 0)),
        ],
        out_specs=pl.BlockSpec((1, tm, w), lambda bi, i: (bi, i, 0)),
        out_shape=jax.ShapeDtypeStruct((b, s, w), BF16),
        scratch_shapes=[pltpu.VMEM((2 * s, w), BF16)],
        compiler_params=_params(("parallel", "arbitrary"), vmem),
        name="fourier",
    )(p3, bd, dft)


def rope_tables(s):
    rows = s // GRID_W
    row_id = jnp.repeat(jnp.arange(rows, dtype=F32), GRID_W)
    col_id = jnp.tile(jnp.arange(GRID_W, dtype=F32), rows)
    n_pairs = HEAD_DIM // 4
    inv_freq = ROPE_BASE ** (-jnp.arange(n_pairs, dtype=F32) / n_pairs)
    ang = jnp.concatenate([row_id[:, None] * inv_freq, col_id[:, None] * inv_freq], axis=-1)
    cos, sin = jnp.cos(ang), jnp.sin(ang)
    return jnp.concatenate([cos, cos] * 2, axis=-1), jnp.concatenate([-sin, sin] * 2, axis=-1)


def _norm_rope(x, g, seg, cos2, sin2):
    half = HEAD_DIM // 2
    x2 = x * x
    hi = x2.astype(BF16)
    lo = (x2 - hi.astype(F32)).astype(BF16)
    ms = jnp.dot(hi, seg, preferred_element_type=F32) + jnp.dot(lo, seg, preferred_element_type=F32)
    y = x * lax.rsqrt(ms + EPS) * g
    lane = lax.broadcasted_iota(jnp.int32, (x.shape[0], V7X_LANES), 1)
    first_half = (lane % HEAD_DIM) < half
    outs = []
    for cb in range(x.shape[1] // V7X_LANES):
        yb = y[:, cb * V7X_LANES:(cb + 1) * V7X_LANES]
        rot = jnp.where(first_half, pltpu.roll(yb, V7X_LANES - half, axis=1), pltpu.roll(yb, half, axis=1))
        outs.append(yb * cos2 + rot * sin2)
    return outs[0] if len(outs) == 1 else jnp.concatenate(outs, axis=-1)


def _gqa_prep_kernel(q_ref, kv_ref, gq_ref, gk_ref, seg_ref, cos_ref, sin_ref, qo_ref, ko_ref, vo_ref):
    d = HEAD_DIM
    nk = GQA_KV_HEADS * d
    cos2, sin2 = cos_ref[...], sin_ref[...]
    q = q_ref[...].astype(F32)
    kv = kv_ref[...].astype(F32)
    qo_ref[...] = _norm_rope(q, gq_ref[...], seg_ref[...], cos2, sin2).astype(BF16)
    ko_ref[...] = _norm_rope(kv[:, :nk], gk_ref[...], seg_ref[:nk, :nk], cos2, sin2).astype(BF16)
    v = kv[:, nk:]
    lane = lax.broadcasted_iota(jnp.int32, v.shape, 1)
    ones_col = jnp.where(lane == d, 1.0, 0.0)
    vo_ref[:, :nk] = jnp.where(lane < d, v, ones_col).astype(BF16)
    vo_ref[:, nk:] = jnp.where(lane < d, pltpu.roll(v, d, axis=1), ones_col).astype(BF16)


def gqa_prep(p2, qk_g, cos2, sin2, *, s, tm):
    m = p2.shape[0]
    nq = GQA_Q_HEADS * HEAD_DIM
    nkv = GQA_KV_HEADS * HEAD_DIM
    assert nkv == V7X_LANES
    tiles_per_seq = s // tm
    q_scale = HEAD_DIM ** -0.5 * LOG2E
    gq = jnp.tile(qk_g[0].astype(F32) * q_scale, GQA_Q_HEADS).reshape(1, nq)
    gk = jnp.tile(qk_g[1].astype(F32), GQA_KV_HEADS).reshape(1, nkv)
    seg = jnp.kron(jnp.eye(GQA_Q_HEADS, dtype=F32), jnp.full((HEAD_DIM, HEAD_DIM), 1.0 / HEAD_DIM, F32)).astype(BF16)
    return pl.pallas_call(
        _gqa_prep_kernel,
        grid=(m // tm,),
        in_specs=[
            pl.BlockSpec((tm, nq), lambda i: (i, OFF_BQ // nq)),
            pl.BlockSpec((tm, 2 * nkv), lambda i: (i, OFF_BKV // (2 * nkv))),
            pl.BlockSpec((1, nq), lambda i: (0, 0)),
            pl.BlockSpec((1, nkv), lambda i: (0, 0)),
            pl.BlockSpec((nq, nq), lambda i: (0, 0)),
            pl.BlockSpec((tm, V7X_LANES), lambda i: (i % tiles_per_seq, 0)),
            pl.BlockSpec((tm, V7X_LANES), lambda i: (i % tiles_per_seq, 0)),
        ],
        out_specs=[
            pl.BlockSpec((tm, nq), lambda i: (i, 0)),
            pl.BlockSpec((tm, nkv), lambda i: (i, 0)),
            pl.BlockSpec((tm, 2 * nkv), lambda i: (i, 0)),
        ],
        out_shape=[
            jax.ShapeDtypeStruct((m, nq), BF16),
            jax.ShapeDtypeStruct((m, nkv), BF16),
            jax.ShapeDtypeStruct((m, 2 * nkv), BF16),
        ],
        compiler_params=_params(("parallel",), 16 * tm * nq * 4),
        name="gqa_prep",
    )(p2, p2, gq, gk, seg, cos2, sin2)


def _gqa_attn_kernel(q_ref, kt_ref, v_ref, o_ref, q_sc, m_ref, acc_ref, *, tk):
    d = HEAD_DIM
    tq = q_ref.shape[1]
    grp = q_ref.shape[2] // d
    s = kt_ref.shape[2]
    for g in range(grp):
        q_sc[g * tq:(g + 1) * tq, :] = q_ref[0, :, g * d:(g + 1) * d]

    def qk(item):
        c, g = item
        return jnp.dot(q_sc[g * tq:(g + 1) * tq, :], kt_ref[0, :, c * tk:(c + 1) * tk],
                       preferred_element_type=F32)

    items = [(c, g) for c in range(s // tk) for g in range(grp)]
    sc_next = qk(items[0])
    for n, (c, g) in enumerate(items):
        rows = slice(g * tq, (g + 1) * tq)
        sc = sc_next
        if n + 1 < len(items):
            sc_next = qk(items[n + 1])
        v = v_ref[0, c * tk:(c + 1) * tk, :]
        row_max = jnp.max(sc, axis=-1, keepdims=True)
        if c == 0:
            m_new = jnp.broadcast_to(row_max, (tq, V7X_LANES))
        else:
            m_prev = m_ref[rows, :]
            m_new = jnp.maximum(m_prev, row_max)
        p = jnp.exp2((sc - pltpu.repeat(m_new, tk // V7X_LANES, axis=1)).astype(BF16))
        pv = jnp.dot(p, v, preferred_element_type=F32)
        if c == 0:
            acc_ref[rows, :] = pv
        else:
            acc_ref[rows, :] = acc_ref[rows, :] * jnp.exp2(m_prev - m_new) + pv
        m_ref[rows, :] = m_new
    for g in range(grp):
        acc = acc_ref[g * tq:(g + 1) * tq, :]
        o_ref[0, :, g * d:(g + 1) * d] = (acc[:, :d] / acc[:, d:d + 1]).astype(o_ref.dtype)


def gqa_attention(q3, kt3, v3, *, tq, tk):
    b, s, nq = q3.shape
    d = HEAD_DIM
    grp = GQA_Q_HEADS // GQA_KV_HEADS
    mrows = grp * tq
    vmem = 2 * (tq * grp * d * 2 + d * s * 2 + s * 128 * 2 + tq * grp * d * 2) + 2 * mrows * 128 * 4 + 3 * mrows * tk * 4
    return pl.pallas_call(
        functools.partial(_gqa_attn_kernel, tk=tk),
        grid=(b, GQA_KV_HEADS, s // tq),
        in_specs=[
            pl.BlockSpec((1, tq, grp * d), lambda bi, kv, i: (bi, i, kv)),
            pl.BlockSpec((1, d, s), lambda bi, kv, i: (bi, kv, 0)),
            pl.BlockSpec((1, s, 2 * d), lambda bi, kv, i: (bi, 0, kv)),
        ],
        out_specs=pl.BlockSpec((1, tq, grp * d), lambda bi, kv, i: (bi, i, kv)),
        out_shape=jax.ShapeDtypeStruct((b, s, nq), BF16),
        scratch_shapes=[pltpu.VMEM((mrows, d), BF16), pltpu.VMEM((mrows, V7X_LANES), F32),
                        pltpu.VMEM((mrows, 2 * d), F32)],
        compiler_params=_params(("parallel", "parallel", "parallel"), vmem),
        name="gqa_attn",
    )(q3, kt3, v3)


def _mlstm_step(chains, ms, q_ref, k_ref, v_ref, r_sc, cm_sc, b_sc, st_sc, h_sc):
    L = MLSTM_CHUNK
    dh = MLSTM_HEAD_DIM
    assert L == dh
    scale = dh ** -0.5
    row_i = lax.broadcasted_iota(jnp.int32, (L, L), 0)
    col_i = lax.broadcasted_iota(jnp.int32, (L, L), 1)

    def col(x_row):
        return jnp.transpose(jnp.broadcast_to(x_row, (L, L)))

    pre = []
    for (hh, direction, c), m in zip(chains, ms):
        off = pl.multiple_of(c * L, L)
        lanes = slice(hh * dh, (hh + 1) * dh)
        q = q_ref[0, pl.ds(off, L), lanes]
        k = k_ref[0, pl.ds(off, L), lanes]
        v = v_ref[0, pl.ds(off, L), lanes]
        r_row = r_sc[hh, direction, pl.ds(c, 1), :]
        cm_row = cm_sc[hh, direction, pl.ds(c, 1), :]
        b_row = b_sc[hh, direction, pl.ds(c, 1), :]
        rmax = jnp.max(r_row, axis=-1, keepdims=True)
        btot = b_row[:, L - 1:L] if direction == 0 else b_row[:, 0:1]
        cmat = jnp.maximum(m, col(cm_row))
        mask = (row_i >= col_i) if direction == 0 else (row_i <= col_i)
        c_last = jnp.maximum(m, rmax)
        w_state = jnp.exp(col(r_row) - c_last) * scale
        pre.append(dict(
            off=off, lanes=lanes, q=q, k=k, v=v,
            w_intra=jnp.where(mask, jnp.exp(r_row - cmat), 0.0) * scale,
            w_inter=jnp.exp(m - cmat),
            den_floor=jnp.exp(-(col(b_row) + cmat)),
            decay=jnp.exp(m - c_last),
            kv_w=jnp.concatenate([w_state * v.astype(F32), w_state], axis=-1).astype(BF16),
            m_new=btot + c_last,
        ))
    s_raw = [lax.dot_general(p["q"], p["k"], (((1,), (1,)), ((), ())), preferred_element_type=F32) for p in pre]
    states = [st_sc[hh, direction] for hh, direction, _ in chains]
    inter = [jnp.dot(p["q"], st.astype(BF16), preferred_element_type=F32) for p, st in zip(pre, states)]
    upd = [lax.dot_general(p["k"], p["kv_w"], (((0,), (0,)), ((), ())), preferred_element_type=F32) for p in pre]
    for n, ((hh, direction, _), p) in enumerate(zip(chains, pre)):
        st_sc[hh, direction] = p["decay"] * states[n] + upd[n]
        v_aug = jnp.concatenate([p["v"], jnp.ones((L, dh), BF16)], axis=-1)
        intra = jnp.dot((s_raw[n] * p["w_intra"]).astype(BF16), v_aug, preferred_element_type=F32)
        h_aug = jnp.concatenate([p["w_inter"], p["w_inter"]], axis=-1) * inter[n] + intra
        h_sc[pl.ds(p["off"], L), p["lanes"]] += h_aug[:, :dh] / jnp.maximum(jnp.abs(h_aug[:, dh:]), p["den_floor"])
    return [p["m_new"] for p in pre]


def _mlstm_kernel(bias_ref, q_ref, k_ref, v_ref, o_ref, gate_ref, g_ref, y_ref,
                  r_sc, cm_sc, b_sc, h_sc, st_sc):
    L = MLSTM_CHUNK
    dh = MLSTM_HEAD_DIM
    hpb = q_ref.shape[2] // dh
    head0 = pl.program_id(1) * hpb
    nc = q_ref.shape[1] // L
    lane = lax.broadcasted_iota(jnp.int32, (nc, L), 1)
    shifts = [1 << t for t in range(int(math.log2(L)))]
    for hh in range(hpb):
        for d in range(2):
            i_pre = gate_ref[0, 2 * d, hh] + bias_ref[2 * d, head0 + hh]
            f_pre = gate_ref[0, 2 * d + 1, hh] + bias_ref[2 * d + 1, head0 + hh]
            logf = jnp.minimum(f_pre, 0.0) - jnp.log1p(jnp.exp(-jnp.abs(f_pre)))
            bc = logf
            for sh in shifts:
                if d == 0:
                    bc = bc + jnp.where(lane >= sh, pltpu.roll(bc, sh, axis=1), 0.0)
                else:
                    bc = bc + jnp.where(lane < L - sh, pltpu.roll(bc, L - sh, axis=1), 0.0)
            r = i_pre - bc
            cm = r
            for sh in shifts:
                if d == 0:
                    cm = jnp.maximum(cm, jnp.where(lane >= sh, pltpu.roll(cm, sh, axis=1), -jnp.inf))
                else:
                    cm = jnp.maximum(cm, jnp.where(lane < L - sh, pltpu.roll(cm, L - sh, axis=1), -jnp.inf))
            r_sc[hh, d] = r
            cm_sc[hh, d] = cm
            b_sc[hh, d] = bc
    st_sc[...] = jnp.zeros(st_sc.shape, F32)
    h_sc[...] = jnp.zeros(h_sc.shape, F32)

    def body(c, ms):
        chains = [(hh, d, c if d == 0 else nc - 1 - c) for hh in range(hpb) for d in range(2)]
        return tuple(_mlstm_step(chains, ms, q_ref, k_ref, v_ref, r_sc, cm_sc, b_sc, st_sc, h_sc))

    lax.fori_loop(0, nc, body, tuple(jnp.zeros((1, 1), F32) for _ in range(2 * hpb)))
    for hh in range(hpb):
        lanes = slice(hh * dh, (hh + 1) * dh)
        hsum = h_sc[:, lanes]
        ms = jnp.mean(hsum * hsum, axis=-1, keepdims=True)
        y = hsum * lax.rsqrt(ms + EPS) * g_ref[:, lanes]
        y_ref[0, :, lanes] = (jax.nn.sigmoid(o_ref[0, :, lanes].astype(F32)) * y).astype(y_ref.dtype)


def mlstm_branch(p3, gates5, gate_bias, norm_g, *, heads_per_block):
    b, s, _ = p3.shape
    L = MLSTM_CHUNK
    hpb = heads_per_block
    wb = hpb * MLSTM_HEAD_DIM
    nc = s // L
    blk = lambda off: pl.BlockSpec((1, s, wb), lambda bi, h, off=off: (bi, 0, off // wb + h))
    vmem = 2 * 5 * s * wb * 2 + s * wb * 4 + 6 * hpb * nc * L * 4 + 4 * hpb * wb * wb * 4 + 3 * s * wb * 4
    return pl.pallas_call(
        _mlstm_kernel,
        grid=(b, MLSTM_HEADS // hpb),
        in_specs=[
            pl.BlockSpec(memory_space=pltpu.SMEM),
            blk(OFF_CQ), blk(OFF_CK), blk(OFF_CV), blk(OFF_CO),
            pl.BlockSpec((1, 4, hpb, nc, L), lambda bi, h: (bi, 0, h, 0, 0)),
            pl.BlockSpec((1, wb), lambda bi, h: (0, h)),
        ],
        out_specs=pl.BlockSpec((1, s, wb), lambda bi, h: (bi, 0, h)),
        out_shape=jax.ShapeDtypeStruct((b, s, BRANCH_WIDTH), BF16),
        scratch_shapes=[
            pltpu.VMEM((hpb, 2, nc, L), F32), pltpu.VMEM((hpb, 2, nc, L), F32), pltpu.VMEM((hpb, 2, nc, L), F32),
            pltpu.VMEM((s, wb), F32),
            pltpu.VMEM((hpb, 2, MLSTM_HEAD_DIM, 2 * MLSTM_HEAD_DIM), F32),
        ],
        compiler_params=_params(("parallel", "parallel"), vmem),
        name="mlstm",
    )(gate_bias.astype(F32), p3, p3, p3, p3, gates5, norm_g.reshape(1, BRANCH_WIDTH).astype(F32))


def _rel_bucket(rel):
    half = REL_BUCKETS // 2
    max_exact = half // 2
    ret = jnp.where(rel > 0, half, 0)
    n = jnp.abs(rel)
    nf = jnp.maximum(n, 1).astype(F32)
    large = max_exact + (jnp.log(nf / max_exact) / math.log(REL_MAX_DIST / max_exact) * (half - max_exact)).astype(jnp.int32)
    large = jnp.minimum(large, half - 1)
    return ret + jnp.where(n < max_exact, n, large)


def diff_bias_tiles(rel_bias, t):
    assert t >= REL_MAX_DIST
    k = jnp.arange(2 * t, dtype=jnp.int32)
    rel = jnp.arange(-2, 3, dtype=jnp.int32)[:, None] * t + jnp.where(k < t, k, k - 2 * t)[None, :]
    onehot = (_rel_bucket(rel)[:, :, None] == jnp.arange(REL_BUCKETS, dtype=jnp.int32)).astype(F32)
    period = jnp.einsum('dkb,bh->hdk', onehot, rel_bias.astype(F32) * LOG2E, precision=lax.Precision.HIGHEST)
    far = period[:, 0::4, 0]
    return period, far


def _diff_attn_kernel(far_ref, q_ref, kt_ref, v_ref, period_ref, lam_ref, g_ref, o_ref,
                      q_sc, vaug_sc, bias_sc, m_ref, acc_ref, *, t, lam_init):
    dq = DIFF_QK_DIM
    dv = DIFF_V_DIM
    s = kt_ref.shape[2]
    nt = s // t
    head = pl.program_id(1)
    i = pl.program_id(2)

    @pl.when(i == 0)
    def _():
        vaug_sc[:, :dv] = v_ref[0]
        lane = lax.broadcasted_iota(jnp.int32, (s, dv), 1)
        vaug_sc[:, dv:] = jnp.where(lane == 0, 1.0, 0.0).astype(BF16)
        for dl in range(period_ref.shape[1]):
            full = jnp.broadcast_to(period_ref[0, dl:dl + 1, :], (t, 2 * t))
            bias_sc[dl] = pltpu.roll(full, 0, axis=1, stride=1, stride_axis=0)[:, :t]

    q_scale = dq ** -0.5 * LOG2E
    q_all = (q_ref[0].astype(F32) * q_scale).astype(BF16)
    q_sc[0] = q_all[:, :dq]
    q_sc[1] = q_all[:, dq:]

    def key_tile(delta):
        j = lax.rem(i + delta, nt)
        return j, pl.multiple_of(j * t, t)

    rb = t // 2
    assert nt % 2 == 0

    def qk(item):
        pair, mp, hf = item
        out = []
        for delta in (2 * pair, 2 * pair + 1):
            _, off = key_tile(delta)
            out.append(jnp.dot(q_sc[mp, hf * rb:(hf + 1) * rb, :], kt_ref[0, mp * dq:(mp + 1) * dq, pl.ds(off, t)],
                               preferred_element_type=F32))
        return out

    items = [(pair, mp, hf) for pair in range(nt // 2) for mp in range(2) for hf in range(2)]
    sc_next = qk(items[0])
    for n, (pair, mp, hf) in enumerate(items):
        scs = sc_next
        if n + 1 < len(items):
            sc_next = qk(items[n + 1])
        rows = slice(hf * rb, (hf + 1) * rb)
        shifts, offs, row_max = [], [], None
        for piece, delta in enumerate((2 * pair, 2 * pair + 1)):
            j, off = key_tile(delta)
            offs.append(off)
            if delta in (0, 1, nt - 1):
                tile = 2 if delta == 0 else jnp.clip(j - i, -2, 2) + 2
                scs[piece] = scs[piece] + bias_sc[tile, rows, :]
                shifts.append(None)
                rm = jnp.max(scs[piece], axis=-1, keepdims=True)
            else:
                shifts.append(jnp.where(j > i, far_ref[head, 1], far_ref[head, 0]))
                rm = jnp.max(scs[piece], axis=-1, keepdims=True) + shifts[-1]
            row_max = rm if row_max is None else jnp.maximum(row_max, rm)
        if pair == 0:
            m_new = jnp.broadcast_to(row_max, (rb, V7X_LANES))
        else:
            m_prev = m_ref[mp, rows, :]
            m_new = jnp.maximum(m_prev, row_max)
        pv = None
        for piece in range(2):
            m_sub = m_new if shifts[piece] is None else m_new - shifts[piece]
            p = jnp.exp2((scs[piece] - pltpu.repeat(m_sub, t // V7X_LANES, axis=1)).astype(BF16))
            term = jnp.dot(p, vaug_sc[pl.ds(offs[piece], t), :], preferred_element_type=F32)
            pv = term if pv is None else pv + term
        if pair == 0:
            acc_ref[mp, rows, :] = pv
        else:
            alpha = jnp.exp2(m_prev - m_new)
            acc_ref[mp, rows, :] = acc_ref[mp, rows, :] * pltpu.repeat(alpha, 2 * dv // V7X_LANES, axis=1) + pv
        m_ref[mp, rows, :] = m_new
    lp = lam_ref[...]
    lam = (jnp.exp(jnp.sum(lp[0:1] * lp[1:2], axis=-1, keepdims=True))
           - jnp.exp(jnp.sum(lp[2:3] * lp[3:4], axis=-1, keepdims=True)) + lam_init)
    o0 = acc_ref[0, :, :dv] / acc_ref[0, :, dv:dv + 1]
    o1 = acc_ref[1, :, :dv] / acc_ref[1, :, dv:dv + 1]
    o = o0 - lam * o1
    ms = jnp.mean(o * o, axis=-1, keepdims=True)
    o_ref[0] = (o * lax.rsqrt(ms + EPS) * g_ref[...] * (1.0 - lam_init)).astype(o_ref.dtype)


def diff_attention(p3, kt3, bias_period, bias_far, lam_params, sub_g, *, t, layer_number):
    b, s, _ = p3.shape
    dv = DIFF_V_DIM
    n_off = bias_period.shape[1]
    assert s // t >= 4, "tiles 2 .. s/t-2 steps away from the query tile must all be beyond REL_MAX_DIST"
    lam_init = 0.8 - 0.6 * math.exp(-0.3 * (layer_number - 1))
    vmem = (2 * (t * 128 * 2 + 128 * s * 2 + s * dv * 2 + t * dv * 2) + n_off * t * t * 4 + s * 2 * dv * 2
            + 2 * t * 128 * 4 * 3 + 8 * t * t * 4)
    return pl.pallas_call(
        functools.partial(_diff_attn_kernel, t=t, lam_init=lam_init),
        grid=(b, DIFF_HEADS, s // t),
        in_specs=[
            pl.BlockSpec(memory_space=pltpu.SMEM),
            pl.BlockSpec((1, t, 128), lambda bi, h, i: (bi, i, OFF_DQ // 128 + h)),
            pl.BlockSpec((1, 128, s), lambda bi, h, i: (bi, h, 0)),
            pl.BlockSpec((1, s, dv), lambda bi, h, i: (bi, 0, OFF_DV // dv + h)),
            pl.BlockSpec((1, n_off, 2 * t), lambda bi, h, i: (h, 0, 0)),
            pl.BlockSpec((4, DIFF_QK_DIM), lambda bi, h, i: (0, 0)),
            pl.BlockSpec((1, dv), lambda bi, h, i: (0, 0)),
        ],
        out_specs=pl.BlockSpec((1, t, dv), lambda bi, h, i: (bi, i, h)),
        out_shape=jax.ShapeDtypeStruct((b, s, BRANCH_WIDTH), BF16),
        scratch_shapes=[pltpu.VMEM((2, t, DIFF_QK_DIM), BF16), pltpu.VMEM((s, 2 * dv), BF16),
                        pltpu.VMEM((n_off, t, t), F32),
                        pltpu.VMEM((2, t, V7X_LANES), F32), pltpu.VMEM((2, t, 2 * dv), F32)],
        compiler_params=_params(("parallel", "parallel", "arbitrary"), vmem),
        name="diff_attn",
    )(bias_far, p3, kt3, p3, bias_period, lam_params.astype(F32), sub_g.reshape(1, dv).astype(F32))


def _merge_kernel(ya_ref, yb_ref, yc_ref, yd_ref, gate_ref, wb_ref, wo_ref, x_ref, o_ref):
    dm = x_ref.shape[1]
    merged = None
    for n, y_ref in enumerate((ya_ref, yb_ref, yc_ref, yd_ref)):
        br = jnp.dot(y_ref[...], wb_ref[n], preferred_element_type=F32)
        term = jax.nn.sigmoid(gate_ref[:, n * dm:(n + 1) * dm].astype(F32)) * br
        merged = term if merged is None else merged + term
    o_ref[...] = x_ref[...] + jnp.dot(merged.astype(BF16), wo_ref[...], preferred_element_type=F32)


def merge_branches(ys, p2, w_branch, w_out, x2, *, tm):
    m, dm = x2.shape
    w = BRANCH_WIDTH
    y_spec = pl.BlockSpec((tm, w), lambda i: (i, 0))
    vmem = 2 * (4 * tm * w * 2 + tm * 4 * dm * 2 + 4 * w * dm * 2 + dm * dm * 2 + 2 * tm * dm * 4) + 6 * tm * dm * 4
    return pl.pallas_call(
        _merge_kernel,
        grid=(m // tm,),
        in_specs=[
            y_spec, y_spec, y_spec, y_spec,
            pl.BlockSpec((tm, N_BRANCHES * dm), lambda i: (i, OFF_GATES // (N_BRANCHES * dm))),
            pl.BlockSpec((N_BRANCHES, w, dm), lambda i: (0, 0, 0)),
            pl.BlockSpec((dm, dm), lambda i: (0, 0)),
            pl.BlockSpec((tm, dm), lambda i: (i, 0)),
        ],
        out_specs=pl.BlockSpec((tm, dm), lambda i: (i, 0)),
        out_shape=jax.ShapeDtypeStruct((m, dm), F32),
        compiler_params=_params(("parallel",), vmem),
        name="merge",
    )(*ys, p2, w_branch, w_out, x2)


HALO_ROWS = 16


def _ffn_up_kernel(x_ref, xp_ref, xn_ref, g_ref, wa_ref, wl_ref, cw_ref, cb_ref, o_ref, xe_sc,
                   *, tiles_per_seq, col_chunk):
    tm, f = o_ref.shape
    h = HALO_ROWS
    pos = pl.program_id(0) % tiles_per_seq

    def norm(x):
        ms = jnp.mean(x * x, axis=-1, keepdims=True)
        return x * lax.rsqrt(ms + EPS) * g_ref[...]

    xe_sc[:h] = jnp.where(pos == 0, 0.0, norm(xp_ref[...])).astype(BF16)
    xe_sc[h:h + tm] = norm(x_ref[...]).astype(BF16)
    xe_sc[h + tm:] = jnp.where(pos == tiles_per_seq - 1, 0.0, norm(xn_ref[...])).astype(BF16)
    k0 = math.sqrt(2.0 / math.pi)
    bounds = list(range(0, f, col_chunk)) + [f]
    chunks = [slice(lo, hi) for lo, hi in zip(bounds[:-1], bounds[1:])]

    def project(cols):
        a_ext = jnp.dot(xe_sc[...], wa_ref[:, cols], preferred_element_type=F32)
        lin = jnp.dot(xe_sc[h:h + tm, :], wl_ref[:, cols], preferred_element_type=F32)
        return a_ext, lin

    nxt = project(chunks[0])
    for n, cols in enumerate(chunks):
        a_ext, lin = nxt
        if n + 1 < len(chunks):
            nxt = project(chunks[n + 1])
        cw = cw_ref[:, cols]
        c = (a_ext[h - 1:h - 1 + tm] * cw[0:1] + a_ext[h:h + tm] * cw[1:2] + a_ext[h + 1:h + 1 + tm] * cw[2:3]
             + cb_ref[:, cols])
        t = jnp.tanh(c * ((c * c) * (k0 * 0.044715) + k0))
        o_ref[:, cols] = ((c + c * t) * lin).astype(o_ref.dtype)


def ffn_up(x2, g, w_a, w_lin, conv_w, conv_b, *, s, tm):
    m, dm = x2.shape
    f = w_a.shape[1]
    tiles_per_seq = s // tm
    hb = tm // HALO_ROWS
    n_halo = m // HALO_ROWS
    vmem = 2 * (tm * dm * 4 + tm * f * 2) + 2 * dm * f * 2 + (tm + 2 * HALO_ROWS) * dm * 2 + 12 * tm * 512 * 4
    return pl.pallas_call(
        functools.partial(_ffn_up_kernel, tiles_per_seq=tiles_per_seq, col_chunk=512),
        grid=(m // tm,),
        in_specs=[
            pl.BlockSpec((tm, dm), lambda i: (i, 0)),
            pl.BlockSpec((HALO_ROWS, dm), lambda i: (jnp.maximum(i * hb - 1, 0), 0)),
            pl.BlockSpec((HALO_ROWS, dm), lambda i: (jnp.minimum((i + 1) * hb, n_halo - 1), 0)),
            pl.BlockSpec((1, dm), lambda i: (0, 0)),
            pl.BlockSpec((dm, f), lambda i: (0, 0), pipeline_mode=pl.Buffered(1)),
            pl.BlockSpec((dm, f), lambda i: (0, 0), pipeline_mode=pl.Buffered(1)),
            pl.BlockSpec((CONV_W, f), lambda i: (0, 0)),
            pl.BlockSpec((1, f), lambda i: (0, 0)),
        ],
        out_specs=pl.BlockSpec((tm, f), lambda i: (i, 0)),
        out_shape=jax.ShapeDtypeStruct((m, f), BF16),
        scratch_shapes=[pltpu.VMEM((tm + 2 * HALO_ROWS, dm), BF16)],
        compiler_params=_params(("parallel",), vmem),
        name="ffn_up",
    )(x2, x2, x2, g.reshape(1, dm).astype(F32), w_a, w_lin, conv_w.astype(F32), conv_b.reshape(1, f).astype(F32))


def _ffn_down_kernel(h_ref, wd_ref, x_ref, gf_ref, o_ref, *, final_norm):
    y = x_ref[...] + jnp.dot(h_ref[...], wd_ref[...], preferred_element_type=F32)
    if final_norm:
        ms = jnp.mean(y * y, axis=-1, keepdims=True)
        y = y * lax.rsqrt(ms + EPS) * gf_ref[...]
    o_ref[...] = y


def ffn_down(hmid, w_down, x2, final_g, *, tm, final_norm):
    m, dm = x2.shape
    f = w_down.shape[0]
    vmem = 2 * (tm * f * 2 + f * dm * 2 + 2 * tm * dm * 4) + 4 * tm * dm * 4
    return pl.pallas_call(
        functools.partial(_ffn_down_kernel, final_norm=final_norm),
        grid=(m // tm,),
        in_specs=[
            pl.BlockSpec((tm, f), lambda i: (i, 0)),
            pl.BlockSpec((f, dm), lambda i: (0, 0)),
            pl.BlockSpec((tm, dm), lambda i: (i, 0)),
            pl.BlockSpec((1, dm), lambda i: (0, 0)),
        ],
        out_specs=pl.BlockSpec((tm, dm), lambda i: (i, 0)),
        out_shape=jax.ShapeDtypeStruct((m, dm), F32),
        compiler_params=_params(("parallel",), vmem),
        name="ffn_down",
    )(hmid, w_down, x2, final_g.reshape(1, dm).astype(F32))


def _arrange_w_in(w):
    widths = (512, 512, 128, 128, 512, 512, 512, 512, 16, 512, 512, 512, 4096)
    offs = [0]
    for wd in widths:
        offs.append(offs[-1] + wd)
    seg = lambda n: w[:, offs[n]:offs[n + 1]]
    a, bq, bk, bv, cq, ck, cv, co, cgate, dq, dk, dv, gates = (seg(n) for n in range(len(widths)))
    main = jnp.concatenate([gates, a, bq, cq, ck, cv, co, dq, dk, dv, bk, bv], axis=1).astype(BF16)
    gate = jnp.pad(cgate, ((0, 0), (0, V7X_LANES - cgate.shape[1]))).astype(BF16)
    return main, gate


def kernel(x, norm_mix_g, w_in, mlstm_gate_bias, qk_norm_g, mlstm_norm_g, diff_lambda, diff_norm_g, rel_bias,
           w_branch, w_out, norm_ffn_g, w_up, conv_w, conv_b, w_down, final_norm_g):
    b, s, dm = x.shape
    depth = w_in.shape[0]
    m = b * s
    d_ff = w_down.shape[1]
    L = MLSTM_CHUNK
    t_diff = 512

    bd, dft = fourier_tables(s)
    cos2, sin2 = rope_tables(s)
    bias_tiles, bias_far = diff_bias_tiles(rel_bias, t_diff)

    x2 = x.reshape(m, dm)
    for layer in range(depth):
        w_main, w_gate = _arrange_w_in(w_in[layer])
        p2, cgate = norm_matmul(x2, norm_mix_g[layer], w_main, w_gate, tm=1024, tn=1280)
        p3 = p2.reshape(b, s, P_WIDTH)

        y_a = fourier_mix(p3, bd, dft, tm=512)

        q_b, k_b, v_b = gqa_prep(p2, qk_norm_g[layer], cos2, sin2, s=s, tm=512)
        kt_b = jnp.swapaxes(k_b.reshape(b, s, GQA_KV_HEADS * HEAD_DIM), 1, 2)
        y_b = gqa_attention(q_b.reshape(b, s, -1), kt_b, v_b.reshape(b, s, -1), tq=256, tk=1024)

        gates5 = jnp.transpose(cgate[:, :4 * MLSTM_HEADS].reshape(b, s, 4, MLSTM_HEADS), (0, 2, 3, 1))
        gates5 = gates5.reshape(b, 4, MLSTM_HEADS, s // L, L)
        y_c = mlstm_branch(p3, gates5, mlstm_gate_bias[layer], mlstm_norm_g[layer], heads_per_block=2)

        kt_d = jnp.swapaxes(p3[:, :, OFF_DK:OFF_DK + BRANCH_WIDTH], 1, 2)
        y_d = diff_attention(p3, kt_d, bias_tiles, bias_far, diff_lambda[layer], diff_norm_g[layer],
                             t=t_diff, layer_number=layer + 1)

        ys = [y.reshape(m, BRANCH_WIDTH) for y in (y_a, y_b, y_c, y_d)]
        x2 = merge_branches(ys, p2, w_branch[layer].astype(BF16), w_out[layer].astype(BF16), x2, tm=512)

        w_a = w_up[layer][:, :d_ff].astype(BF16)
        w_lin = (w_up[layer][:, d_ff:] * 0.5).astype(BF16)
        hmid = ffn_up(x2, norm_ffn_g[layer], w_a, w_lin, conv_w[layer], conv_b[layer], s=s, tm=1024)
        x2 = ffn_down(hmid, w_down[layer].astype(BF16), x2, final_norm_g, tm=512, final_norm=(layer == depth - 1))
    return x2.reshape(b, s, dm)
```

```python
import functools
import math

import jax
import jax.numpy as jnp
from jax import lax
from jax.experimental import pallas as pl
from jax.experimental.pallas import tpu as pltpu

F32 = jnp.float32
BF16 = jnp.bfloat16

GRID_W = 64
HEAD_DIM = 64
BRANCH_WIDTH = 512
N_BRANCHES = 4
FOURIER_GROUP_DIM = 64
GQA_Q_HEADS = 8
GQA_KV_HEADS = 2
MLSTM_HEADS = 4
MLSTM_HEAD_DIM = 128
MLSTM_CHUNK = 128
DIFF_HEADS = 4
DIFF_QK_DIM = 64
DIFF_V_DIM = 128
REL_BUCKETS = 32
REL_MAX_DIST = 128
CONV_W = 3
ROPE_BASE = 10000.0
EPS = 1e-6
LOG2E = math.log2(math.e)

V7X_LANES = 128
V7X_VMEM_BYTES = 64 * 1024 * 1024
V7X_VMEM_CAP = V7X_VMEM_BYTES - 8 * 1024 * 1024

OFF_GATES = 0
OFF_A = 4096
OFF_BQ = 4608
OFF_CQ = 5120
OFF_CK = 5632
OFF_CV = 6144
OFF_CO = 6656
OFF_DQ = 7168
OFF_DK = 7680
OFF_DV = 8192
OFF_BKV = 8704
P_WIDTH = 8960


def _params(sem, vmem_bytes):
    limit = int(min(max(vmem_bytes * 3 // 2 + (4 << 20), 16 << 20), V7X_VMEM_CAP))
    return pltpu.CompilerParams(dimension_semantics=sem, vmem_limit_bytes=limit)


def _norm_mm_kernel(x_ref, g_ref, w_ref, o_ref, xn_ref):
    @pl.when(pl.program_id(1) == 0)
    def _():
        x = x_ref[...]
        ms = jnp.mean(x * x, axis=-1, keepdims=True)
        xn_ref[...] = (x * lax.rsqrt(ms + EPS) * g_ref[...]).astype(BF16)

    o_ref[...] = jnp.dot(xn_ref[...], w_ref[...], preferred_element_type=F32).astype(o_ref.dtype)


def _norm_mm_gate_kernel(x_ref, g_ref, w_ref, wg_ref, o_ref, og_ref, xn_ref):
    @pl.when(pl.program_id(1) == 0)
    def _():
        x = x_ref[...]
        ms = jnp.mean(x * x, axis=-1, keepdims=True)
        xn = (x * lax.rsqrt(ms + EPS) * g_ref[...]).astype(BF16)
        xn_ref[...] = xn
        og_ref[...] = jnp.dot(xn, wg_ref[...], preferred_element_type=F32)

    o_ref[...] = jnp.dot(xn_ref[...], w_ref[...], preferred_element_type=F32).astype(o_ref.dtype)


def norm_matmul(x, g, w, w_gate=None, *, tm, tn):
    m, k = x.shape
    n = w.shape[1]
    grid = (m // tm, n // tn)
    vmem = 2 * tm * k * 4 + tm * k * 2 + 2 * k * tn * 2 + 2 * tm * tn * 2 + 4 * tm * k
    x_spec = pl.BlockSpec((tm, k), lambda i, j: (i, 0))
    g_spec = pl.BlockSpec((1, k), lambda i, j: (0, 0))
    w_spec = pl.BlockSpec((k, tn), lambda i, j: (0, j))
    o_spec = pl.BlockSpec((tm, tn), lambda i, j: (i, j))
    scratch = [pltpu.VMEM((tm, k), BF16)]
    g2 = g.reshape(1, k).astype(F32)
    if w_gate is None:
        return pl.pallas_call(
            _norm_mm_kernel,
            grid=grid,
            in_specs=[x_spec, g_spec, w_spec],
            out_specs=o_spec,
            out_shape=jax.ShapeDtypeStruct((m, n), BF16),
            scratch_shapes=scratch,
            compiler_params=_params(("parallel", "arbitrary"), vmem),
            name="norm_matmul",
        )(x, g2, w)
    ng = w_gate.shape[1]
    return pl.pallas_call(
        _norm_mm_gate_kernel,
        grid=grid,
        in_specs=[x_spec, g_spec, w_spec, pl.BlockSpec((k, ng), lambda i, j: (0, 0))],
        out_specs=[o_spec, pl.BlockSpec((tm, ng), lambda i, j: (i, 0))],
        out_shape=[jax.ShapeDtypeStruct((m, n), BF16), jax.ShapeDtypeStruct((m, ng), F32)],
        scratch_shapes=scratch,
        compiler_params=_params(("parallel", "arbitrary"), vmem),
        name="norm_matmul_gate",
    )(x, g2, w, w_gate)


def _fourier_kernel(a_ref, bd_ref, dft_ref, o_ref, z_ref, *, row_chunk):
    s = a_ref.shape[1]
    w = a_ref.shape[2]

    @pl.when(pl.program_id(1) == 0)
    def _():
        for r in range(0, s, row_chunk):
            a = a_ref[0, r:r + row_chunk, :]
            zc = jnp.dot(a, bd_ref[...], preferred_element_type=F32)
            z_ref[r:r + row_chunk, :] = zc[:, :w].astype(BF16)
            z_ref[s + r:s + r + row_chunk, :] = zc[:, w:].astype(BF16)

    dft = dft_ref[...].reshape(o_ref.shape[1], 2 * s)
    o_ref[0] = jnp.dot(dft, z_ref[...], preferred_element_type=F32).astype(o_ref.dtype)


def fourier_tables(s):
    cg = FOURIER_GROUP_DIM
    jj = jnp.arange(cg, dtype=jnp.int32)
    ang_c = (2.0 * math.pi / cg) * ((jj[:, None] * jj[None, :]) % cg).astype(F32)
    eye_g = jnp.eye(BRANCH_WIDTH // cg, dtype=F32)
    bd_c = jnp.kron(eye_g, jnp.cos(ang_c)) * cg ** -0.5
    bd_s = jnp.kron(eye_g, jnp.sin(ang_c)) * cg ** -0.5
    bd = jnp.concatenate([bd_c, bd_s], axis=1).astype(BF16)
    n_lo = s // cg
    nn = jnp.arange(s, dtype=jnp.int32)
    ang_hi = (2.0 * math.pi / cg) * ((jj[:, None] * nn[None, :]) % cg).astype(F32)
    ll = jnp.arange(n_lo, dtype=jnp.int32)
    ang_lo = (2.0 * math.pi / s) * ((ll[:, None] * nn[None, :]) % s).astype(F32)
    ch, sh = jnp.cos(ang_hi), jnp.sin(ang_hi)
    cl, sl = jnp.cos(ang_lo), jnp.sin(ang_lo)
    scale = s ** -0.5
    a2 = jnp.concatenate([ch, -sh], axis=1)[:, None, :]
    c2 = jnp.concatenate([sh, ch], axis=1)[:, None, :]
    b2 = jnp.concatenate([cl, cl], axis=1)[None, :, :]
    d2 = jnp.concatenate([sl, sl], axis=1)[None, :, :]
    dft = ((a2 * b2 - c2 * d2) * scale).astype(BF16)
    return bd, dft


def fourier_mix(p3, bd, dft, *, tm):
    b, s, _ = p3.shape
    w = BRANCH_WIDTH
    vmem = 2 * s * w * 2 + 2 * tm * 2 * s * 2 + 2 * s * w * 2 + 2 * w * 2 * w * 2 + 2 * tm * w * 2 + 8 * tm * w
    return pl.pallas_call(
        functools.partial(_fourier_kernel, row_chunk=min(s, 512)),
        grid=(b, s // tm),
        in_specs=[
            pl.BlockSpec((1, s, w), lambda bi, i: (bi, 0, OFF_A // w)),
            pl.BlockSpec((w, 2 * w), lambda bi, i: (0, 0)),
            pl.BlockSpec((tm // dft.shape[1], dft.shape[1], 2 * s), lambda bi, i: (i, 0, 0)),
        ],
        out_specs=pl.BlockSpec((1, tm, w), lambda bi, i: (bi, i, 0)),
        out_shape=jax.ShapeDtypeStruct((b, s, w), BF16),
        scratch_shapes=[pltpu.VMEM((2 * s, w), BF16)],
        compiler_params=_params(("parallel", "arbitrary"), vmem),
        name="fourier",
    )(p3, bd, dft)


def rope_tables(s):
    rows = s // GRID_W
    row_id = jnp.repeat(jnp.arange(rows, dtype=F32), GRID_W)
    col_id = jnp.tile(jnp.arange(GRID_W, dtype=F32), rows)
    n_pairs = HEAD_DIM // 4
    inv_freq = ROPE_BASE ** (-jnp.arange(n_pairs, dtype=F32) / n_pairs)
    ang = jnp.concatenate([row_id[:, None] * inv_freq, col_id[:, None] * inv_freq], axis=-1)
    cos, sin = jnp.cos(ang), jnp.sin(ang)
    return jnp.concatenate([cos, cos] * 2, axis=-1), jnp.concatenate([-sin, sin] * 2, axis=-1)


def _norm_rope(x, g, seg, cos2, sin2):
    half = HEAD_DIM // 2
    x2 = x * x
    hi = x2.astype(BF16)
    lo = (x2 - hi.astype(F32)).astype(BF16)
    ms = jnp.dot(hi, seg, preferred_element_type=F32) + jnp.dot(lo, seg, preferred_element_type=F32)
    y = x * lax.rsqrt(ms + EPS) * g
    lane = lax.broadcasted_iota(jnp.int32, (x.shape[0], V7X_LANES), 1)
    first_half = (lane % HEAD_DIM) < half
    outs = []
    for cb in range(x.shape[1] // V7X_LANES):
        yb = y[:, cb * V7X_LANES:(cb + 1) * V7X_LANES]
        rot = jnp.where(first_half, pltpu.roll(yb, V7X_LANES - half, axis=1), pltpu.roll(yb, half, axis=1))
        outs.append(yb * cos2 + rot * sin2)
    return outs[0] if len(outs) == 1 else jnp.concatenate(outs, axis=-1)


def _gqa_prep_kernel(q_ref, kv_ref, gq_ref, gk_ref, seg_ref, cos_ref, sin_ref, qo_ref, ko_ref, vo_ref):
    d = HEAD_DIM
    nk = GQA_KV_HEADS * d
    cos2, sin2 = cos_ref[...], sin_ref[...]
    q = q_ref[...].astype(F32)
    kv = kv_ref[...].astype(F32)
    qo_ref[...] = _norm_rope(q, gq_ref[...], seg_ref[...], cos2, sin2).astype(BF16)
    ko_ref[...] = _norm_rope(kv[:, :nk], gk_ref[...], seg_ref[:nk, :nk], cos2, sin2).astype(BF16)
    v = kv[:, nk:]
    lane = lax.broadcasted_iota(jnp.int32, v.shape, 1)
    ones_col = jnp.where(lane == d, 1.0, 0.0)
    vo_ref[:, :nk] = jnp.where(lane < d, v, ones_col).astype(BF16)
    vo_ref[:, nk:] = jnp.where(lane < d, pltpu.roll(v, d, axis=1), ones_col).astype(BF16)


def gqa_prep(p2, qk_g, cos2, sin2, *, s, tm):
    m = p2.shape[0]
    nq = GQA_Q_HEADS * HEAD_DIM
    nkv = GQA_KV_HEADS * HEAD_DIM
    assert nkv == V7X_LANES
    tiles_per_seq = s // tm
    q_scale = HEAD_DIM ** -0.5 * LOG2E
    gq = jnp.tile(qk_g[0].astype(F32) * q_scale, GQA_Q_HEADS).reshape(1, nq)
    gk = jnp.tile(qk_g[1].astype(F32), GQA_KV_HEADS).reshape(1, nkv)
    seg = jnp.kron(jnp.eye(GQA_Q_HEADS, dtype=F32), jnp.full((HEAD_DIM, HEAD_DIM), 1.0 / HEAD_DIM, F32)).astype(BF16)
    return pl.pallas_call(
        _gqa_prep_kernel,
        grid=(m // tm,),
        in_specs=[
            pl.BlockSpec((tm, nq), lambda i: (i, OFF_BQ // nq)),
            pl.BlockSpec((tm, 2 * nkv), lambda i: (i, OFF_BKV // (2 * nkv))),
            pl.BlockSpec((1, nq), lambda i: (0, 0)),
            pl.BlockSpec((1, nkv), lambda i: (0, 0)),
            pl.BlockSpec((nq, nq), lambda i: (0, 0)),
            pl.BlockSpec((tm, V7X_LANES), lambda i: (i % tiles_per_seq, 0)),
            pl.BlockSpec((tm, V7X_LANES), lambda i: (i % tiles_per_seq, 0)),
        ],
        out_specs=[
            pl.BlockSpec((tm, nq), lambda i: (i, 0)),
            pl.BlockSpec((tm, nkv), lambda i: (i, 0)),
            pl.BlockSpec((tm, 2 * nkv), lambda i: (i, 0)),
        ],
        out_shape=[
            jax.ShapeDtypeStruct((m, nq), BF16),
            jax.ShapeDtypeStruct((m, nkv), BF16),
            jax.ShapeDtypeStruct((m, 2 * nkv), BF16),
        ],
        compiler_params=_params(("parallel",), 16 * tm * nq * 4),
        name="gqa_prep",
    )(p2, p2, gq, gk, seg, cos2, sin2)


def _gqa_attn_kernel(q_ref, kt_ref, v_ref, o_ref, q_sc, sc_buf, m_ref, acc_ref, *, tq, tk):
    d = HEAD_DIM
    grp = q_ref.shape[2] // d
    s = kt_ref.shape[2]
    n_tiles = s // tq
    for g in range(grp):
        q_sc[g] = q_ref[0, :, g * d:(g + 1) * d]

    def qk(tile, item):
        c, g = item
        row0 = pl.multiple_of(tile * tq, tq)
        return jnp.dot(q_sc[g, pl.ds(row0, tq), :], kt_ref[0, :, c * tk:(c + 1) * tk], preferred_element_type=F32)

    items = [(c, g) for c in range(s // tk) for g in range(grp)]
    sc_buf[...] = qk(0, items[0])

    def tile_body(i, carry):
        sc_next = sc_buf[...]
        for n, (c, g) in enumerate(items):
            rows = slice(g * tq, (g + 1) * tq)
            sc = sc_next
            if n + 1 < len(items):
                sc_next = qk(i, items[n + 1])
            else:
                sc_next = qk(jnp.minimum(i + 1, n_tiles - 1), items[0])
            v = v_ref[0, c * tk:(c + 1) * tk, :]
            row_max = jnp.max(sc, axis=-1, keepdims=True)
            if c == 0:
                m_new = jnp.broadcast_to(row_max, (tq, V7X_LANES))
            else:
                m_prev = m_ref[rows, :]
                m_new = jnp.maximum(m_prev, row_max)
            p = jnp.exp2((sc - pltpu.repeat(m_new, tk // V7X_LANES, axis=1)).astype(BF16))
            pv = jnp.dot(p, v, preferred_element_type=F32)
            if c == 0:
                acc_ref[rows, :] = pv
            else:
                acc_ref[rows, :] = acc_ref[rows, :] * jnp.exp2(m_prev - m_new) + pv
            m_ref[rows, :] = m_new
        sc_buf[...] = sc_next
        row0 = pl.multiple_of(i * tq, tq)
        for g in range(grp):
            acc = acc_ref[g * tq:(g + 1) * tq, :]
            o_ref[0, pl.ds(row0, tq), g * d:(g + 1) * d] = (acc[:, :d] / acc[:, d:d + 1]).astype(o_ref.dtype)
        return carry

    lax.fori_loop(0, n_tiles, tile_body, 0)


def gqa_attention(q3, kt3, v3, *, tq, tk):
    b, s, nq = q3.shape
    d = HEAD_DIM
    grp = GQA_Q_HEADS // GQA_KV_HEADS
    mrows = grp * tq
    vmem = (2 * (2 * s * grp * d * 2 + d * s * 2 + s * 128 * 2) + grp * s * 128 * 2 + tq * tk * 4
            + 2 * mrows * 128 * 4 + 12 * tq * tk * 4)
    return pl.pallas_call(
        functools.partial(_gqa_attn_kernel, tq=tq, tk=tk),
        grid=(b, GQA_KV_HEADS),
        in_specs=[
            pl.BlockSpec((1, s, grp * d), lambda bi, kv: (bi, 0, kv)),
            pl.BlockSpec((1, d, s), lambda bi, kv: (bi, kv, 0)),
            pl.BlockSpec((1, s, 2 * d), lambda bi, kv: (bi, 0, kv)),
        ],
        out_specs=pl.BlockSpec((1, s, grp * d), lambda bi, kv: (bi, 0, kv)),
        out_shape=jax.ShapeDtypeStruct((b, s, nq), BF16),
        scratch_shapes=[pltpu.VMEM((grp, s, d), BF16), pltpu.VMEM((tq, tk), F32),
                        pltpu.VMEM((mrows, V7X_LANES), F32), pltpu.VMEM((mrows, 2 * d), F32)],
        compiler_params=_params(("parallel", "parallel"), vmem),
        name="gqa_attn",
    )(q3, kt3, v3)


def _mlstm_step(chains, ms, q_ref, k_ref, v_ref, r_sc, cm_sc, b_sc, st_sc, h_sc):
    L = MLSTM_CHUNK
    dh = MLSTM_HEAD_DIM
    assert L == dh
    scale = dh ** -0.5
    row_i = lax.broadcasted_iota(jnp.int32, (L, L), 0)
    col_i = lax.broadcasted_iota(jnp.int32, (L, L), 1)

    def col(x_row):
        return jnp.transpose(jnp.broadcast_to(x_row, (L, L)))

    pre = []
    for (hh, direction, c), m in zip(chains, ms):
        off = pl.multiple_of(c * L, L)
        lanes = slice(hh * dh, (hh + 1) * dh)
        q = q_ref[0, pl.ds(off, L), lanes]
        k = k_ref[0, pl.ds(off, L), lanes]
        v = v_ref[0, pl.ds(off, L), lanes]
        r_row = r_sc[hh, direction, pl.ds(c, 1), :]
        cm_row = cm_sc[hh, direction, pl.ds(c, 1), :]
        b_row = b_sc[hh, direction, pl.ds(c, 1), :]
        rmax = jnp.max(r_row, axis=-1, keepdims=True)
        btot = b_row[:, L - 1:L] if direction == 0 else b_row[:, 0:1]
        cmat = jnp.maximum(m, col(cm_row))
        mask = (row_i >= col_i) if direction == 0 else (row_i <= col_i)
        c_last = jnp.maximum(m, rmax)
        w_state = jnp.exp(col(r_row) - c_last) * scale
        pre.append(dict(
            off=off, lanes=lanes, q=q, k=k, v=v,
            w_intra=jnp.where(mask, jnp.exp(r_row - cmat), 0.0) * scale,
            w_inter=jnp.exp(m - cmat),
            den_floor=jnp.exp(-(col(b_row) + cmat)),
            decay=jnp.exp(m - c_last),
            kv_w=jnp.concatenate([w_state * v.astype(F32), w_state], axis=-1).astype(BF16),
            m_new=btot + c_last,
        ))
    s_raw = [lax.dot_general(p["q"], p["k"], (((1,), (1,)), ((), ())), preferred_element_type=F32) for p in pre]
    states = [st_sc[hh, direction] for hh, direction, _ in chains]
    inter = [jnp.dot(p["q"], st.astype(BF16), preferred_element_type=F32) for p, st in zip(pre, states)]
    upd = [lax.dot_general(p["k"], p["kv_w"], (((0,), (0,)), ((), ())), preferred_element_type=F32) for p in pre]
    for n, ((hh, direction, _), p) in enumerate(zip(chains, pre)):
        st_sc[hh, direction] = p["decay"] * states[n] + upd[n]
        v_aug = jnp.concatenate([p["v"], jnp.ones((L, dh), BF16)], axis=-1)
        intra = jnp.dot((s_raw[n] * p["w_intra"]).astype(BF16), v_aug, preferred_element_type=F32)
        h_aug = jnp.concatenate([p["w_inter"], p["w_inter"]], axis=-1) * inter[n] + intra
        h_sc[pl.ds(p["off"], L), p["lanes"]] += h_aug[:, :dh] / jnp.maximum(jnp.abs(h_aug[:, dh:]), p["den_floor"])
    return [p["m_new"] for p in pre]


def _mlstm_kernel(bias_ref, q_ref, k_ref, v_ref, o_ref, gate_ref, g_ref, y_ref,
                  r_sc, cm_sc, b_sc, h_sc, st_sc):
    L = MLSTM_CHUNK
    dh = MLSTM_HEAD_DIM
    hpb = q_ref.shape[2] // dh
    head0 = pl.program_id(1) * hpb
    nc = q_ref.shape[1] // L
    lane = lax.broadcasted_iota(jnp.int32, (nc, L), 1)
    shifts = [1 << t for t in range(int(math.log2(L)))]
    for hh in range(hpb):
        for d in range(2):
            i_pre = gate_ref[0, 2 * d, hh] + bias_ref[2 * d, head0 + hh]
            f_pre = gate_ref[0, 2 * d + 1, hh] + bias_ref[2 * d + 1, head0 + hh]
            logf = jnp.minimum(f_pre, 0.0) - jnp.log1p(jnp.exp(-jnp.abs(f_pre)))
            bc = logf
            for sh in shifts:
                if d == 0:
                    bc = bc + jnp.where(lane >= sh, pltpu.roll(bc, sh, axis=1), 0.0)
                else:
                    bc = bc + jnp.where(lane < L - sh, pltpu.roll(bc, L - sh, axis=1), 0.0)
            r = i_pre - bc
            cm = r
            for sh in shifts:
                if d == 0:
                    cm = jnp.maximum(cm, jnp.where(lane >= sh, pltpu.roll(cm, sh, axis=1), -jnp.inf))
                else:
                    cm = jnp.maximum(cm, jnp.where(lane < L - sh, pltpu.roll(cm, L - sh, axis=1), -jnp.inf))
            r_sc[hh, d] = r
            cm_sc[hh, d] = cm
            b_sc[hh, d] = bc
    st_sc[...] = jnp.zeros(st_sc.shape, F32)
    h_sc[...] = jnp.zeros(h_sc.shape, F32)

    def body(c, ms):
        chains = [(hh, d, c if d == 0 else nc - 1 - c) for hh in range(hpb) for d in range(2)]
        return tuple(_mlstm_step(chains, ms, q_ref, k_ref, v_ref, r_sc, cm_sc, b_sc, st_sc, h_sc))

    lax.fori_loop(0, nc, body, tuple(jnp.zeros((1, 1), F32) for _ in range(2 * hpb)))
    for hh in range(hpb):
        lanes = slice(hh * dh, (hh + 1) * dh)
        hsum = h_sc[:, lanes]
        ms = jnp.mean(hsum * hsum, axis=-1, keepdims=True)
        y = hsum * lax.rsqrt(ms + EPS) * g_ref[:, lanes]
        y_ref[0, :, lanes] = (jax.nn.sigmoid(o_ref[0, :, lanes].astype(F32)) * y).astype(y_ref.dtype)


def mlstm_branch(p3, gates5, gate_bias, norm_g, *, heads_per_block):
    b, s, _ = p3.shape
    L = MLSTM_CHUNK
    hpb = heads_per_block
    wb = hpb * MLSTM_HEAD_DIM
    nc = s // L
    blk = lambda off: pl.BlockSpec((1, s, wb), lambda bi, h, off=off: (bi, 0, off // wb + h))
    vmem = 2 * 5 * s * wb * 2 + s * wb * 4 + 6 * hpb * nc * L * 4 + 4 * hpb * wb * wb * 4 + 3 * s * wb * 4
    return pl.pallas_call(
        _mlstm_kernel,
        grid=(b, MLSTM_HEADS // hpb),
        in_specs=[
            pl.BlockSpec(memory_space=pltpu.SMEM),
            blk(OFF_CQ), blk(OFF_CK), blk(OFF_CV), blk(OFF_CO),
            pl.BlockSpec((1, 4, hpb, nc, L), lambda bi, h: (bi, 0, h, 0, 0)),
            pl.BlockSpec((1, wb), lambda bi, h: (0, h)),
        ],
        out_specs=pl.BlockSpec((1, s, wb), lambda bi, h: (bi, 0, h)),
        out_shape=jax.ShapeDtypeStruct((b, s, BRANCH_WIDTH), BF16),
        scratch_shapes=[
            pltpu.VMEM((hpb, 2, nc, L), F32), pltpu.VMEM((hpb, 2, nc, L), F32), pltpu.VMEM((hpb, 2, nc, L), F32),
            pltpu.VMEM((s, wb), F32),
            pltpu.VMEM((hpb, 2, MLSTM_HEAD_DIM, 2 * MLSTM_HEAD_DIM), F32),
        ],
        compiler_params=_params(("parallel", "parallel"), vmem),
        name="mlstm",
    )(gate_bias.astype(F32), p3, p3, p3, p3, gates5, norm_g.reshape(1, BRANCH_WIDTH).astype(F32))


def _rel_bucket(rel):
    half = REL_BUCKETS // 2
    max_exact = half // 2
    ret = jnp.where(rel > 0, half, 0)
    n = jnp.abs(rel)
    nf = jnp.maximum(n, 1).astype(F32)
    large = max_exact + (jnp.log(nf / max_exact) / math.log(REL_MAX_DIST / max_exact) * (half - max_exact)).astype(jnp.int32)
    large = jnp.minimum(large, half - 1)
    return ret + jnp.where(n < max_exact, n, large)


def diff_bias_tiles(rel_bias, t):
    assert t >= REL_MAX_DIST
    k = jnp.arange(2 * t, dtype=jnp.int32)
    rel = jnp.arange(-2, 3, dtype=jnp.int32)[:, None] * t + jnp.where(k < t, k, k - 2 * t)[None, :]
    onehot = (_rel_bucket(rel)[:, :, None] == jnp.arange(REL_BUCKETS, dtype=jnp.int32)).astype(F32)
    period = jnp.einsum('dkb,bh->hdk', onehot, rel_bias.astype(F32) * LOG2E, precision=lax.Precision.HIGHEST)
    far = period[:, 0::4, 0]
    return period, far


def _diff_attn_kernel(far_ref, q_ref, kt_ref, v_ref, period_ref, lam_ref, g_ref, o_ref,
                      q_sc, kt_sc, vaug_sc, bias_sc, sc_buf, m_ref, acc_ref, *, t, lam_init):
    dq = DIFF_QK_DIM
    dv = DIFF_V_DIM
    s = kt_ref.shape[2]
    nt = s // t
    head = pl.program_id(1)
    rb = t // 2
    assert nt % 2 == 0

    kt_sc[:, :s] = kt_ref[0]
    kt_sc[:, s:] = kt_ref[0, :, :t]
    vaug_sc[:s, :dv] = v_ref[0]
    vaug_sc[s:, :dv] = v_ref[0, :t, :]
    lane = lax.broadcasted_iota(jnp.int32, (s + t, dv), 1)
    vaug_sc[:, dv:] = jnp.where(lane == 0, 1.0, 0.0).astype(BF16)
    for dl in range(period_ref.shape[1]):
        full = jnp.broadcast_to(period_ref[0, dl:dl + 1, :], (t, 2 * t))
        bias_sc[dl] = pltpu.roll(full, 0, axis=1, stride=1, stride_axis=0)[:, :t]
    q_scale = dq ** -0.5 * LOG2E
    q_all = (q_ref[0].astype(F32) * q_scale).astype(BF16)
    q_sc[0] = q_all[:, :dq]
    q_sc[1] = q_all[:, dq:]
    lp = lam_ref[...]
    lam = (jnp.exp(jnp.sum(lp[0:1] * lp[1:2], axis=-1, keepdims=True))
           - jnp.exp(jnp.sum(lp[2:3] * lp[3:4], axis=-1, keepdims=True)) + lam_init)

    def key_tile(i, delta):
        j = lax.rem(i + delta, nt)
        return j, pl.multiple_of(j * t, t)

    def qk(i, item):
        pair, mp, hf = item
        row0 = pl.multiple_of(i * t + hf * rb, rb)
        _, off = key_tile(i, 2 * pair)
        return jnp.dot(q_sc[mp, pl.ds(row0, rb), :], kt_sc[mp * dq:(mp + 1) * dq, pl.ds(off, 2 * t)],
                       preferred_element_type=F32)

    items = [(pair, mp, hf) for pair in range(nt // 2) for mp in range(2) for hf in range(2)]
    sc_buf[...] = qk(0, items[0])

    def tile_body(i, carry):
        sc_next = sc_buf[...]
        for n, (pair, mp, hf) in enumerate(items):
            sc = sc_next
            if n + 1 < len(items):
                sc_next = qk(i, items[n + 1])
            else:
                sc_next = qk(jnp.minimum(i + 1, nt - 1), items[0])
            rows = slice(hf * rb, (hf + 1) * rb)
            scs = [sc[:, :t], sc[:, t:]]
            _, off = key_tile(i, 2 * pair)
            shifts, row_max = [], None
            for piece, delta in enumerate((2 * pair, 2 * pair + 1)):
                j, _ = key_tile(i, delta)
                if delta in (0, 1, nt - 1):
                    tile = 2 if delta == 0 else jnp.clip(j - i, -2, 2) + 2
                    scs[piece] = scs[piece] + bias_sc[tile, rows, :]
                    shifts.append(None)
                    rm = jnp.max(scs[piece], axis=-1, keepdims=True)
                else:
                    shifts.append(jnp.where(j > i, far_ref[head, 1], far_ref[head, 0]))
                    rm = jnp.max(scs[piece], axis=-1, keepdims=True) + shifts[-1]
                row_max = rm if row_max is None else jnp.maximum(row_max, rm)
            if pair == 0:
                m_new = jnp.broadcast_to(row_max, (rb, V7X_LANES))
            else:
                m_prev = m_ref[mp, rows, :]
                m_new = jnp.maximum(m_prev, row_max)
            ps = []
            for piece in range(2):
                m_sub = m_new if shifts[piece] is None else m_new - shifts[piece]
                ps.append(jnp.exp2((scs[piece] - pltpu.repeat(m_sub, t // V7X_LANES, axis=1)).astype(BF16)))
            pv = jnp.dot(jnp.concatenate(ps, axis=1), vaug_sc[pl.ds(off, 2 * t), :], preferred_element_type=F32)
            if pair == 0:
                acc_ref[mp, rows, :] = pv
            else:
                alpha = jnp.exp2(m_prev - m_new)
                acc_ref[mp, rows, :] = acc_ref[mp, rows, :] * pltpu.repeat(alpha, 2 * dv // V7X_LANES, axis=1) + pv
            m_ref[mp, rows, :] = m_new
        sc_buf[...] = sc_next
        o0 = acc_ref[0, :, :dv] / acc_ref[0, :, dv:dv + 1]
        o1 = acc_ref[1, :, :dv] / acc_ref[1, :, dv:dv + 1]
        o = o0 - lam * o1
        ms = jnp.mean(o * o, axis=-1, keepdims=True)
        row0 = pl.multiple_of(i * t, t)
        o_ref[0, pl.ds(row0, t), :] = (o * lax.rsqrt(ms + EPS) * g_ref[...] * (1.0 - lam_init)).astype(o_ref.dtype)
        return carry

    lax.fori_loop(0, nt, tile_body, 0)


def diff_attention(p3, kt3, bias_period, bias_far, lam_params, sub_g, *, t, layer_number):
    b, s, _ = p3.shape
    dv = DIFF_V_DIM
    n_off = bias_period.shape[1]
    assert s // t >= 4, "tiles 2 .. s/t-2 steps away from the query tile must all be beyond REL_MAX_DIST"
    lam_init = 0.8 - 0.6 * math.exp(-0.3 * (layer_number - 1))
    vmem = (2 * (s * 128 * 2 + 128 * s * 2 + s * dv * 2 + s * dv * 2) + n_off * t * t * 4 + s * 2 * dv * 2
            + 2 * s * 128 * 2 + t * t * 4 + 2 * t * 128 * 4 * 3 + 8 * t * t * 4)
    return pl.pallas_call(
        functools.partial(_diff_attn_kernel, t=t, lam_init=lam_init),
        grid=(b, DIFF_HEADS),
        in_specs=[
            pl.BlockSpec(memory_space=pltpu.SMEM),
            pl.BlockSpec((1, s, 128), lambda bi, h: (bi, 0, OFF_DQ // 128 + h)),
            pl.BlockSpec((1, 128, s), lambda bi, h: (bi, h, 0)),
            pl.BlockSpec((1, s, dv), lambda bi, h: (bi, 0, OFF_DV // dv + h)),
            pl.BlockSpec((1, n_off, 2 * t), lambda bi, h: (h, 0, 0)),
            pl.BlockSpec((4, DIFF_QK_DIM), lambda bi, h: (0, 0)),
            pl.BlockSpec((1, dv), lambda bi, h: (0, 0)),
        ],
        out_specs=pl.BlockSpec((1, s, dv), lambda bi, h: (bi, 0, h)),
        out_shape=jax.ShapeDtypeStruct((b, s, BRANCH_WIDTH), BF16),
        scratch_shapes=[pltpu.VMEM((2, s, DIFF_QK_DIM), BF16), pltpu.VMEM((2 * DIFF_QK_DIM, s + t), BF16),
                        pltpu.VMEM((s + t, 2 * dv), BF16),
                        pltpu.VMEM((n_off, t, t), F32), pltpu.VMEM((t // 2, 2 * t), F32),
                        pltpu.VMEM((2, t, V7X_LANES), F32), pltpu.VMEM((2, t, 2 * dv), F32)],
        compiler_params=_params(("parallel", "parallel"), vmem),
        name="diff_attn",
    )(bias_far, p3, kt3, p3, bias_period, lam_params.astype(F32), sub_g.reshape(1, dv).astype(F32))


def _merge_kernel(ya_ref, yb_ref, yc_ref, yd_ref, gate_ref, wb_ref, wo_ref, x_ref, o_ref):
    dm = x_ref.shape[1]
    merged = None
    for n, y_ref in enumerate((ya_ref, yb_ref, yc_ref, yd_ref)):
        br = jnp.dot(y_ref[...], wb_ref[n], preferred_element_type=F32)
        term = jax.nn.sigmoid(gate_ref[:, n * dm:(n + 1) * dm].astype(F32)) * br
        merged = term if merged is None else merged + term
    o_ref[...] = x_ref[...] + jnp.dot(merged.astype(BF16), wo_ref[...], preferred_element_type=F32)


def merge_branches(ys, p2, w_branch, w_out, x2, *, tm):
    m, dm = x2.shape
    w = BRANCH_WIDTH
    y_spec = pl.BlockSpec((tm, w), lambda i: (i, 0))
    vmem = 2 * (4 * tm * w * 2 + tm * 4 * dm * 2 + 4 * w * dm * 2 + dm * dm * 2 + 2 * tm * dm * 4) + 6 * tm * dm * 4
    return pl.pallas_call(
        _merge_kernel,
        grid=(m // tm,),
        in_specs=[
            y_spec, y_spec, y_spec, y_spec,
            pl.BlockSpec((tm, N_BRANCHES * dm), lambda i: (i, OFF_GATES // (N_BRANCHES * dm))),
            pl.BlockSpec((N_BRANCHES, w, dm), lambda i: (0, 0, 0)),
            pl.BlockSpec((dm, dm), lambda i: (0, 0)),
            pl.BlockSpec((tm, dm), lambda i: (i, 0)),
        ],
        out_specs=pl.BlockSpec((tm, dm), lambda i: (i, 0)),
        out_shape=jax.ShapeDtypeStruct((m, dm), F32),
        compiler_params=_params(("parallel",), vmem),
        name="merge",
    )(*ys, p2, w_branch, w_out, x2)


HALO_ROWS = 16


def _ffn_up_kernel(x_ref, xp_ref, xn_ref, g_ref, wa_ref, wl_ref, cw_ref, cb_ref, o_ref, xe_sc,
                   *, tiles_per_seq, col_chunk):
    tm, f = o_ref.shape
    h = HALO_ROWS
    pos = pl.program_id(0) % tiles_per_seq

    def norm(x):
        ms = jnp.mean(x * x, axis=-1, keepdims=True)
        return x * lax.rsqrt(ms + EPS) * g_ref[...]

    xe_sc[:h] = jnp.where(pos == 0, 0.0, norm(xp_ref[...])).astype(BF16)
    xe_sc[h:h + tm] = norm(x_ref[...]).astype(BF16)
    xe_sc[h + tm:] = jnp.where(pos == tiles_per_seq - 1, 0.0, norm(xn_ref[...])).astype(BF16)
    k0 = math.sqrt(2.0 / math.pi)
    bounds = list(range(0, f, col_chunk)) + [f]
    chunks = [slice(lo, hi) for lo, hi in zip(bounds[:-1], bounds[1:])]

    def project(cols):
        a_ext = jnp.dot(xe_sc[...], wa_ref[:, cols], preferred_element_type=F32)
        lin = jnp.dot(xe_sc[h:h + tm, :], wl_ref[:, cols], preferred_element_type=F32)
        return a_ext, lin

    nxt = project(chunks[0])
    for n, cols in enumerate(chunks):
        a_ext, lin = nxt
        if n + 1 < len(chunks):
            nxt = project(chunks[n + 1])
        cw = cw_ref[:, cols]
        c = (a_ext[h - 1:h - 1 + tm] * cw[0:1] + a_ext[h:h + tm] * cw[1:2] + a_ext[h + 1:h + 1 + tm] * cw[2:3]
             + cb_ref[:, cols])
        t = jnp.tanh(c * ((c * c) * (k0 * 0.044715) + k0))
        o_ref[:, cols] = ((c + c * t) * lin).astype(o_ref.dtype)


def ffn_up(x2, g, w_a, w_lin, conv_w, conv_b, *, s, tm):
    m, dm = x2.shape
    f = w_a.shape[1]
    tiles_per_seq = s // tm
    hb = tm // HALO_ROWS
    n_halo = m // HALO_ROWS
    vmem = 2 * (tm * dm * 4 + tm * f * 2) + 2 * dm * f * 2 + (tm + 2 * HALO_ROWS) * dm * 2 + 12 * tm * 512 * 4
    return pl.pallas_call(
        functools.partial(_ffn_up_kernel, tiles_per_seq=tiles_per_seq, col_chunk=512),
        grid=(m // tm,),
        in_specs=[
            pl.BlockSpec((tm, dm), lambda i: (i, 0)),
            pl.BlockSpec((HALO_ROWS, dm), lambda i: (jnp.maximum(i * hb - 1, 0), 0)),
            pl.BlockSpec((HALO_ROWS, dm), lambda i: (jnp.minimum((i + 1) * hb, n_halo - 1), 0)),
            pl.BlockSpec((1, dm), lambda i: (0, 0)),
            pl.BlockSpec((dm, f), lambda i: (0, 0), pipeline_mode=pl.Buffered(1)),
            pl.BlockSpec((dm, f), lambda i: (0, 0), pipeline_mode=pl.Buffered(1)),
            pl.BlockSpec((CONV_W, f), lambda i: (0, 0)),
            pl.BlockSpec((1, f), lambda i: (0, 0)),
        ],
        out_specs=pl.BlockSpec((tm, f), lambda i: (i, 0)),
        out_shape=jax.ShapeDtypeStruct((m, f), BF16),
        scratch_shapes=[pltpu.VMEM((tm + 2 * HALO_ROWS, dm), BF16)],
        compiler_params=_params(("parallel",), vmem),
        name="ffn_up",
    )(x2, x2, x2, g.reshape(1, dm).astype(F32), w_a, w_lin, conv_w.astype(F32), conv_b.reshape(1, f).astype(F32))


def _ffn_down_kernel(h_ref, wd_ref, x_ref, gf_ref, o_ref, *, final_norm):
    y = x_ref[...] + jnp.dot(h_ref[...], wd_ref[...], preferred_element_type=F32)
    if final_norm:
        ms = jnp.mean(y * y, axis=-1, keepdims=True)
        y = y * lax.rsqrt(ms + EPS) * gf_ref[...]
    o_ref[...] = y


def ffn_down(hmid, w_down, x2, final_g, *, tm, final_norm):
    m, dm = x2.shape
    f = w_down.shape[0]
    vmem = 2 * (tm * f * 2 + f * dm * 2 + 2 * tm * dm * 4) + 4 * tm * dm * 4
    return pl.pallas_call(
        functools.partial(_ffn_down_kernel, final_norm=final_norm),
        grid=(m // tm,),
        in_specs=[
            pl.BlockSpec((tm, f), lambda i: (i, 0)),
            pl.BlockSpec((f, dm), lambda i: (0, 0)),
            pl.BlockSpec((tm, dm), lambda i: (i, 0)),
            pl.BlockSpec((1, dm), lambda i: (0, 0)),
        ],
        out_specs=pl.BlockSpec((tm, dm), lambda i: (i, 0)),
        out_shape=jax.ShapeDtypeStruct((m, dm), F32),
        compiler_params=_params(("parallel",), vmem),
        name="ffn_down",
    )(hmid, w_down, x2, final_g.reshape(1, dm).astype(F32))


def _arrange_w_in(w):
    widths = (512, 512, 128, 128, 512, 512, 512, 512, 16, 512, 512, 512, 4096)
    offs = [0]
    for wd in widths:
        offs.append(offs[-1] + wd)
    seg = lambda n: w[:, offs[n]:offs[n + 1]]
    a, bq, bk, bv, cq, ck, cv, co, cgate, dq, dk, dv, gates = (seg(n) for n in range(len(widths)))
    main = jnp.concatenate([gates, a, bq, cq, ck, cv, co, dq, dk, dv, bk, bv], axis=1).astype(BF16)
    gate = jnp.pad(cgate, ((0, 0), (0, V7X_LANES - cgate.shape[1]))).astype(BF16)
    return main, gate


def kernel(x, norm_mix_g, w_in, mlstm_gate_bias, qk_norm_g, mlstm_norm_g, diff_lambda, diff_norm_g, rel_bias,
           w_branch, w_out, norm_ffn_g, w_up, conv_w, conv_b, w_down, final_norm_g):
    b, s, dm = x.shape
    depth = w_in.shape[0]
    m = b * s
    d_ff = w_down.shape[1]
    L = MLSTM_CHUNK
    t_diff = 512

    bd, dft = fourier_tables(s)
    cos2, sin2 = rope_tables(s)
    bias_tiles, bias_far = diff_bias_tiles(rel_bias, t_diff)

    x2 = x.reshape(m, dm)
    for layer in range(depth):
        w_main, w_gate = _arrange_w_in(w_in[layer])
        p2, cgate = norm_matmul(x2, norm_mix_g[layer], w_main, w_gate, tm=1024, tn=1280)
        p3 = p2.reshape(b, s, P_WIDTH)

        y_a = fourier_mix(p3, bd, dft, tm=512)

        q_b, k_b, v_b = gqa_prep(p2, qk_norm_g[layer], cos2, sin2, s=s, tm=512)
        kt_b = jnp.swapaxes(k_b.reshape(b, s, GQA_KV_HEADS * HEAD_DIM), 1, 2)
        y_b = gqa_attention(q_b.reshape(b, s, -1), kt_b, v_b.reshape(b, s, -1), tq=256, tk=1024)

        gates5 = jnp.transpose(cgate[:, :4 * MLSTM_HEADS].reshape(b, s, 4, MLSTM_HEADS), (0, 2, 3, 1))
        gates5 = gates5.reshape(b, 4, MLSTM_HEADS, s // L, L)
        y_c = mlstm_branch(p3, gates5, mlstm_gate_bias[layer], mlstm_norm_g[layer], heads_per_block=2)

        kt_d = jnp.swapaxes(p3[:, :, OFF_DK:OFF_DK + BRANCH_WIDTH], 1, 2)
        y_d = diff_attention(p3, kt_d, bias_tiles, bias_far, diff_lambda[layer], diff_norm_g[layer],
                             t=t_diff, layer_number=layer + 1)

        ys = [y.reshape(m, BRANCH_WIDTH) for y in (y_a, y_b, y_c, y_d)]
        x2 = merge_branches(ys, p2, w_branch[layer].astype(BF16), w_out[layer].astype(BF16), x2, tm=512)

        w_a = w_up[layer][:, :d_ff].astype(BF16)
        w_lin = (w_up[layer][:, d_ff:] * 0.5).astype(BF16)
        hmid = ffn_up(x2, norm_ffn_g[layer], w_a, w_lin, conv_w[layer], conv_b[layer], s=s, tm=1024)
        x2 = ffn_down(hmid, w_down[layer].astype(BF16), x2, final_norm_g, tm=512, final_norm=(layer == depth - 1))
    return x2.reshape(b, s, dm)
```

```python
import functools
import math

import jax
import jax.numpy as jnp
from jax import lax
from jax.experimental import pallas as pl
from jax.experimental.pallas import tpu as pltpu

F32 = jnp.float32
BF16 = jnp.bfloat16

GRID_W = 64
HEAD_DIM = 64
BRANCH_WIDTH = 512
N_BRANCHES = 4
FOURIER_GROUP_DIM = 64
GQA_Q_HEADS = 8
GQA_KV_HEADS = 2
MLSTM_HEADS = 4
MLSTM_HEAD_DIM = 128
MLSTM_CHUNK = 128
DIFF_HEADS = 4
DIFF_QK_DIM = 64
DIFF_V_DIM = 128
REL_BUCKETS = 32
REL_MAX_DIST = 128
CONV_W = 3
ROPE_BASE = 10000.0
EPS = 1e-6
LOG2E = math.log2(math.e)

V7X_LANES = 128
V7X_VMEM_BYTES = 64 * 1024 * 1024
V7X_VMEM_CAP = V7X_VMEM_BYTES - 8 * 1024 * 1024

OFF_GATES = 0
OFF_A = 4096
OFF_BQ = 4608
OFF_CQ = 5120
OFF_CK = 5632
OFF_CV = 6144
OFF_CO = 6656
OFF_DQ = 7168
OFF_DK = 7680
OFF_DV = 8192
OFF_BKV = 8704
P_WIDTH = 8960


def _params(sem, vmem_bytes):
    limit = int(min(max(vmem_bytes * 3 // 2 + (4 << 20), 16 << 20), V7X_VMEM_CAP))
    return pltpu.CompilerParams(dimension_semantics=sem, vmem_limit_bytes=limit)


def _norm_mm_kernel(x_ref, g_ref, w_ref, o_ref, xn_ref):
    @pl.when(pl.program_id(1) == 0)
    def _():
        x = x_ref[...]
        ms = jnp.mean(x * x, axis=-1, keepdims=True)
        xn_ref[...] = (x * lax.rsqrt(ms + EPS) * g_ref[...]).astype(BF16)

    o_ref[...] = jnp.dot(xn_ref[...], w_ref[...], preferred_element_type=F32).astype(o_ref.dtype)


def _norm_mm_gate_kernel(x_ref, g_ref, w_ref, wg_ref, o_ref, og_ref, xn_ref):
    @pl.when(pl.program_id(1) == 0)
    def _():
        x = x_ref[...]
        ms = jnp.mean(x * x, axis=-1, keepdims=True)
        xn = (x * lax.rsqrt(ms + EPS) * g_ref[...]).astype(BF16)
        xn_ref[...] = xn
        og_ref[...] = jnp.dot(xn, wg_ref[...], preferred_element_type=F32)

    o_ref[...] = jnp.dot(xn_ref[...], w_ref[...], preferred_element_type=F32).astype(o_ref.dtype)


def norm_matmul(x, g, w, w_gate=None, *, tm, tn):
    m, k = x.shape
    n = w.shape[1]
    grid = (m // tm, n // tn)
    vmem = 2 * tm * k * 4 + tm * k * 2 + 2 * k * tn * 2 + 2 * tm * tn * 2 + 4 * tm * k
    x_spec = pl.BlockSpec((tm, k), lambda i, j: (i, 0))
    g_spec = pl.BlockSpec((1, k), lambda i, j: (0, 0))
    w_spec = pl.BlockSpec((k, tn), lambda i, j: (0, j))
    o_spec = pl.BlockSpec((tm, tn), lambda i, j: (i, j))
    scratch = [pltpu.VMEM((tm, k), BF16)]
    g2 = g.reshape(1, k).astype(F32)
    if w_gate is None:
        return pl.pallas_call(
            _norm_mm_kernel,
            grid=grid,
            in_specs=[x_spec, g_spec, w_spec],
            out_specs=o_spec,
            out_shape=jax.ShapeDtypeStruct((m, n), BF16),
            scratch_shapes=scratch,
            compiler_params=_params(("parallel", "arbitrary"), vmem),
            name="norm_matmul",
        )(x, g2, w)
    ng = w_gate.shape[1]
    return pl.pallas_call(
        _norm_mm_gate_kernel,
        grid=grid,
        in_specs=[x_spec, g_spec, w_spec, pl.BlockSpec((k, ng), lambda i, j: (0, 0))],
        out_specs=[o_spec, pl.BlockSpec((tm, ng), lambda i, j: (i, 0))],
        out_shape=[jax.ShapeDtypeStruct((m, n), BF16), jax.ShapeDtypeStruct((m, ng), F32)],
        scratch_shapes=scratch,
        compiler_params=_params(("parallel", "arbitrary"), vmem),
        name="norm_matmul_gate",
    )(x, g2, w, w_gate)


def _channel_dft_kernel(a_ref, bd_ref, o_ref):
    o_ref[...] = jnp.dot(a_ref[...], bd_ref[...], preferred_element_type=F32).astype(o_ref.dtype)


def _fft_stage1_kernel(z_ref, f1_ref, twc_ref, tws_ref, o_ref):
    n1 = f1_ref.shape[1]
    nb = o_ref.shape[2]
    w = o_ref.shape[4]
    r = jnp.dot(f1_ref[...], z_ref[0], preferred_element_type=F32)
    for j in range(nb):
        rj = r[:, j * 2 * w:(j + 1) * 2 * w]
        tr = rj[:n1, :w] - rj[n1:, w:]
        ti = rj[:n1, w:] + rj[n1:, :w]
        c = pltpu.repeat(twc_ref[j], w // V7X_LANES, axis=1)
        s = pltpu.repeat(tws_ref[j], w // V7X_LANES, axis=1)
        o_ref[0, 0, j] = (tr * c + ti * s).astype(o_ref.dtype)
        o_ref[0, 1, j] = (ti * c - tr * s).astype(o_ref.dtype)


def _fft_stage2_kernel(t_ref, f2_ref, o_ref):
    o_ref[0] = jnp.dot(f2_ref[...], t_ref[0], preferred_element_type=F32).astype(o_ref.dtype)


def _dft_cos_sin(n):
    j = jnp.arange(n, dtype=jnp.int32)
    ang = (2.0 * math.pi / n) * ((j[:, None] * j[None, :]) % n).astype(F32)
    return jnp.cos(ang), jnp.sin(ang)


def fourier_tables(s, n1):
    cg = FOURIER_GROUP_DIM
    n2 = s // n1
    cc, sc = _dft_cos_sin(cg)
    eye_g = jnp.eye(BRANCH_WIDTH // cg, dtype=F32)
    bd = (jnp.concatenate([jnp.kron(eye_g, cc), -jnp.kron(eye_g, sc)], axis=1) * cg ** -0.5).astype(BF16)
    c1, s1 = _dft_cos_sin(n1)
    f1 = (jnp.concatenate([c1, -s1], axis=0) * n1 ** -0.5).astype(BF16)
    c2, s2 = _dft_cos_sin(n2)
    f2 = (jnp.concatenate([c2, s2], axis=1) * n2 ** -0.5).astype(BF16)
    ang = (2.0 * math.pi / s) * (jnp.arange(n2, dtype=jnp.int32)[:, None] * jnp.arange(n1, dtype=jnp.int32)[None, :]).astype(F32)
    twc = jnp.broadcast_to(jnp.cos(ang)[:, :, None], (n2, n1, V7X_LANES))
    tws = jnp.broadcast_to(jnp.sin(ang)[:, :, None], (n2, n1, V7X_LANES))
    return bd, f1, f2, twc, tws


def fourier_mix(p2, tables, *, b, s, tm, nb, tn):
    bd, f1, f2, twc, tws = tables
    m = p2.shape[0]
    w = BRANCH_WIDTH
    n1 = f1.shape[1]
    n2 = s // n1
    z = pl.pallas_call(
        _channel_dft_kernel,
        grid=(m // tm,),
        in_specs=[pl.BlockSpec((tm, w), lambda i: (i, OFF_A // w)), pl.BlockSpec((w, 2 * w), lambda i: (0, 0))],
        out_specs=pl.BlockSpec((tm, 2 * w), lambda i: (i, 0)),
        out_shape=jax.ShapeDtypeStruct((m, 2 * w), BF16),
        compiler_params=_params(("parallel",), 2 * (tm * w * 2 + w * 2 * w * 2 + tm * 2 * w * 2) + tm * 2 * w * 4),
        name="fourier_channel",
    )(p2, bd)
    t = pl.pallas_call(
        _fft_stage1_kernel,
        grid=(b, n2 // nb),
        in_specs=[
            pl.BlockSpec((1, n1, nb * 2 * w), lambda bi, j: (bi, 0, j)),
            pl.BlockSpec((2 * n1, n1), lambda bi, j: (0, 0)),
            pl.BlockSpec((nb, n1, V7X_LANES), lambda bi, j: (j, 0, 0)),
            pl.BlockSpec((nb, n1, V7X_LANES), lambda bi, j: (j, 0, 0)),
        ],
        out_specs=pl.BlockSpec((1, 2, nb, n1, w), lambda bi, j: (bi, 0, j, 0, 0)),
        out_shape=jax.ShapeDtypeStruct((b, 2, n2, n1, w), BF16),
        compiler_params=_params(("parallel", "parallel"), 2 * (n1 * nb * 2 * w * 2 + 2 * nb * n1 * w * 2) + 3 * 2 * n1 * nb * 2 * w * 4),
        name="fourier_stage1",
    )(z.reshape(b, n1, n2 * 2 * w), f1, twc, tws)
    y = pl.pallas_call(
        _fft_stage2_kernel,
        grid=(b, n1 * w // tn),
        in_specs=[
            pl.BlockSpec((1, 2 * n2, tn), lambda bi, j: (bi, 0, j)),
            pl.BlockSpec((n2, 2 * n2), lambda bi, j: (0, 0)),
        ],
        out_specs=pl.BlockSpec((1, n2, tn), lambda bi, j: (bi, 0, j)),
        out_shape=jax.ShapeDtypeStruct((b, n2, n1 * w), BF16),
        compiler_params=_params(("parallel", "parallel"), 2 * (2 * n2 * tn * 2 + n2 * tn * 2) + n2 * tn * 4),
        name="fourier_stage2",
    )(t.reshape(b, 2 * n2, n1 * w), f2)
    return y.reshape(b, s, w)


def rope_tables(s):
    rows = s // GRID_W
    row_id = jnp.repeat(jnp.arange(rows, dtype=F32), GRID_W)
    col_id = jnp.tile(jnp.arange(GRID_W, dtype=F32), rows)
    n_pairs = HEAD_DIM // 4
    inv_freq = ROPE_BASE ** (-jnp.arange(n_pairs, dtype=F32) / n_pairs)
    ang = jnp.concatenate([row_id[:, None] * inv_freq, col_id[:, None] * inv_freq], axis=-1)
    cos, sin = jnp.cos(ang), jnp.sin(ang)
    return jnp.concatenate([cos, cos] * 2, axis=-1), jnp.concatenate([-sin, sin] * 2, axis=-1)


def _norm_rope(x, g, seg, cos2, sin2):
    half = HEAD_DIM // 2
    x2 = x * x
    hi = x2.astype(BF16)
    lo = (x2 - hi.astype(F32)).astype(BF16)
    ms = jnp.dot(hi, seg, preferred_element_type=F32) + jnp.dot(lo, seg, preferred_element_type=F32)
    y = x * lax.rsqrt(ms + EPS) * g
    lane = lax.broadcasted_iota(jnp.int32, (x.shape[0], V7X_LANES), 1)
    first_half = (lane % HEAD_DIM) < half
    outs = []
    for cb in range(x.shape[1] // V7X_LANES):
        yb = y[:, cb * V7X_LANES:(cb + 1) * V7X_LANES]
        rot = jnp.where(first_half, pltpu.roll(yb, V7X_LANES - half, axis=1), pltpu.roll(yb, half, axis=1))
        outs.append(yb * cos2 + rot * sin2)
    return outs[0] if len(outs) == 1 else jnp.concatenate(outs, axis=-1)


def _gqa_prep_kernel(q_ref, kv_ref, gq_ref, gk_ref, seg_ref, cos_ref, sin_ref, qo_ref, ko_ref, vo_ref):
    d = HEAD_DIM
    nk = GQA_KV_HEADS * d
    cos2, sin2 = cos_ref[...], sin_ref[...]
    q = q_ref[...].astype(F32)
    kv = kv_ref[...].astype(F32)
    qo_ref[...] = _norm_rope(q, gq_ref[...], seg_ref[...], cos2, sin2).astype(BF16)
    ko_ref[...] = _norm_rope(kv[:, :nk], gk_ref[...], seg_ref[:nk, :nk], cos2, sin2).astype(BF16)
    v = kv[:, nk:]
    lane = lax.broadcasted_iota(jnp.int32, v.shape, 1)
    ones_col = jnp.where(lane == d, 1.0, 0.0)
    vo_ref[:, :nk] = jnp.where(lane < d, v, ones_col).astype(BF16)
    vo_ref[:, nk:] = jnp.where(lane < d, pltpu.roll(v, d, axis=1), ones_col).astype(BF16)


def gqa_prep(p2, qk_g, cos2, sin2, *, s, tm):
    m = p2.shape[0]
    nq = GQA_Q_HEADS * HEAD_DIM
    nkv = GQA_KV_HEADS * HEAD_DIM
    assert nkv == V7X_LANES
    tiles_per_seq = s // tm
    q_scale = HEAD_DIM ** -0.5 * LOG2E
    gq = jnp.tile(qk_g[0].astype(F32) * q_scale, GQA_Q_HEADS).reshape(1, nq)
    gk = jnp.tile(qk_g[1].astype(F32), GQA_KV_HEADS).reshape(1, nkv)
    seg = jnp.kron(jnp.eye(GQA_Q_HEADS, dtype=F32), jnp.full((HEAD_DIM, HEAD_DIM), 1.0 / HEAD_DIM, F32)).astype(BF16)
    return pl.pallas_call(
        _gqa_prep_kernel,
        grid=(m // tm,),
        in_specs=[
            pl.BlockSpec((tm, nq), lambda i: (i, OFF_BQ // nq)),
            pl.BlockSpec((tm, 2 * nkv), lambda i: (i, OFF_BKV // (2 * nkv))),
            pl.BlockSpec((1, nq), lambda i: (0, 0)),
            pl.BlockSpec((1, nkv), lambda i: (0, 0)),
            pl.BlockSpec((nq, nq), lambda i: (0, 0)),
            pl.BlockSpec((tm, V7X_LANES), lambda i: (i % tiles_per_seq, 0)),
            pl.BlockSpec((tm, V7X_LANES), lambda i: (i % tiles_per_seq, 0)),
        ],
        out_specs=[
            pl.BlockSpec((tm, nq), lambda i: (i, 0)),
            pl.BlockSpec((tm, nkv), lambda i: (i, 0)),
            pl.BlockSpec((tm, 2 * nkv), lambda i: (i, 0)),
        ],
        out_shape=[
            jax.ShapeDtypeStruct((m, nq), BF16),
            jax.ShapeDtypeStruct((m, nkv), BF16),
            jax.ShapeDtypeStruct((m, 2 * nkv), BF16),
        ],
        compiler_params=_params(("parallel",), 16 * tm * nq * 4),
        name="gqa_prep",
    )(p2, p2, gq, gk, seg, cos2, sin2)


def _gqa_attn_kernel(q_ref, kt_ref, v_ref, o_ref, q_sc, m_ref, acc_ref, *, tk):
    d = HEAD_DIM
    tq = q_ref.shape[1]
    grp = q_ref.shape[2] // d
    s = kt_ref.shape[2]
    for g in range(grp):
        q_sc[g * tq:(g + 1) * tq, :] = q_ref[0, :, g * d:(g + 1) * d]

    def qk(item):
        c, g = item
        return jnp.dot(q_sc[g * tq:(g + 1) * tq, :], kt_ref[0, :, c * tk:(c + 1) * tk],
                       preferred_element_type=F32)

    items = [(c, g) for c in range(s // tk) for g in range(grp)]
    sc_next = qk(items[0])
    for n, (c, g) in enumerate(items):
        rows = slice(g * tq, (g + 1) * tq)
        sc = sc_next
        if n + 1 < len(items):
            sc_next = qk(items[n + 1])
        v = v_ref[0, c * tk:(c + 1) * tk, :]
        row_max = jnp.max(sc, axis=-1, keepdims=True)
        if c == 0:
            m_new = jnp.broadcast_to(row_max, (tq, V7X_LANES))
        else:
            m_prev = m_ref[rows, :]
            m_new = jnp.maximum(m_prev, row_max)
        p = jnp.exp2((sc - pltpu.repeat(m_new, tk // V7X_LANES, axis=1)).astype(BF16))
        pv = jnp.dot(p, v, preferred_element_type=F32)
        if c == 0:
            acc_ref[rows, :] = pv
        else:
            acc_ref[rows, :] = acc_ref[rows, :] * jnp.exp2(m_prev - m_new) + pv
        m_ref[rows, :] = m_new
    for g in range(grp):
        acc = acc_ref[g * tq:(g + 1) * tq, :]
        o_ref[0, :, g * d:(g + 1) * d] = (acc[:, :d] / acc[:, d:d + 1]).astype(o_ref.dtype)


def gqa_attention(q3, kt3, v3, *, tq, tk):
    b, s, nq = q3.shape
    d = HEAD_DIM
    grp = GQA_Q_HEADS // GQA_KV_HEADS
    mrows = grp * tq
    vmem = (2 * (tq * grp * d * 2 + d * s * 2 + s * 128 * 2 + tq * grp * d * 2) + 3 * mrows * 128 * 4
            + 12 * tq * tk * 4)
    return pl.pallas_call(
        functools.partial(_gqa_attn_kernel, tk=tk),
        grid=(b, GQA_KV_HEADS, s // tq),
        in_specs=[
            pl.BlockSpec((1, tq, grp * d), lambda bi, kv, i: (bi, i, kv)),
            pl.BlockSpec((1, d, s), lambda bi, kv, i: (bi, kv, 0)),
            pl.BlockSpec((1, s, 2 * d), lambda bi, kv, i: (bi, 0, kv)),
        ],
        out_specs=pl.BlockSpec((1, tq, grp * d), lambda bi, kv, i: (bi, i, kv)),
        out_shape=jax.ShapeDtypeStruct((b, s, nq), BF16),
        scratch_shapes=[pltpu.VMEM((mrows, d), BF16), pltpu.VMEM((mrows, V7X_LANES), F32),
                        pltpu.VMEM((mrows, 2 * d), F32)],
        compiler_params=_params(("parallel", "parallel", "parallel"), vmem),
        name="gqa_attn",
    )(q3, kt3, v3)


def _mlstm_step(chains, ms, q_ref, k_ref, v_ref, r_sc, cm_sc, b_sc, st_sc, h_sc):
    L = MLSTM_CHUNK
    dh = MLSTM_HEAD_DIM
    assert L == dh
    scale = dh ** -0.5
    row_i = lax.broadcasted_iota(jnp.int32, (L, L), 0)
    col_i = lax.broadcasted_iota(jnp.int32, (L, L), 1)

    def col(x_row):
        return jnp.transpose(jnp.broadcast_to(x_row, (L, L)))

    pre = []
    for (hh, direction, c), m in zip(chains, ms):
        off = pl.multiple_of(c * L, L)
        lanes = slice(hh * dh, (hh + 1) * dh)
        q = q_ref[0, pl.ds(off, L), lanes]
        k = k_ref[0, pl.ds(off, L), lanes]
        v = v_ref[0, pl.ds(off, L), lanes]
        r_row = r_sc[hh, direction, pl.ds(c, 1), :]
        cm_row = cm_sc[hh, direction, pl.ds(c, 1), :]
        b_row = b_sc[hh, direction, pl.ds(c, 1), :]
        rmax = jnp.max(r_row, axis=-1, keepdims=True)
        btot = b_row[:, L - 1:L] if direction == 0 else b_row[:, 0:1]
        cmat = jnp.maximum(m, col(cm_row))
        mask = (row_i >= col_i) if direction == 0 else (row_i <= col_i)
        c_last = jnp.maximum(m, rmax)
        w_state = jnp.exp(col(r_row) - c_last) * scale
        pre.append(dict(
            off=off, lanes=lanes, q=q, k=k, v=v,
            w_intra=jnp.where(mask, jnp.exp(r_row - cmat), 0.0) * scale,
            w_inter=jnp.exp(m - cmat),
            den_floor=jnp.exp(-(col(b_row) + cmat)),
            decay=jnp.exp(m - c_last),
            kv_w=jnp.concatenate([w_state * v.astype(F32), w_state], axis=-1).astype(BF16),
            m_new=btot + c_last,
        ))
    s_raw = [lax.dot_general(p["q"], p["k"], (((1,), (1,)), ((), ())), preferred_element_type=F32) for p in pre]
    states = [st_sc[hh, direction] for hh, direction, _ in chains]
    inter = [jnp.dot(p["q"], st.astype(BF16), preferred_element_type=F32) for p, st in zip(pre, states)]
    upd = [lax.dot_general(p["k"], p["kv_w"], (((0,), (0,)), ((), ())), preferred_element_type=F32) for p in pre]
    for n, ((hh, direction, _), p) in enumerate(zip(chains, pre)):
        st_sc[hh, direction] = p["decay"] * states[n] + upd[n]
        v_aug = jnp.concatenate([p["v"], jnp.ones((L, dh), BF16)], axis=-1)
        intra = jnp.dot((s_raw[n] * p["w_intra"]).astype(BF16), v_aug, preferred_element_type=F32)
        h_aug = jnp.concatenate([p["w_inter"], p["w_inter"]], axis=-1) * inter[n] + intra
        h_sc[hh, direction, pl.ds(p["off"], L), :] = h_aug[:, :dh] / jnp.maximum(jnp.abs(h_aug[:, dh:]), p["den_floor"])
    return [p["m_new"] for p in pre]


def _mlstm_kernel(bias_ref, q_ref, k_ref, v_ref, o_ref, gate_ref, g_ref, y_ref,
                  r_sc, cm_sc, b_sc, h_sc, st_sc):
    L = MLSTM_CHUNK
    dh = MLSTM_HEAD_DIM
    hpb = q_ref.shape[2] // dh
    head0 = pl.program_id(1) * hpb
    nc = q_ref.shape[1] // L
    lane = lax.broadcasted_iota(jnp.int32, (nc, L), 1)
    shifts = [1 << t for t in range(int(math.log2(L)))]
    for hh in range(hpb):
        for d in range(2):
            i_pre = gate_ref[0, 2 * d, hh] + bias_ref[2 * d, head0 + hh]
            f_pre = gate_ref[0, 2 * d + 1, hh] + bias_ref[2 * d + 1, head0 + hh]
            logf = jnp.minimum(f_pre, 0.0) - jnp.log1p(jnp.exp(-jnp.abs(f_pre)))
            bc = logf
            for sh in shifts:
                if d == 0:
                    bc = bc + jnp.where(lane >= sh, pltpu.roll(bc, sh, axis=1), 0.0)
                else:
                    bc = bc + jnp.where(lane < L - sh, pltpu.roll(bc, L - sh, axis=1), 0.0)
            r = i_pre - bc
            cm = r
            for sh in shifts:
                if d == 0:
                    cm = jnp.maximum(cm, jnp.where(lane >= sh, pltpu.roll(cm, sh, axis=1), -jnp.inf))
                else:
                    cm = jnp.maximum(cm, jnp.where(lane < L - sh, pltpu.roll(cm, L - sh, axis=1), -jnp.inf))
            r_sc[hh, d] = r
            cm_sc[hh, d] = cm
            b_sc[hh, d] = bc
    st_sc[...] = jnp.zeros(st_sc.shape, F32)

    def body(c, ms):
        chains = [(hh, d, c if d == 0 else nc - 1 - c) for hh in range(hpb) for d in range(2)]
        return tuple(_mlstm_step(chains, ms, q_ref, k_ref, v_ref, r_sc, cm_sc, b_sc, st_sc, h_sc))

    lax.fori_loop(0, nc, body, tuple(jnp.zeros((1, 1), F32) for _ in range(2 * hpb)))
    for hh in range(hpb):
        lanes = slice(hh * dh, (hh + 1) * dh)
        hsum = h_sc[hh, 0] + h_sc[hh, 1]
        ms = jnp.mean(hsum * hsum, axis=-1, keepdims=True)
        y = hsum * lax.rsqrt(ms + EPS) * g_ref[:, lanes]
        y_ref[0, :, lanes] = (jax.nn.sigmoid(o_ref[0, :, lanes].astype(F32)) * y).astype(y_ref.dtype)


def mlstm_branch(p3, gates5, gate_bias, norm_g, *, heads_per_block):
    b, s, _ = p3.shape
    L = MLSTM_CHUNK
    hpb = heads_per_block
    wb = hpb * MLSTM_HEAD_DIM
    nc = s // L
    blk = lambda off: pl.BlockSpec((1, s, wb), lambda bi, h, off=off: (bi, 0, off // wb + h))
    vmem = 2 * 5 * s * wb * 2 + 2 * s * wb * 4 + 6 * hpb * nc * L * 4 + 4 * hpb * wb * wb * 4 + 3 * s * wb * 4
    return pl.pallas_call(
        _mlstm_kernel,
        grid=(b, MLSTM_HEADS // hpb),
        in_specs=[
            pl.BlockSpec(memory_space=pltpu.SMEM),
            blk(OFF_CQ), blk(OFF_CK), blk(OFF_CV), blk(OFF_CO),
            pl.BlockSpec((1, 4, hpb, nc, L), lambda bi, h: (bi, 0, h, 0, 0)),
            pl.BlockSpec((1, wb), lambda bi, h: (0, h)),
        ],
        out_specs=pl.BlockSpec((1, s, wb), lambda bi, h: (bi, 0, h)),
        out_shape=jax.ShapeDtypeStruct((b, s, BRANCH_WIDTH), BF16),
        scratch_shapes=[
            pltpu.VMEM((hpb, 2, nc, L), F32), pltpu.VMEM((hpb, 2, nc, L), F32), pltpu.VMEM((hpb, 2, nc, L), F32),
            pltpu.VMEM((hpb, 2, s, MLSTM_HEAD_DIM), F32),
            pltpu.VMEM((hpb, 2, MLSTM_HEAD_DIM, 2 * MLSTM_HEAD_DIM), F32),
        ],
        compiler_params=_params(("parallel", "parallel"), vmem),
        name="mlstm",
    )(gate_bias.astype(F32), p3, p3, p3, p3, gates5, norm_g.reshape(1, BRANCH_WIDTH).astype(F32))


def _rel_bucket(rel):
    half = REL_BUCKETS // 2
    max_exact = half // 2
    ret = jnp.where(rel > 0, half, 0)
    n = jnp.abs(rel)
    nf = jnp.maximum(n, 1).astype(F32)
    large = max_exact + (jnp.log(nf / max_exact) / math.log(REL_MAX_DIST / max_exact) * (half - max_exact)).astype(jnp.int32)
    large = jnp.minimum(large, half - 1)
    return ret + jnp.where(n < max_exact, n, large)


def diff_bias_tiles(rel_bias, t):
    assert t >= REL_MAX_DIST
    k = jnp.arange(2 * t, dtype=jnp.int32)
    rel = jnp.arange(-2, 3, dtype=jnp.int32)[:, None] * t + jnp.where(k < t, k, k - 2 * t)[None, :]
    onehot = (_rel_bucket(rel)[:, :, None] == jnp.arange(REL_BUCKETS, dtype=jnp.int32)).astype(F32)
    period = jnp.einsum('dkb,bh->hdk', onehot, rel_bias.astype(F32) * LOG2E, precision=lax.Precision.HIGHEST)
    far = period[:, 0::4, 0]
    return period, far


def _diff_attn_kernel(far_ref, q_ref, kt_ref, v_ref, period_ref, lam_ref, g_ref, o_ref,
                      q_sc, kt_sc, vaug_sc, bias_sc, sc_buf, m_ref, acc_ref, *, t, lam_init):
    dq = DIFF_QK_DIM
    dv = DIFF_V_DIM
    s = kt_ref.shape[2]
    nt = s // t
    head = pl.program_id(1)
    rb = t // 2
    assert nt % 2 == 0

    kt_sc[:, :s] = kt_ref[0]
    kt_sc[:, s:] = kt_ref[0, :, :t]
    vaug_sc[:s, :dv] = v_ref[0]
    vaug_sc[s:, :dv] = v_ref[0, :t, :]
    lane = lax.broadcasted_iota(jnp.int32, (s + t, dv), 1)
    vaug_sc[:, dv:] = jnp.where(lane == 0, 1.0, 0.0).astype(BF16)
    for dl in range(period_ref.shape[1]):
        full = jnp.broadcast_to(period_ref[0, dl:dl + 1, :], (t, 2 * t))
        bias_sc[dl] = pltpu.roll(full, 0, axis=1, stride=1, stride_axis=0)[:, :t]
    q_scale = dq ** -0.5 * LOG2E
    q_all = (q_ref[0].astype(F32) * q_scale).astype(BF16)
    q_sc[0] = q_all[:, :dq]
    q_sc[1] = q_all[:, dq:]
    lp = lam_ref[...]
    lam = (jnp.exp(jnp.sum(lp[0:1] * lp[1:2], axis=-1, keepdims=True))
           - jnp.exp(jnp.sum(lp[2:3] * lp[3:4], axis=-1, keepdims=True)) + lam_init)

    def key_tile(i, delta):
        j = lax.rem(i + delta, nt)
        return j, pl.multiple_of(j * t, t)

    def qk(i, item):
        pair, mp, hf = item
        row0 = pl.multiple_of(i * t + hf * rb, rb)
        _, off = key_tile(i, 2 * pair)
        return jnp.dot(q_sc[mp, pl.ds(row0, rb), :], kt_sc[mp * dq:(mp + 1) * dq, pl.ds(off, 2 * t)],
                       preferred_element_type=F32)

    items = [(pair, mp, hf) for pair in range(nt // 2) for mp in range(2) for hf in range(2)]
    sc_buf[...] = qk(0, items[0])

    def tile_body(i, carry):
        sc_next = sc_buf[...]
        for n, (pair, mp, hf) in enumerate(items):
            sc = sc_next
            if n + 1 < len(items):
                sc_next = qk(i, items[n + 1])
            else:
                sc_next = qk(jnp.minimum(i + 1, nt - 1), items[0])
            rows = slice(hf * rb, (hf + 1) * rb)
            scs = [sc[:, :t], sc[:, t:]]
            _, off = key_tile(i, 2 * pair)
            shifts, row_max = [], None
            for piece, delta in enumerate((2 * pair, 2 * pair + 1)):
                j, _ = key_tile(i, delta)
                if delta in (0, 1, nt - 1):
                    tile = 2 if delta == 0 else jnp.clip(j - i, -2, 2) + 2
                    scs[piece] = scs[piece] + bias_sc[tile, rows, :]
                    shifts.append(None)
                    rm = jnp.max(scs[piece], axis=-1, keepdims=True)
                else:
                    shifts.append(jnp.where(j > i, far_ref[head, 1], far_ref[head, 0]))
                    rm = jnp.max(scs[piece], axis=-1, keepdims=True) + shifts[-1]
                row_max = rm if row_max is None else jnp.maximum(row_max, rm)
            if pair == 0:
                m_new = jnp.broadcast_to(row_max, (rb, V7X_LANES))
            else:
                m_prev = m_ref[mp, rows, :]
                m_new = jnp.maximum(m_prev, row_max)
            ps = []
            for piece in range(2):
                m_sub = m_new if shifts[piece] is None else m_new - shifts[piece]
                ps.append(jnp.exp2((scs[piece] - pltpu.repeat(m_sub, t // V7X_LANES, axis=1)).astype(BF16)))
            pv = jnp.dot(jnp.concatenate(ps, axis=1), vaug_sc[pl.ds(off, 2 * t), :], preferred_element_type=F32)
            if pair == 0:
                acc_ref[mp, rows, :] = pv
            else:
                alpha = jnp.exp2(m_prev - m_new)
                acc_ref[mp, rows, :] = acc_ref[mp, rows, :] * pltpu.repeat(alpha, 2 * dv // V7X_LANES, axis=1) + pv
            m_ref[mp, rows, :] = m_new
        sc_buf[...] = sc_next
        o0 = acc_ref[0, :, :dv] / acc_ref[0, :, dv:dv + 1]
        o1 = acc_ref[1, :, :dv] / acc_ref[1, :, dv:dv + 1]
        o = o0 - lam * o1
        ms = jnp.mean(o * o, axis=-1, keepdims=True)
        row0 = pl.multiple_of(i * t, t)
        o_ref[0, pl.ds(row0, t), :] = (o * lax.rsqrt(ms + EPS) * g_ref[...] * (1.0 - lam_init)).astype(o_ref.dtype)
        return carry

    lax.fori_loop(0, nt, tile_body, 0)


def diff_attention(p3, kt3, bias_period, bias_far, lam_params, sub_g, *, t, layer_number):
    b, s, _ = p3.shape
    dv = DIFF_V_DIM
    n_off = bias_period.shape[1]
    assert s // t >= 4, "tiles 2 .. s/t-2 steps away from the query tile must all be beyond REL_MAX_DIST"
    lam_init = 0.8 - 0.6 * math.exp(-0.3 * (layer_number - 1))
    vmem = (2 * (s * 128 * 2 + 128 * s * 2 + s * dv * 2 + s * dv * 2) + n_off * t * t * 4 + s * 2 * dv * 2
            + 2 * s * 128 * 2 + t * t * 4 + 2 * t * 128 * 4 * 3 + 8 * t * t * 4)
    return pl.pallas_call(
        functools.partial(_diff_attn_kernel, t=t, lam_init=lam_init),
        grid=(b, DIFF_HEADS),
        in_specs=[
            pl.BlockSpec(memory_space=pltpu.SMEM),
            pl.BlockSpec((1, s, 128), lambda bi, h: (bi, 0, OFF_DQ // 128 + h)),
            pl.BlockSpec((1, 128, s), lambda bi, h: (bi, h, 0)),
            pl.BlockSpec((1, s, dv), lambda bi, h: (bi, 0, OFF_DV // dv + h)),
            pl.BlockSpec((1, n_off, 2 * t), lambda bi, h: (h, 0, 0)),
            pl.BlockSpec((4, DIFF_QK_DIM), lambda bi, h: (0, 0)),
            pl.BlockSpec((1, dv), lambda bi, h: (0, 0)),
        ],
        out_specs=pl.BlockSpec((1, s, dv), lambda bi, h: (bi, 0, h)),
        out_shape=jax.ShapeDtypeStruct((b, s, BRANCH_WIDTH), BF16),
        scratch_shapes=[pltpu.VMEM((2, s, DIFF_QK_DIM), BF16), pltpu.VMEM((2 * DIFF_QK_DIM, s + t), BF16),
                        pltpu.VMEM((s + t, 2 * dv), BF16),
                        pltpu.VMEM((n_off, t, t), F32), pltpu.VMEM((t // 2, 2 * t), F32),
                        pltpu.VMEM((2, t, V7X_LANES), F32), pltpu.VMEM((2, t, 2 * dv), F32)],
        compiler_params=_params(("parallel", "parallel"), vmem),
        name="diff_attn",
    )(bias_far, p3, kt3, p3, bias_period, lam_params.astype(F32), sub_g.reshape(1, dv).astype(F32))


def _merge_kernel(ya_ref, yb_ref, yc_ref, yd_ref, gate_ref, wb_ref, wo_ref, x_ref, o_ref):
    dm = x_ref.shape[1]
    merged = None
    for n, y_ref in enumerate((ya_ref, yb_ref, yc_ref, yd_ref)):
        br = jnp.dot(y_ref[...], wb_ref[n], preferred_element_type=F32)
        term = jax.nn.sigmoid(gate_ref[:, n * dm:(n + 1) * dm].astype(F32)) * br
        merged = term if merged is None else merged + term
    o_ref[...] = x_ref[...] + jnp.dot(merged.astype(BF16), wo_ref[...], preferred_element_type=F32)


def merge_branches(ys, p2, w_branch, w_out, x2, *, tm):
    m, dm = x2.shape
    w = BRANCH_WIDTH
    y_spec = pl.BlockSpec((tm, w), lambda i: (i, 0))
    vmem = 2 * (4 * tm * w * 2 + tm * 4 * dm * 2 + 4 * w * dm * 2 + dm * dm * 2 + 2 * tm * dm * 4) + 6 * tm * dm * 4
    return pl.pallas_call(
        _merge_kernel,
        grid=(m // tm,),
        in_specs=[
            y_spec, y_spec, y_spec, y_spec,
            pl.BlockSpec((tm, N_BRANCHES * dm), lambda i: (i, OFF_GATES // (N_BRANCHES * dm))),
            pl.BlockSpec((N_BRANCHES, w, dm), lambda i: (0, 0, 0)),
            pl.BlockSpec((dm, dm), lambda i: (0, 0)),
            pl.BlockSpec((tm, dm), lambda i: (i, 0)),
        ],
        out_specs=pl.BlockSpec((tm, dm), lambda i: (i, 0)),
        out_shape=jax.ShapeDtypeStruct((m, dm), F32),
        compiler_params=_params(("parallel",), vmem),
        name="merge",
    )(*ys, p2, w_branch, w_out, x2)


HALO_ROWS = 16


def _ffn_up_kernel(x_ref, xp_ref, xn_ref, g_ref, wa_ref, wl_ref, cw_ref, cb_ref, o_ref, xe_sc,
                   *, tiles_per_seq, col_chunk):
    tm, f = o_ref.shape
    h = HALO_ROWS
    pos = pl.program_id(0) % tiles_per_seq

    def norm(x):
        ms = jnp.mean(x * x, axis=-1, keepdims=True)
        return x * lax.rsqrt(ms + EPS) * g_ref[...]

    xe_sc[:h] = jnp.where(pos == 0, 0.0, norm(xp_ref[...])).astype(BF16)
    xe_sc[h:h + tm] = norm(x_ref[...]).astype(BF16)
    xe_sc[h + tm:] = jnp.where(pos == tiles_per_seq - 1, 0.0, norm(xn_ref[...])).astype(BF16)
    k0 = math.sqrt(2.0 / math.pi)
    bounds = list(range(0, f, col_chunk)) + [f]
    chunks = [slice(lo, hi) for lo, hi in zip(bounds[:-1], bounds[1:])]

    def project(cols):
        a_ext = jnp.dot(xe_sc[...], wa_ref[:, cols], preferred_element_type=F32)
        lin = jnp.dot(xe_sc[h:h + tm, :], wl_ref[:, cols], preferred_element_type=F32)
        return a_ext, lin

    nxt = project(chunks[0])
    for n, cols in enumerate(chunks):
        a_ext, lin = nxt
        if n + 1 < len(chunks):
            nxt = project(chunks[n + 1])
        cw = cw_ref[:, cols]
        c = (a_ext[h - 1:h - 1 + tm] * cw[0:1] + a_ext[h:h + tm] * cw[1:2] + a_ext[h + 1:h + 1 + tm] * cw[2:3]
             + cb_ref[:, cols])
        t = jnp.tanh(c * ((c * c) * (k0 * 0.044715) + k0))
        o_ref[:, cols] = ((c + c * t) * lin).astype(o_ref.dtype)


def ffn_up(x2, g, w_a, w_lin, conv_w, conv_b, *, s, tm):
    m, dm = x2.shape
    f = w_a.shape[1]
    tiles_per_seq = s // tm
    hb = tm // HALO_ROWS
    n_halo = m // HALO_ROWS
    vmem = 2 * (tm * dm * 4 + tm * f * 2) + 2 * dm * f * 2 + (tm + 2 * HALO_ROWS) * dm * 2 + 12 * tm * 512 * 4
    return pl.pallas_call(
        functools.partial(_ffn_up_kernel, tiles_per_seq=tiles_per_seq, col_chunk=512),
        grid=(m // tm,),
        in_specs=[
            pl.BlockSpec((tm, dm), lambda i: (i, 0)),
            pl.BlockSpec((HALO_ROWS, dm), lambda i: (jnp.maximum(i * hb - 1, 0), 0)),
            pl.BlockSpec((HALO_ROWS, dm), lambda i: (jnp.minimum((i + 1) * hb, n_halo - 1), 0)),
            pl.BlockSpec((1, dm), lambda i: (0, 0)),
            pl.BlockSpec((dm, f), lambda i: (0, 0), pipeline_mode=pl.Buffered(1)),
            pl.BlockSpec((dm, f), lambda i: (0, 0), pipeline_mode=pl.Buffered(1)),
            pl.BlockSpec((CONV_W, f), lambda i: (0, 0)),
            pl.BlockSpec((1, f), lambda i: (0, 0)),
        ],
        out_specs=pl.BlockSpec((tm, f), lambda i: (i, 0)),
        out_shape=jax.ShapeDtypeStruct((m, f), BF16),
        scratch_shapes=[pltpu.VMEM((tm + 2 * HALO_ROWS, dm), BF16)],
        compiler_params=_params(("parallel",), vmem),
        name="ffn_up",
    )(x2, x2, x2, g.reshape(1, dm).astype(F32), w_a, w_lin, conv_w.astype(F32), conv_b.reshape(1, f).astype(F32))


def _ffn_down_kernel(h_ref, wd_ref, x_ref, gf_ref, o_ref, *, final_norm):
    y = x_ref[...] + jnp.dot(h_ref[...], wd_ref[...], preferred_element_type=F32)
    if final_norm:
        ms = jnp.mean(y * y, axis=-1, keepdims=True)
        y = y * lax.rsqrt(ms + EPS) * gf_ref[...]
    o_ref[...] = y


def ffn_down(hmid, w_down, x2, final_g, *, tm, final_norm):
    m, dm = x2.shape
    f = w_down.shape[0]
    vmem = 2 * (tm * f * 2 + f * dm * 2 + 2 * tm * dm * 4) + 4 * tm * dm * 4
    return pl.pallas_call(
        functools.partial(_ffn_down_kernel, final_norm=final_norm),
        grid=(m // tm,),
        in_specs=[
            pl.BlockSpec((tm, f), lambda i: (i, 0)),
            pl.BlockSpec((f, dm), lambda i: (0, 0)),
            pl.BlockSpec((tm, dm), lambda i: (i, 0)),
            pl.BlockSpec((1, dm), lambda i: (0, 0)),
        ],
        out_specs=pl.BlockSpec((tm, dm), lambda i: (i, 0)),
        out_shape=jax.ShapeDtypeStruct((m, dm), F32),
        compiler_params=_params(("parallel",), vmem),
        name="ffn_down",
    )(hmid, w_down, x2, final_g.reshape(1, dm).astype(F32))


def _arrange_w_in(w):
    widths = (512, 512, 128, 128, 512, 512, 512, 512, 16, 512, 512, 512, 4096)
    offs = [0]
    for wd in widths:
        offs.append(offs[-1] + wd)
    seg = lambda n: w[:, offs[n]:offs[n + 1]]
    a, bq, bk, bv, cq, ck, cv, co, cgate, dq, dk, dv, gates = (seg(n) for n in range(len(widths)))
    main = jnp.concatenate([gates, a, bq, cq, ck, cv, co, dq, dk, dv, bk, bv], axis=1).astype(BF16)
    gate = jnp.pad(cgate, ((0, 0), (0, V7X_LANES - cgate.shape[1]))).astype(BF16)
    return main, gate


def kernel(x, norm_mix_g, w_in, mlstm_gate_bias, qk_norm_g, mlstm_norm_g, diff_lambda, diff_norm_g, rel_bias,
           w_branch, w_out, norm_ffn_g, w_up, conv_w, conv_b, w_down, final_norm_g):
    b, s, dm = x.shape
    depth = w_in.shape[0]
    m = b * s
    d_ff = w_down.shape[1]
    L = MLSTM_CHUNK
    t_diff = 512

    fft_tables = fourier_tables(s, n1=64)
    cos2, sin2 = rope_tables(s)
    bias_tiles, bias_far = diff_bias_tiles(rel_bias, t_diff)

    x2 = x.reshape(m, dm)
    for layer in range(depth):
        w_main, w_gate = _arrange_w_in(w_in[layer])
        p2, cgate = norm_matmul(x2, norm_mix_g[layer], w_main, w_gate, tm=1024, tn=1280)
        p3 = p2.reshape(b, s, P_WIDTH)

        y_a = fourier_mix(p2, fft_tables, b=b, s=s, tm=1024, nb=8, tn=8192)

        q_b, k_b, v_b = gqa_prep(p2, qk_norm_g[layer], cos2, sin2, s=s, tm=512)
        kt_b = jnp.swapaxes(k_b.reshape(b, s, GQA_KV_HEADS * HEAD_DIM), 1, 2)
        y_b = gqa_attention(q_b.reshape(b, s, -1), kt_b, v_b.reshape(b, s, -1), tq=256, tk=1024)

        gates5 = jnp.transpose(cgate[:, :4 * MLSTM_HEADS].reshape(b, s, 4, MLSTM_HEADS), (0, 2, 3, 1))
        gates5 = gates5.reshape(b, 4, MLSTM_HEADS, s // L, L)
        y_c = mlstm_branch(p3, gates5, mlstm_gate_bias[layer], mlstm_norm_g[layer], heads_per_block=2)

        kt_d = jnp.swapaxes(p3[:, :, OFF_DK:OFF_DK + BRANCH_WIDTH], 1, 2)
        y_d = diff_attention(p3, kt_d, bias_tiles, bias_far, diff_lambda[layer], diff_norm_g[layer],
                             t=t_diff, layer_number=layer + 1)

        ys = [y.reshape(m, BRANCH_WIDTH) for y in (y_a, y_b, y_c, y_d)]
        x2 = merge_branches(ys, p2, w_branch[layer].astype(BF16), w_out[layer].astype(BF16), x2, tm=512)

        w_a = w_up[layer][:, :d_ff].astype(BF16)
        w_lin = (w_up[layer][:, d_ff:] * 0.5).astype(BF16)
        hmid = ffn_up(x2, norm_ffn_g[layer], w_a, w_lin, conv_w[layer], conv_b[layer], s=s, tm=1024)
        x2 = ffn_down(hmid, w_down[layer].astype(BF16), x2, final_norm_g, tm=512, final_norm=(layer == depth - 1))
    return x2.reshape(b, s, dm)
```

```python
import functools
import math

import jax
import jax.numpy as jnp
from jax import lax
from jax.experimental import pallas as pl
from jax.experimental.pallas import tpu as pltpu

F32 = jnp.float32
BF16 = jnp.bfloat16

GRID_W = 64
HEAD_DIM = 64
BRANCH_WIDTH = 512
N_BRANCHES = 4
FOURIER_GROUP_DIM = 64
GQA_Q_HEADS = 8
GQA_KV_HEADS = 2
MLSTM_HEADS = 4
MLSTM_HEAD_DIM = 128
MLSTM_CHUNK = 128
DIFF_HEADS = 4
DIFF_QK_DIM = 64
DIFF_V_DIM = 128
REL_BUCKETS = 32
REL_MAX_DIST = 128
CONV_W = 3
ROPE_BASE = 10000.0
EPS = 1e-6
LOG2E = math.log2(math.e)

V7X_LANES = 128
V7X_VMEM_BYTES = 64 * 1024 * 1024
V7X_VMEM_CAP = V7X_VMEM_BYTES - 8 * 1024 * 1024

OFF_GATES = 0
OFF_A = 4096
OFF_BQ = 4608
OFF_CQ = 5120
OFF_CK = 5632
OFF_CV = 6144
OFF_CO = 6656
OFF_DQ = 7168
OFF_DK = 7680
OFF_DV = 8192
OFF_BKV = 8704
P_WIDTH = 8960


def _params(sem, vmem_bytes):
    limit = int(min(max(vmem_bytes * 3 // 2 + (4 << 20), 16 << 20), V7X_VMEM_CAP))
    return pltpu.CompilerParams(dimension_semantics=sem, vmem_limit_bytes=limit)


def _norm_mm_kernel(x_ref, g_ref, w_ref, o_ref, xn_ref):
    @pl.when(pl.program_id(1) == 0)
    def _():
        x = x_ref[...]
        ms = jnp.mean(x * x, axis=-1, keepdims=True)
        xn_ref[...] = (x * lax.rsqrt(ms + EPS) * g_ref[...]).astype(BF16)

    o_ref[...] = jnp.dot(xn_ref[...], w_ref[...], preferred_element_type=F32).astype(o_ref.dtype)


def _norm_mm_gate_kernel(x_ref, g_ref, w_ref, wg_ref, o_ref, og_ref, xn_ref):
    @pl.when(pl.program_id(1) == 0)
    def _():
        x = x_ref[...]
        ms = jnp.mean(x * x, axis=-1, keepdims=True)
        xn = (x * lax.rsqrt(ms + EPS) * g_ref[...]).astype(BF16)
        xn_ref[...] = xn
        og_ref[...] = jnp.dot(xn, wg_ref[...], preferred_element_type=F32)

    o_ref[...] = jnp.dot(xn_ref[...], w_ref[...], preferred_element_type=F32).astype(o_ref.dtype)


def norm_matmul(x, g, w, w_gate=None, *, tm, tn):
    m, k = x.shape
    n = w.shape[1]
    grid = (m // tm, n // tn)
    vmem = 2 * tm * k * 4 + tm * k * 2 + 2 * k * tn * 2 + 2 * tm * tn * 2 + 4 * tm * k
    x_spec = pl.BlockSpec((tm, k), lambda i, j: (i, 0))
    g_spec = pl.BlockSpec((1, k), lambda i, j: (0, 0))
    w_spec = pl.BlockSpec((k, tn), lambda i, j: (0, j))
    o_spec = pl.BlockSpec((tm, tn), lambda i, j: (i, j))
    scratch = [pltpu.VMEM((tm, k), BF16)]
    g2 = g.reshape(1, k).astype(F32)
    if w_gate is None:
        return pl.pallas_call(
            _norm_mm_kernel,
            grid=grid,
            in_specs=[x_spec, g_spec, w_spec],
            out_specs=o_spec,
            out_shape=jax.ShapeDtypeStruct((m, n), BF16),
            scratch_shapes=scratch,
            compiler_params=_params(("parallel", "arbitrary"), vmem),
            name="norm_matmul",
        )(x, g2, w)
    ng = w_gate.shape[1]
    return pl.pallas_call(
        _norm_mm_gate_kernel,
        grid=grid,
        in_specs=[x_spec, g_spec, w_spec, pl.BlockSpec((k, ng), lambda i, j: (0, 0))],
        out_specs=[o_spec, pl.BlockSpec((tm, ng), lambda i, j: (i, 0))],
        out_shape=[jax.ShapeDtypeStruct((m, n), BF16), jax.ShapeDtypeStruct((m, ng), F32)],
        scratch_shapes=scratch,
        compiler_params=_params(("parallel", "arbitrary"), vmem),
        name="norm_matmul_gate",
    )(x, g2, w, w_gate)


def _fourier_kernel(a_ref, bd_ref, dft_ref, o_ref, z_ref, *, row_chunk):
    s = a_ref.shape[1]
    w = a_ref.shape[2]

    @pl.when(pl.program_id(1) == 0)
    def _():
        for r in range(0, s, row_chunk):
            a = a_ref[0, r:r + row_chunk, :]
            zc = jnp.dot(a, bd_ref[...], preferred_element_type=F32)
            z_ref[r:r + row_chunk, :] = zc[:, :w].astype(BF16)
            z_ref[s + r:s + r + row_chunk, :] = zc[:, w:].astype(BF16)

    dft = dft_ref[...].reshape(o_ref.shape[1], 2 * s)
    o_ref[0] = jnp.dot(dft, z_ref[...], preferred_element_type=F32).astype(o_ref.dtype)


def fourier_tables(s):
    cg = FOURIER_GROUP_DIM
    jj = jnp.arange(cg, dtype=jnp.int32)
    ang_c = (2.0 * math.pi / cg) * ((jj[:, None] * jj[None, :]) % cg).astype(F32)
    eye_g = jnp.eye(BRANCH_WIDTH // cg, dtype=F32)
    bd_c = jnp.kron(eye_g, jnp.cos(ang_c)) * cg ** -0.5
    bd_s = jnp.kron(eye_g, jnp.sin(ang_c)) * cg ** -0.5
    bd = jnp.concatenate([bd_c, bd_s], axis=1).astype(BF16)
    n_lo = s // cg
    nn = jnp.arange(s, dtype=jnp.int32)
    ang_hi = (2.0 * math.pi / cg) * ((jj[:, None] * nn[None, :]) % cg).astype(F32)
    ll = jnp.arange(n_lo, dtype=jnp.int32)
    ang_lo = (2.0 * math.pi / s) * ((ll[:, None] * nn[None, :]) % s).astype(F32)
    ch, sh = jnp.cos(ang_hi), jnp.sin(ang_hi)
    cl, sl = jnp.cos(ang_lo), jnp.sin(ang_lo)
    scale = s ** -0.5
    a2 = jnp.concatenate([ch, -sh], axis=1)[:, None, :]
    c2 = jnp.concatenate([sh, ch], axis=1)[:, None, :]
    b2 = jnp.concatenate([cl, cl], axis=1)[None, :, :]
    d2 = jnp.concatenate([sl, sl], axis=1)[None, :, :]
    dft = ((a2 * b2 - c2 * d2) * scale).astype(BF16)
    return bd, dft


def fourier_mix(p3, bd, dft, *, tm):
    b, s, _ = p3.shape
    w = BRANCH_WIDTH
    vmem = 2 * s * w * 2 + 2 * tm * 2 * s * 2 + 2 * s * w * 2 + 2 * w * 2 * w * 2 + 2 * tm * w * 2 + 8 * tm * w
    return pl.pallas_call(
        functools.partial(_fourier_kernel, row_chunk=min(s, 512)),
        grid=(b, s // tm),
        in_specs=[
            pl.BlockSpec((1, s, w), lambda bi, i: (bi, 0, OFF_A // w)),
            pl.BlockSpec((w, 2 * w), lambda bi, i: (0, 0)),
            pl.BlockSpec((tm // dft.shape[1], dft.shape[1], 2 * s), lambda bi, i: (i, 0, 0)),
        ],
        out_specs=pl.BlockSpec((1, tm, w), lambda bi, i: (bi, i, 0)),
        out_shape=jax.ShapeDtypeStruct((b, s, w), BF16),
        scratch_shapes=[pltpu.VMEM((2 * s, w), BF16)],
        compiler_params=_params(("parallel", "arbitrary"), vmem),
        name="fourier",
    )(p3, bd, dft)


def rope_tables(s):
    rows = s // GRID_W
    row_id = jnp.repeat(jnp.arange(rows, dtype=F32), GRID_W)
    col_id = jnp.tile(jnp.arange(GRID_W, dtype=F32), rows)
    n_pairs = HEAD_DIM // 4
    inv_freq = ROPE_BASE ** (-jnp.arange(n_pairs, dtype=F32) / n_pairs)
    ang = jnp.concatenate([row_id[:, None] * inv_freq, col_id[:, None] * inv_freq], axis=-1)
    cos, sin = jnp.cos(ang), jnp.sin(ang)
    return jnp.concatenate([cos, cos] * 2, axis=-1), jnp.concatenate([-sin, sin] * 2, axis=-1)


def _norm_rope(x, g, seg, cos2, sin2):
    half = HEAD_DIM // 2
    x2 = x * x
    hi = x2.astype(BF16)
    lo = (x2 - hi.astype(F32)).astype(BF16)
    ms = jnp.dot(hi, seg, preferred_element_type=F32) + jnp.dot(lo, seg, preferred_element_type=F32)
    y = x * lax.rsqrt(ms + EPS) * g
    lane = lax.broadcasted_iota(jnp.int32, (x.shape[0], V7X_LANES), 1)
    first_half = (lane % HEAD_DIM) < half
    outs = []
    for cb in range(x.shape[1] // V7X_LANES):
        yb = y[:, cb * V7X_LANES:(cb + 1) * V7X_LANES]
        rot = jnp.where(first_half, pltpu.roll(yb, V7X_LANES - half, axis=1), pltpu.roll(yb, half, axis=1))
        outs.append(yb * cos2 + rot * sin2)
    return outs[0] if len(outs) == 1 else jnp.concatenate(outs, axis=-1)


def _gqa_prep_kernel(q_ref, kv_ref, gq_ref, gk_ref, seg_ref, cos_ref, sin_ref, qo_ref, ko_ref, vo_ref):
    d = HEAD_DIM
    nk = GQA_KV_HEADS * d
    cos2, sin2 = cos_ref[...], sin_ref[...]
    q = q_ref[...].astype(F32)
    kv = kv_ref[...].astype(F32)
    qo_ref[...] = _norm_rope(q, gq_ref[...], seg_ref[...], cos2, sin2).astype(BF16)
    ko_ref[...] = _norm_rope(kv[:, :nk], gk_ref[...], seg_ref[:nk, :nk], cos2, sin2).astype(BF16)
    v = kv[:, nk:]
    lane = lax.broadcasted_iota(jnp.int32, v.shape, 1)
    ones_col = jnp.where(lane == d, 1.0, 0.0)
    vo_ref[:, :nk] = jnp.where(lane < d, v, ones_col).astype(BF16)
    vo_ref[:, nk:] = jnp.where(lane < d, pltpu.roll(v, d, axis=1), ones_col).astype(BF16)


def gqa_prep(p2, qk_g, cos2, sin2, *, s, tm):
    m = p2.shape[0]
    nq = GQA_Q_HEADS * HEAD_DIM
    nkv = GQA_KV_HEADS * HEAD_DIM
    assert nkv == V7X_LANES
    tiles_per_seq = s // tm
    q_scale = HEAD_DIM ** -0.5 * LOG2E
    gq = jnp.tile(qk_g[0].astype(F32) * q_scale, GQA_Q_HEADS).reshape(1, nq)
    gk = jnp.tile(qk_g[1].astype(F32), GQA_KV_HEADS).reshape(1, nkv)
    seg = jnp.kron(jnp.eye(GQA_Q_HEADS, dtype=F32), jnp.full((HEAD_DIM, HEAD_DIM), 1.0 / HEAD_DIM, F32)).astype(BF16)
    return pl.pallas_call(
        _gqa_prep_kernel,
        grid=(m // tm,),
        in_specs=[
            pl.BlockSpec((tm, nq), lambda i: (i, OFF_BQ // nq)),
            pl.BlockSpec((tm, 2 * nkv), lambda i: (i, OFF_BKV // (2 * nkv))),
            pl.BlockSpec((1, nq), lambda i: (0, 0)),
            pl.BlockSpec((1, nkv), lambda i: (0, 0)),
            pl.BlockSpec((nq, nq), lambda i: (0, 0)),
            pl.BlockSpec((tm, V7X_LANES), lambda i: (i % tiles_per_seq, 0)),
            pl.BlockSpec((tm, V7X_LANES), lambda i: (i % tiles_per_seq, 0)),
        ],
        out_specs=[
            pl.BlockSpec((tm, nq), lambda i: (i, 0)),
            pl.BlockSpec((tm, nkv), lambda i: (i, 0)),
            pl.BlockSpec((tm, 2 * nkv), lambda i: (i, 0)),
        ],
        out_shape=[
            jax.ShapeDtypeStruct((m, nq), BF16),
            jax.ShapeDtypeStruct((m, nkv), BF16),
            jax.ShapeDtypeStruct((m, 2 * nkv), BF16),
        ],
        compiler_params=_params(("parallel",), 16 * tm * nq * 4),
        name="gqa_prep",
    )(p2, p2, gq, gk, seg, cos2, sin2)


def _gqa_attn_kernel(q_ref, kt_ref, v_ref, o_ref, q_sc, m_ref, acc_ref, *, tk):
    d = HEAD_DIM
    tq = q_ref.shape[1]
    grp = q_ref.shape[2] // d
    s = kt_ref.shape[2]
    for g in range(grp):
        q_sc[g * tq:(g + 1) * tq, :] = q_ref[0, :, g * d:(g + 1) * d]

    def qk(item):
        c, g = item
        return jnp.dot(q_sc[g * tq:(g + 1) * tq, :], kt_ref[0, :, c * tk:(c + 1) * tk],
                       preferred_element_type=F32)

    items = [(c, g) for c in range(s // tk) for g in range(grp)]
    sc_next = qk(items[0])
    for n, (c, g) in enumerate(items):
        rows = slice(g * tq, (g + 1) * tq)
        sc = sc_next
        if n + 1 < len(items):
            sc_next = qk(items[n + 1])
        v = v_ref[0, c * tk:(c + 1) * tk, :]
        row_max = jnp.max(sc, axis=-1, keepdims=True)
        if c == 0:
            m_new = jnp.broadcast_to(row_max, (tq, V7X_LANES))
        else:
            m_prev = m_ref[rows, :]
            m_new = jnp.maximum(m_prev, row_max)
        p = jnp.exp2((sc - pltpu.repeat(m_new, tk // V7X_LANES, axis=1)).astype(BF16))
        pv = jnp.dot(p, v, preferred_element_type=F32)
        if c == 0:
            acc_ref[rows, :] = pv
        else:
            acc_ref[rows, :] = acc_ref[rows, :] * jnp.exp2(m_prev - m_new) + pv
        m_ref[rows, :] = m_new
    for g in range(grp):
        acc = acc_ref[g * tq:(g + 1) * tq, :]
        o_ref[0, :, g * d:(g + 1) * d] = (acc[:, :d] / acc[:, d:d + 1]).astype(o_ref.dtype)


def gqa_attention(q3, kt3, v3, *, tq, tk):
    b, s, nq = q3.shape
    d = HEAD_DIM
    grp = GQA_Q_HEADS // GQA_KV_HEADS
    mrows = grp * tq
    vmem = (2 * (tq * grp * d * 2 + d * s * 2 + s * 128 * 2 + tq * grp * d * 2) + 3 * mrows * 128 * 4
            + 12 * tq * tk * 4)
    return pl.pallas_call(
        functools.partial(_gqa_attn_kernel, tk=tk),
        grid=(b, GQA_KV_HEADS, s // tq),
        in_specs=[
            pl.BlockSpec((1, tq, grp * d), lambda bi, kv, i: (bi, i, kv)),
            pl.BlockSpec((1, d, s), lambda bi, kv, i: (bi, kv, 0)),
            pl.BlockSpec((1, s, 2 * d), lambda bi, kv, i: (bi, 0, kv)),
        ],
        out_specs=pl.BlockSpec((1, tq, grp * d), lambda bi, kv, i: (bi, i, kv)),
        out_shape=jax.ShapeDtypeStruct((b, s, nq), BF16),
        scratch_shapes=[pltpu.VMEM((mrows, d), BF16), pltpu.VMEM((mrows, V7X_LANES), F32),
                        pltpu.VMEM((mrows, 2 * d), F32)],
        compiler_params=_params(("parallel", "parallel", "parallel"), vmem),
        name="gqa_attn",
    )(q3, kt3, v3)


def _mlstm_step(chains, ms, q_ref, k_ref, v_ref, r_sc, cm_sc, b_sc, st_sc, h_sc):
    L = MLSTM_CHUNK
    dh = MLSTM_HEAD_DIM
    assert L == dh
    scale = dh ** -0.5
    row_i = lax.broadcasted_iota(jnp.int32, (L, L), 0)
    col_i = lax.broadcasted_iota(jnp.int32, (L, L), 1)

    def col(x_row):
        return jnp.transpose(jnp.broadcast_to(x_row, (L, L)))

    pre = []
    for (hh, direction, c), m in zip(chains, ms):
        off = pl.multiple_of(c * L, L)
        lanes = slice(hh * dh, (hh + 1) * dh)
        q = q_ref[0, pl.ds(off, L), lanes]
        k = k_ref[0, pl.ds(off, L), lanes]
        v = v_ref[0, pl.ds(off, L), lanes]
        r_row = r_sc[hh, direction, pl.ds(c, 1), :]
        cm_row = cm_sc[hh, direction, pl.ds(c, 1), :]
        b_row = b_sc[hh, direction, pl.ds(c, 1), :]
        rmax = jnp.max(r_row, axis=-1, keepdims=True)
        btot = b_row[:, L - 1:L] if direction == 0 else b_row[:, 0:1]
        cmat = jnp.maximum(m, col(cm_row))
        mask = (row_i >= col_i) if direction == 0 else (row_i <= col_i)
        c_last = jnp.maximum(m, rmax)
        w_state = jnp.exp(col(r_row) - c_last) * scale
        pre.append(dict(
            off=off, lanes=lanes, q=q, k=k, v=v,
            w_intra=jnp.where(mask, jnp.exp(r_row - cmat), 0.0) * scale,
            w_inter=jnp.exp(m - cmat),
            den_floor=jnp.exp(-(col(b_row) + cmat)),
            decay=jnp.exp(m - c_last),
            kv_w=jnp.concatenate([w_state * v.astype(F32), w_state], axis=-1).astype(BF16),
            m_new=btot + c_last,
        ))
    s_raw = [lax.dot_general(p["q"], p["k"], (((1,), (1,)), ((), ())), preferred_element_type=F32) for p in pre]
    states = [st_sc[hh, direction] for hh, direction, _ in chains]
    inter = [jnp.dot(p["q"], st.astype(BF16), preferred_element_type=F32) for p, st in zip(pre, states)]
    upd = [lax.dot_general(p["k"], p["kv_w"], (((0,), (0,)), ((), ())), preferred_element_type=F32) for p in pre]
    for n, ((hh, direction, _), p) in enumerate(zip(chains, pre)):
        st_sc[hh, direction] = p["decay"] * states[n] + upd[n]
        v_aug = jnp.concatenate([p["v"], jnp.ones((L, dh), BF16)], axis=-1)
        intra = jnp.dot((s_raw[n] * p["w_intra"]).astype(BF16), v_aug, preferred_element_type=F32)
        h_aug = jnp.concatenate([p["w_inter"], p["w_inter"]], axis=-1) * inter[n] + intra
        h_sc[hh, direction, pl.ds(p["off"], L), :] = h_aug[:, :dh] / jnp.maximum(jnp.abs(h_aug[:, dh:]), p["den_floor"])
    return [p["m_new"] for p in pre]


def _mlstm_kernel(bias_ref, q_ref, k_ref, v_ref, o_ref, gate_ref, g_ref, y_ref,
                  r_sc, cm_sc, b_sc, h_sc, st_sc):
    L = MLSTM_CHUNK
    dh = MLSTM_HEAD_DIM
    hpb = q_ref.shape[2] // dh
    head0 = pl.program_id(1) * hpb
    nc = q_ref.shape[1] // L
    lane = lax.broadcasted_iota(jnp.int32, (nc, L), 1)
    shifts = [1 << t for t in range(int(math.log2(L)))]
    for hh in range(hpb):
        for d in range(2):
            i_pre = gate_ref[0, 2 * d, hh] + bias_ref[2 * d, head0 + hh]
            f_pre = gate_ref[0, 2 * d + 1, hh] + bias_ref[2 * d + 1, head0 + hh]
            logf = jnp.minimum(f_pre, 0.0) - jnp.log1p(jnp.exp(-jnp.abs(f_pre)))
            bc = logf
            for sh in shifts:
                if d == 0:
                    bc = bc + jnp.where(lane >= sh, pltpu.roll(bc, sh, axis=1), 0.0)
                else:
                    bc = bc + jnp.where(lane < L - sh, pltpu.roll(bc, L - sh, axis=1), 0.0)
            r = i_pre - bc
            cm = r
            for sh in shifts:
                if d == 0:
                    cm = jnp.maximum(cm, jnp.where(lane >= sh, pltpu.roll(cm, sh, axis=1), -jnp.inf))
                else:
                    cm = jnp.maximum(cm, jnp.where(lane < L - sh, pltpu.roll(cm, L - sh, axis=1), -jnp.inf))
            r_sc[hh, d] = r
            cm_sc[hh, d] = cm
            b_sc[hh, d] = bc
    st_sc[...] = jnp.zeros(st_sc.shape, F32)

    def body(c, ms):
        chains = [(hh, d, c if d == 0 else nc - 1 - c) for hh in range(hpb) for d in range(2)]
        return tuple(_mlstm_step(chains, ms, q_ref, k_ref, v_ref, r_sc, cm_sc, b_sc, st_sc, h_sc))

    lax.fori_loop(0, nc, body, tuple(jnp.zeros((1, 1), F32) for _ in range(2 * hpb)))
    for hh in range(hpb):
        lanes = slice(hh * dh, (hh + 1) * dh)
        hsum = h_sc[hh, 0] + h_sc[hh, 1]
        ms = jnp.mean(hsum * hsum, axis=-1, keepdims=True)
        y = hsum * lax.rsqrt(ms + EPS) * g_ref[:, lanes]
        y_ref[0, :, lanes] = (jax.nn.sigmoid(o_ref[0, :, lanes].astype(F32)) * y).astype(y_ref.dtype)


def mlstm_branch(p3, gates5, gate_bias, norm_g, *, heads_per_block):
    b, s, _ = p3.shape
    L = MLSTM_CHUNK
    hpb = heads_per_block
    wb = hpb * MLSTM_HEAD_DIM
    nc = s // L
    blk = lambda off: pl.BlockSpec((1, s, wb), lambda bi, h, off=off: (bi, 0, off // wb + h))
    vmem = 2 * 5 * s * wb * 2 + 2 * s * wb * 4 + 6 * hpb * nc * L * 4 + 4 * hpb * wb * wb * 4 + 3 * s * wb * 4
    return pl.pallas_call(
        _mlstm_kernel,
        grid=(b, MLSTM_HEADS // hpb),
        in_specs=[
            pl.BlockSpec(memory_space=pltpu.SMEM),
            blk(OFF_CQ), blk(OFF_CK), blk(OFF_CV), blk(OFF_CO),
            pl.BlockSpec((1, 4, hpb, nc, L), lambda bi, h: (bi, 0, h, 0, 0)),
            pl.BlockSpec((1, wb), lambda bi, h: (0, h)),
        ],
        out_specs=pl.BlockSpec((1, s, wb), lambda bi, h: (bi, 0, h)),
        out_shape=jax.ShapeDtypeStruct((b, s, BRANCH_WIDTH), BF16),
        scratch_shapes=[
            pltpu.VMEM((hpb, 2, nc, L), F32), pltpu.VMEM((hpb, 2, nc, L), F32), pltpu.VMEM((hpb, 2, nc, L), F32),
            pltpu.VMEM((hpb, 2, s, MLSTM_HEAD_DIM), F32),
            pltpu.VMEM((hpb, 2, MLSTM_HEAD_DIM, 2 * MLSTM_HEAD_DIM), F32),
        ],
        compiler_params=_params(("parallel", "parallel"), vmem),
        name="mlstm",
    )(gate_bias.astype(F32), p3, p3, p3, p3, gates5, norm_g.reshape(1, BRANCH_WIDTH).astype(F32))


def _rel_bucket(rel):
    half = REL_BUCKETS // 2
    max_exact = half // 2
    ret = jnp.where(rel > 0, half, 0)
    n = jnp.abs(rel)
    nf = jnp.maximum(n, 1).astype(F32)
    large = max_exact + (jnp.log(nf / max_exact) / math.log(REL_MAX_DIST / max_exact) * (half - max_exact)).astype(jnp.int32)
    large = jnp.minimum(large, half - 1)
    return ret + jnp.where(n < max_exact, n, large)


def diff_bias_tiles(rel_bias, t):
    assert t >= REL_MAX_DIST
    k = jnp.arange(2 * t, dtype=jnp.int32)
    rel = jnp.arange(-2, 3, dtype=jnp.int32)[:, None] * t + jnp.where(k < t, k, k - 2 * t)[None, :]
    onehot = (_rel_bucket(rel)[:, :, None] == jnp.arange(REL_BUCKETS, dtype=jnp.int32)).astype(F32)
    period = jnp.einsum('dkb,bh->hdk', onehot, rel_bias.astype(F32) * LOG2E, precision=lax.Precision.HIGHEST)
    far = period[:, 0::4, 0]
    return period, far


def _diff_attn_kernel(far_ref, q_ref, k_ref, v_ref, period_ref, lam_ref, g_ref, o_ref,
                      q_sc, kt_sc, vaug_sc, bias_sc, sc_buf, m_ref, acc_ref, *, t, lam_init):
    dq = DIFF_QK_DIM
    dv = DIFF_V_DIM
    s = k_ref.shape[1]
    nt = s // t
    head = pl.program_id(1)
    rb = t // 2
    assert nt % 2 == 0

    for r in range(0, s, t):
        kt_sc[:, r:r + t] = jnp.transpose(k_ref[0, r:r + t, :].astype(F32)).astype(BF16)
    kt_sc[:, s:] = kt_sc[:, :t]
    vaug_sc[:s, :dv] = v_ref[0]
    vaug_sc[s:, :dv] = v_ref[0, :t, :]
    lane = lax.broadcasted_iota(jnp.int32, (s + t, dv), 1)
    vaug_sc[:, dv:] = jnp.where(lane == 0, 1.0, 0.0).astype(BF16)
    for dl in range(period_ref.shape[1]):
        full = jnp.broadcast_to(period_ref[0, dl:dl + 1, :], (t, 2 * t))
        bias_sc[dl] = pltpu.roll(full, 0, axis=1, stride=1, stride_axis=0)[:, :t]
    q_scale = dq ** -0.5 * LOG2E
    q_all = (q_ref[0].astype(F32) * q_scale).astype(BF16)
    q_sc[0] = q_all[:, :dq]
    q_sc[1] = q_all[:, dq:]
    lp = lam_ref[...]
    lam = (jnp.exp(jnp.sum(lp[0:1] * lp[1:2], axis=-1, keepdims=True))
           - jnp.exp(jnp.sum(lp[2:3] * lp[3:4], axis=-1, keepdims=True)) + lam_init)

    def key_tile(i, delta):
        j = lax.rem(i + delta, nt)
        return j, pl.multiple_of(j * t, t)

    def qk(i, item):
        pair, mp, hf = item
        row0 = pl.multiple_of(i * t + hf * rb, rb)
        _, off = key_tile(i, 2 * pair)
        return jnp.dot(q_sc[mp, pl.ds(row0, rb), :], kt_sc[mp * dq:(mp + 1) * dq, pl.ds(off, 2 * t)],
                       preferred_element_type=F32)

    items = [(pair, mp, hf) for pair in range(nt // 2) for mp in range(2) for hf in range(2)]
    sc_buf[...] = qk(0, items[0])

    def tile_body(i, carry):
        sc_next = sc_buf[...]
        for n, (pair, mp, hf) in enumerate(items):
            sc = sc_next
            if n + 1 < len(items):
                sc_next = qk(i, items[n + 1])
            else:
                sc_next = qk(jnp.minimum(i + 1, nt - 1), items[0])
            rows = slice(hf * rb, (hf + 1) * rb)
            scs = [sc[:, :t], sc[:, t:]]
            _, off = key_tile(i, 2 * pair)
            shifts, row_max = [], None
            for piece, delta in enumerate((2 * pair, 2 * pair + 1)):
                j, _ = key_tile(i, delta)
                if delta in (0, 1, nt - 1):
                    tile = 2 if delta == 0 else jnp.clip(j - i, -2, 2) + 2
                    scs[piece] = scs[piece] + bias_sc[tile, rows, :]
                    shifts.append(None)
                    rm = jnp.max(scs[piece], axis=-1, keepdims=True)
                else:
                    shifts.append(jnp.where(j > i, far_ref[head, 1], far_ref[head, 0]))
                    rm = jnp.max(scs[piece], axis=-1, keepdims=True) + shifts[-1]
                row_max = rm if row_max is None else jnp.maximum(row_max, rm)
            if pair == 0:
                m_new = jnp.broadcast_to(row_max, (rb, V7X_LANES))
            else:
                m_prev = m_ref[mp, rows, :]
                m_new = jnp.maximum(m_prev, row_max)
            ps = []
            for piece in range(2):
                m_sub = m_new if shifts[piece] is None else m_new - shifts[piece]
                ps.append(jnp.exp2((scs[piece] - pltpu.repeat(m_sub, t // V7X_LANES, axis=1)).astype(BF16)))
            pv = jnp.dot(jnp.concatenate(ps, axis=1), vaug_sc[pl.ds(off, 2 * t), :], preferred_element_type=F32)
            if pair == 0:
                acc_ref[mp, rows, :] = pv
            else:
                alpha = jnp.exp2(m_prev - m_new)
                acc_ref[mp, rows, :] = acc_ref[mp, rows, :] * pltpu.repeat(alpha, 2 * dv // V7X_LANES, axis=1) + pv
            m_ref[mp, rows, :] = m_new
        sc_buf[...] = sc_next
        o0 = acc_ref[0, :, :dv] / acc_ref[0, :, dv:dv + 1]
        o1 = acc_ref[1, :, :dv] / acc_ref[1, :, dv:dv + 1]
        o = o0 - lam * o1
        ms = jnp.mean(o * o, axis=-1, keepdims=True)
        row0 = pl.multiple_of(i * t, t)
        o_ref[0, pl.ds(row0, t), :] = (o * lax.rsqrt(ms + EPS) * g_ref[...] * (1.0 - lam_init)).astype(o_ref.dtype)
        return carry

    lax.fori_loop(0, nt, tile_body, 0)


def diff_attention(p3, bias_period, bias_far, lam_params, sub_g, *, t, layer_number):
    b, s, _ = p3.shape
    dv = DIFF_V_DIM
    n_off = bias_period.shape[1]
    assert s // t >= 4, "tiles 2 .. s/t-2 steps away from the query tile must all be beyond REL_MAX_DIST"
    lam_init = 0.8 - 0.6 * math.exp(-0.3 * (layer_number - 1))
    vmem = (2 * (s * 128 * 2 + 128 * s * 2 + s * dv * 2 + s * dv * 2) + n_off * t * t * 4 + s * 2 * dv * 2
            + 2 * s * 128 * 2 + t * t * 4 + 2 * t * 128 * 4 * 3 + 8 * t * t * 4)
    return pl.pallas_call(
        functools.partial(_diff_attn_kernel, t=t, lam_init=lam_init),
        grid=(b, DIFF_HEADS),
        in_specs=[
            pl.BlockSpec(memory_space=pltpu.SMEM),
            pl.BlockSpec((1, s, 128), lambda bi, h: (bi, 0, OFF_DQ // 128 + h)),
            pl.BlockSpec((1, s, 128), lambda bi, h: (bi, 0, OFF_DK // 128 + h)),
            pl.BlockSpec((1, s, dv), lambda bi, h: (bi, 0, OFF_DV // dv + h)),
            pl.BlockSpec((1, n_off, 2 * t), lambda bi, h: (h, 0, 0)),
            pl.BlockSpec((4, DIFF_QK_DIM), lambda bi, h: (0, 0)),
            pl.BlockSpec((1, dv), lambda bi, h: (0, 0)),
        ],
        out_specs=pl.BlockSpec((1, s, dv), lambda bi, h: (bi, 0, h)),
        out_shape=jax.ShapeDtypeStruct((b, s, BRANCH_WIDTH), BF16),
        scratch_shapes=[pltpu.VMEM((2, s, DIFF_QK_DIM), BF16), pltpu.VMEM((2 * DIFF_QK_DIM, s + t), BF16),
                        pltpu.VMEM((s + t, 2 * dv), BF16),
                        pltpu.VMEM((n_off, t, t), F32), pltpu.VMEM((t // 2, 2 * t), F32),
                        pltpu.VMEM((2, t, V7X_LANES), F32), pltpu.VMEM((2, t, 2 * dv), F32)],
        compiler_params=_params(("parallel", "parallel"), vmem),
        name="diff_attn",
    )(bias_far, p3, p3, p3, bias_period, lam_params.astype(F32), sub_g.reshape(1, dv).astype(F32))


def _merge_kernel(ya_ref, yb_ref, yc_ref, yd_ref, gate_ref, wb_ref, wo_ref, x_ref, o_ref, wb_sc, wo_sc):
    dm = x_ref.shape[1]

    @pl.when(pl.program_id(0) == 0)
    def _():
        for n in range(wb_ref.shape[0]):
            wb_sc[n] = wb_ref[n].astype(BF16)
        wo_sc[...] = wo_ref[...].astype(BF16)

    merged = None
    for n, y_ref in enumerate((ya_ref, yb_ref, yc_ref, yd_ref)):
        br = jnp.dot(y_ref[...], wb_sc[n], preferred_element_type=F32)
        term = jax.nn.sigmoid(gate_ref[:, n * dm:(n + 1) * dm].astype(F32)) * br
        merged = term if merged is None else merged + term
    o_ref[...] = x_ref[...] + jnp.dot(merged.astype(BF16), wo_sc[...], preferred_element_type=F32)


def merge_branches(ys, p2, w_branch, w_out, x2, *, tm):
    m, dm = x2.shape
    w = BRANCH_WIDTH
    y_spec = pl.BlockSpec((tm, w), lambda i: (i, 0))
    vmem = (2 * (4 * tm * w * 2 + tm * 4 * dm * 2 + 2 * tm * dm * 4) + (4 * w * dm + dm * dm) * (4 + 2)
            + 6 * tm * dm * 4)
    return pl.pallas_call(
        _merge_kernel,
        grid=(m // tm,),
        in_specs=[
            y_spec, y_spec, y_spec, y_spec,
            pl.BlockSpec((tm, N_BRANCHES * dm), lambda i: (i, OFF_GATES // (N_BRANCHES * dm))),
            pl.BlockSpec((N_BRANCHES, w, dm), lambda i: (0, 0, 0), pipeline_mode=pl.Buffered(1)),
            pl.BlockSpec((dm, dm), lambda i: (0, 0), pipeline_mode=pl.Buffered(1)),
            pl.BlockSpec((tm, dm), lambda i: (i, 0)),
        ],
        out_specs=pl.BlockSpec((tm, dm), lambda i: (i, 0)),
        out_shape=jax.ShapeDtypeStruct((m, dm), F32),
        scratch_shapes=[pltpu.VMEM((N_BRANCHES, w, dm), BF16), pltpu.VMEM((dm, dm), BF16)],
        compiler_params=_params(("arbitrary",), vmem),
        name="merge",
    )(*ys, p2, w_branch.astype(F32), w_out.astype(F32), x2)


HALO_ROWS = 16


def _ffn_up_kernel(x_ref, xp_ref, xn_ref, g_ref, wa_ref, wl_ref, cw_ref, cb_ref, o_ref, xe_sc,
                   *, tiles_per_seq, col_chunk):
    tm, f = o_ref.shape
    h = HALO_ROWS
    pos = pl.program_id(0) % tiles_per_seq

    def norm(x):
        ms = jnp.mean(x * x, axis=-1, keepdims=True)
        return x * lax.rsqrt(ms + EPS) * g_ref[...]

    xe_sc[:h] = jnp.where(pos == 0, 0.0, norm(xp_ref[...])).astype(BF16)
    xe_sc[h:h + tm] = norm(x_ref[...]).astype(BF16)
    xe_sc[h + tm:] = jnp.where(pos == tiles_per_seq - 1, 0.0, norm(xn_ref[...])).astype(BF16)
    k0 = math.sqrt(2.0 / math.pi)
    bounds = list(range(0, f, col_chunk)) + [f]
    chunks = [slice(lo, hi) for lo, hi in zip(bounds[:-1], bounds[1:])]

    def project(cols):
        a_ext = jnp.dot(xe_sc[...], wa_ref[:, cols], preferred_element_type=F32)
        lin = jnp.dot(xe_sc[h:h + tm, :], wl_ref[:, cols], preferred_element_type=F32)
        return a_ext, lin

    nxt = project(chunks[0])
    for n, cols in enumerate(chunks):
        a_ext, lin = nxt
        if n + 1 < len(chunks):
            nxt = project(chunks[n + 1])
        cw = cw_ref[:, cols]
        c = (a_ext[h - 1:h - 1 + tm] * cw[0:1] + a_ext[h:h + tm] * cw[1:2] + a_ext[h + 1:h + 1 + tm] * cw[2:3]
             + cb_ref[:, cols])
        t = jnp.tanh(c * ((c * c) * (k0 * 0.044715) + k0))
        o_ref[:, cols] = ((c + c * t) * lin).astype(o_ref.dtype)


def ffn_up(x2, g, w_up, conv_w, conv_b, *, s, tm):
    m, dm = x2.shape
    f = w_up.shape[1] // 2
    tiles_per_seq = s // tm
    hb = tm // HALO_ROWS
    n_halo = m // HALO_ROWS
    vmem = 2 * (tm * dm * 4 + tm * f * 2) + 2 * dm * f * 2 + (tm + 2 * HALO_ROWS) * dm * 2 + 12 * tm * 512 * 4
    return pl.pallas_call(
        functools.partial(_ffn_up_kernel, tiles_per_seq=tiles_per_seq, col_chunk=512),
        grid=(m // tm,),
        in_specs=[
            pl.BlockSpec((tm, dm), lambda i: (i, 0)),
            pl.BlockSpec((HALO_ROWS, dm), lambda i: (jnp.maximum(i * hb - 1, 0), 0)),
            pl.BlockSpec((HALO_ROWS, dm), lambda i: (jnp.minimum((i + 1) * hb, n_halo - 1), 0)),
            pl.BlockSpec((1, dm), lambda i: (0, 0)),
            pl.BlockSpec((dm, f), lambda i: (0, 0), pipeline_mode=pl.Buffered(1)),
            pl.BlockSpec((dm, f), lambda i: (0, 1), pipeline_mode=pl.Buffered(1)),
            pl.BlockSpec((CONV_W, f), lambda i: (0, 0)),
            pl.BlockSpec((1, f), lambda i: (0, 0)),
        ],
        out_specs=pl.BlockSpec((tm, f), lambda i: (i, 0)),
        out_shape=jax.ShapeDtypeStruct((m, f), BF16),
        scratch_shapes=[pltpu.VMEM((tm + 2 * HALO_ROWS, dm), BF16)],
        compiler_params=_params(("parallel",), vmem),
        name="ffn_up",
    )(x2, x2, x2, g.reshape(1, dm).astype(F32), w_up, w_up, conv_w.astype(F32), conv_b.reshape(1, f).astype(F32))


def _ffn_down_kernel(h_ref, wd_ref, x_ref, gf_ref, o_ref, wd_sc, *, final_norm):
    @pl.when(pl.program_id(0) == 0)
    def _():
        wd_sc[...] = wd_ref[...].astype(BF16)

    y = x_ref[...] + jnp.dot(h_ref[...], wd_sc[...], preferred_element_type=F32)
    if final_norm:
        ms = jnp.mean(y * y, axis=-1, keepdims=True)
        y = y * lax.rsqrt(ms + EPS) * gf_ref[...]
    o_ref[...] = y


def ffn_down(hmid, w_down, x2, final_g, *, tm, final_norm):
    m, dm = x2.shape
    f = w_down.shape[0]
    vmem = 2 * (tm * f * 2 + 2 * tm * dm * 4) + f * dm * 4 + f * dm * 2 + 4 * tm * dm * 4
    return pl.pallas_call(
        functools.partial(_ffn_down_kernel, final_norm=final_norm),
        grid=(m // tm,),
        in_specs=[
            pl.BlockSpec((tm, f), lambda i: (i, 0)),
            pl.BlockSpec((f, dm), lambda i: (0, 0), pipeline_mode=pl.Buffered(1)),
            pl.BlockSpec((tm, dm), lambda i: (i, 0)),
            pl.BlockSpec((1, dm), lambda i: (0, 0)),
        ],
        out_specs=pl.BlockSpec((tm, dm), lambda i: (i, 0)),
        out_shape=jax.ShapeDtypeStruct((m, dm), F32),
        scratch_shapes=[pltpu.VMEM((f, dm), BF16)],
        compiler_params=_params(("arbitrary",), vmem),
        name="ffn_down",
    )(hmid, w_down.astype(F32), x2, final_g.reshape(1, dm).astype(F32))


def _arrange_w_in(w):
    widths = (512, 512, 128, 128, 512, 512, 512, 512, 16, 512, 512, 512, 4096)
    offs = [0]
    for wd in widths:
        offs.append(offs[-1] + wd)
    seg = lambda n: w[:, offs[n]:offs[n + 1]]
    a, bq, bk, bv, cq, ck, cv, co, cgate, dq, dk, dv, gates = (seg(n) for n in range(len(widths)))
    main = jnp.concatenate([gates, a, bq, cq, ck, cv, co, dq, dk, dv, bk, bv], axis=1).astype(BF16)
    gate = jnp.pad(cgate, ((0, 0), (0, V7X_LANES - cgate.shape[1]))).astype(BF16)
    return main, gate


def kernel(x, norm_mix_g, w_in, mlstm_gate_bias, qk_norm_g, mlstm_norm_g, diff_lambda, diff_norm_g, rel_bias,
           w_branch, w_out, norm_ffn_g, w_up, conv_w, conv_b, w_down, final_norm_g):
    b, s, dm = x.shape
    depth = w_in.shape[0]
    m = b * s
    d_ff = w_down.shape[1]
    L = MLSTM_CHUNK
    t_diff = 512

    bd, dft = fourier_tables(s)
    cos2, sin2 = rope_tables(s)
    bias_tiles, bias_far = diff_bias_tiles(rel_bias, t_diff)

    x2 = x.reshape(m, dm)
    for layer in range(depth):
        w_main, w_gate = _arrange_w_in(w_in[layer])
        p2, cgate = norm_matmul(x2, norm_mix_g[layer], w_main, w_gate, tm=1024, tn=1280)
        p3 = p2.reshape(b, s, P_WIDTH)

        y_a = fourier_mix(p3, bd, dft, tm=512)

        q_b, k_b, v_b = gqa_prep(p2, qk_norm_g[layer], cos2, sin2, s=s, tm=512)
        kt_b = jnp.swapaxes(k_b.reshape(b, s, GQA_KV_HEADS * HEAD_DIM), 1, 2)
        y_b = gqa_attention(q_b.reshape(b, s, -1), kt_b, v_b.reshape(b, s, -1), tq=256, tk=1024)

        gates5 = jnp.transpose(cgate[:, :4 * MLSTM_HEADS].reshape(b, s, 4, MLSTM_HEADS), (0, 2, 3, 1))
        gates5 = gates5.reshape(b, 4, MLSTM_HEADS, s // L, L)
        y_c = mlstm_branch(p3, gates5, mlstm_gate_bias[layer], mlstm_norm_g[layer], heads_per_block=2)

        y_d = diff_attention(p3, bias_tiles, bias_far, diff_lambda[layer], diff_norm_g[layer],
                             t=t_diff, layer_number=layer + 1)

        ys = [y.reshape(m, BRANCH_WIDTH) for y in (y_a, y_b, y_c, y_d)]
        x2 = merge_branches(ys, p2, w_branch[layer], w_out[layer], x2, tm=512)

        half_lin = jnp.concatenate([jnp.ones((d_ff,), F32), jnp.full((d_ff,), 0.5, F32)])
        hmid = ffn_up(x2, norm_ffn_g[layer], (w_up[layer] * half_lin).astype(BF16), conv_w[layer], conv_b[layer],
                      s=s, tm=1024)
        x2 = ffn_down(hmid, w_down[layer], x2, final_norm_g, tm=512, final_norm=(layer == depth - 1))
    return x2.reshape(b, s, dm)
```

```python
import functools
import math

import jax
import jax.numpy as jnp
from jax import lax
from jax.experimental import pallas as pl
from jax.experimental.pallas import tpu as pltpu

F32 = jnp.float32
BF16 = jnp.bfloat16

GRID_W = 64
HEAD_DIM = 64
BRANCH_WIDTH = 512
N_BRANCHES = 4
FOURIER_GROUP_DIM = 64
GQA_Q_HEADS = 8
GQA_KV_HEADS = 2
MLSTM_HEADS = 4
MLSTM_HEAD_DIM = 128
MLSTM_CHUNK = 128
DIFF_HEADS = 4
DIFF_QK_DIM = 64
DIFF_V_DIM = 128
REL_BUCKETS = 32
REL_MAX_DIST = 128
CONV_W = 3
ROPE_BASE = 10000.0
EPS = 1e-6
LOG2E = math.log2(math.e)

V7X_LANES = 128
V7X_VMEM_BYTES = 64 * 1024 * 1024
V7X_VMEM_CAP = V7X_VMEM_BYTES - 8 * 1024 * 1024

OFF_GATES = 0
OFF_A = 4096
OFF_BQ = 4608
OFF_CQ = 5120
OFF_CK = 5632
OFF_CV = 6144
OFF_CO = 6656
OFF_DQ = 7168
OFF_DK = 7680
OFF_DV = 8192
OFF_BKV = 8704
P_WIDTH = 8960


def _params(sem, vmem_bytes):
    limit = int(min(max(vmem_bytes * 3 // 2 + (4 << 20), 16 << 20), V7X_VMEM_CAP))
    return pltpu.CompilerParams(dimension_semantics=sem, vmem_limit_bytes=limit)


def _norm_mm_kernel(x_ref, g_ref, w_ref, o_ref, xn_ref):
    @pl.when(pl.program_id(1) == 0)
    def _():
        x = x_ref[...]
        ms = jnp.mean(x * x, axis=-1, keepdims=True)
        xn_ref[...] = (x * lax.rsqrt(ms + EPS) * g_ref[...]).astype(BF16)

    o_ref[...] = jnp.dot(xn_ref[...], w_ref[...], preferred_element_type=F32).astype(o_ref.dtype)


def _norm_mm_gate_kernel(x_ref, g_ref, w_ref, wg_ref, o_ref, og_ref, xn_ref):
    @pl.when(pl.program_id(1) == 0)
    def _():
        x = x_ref[...]
        ms = jnp.mean(x * x, axis=-1, keepdims=True)
        xn = (x * lax.rsqrt(ms + EPS) * g_ref[...]).astype(BF16)
        xn_ref[...] = xn
        og_ref[...] = jnp.dot(xn, wg_ref[...], preferred_element_type=F32)

    o_ref[...] = jnp.dot(xn_ref[...], w_ref[...], preferred_element_type=F32).astype(o_ref.dtype)


def norm_matmul(x, g, w, w_gate=None, *, tm, tn):
    m, k = x.shape
    n = w.shape[1]
    grid = (m // tm, n // tn)
    vmem = 2 * tm * k * 4 + tm * k * 2 + 2 * k * tn * 2 + 2 * tm * tn * 2 + 4 * tm * k
    x_spec = pl.BlockSpec((tm, k), lambda i, j: (i, 0))
    g_spec = pl.BlockSpec((1, k), lambda i, j: (0, 0))
    w_spec = pl.BlockSpec((k, tn), lambda i, j: (0, j))
    o_spec = pl.BlockSpec((tm, tn), lambda i, j: (i, j))
    scratch = [pltpu.VMEM((tm, k), BF16)]
    g2 = g.reshape(1, k).astype(F32)
    if w_gate is None:
        return pl.pallas_call(
            _norm_mm_kernel,
            grid=grid,
            in_specs=[x_spec, g_spec, w_spec],
            out_specs=o_spec,
            out_shape=jax.ShapeDtypeStruct((m, n), BF16),
            scratch_shapes=scratch,
            compiler_params=_params(("parallel", "arbitrary"), vmem),
            name="norm_matmul",
        )(x, g2, w)
    ng = w_gate.shape[1]
    return pl.pallas_call(
        _norm_mm_gate_kernel,
        grid=grid,
        in_specs=[x_spec, g_spec, w_spec, pl.BlockSpec((k, ng), lambda i, j: (0, 0))],
        out_specs=[o_spec, pl.BlockSpec((tm, ng), lambda i, j: (i, 0))],
        out_shape=[jax.ShapeDtypeStruct((m, n), BF16), jax.ShapeDtypeStruct((m, ng), F32)],
        scratch_shapes=scratch,
        compiler_params=_params(("parallel", "arbitrary"), vmem),
        name="norm_matmul_gate",
    )(x, g2, w, w_gate)


FOURIER_HALO = 16


def _fourier_kernel(a_ref, bd_ref, dft_ref, halo_ref, flip_ref, o_ref, z_ref, ext_ref, *, tm, row_chunk):
    s = a_ref.shape[1]
    w = a_ref.shape[2]
    i = pl.program_id(1)

    @pl.when(i == 0)
    def _():
        for r in range(0, s, row_chunk):
            a = a_ref[0, r:r + row_chunk, :]
            zc = jnp.dot(a, bd_ref[...], preferred_element_type=F32)
            z_ref[r:r + row_chunk, :] = zc[:, :w].astype(BF16)
            z_ref[s + r:s + r + row_chunk, :] = zc[:, w:].astype(BF16)

    ext_ref[:tm] = dft_ref[...].reshape(tm, 2 * s)
    ext_ref[tm:] = halo_ref[0]
    p = jnp.dot(ext_ref[:, :s], z_ref[:s], preferred_element_type=F32)
    q = jnp.dot(ext_ref[:, s:], z_ref[s:], preferred_element_type=F32)
    lo = pl.multiple_of(i * tm, tm)
    o_ref[0, pl.ds(lo, tm), :] = (p[:tm] + q[:tm]).astype(o_ref.dtype)
    mirrored = (p[1:tm + 1] - q[1:tm + 1]).astype(BF16)
    hi = pl.multiple_of(s - (i + 1) * tm, tm)
    o_ref[0, pl.ds(hi, tm), :] = jnp.dot(flip_ref[...], mirrored, preferred_element_type=F32).astype(o_ref.dtype)


def fourier_tables(s):
    cg = FOURIER_GROUP_DIM
    jj = jnp.arange(cg, dtype=jnp.int32)
    ang_c = (2.0 * math.pi / cg) * ((jj[:, None] * jj[None, :]) % cg).astype(F32)
    eye_g = jnp.eye(BRANCH_WIDTH // cg, dtype=F32)
    bd_c = jnp.kron(eye_g, jnp.cos(ang_c)) * cg ** -0.5
    bd_s = jnp.kron(eye_g, jnp.sin(ang_c)) * cg ** -0.5
    bd = jnp.concatenate([bd_c, bd_s], axis=1).astype(BF16)
    n_lo = s // cg
    nn = jnp.arange(s, dtype=jnp.int32)
    k1 = jnp.arange(cg // 2 + 1, dtype=jnp.int32)
    ang_hi = (2.0 * math.pi / cg) * ((k1[:, None] * nn[None, :]) % cg).astype(F32)
    ll = jnp.arange(n_lo, dtype=jnp.int32)
    ang_lo = (2.0 * math.pi / s) * ((ll[:, None] * nn[None, :]) % s).astype(F32)
    ch, sh = jnp.cos(ang_hi), jnp.sin(ang_hi)
    cl, sl = jnp.cos(ang_lo), jnp.sin(ang_lo)
    scale = s ** -0.5
    a2 = jnp.concatenate([ch, -sh], axis=1)[:, None, :]
    c2 = jnp.concatenate([sh, ch], axis=1)[:, None, :]
    b2 = jnp.concatenate([cl, cl], axis=1)[None, :, :]
    d2 = jnp.concatenate([sl, sl], axis=1)[None, :, :]
    dft = ((a2 * b2 - c2 * d2) * scale).astype(BF16)
    return bd, dft


def fourier_mix(p3, bd, dft, *, tm):
    b, s, _ = p3.shape
    w = BRANCH_WIDTH
    n_lo = dft.shape[1]
    h = FOURIER_HALO
    assert (s // 2) % tm == 0 and tm % n_lo == 0 and n_lo % h == 0
    r = jnp.arange(tm, dtype=jnp.int32)
    flip = (r[:, None] + r[None, :] == tm - 1).astype(BF16)
    vmem = (s * w * 2 + 2 * w * w * 2 + 2 * (tm + h) * 2 * s * 2 + (tm + h) * 2 * s * 2 + 2 * s * w * 2
            + 2 * s * w * 2 + 8 * (tm + h) * w * 4)
    return pl.pallas_call(
        functools.partial(_fourier_kernel, tm=tm, row_chunk=min(s, 512)),
        grid=(b, s // 2 // tm),
        in_specs=[
            pl.BlockSpec((1, s, w), lambda bi, i: (bi, 0, OFF_A // w), pipeline_mode=pl.Buffered(1)),
            pl.BlockSpec((w, 2 * w), lambda bi, i: (0, 0), pipeline_mode=pl.Buffered(1)),
            pl.BlockSpec((tm // n_lo, n_lo, 2 * s), lambda bi, i: (i, 0, 0)),
            pl.BlockSpec((1, h, 2 * s), lambda bi, i: ((i + 1) * (tm // n_lo), 0, 0)),
            pl.BlockSpec((tm, tm), lambda bi, i: (0, 0)),
        ],
        out_specs=pl.BlockSpec((1, s, w), lambda bi, i: (bi, 0, 0)),
        out_shape=jax.ShapeDtypeStruct((b, s, w), BF16),
        scratch_shapes=[pltpu.VMEM((2 * s, w), BF16), pltpu.VMEM((tm + h, 2 * s), BF16)],
        compiler_params=_params(("parallel", "arbitrary"), vmem),
        name="fourier",
    )(p3, bd, dft, dft, flip)


def rope_tables(s):
    rows = s // GRID_W
    row_id = jnp.repeat(jnp.arange(rows, dtype=F32), GRID_W)
    col_id = jnp.tile(jnp.arange(GRID_W, dtype=F32), rows)
    n_pairs = HEAD_DIM // 4
    inv_freq = ROPE_BASE ** (-jnp.arange(n_pairs, dtype=F32) / n_pairs)
    ang = jnp.concatenate([row_id[:, None] * inv_freq, col_id[:, None] * inv_freq], axis=-1)
    cos, sin = jnp.cos(ang), jnp.sin(ang)
    return jnp.concatenate([cos, cos] * 2, axis=-1), jnp.concatenate([-sin, sin] * 2, axis=-1)


def _norm_rope(x, g, seg, cos2, sin2):
    half = HEAD_DIM // 2
    x2 = x * x
    hi = x2.astype(BF16)
    lo = (x2 - hi.astype(F32)).astype(BF16)
    ms = jnp.dot(hi, seg, preferred_element_type=F32) + jnp.dot(lo, seg, preferred_element_type=F32)
    y = x * lax.rsqrt(ms + EPS) * g
    lane = lax.broadcasted_iota(jnp.int32, (x.shape[0], V7X_LANES), 1)
    first_half = (lane % HEAD_DIM) < half
    outs = []
    for cb in range(x.shape[1] // V7X_LANES):
        yb = y[:, cb * V7X_LANES:(cb + 1) * V7X_LANES]
        rot = jnp.where(first_half, pltpu.roll(yb, V7X_LANES - half, axis=1), pltpu.roll(yb, half, axis=1))
        outs.append(yb * cos2 + rot * sin2)
    return outs[0] if len(outs) == 1 else jnp.concatenate(outs, axis=-1)


def _gqa_prep_kernel(q_ref, kv_ref, gq_ref, gk_ref, seg_ref, cos_ref, sin_ref, qo_ref, ko_ref, vo_ref):
    d = HEAD_DIM
    nk = GQA_KV_HEADS * d
    cos2, sin2 = cos_ref[...], sin_ref[...]
    q = q_ref[...].astype(F32)
    kv = kv_ref[...].astype(F32)
    qo_ref[...] = _norm_rope(q, gq_ref[...], seg_ref[...], cos2, sin2).astype(BF16)
    ko_ref[...] = _norm_rope(kv[:, :nk], gk_ref[...], seg_ref[:nk, :nk], cos2, sin2).astype(BF16)
    v = kv[:, nk:]
    lane = lax.broadcasted_iota(jnp.int32, v.shape, 1)
    ones_col = jnp.where(lane == d, 1.0, 0.0)
    vo_ref[:, :nk] = jnp.where(lane < d, v, ones_col).astype(BF16)
    vo_ref[:, nk:] = jnp.where(lane < d, pltpu.roll(v, d, axis=1), ones_col).astype(BF16)


def gqa_prep(p2, qk_g, cos2, sin2, *, s, tm):
    m = p2.shape[0]
    nq = GQA_Q_HEADS * HEAD_DIM
    nkv = GQA_KV_HEADS * HEAD_DIM
    assert nkv == V7X_LANES
    tiles_per_seq = s // tm
    q_scale = HEAD_DIM ** -0.5 * LOG2E
    gq = jnp.tile(qk_g[0].astype(F32) * q_scale, GQA_Q_HEADS).reshape(1, nq)
    gk = jnp.tile(qk_g[1].astype(F32), GQA_KV_HEADS).reshape(1, nkv)
    seg = jnp.kron(jnp.eye(GQA_Q_HEADS, dtype=F32), jnp.full((HEAD_DIM, HEAD_DIM), 1.0 / HEAD_DIM, F32)).astype(BF16)
    return pl.pallas_call(
        _gqa_prep_kernel,
        grid=(m // tm,),
        in_specs=[
            pl.BlockSpec((tm, nq), lambda i: (i, OFF_BQ // nq)),
            pl.BlockSpec((tm, 2 * nkv), lambda i: (i, OFF_BKV // (2 * nkv))),
            pl.BlockSpec((1, nq), lambda i: (0, 0)),
            pl.BlockSpec((1, nkv), lambda i: (0, 0)),
            pl.BlockSpec((nq, nq), lambda i: (0, 0)),
            pl.BlockSpec((tm, V7X_LANES), lambda i: (i % tiles_per_seq, 0)),
            pl.BlockSpec((tm, V7X_LANES), lambda i: (i % tiles_per_seq, 0)),
        ],
        out_specs=[
            pl.BlockSpec((tm, nq), lambda i: (i, 0)),
            pl.BlockSpec((tm, nkv), lambda i: (i, 0)),
            pl.BlockSpec((tm, 2 * nkv), lambda i: (i, 0)),
        ],
        out_shape=[
            jax.ShapeDtypeStruct((m, nq), BF16),
            jax.ShapeDtypeStruct((m, nkv), BF16),
            jax.ShapeDtypeStruct((m, 2 * nkv), BF16),
        ],
        compiler_params=_params(("parallel",), 16 * tm * nq * 4),
        name="gqa_prep",
    )(p2, p2, gq, gk, seg, cos2, sin2)


def _gqa_attn_kernel(q_ref, kt_ref, v_ref, o_ref, q_sc, m_ref, acc_ref, *, tk):
    d = HEAD_DIM
    tq = q_ref.shape[1]
    grp = q_ref.shape[2] // d
    s = kt_ref.shape[2]
    for g in range(grp):
        q_sc[g * tq:(g + 1) * tq, :] = q_ref[0, :, g * d:(g + 1) * d]

    def qk(item):
        c, g = item
        return jnp.dot(q_sc[g * tq:(g + 1) * tq, :], kt_ref[0, :, c * tk:(c + 1) * tk],
                       preferred_element_type=F32)

    items = [(c, g) for c in range(s // tk) for g in range(grp)]
    sc_next = qk(items[0])
    for n, (c, g) in enumerate(items):
        rows = slice(g * tq, (g + 1) * tq)
        sc = sc_next
        if n + 1 < len(items):
            sc_next = qk(items[n + 1])
        v = v_ref[0, c * tk:(c + 1) * tk, :]
        row_max = jnp.max(sc, axis=-1, keepdims=True)
        if c == 0:
            m_new = jnp.broadcast_to(row_max, (tq, V7X_LANES))
        else:
            m_prev = m_ref[rows, :]
            m_new = jnp.maximum(m_prev, row_max)
        p = jnp.exp2((sc - pltpu.repeat(m_new, tk // V7X_LANES, axis=1)).astype(BF16))
        pv = jnp.dot(p, v, preferred_element_type=F32)
        if c == 0:
            acc_ref[rows, :] = pv
        else:
            acc_ref[rows, :] = acc_ref[rows, :] * jnp.exp2(m_prev - m_new) + pv
        m_ref[rows, :] = m_new
    for g in range(grp):
        acc = acc_ref[g * tq:(g + 1) * tq, :]
        o_ref[0, :, g * d:(g + 1) * d] = (acc[:, :d] / acc[:, d:d + 1]).astype(o_ref.dtype)


def gqa_attention(q3, kt3, v3, *, tq, tk):
    b, s, nq = q3.shape
    d = HEAD_DIM
    grp = GQA_Q_HEADS // GQA_KV_HEADS
    mrows = grp * tq
    vmem = (2 * (tq * grp * d * 2 + d * s * 2 + s * 128 * 2 + tq * grp * d * 2) + 3 * mrows * 128 * 4
            + 12 * tq * tk * 4)
    return pl.pallas_call(
        functools.partial(_gqa_attn_kernel, tk=tk),
        grid=(b, GQA_KV_HEADS, s // tq),
        in_specs=[
            pl.BlockSpec((1, tq, grp * d), lambda bi, kv, i: (bi, i, kv)),
            pl.BlockSpec((1, d, s), lambda bi, kv, i: (bi, kv, 0)),
            pl.BlockSpec((1, s, 2 * d), lambda bi, kv, i: (bi, 0, kv)),
        ],
        out_specs=pl.BlockSpec((1, tq, grp * d), lambda bi, kv, i: (bi, i, kv)),
        out_shape=jax.ShapeDtypeStruct((b, s, nq), BF16),
        scratch_shapes=[pltpu.VMEM((mrows, d), BF16), pltpu.VMEM((mrows, V7X_LANES), F32),
                        pltpu.VMEM((mrows, 2 * d), F32)],
        compiler_params=_params(("parallel", "parallel", "parallel"), vmem),
        name="gqa_attn",
    )(q3, kt3, v3)


def _mlstm_step(chains, ms, q_ref, k_ref, v_ref, r_sc, cm_sc, b_sc, st_sc, h_sc):
    L = MLSTM_CHUNK
    dh = MLSTM_HEAD_DIM
    assert L == dh
    scale = dh ** -0.5
    row_i = lax.broadcasted_iota(jnp.int32, (L, L), 0)
    col_i = lax.broadcasted_iota(jnp.int32, (L, L), 1)

    def col(x_row):
        return jnp.transpose(jnp.broadcast_to(x_row, (L, L)))

    pre = []
    for (hh, direction, c), m in zip(chains, ms):
        off = pl.multiple_of(c * L, L)
        lanes = slice(hh * dh, (hh + 1) * dh)
        q = q_ref[0, pl.ds(off, L), lanes]
        k = k_ref[0, pl.ds(off, L), lanes]
        v = v_ref[0, pl.ds(off, L), lanes]
        r_row = r_sc[hh, direction, pl.ds(c, 1), :]
        cm_row = cm_sc[hh, direction, pl.ds(c, 1), :]
        b_row = b_sc[hh, direction, pl.ds(c, 1), :]
        rmax = jnp.max(r_row, axis=-1, keepdims=True)
        btot = b_row[:, L - 1:L] if direction == 0 else b_row[:, 0:1]
        cmat = jnp.maximum(m, col(cm_row))
        mask = (row_i >= col_i) if direction == 0 else (row_i <= col_i)
        c_last = jnp.maximum(m, rmax)
        w_state = jnp.exp(col(r_row) - c_last) * scale
        pre.append(dict(
            off=off, lanes=lanes, q=q, k=k, v=v,
            w_intra=jnp.where(mask, jnp.exp(r_row - cmat), 0.0) * scale,
            w_inter=jnp.exp(m - cmat),
            den_floor=jnp.exp(-(col(b_row) + cmat)),
            decay=jnp.exp(m - c_last),
            kv_w=jnp.concatenate([w_state * v.astype(F32), w_state], axis=-1).astype(BF16),
            m_new=btot + c_last,
        ))
    s_raw = [lax.dot_general(p["q"], p["k"], (((1,), (1,)), ((), ())), preferred_element_type=F32) for p in pre]
    states = [st_sc[hh, direction] for hh, direction, _ in chains]
    inter = [jnp.dot(p["q"], st.astype(BF16), preferred_element_type=F32) for p, st in zip(pre, states)]
    upd = [lax.dot_general(p["k"], p["kv_w"], (((0,), (0,)), ((), ())), preferred_element_type=F32) for p in pre]
    for n, ((hh, direction, _), p) in enumerate(zip(chains, pre)):
        st_sc[hh, direction] = p["decay"] * states[n] + upd[n]
        v_aug = jnp.concatenate([p["v"], jnp.ones((L, dh), BF16)], axis=-1)
        intra = jnp.dot((s_raw[n] * p["w_intra"]).astype(BF16), v_aug, preferred_element_type=F32)
        h_aug = jnp.concatenate([p["w_inter"], p["w_inter"]], axis=-1) * inter[n] + intra
        h_sc[hh, direction, pl.ds(p["off"], L), :] = h_aug[:, :dh] / jnp.maximum(jnp.abs(h_aug[:, dh:]), p["den_floor"])
    return [p["m_new"] for p in pre]


def _mlstm_kernel(bias_ref, q_ref, k_ref, v_ref, o_ref, gate_ref, g_ref, y_ref,
                  r_sc, cm_sc, b_sc, h_sc, st_sc):
    L = MLSTM_CHUNK
    dh = MLSTM_HEAD_DIM
    hpb = q_ref.shape[2] // dh
    head0 = pl.program_id(1) * hpb
    nc = q_ref.shape[1] // L
    lane = lax.broadcasted_iota(jnp.int32, (nc, L), 1)
    shifts = [1 << t for t in range(int(math.log2(L)))]
    for hh in range(hpb):
        for d in range(2):
            i_pre = gate_ref[0, 2 * d, hh] + bias_ref[2 * d, head0 + hh]
            f_pre = gate_ref[0, 2 * d + 1, hh] + bias_ref[2 * d + 1, head0 + hh]
            logf = jnp.minimum(f_pre, 0.0) - jnp.log1p(jnp.exp(-jnp.abs(f_pre)))
            bc = logf
            for sh in shifts:
                if d == 0:
                    bc = bc + jnp.where(lane >= sh, pltpu.roll(bc, sh, axis=1), 0.0)
                else:
                    bc = bc + jnp.where(lane < L - sh, pltpu.roll(bc, L - sh, axis=1), 0.0)
            r = i_pre - bc
            cm = r
            for sh in shifts:
                if d == 0:
                    cm = jnp.maximum(cm, jnp.where(lane >= sh, pltpu.roll(cm, sh, axis=1), -jnp.inf))
                else:
                    cm = jnp.maximum(cm, jnp.where(lane < L - sh, pltpu.roll(cm, L - sh, axis=1), -jnp.inf))
            r_sc[hh, d] = r
            cm_sc[hh, d] = cm
            b_sc[hh, d] = bc
    st_sc[...] = jnp.zeros(st_sc.shape, F32)

    def body(c, ms):
        chains = [(hh, d, c if d == 0 else nc - 1 - c) for hh in range(hpb) for d in range(2)]
        return tuple(_mlstm_step(chains, ms, q_ref, k_ref, v_ref, r_sc, cm_sc, b_sc, st_sc, h_sc))

    lax.fori_loop(0, nc, body, tuple(jnp.zeros((1, 1), F32) for _ in range(2 * hpb)))
    for hh in range(hpb):
        lanes = slice(hh * dh, (hh + 1) * dh)
        hsum = h_sc[hh, 0] + h_sc[hh, 1]
        ms = jnp.mean(hsum * hsum, axis=-1, keepdims=True)
        y = hsum * lax.rsqrt(ms + EPS) * g_ref[:, lanes]
        y_ref[0, :, lanes] = (jax.nn.sigmoid(o_ref[0, :, lanes].astype(F32)) * y).astype(y_ref.dtype)


def mlstm_branch(p3, gates5, gate_bias, norm_g, *, heads_per_block):
    b, s, _ = p3.shape
    L = MLSTM_CHUNK
    hpb = heads_per_block
    wb = hpb * MLSTM_HEAD_DIM
    nc = s // L
    blk = lambda off: pl.BlockSpec((1, s, wb), lambda bi, h, off=off: (bi, 0, off // wb + h))
    vmem = 2 * 5 * s * wb * 2 + 2 * s * wb * 4 + 6 * hpb * nc * L * 4 + 4 * hpb * wb * wb * 4 + 3 * s * wb * 4
    return pl.pallas_call(
        _mlstm_kernel,
        grid=(b, MLSTM_HEADS // hpb),
        in_specs=[
            pl.BlockSpec(memory_space=pltpu.SMEM),
            blk(OFF_CQ), blk(OFF_CK), blk(OFF_CV), blk(OFF_CO),
            pl.BlockSpec((1, 4, hpb, nc, L), lambda bi, h: (bi, 0, h, 0, 0)),
            pl.BlockSpec((1, wb), lambda bi, h: (0, h)),
        ],
        out_specs=pl.BlockSpec((1, s, wb), lambda bi, h: (bi, 0, h)),
        out_shape=jax.ShapeDtypeStruct((b, s, BRANCH_WIDTH), BF16),
        scratch_shapes=[
            pltpu.VMEM((hpb, 2, nc, L), F32), pltpu.VMEM((hpb, 2, nc, L), F32), pltpu.VMEM((hpb, 2, nc, L), F32),
            pltpu.VMEM((hpb, 2, s, MLSTM_HEAD_DIM), F32),
            pltpu.VMEM((hpb, 2, MLSTM_HEAD_DIM, 2 * MLSTM_HEAD_DIM), F32),
        ],
        compiler_params=_params(("parallel", "parallel"), vmem),
        name="mlstm",
    )(gate_bias.astype(F32), p3, p3, p3, p3, gates5, norm_g.reshape(1, BRANCH_WIDTH).astype(F32))


def _rel_bucket(rel):
    half = REL_BUCKETS // 2
    max_exact = half // 2
    ret = jnp.where(rel > 0, half, 0)
    n = jnp.abs(rel)
    nf = jnp.maximum(n, 1).astype(F32)
    large = max_exact + (jnp.log(nf / max_exact) / math.log(REL_MAX_DIST / max_exact) * (half - max_exact)).astype(jnp.int32)
    large = jnp.minimum(large, half - 1)
    return ret + jnp.where(n < max_exact, n, large)


def diff_bias_tiles(rel_bias, t):
    assert t >= REL_MAX_DIST
    k = jnp.arange(2 * t, dtype=jnp.int32)
    rel = jnp.arange(-2, 3, dtype=jnp.int32)[:, None] * t + jnp.where(k < t, k, k - 2 * t)[None, :]
    onehot = (_rel_bucket(rel)[:, :, None] == jnp.arange(REL_BUCKETS, dtype=jnp.int32)).astype(F32)
    period = jnp.einsum('dkb,bh->hdk', onehot, rel_bias.astype(F32) * LOG2E, precision=lax.Precision.HIGHEST)
    far = period[:, 0::4, 0]
    return period, far


def _diff_attn_kernel(far_ref, q_ref, k_ref, v_ref, period_ref, lam_ref, g_ref, o_ref,
                      q_sc, kt_sc, vaug_sc, bias_sc, sc_buf, m_ref, acc_ref, *, t, lam_init):
    dq = DIFF_QK_DIM
    dv = DIFF_V_DIM
    s = k_ref.shape[1]
    nt = s // t
    head = pl.program_id(1)
    rb = sc_buf.shape[0]
    assert nt % 2 == 0 and t % rb == 0

    for r in range(0, s, t):
        kt_sc[:, r:r + t] = jnp.transpose(k_ref[0, r:r + t, :].astype(F32)).astype(BF16)
    kt_sc[:, s:] = kt_sc[:, :t]
    vaug_sc[:s, :dv] = v_ref[0]
    vaug_sc[s:, :dv] = v_ref[0, :t, :]
    lane = lax.broadcasted_iota(jnp.int32, (s + t, dv), 1)
    vaug_sc[:, dv:] = jnp.where(lane == 0, 1.0, 0.0).astype(BF16)
    for dl in range(period_ref.shape[1]):
        full = jnp.broadcast_to(period_ref[0, dl:dl + 1, :], (t, 2 * t))
        bias_sc[dl] = pltpu.roll(full, 0, axis=1, stride=1, stride_axis=0)[:, :t]
    q_scale = dq ** -0.5 * LOG2E
    q_all = (q_ref[0].astype(F32) * q_scale).astype(BF16)
    q_sc[0] = q_all[:, :dq]
    q_sc[1] = q_all[:, dq:]
    lp = lam_ref[...]
    lam = (jnp.exp(jnp.sum(lp[0:1] * lp[1:2], axis=-1, keepdims=True))
           - jnp.exp(jnp.sum(lp[2:3] * lp[3:4], axis=-1, keepdims=True)) + lam_init)

    def key_tile(i, delta):
        j = lax.rem(i + delta, nt)
        return j, pl.multiple_of(j * t, t)

    def qk(i, item):
        pair, mp, hf = item
        row0 = pl.multiple_of(i * t + hf * rb, rb)
        _, off = key_tile(i, 2 * pair)
        return jnp.dot(q_sc[mp, pl.ds(row0, rb), :], kt_sc[mp * dq:(mp + 1) * dq, pl.ds(off, 2 * t)],
                       preferred_element_type=F32)

    items = [(pair, mp, hf) for pair in range(nt // 2) for mp in range(2) for hf in range(t // rb)]
    sc_buf[...] = qk(0, items[0])

    def tile_body(i, carry):
        sc_next = sc_buf[...]
        for n, (pair, mp, hf) in enumerate(items):
            sc = sc_next
            if n + 1 < len(items):
                sc_next = qk(i, items[n + 1])
            else:
                sc_next = qk(jnp.minimum(i + 1, nt - 1), items[0])
            rows = slice(hf * rb, (hf + 1) * rb)
            scs = [sc[:, :t], sc[:, t:]]
            _, off = key_tile(i, 2 * pair)
            shifts, row_max = [], None
            for piece, delta in enumerate((2 * pair, 2 * pair + 1)):
                j, _ = key_tile(i, delta)
                if delta in (0, 1, nt - 1):
                    tile = 2 if delta == 0 else jnp.clip(j - i, -2, 2) + 2
                    scs[piece] = scs[piece] + bias_sc[tile, rows, :]
                    shifts.append(None)
                    rm = jnp.max(scs[piece], axis=-1, keepdims=True)
                else:
                    shifts.append(jnp.where(j > i, far_ref[head, 1], far_ref[head, 0]))
                    rm = jnp.max(scs[piece], axis=-1, keepdims=True) + shifts[-1]
                row_max = rm if row_max is None else jnp.maximum(row_max, rm)
            if pair == 0:
                m_new = jnp.broadcast_to(row_max, (rb, V7X_LANES))
            else:
                m_prev = m_ref[mp, rows, :]
                m_new = jnp.maximum(m_prev, row_max)
            ps = []
            for piece in range(2):
                m_sub = m_new if shifts[piece] is None else m_new - shifts[piece]
                ps.append(jnp.exp2((scs[piece] - pltpu.repeat(m_sub, t // V7X_LANES, axis=1)).astype(BF16)))
            pv = jnp.dot(jnp.concatenate(ps, axis=1), vaug_sc[pl.ds(off, 2 * t), :], preferred_element_type=F32)
            if pair == 0:
                acc_ref[mp, rows, :] = pv
            else:
                alpha = jnp.exp2(m_prev - m_new)
                acc_ref[mp, rows, :] = acc_ref[mp, rows, :] * pltpu.repeat(alpha, 2 * dv // V7X_LANES, axis=1) + pv
            m_ref[mp, rows, :] = m_new
        sc_buf[...] = sc_next
        o0 = acc_ref[0, :, :dv] / acc_ref[0, :, dv:dv + 1]
        o1 = acc_ref[1, :, :dv] / acc_ref[1, :, dv:dv + 1]
        o = o0 - lam * o1
        ms = jnp.mean(o * o, axis=-1, keepdims=True)
        row0 = pl.multiple_of(i * t, t)
        o_ref[0, pl.ds(row0, t), :] = (o * lax.rsqrt(ms + EPS) * g_ref[...] * (1.0 - lam_init)).astype(o_ref.dtype)
        return carry

    lax.fori_loop(0, nt, tile_body, 0)


def diff_attention(p3, bias_period, bias_far, lam_params, sub_g, *, t, rows_per_item, layer_number):
    b, s, _ = p3.shape
    dv = DIFF_V_DIM
    n_off = bias_period.shape[1]
    assert s // t >= 4, "tiles 2 .. s/t-2 steps away from the query tile must all be beyond REL_MAX_DIST"
    lam_init = 0.8 - 0.6 * math.exp(-0.3 * (layer_number - 1))
    vmem = (2 * (s * 128 * 2 + 128 * s * 2 + s * dv * 2 + s * dv * 2) + n_off * t * t * 4 + s * 2 * dv * 2
            + 2 * s * 128 * 2 + t * t * 4 + 2 * t * 128 * 4 * 3 + 8 * t * t * 4)
    return pl.pallas_call(
        functools.partial(_diff_attn_kernel, t=t, lam_init=lam_init),
        grid=(b, DIFF_HEADS),
        in_specs=[
            pl.BlockSpec(memory_space=pltpu.SMEM),
            pl.BlockSpec((1, s, 128), lambda bi, h: (bi, 0, OFF_DQ // 128 + h)),
            pl.BlockSpec((1, s, 128), lambda bi, h: (bi, 0, OFF_DK // 128 + h)),
            pl.BlockSpec((1, s, dv), lambda bi, h: (bi, 0, OFF_DV // dv + h)),
            pl.BlockSpec((1, n_off, 2 * t), lambda bi, h: (h, 0, 0)),
            pl.BlockSpec((4, DIFF_QK_DIM), lambda bi, h: (0, 0)),
            pl.BlockSpec((1, dv), lambda bi, h: (0, 0)),
        ],
        out_specs=pl.BlockSpec((1, s, dv), lambda bi, h: (bi, 0, h)),
        out_shape=jax.ShapeDtypeStruct((b, s, BRANCH_WIDTH), BF16),
        scratch_shapes=[pltpu.VMEM((2, s, DIFF_QK_DIM), BF16), pltpu.VMEM((2 * DIFF_QK_DIM, s + t), BF16),
                        pltpu.VMEM((s + t, 2 * dv), BF16),
                        pltpu.VMEM((n_off, t, t), F32), pltpu.VMEM((rows_per_item, 2 * t), F32),
                        pltpu.VMEM((2, t, V7X_LANES), F32), pltpu.VMEM((2, t, 2 * dv), F32)],
        compiler_params=_params(("parallel", "parallel"), vmem),
        name="diff_attn",
    )(bias_far, p3, p3, p3, bias_period, lam_params.astype(F32), sub_g.reshape(1, dv).astype(F32))


def _merge_kernel(ya_ref, yb_ref, yc_ref, yd_ref, gate_ref, wb_ref, wo_ref, x_ref, o_ref, wb_sc, wo_sc):
    dm = x_ref.shape[1]

    @pl.when(pl.program_id(0) == 0)
    def _():
        for n in range(wb_ref.shape[0]):
            wb_sc[n] = wb_ref[n].astype(BF16)
        wo_sc[...] = wo_ref[...].astype(BF16)

    merged = None
    for n, y_ref in enumerate((ya_ref, yb_ref, yc_ref, yd_ref)):
        br = jnp.dot(y_ref[...], wb_sc[n], preferred_element_type=F32)
        term = jax.nn.sigmoid(gate_ref[:, n * dm:(n + 1) * dm].astype(F32)) * br
        merged = term if merged is None else merged + term
    o_ref[...] = x_ref[...] + jnp.dot(merged.astype(BF16), wo_sc[...], preferred_element_type=F32)


def merge_branches(ys, p2, w_branch, w_out, x2, *, tm):
    m, dm = x2.shape
    w = BRANCH_WIDTH
    y_spec = pl.BlockSpec((tm, w), lambda i: (i, 0))
    vmem = (2 * (4 * tm * w * 2 + tm * 4 * dm * 2 + 2 * tm * dm * 4) + (4 * w * dm + dm * dm) * (4 + 2)
            + 6 * tm * dm * 4)
    return pl.pallas_call(
        _merge_kernel,
        grid=(m // tm,),
        in_specs=[
            y_spec, y_spec, y_spec, y_spec,
            pl.BlockSpec((tm, N_BRANCHES * dm), lambda i: (i, OFF_GATES // (N_BRANCHES * dm))),
            pl.BlockSpec((N_BRANCHES, w, dm), lambda i: (0, 0, 0), pipeline_mode=pl.Buffered(1)),
            pl.BlockSpec((dm, dm), lambda i: (0, 0), pipeline_mode=pl.Buffered(1)),
            pl.BlockSpec((tm, dm), lambda i: (i, 0)),
        ],
        out_specs=pl.BlockSpec((tm, dm), lambda i: (i, 0)),
        out_shape=jax.ShapeDtypeStruct((m, dm), F32),
        scratch_shapes=[pltpu.VMEM((N_BRANCHES, w, dm), BF16), pltpu.VMEM((dm, dm), BF16)],
        compiler_params=_params(("arbitrary",), vmem),
        name="merge",
    )(*ys, p2, w_branch.astype(F32), w_out.astype(F32), x2)


HALO_ROWS = 16


def _ffn_up_kernel(x_ref, xp_ref, xn_ref, g_ref, wa_ref, wl_ref, cw_ref, cb_ref, o_ref, xe_sc,
                   *, tiles_per_seq, col_chunk):
    tm, f = o_ref.shape
    h = HALO_ROWS
    pos = pl.program_id(0) % tiles_per_seq

    def norm(x):
        ms = jnp.mean(x * x, axis=-1, keepdims=True)
        return x * lax.rsqrt(ms + EPS) * g_ref[...]

    xe_sc[:h] = jnp.where(pos == 0, 0.0, norm(xp_ref[...])).astype(BF16)
    xe_sc[h:h + tm] = norm(x_ref[...]).astype(BF16)
    xe_sc[h + tm:] = jnp.where(pos == tiles_per_seq - 1, 0.0, norm(xn_ref[...])).astype(BF16)
    k0 = math.sqrt(2.0 / math.pi)
    bounds = list(range(0, f, col_chunk)) + [f]
    chunks = [slice(lo, hi) for lo, hi in zip(bounds[:-1], bounds[1:])]

    def project(cols):
        a_ext = jnp.dot(xe_sc[...], wa_ref[:, cols], preferred_element_type=F32)
        lin = jnp.dot(xe_sc[h:h + tm, :], wl_ref[:, cols], preferred_element_type=F32)
        return a_ext, lin

    nxt = project(chunks[0])
    for n, cols in enumerate(chunks):
        a_ext, lin = nxt
        if n + 1 < len(chunks):
            nxt = project(chunks[n + 1])
        cw = cw_ref[:, cols]
        c = (a_ext[h - 1:h - 1 + tm] * cw[0:1] + a_ext[h:h + tm] * cw[1:2] + a_ext[h + 1:h + 1 + tm] * cw[2:3]
             + cb_ref[:, cols])
        t = jnp.tanh(c * ((c * c) * (k0 * 0.044715) + k0))
        o_ref[:, cols] = ((c + c * t) * lin).astype(o_ref.dtype)


def ffn_up(x2, g, w_up, conv_w, conv_b, *, s, tm):
    m, dm = x2.shape
    f = w_up.shape[1] // 2
    tiles_per_seq = s // tm
    hb = tm // HALO_ROWS
    n_halo = m // HALO_ROWS
    vmem = 2 * (tm * dm * 4 + tm * f * 2) + 2 * dm * f * 2 + (tm + 2 * HALO_ROWS) * dm * 2 + 12 * tm * 512 * 4
    return pl.pallas_call(
        functools.partial(_ffn_up_kernel, tiles_per_seq=tiles_per_seq, col_chunk=512),
        grid=(m // tm,),
        in_specs=[
            pl.BlockSpec((tm, dm), lambda i: (i, 0)),
            pl.BlockSpec((HALO_ROWS, dm), lambda i: (jnp.maximum(i * hb - 1, 0), 0)),
            pl.BlockSpec((HALO_ROWS, dm), lambda i: (jnp.minimum((i + 1) * hb, n_halo - 1), 0)),
            pl.BlockSpec((1, dm), lambda i: (0, 0)),
            pl.BlockSpec((dm, f), lambda i: (0, 0), pipeline_mode=pl.Buffered(1)),
            pl.BlockSpec((dm, f), lambda i: (0, 1), pipeline_mode=pl.Buffered(1)),
            pl.BlockSpec((CONV_W, f), lambda i: (0, 0)),
            pl.BlockSpec((1, f), lambda i: (0, 0)),
        ],
        out_specs=pl.BlockSpec((tm, f), lambda i: (i, 0)),
        out_shape=jax.ShapeDtypeStruct((m, f), BF16),
        scratch_shapes=[pltpu.VMEM((tm + 2 * HALO_ROWS, dm), BF16)],
        compiler_params=_params(("parallel",), vmem),
        name="ffn_up",
    )(x2, x2, x2, g.reshape(1, dm).astype(F32), w_up, w_up, conv_w.astype(F32), conv_b.reshape(1, f).astype(F32))


def _ffn_down_kernel(h_ref, wd_ref, x_ref, gf_ref, o_ref, wd_sc, *, final_norm):
    @pl.when(pl.program_id(0) == 0)
    def _():
        wd_sc[...] = wd_ref[...].astype(BF16)

    y = x_ref[...] + jnp.dot(h_ref[...], wd_sc[...], preferred_element_type=F32)
    if final_norm:
        ms = jnp.mean(y * y, axis=-1, keepdims=True)
        y = y * lax.rsqrt(ms + EPS) * gf_ref[...]
    o_ref[...] = y


def ffn_down(hmid, w_down, x2, final_g, *, tm, final_norm):
    m, dm = x2.shape
    f = w_down.shape[0]
    vmem = 2 * (tm * f * 2 + 2 * tm * dm * 4) + f * dm * 4 + f * dm * 2 + 4 * tm * dm * 4
    return pl.pallas_call(
        functools.partial(_ffn_down_kernel, final_norm=final_norm),
        grid=(m // tm,),
        in_specs=[
            pl.BlockSpec((tm, f), lambda i: (i, 0)),
            pl.BlockSpec((f, dm), lambda i: (0, 0), pipeline_mode=pl.Buffered(1)),
            pl.BlockSpec((tm, dm), lambda i: (i, 0)),
            pl.BlockSpec((1, dm), lambda i: (0, 0)),
        ],
        out_specs=pl.BlockSpec((tm, dm), lambda i: (i, 0)),
        out_shape=jax.ShapeDtypeStruct((m, dm), F32),
        scratch_shapes=[pltpu.VMEM((f, dm), BF16)],
        compiler_params=_params(("arbitrary",), vmem),
        name="ffn_down",
    )(hmid, w_down.astype(F32), x2, final_g.reshape(1, dm).astype(F32))


def _arrange_w_in(w):
    widths = (512, 512, 128, 128, 512, 512, 512, 512, 16, 512, 512, 512, 4096)
    offs = [0]
    for wd in widths:
        offs.append(offs[-1] + wd)
    seg = lambda n: w[:, offs[n]:offs[n + 1]]
    a, bq, bk, bv, cq, ck, cv, co, cgate, dq, dk, dv, gates = (seg(n) for n in range(len(widths)))
    main = jnp.concatenate([gates, a, bq, cq, ck, cv, co, dq, dk, dv, bk, bv], axis=1).astype(BF16)
    gate = jnp.pad(cgate, ((0, 0), (0, V7X_LANES - cgate.shape[1]))).astype(BF16)
    return main, gate


def kernel(x, norm_mix_g, w_in, mlstm_gate_bias, qk_norm_g, mlstm_norm_g, diff_lambda, diff_norm_g, rel_bias,
           w_branch, w_out, norm_ffn_g, w_up, conv_w, conv_b, w_down, final_norm_g):
    b, s, dm = x.shape
    depth = w_in.shape[0]
    m = b * s
    d_ff = w_down.shape[1]
    L = MLSTM_CHUNK
    t_diff = 512

    bd, dft = fourier_tables(s)
    cos2, sin2 = rope_tables(s)
    bias_tiles, bias_far = diff_bias_tiles(rel_bias, t_diff)

    x2 = x.reshape(m, dm)
    for layer in range(depth):
        w_main, w_gate = _arrange_w_in(w_in[layer])
        p2, cgate = norm_matmul(x2, norm_mix_g[layer], w_main, w_gate, tm=1024, tn=1280)
        p3 = p2.reshape(b, s, P_WIDTH)

        y_a = fourier_mix(p3, bd, dft, tm=256)

        q_b, k_b, v_b = gqa_prep(p2, qk_norm_g[layer], cos2, sin2, s=s, tm=512)
        kt_b = jnp.swapaxes(k_b.reshape(b, s, GQA_KV_HEADS * HEAD_DIM), 1, 2)
        y_b = gqa_attention(q_b.reshape(b, s, -1), kt_b, v_b.reshape(b, s, -1), tq=512, tk=1024)

        gates5 = jnp.transpose(cgate[:, :4 * MLSTM_HEADS].reshape(b, s, 4, MLSTM_HEADS), (0, 2, 3, 1))
        gates5 = gates5.reshape(b, 4, MLSTM_HEADS, s // L, L)
        y_c = mlstm_branch(p3, gates5, mlstm_gate_bias[layer], mlstm_norm_g[layer], heads_per_block=2)

        y_d = diff_attention(p3, bias_tiles, bias_far, diff_lambda[layer], diff_norm_g[layer],
                             t=t_diff, rows_per_item=t_diff // 2, layer_number=layer + 1)

        ys = [y.reshape(m, BRANCH_WIDTH) for y in (y_a, y_b, y_c, y_d)]
        x2 = merge_branches(ys, p2, w_branch[layer], w_out[layer], x2, tm=512)

        half_lin = jnp.concatenate([jnp.ones((d_ff,), F32), jnp.full((d_ff,), 0.5, F32)])
        hmid = ffn_up(x2, norm_ffn_g[layer], (w_up[layer] * half_lin).astype(BF16), conv_w[layer], conv_b[layer],
                      s=s, tm=1024)
        x2 = ffn_down(hmid, w_down[layer], x2, final_norm_g, tm=512, final_norm=(layer == depth - 1))
    return x2.reshape(b, s, dm)
```

```python
import functools
import math

import jax
import jax.numpy as jnp
from jax import lax
from jax.experimental import pallas as pl
from jax.experimental.pallas import tpu as pltpu

F32 = jnp.float32
BF16 = jnp.bfloat16

GRID_W = 64
HEAD_DIM = 64
BRANCH_WIDTH = 512
N_BRANCHES = 4
FOURIER_GROUP_DIM = 64
GQA_Q_HEADS = 8
GQA_KV_HEADS = 2
MLSTM_HEADS = 4
MLSTM_HEAD_DIM = 128
MLSTM_CHUNK = 128
DIFF_HEADS = 4
DIFF_QK_DIM = 64
DIFF_V_DIM = 128
REL_BUCKETS = 32
REL_MAX_DIST = 128
CONV_W = 3
ROPE_BASE = 10000.0
EPS = 1e-6
LOG2E = math.log2(math.e)

V7X_LANES = 128
V7X_VMEM_BYTES = 64 * 1024 * 1024
V7X_VMEM_CAP = V7X_VMEM_BYTES - 8 * 1024 * 1024

OFF_GATES = 0
OFF_A = 4096
OFF_BQ = 4608
OFF_CQ = 5120
OFF_CK = 5632
OFF_CV = 6144
OFF_CO = 6656
OFF_DQ = 7168
OFF_DK = 7680
OFF_DV = 8192
OFF_BKV = 8704
P_WIDTH = 8960


def _params(sem, vmem_bytes):
    limit = int(min(max(vmem_bytes * 3 // 2 + (4 << 20), 16 << 20), V7X_VMEM_CAP))
    return pltpu.CompilerParams(dimension_semantics=sem, vmem_limit_bytes=limit)


def _norm_mm_kernel(x_ref, g_ref, w_ref, o_ref, xn_ref):
    @pl.when(pl.program_id(1) == 0)
    def _():
        x = x_ref[...]
        ms = jnp.mean(x * x, axis=-1, keepdims=True)
        xn_ref[...] = (x * lax.rsqrt(ms + EPS) * g_ref[...]).astype(BF16)

    o_ref[...] = jnp.dot(xn_ref[...], w_ref[...], preferred_element_type=F32).astype(o_ref.dtype)


def _norm_mm_gate_kernel(x_ref, g_ref, w_ref, wg_ref, o_ref, og_ref, xn_ref):
    @pl.when(pl.program_id(1) == 0)
    def _():
        x = x_ref[...]
        ms = jnp.mean(x * x, axis=-1, keepdims=True)
        xn = (x * lax.rsqrt(ms + EPS) * g_ref[...]).astype(BF16)
        xn_ref[...] = xn
        og_ref[...] = jnp.dot(xn, wg_ref[...], preferred_element_type=F32)

    o_ref[...] = jnp.dot(xn_ref[...], w_ref[...], preferred_element_type=F32).astype(o_ref.dtype)


def norm_matmul(x, g, w, w_gate=None, *, tm, tn):
    m, k = x.shape
    n = w.shape[1]
    grid = (m // tm, n // tn)
    vmem = 2 * tm * k * 4 + tm * k * 2 + 2 * k * tn * 2 + 2 * tm * tn * 2 + 4 * tm * k
    x_spec = pl.BlockSpec((tm, k), lambda i, j: (i, 0))
    g_spec = pl.BlockSpec((1, k), lambda i, j: (0, 0))
    w_spec = pl.BlockSpec((k, tn), lambda i, j: (0, j))
    o_spec = pl.BlockSpec((tm, tn), lambda i, j: (i, j))
    scratch = [pltpu.VMEM((tm, k), BF16)]
    g2 = g.reshape(1, k).astype(F32)
    if w_gate is None:
        return pl.pallas_call(
            _norm_mm_kernel,
            grid=grid,
            in_specs=[x_spec, g_spec, w_spec],
            out_specs=o_spec,
            out_shape=jax.ShapeDtypeStruct((m, n), BF16),
            scratch_shapes=scratch,
            compiler_params=_params(("parallel", "arbitrary"), vmem),
            name="norm_matmul",
        )(x, g2, w)
    ng = w_gate.shape[1]
    return pl.pallas_call(
        _norm_mm_gate_kernel,
        grid=grid,
        in_specs=[x_spec, g_spec, w_spec, pl.BlockSpec((k, ng), lambda i, j: (0, 0))],
        out_specs=[o_spec, pl.BlockSpec((tm, ng), lambda i, j: (i, 0))],
        out_shape=[jax.ShapeDtypeStruct((m, n), BF16), jax.ShapeDtypeStruct((m, ng), F32)],
        scratch_shapes=scratch,
        compiler_params=_params(("parallel", "arbitrary"), vmem),
        name="norm_matmul_gate",
    )(x, g2, w, w_gate)


FOURIER_HALO = 16


def _fourier_kernel(a_ref, bd_ref, dft_ref, halo_ref, flip_ref, o_ref, z_ref, ext_ref, *, tm, row_chunk):
    s = a_ref.shape[1]
    w = a_ref.shape[2]
    i = pl.program_id(1)

    @pl.when(i == 0)
    def _():
        for r in range(0, s, row_chunk):
            a = a_ref[0, r:r + row_chunk, :]
            zc = jnp.dot(a, bd_ref[...], preferred_element_type=F32)
            z_ref[r:r + row_chunk, :] = zc[:, :w].astype(BF16)
            z_ref[s + r:s + r + row_chunk, :] = zc[:, w:].astype(BF16)

    ext_ref[:tm] = dft_ref[...].reshape(tm, 2 * s)
    ext_ref[tm:] = halo_ref[0]
    p = jnp.dot(ext_ref[:, :s], z_ref[:s], preferred_element_type=F32)
    q = jnp.dot(ext_ref[:, s:], z_ref[s:], preferred_element_type=F32)
    lo = pl.multiple_of(i * tm, tm)
    o_ref[0, pl.ds(lo, tm), :] = (p[:tm] + q[:tm]).astype(o_ref.dtype)
    mirrored = (p[1:tm + 1] - q[1:tm + 1]).astype(BF16)
    hi = pl.multiple_of(s - (i + 1) * tm, tm)
    o_ref[0, pl.ds(hi, tm), :] = jnp.dot(flip_ref[...], mirrored, preferred_element_type=F32).astype(o_ref.dtype)


def fourier_tables(s):
    cg = FOURIER_GROUP_DIM
    jj = jnp.arange(cg, dtype=jnp.int32)
    ang_c = (2.0 * math.pi / cg) * ((jj[:, None] * jj[None, :]) % cg).astype(F32)
    eye_g = jnp.eye(BRANCH_WIDTH // cg, dtype=F32)
    bd_c = jnp.kron(eye_g, jnp.cos(ang_c)) * cg ** -0.5
    bd_s = jnp.kron(eye_g, jnp.sin(ang_c)) * cg ** -0.5
    bd = jnp.concatenate([bd_c, bd_s], axis=1).astype(BF16)
    n_lo = s // cg
    nn = jnp.arange(s, dtype=jnp.int32)
    k1 = jnp.arange(cg // 2 + 1, dtype=jnp.int32)
    ang_hi = (2.0 * math.pi / cg) * ((k1[:, None] * nn[None, :]) % cg).astype(F32)
    ll = jnp.arange(n_lo, dtype=jnp.int32)
    ang_lo = (2.0 * math.pi / s) * ((ll[:, None] * nn[None, :]) % s).astype(F32)
    ch, sh = jnp.cos(ang_hi), jnp.sin(ang_hi)
    cl, sl = jnp.cos(ang_lo), jnp.sin(ang_lo)
    scale = s ** -0.5
    a2 = jnp.concatenate([ch, -sh], axis=1)[:, None, :]
    c2 = jnp.concatenate([sh, ch], axis=1)[:, None, :]
    b2 = jnp.concatenate([cl, cl], axis=1)[None, :, :]
    d2 = jnp.concatenate([sl, sl], axis=1)[None, :, :]
    dft = ((a2 * b2 - c2 * d2) * scale).astype(BF16)
    return bd, dft


def fourier_mix(p3, bd, dft, *, tm):
    b, s, _ = p3.shape
    w = BRANCH_WIDTH
    n_lo = dft.shape[1]
    h = FOURIER_HALO
    assert (s // 2) % tm == 0 and tm % n_lo == 0 and n_lo % h == 0
    r = jnp.arange(tm, dtype=jnp.int32)
    flip = (r[:, None] + r[None, :] == tm - 1).astype(BF16)
    vmem = (s * w * 2 + 2 * w * w * 2 + 2 * (tm + h) * 2 * s * 2 + (tm + h) * 2 * s * 2 + 2 * s * w * 2
            + 2 * s * w * 2 + 8 * (tm + h) * w * 4)
    return pl.pallas_call(
        functools.partial(_fourier_kernel, tm=tm, row_chunk=min(s, 512)),
        grid=(b, s // 2 // tm),
        in_specs=[
            pl.BlockSpec((1, s, w), lambda bi, i: (bi, 0, OFF_A // w), pipeline_mode=pl.Buffered(1)),
            pl.BlockSpec((w, 2 * w), lambda bi, i: (0, 0), pipeline_mode=pl.Buffered(1)),
            pl.BlockSpec((tm // n_lo, n_lo, 2 * s), lambda bi, i: (i, 0, 0)),
            pl.BlockSpec((1, h, 2 * s), lambda bi, i: ((i + 1) * (tm // n_lo), 0, 0)),
            pl.BlockSpec((tm, tm), lambda bi, i: (0, 0)),
        ],
        out_specs=pl.BlockSpec((1, s, w), lambda bi, i: (bi, 0, 0)),
        out_shape=jax.ShapeDtypeStruct((b, s, w), BF16),
        scratch_shapes=[pltpu.VMEM((2 * s, w), BF16), pltpu.VMEM((tm + h, 2 * s), BF16)],
        compiler_params=_params(("parallel", "arbitrary"), vmem),
        name="fourier",
    )(p3, bd, dft, dft, flip)


def rope_tables(s):
    rows = s // GRID_W
    row_id = jnp.repeat(jnp.arange(rows, dtype=F32), GRID_W)
    col_id = jnp.tile(jnp.arange(GRID_W, dtype=F32), rows)
    n_pairs = HEAD_DIM // 4
    inv_freq = ROPE_BASE ** (-jnp.arange(n_pairs, dtype=F32) / n_pairs)
    ang = jnp.concatenate([row_id[:, None] * inv_freq, col_id[:, None] * inv_freq], axis=-1)
    cos, sin = jnp.cos(ang), jnp.sin(ang)
    return jnp.concatenate([cos, cos] * 2, axis=-1), jnp.concatenate([-sin, sin] * 2, axis=-1)


def _norm_rope(x, g, seg, cos2, sin2):
    half = HEAD_DIM // 2
    x2 = x * x
    hi = x2.astype(BF16)
    lo = (x2 - hi.astype(F32)).astype(BF16)
    ms = jnp.dot(hi, seg, preferred_element_type=F32) + jnp.dot(lo, seg, preferred_element_type=F32)
    y = x * lax.rsqrt(ms + EPS) * g
    lane = lax.broadcasted_iota(jnp.int32, (x.shape[0], V7X_LANES), 1)
    first_half = (lane % HEAD_DIM) < half
    outs = []
    for cb in range(x.shape[1] // V7X_LANES):
        yb = y[:, cb * V7X_LANES:(cb + 1) * V7X_LANES]
        rot = jnp.where(first_half, pltpu.roll(yb, V7X_LANES - half, axis=1), pltpu.roll(yb, half, axis=1))
        outs.append(yb * cos2 + rot * sin2)
    return outs[0] if len(outs) == 1 else jnp.concatenate(outs, axis=-1)


def _gqa_prep_kernel(q_ref, kv_ref, gq_ref, gk_ref, seg_ref, cos_ref, sin_ref, qo_ref, ko_ref, vo_ref):
    d = HEAD_DIM
    nk = GQA_KV_HEADS * d
    cos2, sin2 = cos_ref[...], sin_ref[...]
    q = q_ref[...].astype(F32)
    kv = kv_ref[...].astype(F32)
    qo_ref[...] = _norm_rope(q, gq_ref[...], seg_ref[...], cos2, sin2).astype(BF16)
    ko_ref[...] = _norm_rope(kv[:, :nk], gk_ref[...], seg_ref[:nk, :nk], cos2, sin2).astype(BF16)
    v = kv[:, nk:]
    lane = lax.broadcasted_iota(jnp.int32, v.shape, 1)
    ones_col = jnp.where(lane == d, 1.0, 0.0)
    vo_ref[:, :nk] = jnp.where(lane < d, v, ones_col).astype(BF16)
    vo_ref[:, nk:] = jnp.where(lane < d, pltpu.roll(v, d, axis=1), ones_col).astype(BF16)


def gqa_prep(p2, qk_g, cos2, sin2, *, s, tm):
    m = p2.shape[0]
    nq = GQA_Q_HEADS * HEAD_DIM
    nkv = GQA_KV_HEADS * HEAD_DIM
    assert nkv == V7X_LANES
    tiles_per_seq = s // tm
    q_scale = HEAD_DIM ** -0.5 * LOG2E
    gq = jnp.tile(qk_g[0].astype(F32) * q_scale, GQA_Q_HEADS).reshape(1, nq)
    gk = jnp.tile(qk_g[1].astype(F32), GQA_KV_HEADS).reshape(1, nkv)
    seg = jnp.kron(jnp.eye(GQA_Q_HEADS, dtype=F32), jnp.full((HEAD_DIM, HEAD_DIM), 1.0 / HEAD_DIM, F32)).astype(BF16)
    return pl.pallas_call(
        _gqa_prep_kernel,
        grid=(m // tm,),
        in_specs=[
            pl.BlockSpec((tm, nq), lambda i: (i, OFF_BQ // nq)),
            pl.BlockSpec((tm, 2 * nkv), lambda i: (i, OFF_BKV // (2 * nkv))),
            pl.BlockSpec((1, nq), lambda i: (0, 0)),
            pl.BlockSpec((1, nkv), lambda i: (0, 0)),
            pl.BlockSpec((nq, nq), lambda i: (0, 0)),
            pl.BlockSpec((tm, V7X_LANES), lambda i: (i % tiles_per_seq, 0)),
            pl.BlockSpec((tm, V7X_LANES), lambda i: (i % tiles_per_seq, 0)),
        ],
        out_specs=[
            pl.BlockSpec((tm, nq), lambda i: (i, 0)),
            pl.BlockSpec((tm, nkv), lambda i: (i, 0)),
            pl.BlockSpec((tm, 2 * nkv), lambda i: (i, 0)),
        ],
        out_shape=[
            jax.ShapeDtypeStruct((m, nq), BF16),
            jax.ShapeDtypeStruct((m, nkv), BF16),
            jax.ShapeDtypeStruct((m, 2 * nkv), BF16),
        ],
        compiler_params=_params(("parallel",), 16 * tm * nq * 4),
        name="gqa_prep",
    )(p2, p2, gq, gk, seg, cos2, sin2)


def _gqa_attn_kernel(q_ref, kt_ref, v_ref, o_ref, q_sc, m_ref, acc_ref, *, tk):
    d = HEAD_DIM
    tq = q_ref.shape[1]
    grp = q_ref.shape[2] // d
    s = kt_ref.shape[2]
    for g in range(grp):
        q_sc[g * tq:(g + 1) * tq, :] = q_ref[0, :, g * d:(g + 1) * d]

    def qk(item):
        c, g = item
        return jnp.dot(q_sc[g * tq:(g + 1) * tq, :], kt_ref[0, :, c * tk:(c + 1) * tk],
                       preferred_element_type=F32)

    items = [(c, g) for c in range(s // tk) for g in range(grp)]
    sc_next = qk(items[0])
    for n, (c, g) in enumerate(items):
        rows = slice(g * tq, (g + 1) * tq)
        sc = sc_next
        if n + 1 < len(items):
            sc_next = qk(items[n + 1])
        v = v_ref[0, c * tk:(c + 1) * tk, :]
        row_max = jnp.max(sc, axis=-1, keepdims=True)
        if c == 0:
            m_new = jnp.broadcast_to(row_max, (tq, V7X_LANES))
        else:
            m_prev = m_ref[rows, :]
            m_new = jnp.maximum(m_prev, row_max)
        p = jnp.exp2((sc - pltpu.repeat(m_new, tk // V7X_LANES, axis=1)).astype(BF16))
        pv = jnp.dot(p, v, preferred_element_type=F32)
        if c == 0:
            acc_ref[rows, :] = pv
        else:
            acc_ref[rows, :] = acc_ref[rows, :] * jnp.exp2(m_prev - m_new) + pv
        m_ref[rows, :] = m_new
    for g in range(grp):
        acc = acc_ref[g * tq:(g + 1) * tq, :]
        o_ref[0, :, g * d:(g + 1) * d] = (acc[:, :d] / acc[:, d:d + 1]).astype(o_ref.dtype)


def gqa_attention(q3, kt3, v3, *, tq, tk):
    b, s, nq = q3.shape
    d = HEAD_DIM
    grp = GQA_Q_HEADS // GQA_KV_HEADS
    mrows = grp * tq
    vmem = (2 * (tq * grp * d * 2 + d * s * 2 + s * 128 * 2 + tq * grp * d * 2) + 3 * mrows * 128 * 4
            + 12 * tq * tk * 4)
    return pl.pallas_call(
        functools.partial(_gqa_attn_kernel, tk=tk),
        grid=(b, GQA_KV_HEADS, s // tq),
        in_specs=[
            pl.BlockSpec((1, tq, grp * d), lambda bi, kv, i: (bi, i, kv)),
            pl.BlockSpec((1, d, s), lambda bi, kv, i: (bi, kv, 0)),
            pl.BlockSpec((1, s, 2 * d), lambda bi, kv, i: (bi, 0, kv)),
        ],
        out_specs=pl.BlockSpec((1, tq, grp * d), lambda bi, kv, i: (bi, i, kv)),
        out_shape=jax.ShapeDtypeStruct((b, s, nq), BF16),
        scratch_shapes=[pltpu.VMEM((mrows, d), BF16), pltpu.VMEM((mrows, V7X_LANES), F32),
                        pltpu.VMEM((mrows, 2 * d), F32)],
        compiler_params=_params(("parallel", "parallel", "parallel"), vmem),
        name="gqa_attn",
    )(q3, kt3, v3)


def _mlstm_step(chains, ms, q_ref, k_ref, v_ref, r_sc, cm_sc, b_sc, st_sc, h_sc):
    L = MLSTM_CHUNK
    dh = MLSTM_HEAD_DIM
    assert L == dh
    scale = dh ** -0.5
    row_i = lax.broadcasted_iota(jnp.int32, (L, L), 0)
    col_i = lax.broadcasted_iota(jnp.int32, (L, L), 1)

    def col(x_row):
        return jnp.transpose(jnp.broadcast_to(x_row, (L, L)))

    pre = []
    for (hh, direction, c), m in zip(chains, ms):
        off = pl.multiple_of(c * L, L)
        lanes = slice(hh * dh, (hh + 1) * dh)
        q = q_ref[0, pl.ds(off, L), lanes]
        k = k_ref[0, pl.ds(off, L), lanes]
        v = v_ref[0, pl.ds(off, L), lanes]
        r_row = r_sc[hh, direction, pl.ds(c, 1), :]
        cm_row = cm_sc[hh, direction, pl.ds(c, 1), :]
        b_row = b_sc[hh, direction, pl.ds(c, 1), :]
        rmax = jnp.max(r_row, axis=-1, keepdims=True)
        btot = b_row[:, L - 1:L] if direction == 0 else b_row[:, 0:1]
        cmat = jnp.maximum(m, col(cm_row))
        mask = (row_i >= col_i) if direction == 0 else (row_i <= col_i)
        c_last = jnp.maximum(m, rmax)
        w_state = jnp.exp(col(r_row) - c_last) * scale
        pre.append(dict(
            off=off, lanes=lanes, q=q, k=k, v=v,
            w_intra=jnp.where(mask, jnp.exp(r_row - cmat), 0.0) * scale,
            w_inter=jnp.exp(m - cmat),
            den_floor=jnp.exp(-(col(b_row) + cmat)),
            decay=jnp.exp(m - c_last),
            kv_w=jnp.concatenate([w_state * v.astype(F32), w_state], axis=-1).astype(BF16),
            m_new=btot + c_last,
        ))
    s_raw = [lax.dot_general(p["q"], p["k"], (((1,), (1,)), ((), ())), preferred_element_type=F32) for p in pre]
    states = [st_sc[hh, direction] for hh, direction, _ in chains]
    inter = [jnp.dot(p["q"], st.astype(BF16), preferred_element_type=F32) for p, st in zip(pre, states)]
    upd = [lax.dot_general(p["k"], p["kv_w"], (((0,), (0,)), ((), ())), preferred_element_type=F32) for p in pre]
    for n, ((hh, direction, _), p) in enumerate(zip(chains, pre)):
        st_sc[hh, direction] = p["decay"] * states[n] + upd[n]
        v_aug = jnp.concatenate([p["v"], jnp.ones((L, dh), BF16)], axis=-1)
        intra = jnp.dot((s_raw[n] * p["w_intra"]).astype(BF16), v_aug, preferred_element_type=F32)
        h_aug = jnp.concatenate([p["w_inter"], p["w_inter"]], axis=-1) * inter[n] + intra
        h_sc[hh, direction, pl.ds(p["off"], L), :] = h_aug[:, :dh] / jnp.maximum(jnp.abs(h_aug[:, dh:]), p["den_floor"])
    return [p["m_new"] for p in pre]


def _mlstm_kernel(bias_ref, q_ref, k_ref, v_ref, o_ref, gate_ref, g_ref, y_ref,
                  r_sc, cm_sc, b_sc, h_sc, st_sc):
    L = MLSTM_CHUNK
    dh = MLSTM_HEAD_DIM
    hpb = q_ref.shape[2] // dh
    head0 = pl.program_id(1) * hpb
    nc = q_ref.shape[1] // L
    lane = lax.broadcasted_iota(jnp.int32, (nc, L), 1)
    shifts = [1 << t for t in range(int(math.log2(L)))]
    for hh in range(hpb):
        for d in range(2):
            i_pre = gate_ref[0, 2 * d, hh] + bias_ref[2 * d, head0 + hh]
            f_pre = gate_ref[0, 2 * d + 1, hh] + bias_ref[2 * d + 1, head0 + hh]
            logf = jnp.minimum(f_pre, 0.0) - jnp.log1p(jnp.exp(-jnp.abs(f_pre)))
            bc = logf
            for sh in shifts:
                if d == 0:
                    bc = bc + jnp.where(lane >= sh, pltpu.roll(bc, sh, axis=1), 0.0)
                else:
                    bc = bc + jnp.where(lane < L - sh, pltpu.roll(bc, L - sh, axis=1), 0.0)
            r = i_pre - bc
            cm = r
            for sh in shifts:
                if d == 0:
                    cm = jnp.maximum(cm, jnp.where(lane >= sh, pltpu.roll(cm, sh, axis=1), -jnp.inf))
                else:
                    cm = jnp.maximum(cm, jnp.where(lane < L - sh, pltpu.roll(cm, L - sh, axis=1), -jnp.inf))
            r_sc[hh, d] = r
            cm_sc[hh, d] = cm
            b_sc[hh, d] = bc
    st_sc[...] = jnp.zeros(st_sc.shape, F32)

    def body(c, ms):
        chains = [(hh, d, c if d == 0 else nc - 1 - c) for hh in range(hpb) for d in range(2)]
        return tuple(_mlstm_step(chains, ms, q_ref, k_ref, v_ref, r_sc, cm_sc, b_sc, st_sc, h_sc))

    lax.fori_loop(0, nc, body, tuple(jnp.zeros((1, 1), F32) for _ in range(2 * hpb)))
    for hh in range(hpb):
        lanes = slice(hh * dh, (hh + 1) * dh)
        hsum = h_sc[hh, 0] + h_sc[hh, 1]
        ms = jnp.mean(hsum * hsum, axis=-1, keepdims=True)
        y = hsum * lax.rsqrt(ms + EPS) * g_ref[:, lanes]
        y_ref[0, :, lanes] = (jax.nn.sigmoid(o_ref[0, :, lanes].astype(F32)) * y).astype(y_ref.dtype)


def mlstm_branch(p3, gates5, gate_bias, norm_g, *, heads_per_block):
    b, s, _ = p3.shape
    L = MLSTM_CHUNK
    hpb = heads_per_block
    wb = hpb * MLSTM_HEAD_DIM
    nc = s // L
    blk = lambda off: pl.BlockSpec((1, s, wb), lambda bi, h, off=off: (bi, 0, off // wb + h))
    vmem = 2 * 5 * s * wb * 2 + 2 * s * wb * 4 + 6 * hpb * nc * L * 4 + 4 * hpb * wb * wb * 4 + 3 * s * wb * 4
    return pl.pallas_call(
        _mlstm_kernel,
        grid=(b, MLSTM_HEADS // hpb),
        in_specs=[
            pl.BlockSpec(memory_space=pltpu.SMEM),
            blk(OFF_CQ), blk(OFF_CK), blk(OFF_CV), blk(OFF_CO),
            pl.BlockSpec((1, 4, hpb, nc, L), lambda bi, h: (bi, 0, h, 0, 0)),
            pl.BlockSpec((1, wb), lambda bi, h: (0, h)),
        ],
        out_specs=pl.BlockSpec((1, s, wb), lambda bi, h: (bi, 0, h)),
        out_shape=jax.ShapeDtypeStruct((b, s, BRANCH_WIDTH), BF16),
        scratch_shapes=[
            pltpu.VMEM((hpb, 2, nc, L), F32), pltpu.VMEM((hpb, 2, nc, L), F32), pltpu.VMEM((hpb, 2, nc, L), F32),
            pltpu.VMEM((hpb, 2, s, MLSTM_HEAD_DIM), F32),
            pltpu.VMEM((hpb, 2, MLSTM_HEAD_DIM, 2 * MLSTM_HEAD_DIM), F32),
        ],
        compiler_params=_params(("parallel", "parallel"), vmem),
        name="mlstm",
    )(gate_bias.astype(F32), p3, p3, p3, p3, gates5, norm_g.reshape(1, BRANCH_WIDTH).astype(F32))


def _rel_bucket(rel):
    half = REL_BUCKETS // 2
    max_exact = half // 2
    ret = jnp.where(rel > 0, half, 0)
    n = jnp.abs(rel)
    nf = jnp.maximum(n, 1).astype(F32)
    large = max_exact + (jnp.log(nf / max_exact) / math.log(REL_MAX_DIST / max_exact) * (half - max_exact)).astype(jnp.int32)
    large = jnp.minimum(large, half - 1)
    return ret + jnp.where(n < max_exact, n, large)


def diff_bias_tiles(rel_bias, t):
    assert t >= REL_MAX_DIST
    k = jnp.arange(2 * t, dtype=jnp.int32)
    rel = jnp.arange(-2, 3, dtype=jnp.int32)[:, None] * t + jnp.where(k < t, k, k - 2 * t)[None, :]
    onehot = (_rel_bucket(rel)[:, :, None] == jnp.arange(REL_BUCKETS, dtype=jnp.int32)).astype(F32)
    period = jnp.einsum('dkb,bh->hdk', onehot, rel_bias.astype(F32) * LOG2E, precision=lax.Precision.HIGHEST)
    far = period[:, 0::4, 0]
    return period, far


def _diff_attn_kernel(far_ref, q_ref, k_ref, v_ref, period_ref, lam_ref, g_ref, o_ref,
                      q_sc, kt_sc, vaug_sc, bias_sc, sc_buf, m_ref, acc_ref, *, t, lam_init):
    dq = DIFF_QK_DIM
    dv = DIFF_V_DIM
    s = k_ref.shape[1]
    nt = s // t
    head = pl.program_id(1)
    rb = sc_buf.shape[0]
    grp = sc_buf.shape[1] // t
    wrap = (grp - 1) * t
    assert nt % grp == 0 and t % rb == 0 and nt >= 4

    tr = max(t, 512)
    for r in range(0, s, tr):
        kt_sc[:, r:r + tr] = jnp.transpose(k_ref[0, r:r + tr, :].astype(F32)).astype(BF16)
    kt_sc[:, s:] = kt_sc[:, :wrap]
    vaug_sc[:s, :dv] = v_ref[0]
    vaug_sc[s:, :dv] = v_ref[0, :wrap, :]
    lane = lax.broadcasted_iota(jnp.int32, (s + wrap, dv), 1)
    vaug_sc[:, dv:] = jnp.where(lane == 0, 1.0, 0.0).astype(BF16)
    for dl in range(period_ref.shape[1]):
        full = jnp.broadcast_to(period_ref[0, dl:dl + 1, :], (t, 2 * t))
        bias_sc[dl] = pltpu.roll(full, 0, axis=1, stride=1, stride_axis=0)[:, :t]
    q_scale = dq ** -0.5 * LOG2E
    q_all = (q_ref[0].astype(F32) * q_scale).astype(BF16)
    q_sc[0] = q_all[:, :dq]
    q_sc[1] = q_all[:, dq:]
    lp = lam_ref[...]
    lam = (jnp.exp(jnp.sum(lp[0:1] * lp[1:2], axis=-1, keepdims=True))
           - jnp.exp(jnp.sum(lp[2:3] * lp[3:4], axis=-1, keepdims=True)) + lam_init)

    def key_tile(i, delta):
        j = lax.rem(i + delta, nt)
        return j, pl.multiple_of(j * t, t)

    def qk(i, item):
        pair, mp, hf = item
        row0 = pl.multiple_of(i * t + hf * rb, rb)
        _, off = key_tile(i, grp * pair)
        return jnp.dot(q_sc[mp, pl.ds(row0, rb), :], kt_sc[mp * dq:(mp + 1) * dq, pl.ds(off, grp * t)],
                       preferred_element_type=F32)

    items = [(pair, mp, hf) for pair in range(nt // grp) for mp in range(2) for hf in range(t // rb)]
    sc_buf[...] = qk(0, items[0])

    def tile_body(i, carry):
        sc_next = sc_buf[...]
        for n, (pair, mp, hf) in enumerate(items):
            sc = sc_next
            if n + 1 < len(items):
                sc_next = qk(i, items[n + 1])
            else:
                sc_next = qk(jnp.minimum(i + 1, nt - 1), items[0])
            rows = slice(hf * rb, (hf + 1) * rb)
            scs = [sc[:, p * t:(p + 1) * t] for p in range(grp)]
            _, off = key_tile(i, grp * pair)
            shifts, row_max = [], None
            for piece, delta in enumerate(range(grp * pair, grp * pair + grp)):
                j, _ = key_tile(i, delta)
                if delta in (0, 1, nt - 1):
                    tile = 2 if delta == 0 else jnp.clip(j - i, -2, 2) + 2
                    scs[piece] = scs[piece] + bias_sc[tile, rows, :]
                    shifts.append(None)
                    rm = jnp.max(scs[piece], axis=-1, keepdims=True)
                else:
                    shifts.append(jnp.where(j > i, far_ref[head, 1], far_ref[head, 0]))
                    rm = jnp.max(scs[piece], axis=-1, keepdims=True) + shifts[-1]
                row_max = rm if row_max is None else jnp.maximum(row_max, rm)
            if pair == 0:
                m_new = jnp.broadcast_to(row_max, (rb, V7X_LANES))
            else:
                m_prev = m_ref[mp, rows, :]
                m_new = jnp.maximum(m_prev, row_max)
            ps = []
            for piece in range(grp):
                m_sub = m_new if shifts[piece] is None else m_new - shifts[piece]
                ps.append(jnp.exp2((scs[piece] - pltpu.repeat(m_sub, t // V7X_LANES, axis=1)).astype(BF16)))
            pv = jnp.dot(jnp.concatenate(ps, axis=1), vaug_sc[pl.ds(off, grp * t), :], preferred_element_type=F32)
            if pair == 0:
                acc_ref[mp, rows, :] = pv
            else:
                alpha = jnp.exp2(m_prev - m_new)
                acc_ref[mp, rows, :] = acc_ref[mp, rows, :] * pltpu.repeat(alpha, 2 * dv // V7X_LANES, axis=1) + pv
            m_ref[mp, rows, :] = m_new
        sc_buf[...] = sc_next
        o0 = acc_ref[0, :, :dv] / acc_ref[0, :, dv:dv + 1]
        o1 = acc_ref[1, :, :dv] / acc_ref[1, :, dv:dv + 1]
        o = o0 - lam * o1
        ms = jnp.mean(o * o, axis=-1, keepdims=True)
        row0 = pl.multiple_of(i * t, t)
        o_ref[0, pl.ds(row0, t), :] = (o * lax.rsqrt(ms + EPS) * g_ref[...] * (1.0 - lam_init)).astype(o_ref.dtype)
        return carry

    lax.fori_loop(0, nt, tile_body, 0)


def diff_attention(p3, bias_period, bias_far, lam_params, sub_g, *, t, rows_per_item, tiles_per_item, layer_number):
    b, s, _ = p3.shape
    dv = DIFF_V_DIM
    n_off = bias_period.shape[1]
    assert s // t >= 4, "tiles 2 .. s/t-2 steps away from the query tile must all be beyond REL_MAX_DIST"
    lam_init = 0.8 - 0.6 * math.exp(-0.3 * (layer_number - 1))
    sw = s + (tiles_per_item - 1) * t
    item = rows_per_item * tiles_per_item * t
    vmem = (2 * (s * 128 * 2 + 128 * s * 2 + s * dv * 2 + s * dv * 2) + n_off * t * t * 4 + sw * (2 * dv + 128) * 2
            + 2 * s * 128 * 2 + item * 4 + 2 * t * 128 * 4 * 3 + 8 * item * 4)
    return pl.pallas_call(
        functools.partial(_diff_attn_kernel, t=t, lam_init=lam_init),
        grid=(b, DIFF_HEADS),
        in_specs=[
            pl.BlockSpec(memory_space=pltpu.SMEM),
            pl.BlockSpec((1, s, 128), lambda bi, h: (bi, 0, OFF_DQ // 128 + h)),
            pl.BlockSpec((1, s, 128), lambda bi, h: (bi, 0, OFF_DK // 128 + h)),
            pl.BlockSpec((1, s, dv), lambda bi, h: (bi, 0, OFF_DV // dv + h)),
            pl.BlockSpec((1, n_off, 2 * t), lambda bi, h: (h, 0, 0)),
            pl.BlockSpec((4, DIFF_QK_DIM), lambda bi, h: (0, 0)),
            pl.BlockSpec((1, dv), lambda bi, h: (0, 0)),
        ],
        out_specs=pl.BlockSpec((1, s, dv), lambda bi, h: (bi, 0, h)),
        out_shape=jax.ShapeDtypeStruct((b, s, BRANCH_WIDTH), BF16),
        scratch_shapes=[pltpu.VMEM((2, s, DIFF_QK_DIM), BF16), pltpu.VMEM((2 * DIFF_QK_DIM, sw), BF16),
                        pltpu.VMEM((sw, 2 * dv), BF16),
                        pltpu.VMEM((n_off, t, t), F32), pltpu.VMEM((rows_per_item, tiles_per_item * t), F32),
                        pltpu.VMEM((2, t, V7X_LANES), F32), pltpu.VMEM((2, t, 2 * dv), F32)],
        compiler_params=_params(("parallel", "parallel"), vmem),
        name="diff_attn",
    )(bias_far, p3, p3, p3, bias_period, lam_params.astype(F32), sub_g.reshape(1, dv).astype(F32))


def _merge_kernel(ya_ref, yb_ref, yc_ref, yd_ref, gate_ref, wb_ref, wo_ref, x_ref, o_ref, wb_sc, wo_sc):
    dm = x_ref.shape[1]

    @pl.when(pl.program_id(0) == 0)
    def _():
        for n in range(wb_ref.shape[0]):
            wb_sc[n] = wb_ref[n].astype(BF16)
        wo_sc[...] = wo_ref[...].astype(BF16)

    merged = None
    for n, y_ref in enumerate((ya_ref, yb_ref, yc_ref, yd_ref)):
        br = jnp.dot(y_ref[...], wb_sc[n], preferred_element_type=F32)
        term = jax.nn.sigmoid(gate_ref[:, n * dm:(n + 1) * dm].astype(F32)) * br
        merged = term if merged is None else merged + term
    o_ref[...] = x_ref[...] + jnp.dot(merged.astype(BF16), wo_sc[...], preferred_element_type=F32)


def merge_branches(ys, p2, w_branch, w_out, x2, *, tm):
    m, dm = x2.shape
    w = BRANCH_WIDTH
    y_spec = pl.BlockSpec((tm, w), lambda i: (i, 0))
    vmem = (2 * (4 * tm * w * 2 + tm * 4 * dm * 2 + 2 * tm * dm * 4) + (4 * w * dm + dm * dm) * (4 + 2)
            + 6 * tm * dm * 4)
    return pl.pallas_call(
        _merge_kernel,
        grid=(m // tm,),
        in_specs=[
            y_spec, y_spec, y_spec, y_spec,
            pl.BlockSpec((tm, N_BRANCHES * dm), lambda i: (i, OFF_GATES // (N_BRANCHES * dm))),
            pl.BlockSpec((N_BRANCHES, w, dm), lambda i: (0, 0, 0), pipeline_mode=pl.Buffered(1)),
            pl.BlockSpec((dm, dm), lambda i: (0, 0), pipeline_mode=pl.Buffered(1)),
            pl.BlockSpec((tm, dm), lambda i: (i, 0)),
        ],
        out_specs=pl.BlockSpec((tm, dm), lambda i: (i, 0)),
        out_shape=jax.ShapeDtypeStruct((m, dm), F32),
        scratch_shapes=[pltpu.VMEM((N_BRANCHES, w, dm), BF16), pltpu.VMEM((dm, dm), BF16)],
        compiler_params=_params(("arbitrary",), vmem),
        name="merge",
    )(*ys, p2, w_branch.astype(F32), w_out.astype(F32), x2)


HALO_ROWS = 16


def _ffn_up_kernel(x_ref, xp_ref, xn_ref, g_ref, wa_ref, wl_ref, cw_ref, cb_ref, o_ref, xe_sc,
                   *, tiles_per_seq, col_chunk):
    tm, f = o_ref.shape
    h = HALO_ROWS
    pos = pl.program_id(0) % tiles_per_seq

    def norm(x):
        ms = jnp.mean(x * x, axis=-1, keepdims=True)
        return x * lax.rsqrt(ms + EPS) * g_ref[...]

    xe_sc[:h] = jnp.where(pos == 0, 0.0, norm(xp_ref[...])).astype(BF16)
    xe_sc[h:h + tm] = norm(x_ref[...]).astype(BF16)
    xe_sc[h + tm:] = jnp.where(pos == tiles_per_seq - 1, 0.0, norm(xn_ref[...])).astype(BF16)
    k0 = math.sqrt(2.0 / math.pi)
    bounds = list(range(0, f, col_chunk)) + [f]
    chunks = [slice(lo, hi) for lo, hi in zip(bounds[:-1], bounds[1:])]

    def project(cols):
        a_ext = jnp.dot(xe_sc[...], wa_ref[:, cols], preferred_element_type=F32)
        lin = jnp.dot(xe_sc[h:h + tm, :], wl_ref[:, cols], preferred_element_type=F32)
        return a_ext, lin

    nxt = project(chunks[0])
    for n, cols in enumerate(chunks):
        a_ext, lin = nxt
        if n + 1 < len(chunks):
            nxt = project(chunks[n + 1])
        cw = cw_ref[:, cols]
        c = (a_ext[h - 1:h - 1 + tm] * cw[0:1] + a_ext[h:h + tm] * cw[1:2] + a_ext[h + 1:h + 1 + tm] * cw[2:3]
             + cb_ref[:, cols])
        t = jnp.tanh(c * ((c * c) * (k0 * 0.044715) + k0))
        o_ref[:, cols] = ((c + c * t) * lin).astype(o_ref.dtype)


def ffn_up(x2, g, w_up, conv_w, conv_b, *, s, tm):
    m, dm = x2.shape
    f = w_up.shape[1] // 2
    tiles_per_seq = s // tm
    hb = tm // HALO_ROWS
    n_halo = m // HALO_ROWS
    vmem = 2 * (tm * dm * 4 + tm * f * 2) + 2 * dm * f * 2 + (tm + 2 * HALO_ROWS) * dm * 2 + 12 * tm * 512 * 4
    return pl.pallas_call(
        functools.partial(_ffn_up_kernel, tiles_per_seq=tiles_per_seq, col_chunk=512),
        grid=(m // tm,),
        in_specs=[
            pl.BlockSpec((tm, dm), lambda i: (i, 0)),
            pl.BlockSpec((HALO_ROWS, dm), lambda i: (jnp.maximum(i * hb - 1, 0), 0)),
            pl.BlockSpec((HALO_ROWS, dm), lambda i: (jnp.minimum((i + 1) * hb, n_halo - 1), 0)),
            pl.BlockSpec((1, dm), lambda i: (0, 0)),
            pl.BlockSpec((dm, f), lambda i: (0, 0), pipeline_mode=pl.Buffered(1)),
            pl.BlockSpec((dm, f), lambda i: (0, 1), pipeline_mode=pl.Buffered(1)),
            pl.BlockSpec((CONV_W, f), lambda i: (0, 0)),
            pl.BlockSpec((1, f), lambda i: (0, 0)),
        ],
        out_specs=pl.BlockSpec((tm, f), lambda i: (i, 0)),
        out_shape=jax.ShapeDtypeStruct((m, f), BF16),
        scratch_shapes=[pltpu.VMEM((tm + 2 * HALO_ROWS, dm), BF16)],
        compiler_params=_params(("parallel",), vmem),
        name="ffn_up",
    )(x2, x2, x2, g.reshape(1, dm).astype(F32), w_up, w_up, conv_w.astype(F32), conv_b.reshape(1, f).astype(F32))


def _ffn_down_kernel(h_ref, wd_ref, x_ref, gf_ref, o_ref, wd_sc, *, final_norm):
    @pl.when(pl.program_id(0) == 0)
    def _():
        wd_sc[...] = wd_ref[...].astype(BF16)

    y = x_ref[...] + jnp.dot(h_ref[...], wd_sc[...], preferred_element_type=F32)
    if final_norm:
        ms = jnp.mean(y * y, axis=-1, keepdims=True)
        y = y * lax.rsqrt(ms + EPS) * gf_ref[...]
    o_ref[...] = y


def ffn_down(hmid, w_down, x2, final_g, *, tm, final_norm):
    m, dm = x2.shape
    f = w_down.shape[0]
    vmem = 2 * (tm * f * 2 + 2 * tm * dm * 4) + f * dm * 4 + f * dm * 2 + 4 * tm * dm * 4
    return pl.pallas_call(
        functools.partial(_ffn_down_kernel, final_norm=final_norm),
        grid=(m // tm,),
        in_specs=[
            pl.BlockSpec((tm, f), lambda i: (i, 0)),
            pl.BlockSpec((f, dm), lambda i: (0, 0), pipeline_mode=pl.Buffered(1)),
            pl.BlockSpec((tm, dm), lambda i: (i, 0)),
            pl.BlockSpec((1, dm), lambda i: (0, 0)),
        ],
        out_specs=pl.BlockSpec((tm, dm), lambda i: (i, 0)),
        out_shape=jax.ShapeDtypeStruct((m, dm), F32),
        scratch_shapes=[pltpu.VMEM((f, dm), BF16)],
        compiler_params=_params(("arbitrary",), vmem),
        name="ffn_down",
    )(hmid, w_down.astype(F32), x2, final_g.reshape(1, dm).astype(F32))


def _arrange_w_in(w):
    widths = (512, 512, 128, 128, 512, 512, 512, 512, 16, 512, 512, 512, 4096)
    offs = [0]
    for wd in widths:
        offs.append(offs[-1] + wd)
    seg = lambda n: w[:, offs[n]:offs[n + 1]]
    a, bq, bk, bv, cq, ck, cv, co, cgate, dq, dk, dv, gates = (seg(n) for n in range(len(widths)))
    main = jnp.concatenate([gates, a, bq, cq, ck, cv, co, dq, dk, dv, bk, bv], axis=1).astype(BF16)
    gate = jnp.pad(cgate, ((0, 0), (0, V7X_LANES - cgate.shape[1]))).astype(BF16)
    return main, gate


def kernel(x, norm_mix_g, w_in, mlstm_gate_bias, qk_norm_g, mlstm_norm_g, diff_lambda, diff_norm_g, rel_bias,
           w_branch, w_out, norm_ffn_g, w_up, conv_w, conv_b, w_down, final_norm_g):
    b, s, dm = x.shape
    depth = w_in.shape[0]
    m = b * s
    d_ff = w_down.shape[1]
    L = MLSTM_CHUNK
    t_diff = 512

    bd, dft = fourier_tables(s)
    cos2, sin2 = rope_tables(s)
    bias_tiles, bias_far = diff_bias_tiles(rel_bias, t_diff)

    x2 = x.reshape(m, dm)
    for layer in range(depth):
        w_main, w_gate = _arrange_w_in(w_in[layer])
        p2, cgate = norm_matmul(x2, norm_mix_g[layer], w_main, w_gate, tm=1024, tn=1280)
        p3 = p2.reshape(b, s, P_WIDTH)

        y_a = fourier_mix(p3, bd, dft, tm=512)

        q_b, k_b, v_b = gqa_prep(p2, qk_norm_g[layer], cos2, sin2, s=s, tm=512)
        kt_b = jnp.swapaxes(k_b.reshape(b, s, GQA_KV_HEADS * HEAD_DIM), 1, 2)
        y_b = gqa_attention(q_b.reshape(b, s, -1), kt_b, v_b.reshape(b, s, -1), tq=512, tk=1024)

        gates5 = jnp.transpose(cgate[:, :4 * MLSTM_HEADS].reshape(b, s, 4, MLSTM_HEADS), (0, 2, 3, 1))
        gates5 = gates5.reshape(b, 4, MLSTM_HEADS, s // L, L)
        y_c = mlstm_branch(p3, gates5, mlstm_gate_bias[layer], mlstm_norm_g[layer], heads_per_block=2)

        y_d = diff_attention(p3, bias_tiles, bias_far, diff_lambda[layer], diff_norm_g[layer],
                             t=t_diff, rows_per_item=256, tiles_per_item=4, layer_number=layer + 1)

        ys = [y.reshape(m, BRANCH_WIDTH) for y in (y_a, y_b, y_c, y_d)]
        x2 = merge_branches(ys, p2, w_branch[layer], w_out[layer], x2, tm=512)

        half_lin = jnp.concatenate([jnp.ones((d_ff,), F32), jnp.full((d_ff,), 0.5, F32)])
        hmid = ffn_up(x2, norm_ffn_g[layer], (w_up[layer] * half_lin).astype(BF16), conv_w[layer], conv_b[layer],
                      s=s, tm=1024)
        x2 = ffn_down(hmid, w_down[layer], x2, final_norm_g, tm=512, final_norm=(layer == depth - 1))
    return x2.reshape(b, s, dm)
```

```python
import functools
import math

import jax
import jax.numpy as jnp
from jax import lax
from jax.experimental import pallas as pl
from jax.experimental.pallas import tpu as pltpu

F32 = jnp.float32
BF16 = jnp.bfloat16

GRID_W = 64
HEAD_DIM = 64
BRANCH_WIDTH = 512
N_BRANCHES = 4
FOURIER_GROUP_DIM = 64
GQA_Q_HEADS = 8
GQA_KV_HEADS = 2
MLSTM_HEADS = 4
MLSTM_HEAD_DIM = 128
MLSTM_CHUNK = 128
DIFF_HEADS = 4
DIFF_QK_DIM = 64
DIFF_V_DIM = 128
REL_BUCKETS = 32
REL_MAX_DIST = 128
CONV_W = 3
ROPE_BASE = 10000.0
EPS = 1e-6
LOG2E = math.log2(math.e)

V7X_LANES = 128
V7X_VMEM_BYTES = 64 * 1024 * 1024
V7X_VMEM_CAP = V7X_VMEM_BYTES - 8 * 1024 * 1024

OFF_GATES = 0
OFF_A = 4096
OFF_BQ = 4608
OFF_CQ = 5120
OFF_CK = 5632
OFF_CV = 6144
OFF_CO = 6656
OFF_DQ = 7168
OFF_DK = 7680
OFF_DV = 8192
OFF_BKV = 8704
P_WIDTH = 8960


def _params(sem, vmem_bytes):
    limit = int(min(max(vmem_bytes * 3 // 2 + (4 << 20), 16 << 20), V7X_VMEM_CAP))
    return pltpu.CompilerParams(dimension_semantics=sem, vmem_limit_bytes=limit)


def _norm_mm_kernel(x_ref, g_ref, w_ref, o_ref, xn_ref):
    @pl.when(pl.program_id(1) == 0)
    def _():
        x = x_ref[...]
        ms = jnp.mean(x * x, axis=-1, keepdims=True)
        xn_ref[...] = (x * lax.rsqrt(ms + EPS) * g_ref[...]).astype(BF16)

    o_ref[...] = jnp.dot(xn_ref[...], w_ref[...], preferred_element_type=F32).astype(o_ref.dtype)


def _norm_mm_gate_kernel(x_ref, g_ref, w_ref, wg_ref, o_ref, og_ref, xn_ref):
    @pl.when(pl.program_id(1) == 0)
    def _():
        x = x_ref[...]
        ms = jnp.mean(x * x, axis=-1, keepdims=True)
        xn = (x * lax.rsqrt(ms + EPS) * g_ref[...]).astype(BF16)
        xn_ref[...] = xn
        og_ref[...] = jnp.dot(xn, wg_ref[...], preferred_element_type=F32)

    o_ref[...] = jnp.dot(xn_ref[...], w_ref[...], preferred_element_type=F32).astype(o_ref.dtype)


def norm_matmul(x, g, w, w_gate=None, *, tm, tn):
    m, k = x.shape
    n = w.shape[1]
    grid = (m // tm, n // tn)
    vmem = 2 * tm * k * 4 + tm * k * 2 + 2 * k * tn * 2 + 2 * tm * tn * 2 + 4 * tm * k
    x_spec = pl.BlockSpec((tm, k), lambda i, j: (i, 0))
    g_spec = pl.BlockSpec((1, k), lambda i, j: (0, 0))
    w_spec = pl.BlockSpec((k, tn), lambda i, j: (0, j))
    o_spec = pl.BlockSpec((tm, tn), lambda i, j: (i, j))
    scratch = [pltpu.VMEM((tm, k), BF16)]
    g2 = g.reshape(1, k).astype(F32)
    if w_gate is None:
        return pl.pallas_call(
            _norm_mm_kernel,
            grid=grid,
            in_specs=[x_spec, g_spec, w_spec],
            out_specs=o_spec,
            out_shape=jax.ShapeDtypeStruct((m, n), BF16),
            scratch_shapes=scratch,
            compiler_params=_params(("parallel", "arbitrary"), vmem),
            name="norm_matmul",
        )(x, g2, w)
    ng = w_gate.shape[1]
    return pl.pallas_call(
        _norm_mm_gate_kernel,
        grid=grid,
        in_specs=[x_spec, g_spec, w_spec, pl.BlockSpec((k, ng), lambda i, j: (0, 0))],
        out_specs=[o_spec, pl.BlockSpec((tm, ng), lambda i, j: (i, 0))],
        out_shape=[jax.ShapeDtypeStruct((m, n), BF16), jax.ShapeDtypeStruct((m, ng), F32)],
        scratch_shapes=scratch,
        compiler_params=_params(("parallel", "arbitrary"), vmem),
        name="norm_matmul_gate",
    )(x, g2, w, w_gate)


FOURIER_HALO = 16


def _fourier_kernel(a_ref, bd_ref, dft_ref, halo_ref, flip_ref, o_ref, z_ref, ext_ref, *, tm, row_chunk):
    s = a_ref.shape[1]
    w = a_ref.shape[2]
    i = pl.program_id(1)

    @pl.when(i == 0)
    def _():
        for r in range(0, s, row_chunk):
            a = a_ref[0, r:r + row_chunk, :]
            zc = jnp.dot(a, bd_ref[...], preferred_element_type=F32)
            z_ref[r:r + row_chunk, :] = zc[:, :w].astype(BF16)
            z_ref[s + r:s + r + row_chunk, :] = zc[:, w:].astype(BF16)

    ext_ref[:tm] = dft_ref[...].reshape(tm, 2 * s)
    ext_ref[tm:] = halo_ref[0]
    p = jnp.dot(ext_ref[:, :s], z_ref[:s], preferred_element_type=F32)
    q = jnp.dot(ext_ref[:, s:], z_ref[s:], preferred_element_type=F32)
    lo = pl.multiple_of(i * tm, tm)
    o_ref[0, pl.ds(lo, tm), :] = (p[:tm] + q[:tm]).astype(o_ref.dtype)
    mirrored = (p[1:tm + 1] - q[1:tm + 1]).astype(BF16)
    hi = pl.multiple_of(s - (i + 1) * tm, tm)
    o_ref[0, pl.ds(hi, tm), :] = jnp.dot(flip_ref[...], mirrored, preferred_element_type=F32).astype(o_ref.dtype)


def fourier_tables(s):
    cg = FOURIER_GROUP_DIM
    jj = jnp.arange(cg, dtype=jnp.int32)
    ang_c = (2.0 * math.pi / cg) * ((jj[:, None] * jj[None, :]) % cg).astype(F32)
    eye_g = jnp.eye(BRANCH_WIDTH // cg, dtype=F32)
    bd_c = jnp.kron(eye_g, jnp.cos(ang_c)) * cg ** -0.5
    bd_s = jnp.kron(eye_g, jnp.sin(ang_c)) * cg ** -0.5
    bd = jnp.concatenate([bd_c, bd_s], axis=1).astype(BF16)
    n_lo = s // cg
    nn = jnp.arange(s, dtype=jnp.int32)
    k1 = jnp.arange(cg // 2 + 1, dtype=jnp.int32)
    ang_hi = (2.0 * math.pi / cg) * ((k1[:, None] * nn[None, :]) % cg).astype(F32)
    ll = jnp.arange(n_lo, dtype=jnp.int32)
    ang_lo = (2.0 * math.pi / s) * ((ll[:, None] * nn[None, :]) % s).astype(F32)
    ch, sh = jnp.cos(ang_hi), jnp.sin(ang_hi)
    cl, sl = jnp.cos(ang_lo), jnp.sin(ang_lo)
    scale = s ** -0.5
    a2 = jnp.concatenate([ch, -sh], axis=1)[:, None, :]
    c2 = jnp.concatenate([sh, ch], axis=1)[:, None, :]
    b2 = jnp.concatenate([cl, cl], axis=1)[None, :, :]
    d2 = jnp.concatenate([sl, sl], axis=1)[None, :, :]
    dft = ((a2 * b2 - c2 * d2) * scale).astype(BF16)
    return bd, dft


def fourier_mix(p3, bd, dft, *, tm):
    b, s, _ = p3.shape
    w = BRANCH_WIDTH
    n_lo = dft.shape[1]
    h = FOURIER_HALO
    assert (s // 2) % tm == 0 and tm % n_lo == 0 and n_lo % h == 0
    r = jnp.arange(tm, dtype=jnp.int32)
    flip = (r[:, None] + r[None, :] == tm - 1).astype(BF16)
    vmem = (s * w * 2 + 2 * w * w * 2 + 2 * (tm + h) * 2 * s * 2 + (tm + h) * 2 * s * 2 + 2 * s * w * 2
            + 2 * s * w * 2 + 8 * (tm + h) * w * 4)
    return pl.pallas_call(
        functools.partial(_fourier_kernel, tm=tm, row_chunk=min(s, 512)),
        grid=(b, s // 2 // tm),
        in_specs=[
            pl.BlockSpec((1, s, w), lambda bi, i: (bi, 0, OFF_A // w), pipeline_mode=pl.Buffered(1)),
            pl.BlockSpec((w, 2 * w), lambda bi, i: (0, 0), pipeline_mode=pl.Buffered(1)),
            pl.BlockSpec((tm // n_lo, n_lo, 2 * s), lambda bi, i: (i, 0, 0)),
            pl.BlockSpec((1, h, 2 * s), lambda bi, i: ((i + 1) * (tm // n_lo), 0, 0)),
            pl.BlockSpec((tm, tm), lambda bi, i: (0, 0)),
        ],
        out_specs=pl.BlockSpec((1, s, w), lambda bi, i: (bi, 0, 0)),
        out_shape=jax.ShapeDtypeStruct((b, s, w), BF16),
        scratch_shapes=[pltpu.VMEM((2 * s, w), BF16), pltpu.VMEM((tm + h, 2 * s), BF16)],
        compiler_params=_params(("parallel", "arbitrary"), vmem),
        name="fourier",
    )(p3, bd, dft, dft, flip)


def rope_tables(s):
    rows = s // GRID_W
    row_id = jnp.repeat(jnp.arange(rows, dtype=F32), GRID_W)
    col_id = jnp.tile(jnp.arange(GRID_W, dtype=F32), rows)
    n_pairs = HEAD_DIM // 4
    inv_freq = ROPE_BASE ** (-jnp.arange(n_pairs, dtype=F32) / n_pairs)
    ang = jnp.concatenate([row_id[:, None] * inv_freq, col_id[:, None] * inv_freq], axis=-1)
    cos, sin = jnp.cos(ang), jnp.sin(ang)
    return jnp.concatenate([cos, cos] * 2, axis=-1), jnp.concatenate([-sin, sin] * 2, axis=-1)


def _norm_rope(x, g, seg, cos2, sin2):
    half = HEAD_DIM // 2
    x2 = x * x
    hi = x2.astype(BF16)
    lo = (x2 - hi.astype(F32)).astype(BF16)
    ms = jnp.dot(hi, seg, preferred_element_type=F32) + jnp.dot(lo, seg, preferred_element_type=F32)
    y = x * lax.rsqrt(ms + EPS) * g
    lane = lax.broadcasted_iota(jnp.int32, (x.shape[0], V7X_LANES), 1)
    first_half = (lane % HEAD_DIM) < half
    outs = []
    for cb in range(x.shape[1] // V7X_LANES):
        yb = y[:, cb * V7X_LANES:(cb + 1) * V7X_LANES]
        rot = jnp.where(first_half, pltpu.roll(yb, V7X_LANES - half, axis=1), pltpu.roll(yb, half, axis=1))
        outs.append(yb * cos2 + rot * sin2)
    return outs[0] if len(outs) == 1 else jnp.concatenate(outs, axis=-1)


def _gqa_prep_kernel(q_ref, kv_ref, gq_ref, gk_ref, seg_ref, cos_ref, sin_ref, qo_ref, ko_ref, vo_ref):
    d = HEAD_DIM
    nk = GQA_KV_HEADS * d
    cos2, sin2 = cos_ref[...], sin_ref[...]
    q = q_ref[...].astype(F32)
    kv = kv_ref[...].astype(F32)
    qo_ref[...] = _norm_rope(q, gq_ref[...], seg_ref[...], cos2, sin2).astype(BF16)
    ko_ref[...] = _norm_rope(kv[:, :nk], gk_ref[...], seg_ref[:nk, :nk], cos2, sin2).astype(BF16)
    v = kv[:, nk:]
    lane = lax.broadcasted_iota(jnp.int32, v.shape, 1)
    ones_col = jnp.where(lane == d, 1.0, 0.0)
    vo_ref[:, :nk] = jnp.where(lane < d, v, ones_col).astype(BF16)
    vo_ref[:, nk:] = jnp.where(lane < d, pltpu.roll(v, d, axis=1), ones_col).astype(BF16)


def gqa_prep(p2, qk_g, cos2, sin2, *, s, tm):
    m = p2.shape[0]
    nq = GQA_Q_HEADS * HEAD_DIM
    nkv = GQA_KV_HEADS * HEAD_DIM
    assert nkv == V7X_LANES
    tiles_per_seq = s // tm
    q_scale = HEAD_DIM ** -0.5 * LOG2E
    gq = jnp.tile(qk_g[0].astype(F32) * q_scale, GQA_Q_HEADS).reshape(1, nq)
    gk = jnp.tile(qk_g[1].astype(F32), GQA_KV_HEADS).reshape(1, nkv)
    seg = jnp.kron(jnp.eye(GQA_Q_HEADS, dtype=F32), jnp.full((HEAD_DIM, HEAD_DIM), 1.0 / HEAD_DIM, F32)).astype(BF16)
    return pl.pallas_call(
        _gqa_prep_kernel,
        grid=(m // tm,),
        in_specs=[
            pl.BlockSpec((tm, nq), lambda i: (i, OFF_BQ // nq)),
            pl.BlockSpec((tm, 2 * nkv), lambda i: (i, OFF_BKV // (2 * nkv))),
            pl.BlockSpec((1, nq), lambda i: (0, 0)),
            pl.BlockSpec((1, nkv), lambda i: (0, 0)),
            pl.BlockSpec((nq, nq), lambda i: (0, 0)),
            pl.BlockSpec((tm, V7X_LANES), lambda i: (i % tiles_per_seq, 0)),
            pl.BlockSpec((tm, V7X_LANES), lambda i: (i % tiles_per_seq, 0)),
        ],
        out_specs=[
            pl.BlockSpec((tm, nq), lambda i: (i, 0)),
            pl.BlockSpec((tm, nkv), lambda i: (i, 0)),
            pl.BlockSpec((tm, 2 * nkv), lambda i: (i, 0)),
        ],
        out_shape=[
            jax.ShapeDtypeStruct((m, nq), BF16),
            jax.ShapeDtypeStruct((m, nkv), BF16),
            jax.ShapeDtypeStruct((m, 2 * nkv), BF16),
        ],
        compiler_params=_params(("parallel",), 16 * tm * nq * 4),
        name="gqa_prep",
    )(p2, p2, gq, gk, seg, cos2, sin2)


def _gqa_attn_kernel(q_ref, kt_ref, v_ref, o_ref, q_sc, m_ref, acc_ref, *, tk):
    d = HEAD_DIM
    tq = q_ref.shape[1]
    grp = q_ref.shape[2] // d
    s = kt_ref.shape[2]
    for g in range(grp):
        q_sc[g * tq:(g + 1) * tq, :] = q_ref[0, :, g * d:(g + 1) * d]

    def qk(item):
        c, g = item
        return jnp.dot(q_sc[g * tq:(g + 1) * tq, :], kt_ref[0, :, c * tk:(c + 1) * tk],
                       preferred_element_type=F32)

    items = [(c, g) for c in range(s // tk) for g in range(grp)]
    sc_next = qk(items[0])
    for n, (c, g) in enumerate(items):
        rows = slice(g * tq, (g + 1) * tq)
        sc = sc_next
        if n + 1 < len(items):
            sc_next = qk(items[n + 1])
        v = v_ref[0, c * tk:(c + 1) * tk, :]
        row_max = jnp.max(sc, axis=-1, keepdims=True)
        if c == 0:
            m_new = jnp.broadcast_to(row_max, (tq, V7X_LANES))
        else:
            m_prev = m_ref[rows, :]
            m_new = jnp.maximum(m_prev, row_max)
        p = jnp.exp2((sc - pltpu.repeat(m_new, tk // V7X_LANES, axis=1)).astype(BF16))
        pv = jnp.dot(p, v, preferred_element_type=F32)
        if c == 0:
            acc_ref[rows, :] = pv
        else:
            acc_ref[rows, :] = acc_ref[rows, :] * jnp.exp2(m_prev - m_new) + pv
        m_ref[rows, :] = m_new
    for g in range(grp):
        acc = acc_ref[g * tq:(g + 1) * tq, :]
        o_ref[0, :, g * d:(g + 1) * d] = (acc[:, :d] / acc[:, d:d + 1]).astype(o_ref.dtype)


def gqa_attention(q3, kt3, v3, *, tq, tk):
    b, s, nq = q3.shape
    d = HEAD_DIM
    grp = GQA_Q_HEADS // GQA_KV_HEADS
    mrows = grp * tq
    vmem = (2 * (tq * grp * d * 2 + d * s * 2 + s * 128 * 2 + tq * grp * d * 2) + 3 * mrows * 128 * 4
            + 12 * tq * tk * 4)
    return pl.pallas_call(
        functools.partial(_gqa_attn_kernel, tk=tk),
        grid=(b, GQA_KV_HEADS, s // tq),
        in_specs=[
            pl.BlockSpec((1, tq, grp * d), lambda bi, kv, i: (bi, i, kv)),
            pl.BlockSpec((1, d, s), lambda bi, kv, i: (bi, kv, 0)),
            pl.BlockSpec((1, s, 2 * d), lambda bi, kv, i: (bi, 0, kv)),
        ],
        out_specs=pl.BlockSpec((1, tq, grp * d), lambda bi, kv, i: (bi, i, kv)),
        out_shape=jax.ShapeDtypeStruct((b, s, nq), BF16),
        scratch_shapes=[pltpu.VMEM((mrows, d), BF16), pltpu.VMEM((mrows, V7X_LANES), F32),
                        pltpu.VMEM((mrows, 2 * d), F32)],
        compiler_params=_params(("parallel", "parallel", "parallel"), vmem),
        name="gqa_attn",
    )(q3, kt3, v3)


def _mlstm_step(chains, ms, q_ref, k_ref, v_ref, r_sc, cm_sc, b_sc, st_sc, h_sc):
    L = MLSTM_CHUNK
    dh = MLSTM_HEAD_DIM
    assert L == dh
    scale = dh ** -0.5
    row_i = lax.broadcasted_iota(jnp.int32, (L, L), 0)
    col_i = lax.broadcasted_iota(jnp.int32, (L, L), 1)

    def col(x_row):
        return jnp.transpose(jnp.broadcast_to(x_row, (L, L)))

    pre = []
    for (hh, direction, c), m in zip(chains, ms):
        off = pl.multiple_of(c * L, L)
        lanes = slice(hh * dh, (hh + 1) * dh)
        q = q_ref[0, pl.ds(off, L), lanes]
        k = k_ref[0, pl.ds(off, L), lanes]
        v = v_ref[0, pl.ds(off, L), lanes]
        r_row = r_sc[hh, direction, pl.ds(c, 1), :]
        cm_row = cm_sc[hh, direction, pl.ds(c, 1), :]
        b_row = b_sc[hh, direction, pl.ds(c, 1), :]
        rmax = jnp.max(r_row, axis=-1, keepdims=True)
        btot = b_row[:, L - 1:L] if direction == 0 else b_row[:, 0:1]
        cmat = jnp.maximum(m, col(cm_row))
        mask = (row_i >= col_i) if direction == 0 else (row_i <= col_i)
        c_last = jnp.maximum(m, rmax)
        w_state = jnp.exp(col(r_row) - c_last) * scale
        pre.append(dict(
            off=off, lanes=lanes, q=q, k=k, v=v,
            w_intra=jnp.where(mask, jnp.exp(r_row - cmat), 0.0) * scale,
            w_inter=jnp.exp(m - cmat),
            den_floor=jnp.exp(-(col(b_row) + cmat)),
            decay=jnp.exp(m - c_last),
            kv_w=jnp.concatenate([w_state * v.astype(F32), w_state], axis=-1).astype(BF16),
            m_new=btot + c_last,
        ))
    s_raw = [lax.dot_general(p["q"], p["k"], (((1,), (1,)), ((), ())), preferred_element_type=F32) for p in pre]
    states = [st_sc[hh, direction] for hh, direction, _ in chains]
    inter = [jnp.dot(p["q"], st.astype(BF16), preferred_element_type=F32) for p, st in zip(pre, states)]
    upd = [lax.dot_general(p["k"], p["kv_w"], (((0,), (0,)), ((), ())), preferred_element_type=F32) for p in pre]
    for n, ((hh, direction, _), p) in enumerate(zip(chains, pre)):
        st_sc[hh, direction] = p["decay"] * states[n] + upd[n]
        v_aug = jnp.concatenate([p["v"], jnp.ones((L, dh), BF16)], axis=-1)
        intra = jnp.dot((s_raw[n] * p["w_intra"]).astype(BF16), v_aug, preferred_element_type=F32)
        h_aug = jnp.concatenate([p["w_inter"], p["w_inter"]], axis=-1) * inter[n] + intra
        h_sc[hh, direction, pl.ds(p["off"], L), :] = h_aug[:, :dh] / jnp.maximum(jnp.abs(h_aug[:, dh:]), p["den_floor"])
    return [p["m_new"] for p in pre]


def _mlstm_kernel(bias_ref, q_ref, k_ref, v_ref, o_ref, gate_ref, g_ref, y_ref,
                  r_sc, cm_sc, b_sc, h_sc, st_sc):
    L = MLSTM_CHUNK
    dh = MLSTM_HEAD_DIM
    hpb = q_ref.shape[2] // dh
    head0 = pl.program_id(1) * hpb
    nc = q_ref.shape[1] // L
    lane = lax.broadcasted_iota(jnp.int32, (nc, L), 1)
    shifts = [1 << t for t in range(int(math.log2(L)))]
    for hh in range(hpb):
        for d in range(2):
            i_pre = gate_ref[0, 2 * d, hh] + bias_ref[2 * d, head0 + hh]
            f_pre = gate_ref[0, 2 * d + 1, hh] + bias_ref[2 * d + 1, head0 + hh]
            logf = jnp.minimum(f_pre, 0.0) - jnp.log1p(jnp.exp(-jnp.abs(f_pre)))
            bc = logf
            for sh in shifts:
                if d == 0:
                    bc = bc + jnp.where(lane >= sh, pltpu.roll(bc, sh, axis=1), 0.0)
                else:
                    bc = bc + jnp.where(lane < L - sh, pltpu.roll(bc, L - sh, axis=1), 0.0)
            r = i_pre - bc
            cm = r
            for sh in shifts:
                if d == 0:
                    cm = jnp.maximum(cm, jnp.where(lane >= sh, pltpu.roll(cm, sh, axis=1), -jnp.inf))
                else:
                    cm = jnp.maximum(cm, jnp.where(lane < L - sh, pltpu.roll(cm, L - sh, axis=1), -jnp.inf))
            r_sc[hh, d] = r
            cm_sc[hh, d] = cm
            b_sc[hh, d] = bc
    st_sc[...] = jnp.zeros(st_sc.shape, F32)

    def body(c, ms):
        chains = [(hh, d, c if d == 0 else nc - 1 - c) for hh in range(hpb) for d in range(2)]
        return tuple(_mlstm_step(chains, ms, q_ref, k_ref, v_ref, r_sc, cm_sc, b_sc, st_sc, h_sc))

    lax.fori_loop(0, nc, body, tuple(jnp.zeros((1, 1), F32) for _ in range(2 * hpb)))
    for hh in range(hpb):
        lanes = slice(hh * dh, (hh + 1) * dh)
        hsum = h_sc[hh, 0] + h_sc[hh, 1]
        ms = jnp.mean(hsum * hsum, axis=-1, keepdims=True)
        y = hsum * lax.rsqrt(ms + EPS) * g_ref[:, lanes]
        y_ref[0, :, lanes] = (jax.nn.sigmoid(o_ref[0, :, lanes].astype(F32)) * y).astype(y_ref.dtype)


def mlstm_branch(p3, gates5, gate_bias, norm_g, *, heads_per_block):
    b, s, _ = p3.shape
    L = MLSTM_CHUNK
    hpb = heads_per_block
    wb = hpb * MLSTM_HEAD_DIM
    nc = s // L
    blk = lambda off: pl.BlockSpec((1, s, wb), lambda bi, h, off=off: (bi, 0, off // wb + h))
    vmem = 2 * 5 * s * wb * 2 + 2 * s * wb * 4 + 6 * hpb * nc * L * 4 + 4 * hpb * wb * wb * 4 + 3 * s * wb * 4
    return pl.pallas_call(
        _mlstm_kernel,
        grid=(b, MLSTM_HEADS // hpb),
        in_specs=[
            pl.BlockSpec(memory_space=pltpu.SMEM),
            blk(OFF_CQ), blk(OFF_CK), blk(OFF_CV), blk(OFF_CO),
            pl.BlockSpec((1, 4, hpb, nc, L), lambda bi, h: (bi, 0, h, 0, 0)),
            pl.BlockSpec((1, wb), lambda bi, h: (0, h)),
        ],
        out_specs=pl.BlockSpec((1, s, wb), lambda bi, h: (bi, 0, h)),
        out_shape=jax.ShapeDtypeStruct((b, s, BRANCH_WIDTH), BF16),
        scratch_shapes=[
            pltpu.VMEM((hpb, 2, nc, L), F32), pltpu.VMEM((hpb, 2, nc, L), F32), pltpu.VMEM((hpb, 2, nc, L), F32),
            pltpu.VMEM((hpb, 2, s, MLSTM_HEAD_DIM), F32),
            pltpu.VMEM((hpb, 2, MLSTM_HEAD_DIM, 2 * MLSTM_HEAD_DIM), F32),
        ],
        compiler_params=_params(("parallel", "parallel"), vmem),
        name="mlstm",
    )(gate_bias.astype(F32), p3, p3, p3, p3, gates5, norm_g.reshape(1, BRANCH_WIDTH).astype(F32))


def _rel_bucket(rel):
    half = REL_BUCKETS // 2
    max_exact = half // 2
    ret = jnp.where(rel > 0, half, 0)
    n = jnp.abs(rel)
    nf = jnp.maximum(n, 1).astype(F32)
    large = max_exact + (jnp.log(nf / max_exact) / math.log(REL_MAX_DIST / max_exact) * (half - max_exact)).astype(jnp.int32)
    large = jnp.minimum(large, half - 1)
    return ret + jnp.where(n < max_exact, n, large)


def diff_bias_tiles(rel_bias, t):
    assert t >= REL_MAX_DIST
    k = jnp.arange(2 * t, dtype=jnp.int32)
    rel = jnp.arange(-2, 3, dtype=jnp.int32)[:, None] * t + jnp.where(k < t, k, k - 2 * t)[None, :]
    onehot = (_rel_bucket(rel)[:, :, None] == jnp.arange(REL_BUCKETS, dtype=jnp.int32)).astype(F32)
    period = jnp.einsum('dkb,bh->hdk', onehot, rel_bias.astype(F32) * LOG2E, precision=lax.Precision.HIGHEST)
    far = period[:, 0::4, 0]
    return period, far


def _diff_attn_kernel(far_ref, q_ref, k_ref, v_ref, period_ref, lam_ref, g_ref, o_ref,
                      q_sc, kt_sc, vaug_sc, bias_sc, sc_buf, m_ref, acc_ref, *, t, lam_init):
    dq = DIFF_QK_DIM
    dv = DIFF_V_DIM
    s = k_ref.shape[1]
    nt = s // t
    head = pl.program_id(1)
    rb = sc_buf.shape[0]
    grp = sc_buf.shape[1] // t
    wrap = (grp - 1) * t
    assert nt % grp == 0 and t % rb == 0 and nt >= 4

    tr = max(t, 512)
    for r in range(0, s, tr):
        kt_sc[:, r:r + tr] = jnp.transpose(k_ref[0, r:r + tr, :].astype(F32)).astype(BF16)
    kt_sc[:, s:] = kt_sc[:, :wrap]
    vaug_sc[:s, :dv] = v_ref[0]
    vaug_sc[s:, :dv] = v_ref[0, :wrap, :]
    lane = lax.broadcasted_iota(jnp.int32, (s + wrap, dv), 1)
    vaug_sc[:, dv:] = jnp.where(lane == 0, 1.0, 0.0).astype(BF16)
    for dl in range(period_ref.shape[1]):
        full = jnp.broadcast_to(period_ref[0, dl:dl + 1, :], (t, 2 * t))
        bias_sc[dl] = pltpu.roll(full, 0, axis=1, stride=1, stride_axis=0)[:, :t]
    q_scale = dq ** -0.5 * LOG2E
    q_all = (q_ref[0].astype(F32) * q_scale).astype(BF16)
    q_sc[0] = q_all[:, :dq]
    q_sc[1] = q_all[:, dq:]
    lp = lam_ref[...]
    lam = (jnp.exp(jnp.sum(lp[0:1] * lp[1:2], axis=-1, keepdims=True))
           - jnp.exp(jnp.sum(lp[2:3] * lp[3:4], axis=-1, keepdims=True)) + lam_init)

    def key_tile(i, delta):
        j = lax.rem(i + delta, nt)
        return j, pl.multiple_of(j * t, t)

    def qk(i, item):
        pair, mp, hf = item
        row0 = pl.multiple_of(i * t + hf * rb, rb)
        _, off = key_tile(i, grp * pair)
        return jnp.dot(q_sc[mp, pl.ds(row0, rb), :], kt_sc[mp * dq:(mp + 1) * dq, pl.ds(off, grp * t)],
                       preferred_element_type=F32)

    items = [(pair, mp, hf) for pair in range(nt // grp) for mp in range(2) for hf in range(t // rb)]
    sc_buf[...] = qk(0, items[0])

    def tile_body(i, carry):
        sc_next = sc_buf[...]
        for n, (pair, mp, hf) in enumerate(items):
            sc = sc_next
            if n + 1 < len(items):
                sc_next = qk(i, items[n + 1])
            else:
                sc_next = qk(jnp.minimum(i + 1, nt - 1), items[0])
            rows = slice(hf * rb, (hf + 1) * rb)
            scs = [sc[:, p * t:(p + 1) * t] for p in range(grp)]
            _, off = key_tile(i, grp * pair)
            shifts, row_max = [], None
            for piece, delta in enumerate(range(grp * pair, grp * pair + grp)):
                j, _ = key_tile(i, delta)
                if delta in (0, 1, nt - 1):
                    tile = 2 if delta == 0 else jnp.clip(j - i, -2, 2) + 2
                    scs[piece] = scs[piece] + bias_sc[tile, rows, :]
                    shifts.append(None)
                    rm = jnp.max(scs[piece], axis=-1, keepdims=True)
                else:
                    shifts.append(jnp.where(j > i, far_ref[head, 1], far_ref[head, 0]))
                    rm = jnp.max(scs[piece], axis=-1, keepdims=True) + shifts[-1]
                row_max = rm if row_max is None else jnp.maximum(row_max, rm)
            if pair == 0:
                m_new = jnp.broadcast_to(row_max, (rb, V7X_LANES))
            else:
                m_prev = m_ref[mp, rows, :]
                m_new = jnp.maximum(m_prev, row_max)
            ps = []
            for piece in range(grp):
                m_sub = m_new if shifts[piece] is None else m_new - shifts[piece]
                ps.append(jnp.exp2((scs[piece] - pltpu.repeat(m_sub, t // V7X_LANES, axis=1)).astype(BF16)))
            pv = jnp.dot(jnp.concatenate(ps, axis=1), vaug_sc[pl.ds(off, grp * t), :], preferred_element_type=F32)
            if pair == 0:
                acc_ref[mp, rows, :] = pv
            else:
                alpha = jnp.exp2(m_prev - m_new)
                acc_ref[mp, rows, :] = acc_ref[mp, rows, :] * pltpu.repeat(alpha, 2 * dv // V7X_LANES, axis=1) + pv
            m_ref[mp, rows, :] = m_new
        sc_buf[...] = sc_next
        o0 = acc_ref[0, :, :dv] / acc_ref[0, :, dv:dv + 1]
        o1 = acc_ref[1, :, :dv] / acc_ref[1, :, dv:dv + 1]
        o = o0 - lam * o1
        ms = jnp.mean(o * o, axis=-1, keepdims=True)
        row0 = pl.multiple_of(i * t, t)
        o_ref[0, pl.ds(row0, t), :] = (o * lax.rsqrt(ms + EPS) * g_ref[...] * (1.0 - lam_init)).astype(o_ref.dtype)
        return carry

    lax.fori_loop(0, nt, tile_body, 0)


def diff_attention(p3, bias_period, bias_far, lam_params, sub_g, *, t, rows_per_item, tiles_per_item, layer_number):
    b, s, _ = p3.shape
    dv = DIFF_V_DIM
    n_off = bias_period.shape[1]
    assert s // t >= 4, "tiles 2 .. s/t-2 steps away from the query tile must all be beyond REL_MAX_DIST"
    lam_init = 0.8 - 0.6 * math.exp(-0.3 * (layer_number - 1))
    sw = s + (tiles_per_item - 1) * t
    item = rows_per_item * tiles_per_item * t
    vmem = (2 * (s * 128 * 2 + 128 * s * 2 + s * dv * 2 + s * dv * 2) + n_off * t * t * 4 + sw * (2 * dv + 128) * 2
            + 2 * s * 128 * 2 + item * 4 + 2 * t * 128 * 4 * 3 + 8 * item * 4)
    return pl.pallas_call(
        functools.partial(_diff_attn_kernel, t=t, lam_init=lam_init),
        grid=(b, DIFF_HEADS),
        in_specs=[
            pl.BlockSpec(memory_space=pltpu.SMEM),
            pl.BlockSpec((1, s, 128), lambda bi, h: (bi, 0, OFF_DQ // 128 + h)),
            pl.BlockSpec((1, s, 128), lambda bi, h: (bi, 0, OFF_DK // 128 + h)),
            pl.BlockSpec((1, s, dv), lambda bi, h: (bi, 0, OFF_DV // dv + h)),
            pl.BlockSpec((1, n_off, 2 * t), lambda bi, h: (h, 0, 0)),
            pl.BlockSpec((4, DIFF_QK_DIM), lambda bi, h: (0, 0)),
            pl.BlockSpec((1, dv), lambda bi, h: (0, 0)),
        ],
        out_specs=pl.BlockSpec((1, s, dv), lambda bi, h: (bi, 0, h)),
        out_shape=jax.ShapeDtypeStruct((b, s, BRANCH_WIDTH), BF16),
        scratch_shapes=[pltpu.VMEM((2, s, DIFF_QK_DIM), BF16), pltpu.VMEM((2 * DIFF_QK_DIM, sw), BF16),
                        pltpu.VMEM((sw, 2 * dv), BF16),
                        pltpu.VMEM((n_off, t, t), F32), pltpu.VMEM((rows_per_item, tiles_per_item * t), F32),
                        pltpu.VMEM((2, t, V7X_LANES), F32), pltpu.VMEM((2, t, 2 * dv), F32)],
        compiler_params=_params(("parallel", "parallel"), vmem),
        name="diff_attn",
    )(bias_far, p3, p3, p3, bias_period, lam_params.astype(F32), sub_g.reshape(1, dv).astype(F32))


def _merge_kernel(ya_ref, yb_ref, yc_ref, yd_ref, gate_ref, wb_ref, wo_ref, x_ref, o_ref, wb_sc, wo_sc):
    dm = x_ref.shape[1]

    @pl.when(pl.program_id(0) == 0)
    def _():
        for n in range(wb_ref.shape[0]):
            wb_sc[n] = wb_ref[n].astype(BF16)
        wo_sc[...] = wo_ref[...].astype(BF16)

    tm = x_ref.shape[0]
    halves = [slice(0, tm // 2), slice(tm // 2, tm)]
    y_refs = (ya_ref, yb_ref, yc_ref, yd_ref)
    branch = [[jnp.dot(y_ref[rows, :], wb_sc[n], preferred_element_type=F32) for n, y_ref in enumerate(y_refs)]
              for rows in halves]
    for rows, brs in zip(halves, branch):
        merged = None
        for n, br in enumerate(brs):
            term = jax.nn.sigmoid(gate_ref[rows, n * dm:(n + 1) * dm].astype(F32)) * br
            merged = term if merged is None else merged + term
        o_ref[rows, :] = x_ref[rows, :] + jnp.dot(merged.astype(BF16), wo_sc[...], preferred_element_type=F32)


def merge_branches(ys, p2, w_branch, w_out, x2, *, tm):
    m, dm = x2.shape
    w = BRANCH_WIDTH
    y_spec = pl.BlockSpec((tm, w), lambda i: (i, 0))
    vmem = (2 * (4 * tm * w * 2 + tm * 4 * dm * 2 + 2 * tm * dm * 4) + (4 * w * dm + dm * dm) * (4 + 2)
            + 6 * tm * dm * 4)
    return pl.pallas_call(
        _merge_kernel,
        grid=(m // tm,),
        in_specs=[
            y_spec, y_spec, y_spec, y_spec,
            pl.BlockSpec((tm, N_BRANCHES * dm), lambda i: (i, OFF_GATES // (N_BRANCHES * dm))),
            pl.BlockSpec((N_BRANCHES, w, dm), lambda i: (0, 0, 0), pipeline_mode=pl.Buffered(1)),
            pl.BlockSpec((dm, dm), lambda i: (0, 0), pipeline_mode=pl.Buffered(1)),
            pl.BlockSpec((tm, dm), lambda i: (i, 0)),
        ],
        out_specs=pl.BlockSpec((tm, dm), lambda i: (i, 0)),
        out_shape=jax.ShapeDtypeStruct((m, dm), F32),
        scratch_shapes=[pltpu.VMEM((N_BRANCHES, w, dm), BF16), pltpu.VMEM((dm, dm), BF16)],
        compiler_params=_params(("arbitrary",), vmem),
        name="merge",
    )(*ys, p2, w_branch.astype(F32), w_out.astype(F32), x2)


HALO_ROWS = 16


def _ffn_up_kernel(x_ref, xp_ref, xn_ref, g_ref, wa_ref, wl_ref, cw_ref, cb_ref, o_ref, xe_sc,
                   *, tiles_per_seq, col_chunk):
    tm, f = o_ref.shape
    h = HALO_ROWS
    pos = pl.program_id(0) % tiles_per_seq

    def norm(x):
        ms = jnp.mean(x * x, axis=-1, keepdims=True)
        return x * lax.rsqrt(ms + EPS) * g_ref[...]

    xe_sc[:h] = jnp.where(pos == 0, 0.0, norm(xp_ref[...])).astype(BF16)
    xe_sc[h:h + tm] = norm(x_ref[...]).astype(BF16)
    xe_sc[h + tm:] = jnp.where(pos == tiles_per_seq - 1, 0.0, norm(xn_ref[...])).astype(BF16)
    k0 = math.sqrt(2.0 / math.pi)
    bounds = list(range(0, f, col_chunk)) + [f]
    chunks = [slice(lo, hi) for lo, hi in zip(bounds[:-1], bounds[1:])]

    def project(cols):
        a_ext = jnp.dot(xe_sc[...], wa_ref[:, cols], preferred_element_type=F32)
        lin = jnp.dot(xe_sc[h:h + tm, :], wl_ref[:, cols], preferred_element_type=F32)
        return a_ext, lin

    nxt = project(chunks[0])
    for n, cols in enumerate(chunks):
        a_ext, lin = nxt
        if n + 1 < len(chunks):
            nxt = project(chunks[n + 1])
        cw = cw_ref[:, cols]
        c = (a_ext[h - 1:h - 1 + tm] * cw[0:1] + a_ext[h:h + tm] * cw[1:2] + a_ext[h + 1:h + 1 + tm] * cw[2:3]
             + cb_ref[:, cols])
        t = jnp.tanh(c * ((c * c) * (k0 * 0.044715) + k0))
        o_ref[:, cols] = ((c + c * t) * lin).astype(o_ref.dtype)


def ffn_up(x2, g, w_up, conv_w, conv_b, *, s, tm):
    m, dm = x2.shape
    f = w_up.shape[1] // 2
    tiles_per_seq = s // tm
    hb = tm // HALO_ROWS
    n_halo = m // HALO_ROWS
    vmem = 2 * (tm * dm * 4 + tm * f * 2) + 2 * dm * f * 2 + (tm + 2 * HALO_ROWS) * dm * 2 + 12 * tm * 512 * 4
    return pl.pallas_call(
        functools.partial(_ffn_up_kernel, tiles_per_seq=tiles_per_seq, col_chunk=512),
        grid=(m // tm,),
        in_specs=[
            pl.BlockSpec((tm, dm), lambda i: (i, 0)),
            pl.BlockSpec((HALO_ROWS, dm), lambda i: (jnp.maximum(i * hb - 1, 0), 0)),
            pl.BlockSpec((HALO_ROWS, dm), lambda i: (jnp.minimum((i + 1) * hb, n_halo - 1), 0)),
            pl.BlockSpec((1, dm), lambda i: (0, 0)),
            pl.BlockSpec((dm, f), lambda i: (0, 0), pipeline_mode=pl.Buffered(1)),
            pl.BlockSpec((dm, f), lambda i: (0, 1), pipeline_mode=pl.Buffered(1)),
            pl.BlockSpec((CONV_W, f), lambda i: (0, 0)),
            pl.BlockSpec((1, f), lambda i: (0, 0)),
        ],
        out_specs=pl.BlockSpec((tm, f), lambda i: (i, 0)),
        out_shape=jax.ShapeDtypeStruct((m, f), BF16),
        scratch_shapes=[pltpu.VMEM((tm + 2 * HALO_ROWS, dm), BF16)],
        compiler_params=_params(("parallel",), vmem),
        name="ffn_up",
    )(x2, x2, x2, g.reshape(1, dm).astype(F32), w_up, w_up, conv_w.astype(F32), conv_b.reshape(1, f).astype(F32))


def _ffn_down_kernel(h_ref, wd_ref, x_ref, gf_ref, o_ref, wd_sc, *, final_norm):
    @pl.when(pl.program_id(0) == 0)
    def _():
        wd_sc[...] = wd_ref[...].astype(BF16)

    y = x_ref[...] + jnp.dot(h_ref[...], wd_sc[...], preferred_element_type=F32)
    if final_norm:
        ms = jnp.mean(y * y, axis=-1, keepdims=True)
        y = y * lax.rsqrt(ms + EPS) * gf_ref[...]
    o_ref[...] = y


def ffn_down(hmid, w_down, x2, final_g, *, tm, final_norm):
    m, dm = x2.shape
    f = w_down.shape[0]
    vmem = 2 * (tm * f * 2 + 2 * tm * dm * 4) + f * dm * 4 + f * dm * 2 + 4 * tm * dm * 4
    return pl.pallas_call(
        functools.partial(_ffn_down_kernel, final_norm=final_norm),
        grid=(m // tm,),
        in_specs=[
            pl.BlockSpec((tm, f), lambda i: (i, 0)),
            pl.BlockSpec((f, dm), lambda i: (0, 0), pipeline_mode=pl.Buffered(1)),
            pl.BlockSpec((tm, dm), lambda i: (i, 0)),
            pl.BlockSpec((1, dm), lambda i: (0, 0)),
        ],
        out_specs=pl.BlockSpec((tm, dm), lambda i: (i, 0)),
        out_shape=jax.ShapeDtypeStruct((m, dm), F32),
        scratch_shapes=[pltpu.VMEM((f, dm), BF16)],
        compiler_params=_params(("arbitrary",), vmem),
        name="ffn_down",
    )(hmid, w_down.astype(F32), x2, final_g.reshape(1, dm).astype(F32))


def _arrange_w_in(w):
    widths = (512, 512, 128, 128, 512, 512, 512, 512, 16, 512, 512, 512, 4096)
    offs = [0]
    for wd in widths:
        offs.append(offs[-1] + wd)
    seg = lambda n: w[:, offs[n]:offs[n + 1]]
    a, bq, bk, bv, cq, ck, cv, co, cgate, dq, dk, dv, gates = (seg(n) for n in range(len(widths)))
    main = jnp.concatenate([gates, a, bq, cq, ck, cv, co, dq, dk, dv, bk, bv], axis=1).astype(BF16)
    gate = jnp.pad(cgate, ((0, 0), (0, V7X_LANES - cgate.shape[1]))).astype(BF16)
    return main, gate


def kernel(x, norm_mix_g, w_in, mlstm_gate_bias, qk_norm_g, mlstm_norm_g, diff_lambda, diff_norm_g, rel_bias,
           w_branch, w_out, norm_ffn_g, w_up, conv_w, conv_b, w_down, final_norm_g):
    b, s, dm = x.shape
    depth = w_in.shape[0]
    m = b * s
    d_ff = w_down.shape[1]
    L = MLSTM_CHUNK
    t_diff = 512

    bd, dft = fourier_tables(s)
    cos2, sin2 = rope_tables(s)
    bias_tiles, bias_far = diff_bias_tiles(rel_bias, t_diff)

    x2 = x.reshape(m, dm)
    for layer in range(depth):
        w_main, w_gate = _arrange_w_in(w_in[layer])
        p2, cgate = norm_matmul(x2, norm_mix_g[layer], w_main, w_gate, tm=1024, tn=1280)
        p3 = p2.reshape(b, s, P_WIDTH)

        y_a = fourier_mix(p3, bd, dft, tm=512)

        q_b, k_b, v_b = gqa_prep(p2, qk_norm_g[layer], cos2, sin2, s=s, tm=512)
        kt_b = jnp.swapaxes(k_b.reshape(b, s, GQA_KV_HEADS * HEAD_DIM), 1, 2)
        y_b = gqa_attention(q_b.reshape(b, s, -1), kt_b, v_b.reshape(b, s, -1), tq=512, tk=1024)

        gates5 = jnp.transpose(cgate[:, :4 * MLSTM_HEADS].reshape(b, s, 4, MLSTM_HEADS), (0, 2, 3, 1))
        gates5 = gates5.reshape(b, 4, MLSTM_HEADS, s // L, L)
        y_c = mlstm_branch(p3, gates5, mlstm_gate_bias[layer], mlstm_norm_g[layer], heads_per_block=2)

        y_d = diff_attention(p3, bias_tiles, bias_far, diff_lambda[layer], diff_norm_g[layer],
                             t=t_diff, rows_per_item=256, tiles_per_item=4, layer_number=layer + 1)

        ys = [y.reshape(m, BRANCH_WIDTH) for y in (y_a, y_b, y_c, y_d)]
        x2 = merge_branches(ys, p2, w_branch[layer], w_out[layer], x2, tm=512)

        half_lin = jnp.concatenate([jnp.ones((d_ff,), F32), jnp.full((d_ff,), 0.5, F32)])
        hmid = ffn_up(x2, norm_ffn_g[layer], (w_up[layer] * half_lin).astype(BF16), conv_w[layer], conv_b[layer],
                      s=s, tm=1024)
        x2 = ffn_down(hmid, w_down[layer], x2, final_norm_g, tm=1024, final_norm=(layer == depth - 1))
    return x2.reshape(b, s, dm)
```

```python
import functools
import math

import jax
import jax.numpy as jnp
from jax import lax
from jax.experimental import pallas as pl
from jax.experimental.pallas import tpu as pltpu

F32 = jnp.float32
BF16 = jnp.bfloat16

GRID_W = 64
HEAD_DIM = 64
BRANCH_WIDTH = 512
N_BRANCHES = 4
FOURIER_GROUP_DIM = 64
GQA_Q_HEADS = 8
GQA_KV_HEADS = 2
MLSTM_HEADS = 4
MLSTM_HEAD_DIM = 128
MLSTM_CHUNK = 128
DIFF_HEADS = 4
DIFF_QK_DIM = 64
DIFF_V_DIM = 128
REL_BUCKETS = 32
REL_MAX_DIST = 128
CONV_W = 3
ROPE_BASE = 10000.0
EPS = 1e-6
LOG2E = math.log2(math.e)

V7X_LANES = 128
V7X_VMEM_BYTES = 64 * 1024 * 1024
V7X_VMEM_CAP = V7X_VMEM_BYTES - 8 * 1024 * 1024

OFF_A = 0
OFF_BQ = 512
OFF_CQ = 1024
OFF_CK = 1536
OFF_CV = 2048
OFF_CO = 2560
OFF_DQ = 3072
OFF_DK = 3584
OFF_DV = 4096
OFF_BKV = 4608
P_WIDTH = 4864


def _params(sem, vmem_bytes):
    limit = int(min(max(vmem_bytes * 3 // 2 + (4 << 20), 16 << 20), V7X_VMEM_CAP))
    return pltpu.CompilerParams(dimension_semantics=sem, vmem_limit_bytes=limit)


def _norm_mm_kernel(x_ref, g_ref, w_ref, o_ref, xn_ref):
    @pl.when(pl.program_id(1) == 0)
    def _():
        x = x_ref[...]
        ms = jnp.mean(x * x, axis=-1, keepdims=True)
        xn_ref[...] = (x * lax.rsqrt(ms + EPS) * g_ref[...]).astype(BF16)

    o_ref[...] = jnp.dot(xn_ref[...], w_ref[...], preferred_element_type=F32).astype(o_ref.dtype)


def _norm_mm_gate_kernel(x_ref, g_ref, w_ref, wg_ref, o_ref, og_ref, xn_ref):
    @pl.when(pl.program_id(1) == 0)
    def _():
        x = x_ref[...]
        ms = jnp.mean(x * x, axis=-1, keepdims=True)
        xn = (x * lax.rsqrt(ms + EPS) * g_ref[...]).astype(BF16)
        xn_ref[...] = xn
        og_ref[...] = jnp.dot(xn, wg_ref[...], preferred_element_type=F32)

    o_ref[...] = jnp.dot(xn_ref[...], w_ref[...], preferred_element_type=F32).astype(o_ref.dtype)


def norm_matmul(x, g, w, w_gate=None, *, tm, tn):
    m, k = x.shape
    n = w.shape[1]
    grid = (m // tm, n // tn)
    vmem = 2 * tm * k * 4 + tm * k * 2 + 2 * k * tn * 2 + 2 * tm * tn * 2 + 4 * tm * k
    x_spec = pl.BlockSpec((tm, k), lambda i, j: (i, 0))
    g_spec = pl.BlockSpec((1, k), lambda i, j: (0, 0))
    w_spec = pl.BlockSpec((k, tn), lambda i, j: (0, j))
    o_spec = pl.BlockSpec((tm, tn), lambda i, j: (i, j))
    scratch = [pltpu.VMEM((tm, k), BF16)]
    g2 = g.reshape(1, k).astype(F32)
    if w_gate is None:
        return pl.pallas_call(
            _norm_mm_kernel,
            grid=grid,
            in_specs=[x_spec, g_spec, w_spec],
            out_specs=o_spec,
            out_shape=jax.ShapeDtypeStruct((m, n), BF16),
            scratch_shapes=scratch,
            compiler_params=_params(("parallel", "arbitrary"), vmem),
            name="norm_matmul",
        )(x, g2, w)
    ng = w_gate.shape[1]
    return pl.pallas_call(
        _norm_mm_gate_kernel,
        grid=grid,
        in_specs=[x_spec, g_spec, w_spec, pl.BlockSpec((k, ng), lambda i, j: (0, 0))],
        out_specs=[o_spec, pl.BlockSpec((tm, ng), lambda i, j: (i, 0))],
        out_shape=[jax.ShapeDtypeStruct((m, n), BF16), jax.ShapeDtypeStruct((m, ng), F32)],
        scratch_shapes=scratch,
        compiler_params=_params(("parallel", "arbitrary"), vmem),
        name="norm_matmul_gate",
    )(x, g2, w, w_gate)


FOURIER_HALO = 16


def _fourier_kernel(a_ref, bd_ref, dft_ref, halo_ref, flip_ref, o_ref, z_ref, ext_ref, *, tm, row_chunk):
    s = a_ref.shape[1]
    w = a_ref.shape[2]
    i = pl.program_id(1)

    @pl.when(i == 0)
    def _():
        for r in range(0, s, row_chunk):
            a = a_ref[0, r:r + row_chunk, :]
            zc = jnp.dot(a, bd_ref[...], preferred_element_type=F32)
            z_ref[r:r + row_chunk, :] = zc[:, :w].astype(BF16)
            z_ref[s + r:s + r + row_chunk, :] = zc[:, w:].astype(BF16)

    ext_ref[:tm] = dft_ref[...].reshape(tm, 2 * s)
    ext_ref[tm:] = halo_ref[0]
    p = jnp.dot(ext_ref[:, :s], z_ref[:s], preferred_element_type=F32)
    q = jnp.dot(ext_ref[:, s:], z_ref[s:], preferred_element_type=F32)
    lo = pl.multiple_of(i * tm, tm)
    o_ref[0, pl.ds(lo, tm), :] = (p[:tm] + q[:tm]).astype(o_ref.dtype)
    mirrored = (p[1:tm + 1] - q[1:tm + 1]).astype(BF16)
    hi = pl.multiple_of(s - (i + 1) * tm, tm)
    o_ref[0, pl.ds(hi, tm), :] = jnp.dot(flip_ref[...], mirrored, preferred_element_type=F32).astype(o_ref.dtype)


def fourier_tables(s):
    cg = FOURIER_GROUP_DIM
    jj = jnp.arange(cg, dtype=jnp.int32)
    ang_c = (2.0 * math.pi / cg) * ((jj[:, None] * jj[None, :]) % cg).astype(F32)
    eye_g = jnp.eye(BRANCH_WIDTH // cg, dtype=F32)
    bd_c = jnp.kron(eye_g, jnp.cos(ang_c)) * cg ** -0.5
    bd_s = jnp.kron(eye_g, jnp.sin(ang_c)) * cg ** -0.5
    bd = jnp.concatenate([bd_c, bd_s], axis=1).astype(BF16)
    n_lo = s // cg
    nn = jnp.arange(s, dtype=jnp.int32)
    k1 = jnp.arange(cg // 2 + 1, dtype=jnp.int32)
    ang_hi = (2.0 * math.pi / cg) * ((k1[:, None] * nn[None, :]) % cg).astype(F32)
    ll = jnp.arange(n_lo, dtype=jnp.int32)
    ang_lo = (2.0 * math.pi / s) * ((ll[:, None] * nn[None, :]) % s).astype(F32)
    ch, sh = jnp.cos(ang_hi), jnp.sin(ang_hi)
    cl, sl = jnp.cos(ang_lo), jnp.sin(ang_lo)
    scale = s ** -0.5
    a2 = jnp.concatenate([ch, -sh], axis=1)[:, None, :]
    c2 = jnp.concatenate([sh, ch], axis=1)[:, None, :]
    b2 = jnp.concatenate([cl, cl], axis=1)[None, :, :]
    d2 = jnp.concatenate([sl, sl], axis=1)[None, :, :]
    dft = ((a2 * b2 - c2 * d2) * scale).astype(BF16)
    return bd, dft


def fourier_mix(p3, bd, dft, *, tm):
    b, s, _ = p3.shape
    w = BRANCH_WIDTH
    n_lo = dft.shape[1]
    h = FOURIER_HALO
    assert (s // 2) % tm == 0 and tm % n_lo == 0 and n_lo % h == 0
    r = jnp.arange(tm, dtype=jnp.int32)
    flip = (r[:, None] + r[None, :] == tm - 1).astype(BF16)
    vmem = (s * w * 2 + 2 * w * w * 2 + 2 * (tm + h) * 2 * s * 2 + (tm + h) * 2 * s * 2 + 2 * s * w * 2
            + 2 * s * w * 2 + 8 * (tm + h) * w * 4)
    return pl.pallas_call(
        functools.partial(_fourier_kernel, tm=tm, row_chunk=min(s, 512)),
        grid=(b, s // 2 // tm),
        in_specs=[
            pl.BlockSpec((1, s, w), lambda bi, i: (bi, 0, OFF_A // w), pipeline_mode=pl.Buffered(1)),
            pl.BlockSpec((w, 2 * w), lambda bi, i: (0, 0), pipeline_mode=pl.Buffered(1)),
            pl.BlockSpec((tm // n_lo, n_lo, 2 * s), lambda bi, i: (i, 0, 0)),
            pl.BlockSpec((1, h, 2 * s), lambda bi, i: ((i + 1) * (tm // n_lo), 0, 0)),
            pl.BlockSpec((tm, tm), lambda bi, i: (0, 0)),
        ],
        out_specs=pl.BlockSpec((1, s, w), lambda bi, i: (bi, 0, 0)),
        out_shape=jax.ShapeDtypeStruct((b, s, w), BF16),
        scratch_shapes=[pltpu.VMEM((2 * s, w), BF16), pltpu.VMEM((tm + h, 2 * s), BF16)],
        compiler_params=_params(("parallel", "arbitrary"), vmem),
        name="fourier",
    )(p3, bd, dft, dft, flip)


def rope_tables(s):
    rows = s // GRID_W
    row_id = jnp.repeat(jnp.arange(rows, dtype=F32), GRID_W)
    col_id = jnp.tile(jnp.arange(GRID_W, dtype=F32), rows)
    n_pairs = HEAD_DIM // 4
    inv_freq = ROPE_BASE ** (-jnp.arange(n_pairs, dtype=F32) / n_pairs)
    ang = jnp.concatenate([row_id[:, None] * inv_freq, col_id[:, None] * inv_freq], axis=-1)
    cos, sin = jnp.cos(ang), jnp.sin(ang)
    return jnp.concatenate([cos, cos] * 2, axis=-1), jnp.concatenate([-sin, sin] * 2, axis=-1)


def _norm_rope(x, g, seg, cos2, sin2):
    half = HEAD_DIM // 2
    x2 = x * x
    hi = x2.astype(BF16)
    lo = (x2 - hi.astype(F32)).astype(BF16)
    ms = jnp.dot(hi, seg, preferred_element_type=F32) + jnp.dot(lo, seg, preferred_element_type=F32)
    y = x * lax.rsqrt(ms + EPS) * g
    lane = lax.broadcasted_iota(jnp.int32, (x.shape[0], V7X_LANES), 1)
    first_half = (lane % HEAD_DIM) < half
    outs = []
    for cb in range(x.shape[1] // V7X_LANES):
        yb = y[:, cb * V7X_LANES:(cb + 1) * V7X_LANES]
        rot = jnp.where(first_half, pltpu.roll(yb, V7X_LANES - half, axis=1), pltpu.roll(yb, half, axis=1))
        outs.append(yb * cos2 + rot * sin2)
    return outs[0] if len(outs) == 1 else jnp.concatenate(outs, axis=-1)


def _gqa_prep_kernel(q_ref, kv_ref, gq_ref, gk_ref, seg_ref, cos_ref, sin_ref, qo_ref, ko_ref, vo_ref):
    d = HEAD_DIM
    nk = GQA_KV_HEADS * d
    cos2, sin2 = cos_ref[...], sin_ref[...]
    q = q_ref[...].astype(F32)
    kv = kv_ref[...].astype(F32)
    qo_ref[...] = _norm_rope(q, gq_ref[...], seg_ref[...], cos2, sin2).astype(BF16)
    ko_ref[...] = _norm_rope(kv[:, :nk], gk_ref[...], seg_ref[:nk, :nk], cos2, sin2).astype(BF16)
    v = kv[:, nk:]
    lane = lax.broadcasted_iota(jnp.int32, v.shape, 1)
    ones_col = jnp.where(lane == d, 1.0, 0.0)
    vo_ref[:, :nk] = jnp.where(lane < d, v, ones_col).astype(BF16)
    vo_ref[:, nk:] = jnp.where(lane < d, pltpu.roll(v, d, axis=1), ones_col).astype(BF16)


def gqa_prep(p2, qk_g, cos2, sin2, *, s, tm):
    m = p2.shape[0]
    nq = GQA_Q_HEADS * HEAD_DIM
    nkv = GQA_KV_HEADS * HEAD_DIM
    assert nkv == V7X_LANES
    tiles_per_seq = s // tm
    q_scale = HEAD_DIM ** -0.5 * LOG2E
    gq = jnp.tile(qk_g[0].astype(F32) * q_scale, GQA_Q_HEADS).reshape(1, nq)
    gk = jnp.tile(qk_g[1].astype(F32), GQA_KV_HEADS).reshape(1, nkv)
    seg = jnp.kron(jnp.eye(GQA_Q_HEADS, dtype=F32), jnp.full((HEAD_DIM, HEAD_DIM), 1.0 / HEAD_DIM, F32)).astype(BF16)
    return pl.pallas_call(
        _gqa_prep_kernel,
        grid=(m // tm,),
        in_specs=[
            pl.BlockSpec((tm, nq), lambda i: (i, OFF_BQ // nq)),
            pl.BlockSpec((tm, 2 * nkv), lambda i: (i, OFF_BKV // (2 * nkv))),
            pl.BlockSpec((1, nq), lambda i: (0, 0)),
            pl.BlockSpec((1, nkv), lambda i: (0, 0)),
            pl.BlockSpec((nq, nq), lambda i: (0, 0)),
            pl.BlockSpec((tm, V7X_LANES), lambda i: (i % tiles_per_seq, 0)),
            pl.BlockSpec((tm, V7X_LANES), lambda i: (i % tiles_per_seq, 0)),
        ],
        out_specs=[
            pl.BlockSpec((tm, nq), lambda i: (i, 0)),
            pl.BlockSpec((tm, nkv), lambda i: (i, 0)),
            pl.BlockSpec((tm, 2 * nkv), lambda i: (i, 0)),
        ],
        out_shape=[
            jax.ShapeDtypeStruct((m, nq), BF16),
            jax.ShapeDtypeStruct((m, nkv), BF16),
            jax.ShapeDtypeStruct((m, 2 * nkv), BF16),
        ],
        compiler_params=_params(("parallel",), 16 * tm * nq * 4),
        name="gqa_prep",
    )(p2, p2, gq, gk, seg, cos2, sin2)


def _gqa_attn_kernel(q_ref, kt_ref, v_ref, o_ref, q_sc, m_ref, acc_ref, *, tk):
    d = HEAD_DIM
    tq = q_ref.shape[1]
    grp = q_ref.shape[2] // d
    s = kt_ref.shape[2]
    for g in range(grp):
        q_sc[g * tq:(g + 1) * tq, :] = q_ref[0, :, g * d:(g + 1) * d]

    def qk(item):
        c, g = item
        return jnp.dot(q_sc[g * tq:(g + 1) * tq, :], kt_ref[0, :, c * tk:(c + 1) * tk],
                       preferred_element_type=F32)

    items = [(c, g) for c in range(s // tk) for g in range(grp)]
    sc_next = qk(items[0])
    for n, (c, g) in enumerate(items):
        rows = slice(g * tq, (g + 1) * tq)
        sc = sc_next
        if n + 1 < len(items):
            sc_next = qk(items[n + 1])
        v = v_ref[0, c * tk:(c + 1) * tk, :]
        row_max = jnp.max(sc, axis=-1, keepdims=True)
        if c == 0:
            m_new = jnp.broadcast_to(row_max, (tq, V7X_LANES))
        else:
            m_prev = m_ref[rows, :]
            m_new = jnp.maximum(m_prev, row_max)
        p = jnp.exp2((sc - pltpu.repeat(m_new, tk // V7X_LANES, axis=1)).astype(BF16))
        pv = jnp.dot(p, v, preferred_element_type=F32)
        if c == 0:
            acc_ref[rows, :] = pv
        else:
            acc_ref[rows, :] = acc_ref[rows, :] * jnp.exp2(m_prev - m_new) + pv
        m_ref[rows, :] = m_new
    for g in range(grp):
        acc = acc_ref[g * tq:(g + 1) * tq, :]
        o_ref[0, :, g * d:(g + 1) * d] = (acc[:, :d] / acc[:, d:d + 1]).astype(o_ref.dtype)


def gqa_attention(q3, kt3, v3, *, tq, tk):
    b, s, nq = q3.shape
    d = HEAD_DIM
    grp = GQA_Q_HEADS // GQA_KV_HEADS
    mrows = grp * tq
    vmem = (2 * (tq * grp * d * 2 + d * s * 2 + s * 128 * 2 + tq * grp * d * 2) + 3 * mrows * 128 * 4
            + 12 * tq * tk * 4)
    return pl.pallas_call(
        functools.partial(_gqa_attn_kernel, tk=tk),
        grid=(b, GQA_KV_HEADS, s // tq),
        in_specs=[
            pl.BlockSpec((1, tq, grp * d), lambda bi, kv, i: (bi, i, kv)),
            pl.BlockSpec((1, d, s), lambda bi, kv, i: (bi, kv, 0)),
            pl.BlockSpec((1, s, 2 * d), lambda bi, kv, i: (bi, 0, kv)),
        ],
        out_specs=pl.BlockSpec((1, tq, grp * d), lambda bi, kv, i: (bi, i, kv)),
        out_shape=jax.ShapeDtypeStruct((b, s, nq), BF16),
        scratch_shapes=[pltpu.VMEM((mrows, d), BF16), pltpu.VMEM((mrows, V7X_LANES), F32),
                        pltpu.VMEM((mrows, 2 * d), F32)],
        compiler_params=_params(("parallel", "parallel", "parallel"), vmem),
        name="gqa_attn",
    )(q3, kt3, v3)


def _mlstm_step(chains, ms, q_ref, k_ref, v_ref, r_sc, cm_sc, b_sc, st_sc, h_sc):
    L = MLSTM_CHUNK
    dh = MLSTM_HEAD_DIM
    assert L == dh
    scale = dh ** -0.5
    row_i = lax.broadcasted_iota(jnp.int32, (L, L), 0)
    col_i = lax.broadcasted_iota(jnp.int32, (L, L), 1)

    def col(x_row):
        return jnp.transpose(jnp.broadcast_to(x_row, (L, L)))

    pre = []
    for (hh, direction, c), m in zip(chains, ms):
        off = pl.multiple_of(c * L, L)
        lanes = slice(hh * dh, (hh + 1) * dh)
        q = q_ref[0, pl.ds(off, L), lanes]
        k = k_ref[0, pl.ds(off, L), lanes]
        v = v_ref[0, pl.ds(off, L), lanes]
        r_row = r_sc[hh, direction, pl.ds(c, 1), :]
        cm_row = cm_sc[hh, direction, pl.ds(c, 1), :]
        b_row = b_sc[hh, direction, pl.ds(c, 1), :]
        rmax = jnp.max(r_row, axis=-1, keepdims=True)
        btot = b_row[:, L - 1:L] if direction == 0 else b_row[:, 0:1]
        cmat = jnp.maximum(m, col(cm_row))
        mask = (row_i >= col_i) if direction == 0 else (row_i <= col_i)
        c_last = jnp.maximum(m, rmax)
        w_state = jnp.exp(col(r_row) - c_last) * scale
        pre.append(dict(
            off=off, lanes=lanes, q=q, k=k, v=v,
            w_intra=jnp.where(mask, jnp.exp(r_row - cmat), 0.0) * scale,
            w_inter=jnp.exp(m - cmat),
            den_floor=jnp.exp(-(col(b_row) + cmat)),
            decay=jnp.exp(m - c_last),
            kv_w=jnp.concatenate([w_state * v.astype(F32), w_state], axis=-1).astype(BF16),
            m_new=btot + c_last,
        ))
    s_raw = [lax.dot_general(p["q"], p["k"], (((1,), (1,)), ((), ())), preferred_element_type=F32) for p in pre]
    states = [st_sc[hh, direction] for hh, direction, _ in chains]
    inter = [jnp.dot(p["q"], st.astype(BF16), preferred_element_type=F32) for p, st in zip(pre, states)]
    upd = [lax.dot_general(p["k"], p["kv_w"], (((0,), (0,)), ((), ())), preferred_element_type=F32) for p in pre]
    for n, ((hh, direction, _), p) in enumerate(zip(chains, pre)):
        st_sc[hh, direction] = p["decay"] * states[n] + upd[n]
        v_aug = jnp.concatenate([p["v"], jnp.ones((L, dh), BF16)], axis=-1)
        intra = jnp.dot((s_raw[n] * p["w_intra"]).astype(BF16), v_aug, preferred_element_type=F32)
        h_aug = jnp.concatenate([p["w_inter"], p["w_inter"]], axis=-1) * inter[n] + intra
        h_sc[hh, direction, pl.ds(p["off"], L), :] = h_aug[:, :dh] / jnp.maximum(jnp.abs(h_aug[:, dh:]), p["den_floor"])
    return [p["m_new"] for p in pre]


def _mlstm_kernel(bias_ref, q_ref, k_ref, v_ref, o_ref, gate_ref, g_ref, y_ref,
                  r_sc, cm_sc, b_sc, h_sc, st_sc):
    L = MLSTM_CHUNK
    dh = MLSTM_HEAD_DIM
    hpb = q_ref.shape[2] // dh
    head0 = pl.program_id(1) * hpb
    nc = q_ref.shape[1] // L
    lane = lax.broadcasted_iota(jnp.int32, (nc, L), 1)
    shifts = [1 << t for t in range(int(math.log2(L)))]
    for hh in range(hpb):
        for d in range(2):
            i_pre = gate_ref[0, 2 * d, hh] + bias_ref[2 * d, head0 + hh]
            f_pre = gate_ref[0, 2 * d + 1, hh] + bias_ref[2 * d + 1, head0 + hh]
            logf = jnp.minimum(f_pre, 0.0) - jnp.log1p(jnp.exp(-jnp.abs(f_pre)))
            bc = logf
            for sh in shifts:
                if d == 0:
                    bc = bc + jnp.where(lane >= sh, pltpu.roll(bc, sh, axis=1), 0.0)
                else:
                    bc = bc + jnp.where(lane < L - sh, pltpu.roll(bc, L - sh, axis=1), 0.0)
            r = i_pre - bc
            cm = r
            for sh in shifts:
                if d == 0:
                    cm = jnp.maximum(cm, jnp.where(lane >= sh, pltpu.roll(cm, sh, axis=1), -jnp.inf))
                else:
                    cm = jnp.maximum(cm, jnp.where(lane < L - sh, pltpu.roll(cm, L - sh, axis=1), -jnp.inf))
            r_sc[hh, d] = r
            cm_sc[hh, d] = cm
            b_sc[hh, d] = bc
    st_sc[...] = jnp.zeros(st_sc.shape, F32)

    def body(c, ms):
        chains = [(hh, d, c if d == 0 else nc - 1 - c) for hh in range(hpb) for d in range(2)]
        return tuple(_mlstm_step(chains, ms, q_ref, k_ref, v_ref, r_sc, cm_sc, b_sc, st_sc, h_sc))

    lax.fori_loop(0, nc, body, tuple(jnp.zeros((1, 1), F32) for _ in range(2 * hpb)))
    for hh in range(hpb):
        lanes = slice(hh * dh, (hh + 1) * dh)
        hsum = h_sc[hh, 0] + h_sc[hh, 1]
        ms = jnp.mean(hsum * hsum, axis=-1, keepdims=True)
        y = hsum * lax.rsqrt(ms + EPS) * g_ref[:, lanes]
        y_ref[0, :, lanes] = (jax.nn.sigmoid(o_ref[0, :, lanes].astype(F32)) * y).astype(y_ref.dtype)


def mlstm_branch(p3, gates5, gate_bias, norm_g, *, heads_per_block):
    b, s, _ = p3.shape
    L = MLSTM_CHUNK
    hpb = heads_per_block
    wb = hpb * MLSTM_HEAD_DIM
    nc = s // L
    blk = lambda off: pl.BlockSpec((1, s, wb), lambda bi, h, off=off: (bi, 0, off // wb + h))
    vmem = 2 * 5 * s * wb * 2 + 2 * s * wb * 4 + 6 * hpb * nc * L * 4 + 4 * hpb * wb * wb * 4 + 3 * s * wb * 4
    return pl.pallas_call(
        _mlstm_kernel,
        grid=(b, MLSTM_HEADS // hpb),
        in_specs=[
            pl.BlockSpec(memory_space=pltpu.SMEM),
            blk(OFF_CQ), blk(OFF_CK), blk(OFF_CV), blk(OFF_CO),
            pl.BlockSpec((1, 4, hpb, nc, L), lambda bi, h: (bi, 0, h, 0, 0)),
            pl.BlockSpec((1, wb), lambda bi, h: (0, h)),
        ],
        out_specs=pl.BlockSpec((1, s, wb), lambda bi, h: (bi, 0, h)),
        out_shape=jax.ShapeDtypeStruct((b, s, BRANCH_WIDTH), BF16),
        scratch_shapes=[
            pltpu.VMEM((hpb, 2, nc, L), F32), pltpu.VMEM((hpb, 2, nc, L), F32), pltpu.VMEM((hpb, 2, nc, L), F32),
            pltpu.VMEM((hpb, 2, s, MLSTM_HEAD_DIM), F32),
            pltpu.VMEM((hpb, 2, MLSTM_HEAD_DIM, 2 * MLSTM_HEAD_DIM), F32),
        ],
        compiler_params=_params(("parallel", "parallel"), vmem),
        name="mlstm",
    )(gate_bias.astype(F32), p3, p3, p3, p3, gates5, norm_g.reshape(1, BRANCH_WIDTH).astype(F32))


def _rel_bucket(rel):
    half = REL_BUCKETS // 2
    max_exact = half // 2
    ret = jnp.where(rel > 0, half, 0)
    n = jnp.abs(rel)
    nf = jnp.maximum(n, 1).astype(F32)
    large = max_exact + (jnp.log(nf / max_exact) / math.log(REL_MAX_DIST / max_exact) * (half - max_exact)).astype(jnp.int32)
    large = jnp.minimum(large, half - 1)
    return ret + jnp.where(n < max_exact, n, large)


def diff_bias_tiles(rel_bias, t):
    assert t >= REL_MAX_DIST
    k = jnp.arange(2 * t, dtype=jnp.int32)
    rel = jnp.arange(-2, 3, dtype=jnp.int32)[:, None] * t + jnp.where(k < t, k, k - 2 * t)[None, :]
    onehot = (_rel_bucket(rel)[:, :, None] == jnp.arange(REL_BUCKETS, dtype=jnp.int32)).astype(F32)
    period = jnp.einsum('dkb,bh->hdk', onehot, rel_bias.astype(F32) * LOG2E, precision=lax.Precision.HIGHEST)
    far = period[:, 0::4, 0]
    return period, far


def _diff_attn_kernel(far_ref, q_ref, k_ref, v_ref, period_ref, lam_ref, g_ref, o_ref,
                      q_sc, kt_sc, vaug_sc, bias_sc, sc_buf, m_ref, acc_ref, *, t, lam_init):
    dq = DIFF_QK_DIM
    dv = DIFF_V_DIM
    s = k_ref.shape[1]
    nt = s // t
    head = pl.program_id(1)
    rb = sc_buf.shape[0]
    grp = sc_buf.shape[1] // t
    wrap = (grp - 1) * t
    assert nt % grp == 0 and t % rb == 0 and nt >= 4

    tr = max(t, 512)
    for r in range(0, s, tr):
        kt_sc[:, r:r + tr] = jnp.transpose(k_ref[0, r:r + tr, :].astype(F32)).astype(BF16)
    kt_sc[:, s:] = kt_sc[:, :wrap]
    vaug_sc[:s, :dv] = v_ref[0]
    vaug_sc[s:, :dv] = v_ref[0, :wrap, :]
    lane = lax.broadcasted_iota(jnp.int32, (s + wrap, dv), 1)
    vaug_sc[:, dv:] = jnp.where(lane == 0, 1.0, 0.0).astype(BF16)
    for dl in range(period_ref.shape[1]):
        full = jnp.broadcast_to(period_ref[0, dl:dl + 1, :], (t, 2 * t))
        bias_sc[dl] = pltpu.roll(full, 0, axis=1, stride=1, stride_axis=0)[:, :t]
    q_scale = dq ** -0.5 * LOG2E
    q_all = (q_ref[0].astype(F32) * q_scale).astype(BF16)
    q_sc[0] = q_all[:, :dq]
    q_sc[1] = q_all[:, dq:]
    lp = lam_ref[...]
    lam = (jnp.exp(jnp.sum(lp[0:1] * lp[1:2], axis=-1, keepdims=True))
           - jnp.exp(jnp.sum(lp[2:3] * lp[3:4], axis=-1, keepdims=True)) + lam_init)

    def key_tile(i, delta):
        j = lax.rem(i + delta, nt)
        return j, pl.multiple_of(j * t, t)

    def qk(i, item):
        pair, mp, hf = item
        row0 = pl.multiple_of(i * t + hf * rb, rb)
        _, off = key_tile(i, grp * pair)
        return jnp.dot(q_sc[mp, pl.ds(row0, rb), :], kt_sc[mp * dq:(mp + 1) * dq, pl.ds(off, grp * t)],
                       preferred_element_type=F32)

    items = [(pair, mp, hf) for pair in range(nt // grp) for mp in range(2) for hf in range(t // rb)]
    sc_buf[...] = qk(0, items[0])

    def tile_body(i, carry):
        sc_next = sc_buf[...]
        for n, (pair, mp, hf) in enumerate(items):
            sc = sc_next
            if n + 1 < len(items):
                sc_next = qk(i, items[n + 1])
            else:
                sc_next = qk(jnp.minimum(i + 1, nt - 1), items[0])
            rows = slice(hf * rb, (hf + 1) * rb)
            scs = [sc[:, p * t:(p + 1) * t] for p in range(grp)]
            _, off = key_tile(i, grp * pair)
            shifts, row_max = [], None
            for piece, delta in enumerate(range(grp * pair, grp * pair + grp)):
                j, _ = key_tile(i, delta)
                if delta in (0, 1, nt - 1):
                    tile = 2 if delta == 0 else jnp.clip(j - i, -2, 2) + 2
                    scs[piece] = scs[piece] + bias_sc[tile, rows, :]
                    shifts.append(None)
                    rm = jnp.max(scs[piece], axis=-1, keepdims=True)
                else:
                    shifts.append(jnp.where(j > i, far_ref[head, 1], far_ref[head, 0]))
                    rm = jnp.max(scs[piece], axis=-1, keepdims=True) + shifts[-1]
                row_max = rm if row_max is None else jnp.maximum(row_max, rm)
            if pair == 0:
                m_new = jnp.broadcast_to(row_max, (rb, V7X_LANES))
            else:
                m_prev = m_ref[mp, rows, :]
                m_new = jnp.maximum(m_prev, row_max)
            ps = []
            for piece in range(grp):
                m_sub = m_new if shifts[piece] is None else m_new - shifts[piece]
                ps.append(jnp.exp2((scs[piece] - pltpu.repeat(m_sub, t // V7X_LANES, axis=1)).astype(BF16)))
            pv = jnp.dot(jnp.concatenate(ps, axis=1), vaug_sc[pl.ds(off, grp * t), :], preferred_element_type=F32)
            if pair == 0:
                acc_ref[mp, rows, :] = pv
            else:
                alpha = jnp.exp2(m_prev - m_new)
                acc_ref[mp, rows, :] = acc_ref[mp, rows, :] * pltpu.repeat(alpha, 2 * dv // V7X_LANES, axis=1) + pv
            m_ref[mp, rows, :] = m_new
        sc_buf[...] = sc_next
        o0 = acc_ref[0, :, :dv] / acc_ref[0, :, dv:dv + 1]
        o1 = acc_ref[1, :, :dv] / acc_ref[1, :, dv:dv + 1]
        o = o0 - lam * o1
        ms = jnp.mean(o * o, axis=-1, keepdims=True)
        row0 = pl.multiple_of(i * t, t)
        o_ref[0, pl.ds(row0, t), :] = (o * lax.rsqrt(ms + EPS) * g_ref[...] * (1.0 - lam_init)).astype(o_ref.dtype)
        return carry

    lax.fori_loop(0, nt, tile_body, 0)


def diff_attention(p3, bias_period, bias_far, lam_params, sub_g, *, t, rows_per_item, tiles_per_item, layer_number):
    b, s, _ = p3.shape
    dv = DIFF_V_DIM
    n_off = bias_period.shape[1]
    assert s // t >= 4, "tiles 2 .. s/t-2 steps away from the query tile must all be beyond REL_MAX_DIST"
    lam_init = 0.8 - 0.6 * math.exp(-0.3 * (layer_number - 1))
    sw = s + (tiles_per_item - 1) * t
    item = rows_per_item * tiles_per_item * t
    vmem = (2 * (s * 128 * 2 + 128 * s * 2 + s * dv * 2 + s * dv * 2) + n_off * t * t * 4 + sw * (2 * dv + 128) * 2
            + 2 * s * 128 * 2 + item * 4 + 2 * t * 128 * 4 * 3 + 8 * item * 4)
    return pl.pallas_call(
        functools.partial(_diff_attn_kernel, t=t, lam_init=lam_init),
        grid=(b, DIFF_HEADS),
        in_specs=[
            pl.BlockSpec(memory_space=pltpu.SMEM),
            pl.BlockSpec((1, s, 128), lambda bi, h: (bi, 0, OFF_DQ // 128 + h)),
            pl.BlockSpec((1, s, 128), lambda bi, h: (bi, 0, OFF_DK // 128 + h)),
            pl.BlockSpec((1, s, dv), lambda bi, h: (bi, 0, OFF_DV // dv + h)),
            pl.BlockSpec((1, n_off, 2 * t), lambda bi, h: (h, 0, 0)),
            pl.BlockSpec((4, DIFF_QK_DIM), lambda bi, h: (0, 0)),
            pl.BlockSpec((1, dv), lambda bi, h: (0, 0)),
        ],
        out_specs=pl.BlockSpec((1, s, dv), lambda bi, h: (bi, 0, h)),
        out_shape=jax.ShapeDtypeStruct((b, s, BRANCH_WIDTH), BF16),
        scratch_shapes=[pltpu.VMEM((2, s, DIFF_QK_DIM), BF16), pltpu.VMEM((2 * DIFF_QK_DIM, sw), BF16),
                        pltpu.VMEM((sw, 2 * dv), BF16),
                        pltpu.VMEM((n_off, t, t), F32), pltpu.VMEM((rows_per_item, tiles_per_item * t), F32),
                        pltpu.VMEM((2, t, V7X_LANES), F32), pltpu.VMEM((2, t, 2 * dv), F32)],
        compiler_params=_params(("parallel", "parallel"), vmem),
        name="diff_attn",
    )(bias_far, p3, p3, p3, bias_period, lam_params.astype(F32), sub_g.reshape(1, dv).astype(F32))


def _merge_kernel(ya_ref, yb_ref, yc_ref, yd_ref, g_ref, wg_ref, wb_ref, wo_ref, x_ref, o_ref, wb_sc, wo_sc):
    dm = x_ref.shape[1]

    @pl.when(pl.program_id(0) == 0)
    def _():
        for n in range(wb_ref.shape[0]):
            wb_sc[n] = wb_ref[n].astype(BF16)
        wo_sc[...] = wo_ref[...].astype(BF16)

    tm = x_ref.shape[0]
    halves = [slice(0, tm // 2), slice(tm // 2, tm)]
    y_refs = (ya_ref, yb_ref, yc_ref, yd_ref)
    gates, branch = [], []
    for rows in halves:
        x = x_ref[rows, :]
        ms = jnp.mean(x * x, axis=-1, keepdims=True)
        u = (x * lax.rsqrt(ms + EPS) * g_ref[...]).astype(BF16)
        gates.append([jnp.dot(u, wg_ref[:, n * dm:(n + 1) * dm], preferred_element_type=F32)
                      for n in range(len(y_refs))])
        branch.append([jnp.dot(y_ref[rows, :], wb_sc[n], preferred_element_type=F32)
                       for n, y_ref in enumerate(y_refs)])
    for rows, gts, brs in zip(halves, gates, branch):
        merged = None
        for gt, br in zip(gts, brs):
            term = jax.nn.sigmoid(gt) * br
            merged = term if merged is None else merged + term
        o_ref[rows, :] = x_ref[rows, :] + jnp.dot(merged.astype(BF16), wo_sc[...], preferred_element_type=F32)


def merge_branches(ys, norm_g, w_gates, w_branch, w_out, x2, *, tm):
    m, dm = x2.shape
    w = BRANCH_WIDTH
    y_spec = pl.BlockSpec((tm, w), lambda i: (i, 0))
    vmem = (2 * (4 * tm * w * 2 + 2 * tm * dm * 4) + N_BRANCHES * dm * dm * 2 + (4 * w * dm + dm * dm) * (4 + 2)
            + 12 * tm * dm * 4)
    return pl.pallas_call(
        _merge_kernel,
        grid=(m // tm,),
        in_specs=[
            y_spec, y_spec, y_spec, y_spec,
            pl.BlockSpec((1, dm), lambda i: (0, 0)),
            pl.BlockSpec((dm, N_BRANCHES * dm), lambda i: (0, 0), pipeline_mode=pl.Buffered(1)),
            pl.BlockSpec((N_BRANCHES, w, dm), lambda i: (0, 0, 0), pipeline_mode=pl.Buffered(1)),
            pl.BlockSpec((dm, dm), lambda i: (0, 0), pipeline_mode=pl.Buffered(1)),
            pl.BlockSpec((tm, dm), lambda i: (i, 0)),
        ],
        out_specs=pl.BlockSpec((tm, dm), lambda i: (i, 0)),
        out_shape=jax.ShapeDtypeStruct((m, dm), F32),
        scratch_shapes=[pltpu.VMEM((N_BRANCHES, w, dm), BF16), pltpu.VMEM((dm, dm), BF16)],
        compiler_params=_params(("arbitrary",), vmem),
        name="merge",
    )(*ys, norm_g.reshape(1, dm).astype(F32), w_gates, w_branch.astype(F32), w_out.astype(F32), x2)


HALO_ROWS = 16


def _ffn_up_kernel(x_ref, xp_ref, xn_ref, g_ref, wa_ref, wl_ref, cw_ref, cb_ref, o_ref, xe_sc,
                   *, tiles_per_seq, col_chunk):
    tm, f = o_ref.shape
    h = HALO_ROWS
    pos = pl.program_id(0) % tiles_per_seq

    def norm(x):
        ms = jnp.mean(x * x, axis=-1, keepdims=True)
        return x * lax.rsqrt(ms + EPS) * g_ref[...]

    xe_sc[:h] = jnp.where(pos == 0, 0.0, norm(xp_ref[...])).astype(BF16)
    xe_sc[h:h + tm] = norm(x_ref[...]).astype(BF16)
    xe_sc[h + tm:] = jnp.where(pos == tiles_per_seq - 1, 0.0, norm(xn_ref[...])).astype(BF16)
    k0 = math.sqrt(2.0 / math.pi)
    bounds = list(range(0, f, col_chunk)) + [f]
    chunks = [slice(lo, hi) for lo, hi in zip(bounds[:-1], bounds[1:])]

    def project(cols):
        a_ext = jnp.dot(xe_sc[...], wa_ref[:, cols], preferred_element_type=F32)
        lin = jnp.dot(xe_sc[h:h + tm, :], wl_ref[:, cols], preferred_element_type=F32)
        return a_ext, lin

    nxt = project(chunks[0])
    for n, cols in enumerate(chunks):
        a_ext, lin = nxt
        if n + 1 < len(chunks):
            nxt = project(chunks[n + 1])
        cw = cw_ref[:, cols]
        c = (a_ext[h - 1:h - 1 + tm] * cw[0:1] + a_ext[h:h + tm] * cw[1:2] + a_ext[h + 1:h + 1 + tm] * cw[2:3]
             + cb_ref[:, cols])
        t = jnp.tanh(c * ((c * c) * (k0 * 0.044715) + k0))
        o_ref[:, cols] = ((c + c * t) * lin).astype(o_ref.dtype)


def ffn_up(x2, g, w_up, conv_w, conv_b, *, s, tm):
    m, dm = x2.shape
    f = w_up.shape[1] // 2
    tiles_per_seq = s // tm
    hb = tm // HALO_ROWS
    n_halo = m // HALO_ROWS
    vmem = 2 * (tm * dm * 4 + tm * f * 2) + 2 * dm * f * 2 + (tm + 2 * HALO_ROWS) * dm * 2 + 12 * tm * 512 * 4
    return pl.pallas_call(
        functools.partial(_ffn_up_kernel, tiles_per_seq=tiles_per_seq, col_chunk=512),
        grid=(m // tm,),
        in_specs=[
            pl.BlockSpec((tm, dm), lambda i: (i, 0)),
            pl.BlockSpec((HALO_ROWS, dm), lambda i: (jnp.maximum(i * hb - 1, 0), 0)),
            pl.BlockSpec((HALO_ROWS, dm), lambda i: (jnp.minimum((i + 1) * hb, n_halo - 1), 0)),
            pl.BlockSpec((1, dm), lambda i: (0, 0)),
            pl.BlockSpec((dm, f), lambda i: (0, 0), pipeline_mode=pl.Buffered(1)),
            pl.BlockSpec((dm, f), lambda i: (0, 1), pipeline_mode=pl.Buffered(1)),
            pl.BlockSpec((CONV_W, f), lambda i: (0, 0)),
            pl.BlockSpec((1, f), lambda i: (0, 0)),
        ],
        out_specs=pl.BlockSpec((tm, f), lambda i: (i, 0)),
        out_shape=jax.ShapeDtypeStruct((m, f), BF16),
        scratch_shapes=[pltpu.VMEM((tm + 2 * HALO_ROWS, dm), BF16)],
        compiler_params=_params(("parallel",), vmem),
        name="ffn_up",
    )(x2, x2, x2, g.reshape(1, dm).astype(F32), w_up, w_up, conv_w.astype(F32), conv_b.reshape(1, f).astype(F32))


def _ffn_down_kernel(h_ref, wd_ref, x_ref, gf_ref, o_ref, wd_sc, *, final_norm):
    @pl.when(pl.program_id(0) == 0)
    def _():
        wd_sc[...] = wd_ref[...].astype(BF16)

    y = x_ref[...] + jnp.dot(h_ref[...], wd_sc[...], preferred_element_type=F32)
    if final_norm:
        ms = jnp.mean(y * y, axis=-1, keepdims=True)
        y = y * lax.rsqrt(ms + EPS) * gf_ref[...]
    o_ref[...] = y


def ffn_down(hmid, w_down, x2, final_g, *, tm, final_norm):
    m, dm = x2.shape
    f = w_down.shape[0]
    vmem = 2 * (tm * f * 2 + 2 * tm * dm * 4) + f * dm * 4 + f * dm * 2 + 4 * tm * dm * 4
    return pl.pallas_call(
        functools.partial(_ffn_down_kernel, final_norm=final_norm),
        grid=(m // tm,),
        in_specs=[
            pl.BlockSpec((tm, f), lambda i: (i, 0)),
            pl.BlockSpec((f, dm), lambda i: (0, 0), pipeline_mode=pl.Buffered(1)),
            pl.BlockSpec((tm, dm), lambda i: (i, 0)),
            pl.BlockSpec((1, dm), lambda i: (0, 0)),
        ],
        out_specs=pl.BlockSpec((tm, dm), lambda i: (i, 0)),
        out_shape=jax.ShapeDtypeStruct((m, dm), F32),
        scratch_shapes=[pltpu.VMEM((f, dm), BF16)],
        compiler_params=_params(("arbitrary",), vmem),
        name="ffn_down",
    )(hmid, w_down.astype(F32), x2, final_g.reshape(1, dm).astype(F32))


def _arrange_w_in(w):
    widths = (512, 512, 128, 128, 512, 512, 512, 512, 16, 512, 512, 512, 4096)
    offs = [0]
    for wd in widths:
        offs.append(offs[-1] + wd)
    seg = lambda n: w[:, offs[n]:offs[n + 1]]
    a, bq, bk, bv, cq, ck, cv, co, cgate, dq, dk, dv, gates = (seg(n) for n in range(len(widths)))
    main = jnp.concatenate([a, bq, cq, ck, cv, co, dq, dk, dv, bk, bv], axis=1).astype(BF16)
    gate = jnp.pad(cgate, ((0, 0), (0, V7X_LANES - cgate.shape[1]))).astype(BF16)
    return main, gate, gates.astype(BF16)


def kernel(x, norm_mix_g, w_in, mlstm_gate_bias, qk_norm_g, mlstm_norm_g, diff_lambda, diff_norm_g, rel_bias,
           w_branch, w_out, norm_ffn_g, w_up, conv_w, conv_b, w_down, final_norm_g):
    b, s, dm = x.shape
    depth = w_in.shape[0]
    m = b * s
    d_ff = w_down.shape[1]
    L = MLSTM_CHUNK
    t_diff = 512

    bd, dft = fourier_tables(s)
    cos2, sin2 = rope_tables(s)
    bias_tiles, bias_far = diff_bias_tiles(rel_bias, t_diff)

    x2 = x.reshape(m, dm)
    for layer in range(depth):
        w_main, w_gate, w_branch_gates = _arrange_w_in(w_in[layer])
        p2, cgate = norm_matmul(x2, norm_mix_g[layer], w_main, w_gate, tm=512, tn=P_WIDTH)
        p3 = p2.reshape(b, s, P_WIDTH)

        y_a = fourier_mix(p3, bd, dft, tm=512)

        q_b, k_b, v_b = gqa_prep(p2, qk_norm_g[layer], cos2, sin2, s=s, tm=512)
        kt_b = jnp.swapaxes(k_b.reshape(b, s, GQA_KV_HEADS * HEAD_DIM), 1, 2)
        y_b = gqa_attention(q_b.reshape(b, s, -1), kt_b, v_b.reshape(b, s, -1), tq=512, tk=1024)

        gates5 = jnp.transpose(cgate[:, :4 * MLSTM_HEADS].reshape(b, s, 4, MLSTM_HEADS), (0, 2, 3, 1))
        gates5 = gates5.reshape(b, 4, MLSTM_HEADS, s // L, L)
        y_c = mlstm_branch(p3, gates5, mlstm_gate_bias[layer], mlstm_norm_g[layer], heads_per_block=2)

        y_d = diff_attention(p3, bias_tiles, bias_far, diff_lambda[layer], diff_norm_g[layer],
                             t=t_diff, rows_per_item=256, tiles_per_item=4, layer_number=layer + 1)

        ys = [y.reshape(m, BRANCH_WIDTH) for y in (y_a, y_b, y_c, y_d)]
        x2 = merge_branches(ys, norm_mix_g[layer], w_branch_gates, w_branch[layer], w_out[layer], x2, tm=512)

        half_lin = jnp.concatenate([jnp.ones((d_ff,), F32), jnp.full((d_ff,), 0.5, F32)])
        hmid = ffn_up(x2, norm_ffn_g[layer], (w_up[layer] * half_lin).astype(BF16), conv_w[layer], conv_b[layer],
                      s=s, tm=1024)
        x2 = ffn_down(hmid, w_down[layer], x2, final_norm_g, tm=1024, final_norm=(layer == depth - 1))
    return x2.reshape(b, s, dm)
```

```python
import functools
import math

import jax
import jax.numpy as jnp
from jax import lax
from jax.experimental import pallas as pl
from jax.experimental.pallas import tpu as pltpu

F32 = jnp.float32
BF16 = jnp.bfloat16

GRID_W = 64
HEAD_DIM = 64
BRANCH_WIDTH = 512
N_BRANCHES = 4
FOURIER_GROUP_DIM = 64
GQA_Q_HEADS = 8
GQA_KV_HEADS = 2
MLSTM_HEADS = 4
MLSTM_HEAD_DIM = 128
MLSTM_CHUNK = 128
DIFF_HEADS = 4
DIFF_QK_DIM = 64
DIFF_V_DIM = 128
REL_BUCKETS = 32
REL_MAX_DIST = 128
CONV_W = 3
ROPE_BASE = 10000.0
EPS = 1e-6
LOG2E = math.log2(math.e)

V7X_LANES = 128
V7X_VMEM_BYTES = 64 * 1024 * 1024
V7X_VMEM_CAP = V7X_VMEM_BYTES - 8 * 1024 * 1024

OFF_A = 0
OFF_BQ = 512
OFF_CQ = 1024
OFF_CK = 1536
OFF_CV = 2048
OFF_CO = 2560
OFF_DQ = 3072
OFF_DK = 3584
OFF_DV = 4096
OFF_BKV = 4608
P_WIDTH = 4864


def _params(sem, vmem_bytes):
    limit = int(min(max(vmem_bytes * 3 // 2 + (4 << 20), 16 << 20), V7X_VMEM_CAP))
    return pltpu.CompilerParams(dimension_semantics=sem, vmem_limit_bytes=limit)


def _norm_mm_kernel(x_ref, g_ref, w_ref, o_ref, xn_ref):
    @pl.when(pl.program_id(1) == 0)
    def _():
        x = x_ref[...]
        ms = jnp.mean(x * x, axis=-1, keepdims=True)
        xn_ref[...] = (x * lax.rsqrt(ms + EPS) * g_ref[...]).astype(BF16)

    o_ref[...] = jnp.dot(xn_ref[...], w_ref[...], preferred_element_type=F32).astype(o_ref.dtype)


def _norm_mm_gate_kernel(x_ref, g_ref, w_ref, wg_ref, o_ref, og_ref, xn_ref):
    @pl.when(pl.program_id(1) == 0)
    def _():
        x = x_ref[...]
        ms = jnp.mean(x * x, axis=-1, keepdims=True)
        xn = (x * lax.rsqrt(ms + EPS) * g_ref[...]).astype(BF16)
        xn_ref[...] = xn
        og_ref[...] = jnp.dot(xn, wg_ref[...], preferred_element_type=F32)

    o_ref[...] = jnp.dot(xn_ref[...], w_ref[...], preferred_element_type=F32).astype(o_ref.dtype)


def norm_matmul(x, g, w, w_gate=None, *, tm, tn):
    m, k = x.shape
    n = w.shape[1]
    grid = (m // tm, n // tn)
    vmem = 2 * tm * k * 4 + tm * k * 2 + 2 * k * tn * 2 + 2 * tm * tn * 2 + 4 * tm * k
    x_spec = pl.BlockSpec((tm, k), lambda i, j: (i, 0))
    g_spec = pl.BlockSpec((1, k), lambda i, j: (0, 0))
    w_spec = pl.BlockSpec((k, tn), lambda i, j: (0, j))
    o_spec = pl.BlockSpec((tm, tn), lambda i, j: (i, j))
    scratch = [pltpu.VMEM((tm, k), BF16)]
    g2 = g.reshape(1, k).astype(F32)
    if w_gate is None:
        return pl.pallas_call(
            _norm_mm_kernel,
            grid=grid,
            in_specs=[x_spec, g_spec, w_spec],
            out_specs=o_spec,
            out_shape=jax.ShapeDtypeStruct((m, n), BF16),
            scratch_shapes=scratch,
            compiler_params=_params(("parallel", "arbitrary"), vmem),
            name="norm_matmul",
        )(x, g2, w)
    ng = w_gate.shape[1]
    return pl.pallas_call(
        _norm_mm_gate_kernel,
        grid=grid,
        in_specs=[x_spec, g_spec, w_spec, pl.BlockSpec((k, ng), lambda i, j: (0, 0))],
        out_specs=[o_spec, pl.BlockSpec((tm, ng), lambda i, j: (i, 0))],
        out_shape=[jax.ShapeDtypeStruct((m, n), BF16), jax.ShapeDtypeStruct((m, ng), F32)],
        scratch_shapes=scratch,
        compiler_params=_params(("parallel", "arbitrary"), vmem),
        name="norm_matmul_gate",
    )(x, g2, w, w_gate)


FOURIER_HALO = 16


def _fourier_kernel(a_ref, bd_ref, dft_ref, halo_ref, flip_ref, o_ref, z_ref, ext_ref, *, tm, row_chunk):
    s = a_ref.shape[1]
    w = a_ref.shape[2]
    i = pl.program_id(1)

    @pl.when(i == 0)
    def _():
        for r in range(0, s, row_chunk):
            a = a_ref[0, r:r + row_chunk, :]
            zc = jnp.dot(a, bd_ref[...], preferred_element_type=F32)
            z_ref[r:r + row_chunk, :] = zc[:, :w].astype(BF16)
            z_ref[s + r:s + r + row_chunk, :] = zc[:, w:].astype(BF16)

    ext_ref[:tm] = dft_ref[...].reshape(tm, 2 * s)
    ext_ref[tm:] = halo_ref[0]
    p = jnp.dot(ext_ref[:, :s], z_ref[:s], preferred_element_type=F32)
    q = jnp.dot(ext_ref[:, s:], z_ref[s:], preferred_element_type=F32)
    lo = pl.multiple_of(i * tm, tm)
    o_ref[0, pl.ds(lo, tm), :] = (p[:tm] + q[:tm]).astype(o_ref.dtype)
    mirrored = (p[1:tm + 1] - q[1:tm + 1]).astype(BF16)
    hi = pl.multiple_of(s - (i + 1) * tm, tm)
    o_ref[0, pl.ds(hi, tm), :] = jnp.dot(flip_ref[...], mirrored, preferred_element_type=F32).astype(o_ref.dtype)


def fourier_tables(s):
    cg = FOURIER_GROUP_DIM
    jj = jnp.arange(cg, dtype=jnp.int32)
    ang_c = (2.0 * math.pi / cg) * ((jj[:, None] * jj[None, :]) % cg).astype(F32)
    eye_g = jnp.eye(BRANCH_WIDTH // cg, dtype=F32)
    bd_c = jnp.kron(eye_g, jnp.cos(ang_c)) * cg ** -0.5
    bd_s = jnp.kron(eye_g, jnp.sin(ang_c)) * cg ** -0.5
    bd = jnp.concatenate([bd_c, bd_s], axis=1).astype(BF16)
    n_lo = s // cg
    nn = jnp.arange(s, dtype=jnp.int32)
    k1 = jnp.arange(cg // 2 + 1, dtype=jnp.int32)
    ang_hi = (2.0 * math.pi / cg) * ((k1[:, None] * nn[None, :]) % cg).astype(F32)
    ll = jnp.arange(n_lo, dtype=jnp.int32)
    ang_lo = (2.0 * math.pi / s) * ((ll[:, None] * nn[None, :]) % s).astype(F32)
    ch, sh = jnp.cos(ang_hi), jnp.sin(ang_hi)
    cl, sl = jnp.cos(ang_lo), jnp.sin(ang_lo)
    scale = s ** -0.5
    a2 = jnp.concatenate([ch, -sh], axis=1)[:, None, :]
    c2 = jnp.concatenate([sh, ch], axis=1)[:, None, :]
    b2 = jnp.concatenate([cl, cl], axis=1)[None, :, :]
    d2 = jnp.concatenate([sl, sl], axis=1)[None, :, :]
    dft = ((a2 * b2 - c2 * d2) * scale).astype(BF16)
    return bd, dft


def fourier_mix(p3, bd, dft, *, tm):
    b, s, _ = p3.shape
    w = BRANCH_WIDTH
    n_lo = dft.shape[1]
    h = FOURIER_HALO
    assert (s // 2) % tm == 0 and tm % n_lo == 0 and n_lo % h == 0
    r = jnp.arange(tm, dtype=jnp.int32)
    flip = (r[:, None] + r[None, :] == tm - 1).astype(BF16)
    vmem = (s * w * 2 + 2 * w * w * 2 + 2 * (tm + h) * 2 * s * 2 + (tm + h) * 2 * s * 2 + 2 * s * w * 2
            + 2 * s * w * 2 + 8 * (tm + h) * w * 4)
    return pl.pallas_call(
        functools.partial(_fourier_kernel, tm=tm, row_chunk=min(s, 512)),
        grid=(b, s // 2 // tm),
        in_specs=[
            pl.BlockSpec((1, s, w), lambda bi, i: (bi, 0, OFF_A // w), pipeline_mode=pl.Buffered(1)),
            pl.BlockSpec((w, 2 * w), lambda bi, i: (0, 0), pipeline_mode=pl.Buffered(1)),
            pl.BlockSpec((tm // n_lo, n_lo, 2 * s), lambda bi, i: (i, 0, 0)),
            pl.BlockSpec((1, h, 2 * s), lambda bi, i: ((i + 1) * (tm // n_lo), 0, 0)),
            pl.BlockSpec((tm, tm), lambda bi, i: (0, 0)),
        ],
        out_specs=pl.BlockSpec((1, s, w), lambda bi, i: (bi, 0, 0)),
        out_shape=jax.ShapeDtypeStruct((b, s, w), BF16),
        scratch_shapes=[pltpu.VMEM((2 * s, w), BF16), pltpu.VMEM((tm + h, 2 * s), BF16)],
        compiler_params=_params(("parallel", "arbitrary"), vmem),
        name="fourier",
    )(p3, bd, dft, dft, flip)


def rope_tables(s):
    rows = s // GRID_W
    row_id = jnp.repeat(jnp.arange(rows, dtype=F32), GRID_W)
    col_id = jnp.tile(jnp.arange(GRID_W, dtype=F32), rows)
    n_pairs = HEAD_DIM // 4
    inv_freq = ROPE_BASE ** (-jnp.arange(n_pairs, dtype=F32) / n_pairs)
    ang = jnp.concatenate([row_id[:, None] * inv_freq, col_id[:, None] * inv_freq], axis=-1)
    cos, sin = jnp.cos(ang), jnp.sin(ang)
    return jnp.concatenate([cos, cos] * 2, axis=-1), jnp.concatenate([-sin, sin] * 2, axis=-1)


def _norm_rope(x, g, seg, cos2, sin2):
    half = HEAD_DIM // 2
    x2 = x * x
    hi = x2.astype(BF16)
    lo = (x2 - hi.astype(F32)).astype(BF16)
    ms = jnp.dot(hi, seg, preferred_element_type=F32) + jnp.dot(lo, seg, preferred_element_type=F32)
    y = x * lax.rsqrt(ms + EPS) * g
    lane = lax.broadcasted_iota(jnp.int32, (x.shape[0], V7X_LANES), 1)
    first_half = (lane % HEAD_DIM) < half
    outs = []
    for cb in range(x.shape[1] // V7X_LANES):
        yb = y[:, cb * V7X_LANES:(cb + 1) * V7X_LANES]
        rot = jnp.where(first_half, pltpu.roll(yb, V7X_LANES - half, axis=1), pltpu.roll(yb, half, axis=1))
        outs.append(yb * cos2 + rot * sin2)
    return outs[0] if len(outs) == 1 else jnp.concatenate(outs, axis=-1)


def _gqa_prep_kernel(q_ref, kv_ref, gq_ref, gk_ref, seg_ref, cos_ref, sin_ref, qo_ref, ko_ref, vo_ref):
    d = HEAD_DIM
    nk = GQA_KV_HEADS * d
    cos2, sin2 = cos_ref[...], sin_ref[...]
    q = q_ref[...].astype(F32)
    kv = kv_ref[...].astype(F32)
    qo_ref[...] = _norm_rope(q, gq_ref[...], seg_ref[...], cos2, sin2).astype(BF16)
    ko_ref[...] = _norm_rope(kv[:, :nk], gk_ref[...], seg_ref[:nk, :nk], cos2, sin2).astype(BF16)
    v = kv[:, nk:]
    lane = lax.broadcasted_iota(jnp.int32, v.shape, 1)
    ones_col = jnp.where(lane == d, 1.0, 0.0)
    vo_ref[:, :nk] = jnp.where(lane < d, v, ones_col).astype(BF16)
    vo_ref[:, nk:] = jnp.where(lane < d, pltpu.roll(v, d, axis=1), ones_col).astype(BF16)


def gqa_prep(p2, qk_g, cos2, sin2, *, s, tm):
    m = p2.shape[0]
    nq = GQA_Q_HEADS * HEAD_DIM
    nkv = GQA_KV_HEADS * HEAD_DIM
    assert nkv == V7X_LANES
    tiles_per_seq = s // tm
    q_scale = HEAD_DIM ** -0.5 * LOG2E
    gq = jnp.tile(qk_g[0].astype(F32) * q_scale, GQA_Q_HEADS).reshape(1, nq)
    gk = jnp.tile(qk_g[1].astype(F32), GQA_KV_HEADS).reshape(1, nkv)
    seg = jnp.kron(jnp.eye(GQA_Q_HEADS, dtype=F32), jnp.full((HEAD_DIM, HEAD_DIM), 1.0 / HEAD_DIM, F32)).astype(BF16)
    return pl.pallas_call(
        _gqa_prep_kernel,
        grid=(m // tm,),
        in_specs=[
            pl.BlockSpec((tm, nq), lambda i: (i, OFF_BQ // nq)),
            pl.BlockSpec((tm, 2 * nkv), lambda i: (i, OFF_BKV // (2 * nkv))),
            pl.BlockSpec((1, nq), lambda i: (0, 0)),
            pl.BlockSpec((1, nkv), lambda i: (0, 0)),
            pl.BlockSpec((nq, nq), lambda i: (0, 0)),
            pl.BlockSpec((tm, V7X_LANES), lambda i: (i % tiles_per_seq, 0)),
            pl.BlockSpec((tm, V7X_LANES), lambda i: (i % tiles_per_seq, 0)),
        ],
        out_specs=[
            pl.BlockSpec((tm, nq), lambda i: (i, 0)),
            pl.BlockSpec((tm, nkv), lambda i: (i, 0)),
            pl.BlockSpec((tm, 2 * nkv), lambda i: (i, 0)),
        ],
        out_shape=[
            jax.ShapeDtypeStruct((m, nq), BF16),
            jax.ShapeDtypeStruct((m, nkv), BF16),
            jax.ShapeDtypeStruct((m, 2 * nkv), BF16),
        ],
        compiler_params=_params(("parallel",), 16 * tm * nq * 4),
        name="gqa_prep",
    )(p2, p2, gq, gk, seg, cos2, sin2)


def _gqa_attn_kernel(q_ref, kt_ref, v_ref, o_ref, q_sc, m_ref, acc_ref, *, tk):
    d = HEAD_DIM
    tq = q_ref.shape[1]
    grp = q_ref.shape[2] // d
    s = kt_ref.shape[2]
    for g in range(grp):
        q_sc[g * tq:(g + 1) * tq, :] = q_ref[0, :, g * d:(g + 1) * d]

    def qk(item):
        c, g = item
        return jnp.dot(q_sc[g * tq:(g + 1) * tq, :], kt_ref[0, :, c * tk:(c + 1) * tk],
                       preferred_element_type=F32)

    items = [(c, g) for c in range(s // tk) for g in range(grp)]
    sc_next = qk(items[0])
    for n, (c, g) in enumerate(items):
        rows = slice(g * tq, (g + 1) * tq)
        sc = sc_next
        if n + 1 < len(items):
            sc_next = qk(items[n + 1])
        v = v_ref[0, c * tk:(c + 1) * tk, :]
        row_max = jnp.max(sc, axis=-1, keepdims=True)
        if c == 0:
            m_new = jnp.broadcast_to(row_max, (tq, V7X_LANES))
        else:
            m_prev = m_ref[rows, :]
            m_new = jnp.maximum(m_prev, row_max)
        p = jnp.exp2((sc - pltpu.repeat(m_new, tk // V7X_LANES, axis=1)).astype(BF16))
        pv = jnp.dot(p, v, preferred_element_type=F32)
        if c == 0:
            acc_ref[rows, :] = pv
        else:
            acc_ref[rows, :] = acc_ref[rows, :] * jnp.exp2(m_prev - m_new) + pv
        m_ref[rows, :] = m_new
    for g in range(grp):
        acc = acc_ref[g * tq:(g + 1) * tq, :]
        o_ref[0, :, g * d:(g + 1) * d] = (acc[:, :d] / acc[:, d:d + 1]).astype(o_ref.dtype)


def gqa_attention(q3, kt3, v3, *, tq, tk):
    b, s, nq = q3.shape
    d = HEAD_DIM
    grp = GQA_Q_HEADS // GQA_KV_HEADS
    mrows = grp * tq
    vmem = (2 * (tq * grp * d * 2 + d * s * 2 + s * 128 * 2 + tq * grp * d * 2) + 3 * mrows * 128 * 4
            + 12 * tq * tk * 4)
    return pl.pallas_call(
        functools.partial(_gqa_attn_kernel, tk=tk),
        grid=(b, GQA_KV_HEADS, s // tq),
        in_specs=[
            pl.BlockSpec((1, tq, grp * d), lambda bi, kv, i: (bi, i, kv)),
            pl.BlockSpec((1, d, s), lambda bi, kv, i: (bi, kv, 0)),
            pl.BlockSpec((1, s, 2 * d), lambda bi, kv, i: (bi, 0, kv)),
        ],
        out_specs=pl.BlockSpec((1, tq, grp * d), lambda bi, kv, i: (bi, i, kv)),
        out_shape=jax.ShapeDtypeStruct((b, s, nq), BF16),
        scratch_shapes=[pltpu.VMEM((mrows, d), BF16), pltpu.VMEM((mrows, V7X_LANES), F32),
                        pltpu.VMEM((mrows, 2 * d), F32)],
        compiler_params=_params(("parallel", "parallel", "parallel"), vmem),
        name="gqa_attn",
    )(q3, kt3, v3)


def _mlstm_step(chains, ms, q_ref, k_ref, v_ref, r_sc, cm_sc, b_sc, st_sc, h_sc):
    L = MLSTM_CHUNK
    dh = MLSTM_HEAD_DIM
    assert L == dh
    scale = dh ** -0.5
    row_i = lax.broadcasted_iota(jnp.int32, (L, L), 0)
    col_i = lax.broadcasted_iota(jnp.int32, (L, L), 1)

    def col(x_row):
        return jnp.transpose(jnp.broadcast_to(x_row, (L, L)))

    pre = []
    for (hh, direction, c), m in zip(chains, ms):
        off = pl.multiple_of(c * L, L)
        lanes = slice(hh * dh, (hh + 1) * dh)
        q = q_ref[0, pl.ds(off, L), lanes]
        k = k_ref[0, pl.ds(off, L), lanes]
        v = v_ref[0, pl.ds(off, L), lanes]
        r_row = r_sc[hh, direction, pl.ds(c, 1), :]
        cm_row = cm_sc[hh, direction, pl.ds(c, 1), :]
        b_row = b_sc[hh, direction, pl.ds(c, 1), :]
        rmax = jnp.max(r_row, axis=-1, keepdims=True)
        btot = b_row[:, L - 1:L] if direction == 0 else b_row[:, 0:1]
        cmat = jnp.maximum(m, col(cm_row))
        mask = (row_i >= col_i) if direction == 0 else (row_i <= col_i)
        c_last = jnp.maximum(m, rmax)
        w_state = jnp.exp(col(r_row) - c_last) * scale
        pre.append(dict(
            off=off, lanes=lanes, q=q, k=k, v=v,
            w_intra=jnp.where(mask, jnp.exp(r_row - cmat), 0.0) * scale,
            w_inter=jnp.exp(m - cmat),
            den_floor=jnp.exp(-(col(b_row) + cmat)),
            decay=jnp.exp(m - c_last),
            kv_w=jnp.concatenate([w_state * v.astype(F32), w_state], axis=-1).astype(BF16),
            m_new=btot + c_last,
        ))
    s_raw = [lax.dot_general(p["q"], p["k"], (((1,), (1,)), ((), ())), preferred_element_type=F32) for p in pre]
    states = [st_sc[hh, direction] for hh, direction, _ in chains]
    inter = [jnp.dot(p["q"], st.astype(BF16), preferred_element_type=F32) for p, st in zip(pre, states)]
    upd = [lax.dot_general(p["k"], p["kv_w"], (((0,), (0,)), ((), ())), preferred_element_type=F32) for p in pre]
    for n, ((hh, direction, _), p) in enumerate(zip(chains, pre)):
        st_sc[hh, direction] = p["decay"] * states[n] + upd[n]
        v_aug = jnp.concatenate([p["v"], jnp.ones((L, dh), BF16)], axis=-1)
        intra = jnp.dot((s_raw[n] * p["w_intra"]).astype(BF16), v_aug, preferred_element_type=F32)
        h_aug = jnp.concatenate([p["w_inter"], p["w_inter"]], axis=-1) * inter[n] + intra
        h_sc[hh, direction, pl.ds(p["off"], L), :] = h_aug[:, :dh] / jnp.maximum(jnp.abs(h_aug[:, dh:]), p["den_floor"])
    return [p["m_new"] for p in pre]


def _mlstm_kernel(bias_ref, q_ref, k_ref, v_ref, o_ref, gate_ref, g_ref, y_ref,
                  r_sc, cm_sc, b_sc, h_sc, st_sc):
    L = MLSTM_CHUNK
    dh = MLSTM_HEAD_DIM
    hpb = q_ref.shape[2] // dh
    head0 = pl.program_id(1) * hpb
    nc = q_ref.shape[1] // L
    lane = lax.broadcasted_iota(jnp.int32, (nc, L), 1)
    shifts = [1 << t for t in range(int(math.log2(L)))]
    for hh in range(hpb):
        for d in range(2):
            i_pre = gate_ref[0, 2 * d, hh] + bias_ref[2 * d, head0 + hh]
            f_pre = gate_ref[0, 2 * d + 1, hh] + bias_ref[2 * d + 1, head0 + hh]
            logf = jnp.minimum(f_pre, 0.0) - jnp.log1p(jnp.exp(-jnp.abs(f_pre)))
            bc = logf
            for sh in shifts:
                if d == 0:
                    bc = bc + jnp.where(lane >= sh, pltpu.roll(bc, sh, axis=1), 0.0)
                else:
                    bc = bc + jnp.where(lane < L - sh, pltpu.roll(bc, L - sh, axis=1), 0.0)
            r = i_pre - bc
            cm = r
            for sh in shifts:
                if d == 0:
                    cm = jnp.maximum(cm, jnp.where(lane >= sh, pltpu.roll(cm, sh, axis=1), -jnp.inf))
                else:
                    cm = jnp.maximum(cm, jnp.where(lane < L - sh, pltpu.roll(cm, L - sh, axis=1), -jnp.inf))
            r_sc[hh, d] = r
            cm_sc[hh, d] = cm
            b_sc[hh, d] = bc
    st_sc[...] = jnp.zeros(st_sc.shape, F32)

    def body(c, ms):
        chains = [(hh, d, c if d == 0 else nc - 1 - c) for hh in range(hpb) for d in range(2)]
        return tuple(_mlstm_step(chains, ms, q_ref, k_ref, v_ref, r_sc, cm_sc, b_sc, st_sc, h_sc))

    lax.fori_loop(0, nc, body, tuple(jnp.zeros((1, 1), F32) for _ in range(2 * hpb)))
    for hh in range(hpb):
        lanes = slice(hh * dh, (hh + 1) * dh)
        hsum = h_sc[hh, 0] + h_sc[hh, 1]
        ms = jnp.mean(hsum * hsum, axis=-1, keepdims=True)
        y = hsum * lax.rsqrt(ms + EPS) * g_ref[:, lanes]
        y_ref[0, :, lanes] = (jax.nn.sigmoid(o_ref[0, :, lanes].astype(F32)) * y).astype(y_ref.dtype)


def mlstm_branch(p3, gates5, gate_bias, norm_g, *, heads_per_block):
    b, s, _ = p3.shape
    L = MLSTM_CHUNK
    hpb = heads_per_block
    wb = hpb * MLSTM_HEAD_DIM
    nc = s // L
    blk = lambda off: pl.BlockSpec((1, s, wb), lambda bi, h, off=off: (bi, 0, off // wb + h))
    vmem = 2 * 5 * s * wb * 2 + 2 * s * wb * 4 + 6 * hpb * nc * L * 4 + 4 * hpb * wb * wb * 4 + 3 * s * wb * 4
    return pl.pallas_call(
        _mlstm_kernel,
        grid=(b, MLSTM_HEADS // hpb),
        in_specs=[
            pl.BlockSpec(memory_space=pltpu.SMEM),
            blk(OFF_CQ), blk(OFF_CK), blk(OFF_CV), blk(OFF_CO),
            pl.BlockSpec((1, 4, hpb, nc, L), lambda bi, h: (bi, 0, h, 0, 0)),
            pl.BlockSpec((1, wb), lambda bi, h: (0, h)),
        ],
        out_specs=pl.BlockSpec((1, s, wb), lambda bi, h: (bi, 0, h)),
        out_shape=jax.ShapeDtypeStruct((b, s, BRANCH_WIDTH), BF16),
        scratch_shapes=[
            pltpu.VMEM((hpb, 2, nc, L), F32), pltpu.VMEM((hpb, 2, nc, L), F32), pltpu.VMEM((hpb, 2, nc, L), F32),
            pltpu.VMEM((hpb, 2, s, MLSTM_HEAD_DIM), F32),
            pltpu.VMEM((hpb, 2, MLSTM_HEAD_DIM, 2 * MLSTM_HEAD_DIM), F32),
        ],
        compiler_params=_params(("parallel", "parallel"), vmem),
        name="mlstm",
    )(gate_bias.astype(F32), p3, p3, p3, p3, gates5, norm_g.reshape(1, BRANCH_WIDTH).astype(F32))


def _rel_bucket(rel):
    half = REL_BUCKETS // 2
    max_exact = half // 2
    ret = jnp.where(rel > 0, half, 0)
    n = jnp.abs(rel)
    nf = jnp.maximum(n, 1).astype(F32)
    large = max_exact + (jnp.log(nf / max_exact) / math.log(REL_MAX_DIST / max_exact) * (half - max_exact)).astype(jnp.int32)
    large = jnp.minimum(large, half - 1)
    return ret + jnp.where(n < max_exact, n, large)


def diff_bias_tiles(rel_bias, t):
    assert t >= REL_MAX_DIST
    k = jnp.arange(2 * t, dtype=jnp.int32)
    rel = jnp.arange(-2, 3, dtype=jnp.int32)[:, None] * t + jnp.where(k < t, k, k - 2 * t)[None, :]
    onehot = (_rel_bucket(rel)[:, :, None] == jnp.arange(REL_BUCKETS, dtype=jnp.int32)).astype(F32)
    period = jnp.einsum('dkb,bh->hdk', onehot, rel_bias.astype(F32) * LOG2E, precision=lax.Precision.HIGHEST)
    far = period[:, 0::4, 0]
    return period, far


def _diff_attn_kernel(far_ref, q_ref, k_ref, v_ref, period_ref, lam_ref, g_ref, o_ref,
                      q_sc, kt_sc, vaug_sc, bias_sc, sc_buf, m_ref, acc_ref, *, t, lam_init):
    dq = DIFF_QK_DIM
    dv = DIFF_V_DIM
    s = k_ref.shape[1]
    nt = s // t
    head = pl.program_id(1)
    rb = sc_buf.shape[0]
    grp = sc_buf.shape[1] // t
    wrap = (grp - 1) * t
    assert nt % grp == 0 and t % rb == 0 and nt >= 4

    tr = max(t, 512)
    for r in range(0, s, tr):
        kt_sc[:, r:r + tr] = jnp.transpose(k_ref[0, r:r + tr, :].astype(F32)).astype(BF16)
    kt_sc[:, s:] = kt_sc[:, :wrap]
    vaug_sc[:s, :dv] = v_ref[0]
    vaug_sc[s:, :dv] = v_ref[0, :wrap, :]
    lane = lax.broadcasted_iota(jnp.int32, (s + wrap, dv), 1)
    vaug_sc[:, dv:] = jnp.where(lane == 0, 1.0, 0.0).astype(BF16)
    for dl in range(period_ref.shape[1]):
        full = jnp.broadcast_to(period_ref[0, dl:dl + 1, :], (t, 2 * t))
        bias_sc[dl] = pltpu.roll(full, 0, axis=1, stride=1, stride_axis=0)[:, :t]
    q_scale = dq ** -0.5 * LOG2E
    q_all = (q_ref[0].astype(F32) * q_scale).astype(BF16)
    q_sc[0] = q_all[:, :dq]
    q_sc[1] = q_all[:, dq:]
    lp = lam_ref[...]
    lam = (jnp.exp(jnp.sum(lp[0:1] * lp[1:2], axis=-1, keepdims=True))
           - jnp.exp(jnp.sum(lp[2:3] * lp[3:4], axis=-1, keepdims=True)) + lam_init)

    def key_tile(i, delta):
        j = lax.rem(i + delta, nt)
        return j, pl.multiple_of(j * t, t)

    def qk(i, item):
        pair, mp, hf = item
        row0 = pl.multiple_of(i * t + hf * rb, rb)
        _, off = key_tile(i, grp * pair)
        return jnp.dot(q_sc[mp, pl.ds(row0, rb), :], kt_sc[mp * dq:(mp + 1) * dq, pl.ds(off, grp * t)],
                       preferred_element_type=F32)

    items = [(pair, mp, hf) for pair in range(nt // grp) for mp in range(2) for hf in range(t // rb)]
    sc_buf[...] = qk(0, items[0])

    def tile_body(i, carry):
        sc_next = sc_buf[...]
        for n, (pair, mp, hf) in enumerate(items):
            sc = sc_next
            if n + 1 < len(items):
                sc_next = qk(i, items[n + 1])
            else:
                sc_next = qk(jnp.minimum(i + 1, nt - 1), items[0])
            rows = slice(hf * rb, (hf + 1) * rb)
            scs = [sc[:, p * t:(p + 1) * t] for p in range(grp)]
            _, off = key_tile(i, grp * pair)
            shifts, row_max = [], None
            for piece, delta in enumerate(range(grp * pair, grp * pair + grp)):
                j, _ = key_tile(i, delta)
                if delta in (0, 1, nt - 1):
                    tile = 2 if delta == 0 else jnp.clip(j - i, -2, 2) + 2
                    scs[piece] = scs[piece] + bias_sc[tile, rows, :]
                    shifts.append(None)
                    rm = jnp.max(scs[piece], axis=-1, keepdims=True)
                else:
                    shifts.append(jnp.where(j > i, far_ref[head, 1], far_ref[head, 0]))
                    rm = jnp.max(scs[piece], axis=-1, keepdims=True) + shifts[-1]
                row_max = rm if row_max is None else jnp.maximum(row_max, rm)
            if pair == 0:
                m_new = jnp.broadcast_to(row_max, (rb, V7X_LANES))
            else:
                m_prev = m_ref[mp, rows, :]
                m_new = jnp.maximum(m_prev, row_max)
            ps = []
            for piece in range(grp):
                m_sub = m_new if shifts[piece] is None else m_new - shifts[piece]
                ps.append(jnp.exp2((scs[piece] - pltpu.repeat(m_sub, t // V7X_LANES, axis=1)).astype(BF16)))
            pv = jnp.dot(jnp.concatenate(ps, axis=1), vaug_sc[pl.ds(off, grp * t), :], preferred_element_type=F32)
            if pair == 0:
                acc_ref[mp, rows, :] = pv
            else:
                alpha = jnp.exp2(m_prev - m_new)
                acc_ref[mp, rows, :] = acc_ref[mp, rows, :] * pltpu.repeat(alpha, 2 * dv // V7X_LANES, axis=1) + pv
            m_ref[mp, rows, :] = m_new
        sc_buf[...] = sc_next
        o0 = acc_ref[0, :, :dv] / acc_ref[0, :, dv:dv + 1]
        o1 = acc_ref[1, :, :dv] / acc_ref[1, :, dv:dv + 1]
        o = o0 - lam * o1
        ms = jnp.mean(o * o, axis=-1, keepdims=True)
        row0 = pl.multiple_of(i * t, t)
        o_ref[0, pl.ds(row0, t), :] = (o * lax.rsqrt(ms + EPS) * g_ref[...] * (1.0 - lam_init)).astype(o_ref.dtype)
        return carry

    lax.fori_loop(0, nt, tile_body, 0)


def diff_attention(p3, bias_period, bias_far, lam_params, sub_g, *, t, rows_per_item, tiles_per_item, layer_number):
    b, s, _ = p3.shape
    dv = DIFF_V_DIM
    n_off = bias_period.shape[1]
    assert s // t >= 4, "tiles 2 .. s/t-2 steps away from the query tile must all be beyond REL_MAX_DIST"
    lam_init = 0.8 - 0.6 * math.exp(-0.3 * (layer_number - 1))
    sw = s + (tiles_per_item - 1) * t
    item = rows_per_item * tiles_per_item * t
    vmem = (2 * (s * 128 * 2 + 128 * s * 2 + s * dv * 2 + s * dv * 2) + n_off * t * t * 4 + sw * (2 * dv + 128) * 2
            + 2 * s * 128 * 2 + item * 4 + 2 * t * 128 * 4 * 3 + 8 * item * 4)
    return pl.pallas_call(
        functools.partial(_diff_attn_kernel, t=t, lam_init=lam_init),
        grid=(b, DIFF_HEADS),
        in_specs=[
            pl.BlockSpec(memory_space=pltpu.SMEM),
            pl.BlockSpec((1, s, 128), lambda bi, h: (bi, 0, OFF_DQ // 128 + h)),
            pl.BlockSpec((1, s, 128), lambda bi, h: (bi, 0, OFF_DK // 128 + h)),
            pl.BlockSpec((1, s, dv), lambda bi, h: (bi, 0, OFF_DV // dv + h)),
            pl.BlockSpec((1, n_off, 2 * t), lambda bi, h: (h, 0, 0)),
            pl.BlockSpec((4, DIFF_QK_DIM), lambda bi, h: (0, 0)),
            pl.BlockSpec((1, dv), lambda bi, h: (0, 0)),
        ],
        out_specs=pl.BlockSpec((1, s, dv), lambda bi, h: (bi, 0, h)),
        out_shape=jax.ShapeDtypeStruct((b, s, BRANCH_WIDTH), BF16),
        scratch_shapes=[pltpu.VMEM((2, s, DIFF_QK_DIM), BF16), pltpu.VMEM((2 * DIFF_QK_DIM, sw), BF16),
                        pltpu.VMEM((sw, 2 * dv), BF16),
                        pltpu.VMEM((n_off, t, t), F32), pltpu.VMEM((rows_per_item, tiles_per_item * t), F32),
                        pltpu.VMEM((2, t, V7X_LANES), F32), pltpu.VMEM((2, t, 2 * dv), F32)],
        compiler_params=_params(("parallel", "parallel"), vmem),
        name="diff_attn",
    )(bias_far, p3, p3, p3, bias_period, lam_params.astype(F32), sub_g.reshape(1, dv).astype(F32))


def _merge_kernel(ya_ref, yb_ref, yc_ref, yd_ref, g_ref, wg_ref, wb_ref, wo_ref, x_ref, o_ref, wb_sc, wo_sc):
    dm = x_ref.shape[1]

    @pl.when(pl.program_id(0) == 0)
    def _():
        for n in range(wb_ref.shape[0]):
            wb_sc[n] = wb_ref[n].astype(BF16)
        wo_sc[...] = wo_ref[...].astype(BF16)

    tm = x_ref.shape[0]
    halves = [slice(0, tm // 2), slice(tm // 2, tm)]
    y_refs = (ya_ref, yb_ref, yc_ref, yd_ref)
    gates, branch = [], []
    for rows in halves:
        x = x_ref[rows, :]
        ms = jnp.mean(x * x, axis=-1, keepdims=True)
        u = (x * lax.rsqrt(ms + EPS) * g_ref[...]).astype(BF16)
        gates.append([jnp.dot(u, wg_ref[:, n * dm:(n + 1) * dm], preferred_element_type=F32)
                      for n in range(len(y_refs))])
        branch.append([jnp.dot(y_ref[rows, :], wb_sc[n], preferred_element_type=F32)
                       for n, y_ref in enumerate(y_refs)])
    for rows, gts, brs in zip(halves, gates, branch):
        merged = None
        for gt, br in zip(gts, brs):
            term = jax.nn.sigmoid(gt) * br
            merged = term if merged is None else merged + term
        o_ref[rows, :] = x_ref[rows, :] + jnp.dot(merged.astype(BF16), wo_sc[...], preferred_element_type=F32)


def merge_branches(ys, norm_g, w_gates, w_branch, w_out, x2, *, tm):
    m, dm = x2.shape
    w = BRANCH_WIDTH
    y_spec = pl.BlockSpec((tm, w), lambda i: (i, 0))
    vmem = (2 * (4 * tm * w * 2 + 2 * tm * dm * 4) + N_BRANCHES * dm * dm * 2 + (4 * w * dm + dm * dm) * (4 + 2)
            + 12 * tm * dm * 4)
    return pl.pallas_call(
        _merge_kernel,
        grid=(m // tm,),
        in_specs=[
            y_spec, y_spec, y_spec, y_spec,
            pl.BlockSpec((1, dm), lambda i: (0, 0)),
            pl.BlockSpec((dm, N_BRANCHES * dm), lambda i: (0, 0), pipeline_mode=pl.Buffered(1)),
            pl.BlockSpec((N_BRANCHES, w, dm), lambda i: (0, 0, 0), pipeline_mode=pl.Buffered(1)),
            pl.BlockSpec((dm, dm), lambda i: (0, 0), pipeline_mode=pl.Buffered(1)),
            pl.BlockSpec((tm, dm), lambda i: (i, 0)),
        ],
        out_specs=pl.BlockSpec((tm, dm), lambda i: (i, 0)),
        out_shape=jax.ShapeDtypeStruct((m, dm), F32),
        scratch_shapes=[pltpu.VMEM((N_BRANCHES, w, dm), BF16), pltpu.VMEM((dm, dm), BF16)],
        compiler_params=_params(("arbitrary",), vmem),
        name="merge",
    )(*ys, norm_g.reshape(1, dm).astype(F32), w_gates, w_branch.astype(F32), w_out.astype(F32), x2)


HALO_ROWS = 16


def _conv_ffn_kernel(x_ref, xp_ref, xn_ref, g_ref, wa_ref, wl_ref, cw_ref, cb_ref, wd_ref, gf_ref, o_ref,
                     xe_sc, wd_sc, *, tiles_per_seq, col_chunk, final_norm):
    tm = o_ref.shape[0]
    f = wa_ref.shape[1]
    h = HALO_ROWS
    pos = pl.program_id(0) % tiles_per_seq

    @pl.when(pl.program_id(0) == 0)
    def _():
        wd_sc[...] = wd_ref[...].astype(BF16)

    def norm(x):
        ms = jnp.mean(x * x, axis=-1, keepdims=True)
        return x * lax.rsqrt(ms + EPS) * g_ref[...]

    xe_sc[:h] = jnp.where(pos == 0, 0.0, norm(xp_ref[...])).astype(BF16)
    xe_sc[h:h + tm] = norm(x_ref[...]).astype(BF16)
    xe_sc[h + tm:] = jnp.where(pos == tiles_per_seq - 1, 0.0, norm(xn_ref[...])).astype(BF16)
    k0 = math.sqrt(2.0 / math.pi)
    bounds = list(range(0, f, col_chunk)) + [f]
    chunks = [slice(lo, hi) for lo, hi in zip(bounds[:-1], bounds[1:])]

    def project(cols):
        a_ext = jnp.dot(xe_sc[...], wa_ref[:, cols], preferred_element_type=F32)
        lin = jnp.dot(xe_sc[h:h + tm, :], wl_ref[:, cols], preferred_element_type=F32)
        return a_ext, lin

    y = x_ref[...]
    nxt = project(chunks[0])
    for n, cols in enumerate(chunks):
        a_ext, lin = nxt
        if n + 1 < len(chunks):
            nxt = project(chunks[n + 1])
        cw = cw_ref[:, cols]
        c = (a_ext[h - 1:h - 1 + tm] * cw[0:1] + a_ext[h:h + tm] * cw[1:2] + a_ext[h + 1:h + 1 + tm] * cw[2:3]
             + cb_ref[:, cols])
        t = jnp.tanh(c * ((c * c) * (k0 * 0.044715) + k0))
        hmid = ((c + c * t) * lin).astype(BF16)
        y = y + jnp.dot(hmid, wd_sc[cols, :], preferred_element_type=F32)
    if final_norm:
        ms = jnp.mean(y * y, axis=-1, keepdims=True)
        y = y * lax.rsqrt(ms + EPS) * gf_ref[...]
    o_ref[...] = y


def conv_ffn(x2, g, w_up, conv_w, conv_b, w_down, final_g, *, s, tm, final_norm):
    m, dm = x2.shape
    f = w_up.shape[1] // 2
    tiles_per_seq = s // tm
    hb = tm // HALO_ROWS
    n_halo = m // HALO_ROWS
    vmem = (2 * 2 * tm * dm * 4 + 2 * dm * f * 2 + f * dm * (4 + 2) + (tm + 2 * HALO_ROWS) * dm * 2
            + 12 * tm * 512 * 4 + 2 * tm * dm * 4)
    return pl.pallas_call(
        functools.partial(_conv_ffn_kernel, tiles_per_seq=tiles_per_seq, col_chunk=1024, final_norm=final_norm),
        grid=(m // tm,),
        in_specs=[
            pl.BlockSpec((tm, dm), lambda i: (i, 0)),
            pl.BlockSpec((HALO_ROWS, dm), lambda i: (jnp.maximum(i * hb - 1, 0), 0)),
            pl.BlockSpec((HALO_ROWS, dm), lambda i: (jnp.minimum((i + 1) * hb, n_halo - 1), 0)),
            pl.BlockSpec((1, dm), lambda i: (0, 0)),
            pl.BlockSpec((dm, f), lambda i: (0, 0), pipeline_mode=pl.Buffered(1)),
            pl.BlockSpec((dm, f), lambda i: (0, 1), pipeline_mode=pl.Buffered(1)),
            pl.BlockSpec((CONV_W, f), lambda i: (0, 0)),
            pl.BlockSpec((1, f), lambda i: (0, 0)),
            pl.BlockSpec((f, dm), lambda i: (0, 0), pipeline_mode=pl.Buffered(1)),
            pl.BlockSpec((1, dm), lambda i: (0, 0)),
        ],
        out_specs=pl.BlockSpec((tm, dm), lambda i: (i, 0)),
        out_shape=jax.ShapeDtypeStruct((m, dm), F32),
        scratch_shapes=[pltpu.VMEM((tm + 2 * HALO_ROWS, dm), BF16), pltpu.VMEM((f, dm), BF16)],
        compiler_params=_params(("arbitrary",), vmem),
        name="conv_ffn",
    )(x2, x2, x2, g.reshape(1, dm).astype(F32), w_up, w_up, conv_w.astype(F32), conv_b.reshape(1, f).astype(F32),
      w_down.astype(F32), final_g.reshape(1, dm).astype(F32))


def _arrange_w_in(w):
    widths = (512, 512, 128, 128, 512, 512, 512, 512, 16, 512, 512, 512, 4096)
    offs = [0]
    for wd in widths:
        offs.append(offs[-1] + wd)
    seg = lambda n: w[:, offs[n]:offs[n + 1]]
    a, bq, bk, bv, cq, ck, cv, co, cgate, dq, dk, dv, gates = (seg(n) for n in range(len(widths)))
    main = jnp.concatenate([a, bq, cq, ck, cv, co, dq, dk, dv, bk, bv], axis=1).astype(BF16)
    gate = jnp.pad(cgate, ((0, 0), (0, V7X_LANES - cgate.shape[1]))).astype(BF16)
    return main, gate, gates.astype(BF16)


def kernel(x, norm_mix_g, w_in, mlstm_gate_bias, qk_norm_g, mlstm_norm_g, diff_lambda, diff_norm_g, rel_bias,
           w_branch, w_out, norm_ffn_g, w_up, conv_w, conv_b, w_down, final_norm_g):
    b, s, dm = x.shape
    depth = w_in.shape[0]
    m = b * s
    d_ff = w_down.shape[1]
    L = MLSTM_CHUNK
    t_diff = 512

    bd, dft = fourier_tables(s)
    cos2, sin2 = rope_tables(s)
    bias_tiles, bias_far = diff_bias_tiles(rel_bias, t_diff)

    x2 = x.reshape(m, dm)
    for layer in range(depth):
        w_main, w_gate, w_branch_gates = _arrange_w_in(w_in[layer])
        p2, cgate = norm_matmul(x2, norm_mix_g[layer], w_main, w_gate, tm=512, tn=P_WIDTH)
        p3 = p2.reshape(b, s, P_WIDTH)

        y_a = fourier_mix(p3, bd, dft, tm=512)

        q_b, k_b, v_b = gqa_prep(p2, qk_norm_g[layer], cos2, sin2, s=s, tm=512)
        kt_b = jnp.swapaxes(k_b.reshape(b, s, GQA_KV_HEADS * HEAD_DIM), 1, 2)
        y_b = gqa_attention(q_b.reshape(b, s, -1), kt_b, v_b.reshape(b, s, -1), tq=512, tk=1024)

        gates5 = jnp.transpose(cgate[:, :4 * MLSTM_HEADS].reshape(b, s, 4, MLSTM_HEADS), (0, 2, 3, 1))
        gates5 = gates5.reshape(b, 4, MLSTM_HEADS, s // L, L)
        y_c = mlstm_branch(p3, gates5, mlstm_gate_bias[layer], mlstm_norm_g[layer], heads_per_block=2)

        y_d = diff_attention(p3, bias_tiles, bias_far, diff_lambda[layer], diff_norm_g[layer],
                             t=t_diff, rows_per_item=256, tiles_per_item=4, layer_number=layer + 1)

        ys = [y.reshape(m, BRANCH_WIDTH) for y in (y_a, y_b, y_c, y_d)]
        x2 = merge_branches(ys, norm_mix_g[layer], w_branch_gates, w_branch[layer], w_out[layer], x2, tm=512)

        half_lin = jnp.concatenate([jnp.ones((d_ff,), F32), jnp.full((d_ff,), 0.5, F32)])
        x2 = conv_ffn(x2, norm_ffn_g[layer], (w_up[layer] * half_lin).astype(BF16), conv_w[layer], conv_b[layer],
                      w_down[layer], final_norm_g, s=s, tm=512, final_norm=(layer == depth - 1))
    return x2.reshape(b, s, dm)
```

```python
import functools
import math

import jax
import jax.numpy as jnp
from jax import lax
from jax.experimental import pallas as pl
from jax.experimental.pallas import tpu as pltpu

F32 = jnp.float32
BF16 = jnp.bfloat16

GRID_W = 64
HEAD_DIM = 64
BRANCH_WIDTH = 512
N_BRANCHES = 4
FOURIER_GROUP_DIM = 64
GQA_Q_HEADS = 8
GQA_KV_HEADS = 2
MLSTM_HEADS = 4
MLSTM_HEAD_DIM = 128
MLSTM_CHUNK = 128
DIFF_HEADS = 4
DIFF_QK_DIM = 64
DIFF_V_DIM = 128
REL_BUCKETS = 32
REL_MAX_DIST = 128
CONV_W = 3
ROPE_BASE = 10000.0
EPS = 1e-6
LOG2E = math.log2(math.e)

V7X_LANES = 128
V7X_VMEM_BYTES = 64 * 1024 * 1024
V7X_VMEM_CAP = V7X_VMEM_BYTES - 8 * 1024 * 1024

OFF_A = 0
OFF_BQ = 512
OFF_CQ = 1024
OFF_CK = 1536
OFF_CV = 2048
OFF_CO = 2560
OFF_DQ = 3072
OFF_DK = 3584
OFF_DV = 4096
OFF_BKV = 4608
P_WIDTH = 4864


def _params(sem, vmem_bytes):
    limit = int(min(max(vmem_bytes * 3 // 2 + (4 << 20), 16 << 20), V7X_VMEM_CAP))
    return pltpu.CompilerParams(dimension_semantics=sem, vmem_limit_bytes=limit)


def _norm_mm_kernel(x_ref, g_ref, w_ref, o_ref, xn_ref):
    @pl.when(pl.program_id(1) == 0)
    def _():
        x = x_ref[...]
        ms = jnp.mean(x * x, axis=-1, keepdims=True)
        xn_ref[...] = (x * lax.rsqrt(ms + EPS) * g_ref[...]).astype(BF16)

    o_ref[...] = jnp.dot(xn_ref[...], w_ref[...], preferred_element_type=F32).astype(o_ref.dtype)


def _norm_mm_gate_kernel(x_ref, g_ref, w_ref, wg_ref, o_ref, og_ref, xn_ref):
    @pl.when(pl.program_id(1) == 0)
    def _():
        x = x_ref[...]
        ms = jnp.mean(x * x, axis=-1, keepdims=True)
        xn = (x * lax.rsqrt(ms + EPS) * g_ref[...]).astype(BF16)
        xn_ref[...] = xn
        og_ref[...] = jnp.dot(xn, wg_ref[...], preferred_element_type=F32)

    o_ref[...] = jnp.dot(xn_ref[...], w_ref[...], preferred_element_type=F32).astype(o_ref.dtype)


def norm_matmul(x, g, w, w_gate=None, *, tm, tn):
    m, k = x.shape
    n = w.shape[1]
    grid = (m // tm, n // tn)
    vmem = 2 * tm * k * 4 + tm * k * 2 + 2 * k * tn * 2 + 2 * tm * tn * 2 + 4 * tm * k
    x_spec = pl.BlockSpec((tm, k), lambda i, j: (i, 0))
    g_spec = pl.BlockSpec((1, k), lambda i, j: (0, 0))
    w_spec = pl.BlockSpec((k, tn), lambda i, j: (0, j))
    o_spec = pl.BlockSpec((tm, tn), lambda i, j: (i, j))
    scratch = [pltpu.VMEM((tm, k), BF16)]
    g2 = g.reshape(1, k).astype(F32)
    if w_gate is None:
        return pl.pallas_call(
            _norm_mm_kernel,
            grid=grid,
            in_specs=[x_spec, g_spec, w_spec],
            out_specs=o_spec,
            out_shape=jax.ShapeDtypeStruct((m, n), BF16),
            scratch_shapes=scratch,
            compiler_params=_params(("parallel", "arbitrary"), vmem),
            name="norm_matmul",
        )(x, g2, w)
    ng = w_gate.shape[1]
    return pl.pallas_call(
        _norm_mm_gate_kernel,
        grid=grid,
        in_specs=[x_spec, g_spec, w_spec, pl.BlockSpec((k, ng), lambda i, j: (0, 0))],
        out_specs=[o_spec, pl.BlockSpec((tm, ng), lambda i, j: (i, 0))],
        out_shape=[jax.ShapeDtypeStruct((m, n), BF16), jax.ShapeDtypeStruct((m, ng), F32)],
        scratch_shapes=scratch,
        compiler_params=_params(("parallel", "arbitrary"), vmem),
        name="norm_matmul_gate",
    )(x, g2, w, w_gate)


FOURIER_HALO = 16


def _fourier_kernel(a_ref, bd_ref, hi_ref, lo_ref, flip_ref, o_ref, z_ref, ext_ref, *, tm, row_chunk):
    s = a_ref.shape[1]
    w = a_ref.shape[2]
    n_lo = lo_ref.shape[1]
    i = pl.program_id(1)

    @pl.when(i == 0)
    def _():
        for r in range(0, s, row_chunk):
            a = a_ref[0, r:r + row_chunk, :]
            zc = jnp.dot(a, bd_ref[...], preferred_element_type=F32)
            z_ref[r:r + row_chunk, :] = zc[:, :w].astype(BF16)
            z_ref[s + r:s + r + row_chunk, :] = zc[:, w:].astype(BF16)

    cl, sl = lo_ref[0], lo_ref[1]
    for r in range(tm // n_lo + 1):
        rows = n_lo if r < tm // n_lo else ext_ref.shape[0] - tm
        k1 = i * (tm // n_lo) + r
        ch = hi_ref[0, pl.ds(k1, 1), :]
        nsh = hi_ref[1, pl.ds(k1, 1), :]
        ext_ref[r * n_lo:r * n_lo + rows, :s] = (cl[:rows] * ch + sl[:rows] * nsh).astype(BF16)
        ext_ref[r * n_lo:r * n_lo + rows, s:] = (cl[:rows] * nsh - sl[:rows] * ch).astype(BF16)
    p = jnp.dot(ext_ref[:, :s], z_ref[:s], preferred_element_type=F32)
    q = jnp.dot(ext_ref[:, s:], z_ref[s:], preferred_element_type=F32)
    lo = pl.multiple_of(i * tm, tm)
    o_ref[0, pl.ds(lo, tm), :] = (p[:tm] + q[:tm]).astype(o_ref.dtype)
    mirrored = (p[1:tm + 1] - q[1:tm + 1]).astype(BF16)
    hi = pl.multiple_of(s - (i + 1) * tm, tm)
    o_ref[0, pl.ds(hi, tm), :] = jnp.dot(flip_ref[...], mirrored, preferred_element_type=F32).astype(o_ref.dtype)


def fourier_tables(s):
    cg = FOURIER_GROUP_DIM
    jj = jnp.arange(cg, dtype=jnp.int32)
    ang_c = (2.0 * math.pi / cg) * ((jj[:, None] * jj[None, :]) % cg).astype(F32)
    eye_g = jnp.eye(BRANCH_WIDTH // cg, dtype=F32)
    bd_c = jnp.kron(eye_g, jnp.cos(ang_c)) * cg ** -0.5
    bd_s = jnp.kron(eye_g, jnp.sin(ang_c)) * cg ** -0.5
    bd = jnp.concatenate([bd_c, bd_s], axis=1).astype(BF16)
    n_lo = s // cg
    nn = jnp.arange(s, dtype=jnp.int32)
    k1 = jnp.arange(cg // 2 + 2, dtype=jnp.int32)
    ang_hi = (2.0 * math.pi / cg) * ((k1[:, None] * nn[None, :]) % cg).astype(F32)
    ll = jnp.arange(n_lo, dtype=jnp.int32)
    ang_lo = (2.0 * math.pi / s) * ((ll[:, None] * nn[None, :]) % s).astype(F32)
    scale = s ** -0.5
    hi = jnp.stack([jnp.cos(ang_hi), -jnp.sin(ang_hi)]) * scale
    lo = jnp.stack([jnp.cos(ang_lo), jnp.sin(ang_lo)])
    return bd, hi, lo


def fourier_mix(p3, bd, hi, lo, *, tm):
    b, s, _ = p3.shape
    w = BRANCH_WIDTH
    n_lo = lo.shape[1]
    h = FOURIER_HALO
    assert (s // 2) % tm == 0 and tm % n_lo == 0 and h <= n_lo
    r = jnp.arange(tm, dtype=jnp.int32)
    flip = (r[:, None] + r[None, :] == tm - 1).astype(BF16)
    vmem = (s * w * 2 + 2 * w * w * 2 + 2 * (hi.size + lo.size) * 4 + (tm + h) * 2 * s * 2 + 2 * s * w * 2
            + 2 * s * w * 2 + 8 * (tm + h) * w * 4 + 8 * n_lo * s * 4)
    return pl.pallas_call(
        functools.partial(_fourier_kernel, tm=tm, row_chunk=min(s, 512)),
        grid=(b, s // 2 // tm),
        in_specs=[
            pl.BlockSpec((1, s, w), lambda bi, i: (bi, 0, OFF_A // w), pipeline_mode=pl.Buffered(1)),
            pl.BlockSpec((w, 2 * w), lambda bi, i: (0, 0), pipeline_mode=pl.Buffered(1)),
            pl.BlockSpec(hi.shape, lambda bi, i: (0, 0, 0)),
            pl.BlockSpec(lo.shape, lambda bi, i: (0, 0, 0)),
            pl.BlockSpec((tm, tm), lambda bi, i: (0, 0)),
        ],
        out_specs=pl.BlockSpec((1, s, w), lambda bi, i: (bi, 0, 0)),
        out_shape=jax.ShapeDtypeStruct((b, s, w), BF16),
        scratch_shapes=[pltpu.VMEM((2 * s, w), BF16), pltpu.VMEM((tm + h, 2 * s), BF16)],
        compiler_params=_params(("parallel", "arbitrary"), vmem),
        name="fourier",
    )(p3, bd, hi, lo, flip)


def rope_tables(s):
    rows = s // GRID_W
    row_id = jnp.repeat(jnp.arange(rows, dtype=F32), GRID_W)
    col_id = jnp.tile(jnp.arange(GRID_W, dtype=F32), rows)
    n_pairs = HEAD_DIM // 4
    inv_freq = ROPE_BASE ** (-jnp.arange(n_pairs, dtype=F32) / n_pairs)
    ang = jnp.concatenate([row_id[:, None] * inv_freq, col_id[:, None] * inv_freq], axis=-1)
    cos, sin = jnp.cos(ang), jnp.sin(ang)
    return jnp.concatenate([cos, cos] * 2, axis=-1), jnp.concatenate([-sin, sin] * 2, axis=-1)


def _norm_rope(x, g, seg, cos2, sin2):
    half = HEAD_DIM // 2
    x2 = x * x
    hi = x2.astype(BF16)
    lo = (x2 - hi.astype(F32)).astype(BF16)
    ms = jnp.dot(hi, seg, preferred_element_type=F32) + jnp.dot(lo, seg, preferred_element_type=F32)
    y = x * lax.rsqrt(ms + EPS) * g
    lane = lax.broadcasted_iota(jnp.int32, (x.shape[0], V7X_LANES), 1)
    first_half = (lane % HEAD_DIM) < half
    outs = []
    for cb in range(x.shape[1] // V7X_LANES):
        yb = y[:, cb * V7X_LANES:(cb + 1) * V7X_LANES]
        rot = jnp.where(first_half, pltpu.roll(yb, V7X_LANES - half, axis=1), pltpu.roll(yb, half, axis=1))
        outs.append(yb * cos2 + rot * sin2)
    return outs[0] if len(outs) == 1 else jnp.concatenate(outs, axis=-1)


def _gqa_prep_kernel(q_ref, kv_ref, gq_ref, gk_ref, seg_ref, cos_ref, sin_ref, qo_ref, ko_ref, vo_ref):
    d = HEAD_DIM
    nk = GQA_KV_HEADS * d
    cos2, sin2 = cos_ref[...], sin_ref[...]
    q = q_ref[...].astype(F32)
    kv = kv_ref[...].astype(F32)
    qo_ref[...] = _norm_rope(q, gq_ref[...], seg_ref[...], cos2, sin2).astype(BF16)
    ko_ref[...] = _norm_rope(kv[:, :nk], gk_ref[...], seg_ref[:nk, :nk], cos2, sin2).astype(BF16)
    v = kv[:, nk:]
    lane = lax.broadcasted_iota(jnp.int32, v.shape, 1)
    ones_col = jnp.where(lane == d, 1.0, 0.0)
    vo_ref[:, :nk] = jnp.where(lane < d, v, ones_col).astype(BF16)
    vo_ref[:, nk:] = jnp.where(lane < d, pltpu.roll(v, d, axis=1), ones_col).astype(BF16)


def gqa_prep(p2, qk_g, cos2, sin2, *, s, tm):
    m = p2.shape[0]
    nq = GQA_Q_HEADS * HEAD_DIM
    nkv = GQA_KV_HEADS * HEAD_DIM
    assert nkv == V7X_LANES
    tiles_per_seq = s // tm
    q_scale = HEAD_DIM ** -0.5 * LOG2E
    gq = jnp.tile(qk_g[0].astype(F32) * q_scale, GQA_Q_HEADS).reshape(1, nq)
    gk = jnp.tile(qk_g[1].astype(F32), GQA_KV_HEADS).reshape(1, nkv)
    seg = jnp.kron(jnp.eye(GQA_Q_HEADS, dtype=F32), jnp.full((HEAD_DIM, HEAD_DIM), 1.0 / HEAD_DIM, F32)).astype(BF16)
    return pl.pallas_call(
        _gqa_prep_kernel,
        grid=(m // tm,),
        in_specs=[
            pl.BlockSpec((tm, nq), lambda i: (i, OFF_BQ // nq)),
            pl.BlockSpec((tm, 2 * nkv), lambda i: (i, OFF_BKV // (2 * nkv))),
            pl.BlockSpec((1, nq), lambda i: (0, 0)),
            pl.BlockSpec((1, nkv), lambda i: (0, 0)),
            pl.BlockSpec((nq, nq), lambda i: (0, 0)),
            pl.BlockSpec((tm, V7X_LANES), lambda i: (i % tiles_per_seq, 0)),
            pl.BlockSpec((tm, V7X_LANES), lambda i: (i % tiles_per_seq, 0)),
        ],
        out_specs=[
            pl.BlockSpec((tm, nq), lambda i: (i, 0)),
            pl.BlockSpec((tm, nkv), lambda i: (i, 0)),
            pl.BlockSpec((tm, 2 * nkv), lambda i: (i, 0)),
        ],
        out_shape=[
            jax.ShapeDtypeStruct((m, nq), BF16),
            jax.ShapeDtypeStruct((m, nkv), BF16),
            jax.ShapeDtypeStruct((m, 2 * nkv), BF16),
        ],
        compiler_params=_params(("parallel",), 16 * tm * nq * 4),
        name="gqa_prep",
    )(p2, p2, gq, gk, seg, cos2, sin2)


def _gqa_attn_kernel(q_ref, kt_ref, v_ref, o_ref, q_sc, m_ref, acc_ref, *, tk):
    d = HEAD_DIM
    tq = q_ref.shape[1]
    grp = q_ref.shape[2] // d
    s = kt_ref.shape[2]
    for g in range(grp):
        q_sc[g * tq:(g + 1) * tq, :] = q_ref[0, :, g * d:(g + 1) * d]

    def qk(item):
        c, g = item
        return jnp.dot(q_sc[g * tq:(g + 1) * tq, :], kt_ref[0, :, c * tk:(c + 1) * tk],
                       preferred_element_type=F32)

    items = [(c, g) for c in range(s // tk) for g in range(grp)]
    sc_next = qk(items[0])
    for n, (c, g) in enumerate(items):
        rows = slice(g * tq, (g + 1) * tq)
        sc = sc_next
        if n + 1 < len(items):
            sc_next = qk(items[n + 1])
        v = v_ref[0, c * tk:(c + 1) * tk, :]
        row_max = jnp.max(sc, axis=-1, keepdims=True)
        if c == 0:
            m_new = jnp.broadcast_to(row_max, (tq, V7X_LANES))
        else:
            m_prev = m_ref[rows, :]
            m_new = jnp.maximum(m_prev, row_max)
        p = jnp.exp2((sc - pltpu.repeat(m_new, tk // V7X_LANES, axis=1)).astype(BF16))
        pv = jnp.dot(p, v, preferred_element_type=F32)
        if c == 0:
            acc_ref[rows, :] = pv
        else:
            acc_ref[rows, :] = acc_ref[rows, :] * jnp.exp2(m_prev - m_new) + pv
        m_ref[rows, :] = m_new
    for g in range(grp):
        acc = acc_ref[g * tq:(g + 1) * tq, :]
        o_ref[0, :, g * d:(g + 1) * d] = (acc[:, :d] / acc[:, d:d + 1]).astype(o_ref.dtype)


def gqa_attention(q3, kt3, v3, *, tq, tk):
    b, s, nq = q3.shape
    d = HEAD_DIM
    grp = GQA_Q_HEADS // GQA_KV_HEADS
    mrows = grp * tq
    vmem = (2 * (tq * grp * d * 2 + d * s * 2 + s * 128 * 2 + tq * grp * d * 2) + 3 * mrows * 128 * 4
            + 12 * tq * tk * 4)
    return pl.pallas_call(
        functools.partial(_gqa_attn_kernel, tk=tk),
        grid=(b, GQA_KV_HEADS, s // tq),
        in_specs=[
            pl.BlockSpec((1, tq, grp * d), lambda bi, kv, i: (bi, i, kv)),
            pl.BlockSpec((1, d, s), lambda bi, kv, i: (bi, kv, 0)),
            pl.BlockSpec((1, s, 2 * d), lambda bi, kv, i: (bi, 0, kv)),
        ],
        out_specs=pl.BlockSpec((1, tq, grp * d), lambda bi, kv, i: (bi, i, kv)),
        out_shape=jax.ShapeDtypeStruct((b, s, nq), BF16),
        scratch_shapes=[pltpu.VMEM((mrows, d), BF16), pltpu.VMEM((mrows, V7X_LANES), F32),
                        pltpu.VMEM((mrows, 2 * d), F32)],
        compiler_params=_params(("parallel", "parallel", "parallel"), vmem),
        name="gqa_attn",
    )(q3, kt3, v3)


def _mlstm_step(chains, ms, q_ref, k_ref, v_ref, r_sc, cm_sc, b_sc, st_sc, h_sc):
    L = MLSTM_CHUNK
    dh = MLSTM_HEAD_DIM
    assert L == dh
    scale = dh ** -0.5
    row_i = lax.broadcasted_iota(jnp.int32, (L, L), 0)
    col_i = lax.broadcasted_iota(jnp.int32, (L, L), 1)

    def col(x_row):
        return jnp.transpose(jnp.broadcast_to(x_row, (L, L)))

    pre = []
    for (hh, direction, c), m in zip(chains, ms):
        off = pl.multiple_of(c * L, L)
        lanes = slice(hh * dh, (hh + 1) * dh)
        q = q_ref[0, pl.ds(off, L), lanes]
        k = k_ref[0, pl.ds(off, L), lanes]
        v = v_ref[0, pl.ds(off, L), lanes]
        r_row = r_sc[hh, direction, pl.ds(c, 1), :]
        cm_row = cm_sc[hh, direction, pl.ds(c, 1), :]
        b_row = b_sc[hh, direction, pl.ds(c, 1), :]
        rmax = jnp.max(r_row, axis=-1, keepdims=True)
        btot = b_row[:, L - 1:L] if direction == 0 else b_row[:, 0:1]
        cmat = jnp.maximum(m, col(cm_row))
        mask = (row_i >= col_i) if direction == 0 else (row_i <= col_i)
        c_last = jnp.maximum(m, rmax)
        w_state = jnp.exp(col(r_row) - c_last) * scale
        pre.append(dict(
            off=off, lanes=lanes, q=q, k=k, v=v,
            w_intra=jnp.where(mask, jnp.exp(r_row - cmat), 0.0) * scale,
            w_inter=jnp.exp(m - cmat),
            den_floor=jnp.exp(-(col(b_row) + cmat)),
            decay=jnp.exp(m - c_last),
            kv_w=jnp.concatenate([w_state * v.astype(F32), w_state], axis=-1).astype(BF16),
            m_new=btot + c_last,
        ))
    s_raw = [lax.dot_general(p["q"], p["k"], (((1,), (1,)), ((), ())), preferred_element_type=F32) for p in pre]
    states = [st_sc[hh, direction] for hh, direction, _ in chains]
    inter = [jnp.dot(p["q"], st.astype(BF16), preferred_element_type=F32) for p, st in zip(pre, states)]
    upd = [lax.dot_general(p["k"], p["kv_w"], (((0,), (0,)), ((), ())), preferred_element_type=F32) for p in pre]
    for n, ((hh, direction, _), p) in enumerate(zip(chains, pre)):
        st_sc[hh, direction] = p["decay"] * states[n] + upd[n]
        v_aug = jnp.concatenate([p["v"], jnp.ones((L, dh), BF16)], axis=-1)
        intra = jnp.dot((s_raw[n] * p["w_intra"]).astype(BF16), v_aug, preferred_element_type=F32)
        h_aug = jnp.concatenate([p["w_inter"], p["w_inter"]], axis=-1) * inter[n] + intra
        h_sc[hh, direction, pl.ds(p["off"], L), :] = h_aug[:, :dh] / jnp.maximum(jnp.abs(h_aug[:, dh:]), p["den_floor"])
    return [p["m_new"] for p in pre]


def _mlstm_kernel(bias_ref, q_ref, k_ref, v_ref, o_ref, gate_ref, g_ref, y_ref,
                  r_sc, cm_sc, b_sc, h_sc, st_sc):
    L = MLSTM_CHUNK
    dh = MLSTM_HEAD_DIM
    hpb = q_ref.shape[2] // dh
    head0 = pl.program_id(1) * hpb
    nc = q_ref.shape[1] // L
    lane = lax.broadcasted_iota(jnp.int32, (nc, L), 1)
    shifts = [1 << t for t in range(int(math.log2(L)))]
    for hh in range(hpb):
        for d in range(2):
            i_pre = gate_ref[0, 2 * d, hh] + bias_ref[2 * d, head0 + hh]
            f_pre = gate_ref[0, 2 * d + 1, hh] + bias_ref[2 * d + 1, head0 + hh]
            logf = jnp.minimum(f_pre, 0.0) - jnp.log1p(jnp.exp(-jnp.abs(f_pre)))
            bc = logf
            for sh in shifts:
                if d == 0:
                    bc = bc + jnp.where(lane >= sh, pltpu.roll(bc, sh, axis=1), 0.0)
                else:
                    bc = bc + jnp.where(lane < L - sh, pltpu.roll(bc, L - sh, axis=1), 0.0)
            r = i_pre - bc
            cm = r
            for sh in shifts:
                if d == 0:
                    cm = jnp.maximum(cm, jnp.where(lane >= sh, pltpu.roll(cm, sh, axis=1), -jnp.inf))
                else:
                    cm = jnp.maximum(cm, jnp.where(lane < L - sh, pltpu.roll(cm, L - sh, axis=1), -jnp.inf))
            r_sc[hh, d] = r
            cm_sc[hh, d] = cm
            b_sc[hh, d] = bc
    st_sc[...] = jnp.zeros(st_sc.shape, F32)

    def body(c, ms):
        chains = [(hh, d, c if d == 0 else nc - 1 - c) for hh in range(hpb) for d in range(2)]
        return tuple(_mlstm_step(chains, ms, q_ref, k_ref, v_ref, r_sc, cm_sc, b_sc, st_sc, h_sc))

    lax.fori_loop(0, nc, body, tuple(jnp.zeros((1, 1), F32) for _ in range(2 * hpb)))
    for hh in range(hpb):
        lanes = slice(hh * dh, (hh + 1) * dh)
        hsum = h_sc[hh, 0] + h_sc[hh, 1]
        ms = jnp.mean(hsum * hsum, axis=-1, keepdims=True)
        y = hsum * lax.rsqrt(ms + EPS) * g_ref[:, lanes]
        y_ref[0, :, lanes] = (jax.nn.sigmoid(o_ref[0, :, lanes].astype(F32)) * y).astype(y_ref.dtype)


def mlstm_branch(p3, gates5, gate_bias, norm_g, *, heads_per_block):
    b, s, _ = p3.shape
    L = MLSTM_CHUNK
    hpb = heads_per_block
    wb = hpb * MLSTM_HEAD_DIM
    nc = s // L
    blk = lambda off: pl.BlockSpec((1, s, wb), lambda bi, h, off=off: (bi, 0, off // wb + h))
    vmem = 2 * 5 * s * wb * 2 + 2 * s * wb * 4 + 6 * hpb * nc * L * 4 + 4 * hpb * wb * wb * 4 + 3 * s * wb * 4
    return pl.pallas_call(
        _mlstm_kernel,
        grid=(b, MLSTM_HEADS // hpb),
        in_specs=[
            pl.BlockSpec(memory_space=pltpu.SMEM),
            blk(OFF_CQ), blk(OFF_CK), blk(OFF_CV), blk(OFF_CO),
            pl.BlockSpec((1, 4, hpb, nc, L), lambda bi, h: (bi, 0, h, 0, 0)),
            pl.BlockSpec((1, wb), lambda bi, h: (0, h)),
        ],
        out_specs=pl.BlockSpec((1, s, wb), lambda bi, h: (bi, 0, h)),
        out_shape=jax.ShapeDtypeStruct((b, s, BRANCH_WIDTH), BF16),
        scratch_shapes=[
            pltpu.VMEM((hpb, 2, nc, L), F32), pltpu.VMEM((hpb, 2, nc, L), F32), pltpu.VMEM((hpb, 2, nc, L), F32),
            pltpu.VMEM((hpb, 2, s, MLSTM_HEAD_DIM), F32),
            pltpu.VMEM((hpb, 2, MLSTM_HEAD_DIM, 2 * MLSTM_HEAD_DIM), F32),
        ],
        compiler_params=_params(("parallel", "parallel"), vmem),
        name="mlstm",
    )(gate_bias.astype(F32), p3, p3, p3, p3, gates5, norm_g.reshape(1, BRANCH_WIDTH).astype(F32))


def _rel_bucket(rel):
    half = REL_BUCKETS // 2
    max_exact = half // 2
    ret = jnp.where(rel > 0, half, 0)
    n = jnp.abs(rel)
    nf = jnp.maximum(n, 1).astype(F32)
    large = max_exact + (jnp.log(nf / max_exact) / math.log(REL_MAX_DIST / max_exact) * (half - max_exact)).astype(jnp.int32)
    large = jnp.minimum(large, half - 1)
    return ret + jnp.where(n < max_exact, n, large)


def diff_bias_tiles(rel_bias, t):
    assert t >= REL_MAX_DIST
    k = jnp.arange(2 * t, dtype=jnp.int32)
    rel = jnp.arange(-2, 3, dtype=jnp.int32)[:, None] * t + jnp.where(k < t, k, k - 2 * t)[None, :]
    onehot = (_rel_bucket(rel)[:, :, None] == jnp.arange(REL_BUCKETS, dtype=jnp.int32)).astype(F32)
    period = jnp.einsum('dkb,bh->hdk', onehot, rel_bias.astype(F32) * LOG2E, precision=lax.Precision.HIGHEST)
    far = period[:, 0::4, 0]
    return period, far


def _diff_attn_kernel(far_ref, q_ref, k_ref, v_ref, period_ref, lam_ref, g_ref, o_ref,
                      q_sc, kt_sc, vaug_sc, bias_sc, sc_buf, m_ref, acc_ref, *, t, lam_init):
    dq = DIFF_QK_DIM
    dv = DIFF_V_DIM
    s = k_ref.shape[1]
    nt = s // t
    head = pl.program_id(1)
    rb = sc_buf.shape[0]
    grp = sc_buf.shape[1] // t
    wrap = (grp - 1) * t
    assert nt % grp == 0 and t % rb == 0 and nt >= 4

    tr = max(t, 512)
    for r in range(0, s, tr):
        kt_sc[:, r:r + tr] = jnp.transpose(k_ref[0, r:r + tr, :].astype(F32)).astype(BF16)
    kt_sc[:, s:] = kt_sc[:, :wrap]
    vaug_sc[:s, :dv] = v_ref[0]
    vaug_sc[s:, :dv] = v_ref[0, :wrap, :]
    lane = lax.broadcasted_iota(jnp.int32, (s + wrap, dv), 1)
    vaug_sc[:, dv:] = jnp.where(lane == 0, 1.0, 0.0).astype(BF16)
    for dl in range(period_ref.shape[1]):
        full = jnp.broadcast_to(period_ref[0, dl:dl + 1, :], (t, 2 * t))
        bias_sc[dl] = pltpu.roll(full, 0, axis=1, stride=1, stride_axis=0)[:, :t]
    q_scale = dq ** -0.5 * LOG2E
    q_all = (q_ref[0].astype(F32) * q_scale).astype(BF16)
    q_sc[0] = q_all[:, :dq]
    q_sc[1] = q_all[:, dq:]
    lp = lam_ref[...]
    lam = (jnp.exp(jnp.sum(lp[0:1] * lp[1:2], axis=-1, keepdims=True))
           - jnp.exp(jnp.sum(lp[2:3] * lp[3:4], axis=-1, keepdims=True)) + lam_init)

    def key_tile(i, delta):
        j = lax.rem(i + delta, nt)
        return j, pl.multiple_of(j * t, t)

    def qk(i, item):
        pair, mp, hf = item
        row0 = pl.multiple_of(i * t + hf * rb, rb)
        _, off = key_tile(i, grp * pair)
        return jnp.dot(q_sc[mp, pl.ds(row0, rb), :], kt_sc[mp * dq:(mp + 1) * dq, pl.ds(off, grp * t)],
                       preferred_element_type=F32)

    items = [(pair, mp, hf) for pair in range(nt // grp) for mp in range(2) for hf in range(t // rb)]
    sc_buf[...] = qk(0, items[0])

    def tile_body(i, carry):
        sc_next = sc_buf[...]
        for n, (pair, mp, hf) in enumerate(items):
            sc = sc_next
            if n + 1 < len(items):
                sc_next = qk(i, items[n + 1])
            else:
                sc_next = qk(jnp.minimum(i + 1, nt - 1), items[0])
            rows = slice(hf * rb, (hf + 1) * rb)
            scs = [sc[:, p * t:(p + 1) * t] for p in range(grp)]
            _, off = key_tile(i, grp * pair)
            shifts, row_max = [], None
            for piece, delta in enumerate(range(grp * pair, grp * pair + grp)):
                j, _ = key_tile(i, delta)
                if delta in (0, 1, nt - 1):
                    tile = 2 if delta == 0 else jnp.clip(j - i, -2, 2) + 2
                    scs[piece] = scs[piece] + bias_sc[tile, rows, :]
                    shifts.append(None)
                    rm = jnp.max(scs[piece], axis=-1, keepdims=True)
                else:
                    shifts.append(jnp.where(j > i, far_ref[head, 1], far_ref[head, 0]))
                    rm = jnp.max(scs[piece], axis=-1, keepdims=True) + shifts[-1]
                row_max = rm if row_max is None else jnp.maximum(row_max, rm)
            if pair == 0:
                m_new = jnp.broadcast_to(row_max, (rb, V7X_LANES))
            else:
                m_prev = m_ref[mp, rows, :]
                m_new = jnp.maximum(m_prev, row_max)
            ps = []
            for piece in range(grp):
                m_sub = m_new if shifts[piece] is None else m_new - shifts[piece]
                ps.append(jnp.exp2((scs[piece] - pltpu.repeat(m_sub, t // V7X_LANES, axis=1)).astype(BF16)))
            pv = jnp.dot(jnp.concatenate(ps, axis=1), vaug_sc[pl.ds(off, grp * t), :], preferred_element_type=F32)
            if pair == 0:
                acc_ref[mp, rows, :] = pv
            else:
                alpha = jnp.exp2(m_prev - m_new)
                acc_ref[mp, rows, :] = acc_ref[mp, rows, :] * pltpu.repeat(alpha, 2 * dv // V7X_LANES, axis=1) + pv
            m_ref[mp, rows, :] = m_new
        sc_buf[...] = sc_next
        o0 = acc_ref[0, :, :dv] / acc_ref[0, :, dv:dv + 1]
        o1 = acc_ref[1, :, :dv] / acc_ref[1, :, dv:dv + 1]
        o = o0 - lam * o1
        ms = jnp.mean(o * o, axis=-1, keepdims=True)
        row0 = pl.multiple_of(i * t, t)
        o_ref[0, pl.ds(row0, t), :] = (o * lax.rsqrt(ms + EPS) * g_ref[...] * (1.0 - lam_init)).astype(o_ref.dtype)
        return carry

    lax.fori_loop(0, nt, tile_body, 0)


def diff_attention(p3, bias_period, bias_far, lam_params, sub_g, *, t, rows_per_item, tiles_per_item, layer_number):
    b, s, _ = p3.shape
    dv = DIFF_V_DIM
    n_off = bias_period.shape[1]
    assert s // t >= 4, "tiles 2 .. s/t-2 steps away from the query tile must all be beyond REL_MAX_DIST"
    lam_init = 0.8 - 0.6 * math.exp(-0.3 * (layer_number - 1))
    sw = s + (tiles_per_item - 1) * t
    item = rows_per_item * tiles_per_item * t
    vmem = (2 * (s * 128 * 2 + 128 * s * 2 + s * dv * 2 + s * dv * 2) + n_off * t * t * 4 + sw * (2 * dv + 128) * 2
            + 2 * s * 128 * 2 + item * 4 + 2 * t * 128 * 4 * 3 + 8 * item * 4)
    return pl.pallas_call(
        functools.partial(_diff_attn_kernel, t=t, lam_init=lam_init),
        grid=(b, DIFF_HEADS),
        in_specs=[
            pl.BlockSpec(memory_space=pltpu.SMEM),
            pl.BlockSpec((1, s, 128), lambda bi, h: (bi, 0, OFF_DQ // 128 + h)),
            pl.BlockSpec((1, s, 128), lambda bi, h: (bi, 0, OFF_DK // 128 + h)),
            pl.BlockSpec((1, s, dv), lambda bi, h: (bi, 0, OFF_DV // dv + h)),
            pl.BlockSpec((1, n_off, 2 * t), lambda bi, h: (h, 0, 0)),
            pl.BlockSpec((4, DIFF_QK_DIM), lambda bi, h: (0, 0)),
            pl.BlockSpec((1, dv), lambda bi, h: (0, 0)),
        ],
        out_specs=pl.BlockSpec((1, s, dv), lambda bi, h: (bi, 0, h)),
        out_shape=jax.ShapeDtypeStruct((b, s, BRANCH_WIDTH), BF16),
        scratch_shapes=[pltpu.VMEM((2, s, DIFF_QK_DIM), BF16), pltpu.VMEM((2 * DIFF_QK_DIM, sw), BF16),
                        pltpu.VMEM((sw, 2 * dv), BF16),
                        pltpu.VMEM((n_off, t, t), F32), pltpu.VMEM((rows_per_item, tiles_per_item * t), F32),
                        pltpu.VMEM((2, t, V7X_LANES), F32), pltpu.VMEM((2, t, 2 * dv), F32)],
        compiler_params=_params(("parallel", "parallel"), vmem),
        name="diff_attn",
    )(bias_far, p3, p3, p3, bias_period, lam_params.astype(F32), sub_g.reshape(1, dv).astype(F32))


def _merge_kernel(ya_ref, yb_ref, yc_ref, yd_ref, g_ref, wg_ref, wb_ref, wo_ref, x_ref, o_ref, wb_sc, wo_sc):
    dm = x_ref.shape[1]

    @pl.when(pl.program_id(0) == 0)
    def _():
        for n in range(wb_ref.shape[0]):
            wb_sc[n] = wb_ref[n].astype(BF16)
        wo_sc[...] = wo_ref[...].astype(BF16)

    tm = x_ref.shape[0]
    halves = [slice(0, tm // 2), slice(tm // 2, tm)]
    y_refs = (ya_ref, yb_ref, yc_ref, yd_ref)
    gates, branch = [], []
    for rows in halves:
        x = x_ref[rows, :]
        ms = jnp.mean(x * x, axis=-1, keepdims=True)
        u = (x * lax.rsqrt(ms + EPS) * g_ref[...]).astype(BF16)
        gates.append([jnp.dot(u, wg_ref[:, n * dm:(n + 1) * dm], preferred_element_type=F32)
                      for n in range(len(y_refs))])
        branch.append([jnp.dot(y_ref[rows, :], wb_sc[n], preferred_element_type=F32)
                       for n, y_ref in enumerate(y_refs)])
    for rows, gts, brs in zip(halves, gates, branch):
        merged = None
        for gt, br in zip(gts, brs):
            term = jax.nn.sigmoid(gt) * br
            merged = term if merged is None else merged + term
        o_ref[rows, :] = x_ref[rows, :] + jnp.dot(merged.astype(BF16), wo_sc[...], preferred_element_type=F32)


def merge_branches(ys, norm_g, w_gates, w_branch, w_out, x2, *, tm):
    m, dm = x2.shape
    w = BRANCH_WIDTH
    y_spec = pl.BlockSpec((tm, w), lambda i: (i, 0))
    vmem = (2 * (4 * tm * w * 2 + 2 * tm * dm * 4) + N_BRANCHES * dm * dm * 2 + (4 * w * dm + dm * dm) * (4 + 2)
            + 12 * tm * dm * 4)
    return pl.pallas_call(
        _merge_kernel,
        grid=(m // tm,),
        in_specs=[
            y_spec, y_spec, y_spec, y_spec,
            pl.BlockSpec((1, dm), lambda i: (0, 0)),
            pl.BlockSpec((dm, N_BRANCHES * dm), lambda i: (0, 0), pipeline_mode=pl.Buffered(1)),
            pl.BlockSpec((N_BRANCHES, w, dm), lambda i: (0, 0, 0), pipeline_mode=pl.Buffered(1)),
            pl.BlockSpec((dm, dm), lambda i: (0, 0), pipeline_mode=pl.Buffered(1)),
            pl.BlockSpec((tm, dm), lambda i: (i, 0)),
        ],
        out_specs=pl.BlockSpec((tm, dm), lambda i: (i, 0)),
        out_shape=jax.ShapeDtypeStruct((m, dm), F32),
        scratch_shapes=[pltpu.VMEM((N_BRANCHES, w, dm), BF16), pltpu.VMEM((dm, dm), BF16)],
        compiler_params=_params(("arbitrary",), vmem),
        name="merge",
    )(*ys, norm_g.reshape(1, dm).astype(F32), w_gates, w_branch.astype(F32), w_out.astype(F32), x2)


HALO_ROWS = 16


def _conv_ffn_kernel(x_ref, xp_ref, xn_ref, g_ref, wa_ref, wl_ref, cw_ref, cb_ref, wd_ref, gf_ref, o_ref,
                     xe_sc, wd_sc, *, tiles_per_seq, col_chunk, final_norm):
    tm = o_ref.shape[0]
    f = wa_ref.shape[1]
    h = HALO_ROWS
    pos = pl.program_id(0) % tiles_per_seq

    @pl.when(pl.program_id(0) == 0)
    def _():
        wd_sc[...] = wd_ref[...].astype(BF16)

    def norm(x):
        ms = jnp.mean(x * x, axis=-1, keepdims=True)
        return x * lax.rsqrt(ms + EPS) * g_ref[...]

    xe_sc[:h] = jnp.where(pos == 0, 0.0, norm(xp_ref[...])).astype(BF16)
    xe_sc[h:h + tm] = norm(x_ref[...]).astype(BF16)
    xe_sc[h + tm:] = jnp.where(pos == tiles_per_seq - 1, 0.0, norm(xn_ref[...])).astype(BF16)
    k0 = math.sqrt(2.0 / math.pi)
    bounds = list(range(0, f, col_chunk)) + [f]
    chunks = [slice(lo, hi) for lo, hi in zip(bounds[:-1], bounds[1:])]

    def project(cols):
        a_ext = jnp.dot(xe_sc[...], wa_ref[:, cols], preferred_element_type=F32)
        lin = jnp.dot(xe_sc[h:h + tm, :], wl_ref[:, cols], preferred_element_type=F32)
        return a_ext, lin

    y = x_ref[...]
    nxt = project(chunks[0])
    for n, cols in enumerate(chunks):
        a_ext, lin = nxt
        if n + 1 < len(chunks):
            nxt = project(chunks[n + 1])
        cw = cw_ref[:, cols]
        c = (a_ext[h - 1:h - 1 + tm] * cw[0:1] + a_ext[h:h + tm] * cw[1:2] + a_ext[h + 1:h + 1 + tm] * cw[2:3]
             + cb_ref[:, cols])
        t = jnp.tanh(c * ((c * c) * (k0 * 0.044715) + k0))
        hmid = ((c + c * t) * lin).astype(BF16)
        y = y + jnp.dot(hmid, wd_sc[cols, :], preferred_element_type=F32)
    if final_norm:
        ms = jnp.mean(y * y, axis=-1, keepdims=True)
        y = y * lax.rsqrt(ms + EPS) * gf_ref[...]
    o_ref[...] = y


def conv_ffn(x2, g, w_up, conv_w, conv_b, w_down, final_g, *, s, tm, final_norm):
    m, dm = x2.shape
    f = w_up.shape[1] // 2
    tiles_per_seq = s // tm
    hb = tm // HALO_ROWS
    n_halo = m // HALO_ROWS
    vmem = (2 * 2 * tm * dm * 4 + 2 * dm * f * 2 + f * dm * (4 + 2) + (tm + 2 * HALO_ROWS) * dm * 2
            + 12 * tm * 512 * 4 + 2 * tm * dm * 4)
    return pl.pallas_call(
        functools.partial(_conv_ffn_kernel, tiles_per_seq=tiles_per_seq, col_chunk=1024, final_norm=final_norm),
        grid=(m // tm,),
        in_specs=[
            pl.BlockSpec((tm, dm), lambda i: (i, 0)),
            pl.BlockSpec((HALO_ROWS, dm), lambda i: (jnp.maximum(i * hb - 1, 0), 0)),
            pl.BlockSpec((HALO_ROWS, dm), lambda i: (jnp.minimum((i + 1) * hb, n_halo - 1), 0)),
            pl.BlockSpec((1, dm), lambda i: (0, 0)),
            pl.BlockSpec((dm, f), lambda i: (0, 0), pipeline_mode=pl.Buffered(1)),
            pl.BlockSpec((dm, f), lambda i: (0, 1), pipeline_mode=pl.Buffered(1)),
            pl.BlockSpec((CONV_W, f), lambda i: (0, 0)),
            pl.BlockSpec((1, f), lambda i: (0, 0)),
            pl.BlockSpec((f, dm), lambda i: (0, 0), pipeline_mode=pl.Buffered(1)),
            pl.BlockSpec((1, dm), lambda i: (0, 0)),
        ],
        out_specs=pl.BlockSpec((tm, dm), lambda i: (i, 0)),
        out_shape=jax.ShapeDtypeStruct((m, dm), F32),
        scratch_shapes=[pltpu.VMEM((tm + 2 * HALO_ROWS, dm), BF16), pltpu.VMEM((f, dm), BF16)],
        compiler_params=_params(("arbitrary",), vmem),
        name="conv_ffn",
    )(x2, x2, x2, g.reshape(1, dm).astype(F32), w_up, w_up, conv_w.astype(F32), conv_b.reshape(1, f).astype(F32),
      w_down.astype(F32), final_g.reshape(1, dm).astype(F32))


def _arrange_w_in(w):
    widths = (512, 512, 128, 128, 512, 512, 512, 512, 16, 512, 512, 512, 4096)
    offs = [0]
    for wd in widths:
        offs.append(offs[-1] + wd)
    seg = lambda n: w[:, offs[n]:offs[n + 1]]
    a, bq, bk, bv, cq, ck, cv, co, cgate, dq, dk, dv, gates = (seg(n) for n in range(len(widths)))
    main = jnp.concatenate([a, bq, cq, ck, cv, co, dq, dk, dv, bk, bv], axis=1).astype(BF16)
    gate = jnp.pad(cgate, ((0, 0), (0, V7X_LANES - cgate.shape[1]))).astype(BF16)
    return main, gate, gates.astype(BF16)


def kernel(x, norm_mix_g, w_in, mlstm_gate_bias, qk_norm_g, mlstm_norm_g, diff_lambda, diff_norm_g, rel_bias,
           w_branch, w_out, norm_ffn_g, w_up, conv_w, conv_b, w_down, final_norm_g):
    b, s, dm = x.shape
    depth = w_in.shape[0]
    m = b * s
    d_ff = w_down.shape[1]
    L = MLSTM_CHUNK
    t_diff = 512

    bd, dft_hi, dft_lo = fourier_tables(s)
    cos2, sin2 = rope_tables(s)
    bias_tiles, bias_far = diff_bias_tiles(rel_bias, t_diff)

    x2 = x.reshape(m, dm)
    for layer in range(depth):
        w_main, w_gate, w_branch_gates = _arrange_w_in(w_in[layer])
        p2, cgate = norm_matmul(x2, norm_mix_g[layer], w_main, w_gate, tm=512, tn=P_WIDTH)
        p3 = p2.reshape(b, s, P_WIDTH)

        y_a = fourier_mix(p3, bd, dft_hi, dft_lo, tm=512)

        q_b, k_b, v_b = gqa_prep(p2, qk_norm_g[layer], cos2, sin2, s=s, tm=512)
        kt_b = jnp.swapaxes(k_b.reshape(b, s, GQA_KV_HEADS * HEAD_DIM), 1, 2)
        y_b = gqa_attention(q_b.reshape(b, s, -1), kt_b, v_b.reshape(b, s, -1), tq=512, tk=1024)

        gates5 = jnp.transpose(cgate[:, :4 * MLSTM_HEADS].reshape(b, s, 4, MLSTM_HEADS), (0, 2, 3, 1))
        gates5 = gates5.reshape(b, 4, MLSTM_HEADS, s // L, L)
        y_c = mlstm_branch(p3, gates5, mlstm_gate_bias[layer], mlstm_norm_g[layer], heads_per_block=2)

        y_d = diff_attention(p3, bias_tiles, bias_far, diff_lambda[layer], diff_norm_g[layer],
                             t=t_diff, rows_per_item=256, tiles_per_item=4, layer_number=layer + 1)

        ys = [y.reshape(m, BRANCH_WIDTH) for y in (y_a, y_b, y_c, y_d)]
        x2 = merge_branches(ys, norm_mix_g[layer], w_branch_gates, w_branch[layer], w_out[layer], x2, tm=512)

        half_lin = jnp.concatenate([jnp.ones((d_ff,), F32), jnp.full((d_ff,), 0.5, F32)])
        x2 = conv_ffn(x2, norm_ffn_g[layer], (w_up[layer] * half_lin).astype(BF16), conv_w[layer], conv_b[layer],
                      w_down[layer], final_norm_g, s=s, tm=512, final_norm=(layer == depth - 1))
    return x2.reshape(b, s, dm)
```

```python
import functools
import math

import jax
import jax.numpy as jnp
from jax import lax
from jax.experimental import pallas as pl
from jax.experimental.pallas import tpu as pltpu

F32 = jnp.float32
BF16 = jnp.bfloat16

GRID_W = 64
HEAD_DIM = 64
BRANCH_WIDTH = 512
N_BRANCHES = 4
FOURIER_GROUP_DIM = 64
GQA_Q_HEADS = 8
GQA_KV_HEADS = 2
MLSTM_HEADS = 4
MLSTM_HEAD_DIM = 128
MLSTM_CHUNK = 128
DIFF_HEADS = 4
DIFF_QK_DIM = 64
DIFF_V_DIM = 128
REL_BUCKETS = 32
REL_MAX_DIST = 128
CONV_W = 3
ROPE_BASE = 10000.0
EPS = 1e-6
LOG2E = math.log2(math.e)

V7X_LANES = 128
V7X_VMEM_BYTES = 64 * 1024 * 1024
V7X_VMEM_CAP = V7X_VMEM_BYTES - 8 * 1024 * 1024

OFF_A = 0
OFF_BQ = 512
OFF_CQ = 1024
OFF_CK = 1536
OFF_CV = 2048
OFF_CO = 2560
OFF_DQ = 3072
OFF_DK = 3584
OFF_DV = 4096
OFF_BKV = 4608
P_WIDTH = 4864


def _params(sem, vmem_bytes):
    limit = int(min(max(vmem_bytes * 3 // 2 + (4 << 20), 16 << 20), V7X_VMEM_CAP))
    return pltpu.CompilerParams(dimension_semantics=sem, vmem_limit_bytes=limit)


def _norm_mm_kernel(x_ref, g_ref, w_ref, o_ref, xn_ref):
    @pl.when(pl.program_id(1) == 0)
    def _():
        x = x_ref[...]
        ms = jnp.mean(x * x, axis=-1, keepdims=True)
        xn_ref[...] = (x * lax.rsqrt(ms + EPS) * g_ref[...]).astype(BF16)

    o_ref[...] = jnp.dot(xn_ref[...], w_ref[...], preferred_element_type=F32).astype(o_ref.dtype)


def _norm_mm_gate_kernel(x_ref, g_ref, w_ref, wg_ref, o_ref, og_ref, xn_ref):
    @pl.when(pl.program_id(1) == 0)
    def _():
        x = x_ref[...]
        ms = jnp.mean(x * x, axis=-1, keepdims=True)
        xn = (x * lax.rsqrt(ms + EPS) * g_ref[...]).astype(BF16)
        xn_ref[...] = xn
        og_ref[...] = jnp.dot(xn, wg_ref[...], preferred_element_type=F32)

    o_ref[...] = jnp.dot(xn_ref[...], w_ref[...], preferred_element_type=F32).astype(o_ref.dtype)


def norm_matmul(x, g, w, w_gate=None, *, tm, tn):
    m, k = x.shape
    n = w.shape[1]
    grid = (m // tm, n // tn)
    vmem = 2 * tm * k * 4 + tm * k * 2 + 2 * k * tn * 2 + 2 * tm * tn * 2 + 4 * tm * k
    x_spec = pl.BlockSpec((tm, k), lambda i, j: (i, 0))
    g_spec = pl.BlockSpec((1, k), lambda i, j: (0, 0))
    w_spec = pl.BlockSpec((k, tn), lambda i, j: (0, j))
    o_spec = pl.BlockSpec((tm, tn), lambda i, j: (i, j))
    scratch = [pltpu.VMEM((tm, k), BF16)]
    g2 = g.reshape(1, k).astype(F32)
    if w_gate is None:
        return pl.pallas_call(
            _norm_mm_kernel,
            grid=grid,
            in_specs=[x_spec, g_spec, w_spec],
            out_specs=o_spec,
            out_shape=jax.ShapeDtypeStruct((m, n), BF16),
            scratch_shapes=scratch,
            compiler_params=_params(("parallel", "arbitrary"), vmem),
            name="norm_matmul",
        )(x, g2, w)
    ng = w_gate.shape[1]
    return pl.pallas_call(
        _norm_mm_gate_kernel,
        grid=grid,
        in_specs=[x_spec, g_spec, w_spec, pl.BlockSpec((k, ng), lambda i, j: (0, 0))],
        out_specs=[o_spec, pl.BlockSpec((tm, ng), lambda i, j: (i, 0))],
        out_shape=[jax.ShapeDtypeStruct((m, n), BF16), jax.ShapeDtypeStruct((m, ng), F32)],
        scratch_shapes=scratch,
        compiler_params=_params(("parallel", "arbitrary"), vmem),
        name="norm_matmul_gate",
    )(x, g2, w, w_gate)


FOURIER_HALO = 16


def _fourier_kernel(a_ref, bd_ref, hi_ref, lo_ref, flip_ref, o_ref, z_ref, ext_ref, *, tm, row_chunk):
    s = a_ref.shape[1]
    w = a_ref.shape[2]
    n_lo = lo_ref.shape[1]
    i = pl.program_id(1)

    @pl.when(i == 0)
    def _():
        for r in range(0, s, row_chunk):
            a = a_ref[0, r:r + row_chunk, :]
            zc = jnp.dot(a, bd_ref[...], preferred_element_type=F32)
            z_ref[r:r + row_chunk, :] = zc[:, :w].astype(BF16)
            z_ref[s + r:s + r + row_chunk, :] = zc[:, w:].astype(BF16)

    cl, sl = lo_ref[0], lo_ref[1]
    for r in range(tm // n_lo + 1):
        rows = n_lo if r < tm // n_lo else ext_ref.shape[0] - tm
        k1 = i * (tm // n_lo) + r
        ch = hi_ref[0, pl.ds(k1, 1), :]
        nsh = hi_ref[1, pl.ds(k1, 1), :]
        ext_ref[r * n_lo:r * n_lo + rows, :s] = (cl[:rows] * ch + sl[:rows] * nsh).astype(BF16)
        ext_ref[r * n_lo:r * n_lo + rows, s:] = (cl[:rows] * nsh - sl[:rows] * ch).astype(BF16)
    p = jnp.dot(ext_ref[:, :s], z_ref[:s], preferred_element_type=F32)
    q = jnp.dot(ext_ref[:, s:], z_ref[s:], preferred_element_type=F32)
    lo = pl.multiple_of(i * tm, tm)
    o_ref[0, pl.ds(lo, tm), :] = (p[:tm] + q[:tm]).astype(o_ref.dtype)
    mirrored = (p[1:tm + 1] - q[1:tm + 1]).astype(BF16)
    hi = pl.multiple_of(s - (i + 1) * tm, tm)
    o_ref[0, pl.ds(hi, tm), :] = jnp.dot(flip_ref[...], mirrored, preferred_element_type=F32).astype(o_ref.dtype)


def fourier_tables(s):
    cg = FOURIER_GROUP_DIM
    jj = jnp.arange(cg, dtype=jnp.int32)
    ang_c = (2.0 * math.pi / cg) * ((jj[:, None] * jj[None, :]) % cg).astype(F32)
    eye_g = jnp.eye(BRANCH_WIDTH // cg, dtype=F32)
    bd_c = jnp.kron(eye_g, jnp.cos(ang_c)) * cg ** -0.5
    bd_s = jnp.kron(eye_g, jnp.sin(ang_c)) * cg ** -0.5
    bd = jnp.concatenate([bd_c, bd_s], axis=1).astype(BF16)
    n_lo = s // cg
    nn = jnp.arange(s, dtype=jnp.int32)
    k1 = jnp.arange(cg // 2 + 2, dtype=jnp.int32)
    ang_hi = (2.0 * math.pi / cg) * ((k1[:, None] * nn[None, :]) % cg).astype(F32)
    ll = jnp.arange(n_lo, dtype=jnp.int32)
    ang_lo = (2.0 * math.pi / s) * ((ll[:, None] * nn[None, :]) % s).astype(F32)
    scale = s ** -0.5
    hi = jnp.stack([jnp.cos(ang_hi), -jnp.sin(ang_hi)]) * scale
    lo = jnp.stack([jnp.cos(ang_lo), jnp.sin(ang_lo)])
    return bd, hi, lo


def fourier_mix(p3, bd, hi, lo, *, tm):
    b, s, _ = p3.shape
    w = BRANCH_WIDTH
    n_lo = lo.shape[1]
    h = FOURIER_HALO
    assert (s // 2) % tm == 0 and tm % n_lo == 0 and h <= n_lo
    r = jnp.arange(tm, dtype=jnp.int32)
    flip = (r[:, None] + r[None, :] == tm - 1).astype(BF16)
    vmem = (s * w * 2 + 2 * w * w * 2 + 2 * (hi.size + lo.size) * 4 + (tm + h) * 2 * s * 2 + 2 * s * w * 2
            + 2 * s * w * 2 + 8 * (tm + h) * w * 4 + 8 * n_lo * s * 4)
    return pl.pallas_call(
        functools.partial(_fourier_kernel, tm=tm, row_chunk=min(s, 512)),
        grid=(b, s // 2 // tm),
        in_specs=[
            pl.BlockSpec((1, s, w), lambda bi, i: (bi, 0, OFF_A // w), pipeline_mode=pl.Buffered(1)),
            pl.BlockSpec((w, 2 * w), lambda bi, i: (0, 0), pipeline_mode=pl.Buffered(1)),
            pl.BlockSpec(hi.shape, lambda bi, i: (0, 0, 0)),
            pl.BlockSpec(lo.shape, lambda bi, i: (0, 0, 0)),
            pl.BlockSpec((tm, tm), lambda bi, i: (0, 0)),
        ],
        out_specs=pl.BlockSpec((1, s, w), lambda bi, i: (bi, 0, 0)),
        out_shape=jax.ShapeDtypeStruct((b, s, w), BF16),
        scratch_shapes=[pltpu.VMEM((2 * s, w), BF16), pltpu.VMEM((tm + h, 2 * s), BF16)],
        compiler_params=_params(("parallel", "arbitrary"), vmem),
        name="fourier",
    )(p3, bd, hi, lo, flip)


def rope_tables(s):
    rows = s // GRID_W
    row_id = jnp.repeat(jnp.arange(rows, dtype=F32), GRID_W)
    col_id = jnp.tile(jnp.arange(GRID_W, dtype=F32), rows)
    n_pairs = HEAD_DIM // 4
    inv_freq = ROPE_BASE ** (-jnp.arange(n_pairs, dtype=F32) / n_pairs)
    ang = jnp.concatenate([row_id[:, None] * inv_freq, col_id[:, None] * inv_freq], axis=-1)
    cos, sin = jnp.cos(ang), jnp.sin(ang)
    return jnp.concatenate([cos, cos] * 2, axis=-1), jnp.concatenate([-sin, sin] * 2, axis=-1)


def _norm_rope(x, g, seg, cos2, sin2):
    half = HEAD_DIM // 2
    x2 = x * x
    hi = x2.astype(BF16)
    lo = (x2 - hi.astype(F32)).astype(BF16)
    ms = jnp.dot(hi, seg, preferred_element_type=F32) + jnp.dot(lo, seg, preferred_element_type=F32)
    y = x * lax.rsqrt(ms + EPS) * g
    lane = lax.broadcasted_iota(jnp.int32, (x.shape[0], V7X_LANES), 1)
    first_half = (lane % HEAD_DIM) < half
    outs = []
    for cb in range(x.shape[1] // V7X_LANES):
        yb = y[:, cb * V7X_LANES:(cb + 1) * V7X_LANES]
        rot = jnp.where(first_half, pltpu.roll(yb, V7X_LANES - half, axis=1), pltpu.roll(yb, half, axis=1))
        outs.append(yb * cos2 + rot * sin2)
    return outs[0] if len(outs) == 1 else jnp.concatenate(outs, axis=-1)


def _gqa_prep_kernel(q_ref, kv_ref, gq_ref, gk_ref, seg_ref, cos_ref, sin_ref, qo_ref, ko_ref, vo_ref):
    d = HEAD_DIM
    nk = GQA_KV_HEADS * d
    cos2, sin2 = cos_ref[...], sin_ref[...]
    q = q_ref[...].astype(F32)
    kv = kv_ref[...].astype(F32)
    qo_ref[...] = _norm_rope(q, gq_ref[...], seg_ref[...], cos2, sin2).astype(BF16)
    ko_ref[...] = _norm_rope(kv[:, :nk], gk_ref[...], seg_ref[:nk, :nk], cos2, sin2).astype(BF16)
    v = kv[:, nk:]
    lane = lax.broadcasted_iota(jnp.int32, v.shape, 1)
    ones_col = jnp.where(lane == d, 1.0, 0.0)
    vo_ref[:, :nk] = jnp.where(lane < d, v, ones_col).astype(BF16)
    vo_ref[:, nk:] = jnp.where(lane < d, pltpu.roll(v, d, axis=1), ones_col).astype(BF16)


def gqa_prep(p2, qk_g, cos2, sin2, *, s, tm):
    m = p2.shape[0]
    nq = GQA_Q_HEADS * HEAD_DIM
    nkv = GQA_KV_HEADS * HEAD_DIM
    assert nkv == V7X_LANES
    tiles_per_seq = s // tm
    q_scale = HEAD_DIM ** -0.5 * LOG2E
    gq = jnp.tile(qk_g[0].astype(F32) * q_scale, GQA_Q_HEADS).reshape(1, nq)
    gk = jnp.tile(qk_g[1].astype(F32), GQA_KV_HEADS).reshape(1, nkv)
    seg = jnp.kron(jnp.eye(GQA_Q_HEADS, dtype=F32), jnp.full((HEAD_DIM, HEAD_DIM), 1.0 / HEAD_DIM, F32)).astype(BF16)
    return pl.pallas_call(
        _gqa_prep_kernel,
        grid=(m // tm,),
        in_specs=[
            pl.BlockSpec((tm, nq), lambda i: (i, OFF_BQ // nq)),
            pl.BlockSpec((tm, 2 * nkv), lambda i: (i, OFF_BKV // (2 * nkv))),
            pl.BlockSpec((1, nq), lambda i: (0, 0)),
            pl.BlockSpec((1, nkv), lambda i: (0, 0)),
            pl.BlockSpec((nq, nq), lambda i: (0, 0)),
            pl.BlockSpec((tm, V7X_LANES), lambda i: (i % tiles_per_seq, 0)),
            pl.BlockSpec((tm, V7X_LANES), lambda i: (i % tiles_per_seq, 0)),
        ],
        out_specs=[
            pl.BlockSpec((tm, nq), lambda i: (i, 0)),
            pl.BlockSpec((tm, nkv), lambda i: (i, 0)),
            pl.BlockSpec((tm, 2 * nkv), lambda i: (i, 0)),
        ],
        out_shape=[
            jax.ShapeDtypeStruct((m, nq), BF16),
            jax.ShapeDtypeStruct((m, nkv), BF16),
            jax.ShapeDtypeStruct((m, 2 * nkv), BF16),
        ],
        compiler_params=_params(("parallel",), 16 * tm * nq * 4),
        name="gqa_prep",
    )(p2, p2, gq, gk, seg, cos2, sin2)


def _gqa_attn_kernel(q_ref, kt_ref, v_ref, o_ref, q_sc, m_ref, acc_ref, *, tk):
    d = HEAD_DIM
    tq = q_ref.shape[1]
    grp = q_ref.shape[2] // d
    s = kt_ref.shape[2]
    for g in range(grp):
        q_sc[g * tq:(g + 1) * tq, :] = q_ref[0, :, g * d:(g + 1) * d]

    def qk(item):
        c, g = item
        return jnp.dot(q_sc[g * tq:(g + 1) * tq, :], kt_ref[0, :, c * tk:(c + 1) * tk],
                       preferred_element_type=F32)

    items = [(c, g) for c in range(s // tk) for g in range(grp)]
    sc_next = qk(items[0])
    for n, (c, g) in enumerate(items):
        rows = slice(g * tq, (g + 1) * tq)
        sc = sc_next
        if n + 1 < len(items):
            sc_next = qk(items[n + 1])
        v = v_ref[0, c * tk:(c + 1) * tk, :]
        row_max = jnp.max(sc, axis=-1, keepdims=True)
        if c == 0:
            m_new = jnp.broadcast_to(row_max, (tq, V7X_LANES))
        else:
            m_prev = m_ref[rows, :]
            m_new = jnp.maximum(m_prev, row_max)
        p = jnp.exp2((sc - pltpu.repeat(m_new, tk // V7X_LANES, axis=1)).astype(BF16))
        pv = jnp.dot(p, v, preferred_element_type=F32)
        if c == 0:
            acc_ref[rows, :] = pv
        else:
            acc_ref[rows, :] = acc_ref[rows, :] * jnp.exp2(m_prev - m_new) + pv
        m_ref[rows, :] = m_new
    for g in range(grp):
        acc = acc_ref[g * tq:(g + 1) * tq, :]
        o_ref[0, :, g * d:(g + 1) * d] = (acc[:, :d] / acc[:, d:d + 1]).astype(o_ref.dtype)


def gqa_attention(q3, kt3, v3, *, tq, tk):
    b, s, nq = q3.shape
    d = HEAD_DIM
    grp = GQA_Q_HEADS // GQA_KV_HEADS
    mrows = grp * tq
    vmem = (2 * (tq * grp * d * 2 + d * s * 2 + s * 128 * 2 + tq * grp * d * 2) + 3 * mrows * 128 * 4
            + 12 * tq * tk * 4)
    return pl.pallas_call(
        functools.partial(_gqa_attn_kernel, tk=tk),
        grid=(b, GQA_KV_HEADS, s // tq),
        in_specs=[
            pl.BlockSpec((1, tq, grp * d), lambda bi, kv, i: (bi, i, kv)),
            pl.BlockSpec((1, d, s), lambda bi, kv, i: (bi, kv, 0)),
            pl.BlockSpec((1, s, 2 * d), lambda bi, kv, i: (bi, 0, kv)),
        ],
        out_specs=pl.BlockSpec((1, tq, grp * d), lambda bi, kv, i: (bi, i, kv)),
        out_shape=jax.ShapeDtypeStruct((b, s, nq), BF16),
        scratch_shapes=[pltpu.VMEM((mrows, d), BF16), pltpu.VMEM((mrows, V7X_LANES), F32),
                        pltpu.VMEM((mrows, 2 * d), F32)],
        compiler_params=_params(("parallel", "parallel", "parallel"), vmem),
        name="gqa_attn",
    )(q3, kt3, v3)


def _mlstm_step(chains, ms, q_ref, kt_sc, v_ref, r_sc, cm_sc, b_sc, st_sc, h_sc):
    L = MLSTM_CHUNK
    dh = MLSTM_HEAD_DIM
    assert L == dh
    scale = dh ** -0.5
    row_i = lax.broadcasted_iota(jnp.int32, (L, L), 0)
    col_i = lax.broadcasted_iota(jnp.int32, (L, L), 1)

    def col(x_row):
        return jnp.transpose(jnp.broadcast_to(x_row, (L, L)))

    pre = []
    for (hh, direction, c), m in zip(chains, ms):
        off = pl.multiple_of(c * L, L)
        lanes = slice(hh * dh, (hh + 1) * dh)
        q = q_ref[0, pl.ds(off, L), lanes]
        kt = kt_sc[hh, :, pl.ds(off, L)]
        v = v_ref[0, pl.ds(off, L), lanes]
        r_row = r_sc[hh, direction, pl.ds(c, 1), :]
        cm_row = cm_sc[hh, direction, pl.ds(c, 1), :]
        b_row = b_sc[hh, direction, pl.ds(c, 1), :]
        rmax = jnp.max(r_row, axis=-1, keepdims=True)
        btot = b_row[:, L - 1:L] if direction == 0 else b_row[:, 0:1]
        cmat = jnp.maximum(m, col(cm_row))
        mask = (row_i >= col_i) if direction == 0 else (row_i <= col_i)
        c_last = jnp.maximum(m, rmax)
        w_state = jnp.exp(r_row - c_last) * scale
        pre.append(dict(
            off=off, lanes=lanes, q=q, kt=kt,
            v_aug=jnp.concatenate([v, jnp.ones((L, dh), BF16)], axis=-1),
            w_intra=jnp.where(mask, jnp.exp(r_row - cmat), 0.0) * scale,
            w_inter=jnp.exp(m - cmat),
            den_floor=jnp.exp(-(col(b_row) + cmat)),
            decay=jnp.exp(m - c_last),
            kt_w=(kt.astype(F32) * w_state).astype(BF16),
            m_new=btot + c_last,
        ))
    s_raw = [jnp.dot(p["q"], p["kt"], preferred_element_type=F32) for p in pre]
    states = [st_sc[hh, direction] for hh, direction, _ in chains]
    inter = [jnp.dot(p["q"], st.astype(BF16), preferred_element_type=F32) for p, st in zip(pre, states)]
    upd = [jnp.dot(p["kt_w"], p["v_aug"], preferred_element_type=F32) for p in pre]
    for n, ((hh, direction, _), p) in enumerate(zip(chains, pre)):
        st_sc[hh, direction] = p["decay"] * states[n] + upd[n]
        intra = jnp.dot((s_raw[n] * p["w_intra"]).astype(BF16), p["v_aug"], preferred_element_type=F32)
        h_aug = jnp.concatenate([p["w_inter"], p["w_inter"]], axis=-1) * inter[n] + intra
        h_sc[hh, direction, pl.ds(p["off"], L), :] = h_aug[:, :dh] / jnp.maximum(jnp.abs(h_aug[:, dh:]), p["den_floor"])
    return [p["m_new"] for p in pre]


def _mlstm_kernel(bias_ref, q_ref, k_ref, v_ref, o_ref, gate_ref, g_ref, y_ref,
                  r_sc, cm_sc, b_sc, h_sc, st_sc, kt_sc):
    L = MLSTM_CHUNK
    dh = MLSTM_HEAD_DIM
    hpb = q_ref.shape[2] // dh
    head0 = pl.program_id(1) * hpb
    nc = q_ref.shape[1] // L
    lane = lax.broadcasted_iota(jnp.int32, (nc, L), 1)
    shifts = [1 << t for t in range(int(math.log2(L)))]
    for hh in range(hpb):
        for d in range(2):
            i_pre = gate_ref[0, 2 * d, hh] + bias_ref[2 * d, head0 + hh]
            f_pre = gate_ref[0, 2 * d + 1, hh] + bias_ref[2 * d + 1, head0 + hh]
            logf = jnp.minimum(f_pre, 0.0) - jnp.log1p(jnp.exp(-jnp.abs(f_pre)))
            bc = logf
            for sh in shifts:
                if d == 0:
                    bc = bc + jnp.where(lane >= sh, pltpu.roll(bc, sh, axis=1), 0.0)
                else:
                    bc = bc + jnp.where(lane < L - sh, pltpu.roll(bc, L - sh, axis=1), 0.0)
            r = i_pre - bc
            cm = r
            for sh in shifts:
                if d == 0:
                    cm = jnp.maximum(cm, jnp.where(lane >= sh, pltpu.roll(cm, sh, axis=1), -jnp.inf))
                else:
                    cm = jnp.maximum(cm, jnp.where(lane < L - sh, pltpu.roll(cm, L - sh, axis=1), -jnp.inf))
            r_sc[hh, d] = r
            cm_sc[hh, d] = cm
            b_sc[hh, d] = bc
    st_sc[...] = jnp.zeros(st_sc.shape, F32)
    tr = 512
    for hh in range(hpb):
        for r in range(0, q_ref.shape[1], tr):
            kt_sc[hh, :, r:r + tr] = jnp.transpose(k_ref[0, r:r + tr, hh * dh:(hh + 1) * dh].astype(F32)).astype(BF16)

    def body(c, ms):
        chains = [(hh, d, c if d == 0 else nc - 1 - c) for hh in range(hpb) for d in range(2)]
        return tuple(_mlstm_step(chains, ms, q_ref, kt_sc, v_ref, r_sc, cm_sc, b_sc, st_sc, h_sc))

    lax.fori_loop(0, nc, body, tuple(jnp.zeros((1, 1), F32) for _ in range(2 * hpb)))
    for hh in range(hpb):
        lanes = slice(hh * dh, (hh + 1) * dh)
        hsum = h_sc[hh, 0] + h_sc[hh, 1]
        ms = jnp.mean(hsum * hsum, axis=-1, keepdims=True)
        y = hsum * lax.rsqrt(ms + EPS) * g_ref[:, lanes]
        y_ref[0, :, lanes] = (jax.nn.sigmoid(o_ref[0, :, lanes].astype(F32)) * y).astype(y_ref.dtype)


def mlstm_branch(p3, gates5, gate_bias, norm_g, *, heads_per_block):
    b, s, _ = p3.shape
    L = MLSTM_CHUNK
    hpb = heads_per_block
    wb = hpb * MLSTM_HEAD_DIM
    nc = s // L
    blk = lambda off: pl.BlockSpec((1, s, wb), lambda bi, h, off=off: (bi, 0, off // wb + h))
    vmem = 2 * 5 * s * wb * 2 + 2 * s * wb * 4 + 6 * hpb * nc * L * 4 + 4 * hpb * wb * wb * 4 + 3 * s * wb * 4
    return pl.pallas_call(
        _mlstm_kernel,
        grid=(b, MLSTM_HEADS // hpb),
        in_specs=[
            pl.BlockSpec(memory_space=pltpu.SMEM),
            blk(OFF_CQ), blk(OFF_CK), blk(OFF_CV), blk(OFF_CO),
            pl.BlockSpec((1, 4, hpb, nc, L), lambda bi, h: (bi, 0, h, 0, 0)),
            pl.BlockSpec((1, wb), lambda bi, h: (0, h)),
        ],
        out_specs=pl.BlockSpec((1, s, wb), lambda bi, h: (bi, 0, h)),
        out_shape=jax.ShapeDtypeStruct((b, s, BRANCH_WIDTH), BF16),
        scratch_shapes=[
            pltpu.VMEM((hpb, 2, nc, L), F32), pltpu.VMEM((hpb, 2, nc, L), F32), pltpu.VMEM((hpb, 2, nc, L), F32),
            pltpu.VMEM((hpb, 2, s, MLSTM_HEAD_DIM), F32),
            pltpu.VMEM((hpb, 2, MLSTM_HEAD_DIM, 2 * MLSTM_HEAD_DIM), F32),
            pltpu.VMEM((hpb, MLSTM_HEAD_DIM, s), BF16),
        ],
        compiler_params=_params(("parallel", "parallel"), vmem),
        name="mlstm",
    )(gate_bias.astype(F32), p3, p3, p3, p3, gates5, norm_g.reshape(1, BRANCH_WIDTH).astype(F32))


def _rel_bucket(rel):
    half = REL_BUCKETS // 2
    max_exact = half // 2
    ret = jnp.where(rel > 0, half, 0)
    n = jnp.abs(rel)
    nf = jnp.maximum(n, 1).astype(F32)
    large = max_exact + (jnp.log(nf / max_exact) / math.log(REL_MAX_DIST / max_exact) * (half - max_exact)).astype(jnp.int32)
    large = jnp.minimum(large, half - 1)
    return ret + jnp.where(n < max_exact, n, large)


def diff_bias_tiles(rel_bias, t):
    assert t >= REL_MAX_DIST
    k = jnp.arange(2 * t, dtype=jnp.int32)
    rel = jnp.arange(-2, 3, dtype=jnp.int32)[:, None] * t + jnp.where(k < t, k, k - 2 * t)[None, :]
    onehot = (_rel_bucket(rel)[:, :, None] == jnp.arange(REL_BUCKETS, dtype=jnp.int32)).astype(F32)
    period = jnp.einsum('dkb,bh->hdk', onehot, rel_bias.astype(F32) * LOG2E, precision=lax.Precision.HIGHEST)
    far = period[:, 0::4, 0]
    return period, far


def _diff_attn_kernel(far_ref, q_ref, k_ref, v_ref, period_ref, lam_ref, g_ref, o_ref,
                      q_sc, kt_sc, vaug_sc, bias_sc, sc_buf, m_ref, acc_ref, *, t, lam_init):
    dq = DIFF_QK_DIM
    dv = DIFF_V_DIM
    s = k_ref.shape[1]
    nt = s // t
    head = pl.program_id(1)
    ahead = sc_buf.shape[0]
    rb = sc_buf.shape[1]
    grp = sc_buf.shape[2] // t
    wrap = (grp - 1) * t
    assert nt % grp == 0 and t % rb == 0 and nt >= 4

    tr = max(t, 512)
    for r in range(0, s, tr):
        kt_sc[:, r:r + tr] = jnp.transpose(k_ref[0, r:r + tr, :].astype(F32)).astype(BF16)
    kt_sc[:, s:] = kt_sc[:, :wrap]
    vaug_sc[:s, :dv] = v_ref[0]
    vaug_sc[s:, :dv] = v_ref[0, :wrap, :]
    lane = lax.broadcasted_iota(jnp.int32, (s + wrap, dv), 1)
    vaug_sc[:, dv:] = jnp.where(lane == 0, 1.0, 0.0).astype(BF16)
    for dl in range(period_ref.shape[1]):
        full = jnp.broadcast_to(period_ref[0, dl:dl + 1, :], (t, 2 * t))
        bias_sc[dl] = pltpu.roll(full, 0, axis=1, stride=1, stride_axis=0)[:, :t]
    q_scale = dq ** -0.5 * LOG2E
    q_all = (q_ref[0].astype(F32) * q_scale).astype(BF16)
    q_sc[0] = q_all[:, :dq]
    q_sc[1] = q_all[:, dq:]
    lp = lam_ref[...]
    lam = (jnp.exp(jnp.sum(lp[0:1] * lp[1:2], axis=-1, keepdims=True))
           - jnp.exp(jnp.sum(lp[2:3] * lp[3:4], axis=-1, keepdims=True)) + lam_init)

    def key_tile(i, delta):
        j = lax.rem(i + delta, nt)
        return j, pl.multiple_of(j * t, t)

    def qk(i, item):
        pair, mp, hf = item
        row0 = pl.multiple_of(i * t + hf * rb, rb)
        _, off = key_tile(i, grp * pair)
        return jnp.dot(q_sc[mp, pl.ds(row0, rb), :], kt_sc[mp * dq:(mp + 1) * dq, pl.ds(off, grp * t)],
                       preferred_element_type=F32)

    items = [(pair, mp, hf) for pair in range(nt // grp) for mp in range(2) for hf in range(t // rb)]
    for a in range(ahead):
        sc_buf[a] = qk(0, items[a])

    def tile_body(i, carry):
        pending = [sc_buf[a] for a in range(ahead)]
        for n, (pair, mp, hf) in enumerate(items):
            sc = pending.pop(0)
            if n + ahead < len(items):
                pending.append(qk(i, items[n + ahead]))
            else:
                pending.append(qk(jnp.minimum(i + 1, nt - 1), items[n + ahead - len(items)]))
            rows = slice(hf * rb, (hf + 1) * rb)
            scs = [sc[:, p * t:(p + 1) * t] for p in range(grp)]
            _, off = key_tile(i, grp * pair)
            shifts, row_max = [], None
            for piece, delta in enumerate(range(grp * pair, grp * pair + grp)):
                j, _ = key_tile(i, delta)
                if delta in (0, 1, nt - 1):
                    tile = 2 if delta == 0 else jnp.clip(j - i, -2, 2) + 2
                    scs[piece] = scs[piece] + bias_sc[tile, rows, :]
                    shifts.append(None)
                    rm = jnp.max(scs[piece], axis=-1, keepdims=True)
                else:
                    shifts.append(jnp.where(j > i, far_ref[head, 1], far_ref[head, 0]))
                    rm = jnp.max(scs[piece], axis=-1, keepdims=True) + shifts[-1]
                row_max = rm if row_max is None else jnp.maximum(row_max, rm)
            if pair == 0:
                m_new = jnp.broadcast_to(row_max, (rb, V7X_LANES))
            else:
                m_prev = m_ref[mp, rows, :]
                m_new = jnp.maximum(m_prev, row_max)
            ps = []
            for piece in range(grp):
                m_sub = m_new if shifts[piece] is None else m_new - shifts[piece]
                ps.append(jnp.exp2((scs[piece] - pltpu.repeat(m_sub, t // V7X_LANES, axis=1)).astype(BF16)))
            pv = jnp.dot(jnp.concatenate(ps, axis=1), vaug_sc[pl.ds(off, grp * t), :], preferred_element_type=F32)
            if pair == 0:
                acc_ref[mp, rows, :] = pv
            else:
                alpha = jnp.exp2(m_prev - m_new)
                acc_ref[mp, rows, :] = acc_ref[mp, rows, :] * pltpu.repeat(alpha, 2 * dv // V7X_LANES, axis=1) + pv
            m_ref[mp, rows, :] = m_new
        for a in range(ahead):
            sc_buf[a] = pending[a]
        o0 = acc_ref[0, :, :dv] / acc_ref[0, :, dv:dv + 1]
        o1 = acc_ref[1, :, :dv] / acc_ref[1, :, dv:dv + 1]
        o = o0 - lam * o1
        ms = jnp.mean(o * o, axis=-1, keepdims=True)
        row0 = pl.multiple_of(i * t, t)
        o_ref[0, pl.ds(row0, t), :] = (o * lax.rsqrt(ms + EPS) * g_ref[...] * (1.0 - lam_init)).astype(o_ref.dtype)
        return carry

    lax.fori_loop(0, nt, tile_body, 0)


def diff_attention(p3, bias_period, bias_far, lam_params, sub_g, *, t, rows_per_item, tiles_per_item, layer_number):
    b, s, _ = p3.shape
    dv = DIFF_V_DIM
    n_off = bias_period.shape[1]
    assert s // t >= 4, "tiles 2 .. s/t-2 steps away from the query tile must all be beyond REL_MAX_DIST"
    lam_init = 0.8 - 0.6 * math.exp(-0.3 * (layer_number - 1))
    sw = s + (tiles_per_item - 1) * t
    item = rows_per_item * tiles_per_item * t
    vmem = (2 * (s * 128 * 2 + 128 * s * 2 + s * dv * 2 + s * dv * 2) + n_off * t * t * 4 + sw * (2 * dv + 128) * 2
            + 2 * s * 128 * 2 + item * 4 + 2 * t * 128 * 4 * 3 + 8 * item * 4)
    return pl.pallas_call(
        functools.partial(_diff_attn_kernel, t=t, lam_init=lam_init),
        grid=(b, DIFF_HEADS),
        in_specs=[
            pl.BlockSpec(memory_space=pltpu.SMEM),
            pl.BlockSpec((1, s, 128), lambda bi, h: (bi, 0, OFF_DQ // 128 + h)),
            pl.BlockSpec((1, s, 128), lambda bi, h: (bi, 0, OFF_DK // 128 + h)),
            pl.BlockSpec((1, s, dv), lambda bi, h: (bi, 0, OFF_DV // dv + h)),
            pl.BlockSpec((1, n_off, 2 * t), lambda bi, h: (h, 0, 0)),
            pl.BlockSpec((4, DIFF_QK_DIM), lambda bi, h: (0, 0)),
            pl.BlockSpec((1, dv), lambda bi, h: (0, 0)),
        ],
        out_specs=pl.BlockSpec((1, s, dv), lambda bi, h: (bi, 0, h)),
        out_shape=jax.ShapeDtypeStruct((b, s, BRANCH_WIDTH), BF16),
        scratch_shapes=[pltpu.VMEM((2, s, DIFF_QK_DIM), BF16), pltpu.VMEM((2 * DIFF_QK_DIM, sw), BF16),
                        pltpu.VMEM((sw, 2 * dv), BF16),
                        pltpu.VMEM((n_off, t, t), F32), pltpu.VMEM((1, rows_per_item, tiles_per_item * t), F32),
                        pltpu.VMEM((2, t, V7X_LANES), F32), pltpu.VMEM((2, t, 2 * dv), F32)],
        compiler_params=_params(("parallel", "parallel"), vmem),
        name="diff_attn",
    )(bias_far, p3, p3, p3, bias_period, lam_params.astype(F32), sub_g.reshape(1, dv).astype(F32))


def _merge_kernel(ya_ref, yb_ref, yc_ref, yd_ref, g_ref, wg_ref, wb_ref, wo_ref, x_ref, o_ref, wb_sc, wo_sc):
    dm = x_ref.shape[1]

    @pl.when(pl.program_id(0) == 0)
    def _():
        for n in range(wb_ref.shape[0]):
            wb_sc[n] = wb_ref[n].astype(BF16)
        wo_sc[...] = wo_ref[...].astype(BF16)

    tm = x_ref.shape[0]
    halves = [slice(0, tm // 2), slice(tm // 2, tm)]
    y_refs = (ya_ref, yb_ref, yc_ref, yd_ref)
    gates, branch = [], []
    for rows in halves:
        x = x_ref[rows, :]
        ms = jnp.mean(x * x, axis=-1, keepdims=True)
        u = (x * lax.rsqrt(ms + EPS) * g_ref[...]).astype(BF16)
        gates.append([jnp.dot(u, wg_ref[:, n * dm:(n + 1) * dm], preferred_element_type=F32)
                      for n in range(len(y_refs))])
        branch.append([jnp.dot(y_ref[rows, :], wb_sc[n], preferred_element_type=F32)
                       for n, y_ref in enumerate(y_refs)])
    for rows, gts, brs in zip(halves, gates, branch):
        merged = None
        for gt, br in zip(gts, brs):
            term = jax.nn.sigmoid(gt) * br
            merged = term if merged is None else merged + term
        o_ref[rows, :] = x_ref[rows, :] + jnp.dot(merged.astype(BF16), wo_sc[...], preferred_element_type=F32)


def merge_branches(ys, norm_g, w_gates, w_branch, w_out, x2, *, tm):
    m, dm = x2.shape
    w = BRANCH_WIDTH
    y_spec = pl.BlockSpec((tm, w), lambda i: (i, 0))
    vmem = (2 * (4 * tm * w * 2 + 2 * tm * dm * 4) + N_BRANCHES * dm * dm * 2 + (4 * w * dm + dm * dm) * (4 + 2)
            + 12 * tm * dm * 4)
    return pl.pallas_call(
        _merge_kernel,
        grid=(m // tm,),
        in_specs=[
            y_spec, y_spec, y_spec, y_spec,
            pl.BlockSpec((1, dm), lambda i: (0, 0)),
            pl.BlockSpec((dm, N_BRANCHES * dm), lambda i: (0, 0), pipeline_mode=pl.Buffered(1)),
            pl.BlockSpec((N_BRANCHES, w, dm), lambda i: (0, 0, 0), pipeline_mode=pl.Buffered(1)),
            pl.BlockSpec((dm, dm), lambda i: (0, 0), pipeline_mode=pl.Buffered(1)),
            pl.BlockSpec((tm, dm), lambda i: (i, 0)),
        ],
        out_specs=pl.BlockSpec((tm, dm), lambda i: (i, 0)),
        out_shape=jax.ShapeDtypeStruct((m, dm), F32),
        scratch_shapes=[pltpu.VMEM((N_BRANCHES, w, dm), BF16), pltpu.VMEM((dm, dm), BF16)],
        compiler_params=_params(("arbitrary",), vmem),
        name="merge",
    )(*ys, norm_g.reshape(1, dm).astype(F32), w_gates, w_branch.astype(F32), w_out.astype(F32), x2)


HALO_ROWS = 16


def _conv_ffn_kernel(x_ref, xp_ref, xn_ref, g_ref, wa_ref, wl_ref, cw_ref, cb_ref, wd_ref, gf_ref, o_ref,
                     xe_sc, wd_sc, *, tiles_per_seq, col_chunk, final_norm):
    tm = o_ref.shape[0]
    f = wa_ref.shape[1]
    h = HALO_ROWS
    pos = pl.program_id(0) % tiles_per_seq

    @pl.when(pl.program_id(0) == 0)
    def _():
        wd_sc[...] = wd_ref[...].astype(BF16)

    def norm(x):
        ms = jnp.mean(x * x, axis=-1, keepdims=True)
        return x * lax.rsqrt(ms + EPS) * g_ref[...]

    xe_sc[:h] = jnp.where(pos == 0, 0.0, norm(xp_ref[...])).astype(BF16)
    xe_sc[h:h + tm] = norm(x_ref[...]).astype(BF16)
    xe_sc[h + tm:] = jnp.where(pos == tiles_per_seq - 1, 0.0, norm(xn_ref[...])).astype(BF16)
    k0 = math.sqrt(2.0 / math.pi)
    bounds = list(range(0, f, col_chunk)) + [f]
    chunks = [slice(lo, hi) for lo, hi in zip(bounds[:-1], bounds[1:])]

    def project(cols):
        a_ext = jnp.dot(xe_sc[...], wa_ref[:, cols], preferred_element_type=F32)
        lin = jnp.dot(xe_sc[h:h + tm, :], wl_ref[:, cols], preferred_element_type=F32)
        return a_ext, lin

    y = x_ref[...]
    nxt = project(chunks[0])
    for n, cols in enumerate(chunks):
        a_ext, lin = nxt
        if n + 1 < len(chunks):
            nxt = project(chunks[n + 1])
        cw = cw_ref[:, cols]
        c = (a_ext[h - 1:h - 1 + tm] * cw[0:1] + a_ext[h:h + tm] * cw[1:2] + a_ext[h + 1:h + 1 + tm] * cw[2:3]
             + cb_ref[:, cols])
        t = jnp.tanh(c * ((c * c) * (k0 * 0.044715) + k0))
        hmid = ((c + c * t) * lin).astype(BF16)
        y = y + jnp.dot(hmid, wd_sc[cols, :], preferred_element_type=F32)
    if final_norm:
        ms = jnp.mean(y * y, axis=-1, keepdims=True)
        y = y * lax.rsqrt(ms + EPS) * gf_ref[...]
    o_ref[...] = y


def conv_ffn(x2, g, w_up, conv_w, conv_b, w_down, final_g, *, s, tm, final_norm):
    m, dm = x2.shape
    f = w_up.shape[1] // 2
    tiles_per_seq = s // tm
    hb = tm // HALO_ROWS
    n_halo = m // HALO_ROWS
    vmem = (2 * 2 * tm * dm * 4 + 2 * dm * f * 2 + f * dm * (4 + 2) + (tm + 2 * HALO_ROWS) * dm * 2
            + 12 * tm * 512 * 4 + 2 * tm * dm * 4)
    return pl.pallas_call(
        functools.partial(_conv_ffn_kernel, tiles_per_seq=tiles_per_seq, col_chunk=1024, final_norm=final_norm),
        grid=(m // tm,),
        in_specs=[
            pl.BlockSpec((tm, dm), lambda i: (i, 0)),
            pl.BlockSpec((HALO_ROWS, dm), lambda i: (jnp.maximum(i * hb - 1, 0), 0)),
            pl.BlockSpec((HALO_ROWS, dm), lambda i: (jnp.minimum((i + 1) * hb, n_halo - 1), 0)),
            pl.BlockSpec((1, dm), lambda i: (0, 0)),
            pl.BlockSpec((dm, f), lambda i: (0, 0), pipeline_mode=pl.Buffered(1)),
            pl.BlockSpec((dm, f), lambda i: (0, 1), pipeline_mode=pl.Buffered(1)),
            pl.BlockSpec((CONV_W, f), lambda i: (0, 0)),
            pl.BlockSpec((1, f), lambda i: (0, 0)),
            pl.BlockSpec((f, dm), lambda i: (0, 0), pipeline_mode=pl.Buffered(1)),
            pl.BlockSpec((1, dm), lambda i: (0, 0)),
        ],
        out_specs=pl.BlockSpec((tm, dm), lambda i: (i, 0)),
        out_shape=jax.ShapeDtypeStruct((m, dm), F32),
        scratch_shapes=[pltpu.VMEM((tm + 2 * HALO_ROWS, dm), BF16), pltpu.VMEM((f, dm), BF16)],
        compiler_params=_params(("arbitrary",), vmem),
        name="conv_ffn",
    )(x2, x2, x2, g.reshape(1, dm).astype(F32), w_up, w_up, conv_w.astype(F32), conv_b.reshape(1, f).astype(F32),
      w_down.astype(F32), final_g.reshape(1, dm).astype(F32))


def _arrange_w_in(w):
    widths = (512, 512, 128, 128, 512, 512, 512, 512, 16, 512, 512, 512, 4096)
    offs = [0]
    for wd in widths:
        offs.append(offs[-1] + wd)
    seg = lambda n: w[:, offs[n]:offs[n + 1]]
    a, bq, bk, bv, cq, ck, cv, co, cgate, dq, dk, dv, gates = (seg(n) for n in range(len(widths)))
    main = jnp.concatenate([a, bq, cq, ck, cv, co, dq, dk, dv, bk, bv], axis=1).astype(BF16)
    gate = jnp.pad(cgate, ((0, 0), (0, V7X_LANES - cgate.shape[1]))).astype(BF16)
    return main, gate, gates.astype(BF16)


def kernel(x, norm_mix_g, w_in, mlstm_gate_bias, qk_norm_g, mlstm_norm_g, diff_lambda, diff_norm_g, rel_bias,
           w_branch, w_out, norm_ffn_g, w_up, conv_w, conv_b, w_down, final_norm_g):
    b, s, dm = x.shape
    depth = w_in.shape[0]
    m = b * s
    d_ff = w_down.shape[1]
    L = MLSTM_CHUNK
    t_diff = 512

    bd, dft_hi, dft_lo = fourier_tables(s)
    cos2, sin2 = rope_tables(s)
    bias_tiles, bias_far = diff_bias_tiles(rel_bias, t_diff)

    x2 = x.reshape(m, dm)
    for layer in range(depth):
        w_main, w_gate, w_branch_gates = _arrange_w_in(w_in[layer])
        p2, cgate = norm_matmul(x2, norm_mix_g[layer], w_main, w_gate, tm=512, tn=P_WIDTH)
        p3 = p2.reshape(b, s, P_WIDTH)

        y_a = fourier_mix(p3, bd, dft_hi, dft_lo, tm=512)

        q_b, k_b, v_b = gqa_prep(p2, qk_norm_g[layer], cos2, sin2, s=s, tm=512)
        kt_b = jnp.swapaxes(k_b.reshape(b, s, GQA_KV_HEADS * HEAD_DIM), 1, 2)
        y_b = gqa_attention(q_b.reshape(b, s, -1), kt_b, v_b.reshape(b, s, -1), tq=512, tk=1024)

        gates5 = jnp.transpose(cgate[:, :4 * MLSTM_HEADS].reshape(b, s, 4, MLSTM_HEADS), (0, 2, 3, 1))
        gates5 = gates5.reshape(b, 4, MLSTM_HEADS, s // L, L)
        y_c = mlstm_branch(p3, gates5, mlstm_gate_bias[layer], mlstm_norm_g[layer], heads_per_block=2)

        y_d = diff_attention(p3, bias_tiles, bias_far, diff_lambda[layer], diff_norm_g[layer],
                             t=t_diff, rows_per_item=256, tiles_per_item=4, layer_number=layer + 1)

        ys = [y.reshape(m, BRANCH_WIDTH) for y in (y_a, y_b, y_c, y_d)]
        x2 = merge_branches(ys, norm_mix_g[layer], w_branch_gates, w_branch[layer], w_out[layer], x2, tm=512)

        half_lin = jnp.concatenate([jnp.ones((d_ff,), F32), jnp.full((d_ff,), 0.5, F32)])
        x2 = conv_ffn(x2, norm_ffn_g[layer], (w_up[layer] * half_lin).astype(BF16), conv_w[layer], conv_b[layer],
                      w_down[layer], final_norm_g, s=s, tm=512, final_norm=(layer == depth - 1))
    return x2.reshape(b, s, dm)
```

```python
import functools
import math

import jax
import jax.numpy as jnp
from jax import lax
from jax.experimental import pallas as pl
from jax.experimental.pallas import tpu as pltpu

F32 = jnp.float32
BF16 = jnp.bfloat16

GRID_W = 64
HEAD_DIM = 64
BRANCH_WIDTH = 512
N_BRANCHES = 4
FOURIER_GROUP_DIM = 64
GQA_Q_HEADS = 8
GQA_KV_HEADS = 2
MLSTM_HEADS = 4
MLSTM_HEAD_DIM = 128
MLSTM_CHUNK = 128
DIFF_HEADS = 4
DIFF_QK_DIM = 64
DIFF_V_DIM = 128
REL_BUCKETS = 32
REL_MAX_DIST = 128
CONV_W = 3
ROPE_BASE = 10000.0
EPS = 1e-6
LOG2E = math.log2(math.e)

V7X_LANES = 128
V7X_VMEM_BYTES = 64 * 1024 * 1024
V7X_VMEM_CAP = V7X_VMEM_BYTES - 8 * 1024 * 1024

OFF_A = 0
OFF_BQ = 512
OFF_CQ = 1024
OFF_CK = 1536
OFF_CV = 2048
OFF_CO = 2560
OFF_DQ = 3072
OFF_DK = 3584
OFF_DV = 4096
OFF_BKV = 4608
P_WIDTH = 4864


def _params(sem, vmem_bytes):
    limit = int(min(max(vmem_bytes * 3 // 2 + (4 << 20), 16 << 20), V7X_VMEM_CAP))
    return pltpu.CompilerParams(dimension_semantics=sem, vmem_limit_bytes=limit)


def _norm_mm_kernel(x_ref, g_ref, w_ref, o_ref, xn_ref):
    @pl.when(pl.program_id(1) == 0)
    def _():
        x = x_ref[...]
        ms = jnp.mean(x * x, axis=-1, keepdims=True)
        xn_ref[...] = (x * lax.rsqrt(ms + EPS) * g_ref[...]).astype(BF16)

    o_ref[...] = jnp.dot(xn_ref[...], w_ref[...], preferred_element_type=F32).astype(o_ref.dtype)


def _norm_mm_gate_kernel(x_ref, g_ref, w_ref, wg_ref, o_ref, og_ref, xn_ref):
    @pl.when(pl.program_id(1) == 0)
    def _():
        x = x_ref[...]
        ms = jnp.mean(x * x, axis=-1, keepdims=True)
        xn = (x * lax.rsqrt(ms + EPS) * g_ref[...]).astype(BF16)
        xn_ref[...] = xn
        og_ref[...] = jnp.dot(xn, wg_ref[...], preferred_element_type=F32)

    o_ref[...] = jnp.dot(xn_ref[...], w_ref[...], preferred_element_type=F32).astype(o_ref.dtype)


def norm_matmul(x, g, w, w_gate=None, *, tm, tn):
    m, k = x.shape
    n = w.shape[1]
    grid = (m // tm, n // tn)
    vmem = 2 * tm * k * 4 + tm * k * 2 + 2 * k * tn * 2 + 2 * tm * tn * 2 + 4 * tm * k
    x_spec = pl.BlockSpec((tm, k), lambda i, j: (i, 0))
    g_spec = pl.BlockSpec((1, k), lambda i, j: (0, 0))
    w_spec = pl.BlockSpec((k, tn), lambda i, j: (0, j), pipeline_mode=pl.Buffered(1 if n == tn else 2))
    o_spec = pl.BlockSpec((tm, tn), lambda i, j: (i, j))
    scratch = [pltpu.VMEM((tm, k), BF16)]
    g2 = g.reshape(1, k).astype(F32)
    if w_gate is None:
        return pl.pallas_call(
            _norm_mm_kernel,
            grid=grid,
            in_specs=[x_spec, g_spec, w_spec],
            out_specs=o_spec,
            out_shape=jax.ShapeDtypeStruct((m, n), BF16),
            scratch_shapes=scratch,
            compiler_params=_params(("parallel", "arbitrary"), vmem),
            name="norm_matmul",
        )(x, g2, w)
    ng = w_gate.shape[1]
    return pl.pallas_call(
        _norm_mm_gate_kernel,
        grid=grid,
        in_specs=[x_spec, g_spec, w_spec, pl.BlockSpec((k, ng), lambda i, j: (0, 0))],
        out_specs=[o_spec, pl.BlockSpec((tm, ng), lambda i, j: (i, 0))],
        out_shape=[jax.ShapeDtypeStruct((m, n), BF16), jax.ShapeDtypeStruct((m, ng), F32)],
        scratch_shapes=scratch,
        compiler_params=_params(("parallel", "arbitrary"), vmem),
        name="norm_matmul_gate",
    )(x, g2, w, w_gate)


FOURIER_HALO = 16


def _fourier_kernel(a_ref, bd_ref, hi_ref, lo_ref, flip_ref, o_ref, z_ref, ext_ref, *, tm, row_chunk):
    s = a_ref.shape[1]
    w = a_ref.shape[2]
    n_lo = lo_ref.shape[1]
    i = pl.program_id(1)

    @pl.when(i == 0)
    def _():
        for r in range(0, s, row_chunk):
            a = a_ref[0, r:r + row_chunk, :]
            zc = jnp.dot(a, bd_ref[...], preferred_element_type=F32)
            z_ref[r:r + row_chunk, :] = zc[:, :w].astype(BF16)
            z_ref[s + r:s + r + row_chunk, :] = zc[:, w:].astype(BF16)

    cl, sl = lo_ref[0], lo_ref[1]
    for r in range(tm // n_lo + 1):
        rows = n_lo if r < tm // n_lo else ext_ref.shape[0] - tm
        k1 = i * (tm // n_lo) + r
        ch = hi_ref[0, pl.ds(k1, 1), :]
        nsh = hi_ref[1, pl.ds(k1, 1), :]
        ext_ref[r * n_lo:r * n_lo + rows, :s] = (cl[:rows] * ch + sl[:rows] * nsh).astype(BF16)
        ext_ref[r * n_lo:r * n_lo + rows, s:] = (cl[:rows] * nsh - sl[:rows] * ch).astype(BF16)
    p = jnp.dot(ext_ref[:, :s], z_ref[:s], preferred_element_type=F32)
    q = jnp.dot(ext_ref[:, s:], z_ref[s:], preferred_element_type=F32)
    lo = pl.multiple_of(i * tm, tm)
    o_ref[0, pl.ds(lo, tm), :] = (p[:tm] + q[:tm]).astype(o_ref.dtype)
    mirrored = (p[1:tm + 1] - q[1:tm + 1]).astype(BF16)
    hi = pl.multiple_of(s - (i + 1) * tm, tm)
    o_ref[0, pl.ds(hi, tm), :] = jnp.dot(flip_ref[...], mirrored, preferred_element_type=F32).astype(o_ref.dtype)


def fourier_tables(s):
    cg = FOURIER_GROUP_DIM
    jj = jnp.arange(cg, dtype=jnp.int32)
    ang_c = (2.0 * math.pi / cg) * ((jj[:, None] * jj[None, :]) % cg).astype(F32)
    eye_g = jnp.eye(BRANCH_WIDTH // cg, dtype=F32)
    bd_c = jnp.kron(eye_g, jnp.cos(ang_c)) * cg ** -0.5
    bd_s = jnp.kron(eye_g, jnp.sin(ang_c)) * cg ** -0.5
    bd = jnp.concatenate([bd_c, bd_s], axis=1).astype(BF16)
    n_lo = s // cg
    nn = jnp.arange(s, dtype=jnp.int32)
    k1 = jnp.arange(cg // 2 + 2, dtype=jnp.int32)
    ang_hi = (2.0 * math.pi / cg) * ((k1[:, None] * nn[None, :]) % cg).astype(F32)
    ll = jnp.arange(n_lo, dtype=jnp.int32)
    ang_lo = (2.0 * math.pi / s) * ((ll[:, None] * nn[None, :]) % s).astype(F32)
    scale = s ** -0.5
    hi = jnp.stack([jnp.cos(ang_hi), -jnp.sin(ang_hi)]) * scale
    lo = jnp.stack([jnp.cos(ang_lo), jnp.sin(ang_lo)])
    return bd, hi, lo


def fourier_mix(p3, bd, hi, lo, *, tm):
    b, s, _ = p3.shape
    w = BRANCH_WIDTH
    n_lo = lo.shape[1]
    h = FOURIER_HALO
    assert (s // 2) % tm == 0 and tm % n_lo == 0 and h <= n_lo
    r = jnp.arange(tm, dtype=jnp.int32)
    flip = (r[:, None] + r[None, :] == tm - 1).astype(BF16)
    vmem = (s * w * 2 + 2 * w * w * 2 + 2 * (hi.size + lo.size) * 4 + (tm + h) * 2 * s * 2 + 2 * s * w * 2
            + 2 * s * w * 2 + 8 * (tm + h) * w * 4 + 8 * n_lo * s * 4)
    return pl.pallas_call(
        functools.partial(_fourier_kernel, tm=tm, row_chunk=min(s, 512)),
        grid=(b, s // 2 // tm),
        in_specs=[
            pl.BlockSpec((1, s, w), lambda bi, i: (bi, 0, OFF_A // w), pipeline_mode=pl.Buffered(1)),
            pl.BlockSpec((w, 2 * w), lambda bi, i: (0, 0), pipeline_mode=pl.Buffered(1)),
            pl.BlockSpec(hi.shape, lambda bi, i: (0, 0, 0)),
            pl.BlockSpec(lo.shape, lambda bi, i: (0, 0, 0)),
            pl.BlockSpec((tm, tm), lambda bi, i: (0, 0)),
        ],
        out_specs=pl.BlockSpec((1, s, w), lambda bi, i: (bi, 0, 0)),
        out_shape=jax.ShapeDtypeStruct((b, s, w), BF16),
        scratch_shapes=[pltpu.VMEM((2 * s, w), BF16), pltpu.VMEM((tm + h, 2 * s), BF16)],
        compiler_params=_params(("parallel", "arbitrary"), vmem),
        name="fourier",
    )(p3, bd, hi, lo, flip)


def rope_tables(s):
    rows = s // GRID_W
    row_id = jnp.repeat(jnp.arange(rows, dtype=F32), GRID_W)
    col_id = jnp.tile(jnp.arange(GRID_W, dtype=F32), rows)
    n_pairs = HEAD_DIM // 4
    inv_freq = ROPE_BASE ** (-jnp.arange(n_pairs, dtype=F32) / n_pairs)
    ang = jnp.concatenate([row_id[:, None] * inv_freq, col_id[:, None] * inv_freq], axis=-1)
    cos, sin = jnp.cos(ang), jnp.sin(ang)
    return jnp.concatenate([cos, cos] * 2, axis=-1), jnp.concatenate([-sin, sin] * 2, axis=-1)


def _norm_rope(x, g, seg, cos2, sin2):
    half = HEAD_DIM // 2
    x2 = x * x
    hi = x2.astype(BF16)
    lo = (x2 - hi.astype(F32)).astype(BF16)
    ms = jnp.dot(hi, seg, preferred_element_type=F32) + jnp.dot(lo, seg, preferred_element_type=F32)
    y = x * lax.rsqrt(ms + EPS) * g
    lane = lax.broadcasted_iota(jnp.int32, (x.shape[0], V7X_LANES), 1)
    first_half = (lane % HEAD_DIM) < half
    outs = []
    for cb in range(x.shape[1] // V7X_LANES):
        yb = y[:, cb * V7X_LANES:(cb + 1) * V7X_LANES]
        rot = jnp.where(first_half, pltpu.roll(yb, V7X_LANES - half, axis=1), pltpu.roll(yb, half, axis=1))
        outs.append(yb * cos2 + rot * sin2)
    return outs[0] if len(outs) == 1 else jnp.concatenate(outs, axis=-1)


def _gqa_prep_kernel(q_ref, kv_ref, gq_ref, gk_ref, seg_ref, cos_ref, sin_ref, qo_ref, ko_ref, vo_ref):
    d = HEAD_DIM
    nk = GQA_KV_HEADS * d
    cos2, sin2 = cos_ref[...], sin_ref[...]
    q = q_ref[...].astype(F32)
    kv = kv_ref[...].astype(F32)
    qo_ref[...] = _norm_rope(q, gq_ref[...], seg_ref[...], cos2, sin2).astype(BF16)
    ko_ref[...] = _norm_rope(kv[:, :nk], gk_ref[...], seg_ref[:nk, :nk], cos2, sin2).astype(BF16)
    v = kv[:, nk:]
    lane = lax.broadcasted_iota(jnp.int32, v.shape, 1)
    ones_col = jnp.where(lane == d, 1.0, 0.0)
    vo_ref[:, :nk] = jnp.where(lane < d, v, ones_col).astype(BF16)
    vo_ref[:, nk:] = jnp.where(lane < d, pltpu.roll(v, d, axis=1), ones_col).astype(BF16)


def gqa_prep(p2, qk_g, cos2, sin2, *, s, tm):
    m = p2.shape[0]
    nq = GQA_Q_HEADS * HEAD_DIM
    nkv = GQA_KV_HEADS * HEAD_DIM
    assert nkv == V7X_LANES
    tiles_per_seq = s // tm
    q_scale = HEAD_DIM ** -0.5 * LOG2E
    gq = jnp.tile(qk_g[0].astype(F32) * q_scale, GQA_Q_HEADS).reshape(1, nq)
    gk = jnp.tile(qk_g[1].astype(F32), GQA_KV_HEADS).reshape(1, nkv)
    seg = jnp.kron(jnp.eye(GQA_Q_HEADS, dtype=F32), jnp.full((HEAD_DIM, HEAD_DIM), 1.0 / HEAD_DIM, F32)).astype(BF16)
    return pl.pallas_call(
        _gqa_prep_kernel,
        grid=(m // tm,),
        in_specs=[
            pl.BlockSpec((tm, nq), lambda i: (i, OFF_BQ // nq)),
            pl.BlockSpec((tm, 2 * nkv), lambda i: (i, OFF_BKV // (2 * nkv))),
            pl.BlockSpec((1, nq), lambda i: (0, 0)),
            pl.BlockSpec((1, nkv), lambda i: (0, 0)),
            pl.BlockSpec((nq, nq), lambda i: (0, 0)),
            pl.BlockSpec((tm, V7X_LANES), lambda i: (i % tiles_per_seq, 0)),
            pl.BlockSpec((tm, V7X_LANES), lambda i: (i % tiles_per_seq, 0)),
        ],
        out_specs=[
            pl.BlockSpec((tm, nq), lambda i: (i, 0)),
            pl.BlockSpec((tm, nkv), lambda i: (i, 0)),
            pl.BlockSpec((tm, 2 * nkv), lambda i: (i, 0)),
        ],
        out_shape=[
            jax.ShapeDtypeStruct((m, nq), BF16),
            jax.ShapeDtypeStruct((m, nkv), BF16),
            jax.ShapeDtypeStruct((m, 2 * nkv), BF16),
        ],
        compiler_params=_params(("parallel",), 16 * tm * nq * 4),
        name="gqa_prep",
    )(p2, p2, gq, gk, seg, cos2, sin2)


def _gqa_attn_kernel(q_ref, kt_ref, v_ref, o_ref, q_sc, m_ref, acc_ref, *, tk):
    d = HEAD_DIM
    tq = q_ref.shape[1]
    grp = q_ref.shape[2] // d
    s = kt_ref.shape[2]
    for g in range(grp):
        q_sc[g * tq:(g + 1) * tq, :] = q_ref[0, :, g * d:(g + 1) * d]

    def qk(item):
        c, g = item
        return jnp.dot(q_sc[g * tq:(g + 1) * tq, :], kt_ref[0, :, c * tk:(c + 1) * tk],
                       preferred_element_type=F32)

    items = [(c, g) for c in range(s // tk) for g in range(grp)]
    sc_next = qk(items[0])
    for n, (c, g) in enumerate(items):
        rows = slice(g * tq, (g + 1) * tq)
        sc = sc_next
        if n + 1 < len(items):
            sc_next = qk(items[n + 1])
        v = v_ref[0, c * tk:(c + 1) * tk, :]
        row_max = jnp.max(sc, axis=-1, keepdims=True)
        if c == 0:
            m_new = jnp.broadcast_to(row_max, (tq, V7X_LANES))
        else:
            m_prev = m_ref[rows, :]
            m_new = jnp.maximum(m_prev, row_max)
        p = jnp.exp2((sc - pltpu.repeat(m_new, tk // V7X_LANES, axis=1)).astype(BF16))
        pv = jnp.dot(p, v, preferred_element_type=F32)
        if c == 0:
            acc_ref[rows, :] = pv
        else:
            acc_ref[rows, :] = acc_ref[rows, :] * jnp.exp2(m_prev - m_new) + pv
        m_ref[rows, :] = m_new
    for g in range(grp):
        acc = acc_ref[g * tq:(g + 1) * tq, :]
        o_ref[0, :, g * d:(g + 1) * d] = (acc[:, :d] / acc[:, d:d + 1]).astype(o_ref.dtype)


def gqa_attention(q3, kt3, v3, *, tq, tk):
    b, s, nq = q3.shape
    d = HEAD_DIM
    grp = GQA_Q_HEADS // GQA_KV_HEADS
    mrows = grp * tq
    vmem = (2 * (tq * grp * d * 2 + d * s * 2 + s * 128 * 2 + tq * grp * d * 2) + 3 * mrows * 128 * 4
            + 12 * tq * tk * 4)
    return pl.pallas_call(
        functools.partial(_gqa_attn_kernel, tk=tk),
        grid=(b, GQA_KV_HEADS, s // tq),
        in_specs=[
            pl.BlockSpec((1, tq, grp * d), lambda bi, kv, i: (bi, i, kv)),
            pl.BlockSpec((1, d, s), lambda bi, kv, i: (bi, kv, 0)),
            pl.BlockSpec((1, s, 2 * d), lambda bi, kv, i: (bi, 0, kv)),
        ],
        out_specs=pl.BlockSpec((1, tq, grp * d), lambda bi, kv, i: (bi, i, kv)),
        out_shape=jax.ShapeDtypeStruct((b, s, nq), BF16),
        scratch_shapes=[pltpu.VMEM((mrows, d), BF16), pltpu.VMEM((mrows, V7X_LANES), F32),
                        pltpu.VMEM((mrows, 2 * d), F32)],
        compiler_params=_params(("parallel", "parallel", "parallel"), vmem),
        name="gqa_attn",
    )(q3, kt3, v3)


def _mlstm_step(chains, ms, q_ref, kt_sc, v_ref, r_sc, cm_sc, b_sc, st_sc, h_sc):
    L = MLSTM_CHUNK
    dh = MLSTM_HEAD_DIM
    assert L == dh
    scale = dh ** -0.5
    row_i = lax.broadcasted_iota(jnp.int32, (L, L), 0)
    col_i = lax.broadcasted_iota(jnp.int32, (L, L), 1)

    def col(x_row):
        return jnp.transpose(jnp.broadcast_to(x_row, (L, L)))

    pre = []
    for (hh, direction, c), m in zip(chains, ms):
        off = pl.multiple_of(c * L, L)
        lanes = slice(hh * dh, (hh + 1) * dh)
        q = q_ref[0, pl.ds(off, L), lanes]
        kt = kt_sc[hh, :, pl.ds(off, L)]
        v = v_ref[0, pl.ds(off, L), lanes]
        r_row = r_sc[hh, direction, pl.ds(c, 1), :]
        cm_row = cm_sc[hh, direction, pl.ds(c, 1), :]
        b_row = b_sc[hh, direction, pl.ds(c, 1), :]
        rmax = jnp.max(r_row, axis=-1, keepdims=True)
        btot = b_row[:, L - 1:L] if direction == 0 else b_row[:, 0:1]
        cmat = jnp.maximum(m, col(cm_row))
        mask = (row_i >= col_i) if direction == 0 else (row_i <= col_i)
        c_last = jnp.maximum(m, rmax)
        w_state = jnp.exp(r_row - c_last) * scale
        pre.append(dict(
            off=off, lanes=lanes, q=q, kt=kt,
            v_aug=jnp.concatenate([v, jnp.ones((L, dh), BF16)], axis=-1),
            w_intra=jnp.where(mask, jnp.exp(r_row - cmat), 0.0) * scale,
            w_inter=jnp.exp(m - cmat),
            den_floor=jnp.exp(-(col(b_row) + cmat)),
            decay=jnp.exp(m - c_last),
            kt_w=(kt.astype(F32) * w_state).astype(BF16),
            m_new=btot + c_last,
        ))
    s_raw = [jnp.dot(p["q"], p["kt"], preferred_element_type=F32) for p in pre]
    states = [st_sc[hh, direction] for hh, direction, _ in chains]
    inter = [jnp.dot(p["q"], st.astype(BF16), preferred_element_type=F32) for p, st in zip(pre, states)]
    upd = [jnp.dot(p["kt_w"], p["v_aug"], preferred_element_type=F32) for p in pre]
    for n, ((hh, direction, _), p) in enumerate(zip(chains, pre)):
        st_sc[hh, direction] = p["decay"] * states[n] + upd[n]
        intra = jnp.dot((s_raw[n] * p["w_intra"]).astype(BF16), p["v_aug"], preferred_element_type=F32)
        h_aug = jnp.concatenate([p["w_inter"], p["w_inter"]], axis=-1) * inter[n] + intra
        h_sc[hh, direction, pl.ds(p["off"], L), :] = h_aug[:, :dh] / jnp.maximum(jnp.abs(h_aug[:, dh:]), p["den_floor"])
    return [p["m_new"] for p in pre]


def _mlstm_kernel(bias_ref, q_ref, k_ref, v_ref, o_ref, gate_ref, g_ref, y_ref,
                  r_sc, cm_sc, b_sc, h_sc, st_sc, kt_sc):
    L = MLSTM_CHUNK
    dh = MLSTM_HEAD_DIM
    hpb = q_ref.shape[2] // dh
    head0 = pl.program_id(1) * hpb
    nc = q_ref.shape[1] // L
    lane = lax.broadcasted_iota(jnp.int32, (nc, L), 1)
    shifts = [1 << t for t in range(int(math.log2(L)))]
    for hh in range(hpb):
        for d in range(2):
            i_pre = gate_ref[0, 2 * d, hh] + bias_ref[2 * d, head0 + hh]
            f_pre = gate_ref[0, 2 * d + 1, hh] + bias_ref[2 * d + 1, head0 + hh]
            logf = jnp.minimum(f_pre, 0.0) - jnp.log1p(jnp.exp(-jnp.abs(f_pre)))
            bc = logf
            for sh in shifts:
                if d == 0:
                    bc = bc + jnp.where(lane >= sh, pltpu.roll(bc, sh, axis=1), 0.0)
                else:
                    bc = bc + jnp.where(lane < L - sh, pltpu.roll(bc, L - sh, axis=1), 0.0)
            r = i_pre - bc
            cm = r
            for sh in shifts:
                if d == 0:
                    cm = jnp.maximum(cm, jnp.where(lane >= sh, pltpu.roll(cm, sh, axis=1), -jnp.inf))
                else:
                    cm = jnp.maximum(cm, jnp.where(lane < L - sh, pltpu.roll(cm, L - sh, axis=1), -jnp.inf))
            r_sc[hh, d] = r
            cm_sc[hh, d] = cm
            b_sc[hh, d] = bc
    st_sc[...] = jnp.zeros(st_sc.shape, F32)
    tr = 512
    for hh in range(hpb):
        for r in range(0, q_ref.shape[1], tr):
            kt_sc[hh, :, r:r + tr] = jnp.transpose(k_ref[0, r:r + tr, hh * dh:(hh + 1) * dh].astype(F32)).astype(BF16)

    def body(c, ms):
        chains = [(hh, d, c if d == 0 else nc - 1 - c) for hh in range(hpb) for d in range(2)]
        return tuple(_mlstm_step(chains, ms, q_ref, kt_sc, v_ref, r_sc, cm_sc, b_sc, st_sc, h_sc))

    lax.fori_loop(0, nc, body, tuple(jnp.zeros((1, 1), F32) for _ in range(2 * hpb)))
    for hh in range(hpb):
        lanes = slice(hh * dh, (hh + 1) * dh)
        hsum = h_sc[hh, 0] + h_sc[hh, 1]
        ms = jnp.mean(hsum * hsum, axis=-1, keepdims=True)
        y = hsum * lax.rsqrt(ms + EPS) * g_ref[:, lanes]
        y_ref[0, :, lanes] = (jax.nn.sigmoid(o_ref[0, :, lanes].astype(F32)) * y).astype(y_ref.dtype)


def mlstm_branch(p3, gates5, gate_bias, norm_g, *, heads_per_block):
    b, s, _ = p3.shape
    L = MLSTM_CHUNK
    hpb = heads_per_block
    wb = hpb * MLSTM_HEAD_DIM
    nc = s // L
    blk = lambda off: pl.BlockSpec((1, s, wb), lambda bi, h, off=off: (bi, 0, off // wb + h))
    vmem = 2 * 5 * s * wb * 2 + 2 * s * wb * 4 + 6 * hpb * nc * L * 4 + 4 * hpb * wb * wb * 4 + 3 * s * wb * 4
    return pl.pallas_call(
        _mlstm_kernel,
        grid=(b, MLSTM_HEADS // hpb),
        in_specs=[
            pl.BlockSpec(memory_space=pltpu.SMEM),
            blk(OFF_CQ), blk(OFF_CK), blk(OFF_CV), blk(OFF_CO),
            pl.BlockSpec((1, 4, hpb, nc, L), lambda bi, h: (bi, 0, h, 0, 0)),
            pl.BlockSpec((1, wb), lambda bi, h: (0, h)),
        ],
        out_specs=pl.BlockSpec((1, s, wb), lambda bi, h: (bi, 0, h)),
        out_shape=jax.ShapeDtypeStruct((b, s, BRANCH_WIDTH), BF16),
        scratch_shapes=[
            pltpu.VMEM((hpb, 2, nc, L), F32), pltpu.VMEM((hpb, 2, nc, L), F32), pltpu.VMEM((hpb, 2, nc, L), F32),
            pltpu.VMEM((hpb, 2, s, MLSTM_HEAD_DIM), F32),
            pltpu.VMEM((hpb, 2, MLSTM_HEAD_DIM, 2 * MLSTM_HEAD_DIM), F32),
            pltpu.VMEM((hpb, MLSTM_HEAD_DIM, s), BF16),
        ],
        compiler_params=_params(("parallel", "parallel"), vmem),
        name="mlstm",
    )(gate_bias.astype(F32), p3, p3, p3, p3, gates5, norm_g.reshape(1, BRANCH_WIDTH).astype(F32))


def _rel_bucket(rel):
    half = REL_BUCKETS // 2
    max_exact = half // 2
    ret = jnp.where(rel > 0, half, 0)
    n = jnp.abs(rel)
    nf = jnp.maximum(n, 1).astype(F32)
    large = max_exact + (jnp.log(nf / max_exact) / math.log(REL_MAX_DIST / max_exact) * (half - max_exact)).astype(jnp.int32)
    large = jnp.minimum(large, half - 1)
    return ret + jnp.where(n < max_exact, n, large)


def diff_bias_tiles(rel_bias, t):
    assert t >= REL_MAX_DIST
    k = jnp.arange(2 * t, dtype=jnp.int32)
    rel = jnp.arange(-2, 3, dtype=jnp.int32)[:, None] * t + jnp.where(k < t, k, k - 2 * t)[None, :]
    onehot = (_rel_bucket(rel)[:, :, None] == jnp.arange(REL_BUCKETS, dtype=jnp.int32)).astype(F32)
    period = jnp.einsum('dkb,bh->hdk', onehot, rel_bias.astype(F32) * LOG2E, precision=lax.Precision.HIGHEST)
    far = period[:, 0::4, 0]
    return period, far


def _diff_attn_kernel(far_ref, q_ref, k_ref, v_ref, period_ref, lam_ref, g_ref, o_ref,
                      q_sc, kt_sc, vaug_sc, bias_sc, sc_buf, m_ref, acc_ref, *, t, lam_init):
    dq = DIFF_QK_DIM
    dv = DIFF_V_DIM
    s = k_ref.shape[1]
    nt = s // t
    head = pl.program_id(1)
    ahead = sc_buf.shape[0]
    rb = sc_buf.shape[1]
    grp = sc_buf.shape[2] // t
    wrap = (grp - 1) * t
    assert nt % grp == 0 and t % rb == 0 and nt >= 4

    tr = max(t, 512)
    for r in range(0, s, tr):
        kt_sc[:, r:r + tr] = jnp.transpose(k_ref[0, r:r + tr, :].astype(F32)).astype(BF16)
    kt_sc[:, s:] = kt_sc[:, :wrap]
    vaug_sc[:s, :dv] = v_ref[0]
    vaug_sc[s:, :dv] = v_ref[0, :wrap, :]
    lane = lax.broadcasted_iota(jnp.int32, (s + wrap, dv), 1)
    vaug_sc[:, dv:] = jnp.where(lane == 0, 1.0, 0.0).astype(BF16)
    for dl in range(period_ref.shape[1]):
        full = jnp.broadcast_to(period_ref[0, dl:dl + 1, :], (t, 2 * t))
        bias_sc[dl] = pltpu.roll(full, 0, axis=1, stride=1, stride_axis=0)[:, :t]
    q_scale = dq ** -0.5 * LOG2E
    q_all = (q_ref[0].astype(F32) * q_scale).astype(BF16)
    q_sc[0] = q_all[:, :dq]
    q_sc[1] = q_all[:, dq:]
    lp = lam_ref[...]
    lam = (jnp.exp(jnp.sum(lp[0:1] * lp[1:2], axis=-1, keepdims=True))
           - jnp.exp(jnp.sum(lp[2:3] * lp[3:4], axis=-1, keepdims=True)) + lam_init)

    def key_tile(i, delta):
        j = lax.rem(i + delta, nt)
        return j, pl.multiple_of(j * t, t)

    def qk(i, item):
        pair, mp, hf = item
        row0 = pl.multiple_of(i * t + hf * rb, rb)
        _, off = key_tile(i, grp * pair)
        return jnp.dot(q_sc[mp, pl.ds(row0, rb), :], kt_sc[mp * dq:(mp + 1) * dq, pl.ds(off, grp * t)],
                       preferred_element_type=F32)

    items = [(pair, mp, hf) for pair in range(nt // grp) for mp in range(2) for hf in range(t // rb)]
    for a in range(ahead):
        sc_buf[a] = qk(0, items[a])

    def tile_body(i, carry):
        pending = [sc_buf[a] for a in range(ahead)]
        for n, (pair, mp, hf) in enumerate(items):
            sc = pending.pop(0)
            if n + ahead < len(items):
                pending.append(qk(i, items[n + ahead]))
            else:
                pending.append(qk(jnp.minimum(i + 1, nt - 1), items[n + ahead - len(items)]))
            rows = slice(hf * rb, (hf + 1) * rb)
            scs = [sc[:, p * t:(p + 1) * t] for p in range(grp)]
            _, off = key_tile(i, grp * pair)
            shifts, row_max = [], None
            for piece, delta in enumerate(range(grp * pair, grp * pair + grp)):
                j, _ = key_tile(i, delta)
                if delta in (0, 1, nt - 1):
                    tile = 2 if delta == 0 else jnp.clip(j - i, -2, 2) + 2
                    scs[piece] = scs[piece] + bias_sc[tile, rows, :]
                    shifts.append(None)
                    rm = jnp.max(scs[piece], axis=-1, keepdims=True)
                else:
                    shifts.append(jnp.where(j > i, far_ref[head, 1], far_ref[head, 0]))
                    rm = jnp.max(scs[piece], axis=-1, keepdims=True) + shifts[-1]
                row_max = rm if row_max is None else jnp.maximum(row_max, rm)
            if pair == 0:
                m_new = jnp.broadcast_to(row_max, (rb, V7X_LANES))
            else:
                m_prev = m_ref[mp, rows, :]
                m_new = jnp.maximum(m_prev, row_max)
            ps = []
            for piece in range(grp):
                m_sub = m_new if shifts[piece] is None else m_new - shifts[piece]
                ps.append(jnp.exp2((scs[piece] - pltpu.repeat(m_sub, t // V7X_LANES, axis=1)).astype(BF16)))
            pv = jnp.dot(jnp.concatenate(ps, axis=1), vaug_sc[pl.ds(off, grp * t), :], preferred_element_type=F32)
            if pair == 0:
                acc_ref[mp, rows, :] = pv
            else:
                alpha = jnp.exp2(m_prev - m_new)
                acc_ref[mp, rows, :] = acc_ref[mp, rows, :] * pltpu.repeat(alpha, 2 * dv // V7X_LANES, axis=1) + pv
            m_ref[mp, rows, :] = m_new
        for a in range(ahead):
            sc_buf[a] = pending[a]
        o0 = acc_ref[0, :, :dv] / acc_ref[0, :, dv:dv + 1]
        o1 = acc_ref[1, :, :dv] / acc_ref[1, :, dv:dv + 1]
        o = o0 - lam * o1
        ms = jnp.mean(o * o, axis=-1, keepdims=True)
        row0 = pl.multiple_of(i * t, t)
        o_ref[0, pl.ds(row0, t), :] = (o * lax.rsqrt(ms + EPS) * g_ref[...] * (1.0 - lam_init)).astype(o_ref.dtype)
        return carry

    lax.fori_loop(0, nt, tile_body, 0)


def diff_attention(p3, bias_period, bias_far, lam_params, sub_g, *, t, rows_per_item, tiles_per_item, layer_number):
    b, s, _ = p3.shape
    dv = DIFF_V_DIM
    n_off = bias_period.shape[1]
    assert s // t >= 4, "tiles 2 .. s/t-2 steps away from the query tile must all be beyond REL_MAX_DIST"
    lam_init = 0.8 - 0.6 * math.exp(-0.3 * (layer_number - 1))
    sw = s + (tiles_per_item - 1) * t
    item = rows_per_item * tiles_per_item * t
    vmem = (2 * (s * 128 * 2 + 128 * s * 2 + s * dv * 2 + s * dv * 2) + n_off * t * t * 4 + sw * (2 * dv + 128) * 2
            + 2 * s * 128 * 2 + item * 4 + 2 * t * 128 * 4 * 3 + 8 * item * 4)
    return pl.pallas_call(
        functools.partial(_diff_attn_kernel, t=t, lam_init=lam_init),
        grid=(b, DIFF_HEADS),
        in_specs=[
            pl.BlockSpec(memory_space=pltpu.SMEM),
            pl.BlockSpec((1, s, 128), lambda bi, h: (bi, 0, OFF_DQ // 128 + h)),
            pl.BlockSpec((1, s, 128), lambda bi, h: (bi, 0, OFF_DK // 128 + h)),
            pl.BlockSpec((1, s, dv), lambda bi, h: (bi, 0, OFF_DV // dv + h)),
            pl.BlockSpec((1, n_off, 2 * t), lambda bi, h: (h, 0, 0)),
            pl.BlockSpec((4, DIFF_QK_DIM), lambda bi, h: (0, 0)),
            pl.BlockSpec((1, dv), lambda bi, h: (0, 0)),
        ],
        out_specs=pl.BlockSpec((1, s, dv), lambda bi, h: (bi, 0, h)),
        out_shape=jax.ShapeDtypeStruct((b, s, BRANCH_WIDTH), BF16),
        scratch_shapes=[pltpu.VMEM((2, s, DIFF_QK_DIM), BF16), pltpu.VMEM((2 * DIFF_QK_DIM, sw), BF16),
                        pltpu.VMEM((sw, 2 * dv), BF16),
                        pltpu.VMEM((n_off, t, t), F32), pltpu.VMEM((1, rows_per_item, tiles_per_item * t), F32),
                        pltpu.VMEM((2, t, V7X_LANES), F32), pltpu.VMEM((2, t, 2 * dv), F32)],
        compiler_params=_params(("parallel", "parallel"), vmem),
        name="diff_attn",
    )(bias_far, p3, p3, p3, bias_period, lam_params.astype(F32), sub_g.reshape(1, dv).astype(F32))


def _merge_kernel(ya_ref, yb_ref, yc_ref, yd_ref, g_ref, wg_ref, wb_ref, wo_ref, x_ref, o_ref, wb_sc, wo_sc):
    dm = x_ref.shape[1]

    @pl.when(pl.program_id(0) == 0)
    def _():
        for n in range(wb_ref.shape[0]):
            wb_sc[n] = wb_ref[n].astype(BF16)
        wo_sc[...] = wo_ref[...].astype(BF16)

    tm = x_ref.shape[0]
    halves = [slice(0, tm // 2), slice(tm // 2, tm)]
    y_refs = (ya_ref, yb_ref, yc_ref, yd_ref)
    gates, branch = [], []
    for rows in halves:
        x = x_ref[rows, :]
        ms = jnp.mean(x * x, axis=-1, keepdims=True)
        u = (x * lax.rsqrt(ms + EPS) * g_ref[...]).astype(BF16)
        gates.append([jnp.dot(u, wg_ref[:, n * dm:(n + 1) * dm], preferred_element_type=F32)
                      for n in range(len(y_refs))])
        branch.append([jnp.dot(y_ref[rows, :], wb_sc[n], preferred_element_type=F32)
                       for n, y_ref in enumerate(y_refs)])
    for rows, gts, brs in zip(halves, gates, branch):
        merged = None
        for gt, br in zip(gts, brs):
            term = jax.nn.sigmoid(gt) * br
            merged = term if merged is None else merged + term
        o_ref[rows, :] = x_ref[rows, :] + jnp.dot(merged.astype(BF16), wo_sc[...], preferred_element_type=F32)


def merge_branches(ys, norm_g, w_gates, w_branch, w_out, x2, *, tm):
    m, dm = x2.shape
    w = BRANCH_WIDTH
    y_spec = pl.BlockSpec((tm, w), lambda i: (i, 0))
    vmem = (2 * (4 * tm * w * 2 + 2 * tm * dm * 4) + N_BRANCHES * dm * dm * 2 + (4 * w * dm + dm * dm) * (4 + 2)
            + 12 * tm * dm * 4)
    return pl.pallas_call(
        _merge_kernel,
        grid=(m // tm,),
        in_specs=[
            y_spec, y_spec, y_spec, y_spec,
            pl.BlockSpec((1, dm), lambda i: (0, 0)),
            pl.BlockSpec((dm, N_BRANCHES * dm), lambda i: (0, 0), pipeline_mode=pl.Buffered(1)),
            pl.BlockSpec((N_BRANCHES, w, dm), lambda i: (0, 0, 0), pipeline_mode=pl.Buffered(1)),
            pl.BlockSpec((dm, dm), lambda i: (0, 0), pipeline_mode=pl.Buffered(1)),
            pl.BlockSpec((tm, dm), lambda i: (i, 0)),
        ],
        out_specs=pl.BlockSpec((tm, dm), lambda i: (i, 0)),
        out_shape=jax.ShapeDtypeStruct((m, dm), F32),
        scratch_shapes=[pltpu.VMEM((N_BRANCHES, w, dm), BF16), pltpu.VMEM((dm, dm), BF16)],
        compiler_params=_params(("arbitrary",), vmem),
        name="merge",
    )(*ys, norm_g.reshape(1, dm).astype(F32), w_gates, w_branch.astype(F32), w_out.astype(F32), x2)


HALO_ROWS = 16


def _conv_ffn_kernel(x_ref, xp_ref, xn_ref, g_ref, wa_ref, wl_ref, cw_ref, cb_ref, wd_ref, gf_ref, o_ref,
                     xe_sc, wd_sc, *, tiles_per_seq, col_chunk, final_norm):
    tm = o_ref.shape[0]
    f = wa_ref.shape[1]
    h = HALO_ROWS
    pos = pl.program_id(0) % tiles_per_seq

    @pl.when(pl.program_id(0) == 0)
    def _():
        wd_sc[...] = wd_ref[...].astype(BF16)

    def norm(x):
        ms = jnp.mean(x * x, axis=-1, keepdims=True)
        return x * lax.rsqrt(ms + EPS) * g_ref[...]

    xe_sc[:h] = jnp.where(pos == 0, 0.0, norm(xp_ref[...])).astype(BF16)
    xe_sc[h:h + tm] = norm(x_ref[...]).astype(BF16)
    xe_sc[h + tm:] = jnp.where(pos == tiles_per_seq - 1, 0.0, norm(xn_ref[...])).astype(BF16)
    k0 = math.sqrt(2.0 / math.pi)
    bounds = list(range(0, f, col_chunk)) + [f]
    chunks = [slice(lo, hi) for lo, hi in zip(bounds[:-1], bounds[1:])]

    def project(cols):
        a_ext = jnp.dot(xe_sc[...], wa_ref[:, cols], preferred_element_type=F32)
        lin = jnp.dot(xe_sc[h:h + tm, :], wl_ref[:, cols], preferred_element_type=F32)
        return a_ext, lin

    y = x_ref[...]
    nxt = project(chunks[0])
    for n, cols in enumerate(chunks):
        a_ext, lin = nxt
        if n + 1 < len(chunks):
            nxt = project(chunks[n + 1])
        cw = cw_ref[:, cols]
        c = (a_ext[h - 1:h - 1 + tm] * cw[0:1] + a_ext[h:h + tm] * cw[1:2] + a_ext[h + 1:h + 1 + tm] * cw[2:3]
             + cb_ref[:, cols])
        t = jnp.tanh(c * ((c * c) * (k0 * 0.044715) + k0))
        hmid = ((c + c * t) * lin).astype(BF16)
        y = y + jnp.dot(hmid, wd_sc[cols, :], preferred_element_type=F32)
    if final_norm:
        ms = jnp.mean(y * y, axis=-1, keepdims=True)
        y = y * lax.rsqrt(ms + EPS) * gf_ref[...]
    o_ref[...] = y


def conv_ffn(x2, g, w_up, conv_w, conv_b, w_down, final_g, *, s, tm, final_norm):
    m, dm = x2.shape
    f = w_up.shape[1] // 2
    tiles_per_seq = s // tm
    hb = tm // HALO_ROWS
    n_halo = m // HALO_ROWS
    vmem = (2 * 2 * tm * dm * 4 + 2 * dm * f * 2 + f * dm * (4 + 2) + (tm + 2 * HALO_ROWS) * dm * 2
            + 12 * tm * 512 * 4 + 2 * tm * dm * 4)
    return pl.pallas_call(
        functools.partial(_conv_ffn_kernel, tiles_per_seq=tiles_per_seq, col_chunk=1024, final_norm=final_norm),
        grid=(m // tm,),
        in_specs=[
            pl.BlockSpec((tm, dm), lambda i: (i, 0)),
            pl.BlockSpec((HALO_ROWS, dm), lambda i: (jnp.maximum(i * hb - 1, 0), 0)),
            pl.BlockSpec((HALO_ROWS, dm), lambda i: (jnp.minimum((i + 1) * hb, n_halo - 1), 0)),
            pl.BlockSpec((1, dm), lambda i: (0, 0)),
            pl.BlockSpec((dm, f), lambda i: (0, 0), pipeline_mode=pl.Buffered(1)),
            pl.BlockSpec((dm, f), lambda i: (0, 1), pipeline_mode=pl.Buffered(1)),
            pl.BlockSpec((CONV_W, f), lambda i: (0, 0)),
            pl.BlockSpec((1, f), lambda i: (0, 0)),
            pl.BlockSpec((f, dm), lambda i: (0, 0), pipeline_mode=pl.Buffered(1)),
            pl.BlockSpec((1, dm), lambda i: (0, 0)),
        ],
        out_specs=pl.BlockSpec((tm, dm), lambda i: (i, 0)),
        out_shape=jax.ShapeDtypeStruct((m, dm), F32),
        scratch_shapes=[pltpu.VMEM((tm + 2 * HALO_ROWS, dm), BF16), pltpu.VMEM((f, dm), BF16)],
        compiler_params=_params(("arbitrary",), vmem),
        name="conv_ffn",
    )(x2, x2, x2, g.reshape(1, dm).astype(F32), w_up, w_up, conv_w.astype(F32), conv_b.reshape(1, f).astype(F32),
      w_down.astype(F32), final_g.reshape(1, dm).astype(F32))


def _arrange_w_in(w):
    widths = (512, 512, 128, 128, 512, 512, 512, 512, 16, 512, 512, 512, 4096)
    offs = [0]
    for wd in widths:
        offs.append(offs[-1] + wd)
    seg = lambda n: w[:, offs[n]:offs[n + 1]]
    a, bq, bk, bv, cq, ck, cv, co, cgate, dq, dk, dv, gates = (seg(n) for n in range(len(widths)))
    main = jnp.concatenate([a, bq, cq, ck, cv, co, dq, dk, dv, bk, bv], axis=1).astype(BF16)
    gate = jnp.pad(cgate, ((0, 0), (0, V7X_LANES - cgate.shape[1]))).astype(BF16)
    return main, gate, gates.astype(BF16)


def kernel(x, norm_mix_g, w_in, mlstm_gate_bias, qk_norm_g, mlstm_norm_g, diff_lambda, diff_norm_g, rel_bias,
           w_branch, w_out, norm_ffn_g, w_up, conv_w, conv_b, w_down, final_norm_g):
    b, s, dm = x.shape
    depth = w_in.shape[0]
    m = b * s
    d_ff = w_down.shape[1]
    L = MLSTM_CHUNK
    t_diff = 512

    bd, dft_hi, dft_lo = fourier_tables(s)
    cos2, sin2 = rope_tables(s)
    bias_tiles, bias_far = diff_bias_tiles(rel_bias, t_diff)

    x2 = x.reshape(m, dm)
    for layer in range(depth):
        w_main, w_gate, w_branch_gates = _arrange_w_in(w_in[layer])
        p2, cgate = norm_matmul(x2, norm_mix_g[layer], w_main, w_gate, tm=1024, tn=P_WIDTH)
        p3 = p2.reshape(b, s, P_WIDTH)

        y_a = fourier_mix(p3, bd, dft_hi, dft_lo, tm=512)

        q_b, k_b, v_b = gqa_prep(p2, qk_norm_g[layer], cos2, sin2, s=s, tm=512)
        kt_b = jnp.swapaxes(k_b.reshape(b, s, GQA_KV_HEADS * HEAD_DIM), 1, 2)
        y_b = gqa_attention(q_b.reshape(b, s, -1), kt_b, v_b.reshape(b, s, -1), tq=512, tk=1024)

        gates5 = jnp.transpose(cgate[:, :4 * MLSTM_HEADS].reshape(b, s, 4, MLSTM_HEADS), (0, 2, 3, 1))
        gates5 = gates5.reshape(b, 4, MLSTM_HEADS, s // L, L)
        y_c = mlstm_branch(p3, gates5, mlstm_gate_bias[layer], mlstm_norm_g[layer], heads_per_block=2)

        y_d = diff_attention(p3, bias_tiles, bias_far, diff_lambda[layer], diff_norm_g[layer],
                             t=t_diff, rows_per_item=256, tiles_per_item=4, layer_number=layer + 1)

        ys = [y.reshape(m, BRANCH_WIDTH) for y in (y_a, y_b, y_c, y_d)]
        x2 = merge_branches(ys, norm_mix_g[layer], w_branch_gates, w_branch[layer], w_out[layer], x2, tm=512)

        half_lin = jnp.concatenate([jnp.ones((d_ff,), F32), jnp.full((d_ff,), 0.5, F32)])
        x2 = conv_ffn(x2, norm_ffn_g[layer], (w_up[layer] * half_lin).astype(BF16), conv_w[layer], conv_b[layer],
                      w_down[layer], final_norm_g, s=s, tm=512, final_norm=(layer == depth - 1))
    return x2.reshape(b, s, dm)
```

```python
import functools
import math

import jax
import jax.numpy as jnp
from jax import lax
from jax.experimental import pallas as pl
from jax.experimental.pallas import tpu as pltpu

F32 = jnp.float32
BF16 = jnp.bfloat16

GRID_W = 64
HEAD_DIM = 64
BRANCH_WIDTH = 512
N_BRANCHES = 4
FOURIER_GROUP_DIM = 64
GQA_Q_HEADS = 8
GQA_KV_HEADS = 2
MLSTM_HEADS = 4
MLSTM_HEAD_DIM = 128
MLSTM_CHUNK = 128
DIFF_HEADS = 4
DIFF_QK_DIM = 64
DIFF_V_DIM = 128
REL_BUCKETS = 32
REL_MAX_DIST = 128
CONV_W = 3
ROPE_BASE = 10000.0
EPS = 1e-6
LOG2E = math.log2(math.e)

V7X_LANES = 128
V7X_VMEM_BYTES = 64 * 1024 * 1024
V7X_VMEM_CAP = V7X_VMEM_BYTES - 8 * 1024 * 1024

TILES = dict(
    proj_rows=512,
    fourier_rows=512,
    gqa_prep_rows=512,
    gqa_q_rows=512, gqa_keys=1024,
    diff_tile=512, diff_q_rows=256, diff_key_tiles=4,
    mlstm_heads=2,
    merge_rows=512,
    ffn_rows=512, ffn_cols=1024,
)
TRANSPOSE_ROWS = 512

OFF_A = 0
OFF_BQ = 512
OFF_CQ = 1024
OFF_CK = 1536
OFF_CV = 2048
OFF_CO = 2560
OFF_DQ = 3072
OFF_DK = 3584
OFF_DV = 4096
OFF_BKV = 4608
P_WIDTH = 4864


def _params(sem, vmem_bytes):
    limit = int(min(max(vmem_bytes * 3 // 2 + (4 << 20), 16 << 20), V7X_VMEM_CAP))
    return pltpu.CompilerParams(dimension_semantics=sem, vmem_limit_bytes=limit)


def _norm_mm_kernel(x_ref, g_ref, w_ref, o_ref, xn_ref):
    @pl.when(pl.program_id(1) == 0)
    def _():
        x = x_ref[...]
        ms = jnp.mean(x * x, axis=-1, keepdims=True)
        xn_ref[...] = (x * lax.rsqrt(ms + EPS) * g_ref[...]).astype(BF16)

    o_ref[...] = jnp.dot(xn_ref[...], w_ref[...], preferred_element_type=F32).astype(o_ref.dtype)


def _norm_mm_gate_kernel(x_ref, g_ref, w_ref, wg_ref, o_ref, og_ref, xn_ref):
    @pl.when(pl.program_id(1) == 0)
    def _():
        x = x_ref[...]
        ms = jnp.mean(x * x, axis=-1, keepdims=True)
        xn = (x * lax.rsqrt(ms + EPS) * g_ref[...]).astype(BF16)
        xn_ref[...] = xn
        og_ref[...] = jnp.dot(xn, wg_ref[...], preferred_element_type=F32)

    o_ref[...] = jnp.dot(xn_ref[...], w_ref[...], preferred_element_type=F32).astype(o_ref.dtype)


def norm_matmul(x, g, w, w_gate=None, *, tm, tn):
    m, k = x.shape
    n = w.shape[1]
    grid = (m // tm, n // tn)
    vmem = 2 * tm * k * 4 + tm * k * 2 + 2 * k * tn * 2 + 2 * tm * tn * 2 + 4 * tm * k
    x_spec = pl.BlockSpec((tm, k), lambda i, j: (i, 0))
    g_spec = pl.BlockSpec((1, k), lambda i, j: (0, 0))
    w_spec = pl.BlockSpec((k, tn), lambda i, j: (0, j))
    o_spec = pl.BlockSpec((tm, tn), lambda i, j: (i, j))
    scratch = [pltpu.VMEM((tm, k), BF16)]
    g2 = g.reshape(1, k).astype(F32)
    if w_gate is None:
        return pl.pallas_call(
            _norm_mm_kernel,
            grid=grid,
            in_specs=[x_spec, g_spec, w_spec],
            out_specs=o_spec,
            out_shape=jax.ShapeDtypeStruct((m, n), BF16),
            scratch_shapes=scratch,
            compiler_params=_params(("parallel", "arbitrary"), vmem),
            name="norm_matmul",
        )(x, g2, w)
    ng = w_gate.shape[1]
    return pl.pallas_call(
        _norm_mm_gate_kernel,
        grid=grid,
        in_specs=[x_spec, g_spec, w_spec, pl.BlockSpec((k, ng), lambda i, j: (0, 0))],
        out_specs=[o_spec, pl.BlockSpec((tm, ng), lambda i, j: (i, 0))],
        out_shape=[jax.ShapeDtypeStruct((m, n), BF16), jax.ShapeDtypeStruct((m, ng), F32)],
        scratch_shapes=scratch,
        compiler_params=_params(("parallel", "arbitrary"), vmem),
        name="norm_matmul_gate",
    )(x, g2, w, w_gate)


FOURIER_HALO = 16


def _fourier_kernel(a_ref, bd_ref, hi_ref, lo_ref, flip_ref, o_ref, z_ref, ext_ref, *, tm, row_chunk):
    s = a_ref.shape[1]
    w = a_ref.shape[2]
    n_lo = lo_ref.shape[1]
    i = pl.program_id(1)

    @pl.when(i == 0)
    def _():
        for r in range(0, s, row_chunk):
            a = a_ref[0, r:r + row_chunk, :]
            zc = jnp.dot(a, bd_ref[...], preferred_element_type=F32)
            z_ref[r:r + row_chunk, :] = zc[:, :w].astype(BF16)
            z_ref[s + r:s + r + row_chunk, :] = zc[:, w:].astype(BF16)

    cl, sl = lo_ref[0], lo_ref[1]
    for r in range(tm // n_lo + 1):
        rows = n_lo if r < tm // n_lo else ext_ref.shape[0] - tm
        k1 = i * (tm // n_lo) + r
        ch = hi_ref[0, pl.ds(k1, 1), :]
        nsh = hi_ref[1, pl.ds(k1, 1), :]
        ext_ref[r * n_lo:r * n_lo + rows, :s] = (cl[:rows] * ch + sl[:rows] * nsh).astype(BF16)
        ext_ref[r * n_lo:r * n_lo + rows, s:] = (cl[:rows] * nsh - sl[:rows] * ch).astype(BF16)
    p = jnp.dot(ext_ref[:, :s], z_ref[:s], preferred_element_type=F32)
    q = jnp.dot(ext_ref[:, s:], z_ref[s:], preferred_element_type=F32)
    lo = pl.multiple_of(i * tm, tm)
    o_ref[0, pl.ds(lo, tm), :] = (p[:tm] + q[:tm]).astype(o_ref.dtype)
    mirrored = (p[1:tm + 1] - q[1:tm + 1]).astype(BF16)
    hi = pl.multiple_of(s - (i + 1) * tm, tm)
    o_ref[0, pl.ds(hi, tm), :] = jnp.dot(flip_ref[...], mirrored, preferred_element_type=F32).astype(o_ref.dtype)


def fourier_tables(s):
    cg = FOURIER_GROUP_DIM
    jj = jnp.arange(cg, dtype=jnp.int32)
    ang_c = (2.0 * math.pi / cg) * ((jj[:, None] * jj[None, :]) % cg).astype(F32)
    eye_g = jnp.eye(BRANCH_WIDTH // cg, dtype=F32)
    bd_c = jnp.kron(eye_g, jnp.cos(ang_c)) * cg ** -0.5
    bd_s = jnp.kron(eye_g, jnp.sin(ang_c)) * cg ** -0.5
    bd = jnp.concatenate([bd_c, bd_s], axis=1).astype(BF16)
    n_lo = s // cg
    nn = jnp.arange(s, dtype=jnp.int32)
    k1 = jnp.arange(cg // 2 + 2, dtype=jnp.int32)
    ang_hi = (2.0 * math.pi / cg) * ((k1[:, None] * nn[None, :]) % cg).astype(F32)
    ll = jnp.arange(n_lo, dtype=jnp.int32)
    ang_lo = (2.0 * math.pi / s) * ((ll[:, None] * nn[None, :]) % s).astype(F32)
    scale = s ** -0.5
    hi = jnp.stack([jnp.cos(ang_hi), -jnp.sin(ang_hi)]) * scale
    lo = jnp.stack([jnp.cos(ang_lo), jnp.sin(ang_lo)])
    return bd, hi, lo


def fourier_mix(p3, bd, hi, lo, *, tm):
    b, s, _ = p3.shape
    w = BRANCH_WIDTH
    n_lo = lo.shape[1]
    h = FOURIER_HALO
    assert (s // 2) % tm == 0 and tm % n_lo == 0 and h <= n_lo
    r = jnp.arange(tm, dtype=jnp.int32)
    flip = (r[:, None] + r[None, :] == tm - 1).astype(BF16)
    vmem = (s * w * 2 + 2 * w * w * 2 + 2 * (hi.size + lo.size) * 4 + (tm + h) * 2 * s * 2 + 2 * s * w * 2
            + 2 * s * w * 2 + 8 * (tm + h) * w * 4 + 8 * n_lo * s * 4)
    return pl.pallas_call(
        functools.partial(_fourier_kernel, tm=tm, row_chunk=min(s, tm)),
        grid=(b, s // 2 // tm),
        in_specs=[
            pl.BlockSpec((1, s, w), lambda bi, i: (bi, 0, OFF_A // w), pipeline_mode=pl.Buffered(1)),
            pl.BlockSpec((w, 2 * w), lambda bi, i: (0, 0), pipeline_mode=pl.Buffered(1)),
            pl.BlockSpec(hi.shape, lambda bi, i: (0, 0, 0)),
            pl.BlockSpec(lo.shape, lambda bi, i: (0, 0, 0)),
            pl.BlockSpec((tm, tm), lambda bi, i: (0, 0)),
        ],
        out_specs=pl.BlockSpec((1, s, w), lambda bi, i: (bi, 0, 0)),
        out_shape=jax.ShapeDtypeStruct((b, s, w), BF16),
        scratch_shapes=[pltpu.VMEM((2 * s, w), BF16), pltpu.VMEM((tm + h, 2 * s), BF16)],
        compiler_params=_params(("parallel", "arbitrary"), vmem),
        name="fourier",
    )(p3, bd, hi, lo, flip)


def rope_tables(s):
    rows = s // GRID_W
    row_id = jnp.repeat(jnp.arange(rows, dtype=F32), GRID_W)
    col_id = jnp.tile(jnp.arange(GRID_W, dtype=F32), rows)
    n_pairs = HEAD_DIM // 4
    inv_freq = ROPE_BASE ** (-jnp.arange(n_pairs, dtype=F32) / n_pairs)
    ang = jnp.concatenate([row_id[:, None] * inv_freq, col_id[:, None] * inv_freq], axis=-1)
    cos, sin = jnp.cos(ang), jnp.sin(ang)
    return jnp.concatenate([cos, cos] * 2, axis=-1), jnp.concatenate([-sin, sin] * 2, axis=-1)


def _norm_rope(x, g, seg, cos2, sin2):
    half = HEAD_DIM // 2
    x2 = x * x
    hi = x2.astype(BF16)
    lo = (x2 - hi.astype(F32)).astype(BF16)
    ms = jnp.dot(hi, seg, preferred_element_type=F32) + jnp.dot(lo, seg, preferred_element_type=F32)
    y = x * lax.rsqrt(ms + EPS) * g
    lane = lax.broadcasted_iota(jnp.int32, (x.shape[0], V7X_LANES), 1)
    first_half = (lane % HEAD_DIM) < half
    outs = []
    for cb in range(x.shape[1] // V7X_LANES):
        yb = y[:, cb * V7X_LANES:(cb + 1) * V7X_LANES]
        rot = jnp.where(first_half, pltpu.roll(yb, V7X_LANES - half, axis=1), pltpu.roll(yb, half, axis=1))
        outs.append(yb * cos2 + rot * sin2)
    return outs[0] if len(outs) == 1 else jnp.concatenate(outs, axis=-1)


def _gqa_prep_kernel(q_ref, kv_ref, gq_ref, gk_ref, seg_ref, cos_ref, sin_ref, qo_ref, ko_ref, vo_ref):
    d = HEAD_DIM
    nk = GQA_KV_HEADS * d
    cos2, sin2 = cos_ref[...], sin_ref[...]
    q = q_ref[...].astype(F32)
    kv = kv_ref[...].astype(F32)
    qo_ref[...] = _norm_rope(q, gq_ref[...], seg_ref[...], cos2, sin2).astype(BF16)
    ko_ref[...] = _norm_rope(kv[:, :nk], gk_ref[...], seg_ref[:nk, :nk], cos2, sin2).astype(BF16)
    v = kv[:, nk:]
    lane = lax.broadcasted_iota(jnp.int32, v.shape, 1)
    ones_col = jnp.where(lane == d, 1.0, 0.0)
    vo_ref[:, :nk] = jnp.where(lane < d, v, ones_col).astype(BF16)
    vo_ref[:, nk:] = jnp.where(lane < d, pltpu.roll(v, d, axis=1), ones_col).astype(BF16)


def gqa_prep(p2, qk_g, cos2, sin2, *, s, tm):
    m = p2.shape[0]
    nq = GQA_Q_HEADS * HEAD_DIM
    nkv = GQA_KV_HEADS * HEAD_DIM
    assert nkv == V7X_LANES
    tiles_per_seq = s // tm
    q_scale = HEAD_DIM ** -0.5 * LOG2E
    gq = jnp.tile(qk_g[0].astype(F32) * q_scale, GQA_Q_HEADS).reshape(1, nq)
    gk = jnp.tile(qk_g[1].astype(F32), GQA_KV_HEADS).reshape(1, nkv)
    seg = jnp.kron(jnp.eye(GQA_Q_HEADS, dtype=F32), jnp.full((HEAD_DIM, HEAD_DIM), 1.0 / HEAD_DIM, F32)).astype(BF16)
    return pl.pallas_call(
        _gqa_prep_kernel,
        grid=(m // tm,),
        in_specs=[
            pl.BlockSpec((tm, nq), lambda i: (i, OFF_BQ // nq)),
            pl.BlockSpec((tm, 2 * nkv), lambda i: (i, OFF_BKV // (2 * nkv))),
            pl.BlockSpec((1, nq), lambda i: (0, 0)),
            pl.BlockSpec((1, nkv), lambda i: (0, 0)),
            pl.BlockSpec((nq, nq), lambda i: (0, 0)),
            pl.BlockSpec((tm, V7X_LANES), lambda i: (i % tiles_per_seq, 0)),
            pl.BlockSpec((tm, V7X_LANES), lambda i: (i % tiles_per_seq, 0)),
        ],
        out_specs=[
            pl.BlockSpec((tm, nq), lambda i: (i, 0)),
            pl.BlockSpec((tm, nkv), lambda i: (i, 0)),
            pl.BlockSpec((tm, 2 * nkv), lambda i: (i, 0)),
        ],
        out_shape=[
            jax.ShapeDtypeStruct((m, nq), BF16),
            jax.ShapeDtypeStruct((m, nkv), BF16),
            jax.ShapeDtypeStruct((m, 2 * nkv), BF16),
        ],
        compiler_params=_params(("parallel",), 16 * tm * nq * 4),
        name="gqa_prep",
    )(p2, p2, gq, gk, seg, cos2, sin2)


def _gqa_attn_kernel(q_ref, kt_ref, v_ref, o_ref, q_sc, m_ref, acc_ref, *, tk):
    d = HEAD_DIM
    tq = q_ref.shape[1]
    grp = q_ref.shape[2] // d
    s = kt_ref.shape[2]
    for g in range(grp):
        q_sc[g * tq:(g + 1) * tq, :] = q_ref[0, :, g * d:(g + 1) * d]

    def qk(item):
        c, g = item
        return jnp.dot(q_sc[g * tq:(g + 1) * tq, :], kt_ref[0, :, c * tk:(c + 1) * tk],
                       preferred_element_type=F32)

    items = [(c, g) for c in range(s // tk) for g in range(grp)]
    sc_next = qk(items[0])
    for n, (c, g) in enumerate(items):
        rows = slice(g * tq, (g + 1) * tq)
        sc = sc_next
        if n + 1 < len(items):
            sc_next = qk(items[n + 1])
        v = v_ref[0, c * tk:(c + 1) * tk, :]
        row_max = jnp.max(sc, axis=-1, keepdims=True)
        if c == 0:
            m_new = jnp.broadcast_to(row_max, (tq, V7X_LANES))
        else:
            m_prev = m_ref[rows, :]
            m_new = jnp.maximum(m_prev, row_max)
        p = jnp.exp2((sc - pltpu.repeat(m_new, tk // V7X_LANES, axis=1)).astype(BF16))
        pv = jnp.dot(p, v, preferred_element_type=F32)
        if c == 0:
            acc_ref[rows, :] = pv
        else:
            acc_ref[rows, :] = acc_ref[rows, :] * jnp.exp2(m_prev - m_new) + pv
        m_ref[rows, :] = m_new
    for g in range(grp):
        acc = acc_ref[g * tq:(g + 1) * tq, :]
        o_ref[0, :, g * d:(g + 1) * d] = (acc[:, :d] / acc[:, d:d + 1]).astype(o_ref.dtype)


def gqa_attention(q3, kt3, v3, *, tq, tk):
    b, s, nq = q3.shape
    d = HEAD_DIM
    grp = GQA_Q_HEADS // GQA_KV_HEADS
    mrows = grp * tq
    vmem = (2 * (tq * grp * d * 2 + d * s * 2 + s * 128 * 2 + tq * grp * d * 2) + 3 * mrows * 128 * 4
            + 12 * tq * tk * 4)
    return pl.pallas_call(
        functools.partial(_gqa_attn_kernel, tk=tk),
        grid=(b, GQA_KV_HEADS, s // tq),
        in_specs=[
            pl.BlockSpec((1, tq, grp * d), lambda bi, kv, i: (bi, i, kv)),
            pl.BlockSpec((1, d, s), lambda bi, kv, i: (bi, kv, 0)),
            pl.BlockSpec((1, s, 2 * d), lambda bi, kv, i: (bi, 0, kv)),
        ],
        out_specs=pl.BlockSpec((1, tq, grp * d), lambda bi, kv, i: (bi, i, kv)),
        out_shape=jax.ShapeDtypeStruct((b, s, nq), BF16),
        scratch_shapes=[pltpu.VMEM((mrows, d), BF16), pltpu.VMEM((mrows, V7X_LANES), F32),
                        pltpu.VMEM((mrows, 2 * d), F32)],
        compiler_params=_params(("parallel", "parallel", "parallel"), vmem),
        name="gqa_attn",
    )(q3, kt3, v3)


def _mlstm_step(chains, ms, q_ref, kt_sc, v_ref, r_sc, cm_sc, b_sc, st_sc, h_sc):
    L = MLSTM_CHUNK
    dh = MLSTM_HEAD_DIM
    assert L == dh
    scale = dh ** -0.5
    row_i = lax.broadcasted_iota(jnp.int32, (L, L), 0)
    col_i = lax.broadcasted_iota(jnp.int32, (L, L), 1)

    def col(x_row):
        return jnp.transpose(jnp.broadcast_to(x_row, (L, L)))

    pre = []
    for (hh, direction, c), m in zip(chains, ms):
        off = pl.multiple_of(c * L, L)
        lanes = slice(hh * dh, (hh + 1) * dh)
        q = q_ref[0, pl.ds(off, L), lanes]
        kt = kt_sc[hh, :, pl.ds(off, L)]
        v = v_ref[0, pl.ds(off, L), lanes]
        r_row = r_sc[hh, direction, pl.ds(c, 1), :]
        cm_row = cm_sc[hh, direction, pl.ds(c, 1), :]
        b_row = b_sc[hh, direction, pl.ds(c, 1), :]
        rmax = jnp.max(r_row, axis=-1, keepdims=True)
        btot = b_row[:, L - 1:L] if direction == 0 else b_row[:, 0:1]
        cmat = jnp.maximum(m, col(cm_row))
        mask = (row_i >= col_i) if direction == 0 else (row_i <= col_i)
        c_last = jnp.maximum(m, rmax)
        w_state = jnp.exp(r_row - c_last) * scale
        pre.append(dict(
            off=off, lanes=lanes, q=q, kt=kt,
            v_aug=jnp.concatenate([v, jnp.ones((L, dh), BF16)], axis=-1),
            w_intra=jnp.where(mask, jnp.exp(r_row - cmat), 0.0) * scale,
            w_inter=jnp.exp(m - cmat),
            den_floor=jnp.exp(-(col(b_row) + cmat)),
            decay=jnp.exp(m - c_last),
            kt_w=(kt.astype(F32) * w_state).astype(BF16),
            m_new=btot + c_last,
        ))
    s_raw = [jnp.dot(p["q"], p["kt"], preferred_element_type=F32) for p in pre]
    states = [st_sc[hh, direction] for hh, direction, _ in chains]
    inter = [jnp.dot(p["q"], st.astype(BF16), preferred_element_type=F32) for p, st in zip(pre, states)]
    upd = [jnp.dot(p["kt_w"], p["v_aug"], preferred_element_type=F32) for p in pre]
    for n, ((hh, direction, _), p) in enumerate(zip(chains, pre)):
        st_sc[hh, direction] = p["decay"] * states[n] + upd[n]
        intra = jnp.dot((s_raw[n] * p["w_intra"]).astype(BF16), p["v_aug"], preferred_element_type=F32)
        h_aug = jnp.concatenate([p["w_inter"], p["w_inter"]], axis=-1) * inter[n] + intra
        h_sc[hh, direction, pl.ds(p["off"], L), :] = h_aug[:, :dh] / jnp.maximum(jnp.abs(h_aug[:, dh:]), p["den_floor"])
    return [p["m_new"] for p in pre]


def _mlstm_kernel(bias_ref, q_ref, k_ref, v_ref, o_ref, gate_ref, g_ref, y_ref,
                  r_sc, cm_sc, b_sc, h_sc, st_sc, kt_sc):
    L = MLSTM_CHUNK
    dh = MLSTM_HEAD_DIM
    hpb = q_ref.shape[2] // dh
    head0 = pl.program_id(1) * hpb
    nc = q_ref.shape[1] // L
    lane = lax.broadcasted_iota(jnp.int32, (nc, L), 1)
    shifts = [1 << t for t in range(int(math.log2(L)))]
    for hh in range(hpb):
        for d in range(2):
            i_pre = gate_ref[0, 2 * d, hh] + bias_ref[2 * d, head0 + hh]
            f_pre = gate_ref[0, 2 * d + 1, hh] + bias_ref[2 * d + 1, head0 + hh]
            logf = jnp.minimum(f_pre, 0.0) - jnp.log1p(jnp.exp(-jnp.abs(f_pre)))
            bc = logf
            for sh in shifts:
                if d == 0:
                    bc = bc + jnp.where(lane >= sh, pltpu.roll(bc, sh, axis=1), 0.0)
                else:
                    bc = bc + jnp.where(lane < L - sh, pltpu.roll(bc, L - sh, axis=1), 0.0)
            r = i_pre - bc
            cm = r
            for sh in shifts:
                if d == 0:
                    cm = jnp.maximum(cm, jnp.where(lane >= sh, pltpu.roll(cm, sh, axis=1), -jnp.inf))
                else:
                    cm = jnp.maximum(cm, jnp.where(lane < L - sh, pltpu.roll(cm, L - sh, axis=1), -jnp.inf))
            r_sc[hh, d] = r
            cm_sc[hh, d] = cm
            b_sc[hh, d] = bc
    st_sc[...] = jnp.zeros(st_sc.shape, F32)
    tr = TRANSPOSE_ROWS
    for hh in range(hpb):
        for r in range(0, q_ref.shape[1], tr):
            kt_sc[hh, :, r:r + tr] = jnp.transpose(k_ref[0, r:r + tr, hh * dh:(hh + 1) * dh].astype(F32)).astype(BF16)

    def body(c, ms):
        chains = [(hh, d, c if d == 0 else nc - 1 - c) for hh in range(hpb) for d in range(2)]
        return tuple(_mlstm_step(chains, ms, q_ref, kt_sc, v_ref, r_sc, cm_sc, b_sc, st_sc, h_sc))

    lax.fori_loop(0, nc, body, tuple(jnp.zeros((1, 1), F32) for _ in range(2 * hpb)))
    for hh in range(hpb):
        lanes = slice(hh * dh, (hh + 1) * dh)
        hsum = h_sc[hh, 0] + h_sc[hh, 1]
        ms = jnp.mean(hsum * hsum, axis=-1, keepdims=True)
        y = hsum * lax.rsqrt(ms + EPS) * g_ref[:, lanes]
        y_ref[0, :, lanes] = (jax.nn.sigmoid(o_ref[0, :, lanes].astype(F32)) * y).astype(y_ref.dtype)


def mlstm_branch(p3, gates5, gate_bias, norm_g, *, heads_per_block):
    b, s, _ = p3.shape
    L = MLSTM_CHUNK
    hpb = heads_per_block
    wb = hpb * MLSTM_HEAD_DIM
    nc = s // L
    blk = lambda off: pl.BlockSpec((1, s, wb), lambda bi, h, off=off: (bi, 0, off // wb + h))
    vmem = 2 * 5 * s * wb * 2 + 2 * s * wb * 4 + 6 * hpb * nc * L * 4 + 4 * hpb * wb * wb * 4 + 3 * s * wb * 4
    return pl.pallas_call(
        _mlstm_kernel,
        grid=(b, MLSTM_HEADS // hpb),
        in_specs=[
            pl.BlockSpec(memory_space=pltpu.SMEM),
            blk(OFF_CQ), blk(OFF_CK), blk(OFF_CV), blk(OFF_CO),
            pl.BlockSpec((1, 4, hpb, nc, L), lambda bi, h: (bi, 0, h, 0, 0)),
            pl.BlockSpec((1, wb), lambda bi, h: (0, h)),
        ],
        out_specs=pl.BlockSpec((1, s, wb), lambda bi, h: (bi, 0, h)),
        out_shape=jax.ShapeDtypeStruct((b, s, BRANCH_WIDTH), BF16),
        scratch_shapes=[
            pltpu.VMEM((hpb, 2, nc, L), F32), pltpu.VMEM((hpb, 2, nc, L), F32), pltpu.VMEM((hpb, 2, nc, L), F32),
            pltpu.VMEM((hpb, 2, s, MLSTM_HEAD_DIM), F32),
            pltpu.VMEM((hpb, 2, MLSTM_HEAD_DIM, 2 * MLSTM_HEAD_DIM), F32),
            pltpu.VMEM((hpb, MLSTM_HEAD_DIM, s), BF16),
        ],
        compiler_params=_params(("parallel", "parallel"), vmem),
        name="mlstm",
    )(gate_bias.astype(F32), p3, p3, p3, p3, gates5, norm_g.reshape(1, BRANCH_WIDTH).astype(F32))


def _rel_bucket(rel):
    half = REL_BUCKETS // 2
    max_exact = half // 2
    ret = jnp.where(rel > 0, half, 0)
    n = jnp.abs(rel)
    nf = jnp.maximum(n, 1).astype(F32)
    large = max_exact + (jnp.log(nf / max_exact) / math.log(REL_MAX_DIST / max_exact) * (half - max_exact)).astype(jnp.int32)
    large = jnp.minimum(large, half - 1)
    return ret + jnp.where(n < max_exact, n, large)


def diff_bias_tiles(rel_bias, t):
    assert t >= REL_MAX_DIST
    k = jnp.arange(2 * t, dtype=jnp.int32)
    rel = jnp.arange(-2, 3, dtype=jnp.int32)[:, None] * t + jnp.where(k < t, k, k - 2 * t)[None, :]
    onehot = (_rel_bucket(rel)[:, :, None] == jnp.arange(REL_BUCKETS, dtype=jnp.int32)).astype(F32)
    period = jnp.einsum('dkb,bh->hdk', onehot, rel_bias.astype(F32) * LOG2E, precision=lax.Precision.HIGHEST)
    far = period[:, 0::4, 0]
    return period, far


def _diff_attn_kernel(far_ref, q_ref, k_ref, v_ref, period_ref, lam_ref, g_ref, o_ref,
                      q_sc, kt_sc, vaug_sc, bias_sc, sc_buf, m_ref, acc_ref, *, t, lam_init):
    dq = DIFF_QK_DIM
    dv = DIFF_V_DIM
    s = k_ref.shape[1]
    nt = s // t
    head = pl.program_id(1)
    ahead = sc_buf.shape[0]
    rb = sc_buf.shape[1]
    grp = sc_buf.shape[2] // t
    wrap = (grp - 1) * t
    assert nt % grp == 0 and t % rb == 0 and nt >= 4

    tr = max(t, TRANSPOSE_ROWS)
    for r in range(0, s, tr):
        kt_sc[:, r:r + tr] = jnp.transpose(k_ref[0, r:r + tr, :].astype(F32)).astype(BF16)
    kt_sc[:, s:] = kt_sc[:, :wrap]
    vaug_sc[:s, :dv] = v_ref[0]
    vaug_sc[s:, :dv] = v_ref[0, :wrap, :]
    lane = lax.broadcasted_iota(jnp.int32, (s + wrap, dv), 1)
    vaug_sc[:, dv:] = jnp.where(lane == 0, 1.0, 0.0).astype(BF16)
    for dl in range(period_ref.shape[1]):
        full = jnp.broadcast_to(period_ref[0, dl:dl + 1, :], (t, 2 * t))
        bias_sc[dl] = pltpu.roll(full, 0, axis=1, stride=1, stride_axis=0)[:, :t]
    q_scale = dq ** -0.5 * LOG2E
    q_all = (q_ref[0].astype(F32) * q_scale).astype(BF16)
    q_sc[0] = q_all[:, :dq]
    q_sc[1] = q_all[:, dq:]
    lp = lam_ref[...]
    lam = (jnp.exp(jnp.sum(lp[0:1] * lp[1:2], axis=-1, keepdims=True))
           - jnp.exp(jnp.sum(lp[2:3] * lp[3:4], axis=-1, keepdims=True)) + lam_init)

    def key_tile(i, delta):
        j = lax.rem(i + delta, nt)
        return j, pl.multiple_of(j * t, t)

    def qk(i, item):
        pair, mp, hf = item
        row0 = pl.multiple_of(i * t + hf * rb, rb)
        _, off = key_tile(i, grp * pair)
        return jnp.dot(q_sc[mp, pl.ds(row0, rb), :], kt_sc[mp * dq:(mp + 1) * dq, pl.ds(off, grp * t)],
                       preferred_element_type=F32)

    items = [(pair, mp, hf) for pair in range(nt // grp) for mp in range(2) for hf in range(t // rb)]
    for a in range(ahead):
        sc_buf[a] = qk(0, items[a])

    def tile_body(i, carry):
        pending = [sc_buf[a] for a in range(ahead)]
        for n, (pair, mp, hf) in enumerate(items):
            sc = pending.pop(0)
            if n + ahead < len(items):
                pending.append(qk(i, items[n + ahead]))
            else:
                pending.append(qk(jnp.minimum(i + 1, nt - 1), items[n + ahead - len(items)]))
            rows = slice(hf * rb, (hf + 1) * rb)
            scs = [sc[:, p * t:(p + 1) * t] for p in range(grp)]
            _, off = key_tile(i, grp * pair)
            shifts, row_max = [], None
            for piece, delta in enumerate(range(grp * pair, grp * pair + grp)):
                j, _ = key_tile(i, delta)
                if delta in (0, 1, nt - 1):
                    tile = 2 if delta == 0 else jnp.clip(j - i, -2, 2) + 2
                    scs[piece] = scs[piece] + bias_sc[tile, rows, :]
                    shifts.append(None)
                    rm = jnp.max(scs[piece], axis=-1, keepdims=True)
                else:
                    shifts.append(jnp.where(j > i, far_ref[head, 1], far_ref[head, 0]))
                    rm = jnp.max(scs[piece], axis=-1, keepdims=True) + shifts[-1]
                row_max = rm if row_max is None else jnp.maximum(row_max, rm)
            if pair == 0:
                m_new = jnp.broadcast_to(row_max, (rb, V7X_LANES))
            else:
                m_prev = m_ref[mp, rows, :]
                m_new = jnp.maximum(m_prev, row_max)
            ps = []
            for piece in range(grp):
                m_sub = m_new if shifts[piece] is None else m_new - shifts[piece]
                ps.append(jnp.exp2((scs[piece] - pltpu.repeat(m_sub, t // V7X_LANES, axis=1)).astype(BF16)))
            pv = jnp.dot(jnp.concatenate(ps, axis=1), vaug_sc[pl.ds(off, grp * t), :], preferred_element_type=F32)
            if pair == 0:
                acc_ref[mp, rows, :] = pv
            else:
                alpha = jnp.exp2(m_prev - m_new)
                acc_ref[mp, rows, :] = acc_ref[mp, rows, :] * pltpu.repeat(alpha, 2 * dv // V7X_LANES, axis=1) + pv
            m_ref[mp, rows, :] = m_new
        for a in range(ahead):
            sc_buf[a] = pending[a]
        o0 = acc_ref[0, :, :dv] / acc_ref[0, :, dv:dv + 1]
        o1 = acc_ref[1, :, :dv] / acc_ref[1, :, dv:dv + 1]
        o = o0 - lam * o1
        ms = jnp.mean(o * o, axis=-1, keepdims=True)
        row0 = pl.multiple_of(i * t, t)
        o_ref[0, pl.ds(row0, t), :] = (o * lax.rsqrt(ms + EPS) * g_ref[...] * (1.0 - lam_init)).astype(o_ref.dtype)
        return carry

    lax.fori_loop(0, nt, tile_body, 0)


def diff_attention(p3, bias_period, bias_far, lam_params, sub_g, *, t, rows_per_item, tiles_per_item, layer_number):
    b, s, _ = p3.shape
    dv = DIFF_V_DIM
    n_off = bias_period.shape[1]
    assert s // t >= 4, "tiles 2 .. s/t-2 steps away from the query tile must all be beyond REL_MAX_DIST"
    lam_init = 0.8 - 0.6 * math.exp(-0.3 * (layer_number - 1))
    sw = s + (tiles_per_item - 1) * t
    item = rows_per_item * tiles_per_item * t
    vmem = (2 * (s * 128 * 2 + 128 * s * 2 + s * dv * 2 + s * dv * 2) + n_off * t * t * 4 + sw * (2 * dv + 128) * 2
            + 2 * s * 128 * 2 + item * 4 + 2 * t * 128 * 4 * 3 + 8 * item * 4)
    return pl.pallas_call(
        functools.partial(_diff_attn_kernel, t=t, lam_init=lam_init),
        grid=(b, DIFF_HEADS),
        in_specs=[
            pl.BlockSpec(memory_space=pltpu.SMEM),
            pl.BlockSpec((1, s, 128), lambda bi, h: (bi, 0, OFF_DQ // 128 + h)),
            pl.BlockSpec((1, s, 128), lambda bi, h: (bi, 0, OFF_DK // 128 + h)),
            pl.BlockSpec((1, s, dv), lambda bi, h: (bi, 0, OFF_DV // dv + h)),
            pl.BlockSpec((1, n_off, 2 * t), lambda bi, h: (h, 0, 0)),
            pl.BlockSpec((4, DIFF_QK_DIM), lambda bi, h: (0, 0)),
            pl.BlockSpec((1, dv), lambda bi, h: (0, 0)),
        ],
        out_specs=pl.BlockSpec((1, s, dv), lambda bi, h: (bi, 0, h)),
        out_shape=jax.ShapeDtypeStruct((b, s, BRANCH_WIDTH), BF16),
        scratch_shapes=[pltpu.VMEM((2, s, DIFF_QK_DIM), BF16), pltpu.VMEM((2 * DIFF_QK_DIM, sw), BF16),
                        pltpu.VMEM((sw, 2 * dv), BF16),
                        pltpu.VMEM((n_off, t, t), F32), pltpu.VMEM((1, rows_per_item, tiles_per_item * t), F32),
                        pltpu.VMEM((2, t, V7X_LANES), F32), pltpu.VMEM((2, t, 2 * dv), F32)],
        compiler_params=_params(("parallel", "parallel"), vmem),
        name="diff_attn",
    )(bias_far, p3, p3, p3, bias_period, lam_params.astype(F32), sub_g.reshape(1, dv).astype(F32))


def _merge_kernel(ya_ref, yb_ref, yc_ref, yd_ref, g_ref, wg_ref, wb_ref, wo_ref, x_ref, o_ref, wb_sc, wo_sc):
    dm = x_ref.shape[1]

    @pl.when(pl.program_id(0) == 0)
    def _():
        for n in range(wb_ref.shape[0]):
            wb_sc[n] = wb_ref[n].astype(BF16)
        wo_sc[...] = wo_ref[...].astype(BF16)

    tm = x_ref.shape[0]
    halves = [slice(0, tm // 2), slice(tm // 2, tm)]
    y_refs = (ya_ref, yb_ref, yc_ref, yd_ref)
    gates, branch = [], []
    for rows in halves:
        x = x_ref[rows, :]
        ms = jnp.mean(x * x, axis=-1, keepdims=True)
        u = (x * lax.rsqrt(ms + EPS) * g_ref[...]).astype(BF16)
        gates.append([jnp.dot(u, wg_ref[:, n * dm:(n + 1) * dm], preferred_element_type=F32)
                      for n in range(len(y_refs))])
        branch.append([jnp.dot(y_ref[rows, :], wb_sc[n], preferred_element_type=F32)
                       for n, y_ref in enumerate(y_refs)])
    for rows, gts, brs in zip(halves, gates, branch):
        merged = None
        for gt, br in zip(gts, brs):
            term = jax.nn.sigmoid(gt) * br
            merged = term if merged is None else merged + term
        o_ref[rows, :] = x_ref[rows, :] + jnp.dot(merged.astype(BF16), wo_sc[...], preferred_element_type=F32)


def merge_branches(ys, norm_g, w_gates, w_branch, w_out, x2, *, tm):
    m, dm = x2.shape
    w = BRANCH_WIDTH
    y_spec = pl.BlockSpec((tm, w), lambda i: (i, 0))
    vmem = (2 * (4 * tm * w * 2 + 2 * tm * dm * 4) + N_BRANCHES * dm * dm * 2 + (4 * w * dm + dm * dm) * (4 + 2)
            + 12 * tm * dm * 4)
    return pl.pallas_call(
        _merge_kernel,
        grid=(m // tm,),
        in_specs=[
            y_spec, y_spec, y_spec, y_spec,
            pl.BlockSpec((1, dm), lambda i: (0, 0)),
            pl.BlockSpec((dm, N_BRANCHES * dm), lambda i: (0, 0), pipeline_mode=pl.Buffered(1)),
            pl.BlockSpec((N_BRANCHES, w, dm), lambda i: (0, 0, 0), pipeline_mode=pl.Buffered(1)),
            pl.BlockSpec((dm, dm), lambda i: (0, 0), pipeline_mode=pl.Buffered(1)),
            pl.BlockSpec((tm, dm), lambda i: (i, 0)),
        ],
        out_specs=pl.BlockSpec((tm, dm), lambda i: (i, 0)),
        out_shape=jax.ShapeDtypeStruct((m, dm), F32),
        scratch_shapes=[pltpu.VMEM((N_BRANCHES, w, dm), BF16), pltpu.VMEM((dm, dm), BF16)],
        compiler_params=_params(("arbitrary",), vmem),
        name="merge",
    )(*ys, norm_g.reshape(1, dm).astype(F32), w_gates, w_branch.astype(F32), w_out.astype(F32), x2)


HALO_ROWS = 16


def _conv_ffn_kernel(x_ref, xp_ref, xn_ref, g_ref, wa_ref, wl_ref, cw_ref, cb_ref, wd_ref, gf_ref, o_ref,
                     xe_sc, wd_sc, *, tiles_per_seq, col_chunk, final_norm):
    tm = o_ref.shape[0]
    f = wa_ref.shape[1]
    h = HALO_ROWS
    pos = pl.program_id(0) % tiles_per_seq

    @pl.when(pl.program_id(0) == 0)
    def _():
        wd_sc[...] = wd_ref[...].astype(BF16)

    def norm(x):
        ms = jnp.mean(x * x, axis=-1, keepdims=True)
        return x * lax.rsqrt(ms + EPS) * g_ref[...]

    xe_sc[:h] = jnp.where(pos == 0, 0.0, norm(xp_ref[...])).astype(BF16)
    xe_sc[h:h + tm] = norm(x_ref[...]).astype(BF16)
    xe_sc[h + tm:] = jnp.where(pos == tiles_per_seq - 1, 0.0, norm(xn_ref[...])).astype(BF16)
    k0 = math.sqrt(2.0 / math.pi)
    bounds = list(range(0, f, col_chunk)) + [f]
    chunks = [slice(lo, hi) for lo, hi in zip(bounds[:-1], bounds[1:])]

    def project(cols):
        a_ext = jnp.dot(xe_sc[...], wa_ref[:, cols], preferred_element_type=F32)
        lin = jnp.dot(xe_sc[h:h + tm, :], wl_ref[:, cols], preferred_element_type=F32)
        return a_ext, lin

    y = x_ref[...]
    nxt = project(chunks[0])
    for n, cols in enumerate(chunks):
        a_ext, lin = nxt
        if n + 1 < len(chunks):
            nxt = project(chunks[n + 1])
        cw = cw_ref[:, cols]
        c = (a_ext[h - 1:h - 1 + tm] * cw[0:1] + a_ext[h:h + tm] * cw[1:2] + a_ext[h + 1:h + 1 + tm] * cw[2:3]
             + cb_ref[:, cols])
        t = jnp.tanh(c * ((c * c) * (k0 * 0.044715) + k0))
        hmid = ((c + c * t) * lin).astype(BF16)
        y = y + jnp.dot(hmid, wd_sc[cols, :], preferred_element_type=F32)
    if final_norm:
        ms = jnp.mean(y * y, axis=-1, keepdims=True)
        y = y * lax.rsqrt(ms + EPS) * gf_ref[...]
    o_ref[...] = y


def conv_ffn(x2, g, w_up, conv_w, conv_b, w_down, final_g, *, s, tm, col_chunk, final_norm):
    m, dm = x2.shape
    f = w_up.shape[1] // 2
    tiles_per_seq = s // tm
    hb = tm // HALO_ROWS
    n_halo = m // HALO_ROWS
    vmem = (2 * 2 * tm * dm * 4 + 2 * dm * f * 2 + f * dm * (4 + 2) + (tm + 2 * HALO_ROWS) * dm * 2
            + 6 * tm * col_chunk * 4 + 2 * tm * dm * 4)
    return pl.pallas_call(
        functools.partial(_conv_ffn_kernel, tiles_per_seq=tiles_per_seq, col_chunk=col_chunk, final_norm=final_norm),
        grid=(m // tm,),
        in_specs=[
            pl.BlockSpec((tm, dm), lambda i: (i, 0)),
            pl.BlockSpec((HALO_ROWS, dm), lambda i: (jnp.maximum(i * hb - 1, 0), 0)),
            pl.BlockSpec((HALO_ROWS, dm), lambda i: (jnp.minimum((i + 1) * hb, n_halo - 1), 0)),
            pl.BlockSpec((1, dm), lambda i: (0, 0)),
            pl.BlockSpec((dm, f), lambda i: (0, 0), pipeline_mode=pl.Buffered(1)),
            pl.BlockSpec((dm, f), lambda i: (0, 1), pipeline_mode=pl.Buffered(1)),
            pl.BlockSpec((CONV_W, f), lambda i: (0, 0)),
            pl.BlockSpec((1, f), lambda i: (0, 0)),
            pl.BlockSpec((f, dm), lambda i: (0, 0), pipeline_mode=pl.Buffered(1)),
            pl.BlockSpec((1, dm), lambda i: (0, 0)),
        ],
        out_specs=pl.BlockSpec((tm, dm), lambda i: (i, 0)),
        out_shape=jax.ShapeDtypeStruct((m, dm), F32),
        scratch_shapes=[pltpu.VMEM((tm + 2 * HALO_ROWS, dm), BF16), pltpu.VMEM((f, dm), BF16)],
        compiler_params=_params(("arbitrary",), vmem),
        name="conv_ffn",
    )(x2, x2, x2, g.reshape(1, dm).astype(F32), w_up, w_up, conv_w.astype(F32), conv_b.reshape(1, f).astype(F32),
      w_down.astype(F32), final_g.reshape(1, dm).astype(F32))


def _arrange_w_in(w):
    widths = (512, 512, 128, 128, 512, 512, 512, 512, 16, 512, 512, 512, 4096)
    offs = [0]
    for wd in widths:
        offs.append(offs[-1] + wd)
    seg = lambda n: w[:, offs[n]:offs[n + 1]]
    a, bq, bk, bv, cq, ck, cv, co, cgate, dq, dk, dv, gates = (seg(n) for n in range(len(widths)))
    main = jnp.concatenate([a, bq, cq, ck, cv, co, dq, dk, dv, bk, bv], axis=1).astype(BF16)
    gate = jnp.pad(cgate, ((0, 0), (0, V7X_LANES - cgate.shape[1]))).astype(BF16)
    return main, gate, gates.astype(BF16)


def kernel(x, norm_mix_g, w_in, mlstm_gate_bias, qk_norm_g, mlstm_norm_g, diff_lambda, diff_norm_g, rel_bias,
           w_branch, w_out, norm_ffn_g, w_up, conv_w, conv_b, w_down, final_norm_g):
    b, s, dm = x.shape
    depth = w_in.shape[0]
    m = b * s
    d_ff = w_down.shape[1]
    L = MLSTM_CHUNK
    tl = TILES

    bd, dft_hi, dft_lo = fourier_tables(s)
    cos2, sin2 = rope_tables(s)
    bias_tiles, bias_far = diff_bias_tiles(rel_bias, tl["diff_tile"])

    x2 = x.reshape(m, dm)
    for layer in range(depth):
        w_main, w_gate, w_branch_gates = _arrange_w_in(w_in[layer])
        p2, cgate = norm_matmul(x2, norm_mix_g[layer], w_main, w_gate, tm=tl["proj_rows"], tn=P_WIDTH)
        p3 = p2.reshape(b, s, P_WIDTH)

        y_a = fourier_mix(p3, bd, dft_hi, dft_lo, tm=tl["fourier_rows"])

        q_b, k_b, v_b = gqa_prep(p2, qk_norm_g[layer], cos2, sin2, s=s, tm=tl["gqa_prep_rows"])
        kt_b = jnp.swapaxes(k_b.reshape(b, s, GQA_KV_HEADS * HEAD_DIM), 1, 2)
        y_b = gqa_attention(q_b.reshape(b, s, -1), kt_b, v_b.reshape(b, s, -1),
                            tq=tl["gqa_q_rows"], tk=tl["gqa_keys"])

        gates5 = jnp.transpose(cgate[:, :4 * MLSTM_HEADS].reshape(b, s, 4, MLSTM_HEADS), (0, 2, 3, 1))
        gates5 = gates5.reshape(b, 4, MLSTM_HEADS, s // L, L)
        y_c = mlstm_branch(p3, gates5, mlstm_gate_bias[layer], mlstm_norm_g[layer],
                           heads_per_block=tl["mlstm_heads"])

        y_d = diff_attention(p3, bias_tiles, bias_far, diff_lambda[layer], diff_norm_g[layer],
                             t=tl["diff_tile"], rows_per_item=tl["diff_q_rows"],
                             tiles_per_item=tl["diff_key_tiles"], layer_number=layer + 1)

        ys = [y.reshape(m, BRANCH_WIDTH) for y in (y_a, y_b, y_c, y_d)]
        x2 = merge_branches(ys, norm_mix_g[layer], w_branch_gates, w_branch[layer], w_out[layer], x2,
                            tm=tl["merge_rows"])

        half_lin = jnp.concatenate([jnp.ones((d_ff,), F32), jnp.full((d_ff,), 0.5, F32)])
        x2 = conv_ffn(x2, norm_ffn_g[layer], (w_up[layer] * half_lin).astype(BF16), conv_w[layer], conv_b[layer],
                      w_down[layer], final_norm_g, s=s, tm=tl["ffn_rows"], col_chunk=tl["ffn_cols"],
                      final_norm=(layer == depth - 1))
    return x2.reshape(b, s, dm)
```

```python
import functools
import math

import jax
import jax.numpy as jnp
from jax import lax
from jax.experimental import pallas as pl
from jax.experimental.pallas import tpu as pltpu

F32 = jnp.float32
BF16 = jnp.bfloat16

GRID_W = 64
HEAD_DIM = 64
BRANCH_WIDTH = 512
N_BRANCHES = 4
FOURIER_GROUP_DIM = 64
GQA_Q_HEADS = 8
GQA_KV_HEADS = 2
MLSTM_HEADS = 4
MLSTM_HEAD_DIM = 128
MLSTM_CHUNK = 128
DIFF_HEADS = 4
DIFF_QK_DIM = 64
DIFF_V_DIM = 128
REL_BUCKETS = 32
REL_MAX_DIST = 128
CONV_W = 3
ROPE_BASE = 10000.0
EPS = 1e-6
LOG2E = math.log2(math.e)

V7X_LANES = 128
V7X_VMEM_BYTES = 64 * 1024 * 1024
V7X_VMEM_CAP = V7X_VMEM_BYTES - 8 * 1024 * 1024

TILES = dict(
    proj_rows=512,
    fourier_rows=512,
    gqa_prep_rows=512,
    gqa_q_rows=512, gqa_keys=1024,
    diff_tile=512, diff_q_rows=256, diff_key_tiles=4,
    mlstm_heads=2,
    merge_rows=512,
    ffn_rows=512, ffn_cols=1024,
)
TRANSPOSE_ROWS = 512

OFF_A = 0
OFF_BQ = 512
OFF_CQ = 1024
OFF_CK = 1536
OFF_CV = 2048
OFF_CO = 2560
OFF_DQ = 3072
OFF_DK = 3584
OFF_DV = 4096
OFF_BKV = 4608
P_WIDTH = 4864


def _params(sem, vmem_bytes):
    limit = int(min(max(vmem_bytes * 3 // 2 + (4 << 20), 16 << 20), V7X_VMEM_CAP))
    return pltpu.CompilerParams(dimension_semantics=sem, vmem_limit_bytes=limit)


def _norm_mm_kernel(x_ref, g_ref, w_ref, o_ref, xn_ref):
    @pl.when(pl.program_id(1) == 0)
    def _():
        x = x_ref[...]
        ms = jnp.mean(x * x, axis=-1, keepdims=True)
        xn_ref[...] = (x * lax.rsqrt(ms + EPS) * g_ref[...]).astype(BF16)

    o_ref[...] = jnp.dot(xn_ref[...], w_ref[...], preferred_element_type=F32).astype(o_ref.dtype)


def _norm_mm_gate_kernel(x_ref, g_ref, w_ref, wg_ref, o_ref, og_ref, xn_ref):
    @pl.when(pl.program_id(1) == 0)
    def _():
        x = x_ref[...]
        ms = jnp.mean(x * x, axis=-1, keepdims=True)
        xn = (x * lax.rsqrt(ms + EPS) * g_ref[...]).astype(BF16)
        xn_ref[...] = xn
        og_ref[...] = jnp.dot(xn, wg_ref[...], preferred_element_type=F32)

    o_ref[...] = jnp.dot(xn_ref[...], w_ref[...], preferred_element_type=F32).astype(o_ref.dtype)


def norm_matmul(x, g, w, w_gate=None, *, tm, tn):
    m, k = x.shape
    n = w.shape[1]
    grid = (m // tm, n // tn)
    vmem = 2 * tm * k * 4 + tm * k * 2 + 2 * k * tn * 2 + 2 * tm * tn * 2 + 4 * tm * k
    x_spec = pl.BlockSpec((tm, k), lambda i, j: (i, 0))
    g_spec = pl.BlockSpec((1, k), lambda i, j: (0, 0))
    w_spec = pl.BlockSpec((k, tn), lambda i, j: (0, j))
    o_spec = pl.BlockSpec((tm, tn), lambda i, j: (i, j))
    scratch = [pltpu.VMEM((tm, k), BF16)]
    g2 = g.reshape(1, k).astype(F32)
    if w_gate is None:
        return pl.pallas_call(
            _norm_mm_kernel,
            grid=grid,
            in_specs=[x_spec, g_spec, w_spec],
            out_specs=o_spec,
            out_shape=jax.ShapeDtypeStruct((m, n), BF16),
            scratch_shapes=scratch,
            compiler_params=_params(("parallel", "arbitrary"), vmem),
            name="norm_matmul",
        )(x, g2, w)
    ng = w_gate.shape[1]
    return pl.pallas_call(
        _norm_mm_gate_kernel,
        grid=grid,
        in_specs=[x_spec, g_spec, w_spec, pl.BlockSpec((k, ng), lambda i, j: (0, 0))],
        out_specs=[o_spec, pl.BlockSpec((tm, ng), lambda i, j: (i, 0))],
        out_shape=[jax.ShapeDtypeStruct((m, n), BF16), jax.ShapeDtypeStruct((m, ng), F32)],
        scratch_shapes=scratch,
        compiler_params=_params(("parallel", "arbitrary"), vmem),
        name="norm_matmul_gate",
    )(x, g2, w, w_gate)


FOURIER_HALO = 16


def _fourier_kernel(a_ref, bd_ref, hi_ref, lo_ref, flip_ref, o_ref, z_ref, ext_ref, *, tm, row_chunk):
    s = a_ref.shape[1]
    w = a_ref.shape[2]
    n_lo = lo_ref.shape[1]
    i = pl.program_id(1)

    @pl.when(i == 0)
    def _():
        for r in range(0, s, row_chunk):
            a = a_ref[0, r:r + row_chunk, :]
            zc = jnp.dot(a, bd_ref[...], preferred_element_type=F32)
            z_ref[r:r + row_chunk, :] = zc[:, :w].astype(BF16)
            z_ref[s + r:s + r + row_chunk, :] = zc[:, w:].astype(BF16)

    cl, sl = lo_ref[0], lo_ref[1]
    for r in range(tm // n_lo + 1):
        rows = n_lo if r < tm // n_lo else ext_ref.shape[0] - tm
        k1 = i * (tm // n_lo) + r
        ch = hi_ref[0, pl.ds(k1, 1), :]
        nsh = hi_ref[1, pl.ds(k1, 1), :]
        ext_ref[r * n_lo:r * n_lo + rows, :s] = (cl[:rows] * ch + sl[:rows] * nsh).astype(BF16)
        ext_ref[r * n_lo:r * n_lo + rows, s:] = (cl[:rows] * nsh - sl[:rows] * ch).astype(BF16)
    p = jnp.dot(ext_ref[:, :s], z_ref[:s], preferred_element_type=F32)
    q = jnp.dot(ext_ref[:, s:], z_ref[s:], preferred_element_type=F32)
    lo = pl.multiple_of(i * tm, tm)
    o_ref[0, pl.ds(lo, tm), :] = (p[:tm] + q[:tm]).astype(o_ref.dtype)
    mirrored = (p[1:tm + 1] - q[1:tm + 1]).astype(BF16)
    hi = pl.multiple_of(s - (i + 1) * tm, tm)
    o_ref[0, pl.ds(hi, tm), :] = jnp.dot(flip_ref[...], mirrored, preferred_element_type=F32).astype(o_ref.dtype)


def fourier_tables(s):
    cg = FOURIER_GROUP_DIM
    jj = jnp.arange(cg, dtype=jnp.int32)
    ang_c = (2.0 * math.pi / cg) * ((jj[:, None] * jj[None, :]) % cg).astype(F32)
    eye_g = jnp.eye(BRANCH_WIDTH // cg, dtype=F32)
    bd_c = jnp.kron(eye_g, jnp.cos(ang_c)) * cg ** -0.5
    bd_s = jnp.kron(eye_g, jnp.sin(ang_c)) * cg ** -0.5
    bd = jnp.concatenate([bd_c, bd_s], axis=1).astype(BF16)
    n_lo = s // cg
    nn = jnp.arange(s, dtype=jnp.int32)
    k1 = jnp.arange(cg // 2 + 2, dtype=jnp.int32)
    ang_hi = (2.0 * math.pi / cg) * ((k1[:, None] * nn[None, :]) % cg).astype(F32)
    ll = jnp.arange(n_lo, dtype=jnp.int32)
    ang_lo = (2.0 * math.pi / s) * ((ll[:, None] * nn[None, :]) % s).astype(F32)
    scale = s ** -0.5
    hi = jnp.stack([jnp.cos(ang_hi), -jnp.sin(ang_hi)]) * scale
    lo = jnp.stack([jnp.cos(ang_lo), jnp.sin(ang_lo)])
    return bd, hi, lo


def fourier_mix(p3, bd, hi, lo, *, tm):
    b, s, _ = p3.shape
    w = BRANCH_WIDTH
    n_lo = lo.shape[1]
    h = FOURIER_HALO
    assert (s // 2) % tm == 0 and tm % n_lo == 0 and h <= n_lo
    r = jnp.arange(tm, dtype=jnp.int32)
    flip = (r[:, None] + r[None, :] == tm - 1).astype(BF16)
    vmem = (s * w * 2 + 2 * w * w * 2 + 2 * (hi.size + lo.size) * 4 + (tm + h) * 2 * s * 2 + 2 * s * w * 2
            + 2 * s * w * 2 + 8 * (tm + h) * w * 4 + 8 * n_lo * s * 4)
    return pl.pallas_call(
        functools.partial(_fourier_kernel, tm=tm, row_chunk=min(s, tm)),
        grid=(b, s // 2 // tm),
        in_specs=[
            pl.BlockSpec((1, s, w), lambda bi, i: (bi, 0, OFF_A // w), pipeline_mode=pl.Buffered(1)),
            pl.BlockSpec((w, 2 * w), lambda bi, i: (0, 0), pipeline_mode=pl.Buffered(1)),
            pl.BlockSpec(hi.shape, lambda bi, i: (0, 0, 0)),
            pl.BlockSpec(lo.shape, lambda bi, i: (0, 0, 0)),
            pl.BlockSpec((tm, tm), lambda bi, i: (0, 0)),
        ],
        out_specs=pl.BlockSpec((1, s, w), lambda bi, i: (bi, 0, 0)),
        out_shape=jax.ShapeDtypeStruct((b, s, w), BF16),
        scratch_shapes=[pltpu.VMEM((2 * s, w), BF16), pltpu.VMEM((tm + h, 2 * s), BF16)],
        compiler_params=_params(("parallel", "arbitrary"), vmem),
        name="fourier",
    )(p3, bd, hi, lo, flip)


def rope_tables(s):
    rows = s // GRID_W
    row_id = jnp.repeat(jnp.arange(rows, dtype=F32), GRID_W)
    col_id = jnp.tile(jnp.arange(GRID_W, dtype=F32), rows)
    n_pairs = HEAD_DIM // 4
    inv_freq = ROPE_BASE ** (-jnp.arange(n_pairs, dtype=F32) / n_pairs)
    ang = jnp.concatenate([row_id[:, None] * inv_freq, col_id[:, None] * inv_freq], axis=-1)
    cos, sin = jnp.cos(ang), jnp.sin(ang)
    return jnp.concatenate([cos, cos] * 2, axis=-1), jnp.concatenate([-sin, sin] * 2, axis=-1)


def _norm_rope(x, g, seg, cos2, sin2):
    half = HEAD_DIM // 2
    x2 = x * x
    hi = x2.astype(BF16)
    lo = (x2 - hi.astype(F32)).astype(BF16)
    ms = jnp.dot(hi, seg, preferred_element_type=F32) + jnp.dot(lo, seg, preferred_element_type=F32)
    y = x * lax.rsqrt(ms + EPS) * g
    lane = lax.broadcasted_iota(jnp.int32, (x.shape[0], V7X_LANES), 1)
    first_half = (lane % HEAD_DIM) < half
    outs = []
    for cb in range(x.shape[1] // V7X_LANES):
        yb = y[:, cb * V7X_LANES:(cb + 1) * V7X_LANES]
        rot = jnp.where(first_half, pltpu.roll(yb, V7X_LANES - half, axis=1), pltpu.roll(yb, half, axis=1))
        outs.append(yb * cos2 + rot * sin2)
    return outs[0] if len(outs) == 1 else jnp.concatenate(outs, axis=-1)


def _gqa_prep_kernel(q_ref, kv_ref, gq_ref, gk_ref, seg_ref, cos_ref, sin_ref, qo_ref, ko_ref, vo_ref):
    d = HEAD_DIM
    nk = GQA_KV_HEADS * d
    cos2, sin2 = cos_ref[...], sin_ref[...]
    q = q_ref[...].astype(F32)
    kv = kv_ref[...].astype(F32)
    qo_ref[...] = _norm_rope(q, gq_ref[...], seg_ref[...], cos2, sin2).astype(BF16)
    ko_ref[...] = _norm_rope(kv[:, :nk], gk_ref[...], seg_ref[:nk, :nk], cos2, sin2).astype(BF16)
    v = kv[:, nk:]
    lane = lax.broadcasted_iota(jnp.int32, v.shape, 1)
    ones_col = jnp.where(lane == d, 1.0, 0.0)
    vo_ref[:, :nk] = jnp.where(lane < d, v, ones_col).astype(BF16)
    vo_ref[:, nk:] = jnp.where(lane < d, pltpu.roll(v, d, axis=1), ones_col).astype(BF16)


def gqa_prep(p2, qk_g, cos2, sin2, *, s, tm):
    m = p2.shape[0]
    nq = GQA_Q_HEADS * HEAD_DIM
    nkv = GQA_KV_HEADS * HEAD_DIM
    assert nkv == V7X_LANES
    tiles_per_seq = s // tm
    q_scale = HEAD_DIM ** -0.5 * LOG2E
    gq = jnp.tile(qk_g[0].astype(F32) * q_scale, GQA_Q_HEADS).reshape(1, nq)
    gk = jnp.tile(qk_g[1].astype(F32), GQA_KV_HEADS).reshape(1, nkv)
    seg = jnp.kron(jnp.eye(GQA_Q_HEADS, dtype=F32), jnp.full((HEAD_DIM, HEAD_DIM), 1.0 / HEAD_DIM, F32)).astype(BF16)
    return pl.pallas_call(
        _gqa_prep_kernel,
        grid=(m // tm,),
        in_specs=[
            pl.BlockSpec((tm, nq), lambda i: (i, OFF_BQ // nq)),
            pl.BlockSpec((tm, 2 * nkv), lambda i: (i, OFF_BKV // (2 * nkv))),
            pl.BlockSpec((1, nq), lambda i: (0, 0)),
            pl.BlockSpec((1, nkv), lambda i: (0, 0)),
            pl.BlockSpec((nq, nq), lambda i: (0, 0)),
            pl.BlockSpec((tm, V7X_LANES), lambda i: (i % tiles_per_seq, 0)),
            pl.BlockSpec((tm, V7X_LANES), lambda i: (i % tiles_per_seq, 0)),
        ],
        out_specs=[
            pl.BlockSpec((tm, nq), lambda i: (i, 0)),
            pl.BlockSpec((tm, nkv), lambda i: (i, 0)),
            pl.BlockSpec((tm, 2 * nkv), lambda i: (i, 0)),
        ],
        out_shape=[
            jax.ShapeDtypeStruct((m, nq), BF16),
            jax.ShapeDtypeStruct((m, nkv), BF16),
            jax.ShapeDtypeStruct((m, 2 * nkv), BF16),
        ],
        compiler_params=_params(("parallel",), 16 * tm * nq * 4),
        name="gqa_prep",
    )(p2, p2, gq, gk, seg, cos2, sin2)


def _gqa_attn_kernel(q_ref, kt_ref, v_ref, o_ref, q_sc, m_ref, acc_ref, *, tk):
    d = HEAD_DIM
    tq = q_ref.shape[1]
    grp = q_ref.shape[2] // d
    s = kt_ref.shape[2]
    for g in range(grp):
        q_sc[g * tq:(g + 1) * tq, :] = q_ref[0, :, g * d:(g + 1) * d]

    def qk(item):
        c, g = item
        return jnp.dot(q_sc[g * tq:(g + 1) * tq, :], kt_ref[0, :, c * tk:(c + 1) * tk],
                       preferred_element_type=F32)

    items = [(c, g) for c in range(s // tk) for g in range(grp)]
    sc_next = qk(items[0])
    for n, (c, g) in enumerate(items):
        rows = slice(g * tq, (g + 1) * tq)
        sc = sc_next
        if n + 1 < len(items):
            sc_next = qk(items[n + 1])
        v = v_ref[0, c * tk:(c + 1) * tk, :]
        row_max = jnp.max(sc, axis=-1, keepdims=True)
        if c == 0:
            m_new = jnp.broadcast_to(row_max, (tq, V7X_LANES))
        else:
            m_prev = m_ref[rows, :]
            m_new = jnp.maximum(m_prev, row_max)
        p = jnp.exp2((sc - pltpu.repeat(m_new, tk // V7X_LANES, axis=1)).astype(BF16))
        pv = jnp.dot(p, v, preferred_element_type=F32)
        if c == 0:
            acc_ref[rows, :] = pv
        else:
            acc_ref[rows, :] = acc_ref[rows, :] * jnp.exp2(m_prev - m_new) + pv
        m_ref[rows, :] = m_new
    for g in range(grp):
        acc = acc_ref[g * tq:(g + 1) * tq, :]
        o_ref[0, :, g * d:(g + 1) * d] = (acc[:, :d] / acc[:, d:d + 1]).astype(o_ref.dtype)


def gqa_attention(q3, kt3, v3, *, tq, tk):
    b, s, nq = q3.shape
    d = HEAD_DIM
    grp = GQA_Q_HEADS // GQA_KV_HEADS
    mrows = grp * tq
    vmem = (2 * (tq * grp * d * 2 + d * s * 2 + s * 128 * 2 + tq * grp * d * 2) + 3 * mrows * 128 * 4
            + 12 * tq * tk * 4)
    return pl.pallas_call(
        functools.partial(_gqa_attn_kernel, tk=tk),
        grid=(b, GQA_KV_HEADS, s // tq),
        in_specs=[
            pl.BlockSpec((1, tq, grp * d), lambda bi, kv, i: (bi, i, kv)),
            pl.BlockSpec((1, d, s), lambda bi, kv, i: (bi, kv, 0)),
            pl.BlockSpec((1, s, 2 * d), lambda bi, kv, i: (bi, 0, kv)),
        ],
        out_specs=pl.BlockSpec((1, tq, grp * d), lambda bi, kv, i: (bi, i, kv)),
        out_shape=jax.ShapeDtypeStruct((b, s, nq), BF16),
        scratch_shapes=[pltpu.VMEM((mrows, d), BF16), pltpu.VMEM((mrows, V7X_LANES), F32),
                        pltpu.VMEM((mrows, 2 * d), F32)],
        compiler_params=_params(("parallel", "parallel", "parallel"), vmem),
        name="gqa_attn",
    )(q3, kt3, v3)


def _mlstm_step(chains, ms, q_ref, kt_sc, v_ref, r_sc, cm_sc, b_sc, st_sc, h_sc):
    L = MLSTM_CHUNK
    dh = MLSTM_HEAD_DIM
    assert L == dh
    scale = dh ** -0.5
    row_i = lax.broadcasted_iota(jnp.int32, (L, L), 0)
    col_i = lax.broadcasted_iota(jnp.int32, (L, L), 1)

    def col(x_row):
        return jnp.transpose(jnp.broadcast_to(x_row, (L, L)))

    pre = []
    for (hh, direction, c), m in zip(chains, ms):
        off = pl.multiple_of(c * L, L)
        lanes = slice(hh * dh, (hh + 1) * dh)
        q = q_ref[0, pl.ds(off, L), lanes]
        kt = kt_sc[hh, :, pl.ds(off, L)]
        v = v_ref[0, pl.ds(off, L), lanes]
        r_row = r_sc[hh, direction, pl.ds(c, 1), :]
        cm_row = cm_sc[hh, direction, pl.ds(c, 1), :]
        b_row = b_sc[hh, direction, pl.ds(c, 1), :]
        rmax = jnp.max(r_row, axis=-1, keepdims=True)
        btot = b_row[:, L - 1:L] if direction == 0 else b_row[:, 0:1]
        cmat = jnp.maximum(m, col(cm_row))
        mask = (row_i >= col_i) if direction == 0 else (row_i <= col_i)
        c_last = jnp.maximum(m, rmax)
        w_state = jnp.exp(r_row - c_last) * scale
        pre.append(dict(
            off=off, lanes=lanes, q=q, kt=kt,
            v_aug=jnp.concatenate([v, jnp.ones((L, dh), BF16)], axis=-1),
            w_intra=jnp.where(mask, jnp.exp(r_row - cmat), 0.0) * scale,
            w_inter=jnp.exp(m - cmat),
            den_floor=jnp.exp(-(col(b_row) + cmat)),
            decay=jnp.exp(m - c_last),
            kt_w=(kt.astype(F32) * w_state).astype(BF16),
            m_new=btot + c_last,
        ))
    s_raw = [jnp.dot(p["q"], p["kt"], preferred_element_type=F32) for p in pre]
    states = [st_sc[hh, direction] for hh, direction, _ in chains]
    inter = [jnp.dot(p["q"], st.astype(BF16), preferred_element_type=F32) for p, st in zip(pre, states)]
    upd = [jnp.dot(p["kt_w"], p["v_aug"], preferred_element_type=F32) for p in pre]
    for n, ((hh, direction, _), p) in enumerate(zip(chains, pre)):
        st_sc[hh, direction] = p["decay"] * states[n] + upd[n]
        intra = jnp.dot((s_raw[n] * p["w_intra"]).astype(BF16), p["v_aug"], preferred_element_type=F32)
        h_aug = jnp.concatenate([p["w_inter"], p["w_inter"]], axis=-1) * inter[n] + intra
        h_sc[hh, direction, pl.ds(p["off"], L), :] = h_aug[:, :dh] / jnp.maximum(jnp.abs(h_aug[:, dh:]), p["den_floor"])
    return [p["m_new"] for p in pre]


def _mlstm_kernel(bias_ref, q_ref, k_ref, v_ref, o_ref, gate_ref, g_ref, y_ref,
                  r_sc, cm_sc, b_sc, h_sc, st_sc, kt_sc):
    L = MLSTM_CHUNK
    dh = MLSTM_HEAD_DIM
    hpb = q_ref.shape[2] // dh
    head0 = pl.program_id(1) * hpb
    nc = q_ref.shape[1] // L
    lane = lax.broadcasted_iota(jnp.int32, (nc, L), 1)
    shifts = [1 << t for t in range(int(math.log2(L)))]
    for hh in range(hpb):
        for d in range(2):
            i_pre = gate_ref[0, 2 * d, hh] + bias_ref[2 * d, head0 + hh]
            f_pre = gate_ref[0, 2 * d + 1, hh] + bias_ref[2 * d + 1, head0 + hh]
            logf = jnp.minimum(f_pre, 0.0) - jnp.log1p(jnp.exp(-jnp.abs(f_pre)))
            bc = logf
            for sh in shifts:
                if d == 0:
                    bc = bc + jnp.where(lane >= sh, pltpu.roll(bc, sh, axis=1), 0.0)
                else:
                    bc = bc + jnp.where(lane < L - sh, pltpu.roll(bc, L - sh, axis=1), 0.0)
            r = i_pre - bc
            cm = r
            for sh in shifts:
                if d == 0:
                    cm = jnp.maximum(cm, jnp.where(lane >= sh, pltpu.roll(cm, sh, axis=1), -jnp.inf))
                else:
                    cm = jnp.maximum(cm, jnp.where(lane < L - sh, pltpu.roll(cm, L - sh, axis=1), -jnp.inf))
            r_sc[hh, d] = r
            cm_sc[hh, d] = cm
            b_sc[hh, d] = bc
    st_sc[...] = jnp.zeros(st_sc.shape, F32)
    tr = TRANSPOSE_ROWS
    for hh in range(hpb):
        for r in range(0, q_ref.shape[1], tr):
            kt_sc[hh, :, r:r + tr] = jnp.transpose(k_ref[0, r:r + tr, hh * dh:(hh + 1) * dh].astype(F32)).astype(BF16)

    def body(c, ms):
        chains = [(hh, d, c if d == 0 else nc - 1 - c) for hh in range(hpb) for d in range(2)]
        return tuple(_mlstm_step(chains, ms, q_ref, kt_sc, v_ref, r_sc, cm_sc, b_sc, st_sc, h_sc))

    lax.fori_loop(0, nc, body, tuple(jnp.zeros((1, 1), F32) for _ in range(2 * hpb)), unroll=2)
    for hh in range(hpb):
        lanes = slice(hh * dh, (hh + 1) * dh)
        hsum = h_sc[hh, 0] + h_sc[hh, 1]
        ms = jnp.mean(hsum * hsum, axis=-1, keepdims=True)
        y = hsum * lax.rsqrt(ms + EPS) * g_ref[:, lanes]
        y_ref[0, :, lanes] = (jax.nn.sigmoid(o_ref[0, :, lanes].astype(F32)) * y).astype(y_ref.dtype)


def mlstm_branch(p3, gates5, gate_bias, norm_g, *, heads_per_block):
    b, s, _ = p3.shape
    L = MLSTM_CHUNK
    hpb = heads_per_block
    wb = hpb * MLSTM_HEAD_DIM
    nc = s // L
    blk = lambda off: pl.BlockSpec((1, s, wb), lambda bi, h, off=off: (bi, 0, off // wb + h))
    vmem = 2 * 5 * s * wb * 2 + 2 * s * wb * 4 + 6 * hpb * nc * L * 4 + 4 * hpb * wb * wb * 4 + 3 * s * wb * 4
    return pl.pallas_call(
        _mlstm_kernel,
        grid=(b, MLSTM_HEADS // hpb),
        in_specs=[
            pl.BlockSpec(memory_space=pltpu.SMEM),
            blk(OFF_CQ), blk(OFF_CK), blk(OFF_CV), blk(OFF_CO),
            pl.BlockSpec((1, 4, hpb, nc, L), lambda bi, h: (bi, 0, h, 0, 0)),
            pl.BlockSpec((1, wb), lambda bi, h: (0, h)),
        ],
        out_specs=pl.BlockSpec((1, s, wb), lambda bi, h: (bi, 0, h)),
        out_shape=jax.ShapeDtypeStruct((b, s, BRANCH_WIDTH), BF16),
        scratch_shapes=[
            pltpu.VMEM((hpb, 2, nc, L), F32), pltpu.VMEM((hpb, 2, nc, L), F32), pltpu.VMEM((hpb, 2, nc, L), F32),
            pltpu.VMEM((hpb, 2, s, MLSTM_HEAD_DIM), F32),
            pltpu.VMEM((hpb, 2, MLSTM_HEAD_DIM, 2 * MLSTM_HEAD_DIM), F32),
            pltpu.VMEM((hpb, MLSTM_HEAD_DIM, s), BF16),
        ],
        compiler_params=_params(("parallel", "parallel"), vmem),
        name="mlstm",
    )(gate_bias.astype(F32), p3, p3, p3, p3, gates5, norm_g.reshape(1, BRANCH_WIDTH).astype(F32))


def _rel_bucket(rel):
    half = REL_BUCKETS // 2
    max_exact = half // 2
    ret = jnp.where(rel > 0, half, 0)
    n = jnp.abs(rel)
    nf = jnp.maximum(n, 1).astype(F32)
    large = max_exact + (jnp.log(nf / max_exact) / math.log(REL_MAX_DIST / max_exact) * (half - max_exact)).astype(jnp.int32)
    large = jnp.minimum(large, half - 1)
    return ret + jnp.where(n < max_exact, n, large)


def diff_bias_tiles(rel_bias, t):
    assert t >= REL_MAX_DIST
    k = jnp.arange(2 * t, dtype=jnp.int32)
    rel = jnp.arange(-2, 3, dtype=jnp.int32)[:, None] * t + jnp.where(k < t, k, k - 2 * t)[None, :]
    onehot = (_rel_bucket(rel)[:, :, None] == jnp.arange(REL_BUCKETS, dtype=jnp.int32)).astype(F32)
    period = jnp.einsum('dkb,bh->hdk', onehot, rel_bias.astype(F32) * LOG2E, precision=lax.Precision.HIGHEST)
    far = period[:, 0::4, 0]
    return period, far


def _diff_attn_kernel(far_ref, q_ref, k_ref, v_ref, period_ref, lam_ref, g_ref, o_ref,
                      q_sc, kt_sc, vaug_sc, bias_sc, sc_buf, m_ref, acc_ref, *, t, lam_init):
    dq = DIFF_QK_DIM
    dv = DIFF_V_DIM
    s = k_ref.shape[1]
    nt = s // t
    head = pl.program_id(1)
    ahead = sc_buf.shape[0]
    rb = sc_buf.shape[1]
    grp = sc_buf.shape[2] // t
    wrap = (grp - 1) * t
    assert nt % grp == 0 and t % rb == 0 and nt >= 4

    tr = max(t, TRANSPOSE_ROWS)
    for r in range(0, s, tr):
        kt_sc[:, r:r + tr] = jnp.transpose(k_ref[0, r:r + tr, :].astype(F32)).astype(BF16)
    kt_sc[:, s:] = kt_sc[:, :wrap]
    vaug_sc[:s, :dv] = v_ref[0]
    vaug_sc[s:, :dv] = v_ref[0, :wrap, :]
    lane = lax.broadcasted_iota(jnp.int32, (s + wrap, dv), 1)
    vaug_sc[:, dv:] = jnp.where(lane == 0, 1.0, 0.0).astype(BF16)
    for dl in range(period_ref.shape[1]):
        full = jnp.broadcast_to(period_ref[0, dl:dl + 1, :], (t, 2 * t))
        bias_sc[dl] = pltpu.roll(full, 0, axis=1, stride=1, stride_axis=0)[:, :t]
    q_scale = dq ** -0.5 * LOG2E
    q_all = (q_ref[0].astype(F32) * q_scale).astype(BF16)
    q_sc[0] = q_all[:, :dq]
    q_sc[1] = q_all[:, dq:]
    lp = lam_ref[...]
    lam = (jnp.exp(jnp.sum(lp[0:1] * lp[1:2], axis=-1, keepdims=True))
           - jnp.exp(jnp.sum(lp[2:3] * lp[3:4], axis=-1, keepdims=True)) + lam_init)

    def key_tile(i, delta):
        j = lax.rem(i + delta, nt)
        return j, pl.multiple_of(j * t, t)

    def qk(i, item):
        pair, mp, hf = item
        row0 = pl.multiple_of(i * t + hf * rb, rb)
        _, off = key_tile(i, grp * pair)
        return jnp.dot(q_sc[mp, pl.ds(row0, rb), :], kt_sc[mp * dq:(mp + 1) * dq, pl.ds(off, grp * t)],
                       preferred_element_type=F32)

    items = [(pair, mp, hf) for pair in range(nt // grp) for mp in range(2) for hf in range(t // rb)]
    for a in range(ahead):
        sc_buf[a] = qk(0, items[a])

    def tile_body(i, carry):
        pending = [sc_buf[a] for a in range(ahead)]
        for n, (pair, mp, hf) in enumerate(items):
            sc = pending.pop(0)
            if n + ahead < len(items):
                pending.append(qk(i, items[n + ahead]))
            else:
                pending.append(qk(jnp.minimum(i + 1, nt - 1), items[n + ahead - len(items)]))
            rows = slice(hf * rb, (hf + 1) * rb)
            scs = [sc[:, p * t:(p + 1) * t] for p in range(grp)]
            _, off = key_tile(i, grp * pair)
            shifts, row_max = [], None
            for piece, delta in enumerate(range(grp * pair, grp * pair + grp)):
                j, _ = key_tile(i, delta)
                if delta in (0, 1, nt - 1):
                    tile = 2 if delta == 0 else jnp.clip(j - i, -2, 2) + 2
                    scs[piece] = scs[piece] + bias_sc[tile, rows, :]
                    shifts.append(None)
                    rm = jnp.max(scs[piece], axis=-1, keepdims=True)
                else:
                    shifts.append(jnp.where(j > i, far_ref[head, 1], far_ref[head, 0]))
                    rm = jnp.max(scs[piece], axis=-1, keepdims=True) + shifts[-1]
                row_max = rm if row_max is None else jnp.maximum(row_max, rm)
            if pair == 0:
                m_new = jnp.broadcast_to(row_max, (rb, V7X_LANES))
            else:
                m_prev = m_ref[mp, rows, :]
                m_new = jnp.maximum(m_prev, row_max)
            ps = []
            for piece in range(grp):
                m_sub = m_new if shifts[piece] is None else m_new - shifts[piece]
                ps.append(jnp.exp2((scs[piece] - pltpu.repeat(m_sub, t // V7X_LANES, axis=1)).astype(BF16)))
            pv = jnp.dot(jnp.concatenate(ps, axis=1), vaug_sc[pl.ds(off, grp * t), :], preferred_element_type=F32)
            if pair == 0:
                acc_ref[mp, rows, :] = pv
            else:
                alpha = jnp.exp2(m_prev - m_new)
                acc_ref[mp, rows, :] = acc_ref[mp, rows, :] * pltpu.repeat(alpha, 2 * dv // V7X_LANES, axis=1) + pv
            m_ref[mp, rows, :] = m_new
        for a in range(ahead):
            sc_buf[a] = pending[a]
        o0 = acc_ref[0, :, :dv] / acc_ref[0, :, dv:dv + 1]
        o1 = acc_ref[1, :, :dv] / acc_ref[1, :, dv:dv + 1]
        o = o0 - lam * o1
        ms = jnp.mean(o * o, axis=-1, keepdims=True)
        row0 = pl.multiple_of(i * t, t)
        o_ref[0, pl.ds(row0, t), :] = (o * lax.rsqrt(ms + EPS) * g_ref[...] * (1.0 - lam_init)).astype(o_ref.dtype)
        return carry

    lax.fori_loop(0, nt, tile_body, 0)


def diff_attention(p3, bias_period, bias_far, lam_params, sub_g, *, t, rows_per_item, tiles_per_item, layer_number):
    b, s, _ = p3.shape
    dv = DIFF_V_DIM
    n_off = bias_period.shape[1]
    assert s // t >= 4, "tiles 2 .. s/t-2 steps away from the query tile must all be beyond REL_MAX_DIST"
    lam_init = 0.8 - 0.6 * math.exp(-0.3 * (layer_number - 1))
    sw = s + (tiles_per_item - 1) * t
    item = rows_per_item * tiles_per_item * t
    vmem = (2 * (s * 128 * 2 + 128 * s * 2 + s * dv * 2 + s * dv * 2) + n_off * t * t * 4 + sw * (2 * dv + 128) * 2
            + 2 * s * 128 * 2 + item * 4 + 2 * t * 128 * 4 * 3 + 8 * item * 4)
    return pl.pallas_call(
        functools.partial(_diff_attn_kernel, t=t, lam_init=lam_init),
        grid=(b, DIFF_HEADS),
        in_specs=[
            pl.BlockSpec(memory_space=pltpu.SMEM),
            pl.BlockSpec((1, s, 128), lambda bi, h: (bi, 0, OFF_DQ // 128 + h)),
            pl.BlockSpec((1, s, 128), lambda bi, h: (bi, 0, OFF_DK // 128 + h)),
            pl.BlockSpec((1, s, dv), lambda bi, h: (bi, 0, OFF_DV // dv + h)),
            pl.BlockSpec((1, n_off, 2 * t), lambda bi, h: (h, 0, 0)),
            pl.BlockSpec((4, DIFF_QK_DIM), lambda bi, h: (0, 0)),
            pl.BlockSpec((1, dv), lambda bi, h: (0, 0)),
        ],
        out_specs=pl.BlockSpec((1, s, dv), lambda bi, h: (bi, 0, h)),
        out_shape=jax.ShapeDtypeStruct((b, s, BRANCH_WIDTH), BF16),
        scratch_shapes=[pltpu.VMEM((2, s, DIFF_QK_DIM), BF16), pltpu.VMEM((2 * DIFF_QK_DIM, sw), BF16),
                        pltpu.VMEM((sw, 2 * dv), BF16),
                        pltpu.VMEM((n_off, t, t), F32), pltpu.VMEM((1, rows_per_item, tiles_per_item * t), F32),
                        pltpu.VMEM((2, t, V7X_LANES), F32), pltpu.VMEM((2, t, 2 * dv), F32)],
        compiler_params=_params(("parallel", "parallel"), vmem),
        name="diff_attn",
    )(bias_far, p3, p3, p3, bias_period, lam_params.astype(F32), sub_g.reshape(1, dv).astype(F32))


def _merge_kernel(ya_ref, yb_ref, yc_ref, yd_ref, g_ref, wg_ref, wb_ref, wo_ref, x_ref, o_ref, wb_sc, wo_sc):
    dm = x_ref.shape[1]

    @pl.when(pl.program_id(0) == 0)
    def _():
        for n in range(wb_ref.shape[0]):
            wb_sc[n] = wb_ref[n].astype(BF16)
        wo_sc[...] = wo_ref[...].astype(BF16)

    tm = x_ref.shape[0]
    halves = [slice(0, tm // 2), slice(tm // 2, tm)]
    y_refs = (ya_ref, yb_ref, yc_ref, yd_ref)
    gates, branch = [], []
    for rows in halves:
        x = x_ref[rows, :]
        ms = jnp.mean(x * x, axis=-1, keepdims=True)
        u = (x * lax.rsqrt(ms + EPS) * g_ref[...]).astype(BF16)
        gates.append([jnp.dot(u, wg_ref[:, n * dm:(n + 1) * dm], preferred_element_type=F32)
                      for n in range(len(y_refs))])
        branch.append([jnp.dot(y_ref[rows, :], wb_sc[n], preferred_element_type=F32)
                       for n, y_ref in enumerate(y_refs)])
    for rows, gts, brs in zip(halves, gates, branch):
        merged = None
        for gt, br in zip(gts, brs):
            term = jax.nn.sigmoid(gt) * br
            merged = term if merged is None else merged + term
        o_ref[rows, :] = x_ref[rows, :] + jnp.dot(merged.astype(BF16), wo_sc[...], preferred_element_type=F32)


def merge_branches(ys, norm_g, w_gates, w_branch, w_out, x2, *, tm):
    m, dm = x2.shape
    w = BRANCH_WIDTH
    y_spec = pl.BlockSpec((tm, w), lambda i: (i, 0))
    vmem = (2 * (4 * tm * w * 2 + 2 * tm * dm * 4) + N_BRANCHES * dm * dm * 2 + (4 * w * dm + dm * dm) * (4 + 2)
            + 12 * tm * dm * 4)
    return pl.pallas_call(
        _merge_kernel,
        grid=(m // tm,),
        in_specs=[
            y_spec, y_spec, y_spec, y_spec,
            pl.BlockSpec((1, dm), lambda i: (0, 0)),
            pl.BlockSpec((dm, N_BRANCHES * dm), lambda i: (0, 0), pipeline_mode=pl.Buffered(1)),
            pl.BlockSpec((N_BRANCHES, w, dm), lambda i: (0, 0, 0), pipeline_mode=pl.Buffered(1)),
            pl.BlockSpec((dm, dm), lambda i: (0, 0), pipeline_mode=pl.Buffered(1)),
            pl.BlockSpec((tm, dm), lambda i: (i, 0)),
        ],
        out_specs=pl.BlockSpec((tm, dm), lambda i: (i, 0)),
        out_shape=jax.ShapeDtypeStruct((m, dm), F32),
        scratch_shapes=[pltpu.VMEM((N_BRANCHES, w, dm), BF16), pltpu.VMEM((dm, dm), BF16)],
        compiler_params=_params(("arbitrary",), vmem),
        name="merge",
    )(*ys, norm_g.reshape(1, dm).astype(F32), w_gates, w_branch.astype(F32), w_out.astype(F32), x2)


HALO_ROWS = 16


def _conv_ffn_kernel(x_ref, xp_ref, xn_ref, g_ref, wa_ref, wl_ref, cw_ref, cb_ref, wd_ref, gf_ref, o_ref,
                     xe_sc, wd_sc, *, tiles_per_seq, col_chunk, final_norm):
    tm = o_ref.shape[0]
    f = wa_ref.shape[1]
    h = HALO_ROWS
    pos = pl.program_id(0) % tiles_per_seq

    @pl.when(pl.program_id(0) == 0)
    def _():
        wd_sc[...] = wd_ref[...].astype(BF16)

    def norm(x):
        ms = jnp.mean(x * x, axis=-1, keepdims=True)
        return x * lax.rsqrt(ms + EPS) * g_ref[...]

    xe_sc[:h] = jnp.where(pos == 0, 0.0, norm(xp_ref[...])).astype(BF16)
    xe_sc[h:h + tm] = norm(x_ref[...]).astype(BF16)
    xe_sc[h + tm:] = jnp.where(pos == tiles_per_seq - 1, 0.0, norm(xn_ref[...])).astype(BF16)
    k0 = math.sqrt(2.0 / math.pi)
    bounds = list(range(0, f, col_chunk)) + [f]
    chunks = [slice(lo, hi) for lo, hi in zip(bounds[:-1], bounds[1:])]

    def project(cols):
        a_ext = jnp.dot(xe_sc[...], wa_ref[:, cols], preferred_element_type=F32)
        lin = jnp.dot(xe_sc[h:h + tm, :], wl_ref[:, cols], preferred_element_type=F32)
        return a_ext, lin

    y = x_ref[...]
    nxt = project(chunks[0])
    for n, cols in enumerate(chunks):
        a_ext, lin = nxt
        if n + 1 < len(chunks):
            nxt = project(chunks[n + 1])
        cw = cw_ref[:, cols]
        c = (a_ext[h - 1:h - 1 + tm] * cw[0:1] + a_ext[h:h + tm] * cw[1:2] + a_ext[h + 1:h + 1 + tm] * cw[2:3]
             + cb_ref[:, cols])
        t = jnp.tanh(c * ((c * c) * (k0 * 0.044715) + k0))
        hmid = ((c + c * t) * lin).astype(BF16)
        y = y + jnp.dot(hmid, wd_sc[cols, :], preferred_element_type=F32)
    if final_norm:
        ms = jnp.mean(y * y, axis=-1, keepdims=True)
        y = y * lax.rsqrt(ms + EPS) * gf_ref[...]
    o_ref[...] = y


def conv_ffn(x2, g, w_up, conv_w, conv_b, w_down, final_g, *, s, tm, col_chunk, final_norm):
    m, dm = x2.shape
    f = w_up.shape[1] // 2
    tiles_per_seq = s // tm
    hb = tm // HALO_ROWS
    n_halo = m // HALO_ROWS
    vmem = (2 * 2 * tm * dm * 4 + 2 * dm * f * 2 + f * dm * (4 + 2) + (tm + 2 * HALO_ROWS) * dm * 2
            + 6 * tm * col_chunk * 4 + 2 * tm * dm * 4)
    return pl.pallas_call(
        functools.partial(_conv_ffn_kernel, tiles_per_seq=tiles_per_seq, col_chunk=col_chunk, final_norm=final_norm),
        grid=(m // tm,),
        in_specs=[
            pl.BlockSpec((tm, dm), lambda i: (i, 0)),
            pl.BlockSpec((HALO_ROWS, dm), lambda i: (jnp.maximum(i * hb - 1, 0), 0)),
            pl.BlockSpec((HALO_ROWS, dm), lambda i: (jnp.minimum((i + 1) * hb, n_halo - 1), 0)),
            pl.BlockSpec((1, dm), lambda i: (0, 0)),
            pl.BlockSpec((dm, f), lambda i: (0, 0), pipeline_mode=pl.Buffered(1)),
            pl.BlockSpec((dm, f), lambda i: (0, 1), pipeline_mode=pl.Buffered(1)),
            pl.BlockSpec((CONV_W, f), lambda i: (0, 0)),
            pl.BlockSpec((1, f), lambda i: (0, 0)),
            pl.BlockSpec((f, dm), lambda i: (0, 0), pipeline_mode=pl.Buffered(1)),
            pl.BlockSpec((1, dm), lambda i: (0, 0)),
        ],
        out_specs=pl.BlockSpec((tm, dm), lambda i: (i, 0)),
        out_shape=jax.ShapeDtypeStruct((m, dm), F32),
        scratch_shapes=[pltpu.VMEM((tm + 2 * HALO_ROWS, dm), BF16), pltpu.VMEM((f, dm), BF16)],
        compiler_params=_params(("arbitrary",), vmem),
        name="conv_ffn",
    )(x2, x2, x2, g.reshape(1, dm).astype(F32), w_up, w_up, conv_w.astype(F32), conv_b.reshape(1, f).astype(F32),
      w_down.astype(F32), final_g.reshape(1, dm).astype(F32))


def _arrange_w_in(w):
    widths = (512, 512, 128, 128, 512, 512, 512, 512, 16, 512, 512, 512, 4096)
    offs = [0]
    for wd in widths:
        offs.append(offs[-1] + wd)
    seg = lambda n: w[:, offs[n]:offs[n + 1]]
    a, bq, bk, bv, cq, ck, cv, co, cgate, dq, dk, dv, gates = (seg(n) for n in range(len(widths)))
    main = jnp.concatenate([a, bq, cq, ck, cv, co, dq, dk, dv, bk, bv], axis=1).astype(BF16)
    gate = jnp.pad(cgate, ((0, 0), (0, V7X_LANES - cgate.shape[1]))).astype(BF16)
    return main, gate, gates.astype(BF16)


def kernel(x, norm_mix_g, w_in, mlstm_gate_bias, qk_norm_g, mlstm_norm_g, diff_lambda, diff_norm_g, rel_bias,
           w_branch, w_out, norm_ffn_g, w_up, conv_w, conv_b, w_down, final_norm_g):
    b, s, dm = x.shape
    depth = w_in.shape[0]
    m = b * s
    d_ff = w_down.shape[1]
    L = MLSTM_CHUNK
    tl = TILES

    bd, dft_hi, dft_lo = fourier_tables(s)
    cos2, sin2 = rope_tables(s)
    bias_tiles, bias_far = diff_bias_tiles(rel_bias, tl["diff_tile"])

    x2 = x.reshape(m, dm)
    for layer in range(depth):
        w_main, w_gate, w_branch_gates = _arrange_w_in(w_in[layer])
        p2, cgate = norm_matmul(x2, norm_mix_g[layer], w_main, w_gate, tm=tl["proj_rows"], tn=P_WIDTH)
        p3 = p2.reshape(b, s, P_WIDTH)

        y_a = fourier_mix(p3, bd, dft_hi, dft_lo, tm=tl["fourier_rows"])

        q_b, k_b, v_b = gqa_prep(p2, qk_norm_g[layer], cos2, sin2, s=s, tm=tl["gqa_prep_rows"])
        kt_b = jnp.swapaxes(k_b.reshape(b, s, GQA_KV_HEADS * HEAD_DIM), 1, 2)
        y_b = gqa_attention(q_b.reshape(b, s, -1), kt_b, v_b.reshape(b, s, -1),
                            tq=tl["gqa_q_rows"], tk=tl["gqa_keys"])

        gates5 = jnp.transpose(cgate[:, :4 * MLSTM_HEADS].reshape(b, s, 4, MLSTM_HEADS), (0, 2, 3, 1))
        gates5 = gates5.reshape(b, 4, MLSTM_HEADS, s // L, L)
        y_c = mlstm_branch(p3, gates5, mlstm_gate_bias[layer], mlstm_norm_g[layer],
                           heads_per_block=tl["mlstm_heads"])

        y_d = diff_attention(p3, bias_tiles, bias_far, diff_lambda[layer], diff_norm_g[layer],
                             t=tl["diff_tile"], rows_per_item=tl["diff_q_rows"],
                             tiles_per_item=tl["diff_key_tiles"], layer_number=layer + 1)

        ys = [y.reshape(m, BRANCH_WIDTH) for y in (y_a, y_b, y_c, y_d)]
        x2 = merge_branches(ys, norm_mix_g[layer], w_branch_gates, w_branch[layer], w_out[layer], x2,
                            tm=tl["merge_rows"])

        half_lin = jnp.concatenate([jnp.ones((d_ff,), F32), jnp.full((d_ff,), 0.5, F32)])
        x2 = conv_ffn(x2, norm_ffn_g[layer], (w_up[layer] * half_lin).astype(BF16), conv_w[layer], conv_b[layer],
                      w_down[layer], final_norm_g, s=s, tm=tl["ffn_rows"], col_chunk=tl["ffn_cols"],
                      final_norm=(layer == depth - 1))
    return x2.reshape(b, s, dm)
```

```python
import functools
import math

import jax
import jax.numpy as jnp
from jax import lax
from jax.experimental import pallas as pl
from jax.experimental.pallas import tpu as pltpu

F32 = jnp.float32
BF16 = jnp.bfloat16

GRID_W = 64
HEAD_DIM = 64
BRANCH_WIDTH = 512
N_BRANCHES = 4
FOURIER_GROUP_DIM = 64
GQA_Q_HEADS = 8
GQA_KV_HEADS = 2
MLSTM_HEADS = 4
MLSTM_HEAD_DIM = 128
MLSTM_CHUNK = 128
DIFF_HEADS = 4
DIFF_QK_DIM = 64
DIFF_V_DIM = 128
REL_BUCKETS = 32
REL_MAX_DIST = 128
CONV_W = 3
ROPE_BASE = 10000.0
EPS = 1e-6
LOG2E = math.log2(math.e)

V7X_LANES = 128
V7X_VMEM_BYTES = 64 * 1024 * 1024
V7X_VMEM_CAP = V7X_VMEM_BYTES - 8 * 1024 * 1024

TILES = dict(
    proj_rows=512,
    fourier_rows=512,
    gqa_prep_rows=512,
    gqa_q_rows=512, gqa_keys=1024,
    diff_tile=512, diff_q_rows=256, diff_key_tiles=4,
    mlstm_heads=2,
    merge_rows=512,
    ffn_rows=512, ffn_cols=1024,
)
TRANSPOSE_ROWS = 512

OFF_A = 0
OFF_BQ = 512
OFF_CQ = 1024
OFF_CK = 1536
OFF_CV = 2048
OFF_CO = 2560
OFF_DQ = 3072
OFF_DK = 3584
OFF_DV = 4096
OFF_BKV = 4608
P_WIDTH = 4864


def _params(sem, vmem_bytes):
    limit = int(min(max(vmem_bytes * 3 // 2 + (4 << 20), 16 << 20), V7X_VMEM_CAP))
    return pltpu.CompilerParams(dimension_semantics=sem, vmem_limit_bytes=limit)


def _norm_mm_kernel(x_ref, g_ref, w_ref, o_ref, xn_ref):
    @pl.when(pl.program_id(1) == 0)
    def _():
        x = x_ref[...]
        ms = jnp.mean(x * x, axis=-1, keepdims=True)
        xn_ref[...] = (x * lax.rsqrt(ms + EPS) * g_ref[...]).astype(BF16)

    o_ref[...] = jnp.dot(xn_ref[...], w_ref[...], preferred_element_type=F32).astype(o_ref.dtype)


def _norm_mm_gate_kernel(x_ref, g_ref, w_ref, wg_ref, o_ref, og_ref, xn_ref):
    @pl.when(pl.program_id(1) == 0)
    def _():
        x = x_ref[...]
        ms = jnp.mean(x * x, axis=-1, keepdims=True)
        xn = (x * lax.rsqrt(ms + EPS) * g_ref[...]).astype(BF16)
        xn_ref[...] = xn
        og_ref[...] = jnp.dot(xn, wg_ref[...], preferred_element_type=F32)

    o_ref[...] = jnp.dot(xn_ref[...], w_ref[...], preferred_element_type=F32).astype(o_ref.dtype)


def norm_matmul(x, g, w, w_gate=None, *, tm, tn):
    m, k = x.shape
    n = w.shape[1]
    grid = (m // tm, n // tn)
    vmem = 2 * tm * k * 4 + tm * k * 2 + 2 * k * tn * 2 + 2 * tm * tn * 2 + 4 * tm * k
    x_spec = pl.BlockSpec((tm, k), lambda i, j: (i, 0))
    g_spec = pl.BlockSpec((1, k), lambda i, j: (0, 0))
    w_spec = pl.BlockSpec((k, tn), lambda i, j: (0, j))
    o_spec = pl.BlockSpec((tm, tn), lambda i, j: (i, j))
    scratch = [pltpu.VMEM((tm, k), BF16)]
    g2 = g.reshape(1, k).astype(F32)
    if w_gate is None:
        return pl.pallas_call(
            _norm_mm_kernel,
            grid=grid,
            in_specs=[x_spec, g_spec, w_spec],
            out_specs=o_spec,
            out_shape=jax.ShapeDtypeStruct((m, n), BF16),
            scratch_shapes=scratch,
            compiler_params=_params(("parallel", "arbitrary"), vmem),
            name="norm_matmul",
        )(x, g2, w)
    ng = w_gate.shape[1]
    return pl.pallas_call(
        _norm_mm_gate_kernel,
        grid=grid,
        in_specs=[x_spec, g_spec, w_spec, pl.BlockSpec((k, ng), lambda i, j: (0, 0))],
        out_specs=[o_spec, pl.BlockSpec((tm, ng), lambda i, j: (i, 0))],
        out_shape=[jax.ShapeDtypeStruct((m, n), BF16), jax.ShapeDtypeStruct((m, ng), F32)],
        scratch_shapes=scratch,
        compiler_params=_params(("parallel", "arbitrary"), vmem),
        name="norm_matmul_gate",
    )(x, g2, w, w_gate)


FOURIER_HALO = 16


def _fourier_kernel(a_ref, bd_ref, hi_ref, lo_ref, flip_ref, o_ref, z_ref, ext_ref, *, tm, row_chunk):
    s = a_ref.shape[1]
    w = a_ref.shape[2]
    n_lo = lo_ref.shape[1]
    i = pl.program_id(1)

    @pl.when(i == 0)
    def _():
        for r in range(0, s, row_chunk):
            a = a_ref[0, r:r + row_chunk, :]
            zc = jnp.dot(a, bd_ref[...], preferred_element_type=F32)
            z_ref[r:r + row_chunk, :] = zc[:, :w].astype(BF16)
            z_ref[s + r:s + r + row_chunk, :] = zc[:, w:].astype(BF16)

    cl, sl = lo_ref[0], lo_ref[1]
    for r in range(tm // n_lo + 1):
        rows = n_lo if r < tm // n_lo else ext_ref.shape[0] - tm
        k1 = i * (tm // n_lo) + r
        ch = hi_ref[0, pl.ds(k1, 1), :]
        nsh = hi_ref[1, pl.ds(k1, 1), :]
        ext_ref[r * n_lo:r * n_lo + rows, :s] = (cl[:rows] * ch + sl[:rows] * nsh).astype(BF16)
        ext_ref[r * n_lo:r * n_lo + rows, s:] = (cl[:rows] * nsh - sl[:rows] * ch).astype(BF16)
    p = jnp.dot(ext_ref[:, :s], z_ref[:s], preferred_element_type=F32)
    q = jnp.dot(ext_ref[:, s:], z_ref[s:], preferred_element_type=F32)
    lo = pl.multiple_of(i * tm, tm)
    o_ref[0, pl.ds(lo, tm), :] = (p[:tm] + q[:tm]).astype(o_ref.dtype)
    mirrored = (p[1:tm + 1] - q[1:tm + 1]).astype(BF16)
    hi = pl.multiple_of(s - (i + 1) * tm, tm)
    o_ref[0, pl.ds(hi, tm), :] = jnp.dot(flip_ref[...], mirrored, preferred_element_type=F32).astype(o_ref.dtype)


def fourier_tables(s):
    cg = FOURIER_GROUP_DIM
    jj = jnp.arange(cg, dtype=jnp.int32)
    ang_c = (2.0 * math.pi / cg) * ((jj[:, None] * jj[None, :]) % cg).astype(F32)
    eye_g = jnp.eye(BRANCH_WIDTH // cg, dtype=F32)
    bd_c = jnp.kron(eye_g, jnp.cos(ang_c)) * cg ** -0.5
    bd_s = jnp.kron(eye_g, jnp.sin(ang_c)) * cg ** -0.5
    bd = jnp.concatenate([bd_c, bd_s], axis=1).astype(BF16)
    n_lo = s // cg
    nn = jnp.arange(s, dtype=jnp.int32)
    k1 = jnp.arange(cg // 2 + 2, dtype=jnp.int32)
    ang_hi = (2.0 * math.pi / cg) * ((k1[:, None] * nn[None, :]) % cg).astype(F32)
    ll = jnp.arange(n_lo, dtype=jnp.int32)
    ang_lo = (2.0 * math.pi / s) * ((ll[:, None] * nn[None, :]) % s).astype(F32)
    scale = s ** -0.5
    hi = jnp.stack([jnp.cos(ang_hi), -jnp.sin(ang_hi)]) * scale
    lo = jnp.stack([jnp.cos(ang_lo), jnp.sin(ang_lo)])
    return bd, hi, lo


def fourier_mix(p3, bd, hi, lo, *, tm):
    b, s, _ = p3.shape
    w = BRANCH_WIDTH
    n_lo = lo.shape[1]
    h = FOURIER_HALO
    assert (s // 2) % tm == 0 and tm % n_lo == 0 and h <= n_lo
    r = jnp.arange(tm, dtype=jnp.int32)
    flip = (r[:, None] + r[None, :] == tm - 1).astype(BF16)
    vmem = (s * w * 2 + 2 * w * w * 2 + 2 * (hi.size + lo.size) * 4 + (tm + h) * 2 * s * 2 + 2 * s * w * 2
            + 2 * s * w * 2 + 8 * (tm + h) * w * 4 + 8 * n_lo * s * 4)
    return pl.pallas_call(
        functools.partial(_fourier_kernel, tm=tm, row_chunk=min(s, tm)),
        grid=(b, s // 2 // tm),
        in_specs=[
            pl.BlockSpec((1, s, w), lambda bi, i: (bi, 0, OFF_A // w), pipeline_mode=pl.Buffered(1)),
            pl.BlockSpec((w, 2 * w), lambda bi, i: (0, 0), pipeline_mode=pl.Buffered(1)),
            pl.BlockSpec(hi.shape, lambda bi, i: (0, 0, 0)),
            pl.BlockSpec(lo.shape, lambda bi, i: (0, 0, 0)),
            pl.BlockSpec((tm, tm), lambda bi, i: (0, 0)),
        ],
        out_specs=pl.BlockSpec((1, s, w), lambda bi, i: (bi, 0, 0)),
        out_shape=jax.ShapeDtypeStruct((b, s, w), BF16),
        scratch_shapes=[pltpu.VMEM((2 * s, w), BF16), pltpu.VMEM((tm + h, 2 * s), BF16)],
        compiler_params=_params(("parallel", "arbitrary"), vmem),
        name="fourier",
    )(p3, bd, hi, lo, flip)


def rope_tables(s):
    rows = s // GRID_W
    row_id = jnp.repeat(jnp.arange(rows, dtype=F32), GRID_W)
    col_id = jnp.tile(jnp.arange(GRID_W, dtype=F32), rows)
    n_pairs = HEAD_DIM // 4
    inv_freq = ROPE_BASE ** (-jnp.arange(n_pairs, dtype=F32) / n_pairs)
    ang = jnp.concatenate([row_id[:, None] * inv_freq, col_id[:, None] * inv_freq], axis=-1)
    cos, sin = jnp.cos(ang), jnp.sin(ang)
    return jnp.concatenate([cos, cos] * 2, axis=-1), jnp.concatenate([-sin, sin] * 2, axis=-1)


def _norm_rope(x, g, seg, cos2, sin2):
    half = HEAD_DIM // 2
    x2 = x * x
    hi = x2.astype(BF16)
    lo = (x2 - hi.astype(F32)).astype(BF16)
    ms = jnp.dot(hi, seg, preferred_element_type=F32) + jnp.dot(lo, seg, preferred_element_type=F32)
    y = x * lax.rsqrt(ms + EPS) * g
    lane = lax.broadcasted_iota(jnp.int32, (x.shape[0], V7X_LANES), 1)
    first_half = (lane % HEAD_DIM) < half
    outs = []
    for cb in range(x.shape[1] // V7X_LANES):
        yb = y[:, cb * V7X_LANES:(cb + 1) * V7X_LANES]
        rot = jnp.where(first_half, pltpu.roll(yb, V7X_LANES - half, axis=1), pltpu.roll(yb, half, axis=1))
        outs.append(yb * cos2 + rot * sin2)
    return outs[0] if len(outs) == 1 else jnp.concatenate(outs, axis=-1)


def _gqa_prep_kernel(q_ref, kv_ref, gq_ref, gk_ref, seg_ref, cos_ref, sin_ref, qo_ref, ko_ref, vo_ref):
    d = HEAD_DIM
    nk = GQA_KV_HEADS * d
    cos2, sin2 = cos_ref[...], sin_ref[...]
    q = q_ref[...].astype(F32)
    kv = kv_ref[...].astype(F32)
    qo_ref[...] = _norm_rope(q, gq_ref[...], seg_ref[...], cos2, sin2).astype(BF16)
    ko_ref[...] = _norm_rope(kv[:, :nk], gk_ref[...], seg_ref[:nk, :nk], cos2, sin2).astype(BF16)
    v = kv[:, nk:]
    lane = lax.broadcasted_iota(jnp.int32, v.shape, 1)
    ones_col = jnp.where(lane == d, 1.0, 0.0)
    vo_ref[:, :nk] = jnp.where(lane < d, v, ones_col).astype(BF16)
    vo_ref[:, nk:] = jnp.where(lane < d, pltpu.roll(v, d, axis=1), ones_col).astype(BF16)


def gqa_prep(p2, qk_g, cos2, sin2, *, s, tm):
    m = p2.shape[0]
    nq = GQA_Q_HEADS * HEAD_DIM
    nkv = GQA_KV_HEADS * HEAD_DIM
    assert nkv == V7X_LANES
    tiles_per_seq = s // tm
    q_scale = HEAD_DIM ** -0.5 * LOG2E
    gq = jnp.tile(qk_g[0].astype(F32) * q_scale, GQA_Q_HEADS).reshape(1, nq)
    gk = jnp.tile(qk_g[1].astype(F32), GQA_KV_HEADS).reshape(1, nkv)
    seg = jnp.kron(jnp.eye(GQA_Q_HEADS, dtype=F32), jnp.full((HEAD_DIM, HEAD_DIM), 1.0 / HEAD_DIM, F32)).astype(BF16)
    return pl.pallas_call(
        _gqa_prep_kernel,
        grid=(m // tm,),
        in_specs=[
            pl.BlockSpec((tm, nq), lambda i: (i, OFF_BQ // nq)),
            pl.BlockSpec((tm, 2 * nkv), lambda i: (i, OFF_BKV // (2 * nkv))),
            pl.BlockSpec((1, nq), lambda i: (0, 0)),
            pl.BlockSpec((1, nkv), lambda i: (0, 0)),
            pl.BlockSpec((nq, nq), lambda i: (0, 0)),
            pl.BlockSpec((tm, V7X_LANES), lambda i: (i % tiles_per_seq, 0)),
            pl.BlockSpec((tm, V7X_LANES), lambda i: (i % tiles_per_seq, 0)),
        ],
        out_specs=[
            pl.BlockSpec((tm, nq), lambda i: (i, 0)),
            pl.BlockSpec((tm, nkv), lambda i: (i, 0)),
            pl.BlockSpec((tm, 2 * nkv), lambda i: (i, 0)),
        ],
        out_shape=[
            jax.ShapeDtypeStruct((m, nq), BF16),
            jax.ShapeDtypeStruct((m, nkv), BF16),
            jax.ShapeDtypeStruct((m, 2 * nkv), BF16),
        ],
        compiler_params=_params(("parallel",), 16 * tm * nq * 4),
        name="gqa_prep",
    )(p2, p2, gq, gk, seg, cos2, sin2)


def _gqa_attn_kernel(q_ref, kt_ref, v_ref, o_ref, q_sc, m_ref, acc_ref, *, tk):
    d = HEAD_DIM
    tq = q_ref.shape[1]
    grp = q_ref.shape[2] // d
    s = kt_ref.shape[2]
    for g in range(grp):
        q_sc[g * tq:(g + 1) * tq, :] = q_ref[0, :, g * d:(g + 1) * d]

    def qk(item):
        c, g = item
        return jnp.dot(q_sc[g * tq:(g + 1) * tq, :], kt_ref[0, :, c * tk:(c + 1) * tk],
                       preferred_element_type=F32)

    items = [(c, g) for c in range(s // tk) for g in range(grp)]
    sc_next = qk(items[0])
    for n, (c, g) in enumerate(items):
        rows = slice(g * tq, (g + 1) * tq)
        sc = sc_next
        if n + 1 < len(items):
            sc_next = qk(items[n + 1])
        v = v_ref[0, c * tk:(c + 1) * tk, :]
        row_max = jnp.max(sc, axis=-1, keepdims=True)
        if c == 0:
            m_new = jnp.broadcast_to(row_max, (tq, V7X_LANES))
        else:
            m_prev = m_ref[rows, :]
            m_new = jnp.maximum(m_prev, row_max)
        p = jnp.exp2((sc - pltpu.repeat(m_new, tk // V7X_LANES, axis=1)).astype(BF16))
        pv = jnp.dot(p, v, preferred_element_type=F32)
        if c == 0:
            acc_ref[rows, :] = pv
        else:
            acc_ref[rows, :] = acc_ref[rows, :] * jnp.exp2(m_prev - m_new) + pv
        m_ref[rows, :] = m_new
    for g in range(grp):
        acc = acc_ref[g * tq:(g + 1) * tq, :]
        o_ref[0, :, g * d:(g + 1) * d] = (acc[:, :d] / acc[:, d:d + 1]).astype(o_ref.dtype)


def gqa_attention(q3, kt3, v3, *, tq, tk):
    b, s, nq = q3.shape
    d = HEAD_DIM
    grp = GQA_Q_HEADS // GQA_KV_HEADS
    mrows = grp * tq
    vmem = (2 * (tq * grp * d * 2 + d * s * 2 + s * 128 * 2 + tq * grp * d * 2) + 3 * mrows * 128 * 4
            + 12 * tq * tk * 4)
    return pl.pallas_call(
        functools.partial(_gqa_attn_kernel, tk=tk),
        grid=(b, GQA_KV_HEADS, s // tq),
        in_specs=[
            pl.BlockSpec((1, tq, grp * d), lambda bi, kv, i: (bi, i, kv)),
            pl.BlockSpec((1, d, s), lambda bi, kv, i: (bi, kv, 0)),
            pl.BlockSpec((1, s, 2 * d), lambda bi, kv, i: (bi, 0, kv)),
        ],
        out_specs=pl.BlockSpec((1, tq, grp * d), lambda bi, kv, i: (bi, i, kv)),
        out_shape=jax.ShapeDtypeStruct((b, s, nq), BF16),
        scratch_shapes=[pltpu.VMEM((mrows, d), BF16), pltpu.VMEM((mrows, V7X_LANES), F32),
                        pltpu.VMEM((mrows, 2 * d), F32)],
        compiler_params=_params(("parallel", "parallel", "parallel"), vmem),
        name="gqa_attn",
    )(q3, kt3, v3)


def _mlstm_step(chains, ms, q_ref, kt_sc, v_ref, r_sc, cm_sc, b_sc, st_sc, h_sc):
    L = MLSTM_CHUNK
    dh = MLSTM_HEAD_DIM
    assert L == dh
    scale = dh ** -0.5
    row_i = lax.broadcasted_iota(jnp.int32, (L, L), 0)
    col_i = lax.broadcasted_iota(jnp.int32, (L, L), 1)

    def col(x_row):
        return jnp.transpose(jnp.broadcast_to(x_row, (L, L)))

    pre = []
    for (hh, direction, c), m in zip(chains, ms):
        off = pl.multiple_of(c * L, L)
        lanes = slice(hh * dh, (hh + 1) * dh)
        q = q_ref[0, pl.ds(off, L), lanes]
        kt = kt_sc[hh, :, pl.ds(off, L)]
        v = v_ref[0, pl.ds(off, L), lanes]
        r_row = r_sc[hh, direction, pl.ds(c, 1), :]
        cm_row = cm_sc[hh, direction, pl.ds(c, 1), :]
        b_row = b_sc[hh, direction, pl.ds(c, 1), :]
        rmax = jnp.max(r_row, axis=-1, keepdims=True)
        btot = b_row[:, L - 1:L] if direction == 0 else b_row[:, 0:1]
        cmat = jnp.maximum(m, col(cm_row))
        mask = (row_i >= col_i) if direction == 0 else (row_i <= col_i)
        c_last = jnp.maximum(m, rmax)
        w_state = jnp.exp(r_row - c_last) * scale
        pre.append(dict(
            off=off, lanes=lanes, q=q, kt=kt,
            v_aug=jnp.concatenate([v, jnp.ones((L, dh), BF16)], axis=-1),
            w_intra=jnp.where(mask, jnp.exp(r_row - cmat), 0.0) * scale,
            w_inter=jnp.exp(m - cmat),
            den_floor=jnp.exp(-(col(b_row) + cmat)),
            decay=jnp.exp(m - c_last),
            kt_w=(kt.astype(F32) * w_state).astype(BF16),
            m_new=btot + c_last,
        ))
    s_raw = [jnp.dot(p["q"], p["kt"], preferred_element_type=F32) for p in pre]
    states = [st_sc[hh, direction] for hh, direction, _ in chains]
    inter = [jnp.dot(p["q"], st.astype(BF16), preferred_element_type=F32) for p, st in zip(pre, states)]
    upd = [jnp.dot(p["kt_w"], p["v_aug"], preferred_element_type=F32) for p in pre]
    for n, ((hh, direction, _), p) in enumerate(zip(chains, pre)):
        st_sc[hh, direction] = p["decay"] * states[n] + upd[n]
        intra = jnp.dot((s_raw[n] * p["w_intra"]).astype(BF16), p["v_aug"], preferred_element_type=F32)
        h_aug = jnp.concatenate([p["w_inter"], p["w_inter"]], axis=-1) * inter[n] + intra
        h_sc[hh, direction, pl.ds(p["off"], L), :] = h_aug[:, :dh] / jnp.maximum(jnp.abs(h_aug[:, dh:]), p["den_floor"])
    return [p["m_new"] for p in pre]


def _mlstm_kernel(bias_ref, q_ref, k_ref, v_ref, o_ref, gate_ref, g_ref, y_ref,
                  r_sc, cm_sc, b_sc, h_sc, st_sc, kt_sc):
    L = MLSTM_CHUNK
    dh = MLSTM_HEAD_DIM
    hpb = q_ref.shape[2] // dh
    head0 = pl.program_id(1) * hpb
    nc = q_ref.shape[1] // L
    lane = lax.broadcasted_iota(jnp.int32, (nc, L), 1)
    shifts = [1 << t for t in range(int(math.log2(L)))]
    for hh in range(hpb):
        for d in range(2):
            i_pre = gate_ref[0, 2 * d, hh] + bias_ref[2 * d, head0 + hh]
            f_pre = gate_ref[0, 2 * d + 1, hh] + bias_ref[2 * d + 1, head0 + hh]
            logf = jnp.minimum(f_pre, 0.0) - jnp.log1p(jnp.exp(-jnp.abs(f_pre)))
            bc = logf
            for sh in shifts:
                if d == 0:
                    bc = bc + jnp.where(lane >= sh, pltpu.roll(bc, sh, axis=1), 0.0)
                else:
                    bc = bc + jnp.where(lane < L - sh, pltpu.roll(bc, L - sh, axis=1), 0.0)
            r = i_pre - bc
            cm = r
            for sh in shifts:
                if d == 0:
                    cm = jnp.maximum(cm, jnp.where(lane >= sh, pltpu.roll(cm, sh, axis=1), -jnp.inf))
                else:
                    cm = jnp.maximum(cm, jnp.where(lane < L - sh, pltpu.roll(cm, L - sh, axis=1), -jnp.inf))
            r_sc[hh, d] = r
            cm_sc[hh, d] = cm
            b_sc[hh, d] = bc
    st_sc[...] = jnp.zeros(st_sc.shape, F32)
    tr = TRANSPOSE_ROWS
    for hh in range(hpb):
        for r in range(0, q_ref.shape[1], tr):
            kt_sc[hh, :, r:r + tr] = jnp.transpose(k_ref[0, r:r + tr, hh * dh:(hh + 1) * dh].astype(F32)).astype(BF16)

    def body(c, ms):
        chains = [(hh, d, c if d == 0 else nc - 1 - c) for hh in range(hpb) for d in range(2)]
        return tuple(_mlstm_step(chains, ms, q_ref, kt_sc, v_ref, r_sc, cm_sc, b_sc, st_sc, h_sc))

    lax.fori_loop(0, nc, body, tuple(jnp.zeros((1, 1), F32) for _ in range(2 * hpb)), unroll=2)
    for hh in range(hpb):
        lanes = slice(hh * dh, (hh + 1) * dh)
        hsum = h_sc[hh, 0] + h_sc[hh, 1]
        ms = jnp.mean(hsum * hsum, axis=-1, keepdims=True)
        y = hsum * lax.rsqrt(ms + EPS) * g_ref[:, lanes]
        y_ref[0, :, lanes] = (jax.nn.sigmoid(o_ref[0, :, lanes].astype(F32)) * y).astype(y_ref.dtype)


def mlstm_branch(p3, gates5, gate_bias, norm_g, *, heads_per_block):
    b, s, _ = p3.shape
    L = MLSTM_CHUNK
    hpb = heads_per_block
    wb = hpb * MLSTM_HEAD_DIM
    nc = s // L
    blk = lambda off: pl.BlockSpec((1, s, wb), lambda bi, h, off=off: (bi, 0, off // wb + h))
    vmem = 2 * 5 * s * wb * 2 + 2 * s * wb * 4 + 6 * hpb * nc * L * 4 + 4 * hpb * wb * wb * 4 + 3 * s * wb * 4
    return pl.pallas_call(
        _mlstm_kernel,
        grid=(b, MLSTM_HEADS // hpb),
        in_specs=[
            pl.BlockSpec(memory_space=pltpu.SMEM),
            blk(OFF_CQ), blk(OFF_CK), blk(OFF_CV), blk(OFF_CO),
            pl.BlockSpec((1, 4, hpb, nc, L), lambda bi, h: (bi, 0, h, 0, 0)),
            pl.BlockSpec((1, wb), lambda bi, h: (0, h)),
        ],
        out_specs=pl.BlockSpec((1, s, wb), lambda bi, h: (bi, 0, h)),
        out_shape=jax.ShapeDtypeStruct((b, s, BRANCH_WIDTH), BF16),
        scratch_shapes=[
            pltpu.VMEM((hpb, 2, nc, L), F32), pltpu.VMEM((hpb, 2, nc, L), F32), pltpu.VMEM((hpb, 2, nc, L), F32),
            pltpu.VMEM((hpb, 2, s, MLSTM_HEAD_DIM), F32),
            pltpu.VMEM((hpb, 2, MLSTM_HEAD_DIM, 2 * MLSTM_HEAD_DIM), F32),
            pltpu.VMEM((hpb, MLSTM_HEAD_DIM, s), BF16),
        ],
        compiler_params=_params(("parallel", "parallel"), vmem),
        name="mlstm",
    )(gate_bias.astype(F32), p3, p3, p3, p3, gates5, norm_g.reshape(1, BRANCH_WIDTH).astype(F32))


def _rel_bucket(rel):
    half = REL_BUCKETS // 2
    max_exact = half // 2
    ret = jnp.where(rel > 0, half, 0)
    n = jnp.abs(rel)
    nf = jnp.maximum(n, 1).astype(F32)
    large = max_exact + (jnp.log(nf / max_exact) / math.log(REL_MAX_DIST / max_exact) * (half - max_exact)).astype(jnp.int32)
    large = jnp.minimum(large, half - 1)
    return ret + jnp.where(n < max_exact, n, large)


def diff_bias_tiles(rel_bias, t):
    assert t >= REL_MAX_DIST
    k = jnp.arange(2 * t, dtype=jnp.int32)
    rel = jnp.arange(-2, 3, dtype=jnp.int32)[:, None] * t + jnp.where(k < t, k, k - 2 * t)[None, :]
    onehot = (_rel_bucket(rel)[:, :, None] == jnp.arange(REL_BUCKETS, dtype=jnp.int32)).astype(F32)
    period = jnp.einsum('dkb,bh->hdk', onehot, rel_bias.astype(F32) * LOG2E, precision=lax.Precision.HIGHEST)
    far = period[:, 0::4, 0]
    return period, far


def _diff_attn_kernel(far_ref, q_ref, k_ref, v_ref, period_ref, lam_ref, g_ref, o_ref,
                      q_sc, kt_sc, vaug_sc, bias_sc, sc_buf, m_ref, acc_ref, *, t, lam_init):
    dq = DIFF_QK_DIM
    dv = DIFF_V_DIM
    s = k_ref.shape[1]
    nt = s // t
    head = pl.program_id(1)
    ahead = sc_buf.shape[0]
    rb = sc_buf.shape[1]
    grp = sc_buf.shape[2] // t
    wrap = (grp - 1) * t
    assert nt % grp == 0 and t % rb == 0 and nt >= 4

    tr = max(t, TRANSPOSE_ROWS)
    for r in range(0, s, tr):
        kt_sc[:, r:r + tr] = jnp.transpose(k_ref[0, r:r + tr, :].astype(F32)).astype(BF16)
    kt_sc[:, s:] = kt_sc[:, :wrap]
    vaug_sc[:s, :dv] = v_ref[0]
    vaug_sc[s:, :dv] = v_ref[0, :wrap, :]
    lane = lax.broadcasted_iota(jnp.int32, (s + wrap, dv), 1)
    vaug_sc[:, dv:] = jnp.where(lane == 0, 1.0, 0.0).astype(BF16)
    for dl in range(period_ref.shape[1]):
        full = jnp.broadcast_to(period_ref[0, dl:dl + 1, :], (t, 2 * t))
        bias_sc[dl] = pltpu.roll(full, 0, axis=1, stride=1, stride_axis=0)[:, :t]
    q_scale = dq ** -0.5 * LOG2E
    q_all = (q_ref[0].astype(F32) * q_scale).astype(BF16)
    q_sc[0] = q_all[:, :dq]
    q_sc[1] = q_all[:, dq:]
    lp = lam_ref[...]
    lam = (jnp.exp(jnp.sum(lp[0:1] * lp[1:2], axis=-1, keepdims=True))
           - jnp.exp(jnp.sum(lp[2:3] * lp[3:4], axis=-1, keepdims=True)) + lam_init)

    def key_tile(i, delta):
        j = lax.rem(i + delta, nt)
        return j, pl.multiple_of(j * t, t)

    def qk(i, item):
        pair, mp, hf = item
        row0 = pl.multiple_of(i * t + hf * rb, rb)
        _, off = key_tile(i, grp * pair)
        return jnp.dot(q_sc[mp, pl.ds(row0, rb), :], kt_sc[mp * dq:(mp + 1) * dq, pl.ds(off, grp * t)],
                       preferred_element_type=F32)

    items = [(pair, mp, hf) for pair in range(nt // grp) for mp in range(2) for hf in range(t // rb)]
    for a in range(ahead):
        sc_buf[a] = qk(0, items[a])

    def tile_body(i, carry):
        pending = [sc_buf[a] for a in range(ahead)]
        for n, (pair, mp, hf) in enumerate(items):
            sc = pending.pop(0)
            if n + ahead < len(items):
                pending.append(qk(i, items[n + ahead]))
            else:
                pending.append(qk(jnp.minimum(i + 1, nt - 1), items[n + ahead - len(items)]))
            rows = slice(hf * rb, (hf + 1) * rb)
            scs = [sc[:, p * t:(p + 1) * t] for p in range(grp)]
            _, off = key_tile(i, grp * pair)
            shifts, row_max = [], None
            for piece, delta in enumerate(range(grp * pair, grp * pair + grp)):
                j, _ = key_tile(i, delta)
                if delta in (0, 1, nt - 1):
                    tile = 2 if delta == 0 else jnp.clip(j - i, -2, 2) + 2
                    scs[piece] = scs[piece] + bias_sc[tile, rows, :]
                    shifts.append(None)
                    rm = jnp.max(scs[piece], axis=-1, keepdims=True)
                else:
                    shifts.append(jnp.where(j > i, far_ref[head, 1], far_ref[head, 0]))
                    rm = jnp.max(scs[piece], axis=-1, keepdims=True) + shifts[-1]
                row_max = rm if row_max is None else jnp.maximum(row_max, rm)
            if pair == 0:
                m_new = jnp.broadcast_to(row_max, (rb, V7X_LANES))
            else:
                m_prev = m_ref[mp, rows, :]
                m_new = jnp.maximum(m_prev, row_max)
            ps = []
            for piece in range(grp):
                m_sub = m_new if shifts[piece] is None else m_new - shifts[piece]
                ps.append(jnp.exp2((scs[piece] - pltpu.repeat(m_sub, t // V7X_LANES, axis=1)).astype(BF16)))
            pv = jnp.dot(jnp.concatenate(ps, axis=1), vaug_sc[pl.ds(off, grp * t), :], preferred_element_type=F32)
            if pair == 0:
                acc_ref[mp, rows, :] = pv
            else:
                alpha = jnp.exp2(m_prev - m_new)
                acc_ref[mp, rows, :] = acc_ref[mp, rows, :] * pltpu.repeat(alpha, 2 * dv // V7X_LANES, axis=1) + pv
            m_ref[mp, rows, :] = m_new
        for a in range(ahead):
            sc_buf[a] = pending[a]
        o0 = acc_ref[0, :, :dv] / acc_ref[0, :, dv:dv + 1]
        o1 = acc_ref[1, :, :dv] / acc_ref[1, :, dv:dv + 1]
        o = o0 - lam * o1
        ms = jnp.mean(o * o, axis=-1, keepdims=True)
        row0 = pl.multiple_of(i * t, t)
        o_ref[0, pl.ds(row0, t), :] = (o * lax.rsqrt(ms + EPS) * g_ref[...] * (1.0 - lam_init)).astype(o_ref.dtype)
        return carry

    lax.fori_loop(0, nt, tile_body, 0, unroll=2)


def diff_attention(p3, bias_period, bias_far, lam_params, sub_g, *, t, rows_per_item, tiles_per_item, layer_number):
    b, s, _ = p3.shape
    dv = DIFF_V_DIM
    n_off = bias_period.shape[1]
    assert s // t >= 4, "tiles 2 .. s/t-2 steps away from the query tile must all be beyond REL_MAX_DIST"
    lam_init = 0.8 - 0.6 * math.exp(-0.3 * (layer_number - 1))
    sw = s + (tiles_per_item - 1) * t
    item = rows_per_item * tiles_per_item * t
    vmem = (2 * (s * 128 * 2 + 128 * s * 2 + s * dv * 2 + s * dv * 2) + n_off * t * t * 4 + sw * (2 * dv + 128) * 2
            + 2 * s * 128 * 2 + item * 4 + 2 * t * 128 * 4 * 3 + 8 * item * 4)
    return pl.pallas_call(
        functools.partial(_diff_attn_kernel, t=t, lam_init=lam_init),
        grid=(b, DIFF_HEADS),
        in_specs=[
            pl.BlockSpec(memory_space=pltpu.SMEM),
            pl.BlockSpec((1, s, 128), lambda bi, h: (bi, 0, OFF_DQ // 128 + h)),
            pl.BlockSpec((1, s, 128), lambda bi, h: (bi, 0, OFF_DK // 128 + h)),
            pl.BlockSpec((1, s, dv), lambda bi, h: (bi, 0, OFF_DV // dv + h)),
            pl.BlockSpec((1, n_off, 2 * t), lambda bi, h: (h, 0, 0)),
            pl.BlockSpec((4, DIFF_QK_DIM), lambda bi, h: (0, 0)),
            pl.BlockSpec((1, dv), lambda bi, h: (0, 0)),
        ],
        out_specs=pl.BlockSpec((1, s, dv), lambda bi, h: (bi, 0, h)),
        out_shape=jax.ShapeDtypeStruct((b, s, BRANCH_WIDTH), BF16),
        scratch_shapes=[pltpu.VMEM((2, s, DIFF_QK_DIM), BF16), pltpu.VMEM((2 * DIFF_QK_DIM, sw), BF16),
                        pltpu.VMEM((sw, 2 * dv), BF16),
                        pltpu.VMEM((n_off, t, t), F32), pltpu.VMEM((1, rows_per_item, tiles_per_item * t), F32),
                        pltpu.VMEM((2, t, V7X_LANES), F32), pltpu.VMEM((2, t, 2 * dv), F32)],
        compiler_params=_params(("parallel", "parallel"), vmem),
        name="diff_attn",
    )(bias_far, p3, p3, p3, bias_period, lam_params.astype(F32), sub_g.reshape(1, dv).astype(F32))


def _merge_kernel(ya_ref, yb_ref, yc_ref, yd_ref, g_ref, wg_ref, wb_ref, wo_ref, x_ref, o_ref, wb_sc, wo_sc):
    dm = x_ref.shape[1]

    @pl.when(pl.program_id(0) == 0)
    def _():
        for n in range(wb_ref.shape[0]):
            wb_sc[n] = wb_ref[n].astype(BF16)
        wo_sc[...] = wo_ref[...].astype(BF16)

    tm = x_ref.shape[0]
    halves = [slice(0, tm // 2), slice(tm // 2, tm)]
    y_refs = (ya_ref, yb_ref, yc_ref, yd_ref)
    gates, branch = [], []
    for rows in halves:
        x = x_ref[rows, :]
        ms = jnp.mean(x * x, axis=-1, keepdims=True)
        u = (x * lax.rsqrt(ms + EPS) * g_ref[...]).astype(BF16)
        gates.append([jnp.dot(u, wg_ref[:, n * dm:(n + 1) * dm], preferred_element_type=F32)
                      for n in range(len(y_refs))])
        branch.append([jnp.dot(y_ref[rows, :], wb_sc[n], preferred_element_type=F32)
                       for n, y_ref in enumerate(y_refs)])
    for rows, gts, brs in zip(halves, gates, branch):
        merged = None
        for gt, br in zip(gts, brs):
            term = jax.nn.sigmoid(gt) * br
            merged = term if merged is None else merged + term
        o_ref[rows, :] = x_ref[rows, :] + jnp.dot(merged.astype(BF16), wo_sc[...], preferred_element_type=F32)


def merge_branches(ys, norm_g, w_gates, w_branch, w_out, x2, *, tm):
    m, dm = x2.shape
    w = BRANCH_WIDTH
    y_spec = pl.BlockSpec((tm, w), lambda i: (i, 0))
    vmem = (2 * (4 * tm * w * 2 + 2 * tm * dm * 4) + N_BRANCHES * dm * dm * 2 + (4 * w * dm + dm * dm) * (4 + 2)
            + 12 * tm * dm * 4)
    return pl.pallas_call(
        _merge_kernel,
        grid=(m // tm,),
        in_specs=[
            y_spec, y_spec, y_spec, y_spec,
            pl.BlockSpec((1, dm), lambda i: (0, 0)),
            pl.BlockSpec((dm, N_BRANCHES * dm), lambda i: (0, 0), pipeline_mode=pl.Buffered(1)),
            pl.BlockSpec((N_BRANCHES, w, dm), lambda i: (0, 0, 0), pipeline_mode=pl.Buffered(1)),
            pl.BlockSpec((dm, dm), lambda i: (0, 0), pipeline_mode=pl.Buffered(1)),
            pl.BlockSpec((tm, dm), lambda i: (i, 0)),
        ],
        out_specs=pl.BlockSpec((tm, dm), lambda i: (i, 0)),
        out_shape=jax.ShapeDtypeStruct((m, dm), F32),
        scratch_shapes=[pltpu.VMEM((N_BRANCHES, w, dm), BF16), pltpu.VMEM((dm, dm), BF16)],
        compiler_params=_params(("arbitrary",), vmem),
        name="merge",
    )(*ys, norm_g.reshape(1, dm).astype(F32), w_gates, w_branch.astype(F32), w_out.astype(F32), x2)


HALO_ROWS = 16


def _conv_ffn_kernel(x_ref, xp_ref, xn_ref, g_ref, wa_ref, wl_ref, cw_ref, cb_ref, wd_ref, gf_ref, o_ref,
                     xe_sc, wd_sc, *, tiles_per_seq, col_chunk, final_norm):
    tm = o_ref.shape[0]
    f = wa_ref.shape[1]
    h = HALO_ROWS
    pos = pl.program_id(0) % tiles_per_seq

    @pl.when(pl.program_id(0) == 0)
    def _():
        wd_sc[...] = wd_ref[...].astype(BF16)

    def norm(x):
        ms = jnp.mean(x * x, axis=-1, keepdims=True)
        return x * lax.rsqrt(ms + EPS) * g_ref[...]

    xe_sc[:h] = jnp.where(pos == 0, 0.0, norm(xp_ref[...])).astype(BF16)
    xe_sc[h:h + tm] = norm(x_ref[...]).astype(BF16)
    xe_sc[h + tm:] = jnp.where(pos == tiles_per_seq - 1, 0.0, norm(xn_ref[...])).astype(BF16)
    k0 = math.sqrt(2.0 / math.pi)
    bounds = list(range(0, f, col_chunk)) + [f]
    chunks = [slice(lo, hi) for lo, hi in zip(bounds[:-1], bounds[1:])]

    def project(cols):
        a_ext = jnp.dot(xe_sc[...], wa_ref[:, cols], preferred_element_type=F32)
        lin = jnp.dot(xe_sc[h:h + tm, :], wl_ref[:, cols], preferred_element_type=F32)
        return a_ext, lin

    y = x_ref[...]
    nxt = project(chunks[0])
    for n, cols in enumerate(chunks):
        a_ext, lin = nxt
        if n + 1 < len(chunks):
            nxt = project(chunks[n + 1])
        cw = cw_ref[:, cols]
        c = (a_ext[h - 1:h - 1 + tm] * cw[0:1] + a_ext[h:h + tm] * cw[1:2] + a_ext[h + 1:h + 1 + tm] * cw[2:3]
             + cb_ref[:, cols])
        t = jnp.tanh(c * ((c * c) * (k0 * 0.044715) + k0))
        hmid = ((c + c * t) * lin).astype(BF16)
        y = y + jnp.dot(hmid, wd_sc[cols, :], preferred_element_type=F32)
    if final_norm:
        ms = jnp.mean(y * y, axis=-1, keepdims=True)
        y = y * lax.rsqrt(ms + EPS) * gf_ref[...]
    o_ref[...] = y


def conv_ffn(x2, g, w_up, conv_w, conv_b, w_down, final_g, *, s, tm, col_chunk, final_norm):
    m, dm = x2.shape
    f = w_up.shape[1] // 2
    tiles_per_seq = s // tm
    hb = tm // HALO_ROWS
    n_halo = m // HALO_ROWS
    vmem = (2 * 2 * tm * dm * 4 + 2 * dm * f * 2 + f * dm * (4 + 2) + (tm + 2 * HALO_ROWS) * dm * 2
            + 6 * tm * col_chunk * 4 + 2 * tm * dm * 4)
    return pl.pallas_call(
        functools.partial(_conv_ffn_kernel, tiles_per_seq=tiles_per_seq, col_chunk=col_chunk, final_norm=final_norm),
        grid=(m // tm,),
        in_specs=[
            pl.BlockSpec((tm, dm), lambda i: (i, 0)),
            pl.BlockSpec((HALO_ROWS, dm), lambda i: (jnp.maximum(i * hb - 1, 0), 0)),
            pl.BlockSpec((HALO_ROWS, dm), lambda i: (jnp.minimum((i + 1) * hb, n_halo - 1), 0)),
            pl.BlockSpec((1, dm), lambda i: (0, 0)),
            pl.BlockSpec((dm, f), lambda i: (0, 0), pipeline_mode=pl.Buffered(1)),
            pl.BlockSpec((dm, f), lambda i: (0, 1), pipeline_mode=pl.Buffered(1)),
            pl.BlockSpec((CONV_W, f), lambda i: (0, 0)),
            pl.BlockSpec((1, f), lambda i: (0, 0)),
            pl.BlockSpec((f, dm), lambda i: (0, 0), pipeline_mode=pl.Buffered(1)),
            pl.BlockSpec((1, dm), lambda i: (0, 0)),
        ],
        out_specs=pl.BlockSpec((tm, dm), lambda i: (i, 0)),
        out_shape=jax.ShapeDtypeStruct((m, dm), F32),
        scratch_shapes=[pltpu.VMEM((tm + 2 * HALO_ROWS, dm), BF16), pltpu.VMEM((f, dm), BF16)],
        compiler_params=_params(("arbitrary",), vmem),
        name="conv_ffn",
    )(x2, x2, x2, g.reshape(1, dm).astype(F32), w_up, w_up, conv_w.astype(F32), conv_b.reshape(1, f).astype(F32),
      w_down.astype(F32), final_g.reshape(1, dm).astype(F32))


def _arrange_w_in(w):
    widths = (512, 512, 128, 128, 512, 512, 512, 512, 16, 512, 512, 512, 4096)
    offs = [0]
    for wd in widths:
        offs.append(offs[-1] + wd)
    seg = lambda n: w[:, offs[n]:offs[n + 1]]
    a, bq, bk, bv, cq, ck, cv, co, cgate, dq, dk, dv, gates = (seg(n) for n in range(len(widths)))
    main = jnp.concatenate([a, bq, cq, ck, cv, co, dq, dk, dv, bk, bv], axis=1).astype(BF16)
    gate = jnp.pad(cgate, ((0, 0), (0, V7X_LANES - cgate.shape[1]))).astype(BF16)
    return main, gate, gates.astype(BF16)


def kernel(x, norm_mix_g, w_in, mlstm_gate_bias, qk_norm_g, mlstm_norm_g, diff_lambda, diff_norm_g, rel_bias,
           w_branch, w_out, norm_ffn_g, w_up, conv_w, conv_b, w_down, final_norm_g):
    b, s, dm = x.shape
    depth = w_in.shape[0]
    m = b * s
    d_ff = w_down.shape[1]
    L = MLSTM_CHUNK
    tl = TILES

    bd, dft_hi, dft_lo = fourier_tables(s)
    cos2, sin2 = rope_tables(s)
    bias_tiles, bias_far = diff_bias_tiles(rel_bias, tl["diff_tile"])

    x2 = x.reshape(m, dm)
    for layer in range(depth):
        w_main, w_gate, w_branch_gates = _arrange_w_in(w_in[layer])
        p2, cgate = norm_matmul(x2, norm_mix_g[layer], w_main, w_gate, tm=tl["proj_rows"], tn=P_WIDTH)
        p3 = p2.reshape(b, s, P_WIDTH)

        y_a = fourier_mix(p3, bd, dft_hi, dft_lo, tm=tl["fourier_rows"])

        q_b, k_b, v_b = gqa_prep(p2, qk_norm_g[layer], cos2, sin2, s=s, tm=tl["gqa_prep_rows"])
        kt_b = jnp.swapaxes(k_b.reshape(b, s, GQA_KV_HEADS * HEAD_DIM), 1, 2)
        y_b = gqa_attention(q_b.reshape(b, s, -1), kt_b, v_b.reshape(b, s, -1),
                            tq=tl["gqa_q_rows"], tk=tl["gqa_keys"])

        gates5 = jnp.transpose(cgate[:, :4 * MLSTM_HEADS].reshape(b, s, 4, MLSTM_HEADS), (0, 2, 3, 1))
        gates5 = gates5.reshape(b, 4, MLSTM_HEADS, s // L, L)
        y_c = mlstm_branch(p3, gates5, mlstm_gate_bias[layer], mlstm_norm_g[layer],
                           heads_per_block=tl["mlstm_heads"])

        y_d = diff_attention(p3, bias_tiles, bias_far, diff_lambda[layer], diff_norm_g[layer],
                             t=tl["diff_tile"], rows_per_item=tl["diff_q_rows"],
                             tiles_per_item=tl["diff_key_tiles"], layer_number=layer + 1)

        ys = [y.reshape(m, BRANCH_WIDTH) for y in (y_a, y_b, y_c, y_d)]
        x2 = merge_branches(ys, norm_mix_g[layer], w_branch_gates, w_branch[layer], w_out[layer], x2,
                            tm=tl["merge_rows"])

        half_lin = jnp.concatenate([jnp.ones((d_ff,), F32), jnp.full((d_ff,), 0.5, F32)])
        x2 = conv_ffn(x2, norm_ffn_g[layer], (w_up[layer] * half_lin).astype(BF16), conv_w[layer], conv_b[layer],
                      w_down[layer], final_norm_g, s=s, tm=tl["ffn_rows"], col_chunk=tl["ffn_cols"],
                      final_norm=(layer == depth - 1))
    return x2.reshape(b, s, dm)
```

```python
import functools
import math

import jax
import jax.numpy as jnp
from jax import lax
from jax.experimental import pallas as pl
from jax.experimental.pallas import tpu as pltpu

F32 = jnp.float32
BF16 = jnp.bfloat16

GRID_W = 64
HEAD_DIM = 64
BRANCH_WIDTH = 512
N_BRANCHES = 4
FOURIER_GROUP_DIM = 64
GQA_Q_HEADS = 8
GQA_KV_HEADS = 2
MLSTM_HEADS = 4
MLSTM_HEAD_DIM = 128
MLSTM_CHUNK = 128
DIFF_HEADS = 4
DIFF_QK_DIM = 64
DIFF_V_DIM = 128
REL_BUCKETS = 32
REL_MAX_DIST = 128
CONV_W = 3
ROPE_BASE = 10000.0
EPS = 1e-6
LOG2E = math.log2(math.e)

V7X_LANES = 128
V7X_VMEM_BYTES = 64 * 1024 * 1024
V7X_VMEM_CAP = V7X_VMEM_BYTES - 8 * 1024 * 1024

TILES = dict(
    proj_rows=512,
    fourier_rows=512,
    gqa_prep_rows=512,
    gqa_q_rows=512, gqa_keys=1024,
    diff_tile=512, diff_q_rows=256, diff_key_tiles=4,
    mlstm_heads=2,
    merge_rows=512,
    ffn_rows=512, ffn_cols=1024,
)
TRANSPOSE_ROWS = 512

OFF_A = 0
OFF_BQ = 512
OFF_CQ = 1024
OFF_CK = 1536
OFF_CV = 2048
OFF_CO = 2560
OFF_DQ = 3072
OFF_DK = 3584
OFF_DV = 4096
OFF_BKV = 4608
P_WIDTH = 4864


def _params(sem, vmem_bytes):
    limit = int(min(max(vmem_bytes * 3 // 2 + (4 << 20), 16 << 20), V7X_VMEM_CAP))
    return pltpu.CompilerParams(dimension_semantics=sem, vmem_limit_bytes=limit)


def _norm_mm_kernel(x_ref, g_ref, w_ref, o_ref, xn_ref):
    @pl.when(pl.program_id(1) == 0)
    def _():
        x = x_ref[...]
        ms = jnp.mean(x * x, axis=-1, keepdims=True)
        xn_ref[...] = (x * lax.rsqrt(ms + EPS) * g_ref[...]).astype(BF16)

    o_ref[...] = jnp.dot(xn_ref[...], w_ref[...], preferred_element_type=F32).astype(o_ref.dtype)


def _norm_mm_gate_kernel(x_ref, g_ref, w_ref, wg_ref, o_ref, og_ref, xn_ref):
    @pl.when(pl.program_id(1) == 0)
    def _():
        x = x_ref[...]
        ms = jnp.mean(x * x, axis=-1, keepdims=True)
        xn = (x * lax.rsqrt(ms + EPS) * g_ref[...]).astype(BF16)
        xn_ref[...] = xn
        og_ref[...] = jnp.dot(xn, wg_ref[...], preferred_element_type=F32)

    o_ref[...] = jnp.dot(xn_ref[...], w_ref[...], preferred_element_type=F32).astype(o_ref.dtype)


def norm_matmul(x, g, w, w_gate=None, *, tm, tn):
    m, k = x.shape
    n = w.shape[1]
    grid = (m // tm, n // tn)
    vmem = 2 * tm * k * 4 + tm * k * 2 + 2 * k * tn * 2 + 2 * tm * tn * 2 + 4 * tm * k
    x_spec = pl.BlockSpec((tm, k), lambda i, j: (i, 0))
    g_spec = pl.BlockSpec((1, k), lambda i, j: (0, 0))
    w_spec = pl.BlockSpec((k, tn), lambda i, j: (0, j))
    o_spec = pl.BlockSpec((tm, tn), lambda i, j: (i, j))
    scratch = [pltpu.VMEM((tm, k), BF16)]
    g2 = g.reshape(1, k).astype(F32)
    if w_gate is None:
        return pl.pallas_call(
            _norm_mm_kernel,
            grid=grid,
            in_specs=[x_spec, g_spec, w_spec],
            out_specs=o_spec,
            out_shape=jax.ShapeDtypeStruct((m, n), BF16),
            scratch_shapes=scratch,
            compiler_params=_params(("parallel", "arbitrary"), vmem),
            name="norm_matmul",
        )(x, g2, w)
    ng = w_gate.shape[1]
    return pl.pallas_call(
        _norm_mm_gate_kernel,
        grid=grid,
        in_specs=[x_spec, g_spec, w_spec, pl.BlockSpec((k, ng), lambda i, j: (0, 0))],
        out_specs=[o_spec, pl.BlockSpec((tm, ng), lambda i, j: (i, 0))],
        out_shape=[jax.ShapeDtypeStruct((m, n), BF16), jax.ShapeDtypeStruct((m, ng), F32)],
        scratch_shapes=scratch,
        compiler_params=_params(("parallel", "arbitrary"), vmem),
        name="norm_matmul_gate",
    )(x, g2, w, w_gate)


FOURIER_HALO = 16


def _fourier_kernel(a_ref, bd_ref, hi_ref, lo_ref, flip_ref, o_ref, z_ref, ext_ref, *, tm, row_chunk):
    s = a_ref.shape[1]
    w = a_ref.shape[2]
    n_lo = lo_ref.shape[1]
    i = pl.program_id(1)

    @pl.when(i == 0)
    def _():
        for r in range(0, s, row_chunk):
            a = a_ref[0, r:r + row_chunk, :]
            zc = jnp.dot(a, bd_ref[...], preferred_element_type=F32)
            z_ref[r:r + row_chunk, :] = zc[:, :w].astype(BF16)
            z_ref[s + r:s + r + row_chunk, :] = zc[:, w:].astype(BF16)

    cl, sl = lo_ref[0], lo_ref[1]
    for r in range(tm // n_lo + 1):
        rows = n_lo if r < tm // n_lo else ext_ref.shape[0] - tm
        k1 = i * (tm // n_lo) + r
        ch = hi_ref[0, pl.ds(k1, 1), :]
        nsh = hi_ref[1, pl.ds(k1, 1), :]
        ext_ref[r * n_lo:r * n_lo + rows, :s] = (cl[:rows] * ch + sl[:rows] * nsh).astype(BF16)
        ext_ref[r * n_lo:r * n_lo + rows, s:] = (cl[:rows] * nsh - sl[:rows] * ch).astype(BF16)
    p = jnp.dot(ext_ref[:, :s], z_ref[:s], preferred_element_type=F32)
    q = jnp.dot(ext_ref[:, s:], z_ref[s:], preferred_element_type=F32)
    lo = pl.multiple_of(i * tm, tm)
    o_ref[0, pl.ds(lo, tm), :] = (p[:tm] + q[:tm]).astype(o_ref.dtype)
    mirrored = (p[1:tm + 1] - q[1:tm + 1]).astype(BF16)
    hi = pl.multiple_of(s - (i + 1) * tm, tm)
    o_ref[0, pl.ds(hi, tm), :] = jnp.dot(flip_ref[...], mirrored, preferred_element_type=F32).astype(o_ref.dtype)


def fourier_tables(s):
    cg = FOURIER_GROUP_DIM
    jj = jnp.arange(cg, dtype=jnp.int32)
    ang_c = (2.0 * math.pi / cg) * ((jj[:, None] * jj[None, :]) % cg).astype(F32)
    eye_g = jnp.eye(BRANCH_WIDTH // cg, dtype=F32)
    bd_c = jnp.kron(eye_g, jnp.cos(ang_c)) * cg ** -0.5
    bd_s = jnp.kron(eye_g, jnp.sin(ang_c)) * cg ** -0.5
    bd = jnp.concatenate([bd_c, bd_s], axis=1).astype(BF16)
    n_lo = s // cg
    nn = jnp.arange(s, dtype=jnp.int32)
    k1 = jnp.arange(cg // 2 + 2, dtype=jnp.int32)
    ang_hi = (2.0 * math.pi / cg) * ((k1[:, None] * nn[None, :]) % cg).astype(F32)
    ll = jnp.arange(n_lo, dtype=jnp.int32)
    ang_lo = (2.0 * math.pi / s) * ((ll[:, None] * nn[None, :]) % s).astype(F32)
    scale = s ** -0.5
    hi = jnp.stack([jnp.cos(ang_hi), -jnp.sin(ang_hi)]) * scale
    lo = jnp.stack([jnp.cos(ang_lo), jnp.sin(ang_lo)])
    return bd, hi, lo


def fourier_mix(p3, bd, hi, lo, *, tm):
    b, s, _ = p3.shape
    w = BRANCH_WIDTH
    n_lo = lo.shape[1]
    h = FOURIER_HALO
    assert (s // 2) % tm == 0 and tm % n_lo == 0 and h <= n_lo
    r = jnp.arange(tm, dtype=jnp.int32)
    flip = (r[:, None] + r[None, :] == tm - 1).astype(BF16)
    vmem = (s * w * 2 + 2 * w * w * 2 + 2 * (hi.size + lo.size) * 4 + (tm + h) * 2 * s * 2 + 2 * s * w * 2
            + 2 * s * w * 2 + 8 * (tm + h) * w * 4 + 8 * n_lo * s * 4)
    return pl.pallas_call(
        functools.partial(_fourier_kernel, tm=tm, row_chunk=min(s, tm)),
        grid=(b, s // 2 // tm),
        in_specs=[
            pl.BlockSpec((1, s, w), lambda bi, i: (bi, 0, OFF_A // w), pipeline_mode=pl.Buffered(1)),
            pl.BlockSpec((w, 2 * w), lambda bi, i: (0, 0), pipeline_mode=pl.Buffered(1)),
            pl.BlockSpec(hi.shape, lambda bi, i: (0, 0, 0)),
            pl.BlockSpec(lo.shape, lambda bi, i: (0, 0, 0)),
            pl.BlockSpec((tm, tm), lambda bi, i: (0, 0)),
        ],
        out_specs=pl.BlockSpec((1, s, w), lambda bi, i: (bi, 0, 0)),
        out_shape=jax.ShapeDtypeStruct((b, s, w), BF16),
        scratch_shapes=[pltpu.VMEM((2 * s, w), BF16), pltpu.VMEM((tm + h, 2 * s), BF16)],
        compiler_params=_params(("parallel", "arbitrary"), vmem),
        name="fourier",
    )(p3, bd, hi, lo, flip)


def rope_tables(s):
    rows = s // GRID_W
    row_id = jnp.repeat(jnp.arange(rows, dtype=F32), GRID_W)
    col_id = jnp.tile(jnp.arange(GRID_W, dtype=F32), rows)
    n_pairs = HEAD_DIM // 4
    inv_freq = ROPE_BASE ** (-jnp.arange(n_pairs, dtype=F32) / n_pairs)
    ang = jnp.concatenate([row_id[:, None] * inv_freq, col_id[:, None] * inv_freq], axis=-1)
    cos, sin = jnp.cos(ang), jnp.sin(ang)
    return jnp.concatenate([cos, cos] * 2, axis=-1), jnp.concatenate([-sin, sin] * 2, axis=-1)


def _norm_rope(x, g, seg, cos2, sin2):
    half = HEAD_DIM // 2
    x2 = x * x
    hi = x2.astype(BF16)
    lo = (x2 - hi.astype(F32)).astype(BF16)
    ms = jnp.dot(hi, seg, preferred_element_type=F32) + jnp.dot(lo, seg, preferred_element_type=F32)
    y = x * lax.rsqrt(ms + EPS) * g
    lane = lax.broadcasted_iota(jnp.int32, (x.shape[0], V7X_LANES), 1)
    first_half = (lane % HEAD_DIM) < half
    outs = []
    for cb in range(x.shape[1] // V7X_LANES):
        yb = y[:, cb * V7X_LANES:(cb + 1) * V7X_LANES]
        rot = jnp.where(first_half, pltpu.roll(yb, V7X_LANES - half, axis=1), pltpu.roll(yb, half, axis=1))
        outs.append(yb * cos2 + rot * sin2)
    return outs[0] if len(outs) == 1 else jnp.concatenate(outs, axis=-1)


def _gqa_prep_kernel(q_ref, kv_ref, gq_ref, gk_ref, seg_ref, cos_ref, sin_ref, qo_ref, ko_ref, vo_ref):
    d = HEAD_DIM
    nk = GQA_KV_HEADS * d
    cos2, sin2 = cos_ref[...], sin_ref[...]
    q = q_ref[...].astype(F32)
    kv = kv_ref[...].astype(F32)
    qo_ref[...] = _norm_rope(q, gq_ref[...], seg_ref[...], cos2, sin2).astype(BF16)
    ko_ref[...] = _norm_rope(kv[:, :nk], gk_ref[...], seg_ref[:nk, :nk], cos2, sin2).astype(BF16)
    v = kv[:, nk:]
    lane = lax.broadcasted_iota(jnp.int32, v.shape, 1)
    ones_col = jnp.where(lane == d, 1.0, 0.0)
    vo_ref[:, :nk] = jnp.where(lane < d, v, ones_col).astype(BF16)
    vo_ref[:, nk:] = jnp.where(lane < d, pltpu.roll(v, d, axis=1), ones_col).astype(BF16)


def gqa_prep(p2, qk_g, cos2, sin2, *, s, tm):
    m = p2.shape[0]
    nq = GQA_Q_HEADS * HEAD_DIM
    nkv = GQA_KV_HEADS * HEAD_DIM
    assert nkv == V7X_LANES
    tiles_per_seq = s // tm
    q_scale = HEAD_DIM ** -0.5 * LOG2E
    gq = jnp.tile(qk_g[0].astype(F32) * q_scale, GQA_Q_HEADS).reshape(1, nq)
    gk = jnp.tile(qk_g[1].astype(F32), GQA_KV_HEADS).reshape(1, nkv)
    seg = jnp.kron(jnp.eye(GQA_Q_HEADS, dtype=F32), jnp.full((HEAD_DIM, HEAD_DIM), 1.0 / HEAD_DIM, F32)).astype(BF16)
    return pl.pallas_call(
        _gqa_prep_kernel,
        grid=(m // tm,),
        in_specs=[
            pl.BlockSpec((tm, nq), lambda i: (i, OFF_BQ // nq)),
            pl.BlockSpec((tm, 2 * nkv), lambda i: (i, OFF_BKV // (2 * nkv))),
            pl.BlockSpec((1, nq), lambda i: (0, 0)),
            pl.BlockSpec((1, nkv), lambda i: (0, 0)),
            pl.BlockSpec((nq, nq), lambda i: (0, 0)),
            pl.BlockSpec((tm, V7X_LANES), lambda i: (i % tiles_per_seq, 0)),
            pl.BlockSpec((tm, V7X_LANES), lambda i: (i % tiles_per_seq, 0)),
        ],
        out_specs=[
            pl.BlockSpec((tm, nq), lambda i: (i, 0)),
            pl.BlockSpec((tm, nkv), lambda i: (i, 0)),
            pl.BlockSpec((tm, 2 * nkv), lambda i: (i, 0)),
        ],
        out_shape=[
            jax.ShapeDtypeStruct((m, nq), BF16),
            jax.ShapeDtypeStruct((m, nkv), BF16),
            jax.ShapeDtypeStruct((m, 2 * nkv), BF16),
        ],
        compiler_params=_params(("parallel",), 16 * tm * nq * 4),
        name="gqa_prep",
    )(p2, p2, gq, gk, seg, cos2, sin2)


def _gqa_attn_kernel(q_ref, kt_ref, v_ref, o_ref, q_sc, m_ref, acc_ref, *, tk):
    d = HEAD_DIM
    tq = q_ref.shape[1]
    grp = q_ref.shape[2] // d
    s = kt_ref.shape[2]
    for g in range(grp):
        q_sc[g * tq:(g + 1) * tq, :] = q_ref[0, :, g * d:(g + 1) * d]

    def qk(item):
        c, g = item
        return jnp.dot(q_sc[g * tq:(g + 1) * tq, :], kt_ref[0, :, c * tk:(c + 1) * tk],
                       preferred_element_type=F32)

    items = [(c, g) for c in range(s // tk) for g in range(grp)]
    sc_next = qk(items[0])
    for n, (c, g) in enumerate(items):
        rows = slice(g * tq, (g + 1) * tq)
        sc = sc_next
        if n + 1 < len(items):
            sc_next = qk(items[n + 1])
        v = v_ref[0, c * tk:(c + 1) * tk, :]
        row_max = jnp.max(sc, axis=-1, keepdims=True)
        if c == 0:
            m_new = jnp.broadcast_to(row_max, (tq, V7X_LANES))
        else:
            m_prev = m_ref[rows, :]
            m_new = jnp.maximum(m_prev, row_max)
        p = jnp.exp2((sc - pltpu.repeat(m_new, tk // V7X_LANES, axis=1)).astype(BF16))
        pv = jnp.dot(p, v, preferred_element_type=F32)
        if c == 0:
            acc_ref[rows, :] = pv
        else:
            acc_ref[rows, :] = acc_ref[rows, :] * jnp.exp2(m_prev - m_new) + pv
        m_ref[rows, :] = m_new
    for g in range(grp):
        acc = acc_ref[g * tq:(g + 1) * tq, :]
        o_ref[0, :, g * d:(g + 1) * d] = (acc[:, :d] / acc[:, d:d + 1]).astype(o_ref.dtype)


def gqa_attention(q3, kt3, v3, *, tq, tk):
    b, s, nq = q3.shape
    d = HEAD_DIM
    grp = GQA_Q_HEADS // GQA_KV_HEADS
    mrows = grp * tq
    vmem = (2 * (tq * grp * d * 2 + d * s * 2 + s * 128 * 2 + tq * grp * d * 2) + 3 * mrows * 128 * 4
            + 12 * tq * tk * 4)
    return pl.pallas_call(
        functools.partial(_gqa_attn_kernel, tk=tk),
        grid=(b, GQA_KV_HEADS, s // tq),
        in_specs=[
            pl.BlockSpec((1, tq, grp * d), lambda bi, kv, i: (bi, i, kv)),
            pl.BlockSpec((1, d, s), lambda bi, kv, i: (bi, kv, 0)),
            pl.BlockSpec((1, s, 2 * d), lambda bi, kv, i: (bi, 0, kv)),
        ],
        out_specs=pl.BlockSpec((1, tq, grp * d), lambda bi, kv, i: (bi, i, kv)),
        out_shape=jax.ShapeDtypeStruct((b, s, nq), BF16),
        scratch_shapes=[pltpu.VMEM((mrows, d), BF16), pltpu.VMEM((mrows, V7X_LANES), F32),
                        pltpu.VMEM((mrows, 2 * d), F32)],
        compiler_params=_params(("parallel", "parallel", "parallel"), vmem),
        name="gqa_attn",
    )(q3, kt3, v3)


def _mlstm_step(chains, ms, q_ref, kt_sc, v_ref, r_sc, cm_sc, b_sc, st_sc, h_sc):
    L = MLSTM_CHUNK
    dh = MLSTM_HEAD_DIM
    assert L == dh
    scale = dh ** -0.5
    row_i = lax.broadcasted_iota(jnp.int32, (L, L), 0)
    col_i = lax.broadcasted_iota(jnp.int32, (L, L), 1)

    def col(x_row):
        return jnp.transpose(jnp.broadcast_to(x_row, (L, L)))

    pre = []
    for (hh, direction, c), m in zip(chains, ms):
        off = pl.multiple_of(c * L, L)
        lanes = slice(hh * dh, (hh + 1) * dh)
        q = q_ref[0, pl.ds(off, L), lanes]
        kt = kt_sc[hh, :, pl.ds(off, L)]
        v = v_ref[0, pl.ds(off, L), lanes]
        r_row = r_sc[hh, direction, pl.ds(c, 1), :]
        cm_row = cm_sc[hh, direction, pl.ds(c, 1), :]
        b_row = b_sc[hh, direction, pl.ds(c, 1), :]
        rmax = jnp.max(r_row, axis=-1, keepdims=True)
        btot = b_row[:, L - 1:L] if direction == 0 else b_row[:, 0:1]
        cmat = jnp.maximum(m, col(cm_row))
        mask = (row_i >= col_i) if direction == 0 else (row_i <= col_i)
        c_last = jnp.maximum(m, rmax)
        w_state = jnp.exp(r_row - c_last) * scale
        pre.append(dict(
            off=off, lanes=lanes, q=q, kt=kt,
            v_aug=jnp.concatenate([v, jnp.ones((L, dh), BF16)], axis=-1),
            w_intra=jnp.where(mask, jnp.exp(r_row - cmat), 0.0) * scale,
            w_inter=jnp.exp(m - cmat),
            den_floor=jnp.exp(-(col(b_row) + cmat)),
            decay=jnp.exp(m - c_last),
            kt_w=(kt.astype(F32) * w_state).astype(BF16),
            m_new=btot + c_last,
        ))
    s_raw = [jnp.dot(p["q"], p["kt"], preferred_element_type=F32) for p in pre]
    states = [st_sc[hh, direction] for hh, direction, _ in chains]
    inter = [jnp.dot(p["q"], st.astype(BF16), preferred_element_type=F32) for p, st in zip(pre, states)]
    upd = [jnp.dot(p["kt_w"], p["v_aug"], preferred_element_type=F32) for p in pre]
    for n, ((hh, direction, _), p) in enumerate(zip(chains, pre)):
        st_sc[hh, direction] = p["decay"] * states[n] + upd[n]
        intra = jnp.dot((s_raw[n] * p["w_intra"]).astype(BF16), p["v_aug"], preferred_element_type=F32)
        h_aug = jnp.concatenate([p["w_inter"], p["w_inter"]], axis=-1) * inter[n] + intra
        h_sc[hh, direction, pl.ds(p["off"], L), :] = h_aug[:, :dh] / jnp.maximum(jnp.abs(h_aug[:, dh:]), p["den_floor"])
    return [p["m_new"] for p in pre]


def _mlstm_kernel(bias_ref, q_ref, k_ref, v_ref, o_ref, gate_ref, g_ref, y_ref,
                  r_sc, cm_sc, b_sc, h_sc, st_sc, kt_sc):
    L = MLSTM_CHUNK
    dh = MLSTM_HEAD_DIM
    hpb = q_ref.shape[2] // dh
    head0 = pl.program_id(1) * hpb
    nc = q_ref.shape[1] // L
    lane = lax.broadcasted_iota(jnp.int32, (nc, L), 1)
    shifts = [1 << t for t in range(int(math.log2(L)))]
    for hh in range(hpb):
        for d in range(2):
            i_pre = gate_ref[0, 2 * d, hh] + bias_ref[2 * d, head0 + hh]
            f_pre = gate_ref[0, 2 * d + 1, hh] + bias_ref[2 * d + 1, head0 + hh]
            logf = jnp.minimum(f_pre, 0.0) - jnp.log1p(jnp.exp(-jnp.abs(f_pre)))
            bc = logf
            for sh in shifts:
                if d == 0:
                    bc = bc + jnp.where(lane >= sh, pltpu.roll(bc, sh, axis=1), 0.0)
                else:
                    bc = bc + jnp.where(lane < L - sh, pltpu.roll(bc, L - sh, axis=1), 0.0)
            r = i_pre - bc
            cm = r
            for sh in shifts:
                if d == 0:
                    cm = jnp.maximum(cm, jnp.where(lane >= sh, pltpu.roll(cm, sh, axis=1), -jnp.inf))
                else:
                    cm = jnp.maximum(cm, jnp.where(lane < L - sh, pltpu.roll(cm, L - sh, axis=1), -jnp.inf))
            r_sc[hh, d] = r
            cm_sc[hh, d] = cm
            b_sc[hh, d] = bc
    st_sc[...] = jnp.zeros(st_sc.shape, F32)
    tr = TRANSPOSE_ROWS
    for hh in range(hpb):
        for r in range(0, q_ref.shape[1], tr):
            kt_sc[hh, :, r:r + tr] = jnp.transpose(k_ref[0, r:r + tr, hh * dh:(hh + 1) * dh].astype(F32)).astype(BF16)

    def body(c, ms):
        chains = [(hh, d, c if d == 0 else nc - 1 - c) for hh in range(hpb) for d in range(2)]
        return tuple(_mlstm_step(chains, ms, q_ref, kt_sc, v_ref, r_sc, cm_sc, b_sc, st_sc, h_sc))

    lax.fori_loop(0, nc, body, tuple(jnp.zeros((1, 1), F32) for _ in range(2 * hpb)), unroll=2)
    for hh in range(hpb):
        lanes = slice(hh * dh, (hh + 1) * dh)
        hsum = h_sc[hh, 0] + h_sc[hh, 1]
        ms = jnp.mean(hsum * hsum, axis=-1, keepdims=True)
        y = hsum * lax.rsqrt(ms + EPS) * g_ref[:, lanes]
        y_ref[0, :, lanes] = (jax.nn.sigmoid(o_ref[0, :, lanes].astype(F32)) * y).astype(y_ref.dtype)


def mlstm_branch(p3, gates5, gate_bias, norm_g, *, heads_per_block):
    b, s, _ = p3.shape
    L = MLSTM_CHUNK
    hpb = heads_per_block
    wb = hpb * MLSTM_HEAD_DIM
    nc = s // L
    blk = lambda off: pl.BlockSpec((1, s, wb), lambda bi, h, off=off: (bi, 0, off // wb + h))
    vmem = 2 * 5 * s * wb * 2 + 2 * s * wb * 4 + 6 * hpb * nc * L * 4 + 4 * hpb * wb * wb * 4 + 3 * s * wb * 4
    return pl.pallas_call(
        _mlstm_kernel,
        grid=(b, MLSTM_HEADS // hpb),
        in_specs=[
            pl.BlockSpec(memory_space=pltpu.SMEM),
            blk(OFF_CQ), blk(OFF_CK), blk(OFF_CV), blk(OFF_CO),
            pl.BlockSpec((1, 4, hpb, nc, L), lambda bi, h: (bi, 0, h, 0, 0)),
            pl.BlockSpec((1, wb), lambda bi, h: (0, h)),
        ],
        out_specs=pl.BlockSpec((1, s, wb), lambda bi, h: (bi, 0, h)),
        out_shape=jax.ShapeDtypeStruct((b, s, BRANCH_WIDTH), BF16),
        scratch_shapes=[
            pltpu.VMEM((hpb, 2, nc, L), F32), pltpu.VMEM((hpb, 2, nc, L), F32), pltpu.VMEM((hpb, 2, nc, L), F32),
            pltpu.VMEM((hpb, 2, s, MLSTM_HEAD_DIM), F32),
            pltpu.VMEM((hpb, 2, MLSTM_HEAD_DIM, 2 * MLSTM_HEAD_DIM), F32),
            pltpu.VMEM((hpb, MLSTM_HEAD_DIM, s), BF16),
        ],
        compiler_params=_params(("parallel", "parallel"), vmem),
        name="mlstm",
    )(gate_bias.astype(F32), p3, p3, p3, p3, gates5, norm_g.reshape(1, BRANCH_WIDTH).astype(F32))


def _rel_bucket(rel):
    half = REL_BUCKETS // 2
    max_exact = half // 2
    ret = jnp.where(rel > 0, half, 0)
    n = jnp.abs(rel)
    nf = jnp.maximum(n, 1).astype(F32)
    large = max_exact + (jnp.log(nf / max_exact) / math.log(REL_MAX_DIST / max_exact) * (half - max_exact)).astype(jnp.int32)
    large = jnp.minimum(large, half - 1)
    return ret + jnp.where(n < max_exact, n, large)


def diff_bias_tiles(rel_bias, t):
    assert t >= REL_MAX_DIST
    k = jnp.arange(2 * t, dtype=jnp.int32)
    rel = jnp.arange(-2, 3, dtype=jnp.int32)[:, None] * t + jnp.where(k < t, k, k - 2 * t)[None, :]
    onehot = (_rel_bucket(rel)[:, :, None] == jnp.arange(REL_BUCKETS, dtype=jnp.int32)).astype(F32)
    period = jnp.einsum('dkb,bh->hdk', onehot, rel_bias.astype(F32) * LOG2E, precision=lax.Precision.HIGHEST)
    far = period[:, 0::4, 0]
    return period, far


def _diff_attn_kernel(far_ref, q_ref, k_ref, v_ref, period_ref, lam_ref, g_ref, o_ref,
                      q_sc, kt_sc, vaug_sc, bias_sc, sc_buf, m_ref, acc_ref, *, t, lam_init):
    dq = DIFF_QK_DIM
    dv = DIFF_V_DIM
    s = k_ref.shape[1]
    nt = s // t
    head = pl.program_id(1)
    ahead = sc_buf.shape[0]
    rb = sc_buf.shape[1]
    grp = sc_buf.shape[2] // t
    wrap = (grp - 1) * t
    assert nt % grp == 0 and t % rb == 0 and nt >= 4

    tr = max(t, TRANSPOSE_ROWS)
    for r in range(0, s, tr):
        kt_sc[:, r:r + tr] = jnp.transpose(k_ref[0, r:r + tr, :].astype(F32)).astype(BF16)
    kt_sc[:, s:] = kt_sc[:, :wrap]
    vaug_sc[:s, :dv] = v_ref[0]
    vaug_sc[s:, :dv] = v_ref[0, :wrap, :]
    lane = lax.broadcasted_iota(jnp.int32, (s + wrap, dv), 1)
    vaug_sc[:, dv:] = jnp.where(lane == 0, 1.0, 0.0).astype(BF16)
    for dl in range(period_ref.shape[1]):
        full = jnp.broadcast_to(period_ref[0, dl:dl + 1, :], (t, 2 * t))
        bias_sc[dl] = pltpu.roll(full, 0, axis=1, stride=1, stride_axis=0)[:, :t]
    q_scale = dq ** -0.5 * LOG2E
    q_all = (q_ref[0].astype(F32) * q_scale).astype(BF16)
    q_sc[0] = q_all[:, :dq]
    q_sc[1] = q_all[:, dq:]
    lp = lam_ref[...]
    lam = (jnp.exp(jnp.sum(lp[0:1] * lp[1:2], axis=-1, keepdims=True))
           - jnp.exp(jnp.sum(lp[2:3] * lp[3:4], axis=-1, keepdims=True)) + lam_init)

    def key_tile(i, delta):
        j = lax.rem(i + delta, nt)
        return j, pl.multiple_of(j * t, t)

    def qk(i, item):
        pair, mp, hf = item
        row0 = pl.multiple_of(i * t + hf * rb, rb)
        _, off = key_tile(i, grp * pair)
        return jnp.dot(q_sc[mp, pl.ds(row0, rb), :], kt_sc[mp * dq:(mp + 1) * dq, pl.ds(off, grp * t)],
                       preferred_element_type=F32)

    items = [(pair, mp, hf) for pair in range(nt // grp) for mp in range(2) for hf in range(t // rb)]
    for a in range(ahead):
        sc_buf[a] = qk(0, items[a])

    def tile_body(i, carry):
        pending = [sc_buf[a] for a in range(ahead)]
        for n, (pair, mp, hf) in enumerate(items):
            sc = pending.pop(0)
            if n + ahead < len(items):
                pending.append(qk(i, items[n + ahead]))
            else:
                pending.append(qk(jnp.minimum(i + 1, nt - 1), items[n + ahead - len(items)]))
            rows = slice(hf * rb, (hf + 1) * rb)
            scs = [sc[:, p * t:(p + 1) * t] for p in range(grp)]
            _, off = key_tile(i, grp * pair)
            shifts, row_max = [], None
            for piece, delta in enumerate(range(grp * pair, grp * pair + grp)):
                j, _ = key_tile(i, delta)
                if delta in (0, 1, nt - 1):
                    tile = 2 if delta == 0 else jnp.clip(j - i, -2, 2) + 2
                    scs[piece] = scs[piece] + bias_sc[tile, rows, :]
                    shifts.append(None)
                    rm = jnp.max(scs[piece], axis=-1, keepdims=True)
                else:
                    shifts.append(jnp.where(j > i, far_ref[head, 1], far_ref[head, 0]))
                    rm = jnp.max(scs[piece], axis=-1, keepdims=True) + shifts[-1]
                row_max = rm if row_max is None else jnp.maximum(row_max, rm)
            if pair == 0:
                m_new = jnp.broadcast_to(row_max, (rb, V7X_LANES))
            else:
                m_prev = m_ref[mp, rows, :]
                m_new = jnp.maximum(m_prev, row_max)
            ps = []
            for piece in range(grp):
                m_sub = m_new if shifts[piece] is None else m_new - shifts[piece]
                ps.append(jnp.exp2((scs[piece] - pltpu.repeat(m_sub, t // V7X_LANES, axis=1)).astype(BF16)))
            pv = jnp.dot(jnp.concatenate(ps, axis=1), vaug_sc[pl.ds(off, grp * t), :], preferred_element_type=F32)
            if pair == 0:
                acc_ref[mp, rows, :] = pv
            else:
                alpha = jnp.exp2(m_prev - m_new)
                acc_ref[mp, rows, :] = acc_ref[mp, rows, :] * pltpu.repeat(alpha, 2 * dv // V7X_LANES, axis=1) + pv
            m_ref[mp, rows, :] = m_new
        for a in range(ahead):
            sc_buf[a] = pending[a]
        o0 = acc_ref[0, :, :dv] / acc_ref[0, :, dv:dv + 1]
        o1 = acc_ref[1, :, :dv] / acc_ref[1, :, dv:dv + 1]
        o = o0 - lam * o1
        ms = jnp.mean(o * o, axis=-1, keepdims=True)
        row0 = pl.multiple_of(i * t, t)
        o_ref[0, pl.ds(row0, t), :] = (o * lax.rsqrt(ms + EPS) * g_ref[...] * (1.0 - lam_init)).astype(o_ref.dtype)
        return carry

    lax.fori_loop(0, nt, tile_body, 0, unroll=4)


def diff_attention(p3, bias_period, bias_far, lam_params, sub_g, *, t, rows_per_item, tiles_per_item, layer_number):
    b, s, _ = p3.shape
    dv = DIFF_V_DIM
    n_off = bias_period.shape[1]
    assert s // t >= 4, "tiles 2 .. s/t-2 steps away from the query tile must all be beyond REL_MAX_DIST"
    lam_init = 0.8 - 0.6 * math.exp(-0.3 * (layer_number - 1))
    sw = s + (tiles_per_item - 1) * t
    item = rows_per_item * tiles_per_item * t
    vmem = (2 * (s * 128 * 2 + 128 * s * 2 + s * dv * 2 + s * dv * 2) + n_off * t * t * 4 + sw * (2 * dv + 128) * 2
            + 2 * s * 128 * 2 + item * 4 + 2 * t * 128 * 4 * 3 + 8 * item * 4)
    return pl.pallas_call(
        functools.partial(_diff_attn_kernel, t=t, lam_init=lam_init),
        grid=(b, DIFF_HEADS),
        in_specs=[
            pl.BlockSpec(memory_space=pltpu.SMEM),
            pl.BlockSpec((1, s, 128), lambda bi, h: (bi, 0, OFF_DQ // 128 + h)),
            pl.BlockSpec((1, s, 128), lambda bi, h: (bi, 0, OFF_DK // 128 + h)),
            pl.BlockSpec((1, s, dv), lambda bi, h: (bi, 0, OFF_DV // dv + h)),
            pl.BlockSpec((1, n_off, 2 * t), lambda bi, h: (h, 0, 0)),
            pl.BlockSpec((4, DIFF_QK_DIM), lambda bi, h: (0, 0)),
            pl.BlockSpec((1, dv), lambda bi, h: (0, 0)),
        ],
        out_specs=pl.BlockSpec((1, s, dv), lambda bi, h: (bi, 0, h)),
        out_shape=jax.ShapeDtypeStruct((b, s, BRANCH_WIDTH), BF16),
        scratch_shapes=[pltpu.VMEM((2, s, DIFF_QK_DIM), BF16), pltpu.VMEM((2 * DIFF_QK_DIM, sw), BF16),
                        pltpu.VMEM((sw, 2 * dv), BF16),
                        pltpu.VMEM((n_off, t, t), F32), pltpu.VMEM((1, rows_per_item, tiles_per_item * t), F32),
                        pltpu.VMEM((2, t, V7X_LANES), F32), pltpu.VMEM((2, t, 2 * dv), F32)],
        compiler_params=_params(("parallel", "parallel"), vmem),
        name="diff_attn",
    )(bias_far, p3, p3, p3, bias_period, lam_params.astype(F32), sub_g.reshape(1, dv).astype(F32))


def _merge_kernel(ya_ref, yb_ref, yc_ref, yd_ref, g_ref, wg_ref, wb_ref, wo_ref, x_ref, o_ref, wb_sc, wo_sc):
    dm = x_ref.shape[1]

    @pl.when(pl.program_id(0) == 0)
    def _():
        for n in range(wb_ref.shape[0]):
            wb_sc[n] = wb_ref[n].astype(BF16)
        wo_sc[...] = wo_ref[...].astype(BF16)

    tm = x_ref.shape[0]
    halves = [slice(0, tm // 2), slice(tm // 2, tm)]
    y_refs = (ya_ref, yb_ref, yc_ref, yd_ref)
    gates, branch = [], []
    for rows in halves:
        x = x_ref[rows, :]
        ms = jnp.mean(x * x, axis=-1, keepdims=True)
        u = (x * lax.rsqrt(ms + EPS) * g_ref[...]).astype(BF16)
        gates.append([jnp.dot(u, wg_ref[:, n * dm:(n + 1) * dm], preferred_element_type=F32)
                      for n in range(len(y_refs))])
        branch.append([jnp.dot(y_ref[rows, :], wb_sc[n], preferred_element_type=F32)
                       for n, y_ref in enumerate(y_refs)])
    for rows, gts, brs in zip(halves, gates, branch):
        merged = None
        for gt, br in zip(gts, brs):
            term = jax.nn.sigmoid(gt) * br
            merged = term if merged is None else merged + term
        o_ref[rows, :] = x_ref[rows, :] + jnp.dot(merged.astype(BF16), wo_sc[...], preferred_element_type=F32)


def merge_branches(ys, norm_g, w_gates, w_branch, w_out, x2, *, tm):
    m, dm = x2.shape
    w = BRANCH_WIDTH
    y_spec = pl.BlockSpec((tm, w), lambda i: (i, 0))
    vmem = (2 * (4 * tm * w * 2 + 2 * tm * dm * 4) + N_BRANCHES * dm * dm * 2 + (4 * w * dm + dm * dm) * (4 + 2)
            + 12 * tm * dm * 4)
    return pl.pallas_call(
        _merge_kernel,
        grid=(m // tm,),
        in_specs=[
            y_spec, y_spec, y_spec, y_spec,
            pl.BlockSpec((1, dm), lambda i: (0, 0)),
            pl.BlockSpec((dm, N_BRANCHES * dm), lambda i: (0, 0), pipeline_mode=pl.Buffered(1)),
            pl.BlockSpec((N_BRANCHES, w, dm), lambda i: (0, 0, 0), pipeline_mode=pl.Buffered(1)),
            pl.BlockSpec((dm, dm), lambda i: (0, 0), pipeline_mode=pl.Buffered(1)),
            pl.BlockSpec((tm, dm), lambda i: (i, 0)),
        ],
        out_specs=pl.BlockSpec((tm, dm), lambda i: (i, 0)),
        out_shape=jax.ShapeDtypeStruct((m, dm), F32),
        scratch_shapes=[pltpu.VMEM((N_BRANCHES, w, dm), BF16), pltpu.VMEM((dm, dm), BF16)],
        compiler_params=_params(("arbitrary",), vmem),
        name="merge",
    )(*ys, norm_g.reshape(1, dm).astype(F32), w_gates, w_branch.astype(F32), w_out.astype(F32), x2)


HALO_ROWS = 16


def _conv_ffn_kernel(x_ref, xp_ref, xn_ref, g_ref, wa_ref, wl_ref, cw_ref, cb_ref, wd_ref, gf_ref, o_ref,
                     xe_sc, wd_sc, *, tiles_per_seq, col_chunk, final_norm):
    tm = o_ref.shape[0]
    f = wa_ref.shape[1]
    h = HALO_ROWS
    pos = pl.program_id(0) % tiles_per_seq

    @pl.when(pl.program_id(0) == 0)
    def _():
        wd_sc[...] = wd_ref[...].astype(BF16)

    def norm(x):
        ms = jnp.mean(x * x, axis=-1, keepdims=True)
        return x * lax.rsqrt(ms + EPS) * g_ref[...]

    xe_sc[:h] = jnp.where(pos == 0, 0.0, norm(xp_ref[...])).astype(BF16)
    xe_sc[h:h + tm] = norm(x_ref[...]).astype(BF16)
    xe_sc[h + tm:] = jnp.where(pos == tiles_per_seq - 1, 0.0, norm(xn_ref[...])).astype(BF16)
    k0 = math.sqrt(2.0 / math.pi)
    bounds = list(range(0, f, col_chunk)) + [f]
    chunks = [slice(lo, hi) for lo, hi in zip(bounds[:-1], bounds[1:])]

    def project(cols):
        a_ext = jnp.dot(xe_sc[...], wa_ref[:, cols], preferred_element_type=F32)
        lin = jnp.dot(xe_sc[h:h + tm, :], wl_ref[:, cols], preferred_element_type=F32)
        return a_ext, lin

    y = x_ref[...]
    nxt = project(chunks[0])
    for n, cols in enumerate(chunks):
        a_ext, lin = nxt
        if n + 1 < len(chunks):
            nxt = project(chunks[n + 1])
        cw = cw_ref[:, cols]
        c = (a_ext[h - 1:h - 1 + tm] * cw[0:1] + a_ext[h:h + tm] * cw[1:2] + a_ext[h + 1:h + 1 + tm] * cw[2:3]
             + cb_ref[:, cols])
        t = jnp.tanh(c * ((c * c) * (k0 * 0.044715) + k0))
        hmid = ((c + c * t) * lin).astype(BF16)
        y = y + jnp.dot(hmid, wd_sc[cols, :], preferred_element_type=F32)
    if final_norm:
        ms = jnp.mean(y * y, axis=-1, keepdims=True)
        y = y * lax.rsqrt(ms + EPS) * gf_ref[...]
    o_ref[...] = y


def conv_ffn(x2, g, w_up, conv_w, conv_b, w_down, final_g, *, s, tm, col_chunk, final_norm):
    m, dm = x2.shape
    f = w_up.shape[1] // 2
    tiles_per_seq = s // tm
    hb = tm // HALO_ROWS
    n_halo = m // HALO_ROWS
    vmem = (2 * 2 * tm * dm * 4 + 2 * dm * f * 2 + f * dm * (4 + 2) + (tm + 2 * HALO_ROWS) * dm * 2
            + 6 * tm * col_chunk * 4 + 2 * tm * dm * 4)
    return pl.pallas_call(
        functools.partial(_conv_ffn_kernel, tiles_per_seq=tiles_per_seq, col_chunk=col_chunk, final_norm=final_norm),
        grid=(m // tm,),
        in_specs=[
            pl.BlockSpec((tm, dm), lambda i: (i, 0)),
            pl.BlockSpec((HALO_ROWS, dm), lambda i: (jnp.maximum(i * hb - 1, 0), 0)),
            pl.BlockSpec((HALO_ROWS, dm), lambda i: (jnp.minimum((i + 1) * hb, n_halo - 1), 0)),
            pl.BlockSpec((1, dm), lambda i: (0, 0)),
            pl.BlockSpec((dm, f), lambda i: (0, 0), pipeline_mode=pl.Buffered(1)),
            pl.BlockSpec((dm, f), lambda i: (0, 1), pipeline_mode=pl.Buffered(1)),
            pl.BlockSpec((CONV_W, f), lambda i: (0, 0)),
            pl.BlockSpec((1, f), lambda i: (0, 0)),
            pl.BlockSpec((f, dm), lambda i: (0, 0), pipeline_mode=pl.Buffered(1)),
            pl.BlockSpec((1, dm), lambda i: (0, 0)),
        ],
        out_specs=pl.BlockSpec((tm, dm), lambda i: (i, 0)),
        out_shape=jax.ShapeDtypeStruct((m, dm), F32),
        scratch_shapes=[pltpu.VMEM((tm + 2 * HALO_ROWS, dm), BF16), pltpu.VMEM((f, dm), BF16)],
        compiler_params=_params(("arbitrary",), vmem),
        name="conv_ffn",
    )(x2, x2, x2, g.reshape(1, dm).astype(F32), w_up, w_up, conv_w.astype(F32), conv_b.reshape(1, f).astype(F32),
      w_down.astype(F32), final_g.reshape(1, dm).astype(F32))


def _arrange_w_in(w):
    widths = (512, 512, 128, 128, 512, 512, 512, 512, 16, 512, 512, 512, 4096)
    offs = [0]
    for wd in widths:
        offs.append(offs[-1] + wd)
    seg = lambda n: w[:, offs[n]:offs[n + 1]]
    a, bq, bk, bv, cq, ck, cv, co, cgate, dq, dk, dv, gates = (seg(n) for n in range(len(widths)))
    main = jnp.concatenate([a, bq, cq, ck, cv, co, dq, dk, dv, bk, bv], axis=1).astype(BF16)
    gate = jnp.pad(cgate, ((0, 0), (0, V7X_LANES - cgate.shape[1]))).astype(BF16)
    return main, gate, gates.astype(BF16)


def kernel(x, norm_mix_g, w_in, mlstm_gate_bias, qk_norm_g, mlstm_norm_g, diff_lambda, diff_norm_g, rel_bias,
           w_branch, w_out, norm_ffn_g, w_up, conv_w, conv_b, w_down, final_norm_g):
    b, s, dm = x.shape
    depth = w_in.shape[0]
    m = b * s
    d_ff = w_down.shape[1]
    L = MLSTM_CHUNK
    tl = TILES

    bd, dft_hi, dft_lo = fourier_tables(s)
    cos2, sin2 = rope_tables(s)
    bias_tiles, bias_far = diff_bias_tiles(rel_bias, tl["diff_tile"])

    x2 = x.reshape(m, dm)
    for layer in range(depth):
        w_main, w_gate, w_branch_gates = _arrange_w_in(w_in[layer])
        p2, cgate = norm_matmul(x2, norm_mix_g[layer], w_main, w_gate, tm=tl["proj_rows"], tn=P_WIDTH)
        p3 = p2.reshape(b, s, P_WIDTH)

        y_a = fourier_mix(p3, bd, dft_hi, dft_lo, tm=tl["fourier_rows"])

        q_b, k_b, v_b = gqa_prep(p2, qk_norm_g[layer], cos2, sin2, s=s, tm=tl["gqa_prep_rows"])
        kt_b = jnp.swapaxes(k_b.reshape(b, s, GQA_KV_HEADS * HEAD_DIM), 1, 2)
        y_b = gqa_attention(q_b.reshape(b, s, -1), kt_b, v_b.reshape(b, s, -1),
                            tq=tl["gqa_q_rows"], tk=tl["gqa_keys"])

        gates5 = jnp.transpose(cgate[:, :4 * MLSTM_HEADS].reshape(b, s, 4, MLSTM_HEADS), (0, 2, 3, 1))
        gates5 = gates5.reshape(b, 4, MLSTM_HEADS, s // L, L)
        y_c = mlstm_branch(p3, gates5, mlstm_gate_bias[layer], mlstm_norm_g[layer],
                           heads_per_block=tl["mlstm_heads"])

        y_d = diff_attention(p3, bias_tiles, bias_far, diff_lambda[layer], diff_norm_g[layer],
                             t=tl["diff_tile"], rows_per_item=tl["diff_q_rows"],
                             tiles_per_item=tl["diff_key_tiles"], layer_number=layer + 1)

        ys = [y.reshape(m, BRANCH_WIDTH) for y in (y_a, y_b, y_c, y_d)]
        x2 = merge_branches(ys, norm_mix_g[layer], w_branch_gates, w_branch[layer], w_out[layer], x2,
                            tm=tl["merge_rows"])

        half_lin = jnp.concatenate([jnp.ones((d_ff,), F32), jnp.full((d_ff,), 0.5, F32)])
        x2 = conv_ffn(x2, norm_ffn_g[layer], (w_up[layer] * half_lin).astype(BF16), conv_w[layer], conv_b[layer],
                      w_down[layer], final_norm_g, s=s, tm=tl["ffn_rows"], col_chunk=tl["ffn_cols"],
                      final_norm=(layer == depth - 1))
    return x2.reshape(b, s, dm)
```

```python
import functools
import math

import jax
import jax.numpy as jnp
from jax import lax
from jax.experimental import pallas as pl
from jax.experimental.pallas import tpu as pltpu

F32 = jnp.float32
BF16 = jnp.bfloat16

GRID_W = 64
HEAD_DIM = 64
BRANCH_WIDTH = 512
N_BRANCHES = 4
FOURIER_GROUP_DIM = 64
GQA_Q_HEADS = 8
GQA_KV_HEADS = 2
MLSTM_HEADS = 4
MLSTM_HEAD_DIM = 128
MLSTM_CHUNK = 128
DIFF_HEADS = 4
DIFF_QK_DIM = 64
DIFF_V_DIM = 128
REL_BUCKETS = 32
REL_MAX_DIST = 128
CONV_W = 3
ROPE_BASE = 10000.0
EPS = 1e-6
LOG2E = math.log2(math.e)

V7X_LANES = 128
V7X_VMEM_BYTES = 64 * 1024 * 1024
V7X_VMEM_CAP = V7X_VMEM_BYTES - 8 * 1024 * 1024

TILES = dict(
    proj_rows=512,
    fourier_rows=512,
    gqa_prep_rows=512,
    gqa_q_rows=512, gqa_keys=1024, gqa_row_blocks=2,
    diff_tile=512, diff_q_rows=256, diff_key_tiles=4,
    mlstm_heads=2,
    merge_rows=512,
    ffn_rows=512, ffn_cols=1024,
)
TRANSPOSE_ROWS = 512

OFF_A = 0
OFF_BQ = 512
OFF_CQ = 1024
OFF_CK = 1536
OFF_CV = 2048
OFF_CO = 2560
OFF_DQ = 3072
OFF_DK = 3584
OFF_DV = 4096
OFF_BKV = 4608
P_WIDTH = 4864


def _params(sem, vmem_bytes):
    limit = int(min(max(vmem_bytes * 3 // 2 + (4 << 20), 16 << 20), V7X_VMEM_CAP))
    return pltpu.CompilerParams(dimension_semantics=sem, vmem_limit_bytes=limit)


def _norm_mm_kernel(x_ref, g_ref, w_ref, o_ref, xn_ref):
    @pl.when(pl.program_id(1) == 0)
    def _():
        x = x_ref[...]
        ms = jnp.mean(x * x, axis=-1, keepdims=True)
        xn_ref[...] = (x * lax.rsqrt(ms + EPS) * g_ref[...]).astype(BF16)

    o_ref[...] = jnp.dot(xn_ref[...], w_ref[...], preferred_element_type=F32).astype(o_ref.dtype)


def _norm_mm_gate_kernel(x_ref, g_ref, w_ref, wg_ref, o_ref, og_ref, xn_ref):
    @pl.when(pl.program_id(1) == 0)
    def _():
        x = x_ref[...]
        ms = jnp.mean(x * x, axis=-1, keepdims=True)
        xn = (x * lax.rsqrt(ms + EPS) * g_ref[...]).astype(BF16)
        xn_ref[...] = xn
        og_ref[...] = jnp.dot(xn, wg_ref[...], preferred_element_type=F32)

    o_ref[...] = jnp.dot(xn_ref[...], w_ref[...], preferred_element_type=F32).astype(o_ref.dtype)


def norm_matmul(x, g, w, w_gate=None, *, tm, tn):
    m, k = x.shape
    n = w.shape[1]
    grid = (m // tm, n // tn)
    vmem = 2 * tm * k * 4 + tm * k * 2 + 2 * k * tn * 2 + 2 * tm * tn * 2 + 4 * tm * k
    x_spec = pl.BlockSpec((tm, k), lambda i, j: (i, 0))
    g_spec = pl.BlockSpec((1, k), lambda i, j: (0, 0))
    w_spec = pl.BlockSpec((k, tn), lambda i, j: (0, j))
    o_spec = pl.BlockSpec((tm, tn), lambda i, j: (i, j))
    scratch = [pltpu.VMEM((tm, k), BF16)]
    g2 = g.reshape(1, k).astype(F32)
    if w_gate is None:
        return pl.pallas_call(
            _norm_mm_kernel,
            grid=grid,
            in_specs=[x_spec, g_spec, w_spec],
            out_specs=o_spec,
            out_shape=jax.ShapeDtypeStruct((m, n), BF16),
            scratch_shapes=scratch,
            compiler_params=_params(("parallel", "arbitrary"), vmem),
            name="norm_matmul",
        )(x, g2, w)
    ng = w_gate.shape[1]
    return pl.pallas_call(
        _norm_mm_gate_kernel,
        grid=grid,
        in_specs=[x_spec, g_spec, w_spec, pl.BlockSpec((k, ng), lambda i, j: (0, 0))],
        out_specs=[o_spec, pl.BlockSpec((tm, ng), lambda i, j: (i, 0))],
        out_shape=[jax.ShapeDtypeStruct((m, n), BF16), jax.ShapeDtypeStruct((m, ng), F32)],
        scratch_shapes=scratch,
        compiler_params=_params(("parallel", "arbitrary"), vmem),
        name="norm_matmul_gate",
    )(x, g2, w, w_gate)


FOURIER_HALO = 16


def _fourier_kernel(a_ref, bd_ref, hi_ref, lo_ref, flip_ref, o_ref, z_ref, ext_ref, *, tm, row_chunk):
    s = a_ref.shape[1]
    w = a_ref.shape[2]
    n_lo = lo_ref.shape[1]
    i = pl.program_id(1)

    @pl.when(i == 0)
    def _():
        for r in range(0, s, row_chunk):
            a = a_ref[0, r:r + row_chunk, :]
            zc = jnp.dot(a, bd_ref[...], preferred_element_type=F32)
            z_ref[r:r + row_chunk, :] = zc[:, :w].astype(BF16)
            z_ref[s + r:s + r + row_chunk, :] = zc[:, w:].astype(BF16)

    cl, sl = lo_ref[0], lo_ref[1]
    for r in range(tm // n_lo + 1):
        rows = n_lo if r < tm // n_lo else ext_ref.shape[0] - tm
        k1 = i * (tm // n_lo) + r
        ch = hi_ref[0, pl.ds(k1, 1), :]
        nsh = hi_ref[1, pl.ds(k1, 1), :]
        ext_ref[r * n_lo:r * n_lo + rows, :s] = (cl[:rows] * ch + sl[:rows] * nsh).astype(BF16)
        ext_ref[r * n_lo:r * n_lo + rows, s:] = (cl[:rows] * nsh - sl[:rows] * ch).astype(BF16)
    p = jnp.dot(ext_ref[:, :s], z_ref[:s], preferred_element_type=F32)
    q = jnp.dot(ext_ref[:, s:], z_ref[s:], preferred_element_type=F32)
    lo = pl.multiple_of(i * tm, tm)
    o_ref[0, pl.ds(lo, tm), :] = (p[:tm] + q[:tm]).astype(o_ref.dtype)
    mirrored = (p[1:tm + 1] - q[1:tm + 1]).astype(BF16)
    hi = pl.multiple_of(s - (i + 1) * tm, tm)
    o_ref[0, pl.ds(hi, tm), :] = jnp.dot(flip_ref[...], mirrored, preferred_element_type=F32).astype(o_ref.dtype)


def fourier_tables(s):
    cg = FOURIER_GROUP_DIM
    jj = jnp.arange(cg, dtype=jnp.int32)
    ang_c = (2.0 * math.pi / cg) * ((jj[:, None] * jj[None, :]) % cg).astype(F32)
    eye_g = jnp.eye(BRANCH_WIDTH // cg, dtype=F32)
    bd_c = jnp.kron(eye_g, jnp.cos(ang_c)) * cg ** -0.5
    bd_s = jnp.kron(eye_g, jnp.sin(ang_c)) * cg ** -0.5
    bd = jnp.concatenate([bd_c, bd_s], axis=1).astype(BF16)
    n_lo = s // cg
    nn = jnp.arange(s, dtype=jnp.int32)
    k1 = jnp.arange(cg // 2 + 2, dtype=jnp.int32)
    ang_hi = (2.0 * math.pi / cg) * ((k1[:, None] * nn[None, :]) % cg).astype(F32)
    ll = jnp.arange(n_lo, dtype=jnp.int32)
    ang_lo = (2.0 * math.pi / s) * ((ll[:, None] * nn[None, :]) % s).astype(F32)
    scale = s ** -0.5
    hi = jnp.stack([jnp.cos(ang_hi), -jnp.sin(ang_hi)]) * scale
    lo = jnp.stack([jnp.cos(ang_lo), jnp.sin(ang_lo)])
    return bd, hi, lo


def fourier_mix(p3, bd, hi, lo, *, tm):
    b, s, _ = p3.shape
    w = BRANCH_WIDTH
    n_lo = lo.shape[1]
    h = FOURIER_HALO
    assert (s // 2) % tm == 0 and tm % n_lo == 0 and h <= n_lo
    r = jnp.arange(tm, dtype=jnp.int32)
    flip = (r[:, None] + r[None, :] == tm - 1).astype(BF16)
    vmem = (s * w * 2 + 2 * w * w * 2 + 2 * (hi.size + lo.size) * 4 + (tm + h) * 2 * s * 2 + 2 * s * w * 2
            + 2 * s * w * 2 + 8 * (tm + h) * w * 4 + 8 * n_lo * s * 4)
    return pl.pallas_call(
        functools.partial(_fourier_kernel, tm=tm, row_chunk=min(s, tm)),
        grid=(b, s // 2 // tm),
        in_specs=[
            pl.BlockSpec((1, s, w), lambda bi, i: (bi, 0, OFF_A // w), pipeline_mode=pl.Buffered(1)),
            pl.BlockSpec((w, 2 * w), lambda bi, i: (0, 0), pipeline_mode=pl.Buffered(1)),
            pl.BlockSpec(hi.shape, lambda bi, i: (0, 0, 0)),
            pl.BlockSpec(lo.shape, lambda bi, i: (0, 0, 0)),
            pl.BlockSpec((tm, tm), lambda bi, i: (0, 0)),
        ],
        out_specs=pl.BlockSpec((1, s, w), lambda bi, i: (bi, 0, 0)),
        out_shape=jax.ShapeDtypeStruct((b, s, w), BF16),
        scratch_shapes=[pltpu.VMEM((2 * s, w), BF16), pltpu.VMEM((tm + h, 2 * s), BF16)],
        compiler_params=_params(("parallel", "arbitrary"), vmem),
        name="fourier",
    )(p3, bd, hi, lo, flip)


def rope_tables(s):
    rows = s // GRID_W
    row_id = jnp.repeat(jnp.arange(rows, dtype=F32), GRID_W)
    col_id = jnp.tile(jnp.arange(GRID_W, dtype=F32), rows)
    n_pairs = HEAD_DIM // 4
    inv_freq = ROPE_BASE ** (-jnp.arange(n_pairs, dtype=F32) / n_pairs)
    ang = jnp.concatenate([row_id[:, None] * inv_freq, col_id[:, None] * inv_freq], axis=-1)
    cos, sin = jnp.cos(ang), jnp.sin(ang)
    return jnp.concatenate([cos, cos] * 2, axis=-1), jnp.concatenate([-sin, sin] * 2, axis=-1)


def _norm_rope(x, g, seg, cos2, sin2):
    half = HEAD_DIM // 2
    x2 = x * x
    hi = x2.astype(BF16)
    lo = (x2 - hi.astype(F32)).astype(BF16)
    ms = jnp.dot(hi, seg, preferred_element_type=F32) + jnp.dot(lo, seg, preferred_element_type=F32)
    y = x * lax.rsqrt(ms + EPS) * g
    lane = lax.broadcasted_iota(jnp.int32, (x.shape[0], V7X_LANES), 1)
    first_half = (lane % HEAD_DIM) < half
    outs = []
    for cb in range(x.shape[1] // V7X_LANES):
        yb = y[:, cb * V7X_LANES:(cb + 1) * V7X_LANES]
        rot = jnp.where(first_half, pltpu.roll(yb, V7X_LANES - half, axis=1), pltpu.roll(yb, half, axis=1))
        outs.append(yb * cos2 + rot * sin2)
    return outs[0] if len(outs) == 1 else jnp.concatenate(outs, axis=-1)


def _gqa_prep_kernel(q_ref, kv_ref, gq_ref, gk_ref, seg_ref, cos_ref, sin_ref, qo_ref, ko_ref, vo_ref):
    d = HEAD_DIM
    nk = GQA_KV_HEADS * d
    cos2, sin2 = cos_ref[...], sin_ref[...]
    q = q_ref[...].astype(F32)
    kv = kv_ref[...].astype(F32)
    qo_ref[...] = _norm_rope(q, gq_ref[...], seg_ref[...], cos2, sin2).astype(BF16)
    ko_ref[...] = _norm_rope(kv[:, :nk], gk_ref[...], seg_ref[:nk, :nk], cos2, sin2).astype(BF16)
    v = kv[:, nk:]
    lane = lax.broadcasted_iota(jnp.int32, v.shape, 1)
    ones_col = jnp.where(lane == d, 1.0, 0.0)
    vo_ref[:, :nk] = jnp.where(lane < d, v, ones_col).astype(BF16)
    vo_ref[:, nk:] = jnp.where(lane < d, pltpu.roll(v, d, axis=1), ones_col).astype(BF16)


def gqa_prep(p2, qk_g, cos2, sin2, *, s, tm):
    m = p2.shape[0]
    nq = GQA_Q_HEADS * HEAD_DIM
    nkv = GQA_KV_HEADS * HEAD_DIM
    assert nkv == V7X_LANES
    tiles_per_seq = s // tm
    q_scale = HEAD_DIM ** -0.5 * LOG2E
    gq = jnp.tile(qk_g[0].astype(F32) * q_scale, GQA_Q_HEADS).reshape(1, nq)
    gk = jnp.tile(qk_g[1].astype(F32), GQA_KV_HEADS).reshape(1, nkv)
    seg = jnp.kron(jnp.eye(GQA_Q_HEADS, dtype=F32), jnp.full((HEAD_DIM, HEAD_DIM), 1.0 / HEAD_DIM, F32)).astype(BF16)
    return pl.pallas_call(
        _gqa_prep_kernel,
        grid=(m // tm,),
        in_specs=[
            pl.BlockSpec((tm, nq), lambda i: (i, OFF_BQ // nq)),
            pl.BlockSpec((tm, 2 * nkv), lambda i: (i, OFF_BKV // (2 * nkv))),
            pl.BlockSpec((1, nq), lambda i: (0, 0)),
            pl.BlockSpec((1, nkv), lambda i: (0, 0)),
            pl.BlockSpec((nq, nq), lambda i: (0, 0)),
            pl.BlockSpec((tm, V7X_LANES), lambda i: (i % tiles_per_seq, 0)),
            pl.BlockSpec((tm, V7X_LANES), lambda i: (i % tiles_per_seq, 0)),
        ],
        out_specs=[
            pl.BlockSpec((tm, nq), lambda i: (i, 0)),
            pl.BlockSpec((tm, nkv), lambda i: (i, 0)),
            pl.BlockSpec((tm, 2 * nkv), lambda i: (i, 0)),
        ],
        out_shape=[
            jax.ShapeDtypeStruct((m, nq), BF16),
            jax.ShapeDtypeStruct((m, nkv), BF16),
            jax.ShapeDtypeStruct((m, 2 * nkv), BF16),
        ],
        compiler_params=_params(("parallel",), 16 * tm * nq * 4),
        name="gqa_prep",
    )(p2, p2, gq, gk, seg, cos2, sin2)


def _gqa_attn_kernel(q_ref, kt_ref, v_ref, o_ref, q_sc, m_ref, acc_ref, *, tq, tk):
    d = HEAD_DIM
    n_blk = q_ref.shape[1] // tq
    grp = q_ref.shape[2] // d
    s = kt_ref.shape[2]
    n_chunks = s // tk

    def base(r, g):
        return (r * grp + g) * tq

    for r in range(n_blk):
        for g in range(grp):
            q_sc[base(r, g):base(r, g) + tq, :] = q_ref[0, r * tq:(r + 1) * tq, g * d:(g + 1) * d]

    def qk(item):
        r, c, g = item
        return jnp.dot(q_sc[base(r, g):base(r, g) + tq, :], kt_ref[0, :, c * tk:(c + 1) * tk],
                       preferred_element_type=F32)

    items = [(r, c, g) for r in range(n_blk) for c in range(n_chunks) for g in range(grp)]
    sc_next = qk(items[0])
    for n, (r, c, g) in enumerate(items):
        rows = slice(base(r, g), base(r, g) + tq)
        sc = sc_next
        if n + 1 < len(items):
            sc_next = qk(items[n + 1])
        v = v_ref[0, c * tk:(c + 1) * tk, :]
        row_max = jnp.max(sc, axis=-1, keepdims=True)
        if c == 0:
            m_new = jnp.broadcast_to(row_max, (tq, V7X_LANES))
        else:
            m_prev = m_ref[rows, :]
            m_new = jnp.maximum(m_prev, row_max)
        p = jnp.exp2((sc - pltpu.repeat(m_new, tk // V7X_LANES, axis=1)).astype(BF16))
        pv = jnp.dot(p, v, preferred_element_type=F32)
        if c == 0:
            acc_ref[rows, :] = pv
        else:
            acc_ref[rows, :] = acc_ref[rows, :] * jnp.exp2(m_prev - m_new) + pv
        m_ref[rows, :] = m_new
        if c == n_chunks - 1:
            acc = acc_ref[rows, :]
            o_ref[0, r * tq:(r + 1) * tq, g * d:(g + 1) * d] = (acc[:, :d] / acc[:, d:d + 1]).astype(o_ref.dtype)


def gqa_attention(q3, kt3, v3, *, tq, tk, row_blocks):
    b, s, nq = q3.shape
    d = HEAD_DIM
    grp = GQA_Q_HEADS // GQA_KV_HEADS
    tb = row_blocks * tq
    mrows = grp * tb
    vmem = (2 * (tb * grp * d * 2 + d * s * 2 + s * 128 * 2 + tb * grp * d * 2) + 3 * mrows * 128 * 4
            + 12 * tq * tk * 4)
    return pl.pallas_call(
        functools.partial(_gqa_attn_kernel, tq=tq, tk=tk),
        grid=(b, GQA_KV_HEADS, s // tb),
        in_specs=[
            pl.BlockSpec((1, tb, grp * d), lambda bi, kv, i: (bi, i, kv)),
            pl.BlockSpec((1, d, s), lambda bi, kv, i: (bi, kv, 0)),
            pl.BlockSpec((1, s, 2 * d), lambda bi, kv, i: (bi, 0, kv)),
        ],
        out_specs=pl.BlockSpec((1, tb, grp * d), lambda bi, kv, i: (bi, i, kv)),
        out_shape=jax.ShapeDtypeStruct((b, s, nq), BF16),
        scratch_shapes=[pltpu.VMEM((mrows, d), BF16), pltpu.VMEM((mrows, V7X_LANES), F32),
                        pltpu.VMEM((mrows, 2 * d), F32)],
        compiler_params=_params(("parallel", "parallel", "parallel"), vmem),
        name="gqa_attn",
    )(q3, kt3, v3)


def _mlstm_step(chains, ms, q_ref, kt_sc, v_ref, r_sc, cm_sc, b_sc, st_sc, h_sc):
    L = MLSTM_CHUNK
    dh = MLSTM_HEAD_DIM
    assert L == dh
    scale = dh ** -0.5
    row_i = lax.broadcasted_iota(jnp.int32, (L, L), 0)
    col_i = lax.broadcasted_iota(jnp.int32, (L, L), 1)

    def col(x_row):
        return jnp.transpose(jnp.broadcast_to(x_row, (L, L)))

    pre = []
    for (hh, direction, c), m in zip(chains, ms):
        off = pl.multiple_of(c * L, L)
        lanes = slice(hh * dh, (hh + 1) * dh)
        q = q_ref[0, pl.ds(off, L), lanes]
        kt = kt_sc[hh, :, pl.ds(off, L)]
        v = v_ref[0, pl.ds(off, L), lanes]
        r_row = r_sc[hh, direction, pl.ds(c, 1), :]
        cm_row = cm_sc[hh, direction, pl.ds(c, 1), :]
        b_row = b_sc[hh, direction, pl.ds(c, 1), :]
        rmax = jnp.max(r_row, axis=-1, keepdims=True)
        btot = b_row[:, L - 1:L] if direction == 0 else b_row[:, 0:1]
        cmat = jnp.maximum(m, col(cm_row))
        mask = (row_i >= col_i) if direction == 0 else (row_i <= col_i)
        c_last = jnp.maximum(m, rmax)
        w_state = jnp.exp(r_row - c_last) * scale
        pre.append(dict(
            off=off, lanes=lanes, q=q, kt=kt,
            v_aug=jnp.concatenate([v, jnp.ones((L, dh), BF16)], axis=-1),
            w_intra=jnp.where(mask, jnp.exp(r_row - cmat), 0.0) * scale,
            w_inter=jnp.exp(m - cmat),
            den_floor=jnp.exp(-(col(b_row) + cmat)),
            decay=jnp.exp(m - c_last),
            kt_w=(kt.astype(F32) * w_state).astype(BF16),
            m_new=btot + c_last,
        ))
    s_raw = [jnp.dot(p["q"], p["kt"], preferred_element_type=F32) for p in pre]
    states = [st_sc[hh, direction] for hh, direction, _ in chains]
    inter = [jnp.dot(p["q"], st.astype(BF16), preferred_element_type=F32) for p, st in zip(pre, states)]
    upd = [jnp.dot(p["kt_w"], p["v_aug"], preferred_element_type=F32) for p in pre]
    for n, ((hh, direction, _), p) in enumerate(zip(chains, pre)):
        st_sc[hh, direction] = p["decay"] * states[n] + upd[n]
        intra = jnp.dot((s_raw[n] * p["w_intra"]).astype(BF16), p["v_aug"], preferred_element_type=F32)
        h_aug = jnp.concatenate([p["w_inter"], p["w_inter"]], axis=-1) * inter[n] + intra
        h_sc[hh, direction, pl.ds(p["off"], L), :] = h_aug[:, :dh] / jnp.maximum(jnp.abs(h_aug[:, dh:]), p["den_floor"])
    return [p["m_new"] for p in pre]


def _mlstm_kernel(bias_ref, q_ref, k_ref, v_ref, o_ref, gate_ref, g_ref, y_ref,
                  r_sc, cm_sc, b_sc, h_sc, st_sc, kt_sc):
    L = MLSTM_CHUNK
    dh = MLSTM_HEAD_DIM
    hpb = q_ref.shape[2] // dh
    head0 = pl.program_id(1) * hpb
    nc = q_ref.shape[1] // L
    lane = lax.broadcasted_iota(jnp.int32, (nc, L), 1)
    shifts = [1 << t for t in range(int(math.log2(L)))]
    for hh in range(hpb):
        for d in range(2):
            i_pre = gate_ref[0, 2 * d, hh] + bias_ref[2 * d, head0 + hh]
            f_pre = gate_ref[0, 2 * d + 1, hh] + bias_ref[2 * d + 1, head0 + hh]
            logf = jnp.minimum(f_pre, 0.0) - jnp.log1p(jnp.exp(-jnp.abs(f_pre)))
            bc = logf
            for sh in shifts:
                if d == 0:
                    bc = bc + jnp.where(lane >= sh, pltpu.roll(bc, sh, axis=1), 0.0)
                else:
                    bc = bc + jnp.where(lane < L - sh, pltpu.roll(bc, L - sh, axis=1), 0.0)
            r = i_pre - bc
            cm = r
            for sh in shifts:
                if d == 0:
                    cm = jnp.maximum(cm, jnp.where(lane >= sh, pltpu.roll(cm, sh, axis=1), -jnp.inf))
                else:
                    cm = jnp.maximum(cm, jnp.where(lane < L - sh, pltpu.roll(cm, L - sh, axis=1), -jnp.inf))
            r_sc[hh, d] = r
            cm_sc[hh, d] = cm
            b_sc[hh, d] = bc
    st_sc[...] = jnp.zeros(st_sc.shape, F32)
    tr = TRANSPOSE_ROWS
    for hh in range(hpb):
        for r in range(0, q_ref.shape[1], tr):
            kt_sc[hh, :, r:r + tr] = jnp.transpose(k_ref[0, r:r + tr, hh * dh:(hh + 1) * dh].astype(F32)).astype(BF16)

    def body(c, ms):
        chains = [(hh, d, c if d == 0 else nc - 1 - c) for hh in range(hpb) for d in range(2)]
        return tuple(_mlstm_step(chains, ms, q_ref, kt_sc, v_ref, r_sc, cm_sc, b_sc, st_sc, h_sc))

    lax.fori_loop(0, nc, body, tuple(jnp.zeros((1, 1), F32) for _ in range(2 * hpb)), unroll=2)
    for hh in range(hpb):
        lanes = slice(hh * dh, (hh + 1) * dh)
        hsum = h_sc[hh, 0] + h_sc[hh, 1]
        ms = jnp.mean(hsum * hsum, axis=-1, keepdims=True)
        y = hsum * lax.rsqrt(ms + EPS) * g_ref[:, lanes]
        y_ref[0, :, lanes] = (jax.nn.sigmoid(o_ref[0, :, lanes].astype(F32)) * y).astype(y_ref.dtype)


def mlstm_branch(p3, gates5, gate_bias, norm_g, *, heads_per_block):
    b, s, _ = p3.shape
    L = MLSTM_CHUNK
    hpb = heads_per_block
    wb = hpb * MLSTM_HEAD_DIM
    nc = s // L
    blk = lambda off: pl.BlockSpec((1, s, wb), lambda bi, h, off=off: (bi, 0, off // wb + h))
    vmem = 2 * 5 * s * wb * 2 + 2 * s * wb * 4 + 6 * hpb * nc * L * 4 + 4 * hpb * wb * wb * 4 + 3 * s * wb * 4
    return pl.pallas_call(
        _mlstm_kernel,
        grid=(b, MLSTM_HEADS // hpb),
        in_specs=[
            pl.BlockSpec(memory_space=pltpu.SMEM),
            blk(OFF_CQ), blk(OFF_CK), blk(OFF_CV), blk(OFF_CO),
            pl.BlockSpec((1, 4, hpb, nc, L), lambda bi, h: (bi, 0, h, 0, 0)),
            pl.BlockSpec((1, wb), lambda bi, h: (0, h)),
        ],
        out_specs=pl.BlockSpec((1, s, wb), lambda bi, h: (bi, 0, h)),
        out_shape=jax.ShapeDtypeStruct((b, s, BRANCH_WIDTH), BF16),
        scratch_shapes=[
            pltpu.VMEM((hpb, 2, nc, L), F32), pltpu.VMEM((hpb, 2, nc, L), F32), pltpu.VMEM((hpb, 2, nc, L), F32),
            pltpu.VMEM((hpb, 2, s, MLSTM_HEAD_DIM), F32),
            pltpu.VMEM((hpb, 2, MLSTM_HEAD_DIM, 2 * MLSTM_HEAD_DIM), F32),
            pltpu.VMEM((hpb, MLSTM_HEAD_DIM, s), BF16),
        ],
        compiler_params=_params(("parallel", "parallel"), vmem),
        name="mlstm",
    )(gate_bias.astype(F32), p3, p3, p3, p3, gates5, norm_g.reshape(1, BRANCH_WIDTH).astype(F32))


def _rel_bucket(rel):
    half = REL_BUCKETS // 2
    max_exact = half // 2
    ret = jnp.where(rel > 0, half, 0)
    n = jnp.abs(rel)
    nf = jnp.maximum(n, 1).astype(F32)
    large = max_exact + (jnp.log(nf / max_exact) / math.log(REL_MAX_DIST / max_exact) * (half - max_exact)).astype(jnp.int32)
    large = jnp.minimum(large, half - 1)
    return ret + jnp.where(n < max_exact, n, large)


def diff_bias_tiles(rel_bias, t):
    assert t >= REL_MAX_DIST
    k = jnp.arange(2 * t, dtype=jnp.int32)
    rel = jnp.arange(-2, 3, dtype=jnp.int32)[:, None] * t + jnp.where(k < t, k, k - 2 * t)[None, :]
    onehot = (_rel_bucket(rel)[:, :, None] == jnp.arange(REL_BUCKETS, dtype=jnp.int32)).astype(F32)
    period = jnp.einsum('dkb,bh->hdk', onehot, rel_bias.astype(F32) * LOG2E, precision=lax.Precision.HIGHEST)
    far = period[:, 0::4, 0]
    return period, far


def _diff_attn_kernel(far_ref, q_ref, k_ref, v_ref, period_ref, lam_ref, g_ref, o_ref,
                      q_sc, kt_sc, vaug_sc, bias_sc, sc_buf, m_ref, acc_ref, *, t, lam_init):
    dq = DIFF_QK_DIM
    dv = DIFF_V_DIM
    s = k_ref.shape[1]
    nt = s // t
    head = pl.program_id(1)
    ahead = sc_buf.shape[0]
    rb = sc_buf.shape[1]
    grp = sc_buf.shape[2] // t
    wrap = (grp - 1) * t
    assert nt % grp == 0 and t % rb == 0 and nt >= 4

    tr = max(t, TRANSPOSE_ROWS)
    for r in range(0, s, tr):
        kt_sc[:, r:r + tr] = jnp.transpose(k_ref[0, r:r + tr, :].astype(F32)).astype(BF16)
    kt_sc[:, s:] = kt_sc[:, :wrap]
    vaug_sc[:s, :dv] = v_ref[0]
    vaug_sc[s:, :dv] = v_ref[0, :wrap, :]
    lane = lax.broadcasted_iota(jnp.int32, (s + wrap, dv), 1)
    vaug_sc[:, dv:] = jnp.where(lane == 0, 1.0, 0.0).astype(BF16)
    for dl in range(period_ref.shape[1]):
        full = jnp.broadcast_to(period_ref[0, dl:dl + 1, :], (t, 2 * t))
        bias_sc[dl] = pltpu.roll(full, 0, axis=1, stride=1, stride_axis=0)[:, :t]
    q_scale = dq ** -0.5 * LOG2E
    q_all = (q_ref[0].astype(F32) * q_scale).astype(BF16)
    q_sc[0] = q_all[:, :dq]
    q_sc[1] = q_all[:, dq:]
    lp = lam_ref[...]
    lam = (jnp.exp(jnp.sum(lp[0:1] * lp[1:2], axis=-1, keepdims=True))
           - jnp.exp(jnp.sum(lp[2:3] * lp[3:4], axis=-1, keepdims=True)) + lam_init)

    def key_tile(i, delta):
        j = lax.rem(i + delta, nt)
        return j, pl.multiple_of(j * t, t)

    def qk(i, item):
        pair, mp, hf = item
        row0 = pl.multiple_of(i * t + hf * rb, rb)
        _, off = key_tile(i, grp * pair)
        return jnp.dot(q_sc[mp, pl.ds(row0, rb), :], kt_sc[mp * dq:(mp + 1) * dq, pl.ds(off, grp * t)],
                       preferred_element_type=F32)

    items = [(pair, mp, hf) for pair in range(nt // grp) for mp in range(2) for hf in range(t // rb)]
    for a in range(ahead):
        sc_buf[a] = qk(0, items[a])

    def tile_body(i, carry):
        pending = [sc_buf[a] for a in range(ahead)]
        for n, (pair, mp, hf) in enumerate(items):
            sc = pending.pop(0)
            if n + ahead < len(items):
                pending.append(qk(i, items[n + ahead]))
            else:
                pending.append(qk(jnp.minimum(i + 1, nt - 1), items[n + ahead - len(items)]))
            rows = slice(hf * rb, (hf + 1) * rb)
            scs = [sc[:, p * t:(p + 1) * t] for p in range(grp)]
            _, off = key_tile(i, grp * pair)
            shifts, row_max = [], None
            for piece, delta in enumerate(range(grp * pair, grp * pair + grp)):
                j, _ = key_tile(i, delta)
                if delta in (0, 1, nt - 1):
                    tile = 2 if delta == 0 else jnp.clip(j - i, -2, 2) + 2
                    scs[piece] = scs[piece] + bias_sc[tile, rows, :]
                    shifts.append(None)
                    rm = jnp.max(scs[piece], axis=-1, keepdims=True)
                else:
                    shifts.append(jnp.where(j > i, far_ref[head, 1], far_ref[head, 0]))
                    rm = jnp.max(scs[piece], axis=-1, keepdims=True) + shifts[-1]
                row_max = rm if row_max is None else jnp.maximum(row_max, rm)
            if pair == 0:
                m_new = jnp.broadcast_to(row_max, (rb, V7X_LANES))
            else:
                m_prev = m_ref[mp, rows, :]
                m_new = jnp.maximum(m_prev, row_max)
            ps = []
            for piece in range(grp):
                m_sub = m_new if shifts[piece] is None else m_new - shifts[piece]
                ps.append(jnp.exp2((scs[piece] - pltpu.repeat(m_sub, t // V7X_LANES, axis=1)).astype(BF16)))
            pv = jnp.dot(jnp.concatenate(ps, axis=1), vaug_sc[pl.ds(off, grp * t), :], preferred_element_type=F32)
            if pair == 0:
                acc_ref[mp, rows, :] = pv
            else:
                alpha = jnp.exp2(m_prev - m_new)
                acc_ref[mp, rows, :] = acc_ref[mp, rows, :] * pltpu.repeat(alpha, 2 * dv // V7X_LANES, axis=1) + pv
            m_ref[mp, rows, :] = m_new
        for a in range(ahead):
            sc_buf[a] = pending[a]
        o0 = acc_ref[0, :, :dv] / acc_ref[0, :, dv:dv + 1]
        o1 = acc_ref[1, :, :dv] / acc_ref[1, :, dv:dv + 1]
        o = o0 - lam * o1
        ms = jnp.mean(o * o, axis=-1, keepdims=True)
        row0 = pl.multiple_of(i * t, t)
        o_ref[0, pl.ds(row0, t), :] = (o * lax.rsqrt(ms + EPS) * g_ref[...] * (1.0 - lam_init)).astype(o_ref.dtype)
        return carry

    lax.fori_loop(0, nt, tile_body, 0, unroll=4)


def diff_attention(p3, bias_period, bias_far, lam_params, sub_g, *, t, rows_per_item, tiles_per_item, layer_number):
    b, s, _ = p3.shape
    dv = DIFF_V_DIM
    n_off = bias_period.shape[1]
    assert s // t >= 4, "tiles 2 .. s/t-2 steps away from the query tile must all be beyond REL_MAX_DIST"
    lam_init = 0.8 - 0.6 * math.exp(-0.3 * (layer_number - 1))
    sw = s + (tiles_per_item - 1) * t
    item = rows_per_item * tiles_per_item * t
    vmem = (2 * (s * 128 * 2 + 128 * s * 2 + s * dv * 2 + s * dv * 2) + n_off * t * t * 4 + sw * (2 * dv + 128) * 2
            + 2 * s * 128 * 2 + item * 4 + 2 * t * 128 * 4 * 3 + 8 * item * 4)
    return pl.pallas_call(
        functools.partial(_diff_attn_kernel, t=t, lam_init=lam_init),
        grid=(b, DIFF_HEADS),
        in_specs=[
            pl.BlockSpec(memory_space=pltpu.SMEM),
            pl.BlockSpec((1, s, 128), lambda bi, h: (bi, 0, OFF_DQ // 128 + h)),
            pl.BlockSpec((1, s, 128), lambda bi, h: (bi, 0, OFF_DK // 128 + h)),
            pl.BlockSpec((1, s, dv), lambda bi, h: (bi, 0, OFF_DV // dv + h)),
            pl.BlockSpec((1, n_off, 2 * t), lambda bi, h: (h, 0, 0)),
            pl.BlockSpec((4, DIFF_QK_DIM), lambda bi, h: (0, 0)),
            pl.BlockSpec((1, dv), lambda bi, h: (0, 0)),
        ],
        out_specs=pl.BlockSpec((1, s, dv), lambda bi, h: (bi, 0, h)),
        out_shape=jax.ShapeDtypeStruct((b, s, BRANCH_WIDTH), BF16),
        scratch_shapes=[pltpu.VMEM((2, s, DIFF_QK_DIM), BF16), pltpu.VMEM((2 * DIFF_QK_DIM, sw), BF16),
                        pltpu.VMEM((sw, 2 * dv), BF16),
                        pltpu.VMEM((n_off, t, t), F32), pltpu.VMEM((1, rows_per_item, tiles_per_item * t), F32),
                        pltpu.VMEM((2, t, V7X_LANES), F32), pltpu.VMEM((2, t, 2 * dv), F32)],
        compiler_params=_params(("parallel", "parallel"), vmem),
        name="diff_attn",
    )(bias_far, p3, p3, p3, bias_period, lam_params.astype(F32), sub_g.reshape(1, dv).astype(F32))


def _merge_kernel(ya_ref, yb_ref, yc_ref, yd_ref, g_ref, wg_ref, wb_ref, wo_ref, x_ref, o_ref, wb_sc, wo_sc):
    dm = x_ref.shape[1]

    @pl.when(pl.program_id(0) == 0)
    def _():
        for n in range(wb_ref.shape[0]):
            wb_sc[n] = wb_ref[n].astype(BF16)
        wo_sc[...] = wo_ref[...].astype(BF16)

    tm = x_ref.shape[0]
    halves = [slice(0, tm // 2), slice(tm // 2, tm)]
    y_refs = (ya_ref, yb_ref, yc_ref, yd_ref)
    gates, branch = [], []
    for rows in halves:
        x = x_ref[rows, :]
        ms = jnp.mean(x * x, axis=-1, keepdims=True)
        u = (x * lax.rsqrt(ms + EPS) * g_ref[...]).astype(BF16)
        gates.append([jnp.dot(u, wg_ref[:, n * dm:(n + 1) * dm], preferred_element_type=F32)
                      for n in range(len(y_refs))])
        branch.append([jnp.dot(y_ref[rows, :], wb_sc[n], preferred_element_type=F32)
                       for n, y_ref in enumerate(y_refs)])
    for rows, gts, brs in zip(halves, gates, branch):
        merged = None
        for gt, br in zip(gts, brs):
            term = jax.nn.sigmoid(gt) * br
            merged = term if merged is None else merged + term
        o_ref[rows, :] = x_ref[rows, :] + jnp.dot(merged.astype(BF16), wo_sc[...], preferred_element_type=F32)


def merge_branches(ys, norm_g, w_gates, w_branch, w_out, x2, *, tm):
    m, dm = x2.shape
    w = BRANCH_WIDTH
    y_spec = pl.BlockSpec((tm, w), lambda i: (i, 0))
    vmem = (2 * (4 * tm * w * 2 + 2 * tm * dm * 4) + N_BRANCHES * dm * dm * 2 + (4 * w * dm + dm * dm) * (4 + 2)
            + 12 * tm * dm * 4)
    return pl.pallas_call(
        _merge_kernel,
        grid=(m // tm,),
        in_specs=[
            y_spec, y_spec, y_spec, y_spec,
            pl.BlockSpec((1, dm), lambda i: (0, 0)),
            pl.BlockSpec((dm, N_BRANCHES * dm), lambda i: (0, 0), pipeline_mode=pl.Buffered(1)),
            pl.BlockSpec((N_BRANCHES, w, dm), lambda i: (0, 0, 0), pipeline_mode=pl.Buffered(1)),
            pl.BlockSpec((dm, dm), lambda i: (0, 0), pipeline_mode=pl.Buffered(1)),
            pl.BlockSpec((tm, dm), lambda i: (i, 0)),
        ],
        out_specs=pl.BlockSpec((tm, dm), lambda i: (i, 0)),
        out_shape=jax.ShapeDtypeStruct((m, dm), F32),
        scratch_shapes=[pltpu.VMEM((N_BRANCHES, w, dm), BF16), pltpu.VMEM((dm, dm), BF16)],
        compiler_params=_params(("arbitrary",), vmem),
        name="merge",
    )(*ys, norm_g.reshape(1, dm).astype(F32), w_gates, w_branch.astype(F32), w_out.astype(F32), x2)


HALO_ROWS = 16


def _conv_ffn_kernel(x_ref, xp_ref, xn_ref, g_ref, wa_ref, wl_ref, cw_ref, cb_ref, wd_ref, gf_ref, o_ref,
                     xe_sc, wd_sc, *, tiles_per_seq, col_chunk, final_norm):
    tm = o_ref.shape[0]
    f = wa_ref.shape[1]
    h = HALO_ROWS
    pos = pl.program_id(0) % tiles_per_seq

    @pl.when(pl.program_id(0) == 0)
    def _():
        wd_sc[...] = wd_ref[...].astype(BF16)

    def norm(x):
        ms = jnp.mean(x * x, axis=-1, keepdims=True)
        return x * lax.rsqrt(ms + EPS) * g_ref[...]

    xe_sc[:h] = jnp.where(pos == 0, 0.0, norm(xp_ref[...])).astype(BF16)
    xe_sc[h:h + tm] = norm(x_ref[...]).astype(BF16)
    xe_sc[h + tm:] = jnp.where(pos == tiles_per_seq - 1, 0.0, norm(xn_ref[...])).astype(BF16)
    k0 = math.sqrt(2.0 / math.pi)
    bounds = list(range(0, f, col_chunk)) + [f]
    chunks = [slice(lo, hi) for lo, hi in zip(bounds[:-1], bounds[1:])]

    def project(cols):
        a_ext = jnp.dot(xe_sc[...], wa_ref[:, cols], preferred_element_type=F32)
        lin = jnp.dot(xe_sc[h:h + tm, :], wl_ref[:, cols], preferred_element_type=F32)
        return a_ext, lin

    y = x_ref[...]
    nxt = project(chunks[0])
    for n, cols in enumerate(chunks):
        a_ext, lin = nxt
        if n + 1 < len(chunks):
            nxt = project(chunks[n + 1])
        cw = cw_ref[:, cols]
        c = (a_ext[h - 1:h - 1 + tm] * cw[0:1] + a_ext[h:h + tm] * cw[1:2] + a_ext[h + 1:h + 1 + tm] * cw[2:3]
             + cb_ref[:, cols])
        t = jnp.tanh(c * ((c * c) * (k0 * 0.044715) + k0))
        hmid = ((c + c * t) * lin).astype(BF16)
        y = y + jnp.dot(hmid, wd_sc[cols, :], preferred_element_type=F32)
    if final_norm:
        ms = jnp.mean(y * y, axis=-1, keepdims=True)
        y = y * lax.rsqrt(ms + EPS) * gf_ref[...]
    o_ref[...] = y


def conv_ffn(x2, g, w_up, conv_w, conv_b, w_down, final_g, *, s, tm, col_chunk, final_norm):
    m, dm = x2.shape
    f = w_up.shape[1] // 2
    tiles_per_seq = s // tm
    hb = tm // HALO_ROWS
    n_halo = m // HALO_ROWS
    vmem = (2 * 2 * tm * dm * 4 + 2 * dm * f * 2 + f * dm * (4 + 2) + (tm + 2 * HALO_ROWS) * dm * 2
            + 6 * tm * col_chunk * 4 + 2 * tm * dm * 4)
    return pl.pallas_call(
        functools.partial(_conv_ffn_kernel, tiles_per_seq=tiles_per_seq, col_chunk=col_chunk, final_norm=final_norm),
        grid=(m // tm,),
        in_specs=[
            pl.BlockSpec((tm, dm), lambda i: (i, 0)),
            pl.BlockSpec((HALO_ROWS, dm), lambda i: (jnp.maximum(i * hb - 1, 0), 0)),
            pl.BlockSpec((HALO_ROWS, dm), lambda i: (jnp.minimum((i + 1) * hb, n_halo - 1), 0)),
            pl.BlockSpec((1, dm), lambda i: (0, 0)),
            pl.BlockSpec((dm, f), lambda i: (0, 0), pipeline_mode=pl.Buffered(1)),
            pl.BlockSpec((dm, f), lambda i: (0, 1), pipeline_mode=pl.Buffered(1)),
            pl.BlockSpec((CONV_W, f), lambda i: (0, 0)),
            pl.BlockSpec((1, f), lambda i: (0, 0)),
            pl.BlockSpec((f, dm), lambda i: (0, 0), pipeline_mode=pl.Buffered(1)),
            pl.BlockSpec((1, dm), lambda i: (0, 0)),
        ],
        out_specs=pl.BlockSpec((tm, dm), lambda i: (i, 0)),
        out_shape=jax.ShapeDtypeStruct((m, dm), F32),
        scratch_shapes=[pltpu.VMEM((tm + 2 * HALO_ROWS, dm), BF16), pltpu.VMEM((f, dm), BF16)],
        compiler_params=_params(("arbitrary",), vmem),
        name="conv_ffn",
    )(x2, x2, x2, g.reshape(1, dm).astype(F32), w_up, w_up, conv_w.astype(F32), conv_b.reshape(1, f).astype(F32),
      w_down.astype(F32), final_g.reshape(1, dm).astype(F32))


def _arrange_w_in(w):
    widths = (512, 512, 128, 128, 512, 512, 512, 512, 16, 512, 512, 512, 4096)
    offs = [0]
    for wd in widths:
        offs.append(offs[-1] + wd)
    seg = lambda n: w[:, offs[n]:offs[n + 1]]
    a, bq, bk, bv, cq, ck, cv, co, cgate, dq, dk, dv, gates = (seg(n) for n in range(len(widths)))
    main = jnp.concatenate([a, bq, cq, ck, cv, co, dq, dk, dv, bk, bv], axis=1).astype(BF16)
    gate = jnp.pad(cgate, ((0, 0), (0, V7X_LANES - cgate.shape[1]))).astype(BF16)
    return main, gate, gates.astype(BF16)


def kernel(x, norm_mix_g, w_in, mlstm_gate_bias, qk_norm_g, mlstm_norm_g, diff_lambda, diff_norm_g, rel_bias,
           w_branch, w_out, norm_ffn_g, w_up, conv_w, conv_b, w_down, final_norm_g):
    b, s, dm = x.shape
    depth = w_in.shape[0]
    m = b * s
    d_ff = w_down.shape[1]
    L = MLSTM_CHUNK
    tl = TILES

    bd, dft_hi, dft_lo = fourier_tables(s)
    cos2, sin2 = rope_tables(s)
    bias_tiles, bias_far = diff_bias_tiles(rel_bias, tl["diff_tile"])

    x2 = x.reshape(m, dm)
    for layer in range(depth):
        w_main, w_gate, w_branch_gates = _arrange_w_in(w_in[layer])
        p2, cgate = norm_matmul(x2, norm_mix_g[layer], w_main, w_gate, tm=tl["proj_rows"], tn=P_WIDTH)
        p3 = p2.reshape(b, s, P_WIDTH)

        y_a = fourier_mix(p3, bd, dft_hi, dft_lo, tm=tl["fourier_rows"])

        q_b, k_b, v_b = gqa_prep(p2, qk_norm_g[layer], cos2, sin2, s=s, tm=tl["gqa_prep_rows"])
        kt_b = jnp.swapaxes(k_b.reshape(b, s, GQA_KV_HEADS * HEAD_DIM), 1, 2)
        y_b = gqa_attention(q_b.reshape(b, s, -1), kt_b, v_b.reshape(b, s, -1),
                            tq=tl["gqa_q_rows"], tk=tl["gqa_keys"], row_blocks=tl["gqa_row_blocks"])

        gates5 = jnp.transpose(cgate[:, :4 * MLSTM_HEADS].reshape(b, s, 4, MLSTM_HEADS), (0, 2, 3, 1))
        gates5 = gates5.reshape(b, 4, MLSTM_HEADS, s // L, L)
        y_c = mlstm_branch(p3, gates5, mlstm_gate_bias[layer], mlstm_norm_g[layer],
                           heads_per_block=tl["mlstm_heads"])

        y_d = diff_attention(p3, bias_tiles, bias_far, diff_lambda[layer], diff_norm_g[layer],
                             t=tl["diff_tile"], rows_per_item=tl["diff_q_rows"],
                             tiles_per_item=tl["diff_key_tiles"], layer_number=layer + 1)

        ys = [y.reshape(m, BRANCH_WIDTH) for y in (y_a, y_b, y_c, y_d)]
        x2 = merge_branches(ys, norm_mix_g[layer], w_branch_gates, w_branch[layer], w_out[layer], x2,
                            tm=tl["merge_rows"])

        half_lin = jnp.concatenate([jnp.ones((d_ff,), F32), jnp.full((d_ff,), 0.5, F32)])
        x2 = conv_ffn(x2, norm_ffn_g[layer], (w_up[layer] * half_lin).astype(BF16), conv_w[layer], conv_b[layer],
                      w_down[layer], final_norm_g, s=s, tm=tl["ffn_rows"], col_chunk=tl["ffn_cols"],
                      final_norm=(layer == depth - 1))
    return x2.reshape(b, s, dm)
```

```python
import functools
import math

import jax
import jax.numpy as jnp
from jax import lax
from jax.experimental import pallas as pl
from jax.experimental.pallas import tpu as pltpu

F32 = jnp.float32
BF16 = jnp.bfloat16

GRID_W = 64
HEAD_DIM = 64
BRANCH_WIDTH = 512
N_BRANCHES = 4
FOURIER_GROUP_DIM = 64
GQA_Q_HEADS = 8
GQA_KV_HEADS = 2
MLSTM_HEADS = 4
MLSTM_HEAD_DIM = 128
MLSTM_CHUNK = 128
DIFF_HEADS = 4
DIFF_QK_DIM = 64
DIFF_V_DIM = 128
REL_BUCKETS = 32
REL_MAX_DIST = 128
CONV_W = 3
ROPE_BASE = 10000.0
EPS = 1e-6
LOG2E = math.log2(math.e)

V7X_LANES = 128
V7X_VMEM_BYTES = 64 * 1024 * 1024
V7X_VMEM_CAP = V7X_VMEM_BYTES - 8 * 1024 * 1024

TILES = dict(
    proj_rows=512,
    fourier_rows=512,
    gqa_prep_rows=512,
    gqa_q_rows=512, gqa_keys=1024, gqa_row_blocks=2,
    diff_tile=512, diff_q_rows=256, diff_key_tiles=4,
    mlstm_heads=2,
    merge_rows=512,
    ffn_rows=512, ffn_cols=1024,
)
TRANSPOSE_ROWS = 512

OFF_A = 0
OFF_BQ = 512
OFF_CQ = 1024
OFF_CK = 1536
OFF_CV = 2048
OFF_CO = 2560
OFF_DQ = 3072
OFF_DK = 3584
OFF_DV = 4096
OFF_BKV = 4608
P_WIDTH = 4864


def _params(sem, vmem_bytes):
    limit = int(min(max(vmem_bytes * 3 // 2 + (4 << 20), 16 << 20), V7X_VMEM_CAP))
    return pltpu.CompilerParams(dimension_semantics=sem, vmem_limit_bytes=limit)


def _norm_mm_gate_kernel(x_ref, g_ref, w_ref, wg_ref, o_ref, og_ref, xn_ref):
    @pl.when(pl.program_id(1) == 0)
    def _():
        x = x_ref[...]
        ms = jnp.mean(x * x, axis=-1, keepdims=True)
        xn = (x * lax.rsqrt(ms + EPS) * g_ref[...]).astype(BF16)
        xn_ref[...] = xn
        og_ref[...] = jnp.dot(xn, wg_ref[...], preferred_element_type=F32)

    o_ref[...] = jnp.dot(xn_ref[...], w_ref[...], preferred_element_type=F32).astype(o_ref.dtype)


def norm_matmul(x, g, w, w_gate, *, tm, tn):
    m, k = x.shape
    n = w.shape[1]
    grid = (m // tm, n // tn)
    vmem = 2 * tm * k * 4 + tm * k * 2 + 2 * k * tn * 2 + 2 * tm * tn * 2 + 4 * tm * k
    x_spec = pl.BlockSpec((tm, k), lambda i, j: (i, 0))
    g_spec = pl.BlockSpec((1, k), lambda i, j: (0, 0))
    w_spec = pl.BlockSpec((k, tn), lambda i, j: (0, j))
    o_spec = pl.BlockSpec((tm, tn), lambda i, j: (i, j))
    scratch = [pltpu.VMEM((tm, k), BF16)]
    g2 = g.reshape(1, k).astype(F32)
    ng = w_gate.shape[1]
    return pl.pallas_call(
        _norm_mm_gate_kernel,
        grid=grid,
        in_specs=[x_spec, g_spec, w_spec, pl.BlockSpec((k, ng), lambda i, j: (0, 0))],
        out_specs=[o_spec, pl.BlockSpec((tm, ng), lambda i, j: (i, 0))],
        out_shape=[jax.ShapeDtypeStruct((m, n), BF16), jax.ShapeDtypeStruct((m, ng), F32)],
        scratch_shapes=scratch,
        compiler_params=_params(("parallel", "arbitrary"), vmem),
        name="norm_matmul_gate",
    )(x, g2, w, w_gate)


FOURIER_HALO = 16


def _fourier_kernel(a_ref, bd_ref, hi_ref, lo_ref, flip_ref, o_ref, z_ref, ext_ref, *, tm, row_chunk):
    s = a_ref.shape[1]
    w = a_ref.shape[2]
    n_lo = lo_ref.shape[1]
    i = pl.program_id(1)

    @pl.when(i == 0)
    def _():
        for r in range(0, s, row_chunk):
            a = a_ref[0, r:r + row_chunk, :]
            zc = jnp.dot(a, bd_ref[...], preferred_element_type=F32)
            z_ref[r:r + row_chunk, :] = zc[:, :w].astype(BF16)
            z_ref[s + r:s + r + row_chunk, :] = zc[:, w:].astype(BF16)

    cl, sl = lo_ref[0], lo_ref[1]
    for r in range(tm // n_lo + 1):
        rows = n_lo if r < tm // n_lo else ext_ref.shape[0] - tm
        k1 = i * (tm // n_lo) + r
        ch = hi_ref[0, pl.ds(k1, 1), :]
        nsh = hi_ref[1, pl.ds(k1, 1), :]
        ext_ref[r * n_lo:r * n_lo + rows, :s] = (cl[:rows] * ch + sl[:rows] * nsh).astype(BF16)
        ext_ref[r * n_lo:r * n_lo + rows, s:] = (cl[:rows] * nsh - sl[:rows] * ch).astype(BF16)
    p = jnp.dot(ext_ref[:, :s], z_ref[:s], preferred_element_type=F32)
    q = jnp.dot(ext_ref[:, s:], z_ref[s:], preferred_element_type=F32)
    lo = pl.multiple_of(i * tm, tm)
    o_ref[0, pl.ds(lo, tm), :] = (p[:tm] + q[:tm]).astype(o_ref.dtype)
    mirrored = (p[1:tm + 1] - q[1:tm + 1]).astype(BF16)
    hi = pl.multiple_of(s - (i + 1) * tm, tm)
    o_ref[0, pl.ds(hi, tm), :] = jnp.dot(flip_ref[...], mirrored, preferred_element_type=F32).astype(o_ref.dtype)


def fourier_tables(s):
    cg = FOURIER_GROUP_DIM
    jj = jnp.arange(cg, dtype=jnp.int32)
    ang_c = (2.0 * math.pi / cg) * ((jj[:, None] * jj[None, :]) % cg).astype(F32)
    eye_g = jnp.eye(BRANCH_WIDTH // cg, dtype=F32)
    bd_c = jnp.kron(eye_g, jnp.cos(ang_c)) * cg ** -0.5
    bd_s = jnp.kron(eye_g, jnp.sin(ang_c)) * cg ** -0.5
    bd = jnp.concatenate([bd_c, bd_s], axis=1).astype(BF16)
    n_lo = s // cg
    nn = jnp.arange(s, dtype=jnp.int32)
    k1 = jnp.arange(cg // 2 + 2, dtype=jnp.int32)
    ang_hi = (2.0 * math.pi / cg) * ((k1[:, None] * nn[None, :]) % cg).astype(F32)
    ll = jnp.arange(n_lo, dtype=jnp.int32)
    ang_lo = (2.0 * math.pi / s) * ((ll[:, None] * nn[None, :]) % s).astype(F32)
    scale = s ** -0.5
    hi = jnp.stack([jnp.cos(ang_hi), -jnp.sin(ang_hi)]) * scale
    lo = jnp.stack([jnp.cos(ang_lo), jnp.sin(ang_lo)])
    return bd, hi, lo


def fourier_mix(p3, bd, hi, lo, *, tm):
    b, s, _ = p3.shape
    w = BRANCH_WIDTH
    n_lo = lo.shape[1]
    h = FOURIER_HALO
    assert (s // 2) % tm == 0 and tm % n_lo == 0 and h <= n_lo
    r = jnp.arange(tm, dtype=jnp.int32)
    flip = (r[:, None] + r[None, :] == tm - 1).astype(BF16)
    vmem = (s * w * 2 + 2 * w * w * 2 + 2 * (hi.size + lo.size) * 4 + (tm + h) * 2 * s * 2 + 2 * s * w * 2
            + 2 * s * w * 2 + 8 * (tm + h) * w * 4 + 8 * n_lo * s * 4)
    return pl.pallas_call(
        functools.partial(_fourier_kernel, tm=tm, row_chunk=min(s, tm)),
        grid=(b, s // 2 // tm),
        in_specs=[
            pl.BlockSpec((1, s, w), lambda bi, i: (bi, 0, OFF_A // w), pipeline_mode=pl.Buffered(1)),
            pl.BlockSpec((w, 2 * w), lambda bi, i: (0, 0), pipeline_mode=pl.Buffered(1)),
            pl.BlockSpec(hi.shape, lambda bi, i: (0, 0, 0)),
            pl.BlockSpec(lo.shape, lambda bi, i: (0, 0, 0)),
            pl.BlockSpec((tm, tm), lambda bi, i: (0, 0)),
        ],
        out_specs=pl.BlockSpec((1, s, w), lambda bi, i: (bi, 0, 0)),
        out_shape=jax.ShapeDtypeStruct((b, s, w), BF16),
        scratch_shapes=[pltpu.VMEM((2 * s, w), BF16), pltpu.VMEM((tm + h, 2 * s), BF16)],
        compiler_params=_params(("parallel", "arbitrary"), vmem),
        name="fourier",
    )(p3, bd, hi, lo, flip)


def rope_tables(s):
    rows = s // GRID_W
    row_id = jnp.repeat(jnp.arange(rows, dtype=F32), GRID_W)
    col_id = jnp.tile(jnp.arange(GRID_W, dtype=F32), rows)
    n_pairs = HEAD_DIM // 4
    inv_freq = ROPE_BASE ** (-jnp.arange(n_pairs, dtype=F32) / n_pairs)
    ang = jnp.concatenate([row_id[:, None] * inv_freq, col_id[:, None] * inv_freq], axis=-1)
    cos, sin = jnp.cos(ang), jnp.sin(ang)
    return jnp.concatenate([cos, cos] * 2, axis=-1), jnp.concatenate([-sin, sin] * 2, axis=-1)


def _norm_rope(x, g, seg, cos2, sin2):
    half = HEAD_DIM // 2
    x2 = x * x
    hi = x2.astype(BF16)
    lo = (x2 - hi.astype(F32)).astype(BF16)
    ms = jnp.dot(hi, seg, preferred_element_type=F32) + jnp.dot(lo, seg, preferred_element_type=F32)
    y = x * lax.rsqrt(ms + EPS) * g
    lane = lax.broadcasted_iota(jnp.int32, (x.shape[0], V7X_LANES), 1)
    first_half = (lane % HEAD_DIM) < half
    outs = []
    for cb in range(x.shape[1] // V7X_LANES):
        yb = y[:, cb * V7X_LANES:(cb + 1) * V7X_LANES]
        rot = jnp.where(first_half, pltpu.roll(yb, V7X_LANES - half, axis=1), pltpu.roll(yb, half, axis=1))
        outs.append(yb * cos2 + rot * sin2)
    return outs[0] if len(outs) == 1 else jnp.concatenate(outs, axis=-1)


def _gqa_prep_kernel(q_ref, kv_ref, gq_ref, gk_ref, seg_ref, cos_ref, sin_ref, qo_ref, ko_ref, vo_ref):
    d = HEAD_DIM
    nk = GQA_KV_HEADS * d
    cos2, sin2 = cos_ref[...], sin_ref[...]
    q = q_ref[...].astype(F32)
    kv = kv_ref[...].astype(F32)
    qo_ref[...] = _norm_rope(q, gq_ref[...], seg_ref[...], cos2, sin2).astype(BF16)
    ko_ref[...] = _norm_rope(kv[:, :nk], gk_ref[...], seg_ref[:nk, :nk], cos2, sin2).astype(BF16)
    v = kv[:, nk:]
    lane = lax.broadcasted_iota(jnp.int32, v.shape, 1)
    ones_col = jnp.where(lane == d, 1.0, 0.0)
    vo_ref[:, :nk] = jnp.where(lane < d, v, ones_col).astype(BF16)
    vo_ref[:, nk:] = jnp.where(lane < d, pltpu.roll(v, d, axis=1), ones_col).astype(BF16)


def gqa_prep(p2, qk_g, cos2, sin2, *, s, tm):
    m = p2.shape[0]
    nq = GQA_Q_HEADS * HEAD_DIM
    nkv = GQA_KV_HEADS * HEAD_DIM
    assert nkv == V7X_LANES
    tiles_per_seq = s // tm
    q_scale = HEAD_DIM ** -0.5 * LOG2E
    gq = jnp.tile(qk_g[0].astype(F32) * q_scale, GQA_Q_HEADS).reshape(1, nq)
    gk = jnp.tile(qk_g[1].astype(F32), GQA_KV_HEADS).reshape(1, nkv)
    seg = jnp.kron(jnp.eye(GQA_Q_HEADS, dtype=F32), jnp.full((HEAD_DIM, HEAD_DIM), 1.0 / HEAD_DIM, F32)).astype(BF16)
    return pl.pallas_call(
        _gqa_prep_kernel,
        grid=(m // tm,),
        in_specs=[
            pl.BlockSpec((tm, nq), lambda i: (i, OFF_BQ // nq)),
            pl.BlockSpec((tm, 2 * nkv), lambda i: (i, OFF_BKV // (2 * nkv))),
            pl.BlockSpec((1, nq), lambda i: (0, 0)),
            pl.BlockSpec((1, nkv), lambda i: (0, 0)),
            pl.BlockSpec((nq, nq), lambda i: (0, 0)),
            pl.BlockSpec((tm, V7X_LANES), lambda i: (i % tiles_per_seq, 0)),
            pl.BlockSpec((tm, V7X_LANES), lambda i: (i % tiles_per_seq, 0)),
        ],
        out_specs=[
            pl.BlockSpec((tm, nq), lambda i: (i, 0)),
            pl.BlockSpec((tm, nkv), lambda i: (i, 0)),
            pl.BlockSpec((tm, 2 * nkv), lambda i: (i, 0)),
        ],
        out_shape=[
            jax.ShapeDtypeStruct((m, nq), BF16),
            jax.ShapeDtypeStruct((m, nkv), BF16),
            jax.ShapeDtypeStruct((m, 2 * nkv), BF16),
        ],
        compiler_params=_params(("parallel",), 16 * tm * nq * 4),
        name="gqa_prep",
    )(p2, p2, gq, gk, seg, cos2, sin2)


def _gqa_attn_kernel(q_ref, kt_ref, v_ref, o_ref, q_sc, m_ref, acc_ref, *, tq, tk):
    d = HEAD_DIM
    n_blk = q_ref.shape[1] // tq
    grp = q_ref.shape[2] // d
    s = kt_ref.shape[2]
    n_chunks = s // tk

    def base(r, g):
        return (r * grp + g) * tq

    for r in range(n_blk):
        for g in range(grp):
            q_sc[base(r, g):base(r, g) + tq, :] = q_ref[0, r * tq:(r + 1) * tq, g * d:(g + 1) * d]

    def qk(item):
        r, c, g = item
        return jnp.dot(q_sc[base(r, g):base(r, g) + tq, :], kt_ref[0, :, c * tk:(c + 1) * tk],
                       preferred_element_type=F32)

    items = [(r, c, g) for r in range(n_blk) for c in range(n_chunks) for g in range(grp)]
    sc_next = qk(items[0])
    for n, (r, c, g) in enumerate(items):
        rows = slice(base(r, g), base(r, g) + tq)
        sc = sc_next
        if n + 1 < len(items):
            sc_next = qk(items[n + 1])
        v = v_ref[0, c * tk:(c + 1) * tk, :]
        row_max = jnp.max(sc, axis=-1, keepdims=True)
        if c == 0:
            m_new = jnp.broadcast_to(row_max, (tq, V7X_LANES))
        else:
            m_prev = m_ref[rows, :]
            m_new = jnp.maximum(m_prev, row_max)
        p = jnp.exp2((sc - pltpu.repeat(m_new, tk // V7X_LANES, axis=1)).astype(BF16))
        pv = jnp.dot(p, v, preferred_element_type=F32)
        if c == 0:
            acc_ref[rows, :] = pv
        else:
            acc_ref[rows, :] = acc_ref[rows, :] * jnp.exp2(m_prev - m_new) + pv
        m_ref[rows, :] = m_new
        if c == n_chunks - 1:
            acc = acc_ref[rows, :]
            o_ref[0, r * tq:(r + 1) * tq, g * d:(g + 1) * d] = (acc[:, :d] / acc[:, d:d + 1]).astype(o_ref.dtype)


def gqa_attention(q3, kt3, v3, *, tq, tk, row_blocks):
    b, s, nq = q3.shape
    d = HEAD_DIM
    grp = GQA_Q_HEADS // GQA_KV_HEADS
    tb = row_blocks * tq
    mrows = grp * tb
    vmem = (2 * (tb * grp * d * 2 + d * s * 2 + s * 128 * 2 + tb * grp * d * 2) + 3 * mrows * 128 * 4
            + 12 * tq * tk * 4)
    return pl.pallas_call(
        functools.partial(_gqa_attn_kernel, tq=tq, tk=tk),
        grid=(b, GQA_KV_HEADS, s // tb),
        in_specs=[
            pl.BlockSpec((1, tb, grp * d), lambda bi, kv, i: (bi, i, kv)),
            pl.BlockSpec((1, d, s), lambda bi, kv, i: (bi, kv, 0)),
            pl.BlockSpec((1, s, 2 * d), lambda bi, kv, i: (bi, 0, kv)),
        ],
        out_specs=pl.BlockSpec((1, tb, grp * d), lambda bi, kv, i: (bi, i, kv)),
        out_shape=jax.ShapeDtypeStruct((b, s, nq), BF16),
        scratch_shapes=[pltpu.VMEM((mrows, d), BF16), pltpu.VMEM((mrows, V7X_LANES), F32),
                        pltpu.VMEM((mrows, 2 * d), F32)],
        compiler_params=_params(("parallel", "parallel", "parallel"), vmem),
        name="gqa_attn",
    )(q3, kt3, v3)


def _mlstm_step(chains, ms, q_ref, kt_sc, v_ref, r_sc, cm_sc, b_sc, st_sc, h_sc):
    L = MLSTM_CHUNK
    dh = MLSTM_HEAD_DIM
    assert L == dh
    scale = dh ** -0.5
    row_i = lax.broadcasted_iota(jnp.int32, (L, L), 0)
    col_i = lax.broadcasted_iota(jnp.int32, (L, L), 1)

    def col(x_row):
        return jnp.transpose(jnp.broadcast_to(x_row, (L, L)))

    pre = []
    for (hh, direction, c), m in zip(chains, ms):
        off = pl.multiple_of(c * L, L)
        lanes = slice(hh * dh, (hh + 1) * dh)
        q = q_ref[0, pl.ds(off, L), lanes]
        kt = kt_sc[hh, :, pl.ds(off, L)]
        v = v_ref[0, pl.ds(off, L), lanes]
        r_row = r_sc[hh, direction, pl.ds(c, 1), :]
        cm_row = cm_sc[hh, direction, pl.ds(c, 1), :]
        b_row = b_sc[hh, direction, pl.ds(c, 1), :]
        rmax = jnp.max(r_row, axis=-1, keepdims=True)
        btot = b_row[:, L - 1:L] if direction == 0 else b_row[:, 0:1]
        cmat = jnp.maximum(m, col(cm_row))
        mask = (row_i >= col_i) if direction == 0 else (row_i <= col_i)
        c_last = jnp.maximum(m, rmax)
        w_state = jnp.exp(r_row - c_last) * scale
        pre.append(dict(
            off=off, lanes=lanes, q=q, kt=kt,
            v_aug=jnp.concatenate([v, jnp.ones((L, dh), BF16)], axis=-1),
            w_intra=jnp.where(mask, jnp.exp(r_row - cmat), 0.0) * scale,
            w_inter=jnp.exp(m - cmat),
            den_floor=jnp.exp(-(col(b_row) + cmat)),
            decay=jnp.exp(m - c_last),
            kt_w=(kt.astype(F32) * w_state).astype(BF16),
            m_new=btot + c_last,
        ))
    s_raw = [jnp.dot(p["q"], p["kt"], preferred_element_type=F32) for p in pre]
    states = [st_sc[hh, direction] for hh, direction, _ in chains]
    inter = [jnp.dot(p["q"], st.astype(BF16), preferred_element_type=F32) for p, st in zip(pre, states)]
    upd = [jnp.dot(p["kt_w"], p["v_aug"], preferred_element_type=F32) for p in pre]
    for n, ((hh, direction, _), p) in enumerate(zip(chains, pre)):
        st_sc[hh, direction] = p["decay"] * states[n] + upd[n]
        intra = jnp.dot((s_raw[n] * p["w_intra"]).astype(BF16), p["v_aug"], preferred_element_type=F32)
        h_aug = jnp.concatenate([p["w_inter"], p["w_inter"]], axis=-1) * inter[n] + intra
        h_sc[hh, direction, pl.ds(p["off"], L), :] = h_aug[:, :dh] / jnp.maximum(jnp.abs(h_aug[:, dh:]), p["den_floor"])
    return [p["m_new"] for p in pre]


def _mlstm_kernel(bias_ref, q_ref, k_ref, v_ref, o_ref, gate_ref, g_ref, y_ref,
                  r_sc, cm_sc, b_sc, h_sc, st_sc, kt_sc):
    L = MLSTM_CHUNK
    dh = MLSTM_HEAD_DIM
    hpb = q_ref.shape[2] // dh
    head0 = pl.program_id(1) * hpb
    nc = q_ref.shape[1] // L
    lane = lax.broadcasted_iota(jnp.int32, (nc, L), 1)
    shifts = [1 << t for t in range(int(math.log2(L)))]
    for hh in range(hpb):
        for d in range(2):
            i_pre = gate_ref[0, 2 * d, hh] + bias_ref[2 * d, head0 + hh]
            f_pre = gate_ref[0, 2 * d + 1, hh] + bias_ref[2 * d + 1, head0 + hh]
            logf = jnp.minimum(f_pre, 0.0) - jnp.log1p(jnp.exp(-jnp.abs(f_pre)))
            bc = logf
            for sh in shifts:
                if d == 0:
                    bc = bc + jnp.where(lane >= sh, pltpu.roll(bc, sh, axis=1), 0.0)
                else:
                    bc = bc + jnp.where(lane < L - sh, pltpu.roll(bc, L - sh, axis=1), 0.0)
            r = i_pre - bc
            cm = r
            for sh in shifts:
                if d == 0:
                    cm = jnp.maximum(cm, jnp.where(lane >= sh, pltpu.roll(cm, sh, axis=1), -jnp.inf))
                else:
                    cm = jnp.maximum(cm, jnp.where(lane < L - sh, pltpu.roll(cm, L - sh, axis=1), -jnp.inf))
            r_sc[hh, d] = r
            cm_sc[hh, d] = cm
            b_sc[hh, d] = bc
    st_sc[...] = jnp.zeros(st_sc.shape, F32)
    tr = TRANSPOSE_ROWS
    for hh in range(hpb):
        for r in range(0, q_ref.shape[1], tr):
            kt_sc[hh, :, r:r + tr] = jnp.transpose(k_ref[0, r:r + tr, hh * dh:(hh + 1) * dh].astype(F32)).astype(BF16)

    def body(c, ms):
        chains = [(hh, d, c if d == 0 else nc - 1 - c) for hh in range(hpb) for d in range(2)]
        return tuple(_mlstm_step(chains, ms, q_ref, kt_sc, v_ref, r_sc, cm_sc, b_sc, st_sc, h_sc))

    lax.fori_loop(0, nc, body, tuple(jnp.zeros((1, 1), F32) for _ in range(2 * hpb)), unroll=2)
    for hh in range(hpb):
        lanes = slice(hh * dh, (hh + 1) * dh)
        hsum = h_sc[hh, 0] + h_sc[hh, 1]
        ms = jnp.mean(hsum * hsum, axis=-1, keepdims=True)
        y = hsum * lax.rsqrt(ms + EPS) * g_ref[:, lanes]
        y_ref[0, :, lanes] = (jax.nn.sigmoid(o_ref[0, :, lanes].astype(F32)) * y).astype(y_ref.dtype)


def mlstm_branch(p3, gates5, gate_bias, norm_g, *, heads_per_block):
    b, s, _ = p3.shape
    L = MLSTM_CHUNK
    hpb = heads_per_block
    wb = hpb * MLSTM_HEAD_DIM
    nc = s // L
    blk = lambda off: pl.BlockSpec((1, s, wb), lambda bi, h, off=off: (bi, 0, off // wb + h))
    vmem = 2 * 5 * s * wb * 2 + 2 * s * wb * 4 + 6 * hpb * nc * L * 4 + 4 * hpb * wb * wb * 4 + 3 * s * wb * 4
    return pl.pallas_call(
        _mlstm_kernel,
        grid=(b, MLSTM_HEADS // hpb),
        in_specs=[
            pl.BlockSpec(memory_space=pltpu.SMEM),
            blk(OFF_CQ), blk(OFF_CK), blk(OFF_CV), blk(OFF_CO),
            pl.BlockSpec((1, 4, hpb, nc, L), lambda bi, h: (bi, 0, h, 0, 0)),
            pl.BlockSpec((1, wb), lambda bi, h: (0, h)),
        ],
        out_specs=pl.BlockSpec((1, s, wb), lambda bi, h: (bi, 0, h)),
        out_shape=jax.ShapeDtypeStruct((b, s, BRANCH_WIDTH), BF16),
        scratch_shapes=[
            pltpu.VMEM((hpb, 2, nc, L), F32), pltpu.VMEM((hpb, 2, nc, L), F32), pltpu.VMEM((hpb, 2, nc, L), F32),
            pltpu.VMEM((hpb, 2, s, MLSTM_HEAD_DIM), F32),
            pltpu.VMEM((hpb, 2, MLSTM_HEAD_DIM, 2 * MLSTM_HEAD_DIM), F32),
            pltpu.VMEM((hpb, MLSTM_HEAD_DIM, s), BF16),
        ],
        compiler_params=_params(("parallel", "parallel"), vmem),
        name="mlstm",
    )(gate_bias.astype(F32), p3, p3, p3, p3, gates5, norm_g.reshape(1, BRANCH_WIDTH).astype(F32))


def _rel_bucket(rel):
    half = REL_BUCKETS // 2
    max_exact = half // 2
    ret = jnp.where(rel > 0, half, 0)
    n = jnp.abs(rel)
    nf = jnp.maximum(n, 1).astype(F32)
    large = max_exact + (jnp.log(nf / max_exact) / math.log(REL_MAX_DIST / max_exact) * (half - max_exact)).astype(jnp.int32)
    large = jnp.minimum(large, half - 1)
    return ret + jnp.where(n < max_exact, n, large)


def diff_bias_tiles(rel_bias, t):
    assert t >= REL_MAX_DIST
    k = jnp.arange(2 * t, dtype=jnp.int32)
    rel = jnp.arange(-2, 3, dtype=jnp.int32)[:, None] * t + jnp.where(k < t, k, k - 2 * t)[None, :]
    onehot = (_rel_bucket(rel)[:, :, None] == jnp.arange(REL_BUCKETS, dtype=jnp.int32)).astype(F32)
    period = jnp.einsum('dkb,bh->hdk', onehot, rel_bias.astype(F32) * LOG2E, precision=lax.Precision.HIGHEST)
    far = period[:, 0::4, 0]
    return period, far


def _diff_attn_kernel(far_ref, q_ref, k_ref, v_ref, period_ref, lam_ref, g_ref, o_ref,
                      q_sc, kt_sc, vaug_sc, bias_sc, sc_buf, m_ref, acc_ref, *, t, lam_init):
    dq = DIFF_QK_DIM
    dv = DIFF_V_DIM
    s = k_ref.shape[1]
    nt = s // t
    head = pl.program_id(1)
    ahead = sc_buf.shape[0]
    rb = sc_buf.shape[1]
    grp = sc_buf.shape[2] // t
    wrap = (grp - 1) * t
    assert nt % grp == 0 and t % rb == 0 and nt >= 4

    tr = max(t, TRANSPOSE_ROWS)
    for r in range(0, s, tr):
        kt_sc[:, r:r + tr] = jnp.transpose(k_ref[0, r:r + tr, :].astype(F32)).astype(BF16)
    kt_sc[:, s:] = kt_sc[:, :wrap]
    vaug_sc[:s, :dv] = v_ref[0]
    vaug_sc[s:, :dv] = v_ref[0, :wrap, :]
    lane = lax.broadcasted_iota(jnp.int32, (s + wrap, dv), 1)
    vaug_sc[:, dv:] = jnp.where(lane == 0, 1.0, 0.0).astype(BF16)
    for dl in range(period_ref.shape[1]):
        full = jnp.broadcast_to(period_ref[0, dl:dl + 1, :], (t, 2 * t))
        bias_sc[dl] = pltpu.roll(full, 0, axis=1, stride=1, stride_axis=0)[:, :t]
    q_scale = dq ** -0.5 * LOG2E
    q_all = (q_ref[0].astype(F32) * q_scale).astype(BF16)
    q_sc[0] = q_all[:, :dq]
    q_sc[1] = q_all[:, dq:]
    lp = lam_ref[...]
    lam = (jnp.exp(jnp.sum(lp[0:1] * lp[1:2], axis=-1, keepdims=True))
           - jnp.exp(jnp.sum(lp[2:3] * lp[3:4], axis=-1, keepdims=True)) + lam_init)

    def key_tile(i, delta):
        j = lax.rem(i + delta, nt)
        return j, pl.multiple_of(j * t, t)

    def qk(i, item):
        pair, mp, hf = item
        row0 = pl.multiple_of(i * t + hf * rb, rb)
        _, off = key_tile(i, grp * pair)
        return jnp.dot(q_sc[mp, pl.ds(row0, rb), :], kt_sc[mp * dq:(mp + 1) * dq, pl.ds(off, grp * t)],
                       preferred_element_type=F32)

    items = [(pair, mp, hf) for pair in range(nt // grp) for mp in range(2) for hf in range(t // rb)]
    for a in range(ahead):
        sc_buf[a] = qk(0, items[a])

    def tile_body(i, carry):
        pending = [sc_buf[a] for a in range(ahead)]
        for n, (pair, mp, hf) in enumerate(items):
            sc = pending.pop(0)
            if n + ahead < len(items):
                pending.append(qk(i, items[n + ahead]))
            else:
                pending.append(qk(jnp.minimum(i + 1, nt - 1), items[n + ahead - len(items)]))
            rows = slice(hf * rb, (hf + 1) * rb)
            scs = [sc[:, p * t:(p + 1) * t] for p in range(grp)]
            _, off = key_tile(i, grp * pair)
            shifts, row_max = [], None
            for piece, delta in enumerate(range(grp * pair, grp * pair + grp)):
                j, _ = key_tile(i, delta)
                if delta in (0, 1, nt - 1):
                    tile = 2 if delta == 0 else jnp.clip(j - i, -2, 2) + 2
                    scs[piece] = scs[piece] + bias_sc[tile, rows, :]
                    shifts.append(None)
                    rm = jnp.max(scs[piece], axis=-1, keepdims=True)
                else:
                    shifts.append(jnp.where(j > i, far_ref[head, 1], far_ref[head, 0]))
                    rm = jnp.max(scs[piece], axis=-1, keepdims=True) + shifts[-1]
                row_max = rm if row_max is None else jnp.maximum(row_max, rm)
            if pair == 0:
                m_new = jnp.broadcast_to(row_max, (rb, V7X_LANES))
            else:
                m_prev = m_ref[mp, rows, :]
                m_new = jnp.maximum(m_prev, row_max)
            ps = []
            for piece in range(grp):
                m_sub = m_new if shifts[piece] is None else m_new - shifts[piece]
                ps.append(jnp.exp2((scs[piece] - pltpu.repeat(m_sub, t // V7X_LANES, axis=1)).astype(BF16)))
            pv = jnp.dot(jnp.concatenate(ps, axis=1), vaug_sc[pl.ds(off, grp * t), :], preferred_element_type=F32)
            if pair == 0:
                acc_ref[mp, rows, :] = pv
            else:
                alpha = jnp.exp2(m_prev - m_new)
                acc_ref[mp, rows, :] = acc_ref[mp, rows, :] * pltpu.repeat(alpha, 2 * dv // V7X_LANES, axis=1) + pv
            m_ref[mp, rows, :] = m_new
        for a in range(ahead):
            sc_buf[a] = pending[a]
        o0 = acc_ref[0, :, :dv] / acc_ref[0, :, dv:dv + 1]
        o1 = acc_ref[1, :, :dv] / acc_ref[1, :, dv:dv + 1]
        o = o0 - lam * o1
        ms = jnp.mean(o * o, axis=-1, keepdims=True)
        row0 = pl.multiple_of(i * t, t)
        o_ref[0, pl.ds(row0, t), :] = (o * lax.rsqrt(ms + EPS) * g_ref[...] * (1.0 - lam_init)).astype(o_ref.dtype)
        return carry

    lax.fori_loop(0, nt, tile_body, 0, unroll=4)


def diff_attention(p3, bias_period, bias_far, lam_params, sub_g, *, t, rows_per_item, tiles_per_item, layer_number):
    b, s, _ = p3.shape
    dv = DIFF_V_DIM
    n_off = bias_period.shape[1]
    assert s // t >= 4, "tiles 2 .. s/t-2 steps away from the query tile must all be beyond REL_MAX_DIST"
    lam_init = 0.8 - 0.6 * math.exp(-0.3 * (layer_number - 1))
    sw = s + (tiles_per_item - 1) * t
    item = rows_per_item * tiles_per_item * t
    vmem = (2 * (s * 128 * 2 + 128 * s * 2 + s * dv * 2 + s * dv * 2) + n_off * t * t * 4 + sw * (2 * dv + 128) * 2
            + 2 * s * 128 * 2 + item * 4 + 2 * t * 128 * 4 * 3 + 8 * item * 4)
    return pl.pallas_call(
        functools.partial(_diff_attn_kernel, t=t, lam_init=lam_init),
        grid=(b, DIFF_HEADS),
        in_specs=[
            pl.BlockSpec(memory_space=pltpu.SMEM),
            pl.BlockSpec((1, s, 128), lambda bi, h: (bi, 0, OFF_DQ // 128 + h)),
            pl.BlockSpec((1, s, 128), lambda bi, h: (bi, 0, OFF_DK // 128 + h)),
            pl.BlockSpec((1, s, dv), lambda bi, h: (bi, 0, OFF_DV // dv + h)),
            pl.BlockSpec((1, n_off, 2 * t), lambda bi, h: (h, 0, 0)),
            pl.BlockSpec((4, DIFF_QK_DIM), lambda bi, h: (0, 0)),
            pl.BlockSpec((1, dv), lambda bi, h: (0, 0)),
        ],
        out_specs=pl.BlockSpec((1, s, dv), lambda bi, h: (bi, 0, h)),
        out_shape=jax.ShapeDtypeStruct((b, s, BRANCH_WIDTH), BF16),
        scratch_shapes=[pltpu.VMEM((2, s, DIFF_QK_DIM), BF16), pltpu.VMEM((2 * DIFF_QK_DIM, sw), BF16),
                        pltpu.VMEM((sw, 2 * dv), BF16),
                        pltpu.VMEM((n_off, t, t), F32), pltpu.VMEM((1, rows_per_item, tiles_per_item * t), F32),
                        pltpu.VMEM((2, t, V7X_LANES), F32), pltpu.VMEM((2, t, 2 * dv), F32)],
        compiler_params=_params(("parallel", "parallel"), vmem),
        name="diff_attn",
    )(bias_far, p3, p3, p3, bias_period, lam_params.astype(F32), sub_g.reshape(1, dv).astype(F32))


def _merge_kernel(ya_ref, yb_ref, yc_ref, yd_ref, g_ref, wg_ref, wb_ref, wo_ref, x_ref, o_ref, wb_sc, wo_sc):
    dm = x_ref.shape[1]

    @pl.when(pl.program_id(0) == 0)
    def _():
        for n in range(wb_ref.shape[0]):
            wb_sc[n] = wb_ref[n].astype(BF16)
        wo_sc[...] = wo_ref[...].astype(BF16)

    tm = x_ref.shape[0]
    halves = [slice(0, tm // 2), slice(tm // 2, tm)]
    y_refs = (ya_ref, yb_ref, yc_ref, yd_ref)
    gates, branch = [], []
    for rows in halves:
        x = x_ref[rows, :]
        ms = jnp.mean(x * x, axis=-1, keepdims=True)
        u = (x * lax.rsqrt(ms + EPS) * g_ref[...]).astype(BF16)
        gates.append([jnp.dot(u, wg_ref[:, n * dm:(n + 1) * dm], preferred_element_type=F32)
                      for n in range(len(y_refs))])
        branch.append([jnp.dot(y_ref[rows, :], wb_sc[n], preferred_element_type=F32)
                       for n, y_ref in enumerate(y_refs)])
    for rows, gts, brs in zip(halves, gates, branch):
        merged = None
        for gt, br in zip(gts, brs):
            term = jax.nn.sigmoid(gt) * br
            merged = term if merged is None else merged + term
        o_ref[rows, :] = x_ref[rows, :] + jnp.dot(merged.astype(BF16), wo_sc[...], preferred_element_type=F32)


def merge_branches(ys, norm_g, w_gates, w_branch, w_out, x2, *, tm):
    m, dm = x2.shape
    w = BRANCH_WIDTH
    y_spec = pl.BlockSpec((tm, w), lambda i: (i, 0))
    vmem = (2 * (4 * tm * w * 2 + 2 * tm * dm * 4) + N_BRANCHES * dm * dm * 2 + (4 * w * dm + dm * dm) * (4 + 2)
            + 12 * tm * dm * 4)
    return pl.pallas_call(
        _merge_kernel,
        grid=(m // tm,),
        in_specs=[
            y_spec, y_spec, y_spec, y_spec,
            pl.BlockSpec((1, dm), lambda i: (0, 0)),
            pl.BlockSpec((dm, N_BRANCHES * dm), lambda i: (0, 0), pipeline_mode=pl.Buffered(1)),
            pl.BlockSpec((N_BRANCHES, w, dm), lambda i: (0, 0, 0), pipeline_mode=pl.Buffered(1)),
            pl.BlockSpec((dm, dm), lambda i: (0, 0), pipeline_mode=pl.Buffered(1)),
            pl.BlockSpec((tm, dm), lambda i: (i, 0)),
        ],
        out_specs=pl.BlockSpec((tm, dm), lambda i: (i, 0)),
        out_shape=jax.ShapeDtypeStruct((m, dm), F32),
        scratch_shapes=[pltpu.VMEM((N_BRANCHES, w, dm), BF16), pltpu.VMEM((dm, dm), BF16)],
        compiler_params=_params(("arbitrary",), vmem),
        name="merge",
    )(*ys, norm_g.reshape(1, dm).astype(F32), w_gates, w_branch.astype(F32), w_out.astype(F32), x2)


HALO_ROWS = 16


def _conv_ffn_kernel(x_ref, xp_ref, xn_ref, g_ref, wa_ref, wl_ref, cw_ref, cb_ref, wd_ref, gf_ref, o_ref,
                     xe_sc, wd_sc, *, tiles_per_seq, col_chunk, final_norm):
    tm = o_ref.shape[0]
    f = wa_ref.shape[1]
    h = HALO_ROWS
    pos = pl.program_id(0) % tiles_per_seq

    @pl.when(pl.program_id(0) == 0)
    def _():
        wd_sc[...] = wd_ref[...].astype(BF16)

    def norm(x):
        ms = jnp.mean(x * x, axis=-1, keepdims=True)
        return x * lax.rsqrt(ms + EPS) * g_ref[...]

    xe_sc[:h] = jnp.where(pos == 0, 0.0, norm(xp_ref[...])).astype(BF16)
    xe_sc[h:h + tm] = norm(x_ref[...]).astype(BF16)
    xe_sc[h + tm:] = jnp.where(pos == tiles_per_seq - 1, 0.0, norm(xn_ref[...])).astype(BF16)
    k0 = math.sqrt(2.0 / math.pi)
    bounds = list(range(0, f, col_chunk)) + [f]
    chunks = [slice(lo, hi) for lo, hi in zip(bounds[:-1], bounds[1:])]

    def project(cols):
        a_ext = jnp.dot(xe_sc[...], wa_ref[:, cols], preferred_element_type=F32)
        lin = jnp.dot(xe_sc[h:h + tm, :], wl_ref[:, cols], preferred_element_type=F32)
        return a_ext, lin

    y = x_ref[...]
    nxt = project(chunks[0])
    for n, cols in enumerate(chunks):
        a_ext, lin = nxt
        if n + 1 < len(chunks):
            nxt = project(chunks[n + 1])
        cw = cw_ref[:, cols]
        c = (a_ext[h - 1:h - 1 + tm] * cw[0:1] + a_ext[h:h + tm] * cw[1:2] + a_ext[h + 1:h + 1 + tm] * cw[2:3]
             + cb_ref[:, cols])
        t = jnp.tanh(c * ((c * c) * (k0 * 0.044715) + k0))
        hmid = ((c + c * t) * lin).astype(BF16)
        y = y + jnp.dot(hmid, wd_sc[cols, :], preferred_element_type=F32)
    if final_norm:
        ms = jnp.mean(y * y, axis=-1, keepdims=True)
        y = y * lax.rsqrt(ms + EPS) * gf_ref[...]
    o_ref[...] = y


def conv_ffn(x2, g, w_up, conv_w, conv_b, w_down, final_g, *, s, tm, col_chunk, final_norm):
    m, dm = x2.shape
    f = w_up.shape[1] // 2
    tiles_per_seq = s // tm
    hb = tm // HALO_ROWS
    n_halo = m // HALO_ROWS
    vmem = (2 * 2 * tm * dm * 4 + 2 * dm * f * 2 + f * dm * (4 + 2) + (tm + 2 * HALO_ROWS) * dm * 2
            + 6 * tm * col_chunk * 4 + 2 * tm * dm * 4)
    return pl.pallas_call(
        functools.partial(_conv_ffn_kernel, tiles_per_seq=tiles_per_seq, col_chunk=col_chunk, final_norm=final_norm),
        grid=(m // tm,),
        in_specs=[
            pl.BlockSpec((tm, dm), lambda i: (i, 0)),
            pl.BlockSpec((HALO_ROWS, dm), lambda i: (jnp.maximum(i * hb - 1, 0), 0)),
            pl.BlockSpec((HALO_ROWS, dm), lambda i: (jnp.minimum((i + 1) * hb, n_halo - 1), 0)),
            pl.BlockSpec((1, dm), lambda i: (0, 0)),
            pl.BlockSpec((dm, f), lambda i: (0, 0), pipeline_mode=pl.Buffered(1)),
            pl.BlockSpec((dm, f), lambda i: (0, 1), pipeline_mode=pl.Buffered(1)),
            pl.BlockSpec((CONV_W, f), lambda i: (0, 0)),
            pl.BlockSpec((1, f), lambda i: (0, 0)),
            pl.BlockSpec((f, dm), lambda i: (0, 0), pipeline_mode=pl.Buffered(1)),
            pl.BlockSpec((1, dm), lambda i: (0, 0)),
        ],
        out_specs=pl.BlockSpec((tm, dm), lambda i: (i, 0)),
        out_shape=jax.ShapeDtypeStruct((m, dm), F32),
        scratch_shapes=[pltpu.VMEM((tm + 2 * HALO_ROWS, dm), BF16), pltpu.VMEM((f, dm), BF16)],
        compiler_params=_params(("arbitrary",), vmem),
        name="conv_ffn",
    )(x2, x2, x2, g.reshape(1, dm).astype(F32), w_up, w_up, conv_w.astype(F32), conv_b.reshape(1, f).astype(F32),
      w_down.astype(F32), final_g.reshape(1, dm).astype(F32))


def _arrange_w_in(w):
    widths = (512, 512, 128, 128, 512, 512, 512, 512, 16, 512, 512, 512, 4096)
    offs = [0]
    for wd in widths:
        offs.append(offs[-1] + wd)
    seg = lambda n: w[:, offs[n]:offs[n + 1]]
    a, bq, bk, bv, cq, ck, cv, co, cgate, dq, dk, dv, gates = (seg(n) for n in range(len(widths)))
    main = jnp.concatenate([a, bq, cq, ck, cv, co, dq, dk, dv, bk, bv], axis=1).astype(BF16)
    gate = jnp.pad(cgate, ((0, 0), (0, V7X_LANES - cgate.shape[1]))).astype(BF16)
    return main, gate, gates.astype(BF16)


def kernel(x, norm_mix_g, w_in, mlstm_gate_bias, qk_norm_g, mlstm_norm_g, diff_lambda, diff_norm_g, rel_bias,
           w_branch, w_out, norm_ffn_g, w_up, conv_w, conv_b, w_down, final_norm_g):
    b, s, dm = x.shape
    depth = w_in.shape[0]
    m = b * s
    d_ff = w_down.shape[1]
    L = MLSTM_CHUNK
    tl = TILES

    bd, dft_hi, dft_lo = fourier_tables(s)
    cos2, sin2 = rope_tables(s)
    bias_tiles, bias_far = diff_bias_tiles(rel_bias, tl["diff_tile"])

    x2 = x.reshape(m, dm)
    for layer in range(depth):
        w_main, w_gate, w_branch_gates = _arrange_w_in(w_in[layer])
        p2, cgate = norm_matmul(x2, norm_mix_g[layer], w_main, w_gate, tm=tl["proj_rows"], tn=P_WIDTH)
        p3 = p2.reshape(b, s, P_WIDTH)

        y_a = fourier_mix(p3, bd, dft_hi, dft_lo, tm=tl["fourier_rows"])

        q_b, k_b, v_b = gqa_prep(p2, qk_norm_g[layer], cos2, sin2, s=s, tm=tl["gqa_prep_rows"])
        kt_b = jnp.swapaxes(k_b.reshape(b, s, GQA_KV_HEADS * HEAD_DIM), 1, 2)
        y_b = gqa_attention(q_b.reshape(b, s, -1), kt_b, v_b.reshape(b, s, -1),
                            tq=tl["gqa_q_rows"], tk=tl["gqa_keys"], row_blocks=tl["gqa_row_blocks"])

        gates5 = jnp.transpose(cgate[:, :4 * MLSTM_HEADS].reshape(b, s, 4, MLSTM_HEADS), (0, 2, 3, 1))
        gates5 = gates5.reshape(b, 4, MLSTM_HEADS, s // L, L)
        y_c = mlstm_branch(p3, gates5, mlstm_gate_bias[layer], mlstm_norm_g[layer],
                           heads_per_block=tl["mlstm_heads"])

        y_d = diff_attention(p3, bias_tiles, bias_far, diff_lambda[layer], diff_norm_g[layer],
                             t=tl["diff_tile"], rows_per_item=tl["diff_q_rows"],
                             tiles_per_item=tl["diff_key_tiles"], layer_number=layer + 1)

        ys = [y.reshape(m, BRANCH_WIDTH) for y in (y_a, y_b, y_c, y_d)]
        x2 = merge_branches(ys, norm_mix_g[layer], w_branch_gates, w_branch[layer], w_out[layer], x2,
                            tm=tl["merge_rows"])

        half_lin = jnp.concatenate([jnp.ones((d_ff,), F32), jnp.full((d_ff,), 0.5, F32)])
        x2 = conv_ffn(x2, norm_ffn_g[layer], (w_up[layer] * half_lin).astype(BF16), conv_w[layer], conv_b[layer],
                      w_down[layer], final_norm_g, s=s, tm=tl["ffn_rows"], col_chunk=tl["ffn_cols"],
                      final_norm=(layer == depth - 1))
    return x2.reshape(b, s, dm)
```
